```python
import jax, jax.numpy as jnp
from jax import lax
import numpy as np

D_MODEL = 1024
BATCH = 2
SEQ = 8192
DEPTH = 1
DEC_BATCH = 8
DEC_SEQ = 64
PAST_LEN = 2048

CHUNK = 64
MIX_WIDTH = D_MODEL
RWKV_WIDTH = MIX_WIDTH // 2
RWKV_HEAD = 64
RWKV_HEADS = RWKV_WIDTH // RWKV_HEAD
DECAY_LORA = 32
AAA_LORA = 32
GATE_LORA = 64
GLA_WIDTH = MIX_WIDTH - RWKV_WIDTH
GLA_HEADS = 4
GLA_DV = GLA_WIDTH // GLA_HEADS
GLA_DK = GLA_DV // 2
GLA_KEY_WIDTH = GLA_HEADS * GLA_DK
GLA_GATE_RANK = 16
GLA_TAU = 16.0
RWKV_PROJ = 3 * RWKV_WIDTH + DECAY_LORA + AAA_LORA + GATE_LORA
GLA_PROJ = 2 * GLA_KEY_WIDTH + GLA_WIDTH + GLA_GATE_RANK + GLA_WIDTH
IN_PROJ = RWKV_PROJ + GLA_PROJ
N_GROUPS = 4
EXPERTS_PER_GROUP = 4
N_EXPERTS = N_GROUPS * EXPERTS_PER_GROUP
TOP_K = 2
D_EXPERT = 512
RMS_EPS = 1e-6
LNX_EPS = 64e-5

kernel_name = "rwkv7_gla_hier_moe_streaming_step"


def rmsnorm(x, g):
    xf = x.astype(jnp.float32)
    y = xf * lax.rsqrt(jnp.mean(xf * xf, axis=-1, keepdims=True) + RMS_EPS)
    return (y * g).astype(x.dtype)


def modulation(c, w, b, n):
    m = jax.nn.silu(c) @ w + b
    return jnp.split(m[:, None, :], n, axis=-1)


def rwkv7_group(p, shift_prev, s0, lp):
    B, L, _ = p.shape
    W = RWKV_WIDTH
    xx = jnp.concatenate([shift_prev[:, None, :].astype(p.dtype), p[:, :-1]], axis=1)
    ps = p + (xx - p) * lp["mu_shift"]
    r, k, v, dw, da, dg = jnp.split(
        ps, [W, 2 * W, 3 * W, 3 * W + DECAY_LORA, 3 * W + DECAY_LORA + AAA_LORA], axis=-1)
    logw = -jax.nn.softplus(-(lp["w0"] + jnp.tanh(dw) @ lp["w_decay_up"])) - 0.5
    decay = jnp.exp(-jnp.exp(logw.astype(jnp.float32)))
    a = jax.nn.sigmoid(lp["a0"] + da @ lp["w_a_up"])
    g = jax.nn.sigmoid(dg) @ lp["w_g_up"]
    heads = lambda t: t.reshape(B, L, RWKV_HEADS, RWKV_HEAD)
    kk = heads(k * lp["k_k"]).astype(jnp.float32)
    kk = kk / jnp.maximum(jnp.sqrt(jnp.sum(kk * kk, axis=-1, keepdims=True)), 1e-12)
    k = k * (1 + (a - 1) * lp["k_a"])
    r_h, k_h, v_h = heads(r), heads(k), heads(v)

    def step(S, inp):
        r_t, w_t, k_t, v_t, kk_t, a_t = inp
        sa = jnp.einsum("bhvk,bhk->bhv", S, -kk_t)
        S = (S * w_t[:, :, None, :] + sa[..., None] * (kk_t * a_t)[:, :, None, :]
             + v_t[..., None] * k_t[:, :, None, :])
        return S, jnp.einsum("bhvk,bhk->bhv", S, r_t)

    tm = lambda t: jnp.moveaxis(t.astype(jnp.float32), 1, 0)
    S, y = lax.scan(step, s0.astype(jnp.float32),
                    (tm(r_h), tm(heads(decay)), tm(k_h), tm(v_h), tm(kk), tm(heads(a))),
                    unroll=4)
    y = jnp.moveaxis(y, 0, 1)
    mu = jnp.mean(y, axis=-1, keepdims=True)
    var = jnp.mean(jnp.square(y - mu), axis=-1, keepdims=True)
    y = ((y - mu) * lax.rsqrt(var + LNX_EPS)).reshape(B, L, W) * lp["lnx_w"] + lp["lnx_b"]
    bonus = jnp.sum(r_h * k_h * lp["r_k"], axis=-1, keepdims=True) * v_h
    y = (y + bonus.reshape(B, L, W)) * g
    return y.astype(p.dtype), p[:, -1], S.astype(s0.dtype)


def gla_group(p, s0, lp):
    B, L, _ = p.shape
    KW, GW = GLA_KEY_WIDTH, GLA_WIDTH
    q, k, v, dgt, og = jnp.split(p, [KW, 2 * KW, 2 * KW + GW, 2 * KW + GW + GLA_GATE_RANK], axis=-1)
    log_alpha = jax.nn.log_sigmoid(
        (dgt @ lp["w_gla_gate_up"] + lp["b_gla_gate"]).astype(jnp.float32)) / GLA_TAU
    cs = CHUNK if L >= CHUNK else L
    n = L // cs
    blk = lambda t, d: t.reshape(B, n, cs, GLA_HEADS, d).transpose(0, 3, 1, 2, 4).astype(jnp.float32)
    qb = blk(q, GLA_DK) * (GLA_DK ** -0.5)
    kb, vb, lab = blk(k, GLA_DK), blk(v, GLA_DV), blk(log_alpha, GLA_DK)
    b = jnp.cumsum(lab, axis=3)
    b_last = b[:, :, :, -1:, :]
    q_dec = qb * jnp.exp(b)
    k_inv = kb * jnp.exp(-b)
    k_end = kb * jnp.exp(b_last - b)
    causal = jnp.tril(jnp.ones((cs, cs), dtype=bool))
    att = jnp.where(causal, jnp.einsum("bhnid,bhnjd->bhnij", q_dec, k_inv), 0.0)
    o = jnp.einsum("bhnij,bhnjv->bhniv", att, vb)
    u = jnp.einsum("bhnjd,bhnjv->bhndv", k_end, vb)
    dec = jnp.exp(b_last[:, :, :, 0, :])

    def step(S, inp):
        d_n, u_n = inp
        return d_n[..., None] * S + u_n, S

    S, s_in = lax.scan(step, s0.astype(jnp.float32),
                       (jnp.moveaxis(dec, 2, 0), jnp.moveaxis(u, 2, 0)))
    o = o + jnp.einsum("bhnid,nbhdv->bhniv", q_dec, s_in)
    o = o.transpose(0, 2, 3, 1, 4).reshape(B, L, GLA_HEADS, GLA_DV)
    o = o * lax.rsqrt(jnp.mean(o * o, axis=-1, keepdims=True) + RMS_EPS)
    o = o.reshape(B, L, GW) * lp["gla_norm_g"] * jax.nn.silu(og)
    return o.astype(p.dtype), S.astype(s0.dtype)


def hier_moe(h, lp):
    B, L, D = h.shape
    t = h.reshape(B * L, D)
    lg = (t @ lp["w_router_group"] + lp["b_router_group"]).astype(jnp.float32)
    pg = jax.nn.softmax(lg, axis=-1)
    gi = jnp.argmax(lg, axis=-1)
    pg_top = jnp.take_along_axis(pg, gi[:, None], axis=1)
    le = (t @ lp["w_router_expert"] + lp["b_router_expert"]).astype(jnp.float32)
    le = le.reshape(-1, N_GROUPS, EXPERTS_PER_GROUP)
    le = jnp.take_along_axis(le, gi[:, None, None], axis=1)[:, 0]
    top_v, top_i = lax.top_k(jax.nn.softmax(le, axis=-1), TOP_K)
    wts = pg_top * top_v / jnp.sum(top_v, axis=-1, keepdims=True)
    eid = gi[:, None] * EXPERTS_PER_GROUP + top_i
    comb = jnp.einsum("tk,tke->te", wts, jax.nn.one_hot(eid, N_EXPERTS, dtype=jnp.float32))
    out = jnp.zeros((B * L, D), jnp.float32)
    for e in range(N_EXPERTS):
        hid = jax.nn.silu(t @ lp["w_expert_gate"][e]) * (t @ lp["w_expert_up"][e])
        out = out + comb[:, e:e + 1] * (hid @ lp["w_expert_down"][e])
    return out.reshape(B, L, D).astype(h.dtype)


def layer(x, c, shift0, wkv0, gla0, lp):
    sh1, sc1, gt1, sh2, sc2, gt2 = modulation(c, lp["w_ada"], lp["b_ada"], 6)
    h = rmsnorm(x, lp["norm1_g"]) * (1 + sc1) + sh1
    proj = h @ lp["w_in"]
    y_r, new_shift, new_wkv = rwkv7_group(proj[..., :RWKV_PROJ], shift0, wkv0, lp)
    y_g, new_gla = gla_group(proj[..., RWKV_PROJ:], gla0, lp)
    x = x + gt1 * (jnp.concatenate([y_r, y_g], axis=-1) @ lp["w_out"])
    h = rmsnorm(x, lp["norm2_g"]) * (1 + sc2) + sh2
    x = x + gt2 * hier_moe(h, lp)
    return x, new_shift, new_wkv, new_gla


def setup_inputs(seed: int = 0) -> dict:
    key = jax.random.key(seed)
    keys = iter(jax.random.split(key, 48))
    f32 = jnp.float32

    def nrm(shape, scale):
        return jax.random.normal(next(keys), shape, f32) * scale

    def unif(shape, lo, hi):
        return jax.random.uniform(next(keys), shape, f32, lo, hi)

    D, L = D_MODEL, DEPTH
    return {
        "x_prompt": nrm((BATCH, SEQ, D), 1.0),
        "x_sample": nrm((DEC_BATCH, DEC_SEQ, D), 1.0),
        "c_prompt": nrm((BATCH, D), 1.0),
        "c_sample": nrm((DEC_BATCH, D), 1.0),
        "state_rwkv_shift": nrm((L, DEC_BATCH, RWKV_PROJ), 1.0),
        "state_rwkv_wkv": nrm((L, DEC_BATCH, RWKV_HEADS, RWKV_HEAD, RWKV_HEAD), 0.5),
        "state_gla_kv": nrm((L, DEC_BATCH, GLA_HEADS, GLA_DK, GLA_DV), 0.5),
        "w_ada": nrm((L, D, 6 * D), D ** -0.5),
        "b_ada": nrm((L, 6 * D), 0.02),
        "norm1_g": 1.0 + nrm((L, D), 0.02),
        "norm2_g": 1.0 + nrm((L, D), 0.02),
        "w_in": nrm((L, D, IN_PROJ), D ** -0.5),
        "mu_shift": unif((L, RWKV_PROJ), 0.0, 1.0),
        "w0": unif((L, RWKV_WIDTH), -6.0, 1.0),
        "w_decay_up": nrm((L, DECAY_LORA, RWKV_WIDTH), DECAY_LORA ** -0.5),
        "a0": nrm((L, RWKV_WIDTH), 0.1),
        "w_a_up": nrm((L, AAA_LORA, RWKV_WIDTH), AAA_LORA ** -0.5),
        "w_g_up": nrm((L, GATE_LORA, RWKV_WIDTH), GATE_LORA ** -0.5),
        "k_k": 0.85 + nrm((L, RWKV_WIDTH), 0.02),
        "k_a": 1.0 + nrm((L, RWKV_WIDTH), 0.02),
        "r_k": nrm((L, RWKV_HEADS, RWKV_HEAD), 0.1),
        "lnx_w": 1.0 + nrm((L, RWKV_WIDTH), 0.02),
        "lnx_b": nrm((L, RWKV_WIDTH), 0.01),
        "w_gla_gate_up": nrm((L, GLA_GATE_RANK, GLA_KEY_WIDTH), GLA_GATE_RANK ** -0.5),
        "b_gla_gate": nrm((L, GLA_KEY_WIDTH), 0.1),
        "gla_norm_g": 1.0 + nrm((L, GLA_WIDTH), 0.02),
        "w_out": nrm((L, MIX_WIDTH, D), MIX_WIDTH ** -0.5),
        "w_router_group": nrm((L, D, N_GROUPS), D ** -0.5),
        "b_router_group": nrm((L, N_GROUPS), 0.01),
        "w_router_expert": nrm((L, D, N_EXPERTS), D ** -0.5),
        "b_router_expert": nrm((L, N_EXPERTS), 0.01),
        "w_expert_gate": nrm((L, N_EXPERTS, D, D_EXPERT), D ** -0.5),
        "w_expert_up": nrm((L, N_EXPERTS, D, D_EXPERT), D ** -0.5),
        "w_expert_down": nrm((L, N_EXPERTS, D_EXPERT, D), D_EXPERT ** -0.5),
        "w_ada_final": nrm((D, 2 * D), D ** -0.5),
        "b_ada_final": nrm((2 * D,), 0.02),
        "normf_g": 1.0 + nrm((D,), 0.02),
    }


def reference(x_prompt, x_sample, c_prompt, c_sample, state_rwkv_shift, state_rwkv_wkv, state_gla_kv,
              w_ada, b_ada, norm1_g, norm2_g, w_in, mu_shift, w0, w_decay_up, a0, w_a_up, w_g_up,
              k_k, k_a, r_k, lnx_w, lnx_b, w_gla_gate_up, b_gla_gate, gla_norm_g, w_out,
              w_router_group, b_router_group, w_router_expert, b_router_expert,
              w_expert_gate, w_expert_up, w_expert_down, w_ada_final, b_ada_final, normf_g):
    def layer_params(l):
        return dict(w_ada=w_ada[l], b_ada=b_ada[l], norm1_g=norm1_g[l], norm2_g=norm2_g[l],
                    w_in=w_in[l], mu_shift=mu_shift[l], w0=w0[l], w_decay_up=w_decay_up[l],
                    a0=a0[l], w_a_up=w_a_up[l], w_g_up=w_g_up[l], k_k=k_k[l], k_a=k_a[l],
                    r_k=r_k[l], lnx_w=lnx_w[l], lnx_b=lnx_b[l], w_gla_gate_up=w_gla_gate_up[l],
                    b_gla_gate=b_gla_gate[l], gla_norm_g=gla_norm_g[l], w_out=w_out[l],
                    w_router_group=w_router_group[l], b_router_group=b_router_group[l],
                    w_router_expert=w_router_expert[l], b_router_expert=b_router_expert[l],
                    w_expert_gate=w_expert_gate[l], w_expert_up=w_expert_up[l],
                    w_expert_down=w_expert_down[l])

    def run(x, c, shift0, wkv0, gla0):
        shifts, wkvs, glas = [], [], []
        for l in range(DEPTH):
            x, s_shift, s_wkv, s_gla = layer(x, c, shift0[l], wkv0[l], gla0[l], layer_params(l))
            shifts.append(s_shift)
            wkvs.append(s_wkv)
            glas.append(s_gla)
        shf, scf = modulation(c, w_ada_final, b_ada_final, 2)
        y = rmsnorm(x, normf_g) * (1 + scf) + shf
        return y, jnp.stack(shifts), jnp.stack(wkvs), jnp.stack(glas)

    bp = x_prompt.shape[0]
    shift_p0 = jnp.zeros((DEPTH, bp, RWKV_PROJ), x_prompt.dtype)
    wkv_p0 = jnp.zeros((DEPTH, bp, RWKV_HEADS, RWKV_HEAD, RWKV_HEAD), state_rwkv_wkv.dtype)
    gla_p0 = jnp.zeros((DEPTH, bp, GLA_HEADS, GLA_DK, GLA_DV), state_gla_kv.dtype)
    y_p, sh_p, wkv_p, gla_p = run(x_prompt, c_prompt, shift_p0, wkv_p0, gla_p0)
    y_s, sh_s, wkv_s, gla_s = run(x_sample, c_sample, state_rwkv_shift, state_rwkv_wkv, state_gla_kv)
    return (y_p, y_s, sh_p, wkv_p, gla_p, sh_s, wkv_s, gla_s)
```

```python
import functools

import jax
import jax.numpy as jnp
from jax import lax
from jax.experimental import pallas as pl
from jax.experimental.pallas import tpu as pltpu

F32 = jnp.float32
BF16 = jnp.bfloat16

LANES = 128
VMEM_LIMIT_BYTES = 56 * 1024 * 1024

D_MODEL = 1024
CHUNK = 64
RWKV_WIDTH = 512
RWKV_HEAD = 64
RWKV_HEADS = RWKV_WIDTH // RWKV_HEAD
RWKV_PAIRS = RWKV_HEADS // 2
DECAY_LORA = 32
AAA_LORA = 32
GATE_LORA = 64
LORA_PAD = DECAY_LORA + AAA_LORA + GATE_LORA
RWKV_PROJ = 3 * RWKV_WIDTH + LORA_PAD
GLA_WIDTH = 512
GLA_HEADS = 4
GLA_PAIRS = GLA_HEADS // 2
GLA_DV = GLA_WIDTH // GLA_HEADS
GLA_DK = GLA_DV // 2
GLA_KEY_WIDTH = GLA_HEADS * GLA_DK
GLA_GATE_RANK = 16
GLA_TAU = 16.0
GLA_PROJ = 2 * GLA_KEY_WIDTH + 2 * GLA_WIDTH + GLA_GATE_RANK
GLA_PROJ_PAD = RWKV_PROJ
IN_PROJ_PAD = RWKV_PROJ + GLA_PROJ_PAD
N_GROUPS = 4
EXPERTS_PER_GROUP = 4
N_EXPERTS = N_GROUPS * EXPERTS_PER_GROUP
D_EXPERT = 512
RMS_EPS = 1e-6
LNX_EPS = 64e-5

_NN = (((1,), (0,)), ((), ()))
_NT = (((1,), (1,)), ((), ()))
_TN = (((0,), (0,)), ((), ()))


def _dg(a, b, dims=_NN):
    return lax.dot_general(a, b, dims, preferred_element_type=F32)


def _split2(x):
    hi = x.astype(BF16)
    lo = (x - hi.astype(F32)).astype(BF16)
    return hi, lo


def _split3(x):
    hi = x.astype(BF16)
    r1 = x - hi.astype(F32)
    mid = r1.astype(BF16)
    lo = (r1 - mid.astype(F32)).astype(BF16)
    return hi, mid, lo


def _mm1(a, b, dims=_NN):
    return _dg(a.astype(BF16), b.astype(BF16), dims)


def _mm3(a, b, dims=_NN):
    ah, al = _split2(a)
    bh, bl = _split2(b)
    return _dg(ah, bh, dims) + (_dg(ah, bl, dims) + _dg(al, bh, dims))


def _mm_exact_lhs(e, x, dims=_NN):
    h, m, l = _split3(x)
    return _dg(e, h, dims) + (_dg(e, m, dims) + _dg(e, l, dims))


def _mm_exact_rhs(x, e, dims=_NN):
    h, m, l = _split3(x)
    return _dg(h, e, dims) + (_dg(m, e, dims) + _dg(l, e, dims))


def _softplus(z):
    return jnp.maximum(z, 0.0) + jnp.log(1.0 + jnp.exp(-jnp.abs(z)))


def _sigmoid(z):
    return 1.0 / (1.0 + jnp.exp(-z))


def _silu(z):
    return z * _sigmoid(z)


def _params(*sem):
    return pltpu.CompilerParams(dimension_semantics=sem, vmem_limit_bytes=VMEM_LIMIT_BYTES)


def _mod_kernel(c_ref, w_ref, b_ref, o_ref):
    o_ref[...] = _mm1(_silu(c_ref[...]), w_ref[...]) + b_ref[...]


def _modulation(c, w, b, tn):
    rows, d = c.shape
    n = w.shape[1]
    return pl.pallas_call(
        _mod_kernel,
        grid=(n // tn,),
        in_specs=[pl.BlockSpec((rows, d), lambda j: (0, 0)),
                  pl.BlockSpec((d, tn), lambda j: (0, j)),
                  pl.BlockSpec((1, tn), lambda j: (0, j))],
        out_specs=pl.BlockSpec((rows, tn), lambda j: (0, j)),
        out_shape=jax.ShapeDtypeStruct((rows, n), F32),
        compiler_params=_params("parallel"),
        name="modulation",
    )(c, w, b.reshape(1, n))


def _rms_mod(x, g, sc, sh):
    ms = jnp.mean(x * x, axis=-1, keepdims=True)
    return (x * lax.rsqrt(ms + RMS_EPS) * g) * (1.0 + sc) + sh


def _inproj_kernel(x_ref, sc_ref, sh_ref, g_ref, w_ref, o_ref, *, n_step):
    hb = _rms_mod(x_ref[0], g_ref[...], sc_ref[0], sh_ref[0]).astype(BF16)
    for j in range(IN_PROJ_PAD // n_step):
        cols = slice(j * n_step, (j + 1) * n_step)
        o_ref[0, :, cols] = _dg(hb, w_ref[:, cols])


def _in_proj(x, sc, sh, g, w, tm):
    bsz, seq, d = x.shape
    vec = pl.BlockSpec((1, 1, d), lambda b, i: (b, 0, 0))
    return pl.pallas_call(
        functools.partial(_inproj_kernel, n_step=2 * LANES),
        grid=(bsz, seq // tm),
        in_specs=[pl.BlockSpec((1, tm, d), lambda b, i: (b, i, 0)), vec, vec,
                  pl.BlockSpec((1, d), lambda b, i: (0, 0)),
                  pl.BlockSpec((d, IN_PROJ_PAD), lambda b, i: (0, 0))],
        out_specs=pl.BlockSpec((1, tm, IN_PROJ_PAD), lambda b, i: (b, i, 0)),
        out_shape=jax.ShapeDtypeStruct((bsz, seq, IN_PROJ_PAD), F32),
        compiler_params=_params("parallel", "parallel"),
        name="norm1_in_proj",
    )(x, sc, sh, g, w)


def _rwkv_prep_kernel(p_ref, prev_ref, mu_ref, w0_ref, a0_ref, kk_ref, ka_ref, rk_ref,
                      wd_ref, wa_ref, wg_ref, seg_ref, tri_ref, ones_ref, sel_ref,
                      at_ref, rt_ref, bt_ref, kt_ref, be_ref, ke_ref, v_ref, bonus_ref, g_ref, gl_ref):
    p = p_ref[0]
    tr = p.shape[0]
    row = lax.broadcasted_iota(jnp.int32, p.shape, 0)
    xx = jnp.where(row == 0, prev_ref[0, 0], pltpu.roll(p, 1, 0))
    ps = p + (xx - p) * mu_ref[...]
    w = RWKV_WIDTH
    r, k, v, lora = ps[:, :w], ps[:, w:2 * w], ps[:, 2 * w:3 * w], ps[:, 3 * w:]
    logw = -_softplus(-(w0_ref[...] + _mm1(jnp.tanh(lora), wd_ref[...]))) - 0.5
    lw = -jnp.exp(logw)
    a = _sigmoid(a0_ref[...] + _mm1(lora, wa_ref[...]))
    g = _mm1(_sigmoid(lora), wg_ref[...])
    seg = seg_ref[...]
    kk = k * kk_ref[...]
    kk = kk / jnp.maximum(jnp.sqrt(_mm_exact_rhs(kk * kk, seg)), 1e-12)
    k2 = k * (1.0 + (a - 1.0) * ka_ref[...])
    cum = _mm_exact_lhs(tri_ref[...], lw)
    tot = _mm_exact_lhs(ones_ref[...], lw)
    kka = kk * a
    ginv = jnp.exp(-cum)
    gend = jnp.exp(tot - cum)
    at_ref[0] = -kk * jnp.exp(cum - lw)
    rt_ref[0] = r * jnp.exp(cum)
    bt_ref[0] = kka * ginv
    kt_ref[0] = k2 * ginv
    be_ref[0] = kka * gend
    ke_ref[0] = k2 * gend
    v_ref[0] = v
    bonus_ref[0] = _mm_exact_rhs(r * k2 * rk_ref[...], seg) * v
    g_ref[0] = g
    if tr == CHUNK:
        gl_ref[0] = jnp.exp(jnp.sum(lw, axis=0, keepdims=True))
    else:
        gl_ref[0] = jnp.exp(_mm_exact_lhs(sel_ref[...], lw))


def _rwkv_prep(proj, prev, lp, consts, tr):
    bsz, seq, _ = proj.shape
    w = RWKV_WIDTH
    n_chunk = tr // CHUNK
    row = lambda n: pl.BlockSpec((1, n), lambda b, i: (0, 0))
    full = lambda a: pl.BlockSpec(a.shape, lambda b, i: (0,) * a.ndim)
    tok = pl.BlockSpec((1, tr, w), lambda b, i: (b, i, 0))
    out_tok = jax.ShapeDtypeStruct((bsz, seq, w), F32)
    tri, ones, sel = consts["tri"], consts["ones"], consts["sel"]
    return pl.pallas_call(
        _rwkv_prep_kernel,
        grid=(bsz, seq // tr),
        in_specs=[pl.BlockSpec((1, tr, RWKV_PROJ), lambda b, i: (b, i, 0)),
                  pl.BlockSpec((1, 1, 1, RWKV_PROJ), lambda b, i: (b, i, 0, 0)),
                  row(RWKV_PROJ), row(w), row(w), row(w), row(w), row(w),
                  full(lp["wd"]), full(lp["wa"]), full(lp["wg"]), full(consts["seg"]),
                  full(tri), full(ones), full(sel)],
        out_specs=[tok] * 9 + [pl.BlockSpec((1, n_chunk, w), lambda b, i: (b, i, 0))],
        out_shape=[out_tok] * 9 + [jax.ShapeDtypeStruct((bsz, seq // CHUNK, w), F32)],
        compiler_params=_params("parallel", "parallel"),
        name="rwkv_prep",
    )(proj, prev, lp["mu"], lp["w0"], lp["a0"], lp["k_k"], lp["k_a"], lp["r_k"],
      lp["wd"], lp["wa"], lp["wg"], consts["seg"], tri, ones, sel)


def _pair_masks():
    lane = lax.broadcasted_iota(jnp.int32, (CHUNK, LANES), 1)
    row = lax.broadcasted_iota(jnp.int32, (CHUNK, LANES), 0)
    first = lane < RWKV_HEAD
    col = jnp.where(first, lane, lane - RWKV_HEAD)
    return first, row, col


def _block_diag(x, first):
    z = jnp.zeros_like(x)
    return jnp.concatenate([jnp.where(first, x, z), jnp.where(first, z, x)], axis=0)


def _rwkv_chunk_kernel(at_ref, rt_ref, bt_ref, kt_ref, be_ref, ke_ref, v_ref, gl_ref, s0_ref,
                       y_ref, s_ref, *, bb):
    @pl.when(pl.program_id(1) == 0)
    def _():
        s_ref[...] = s0_ref[...]

    first, row, col = _pair_masks()
    strict = col < row
    incl = col <= row
    eye = jnp.where(col == row, 1.0, 0.0).astype(F32)
    same8 = (col >> 3) == (row >> 3)
    lane2 = lax.broadcasted_iota(jnp.int32, (LANES, LANES), 1)
    row2 = lax.broadcasted_iota(jnp.int32, (LANES, LANES), 0)
    same_head = (lane2 < RWKV_HEAD) == (row2 < RWKV_HEAD)
    bd = functools.partial(_block_diag, first=first)
    c = CHUNK

    def pmm(p, q):
        return _mm3(p, bd(q))

    for b in range(bb):
        for pr in range(RWKV_PAIRS):
            sl = slice(pr * LANES, (pr + 1) * LANES)
            at, rt, bt, kt = at_ref[b, :, sl], rt_ref[b, :, sl], bt_ref[b, :, sl], kt_ref[b, :, sl]
            be, ke, v = be_ref[b, :, sl], ke_ref[b, :, sl], v_ref[b, :, sl]
            gl = gl_ref[b, 0, :, sl]
            g_all = _mm3(jnp.concatenate([at, rt], axis=0),
                         jnp.concatenate([bd(kt), bd(bt)], axis=0), _NT)
            zero = jnp.zeros((c, LANES), F32)
            aak = jnp.where(strict, g_all[:c, :LANES], zero)
            aab = jnp.where(strict, g_all[:c, LANES:], zero)
            ark = jnp.where(incl, g_all[c:, :LANES], zero)
            arb = jnp.where(incl, g_all[c:, LANES:], zero)
            a8 = jnp.where(same8, aab, zero)
            p2 = pmm(a8, a8)
            p4 = pmm(p2, p2)
            t = eye + a8
            t = t + pmm(p2, t)
            t = t + pmm(p4, t)
            for lvl in (3, 4, 5):
                joins = ((col >> (lvl + 1)) == (row >> (lvl + 1))) & ((col >> lvl) != (row >> lvl))
                e = jnp.where(joins, aab, zero)
                t = t + pmm(pmm(t, e), t)
            z = pmm(aak, v)
            wu = _mm3(t, jnp.concatenate([bd(at), bd(z)], axis=1))
            w_, u0 = wu[:, :LANES], wu[:, LANES:]
            s = s_ref[b, pr]
            x = _mm3(jnp.concatenate([w_, rt], axis=0), s, _NT)
            u = x[:c] + u0
            y = x[c:] + _mm3(jnp.concatenate([arb, ark], axis=1),
                             jnp.concatenate([bd(u), bd(v)], axis=0))
            upd = _mm3(jnp.concatenate([u, v], axis=0), jnp.concatenate([be, ke], axis=0), _TN)
            y_ref[b, :, sl] = y
            s_ref[b, pr] = s * gl + jnp.where(same_head, upd, jnp.zeros_like(upd))


def _rwkv_chunk(prep, gl, s0, bb):
    bsz, seq, w = prep[0].shape
    tok = pl.BlockSpec((bb, CHUNK, w), lambda i, c: (i, c, 0))
    st = pl.BlockSpec((bb, RWKV_PAIRS, LANES, LANES), lambda i, c: (i, 0, 0, 0))
    return pl.pallas_call(
        functools.partial(_rwkv_chunk_kernel, bb=bb),
        grid=(bsz // bb, seq // CHUNK),
        in_specs=[tok] * 7 + [pl.BlockSpec((bb, 1, 1, w), lambda i, c: (i, c, 0, 0)), st],
        out_specs=[tok, st],
        out_shape=[jax.ShapeDtypeStruct((bsz, seq, w), F32),
                   jax.ShapeDtypeStruct((bsz, RWKV_PAIRS, LANES, LANES), F32)],
        compiler_params=_params("parallel", "arbitrary"),
        name="rwkv_chunk",
    )(*prep, gl, s0)


def _gla_kernel(p_ref, wgate_ref, bgate_ref, tri_ref, ng_ref, s0_ref, y_ref, s_ref, *, bb):
    @pl.when(pl.program_id(1) == 0)
    def _():
        s_ref[...] = s0_ref[...]

    first, row, col = _pair_masks()
    incl = col <= row
    bd = functools.partial(_block_diag, first=first)
    lane2 = lax.broadcasted_iota(jnp.int32, (2 * GLA_DV, LANES), 1)
    row2 = lax.broadcasted_iota(jnp.int32, (2 * GLA_DV, LANES), 0)
    same_head = (lane2 < GLA_DK) == (row2 < GLA_DV)
    kw, gw = GLA_KEY_WIDTH, GLA_WIDTH
    zero = jnp.zeros((CHUNK, LANES), F32)
    for b in range(bb):
        x = p_ref[b]
        q, k, v = x[:, :kw], x[:, kw:2 * kw], x[:, 2 * kw:2 * kw + gw]
        og, dgt = x[:, 2 * kw + gw:2 * kw + 2 * gw], x[:, 2 * kw + 2 * gw:]
        gate = _mm1(dgt, wgate_ref[...]) + bgate_ref[...]
        la = (jnp.minimum(gate, 0.0) - jnp.log(1.0 + jnp.exp(-jnp.abs(gate)))) / GLA_TAU
        cum = _mm_exact_lhs(tri_ref[...], la)
        last = cum[CHUNK - 1:CHUNK, :]
        q_dec = (q * (GLA_DK ** -0.5)) * jnp.exp(cum)
        k_inv = k * jnp.exp(-cum)
        k_end = k * jnp.exp(last - cum)
        dec = jnp.exp(last)
        outs = []
        for pr in range(GLA_PAIRS):
            sl = slice(pr * LANES, (pr + 1) * LANES)
            qd, ki, ke = q_dec[:, sl], k_inv[:, sl], k_end[:, sl]
            vp = v[:, pr * 2 * GLA_DV:(pr + 1) * 2 * GLA_DV]
            att = jnp.where(incl, _mm1(qd, bd(ki), _NT), zero)
            vbd = jnp.concatenate([jnp.concatenate([vp[:, :GLA_DV], zero], axis=1),
                                   jnp.concatenate([zero, vp[:, GLA_DV:]], axis=1)], axis=0)
            st = s_ref[b, pr]
            outs.append(_mm1(att, vbd) + _mm1(qd, st, _NT))
            upd = _mm1(vp, ke, _TN)
            s_ref[b, pr] = st * dec[:, sl] + jnp.where(same_head, upd, jnp.zeros_like(upd))
        normed = []
        for o in outs:
            for h in range(2):
                oh = o[:, h * GLA_DV:(h + 1) * GLA_DV]
                normed.append(oh * lax.rsqrt(jnp.mean(oh * oh, axis=-1, keepdims=True) + RMS_EPS))
        y_ref[b] = jnp.concatenate(normed, axis=1) * ng_ref[...] * _silu(og)


def _gla(proj, lp, consts, s0, bb):
    bsz, seq, _ = proj.shape
    full = lambda a: pl.BlockSpec(a.shape, lambda i, c: (0,) * a.ndim)
    st = pl.BlockSpec((bb, GLA_PAIRS, 2 * GLA_DV, LANES), lambda i, c: (i, 0, 0, 0))
    return pl.pallas_call(
        functools.partial(_gla_kernel, bb=bb),
        grid=(bsz // bb, seq // CHUNK),
        in_specs=[pl.BlockSpec((bb, CHUNK, GLA_PROJ_PAD), lambda i, c: (i, c, 1)),
                  full(lp["wgate"]), full(lp["bgate"]), full(consts["tri64"]), full(lp["gla_g"]), st],
        out_specs=[pl.BlockSpec((bb, CHUNK, GLA_WIDTH), lambda i, c: (i, c, 0)), st],
        out_shape=[jax.ShapeDtypeStruct((bsz, seq, GLA_WIDTH), F32),
                   jax.ShapeDtypeStruct((bsz, GLA_PAIRS, 2 * GLA_DV, LANES), F32)],
        compiler_params=_params("parallel", "arbitrary"),
        name="gla_chunk",
    )(proj, lp["wgate"], lp["bgate"], consts["tri64"], lp["gla_g"], s0)


def _route(lg):
    lane = lax.broadcasted_iota(jnp.int32, lg.shape, 1)
    lane_f = lane.astype(F32)
    neg = jnp.full_like(lg, -jnp.inf)
    big = jnp.full_like(lg, float(LANES))

    def first_argmax(vals, mx):
        return jnp.min(jnp.where(vals == mx, lane_f, big), axis=-1, keepdims=True)

    is_group = lane < N_GROUPS
    lgm = jnp.where(is_group, lg, neg)
    gmax = jnp.max(lgm, axis=-1, keepdims=True)
    gi = first_argmax(lgm, gmax)
    pg_top = 1.0 / jnp.sum(jnp.where(is_group, jnp.exp(lg - gmax), 0.0), axis=-1, keepdims=True)
    expert = lane - N_GROUPS
    in_group = (expert >= 0) & (expert < N_EXPERTS) & ((expert >> 2).astype(F32) == gi)
    le = jnp.where(in_group, lg, neg)
    m1 = jnp.max(le, axis=-1, keepdims=True)
    i1 = first_argmax(le, m1)
    le2 = jnp.where(lane_f == i1, neg, le)
    m2 = jnp.max(le2, axis=-1, keepdims=True)
    i2 = first_argmax(le2, m2)
    p2 = jnp.exp(m2 - m1)
    w1 = pg_top / (1.0 + p2)
    w2 = pg_top * p2 / (1.0 + p2)
    return jnp.where(lane_f == i1, w1, 0.0) + jnp.where(lane_f == i2, w2, 0.0)


def _outproj_kernel(yr_ref, bonus_ref, g_ref, yg_ref, x_ref, gt_ref, sc_ref, sh_ref,
                    lnw_ref, lnb_ref, seg_ref, wout_ref, g2_ref, wr_ref, br_ref,
                    x1_ref, h2_ref, comb_ref):
    seg = seg_ref[...]
    y = yr_ref[0]
    inv_n = 1.0 / RWKV_HEAD
    d = y - _mm_exact_rhs(y, seg) * inv_n
    var = _mm_exact_rhs(d * d, seg) * inv_n
    yn = d * lax.rsqrt(var + LNX_EPS) * lnw_ref[...] + lnb_ref[...]
    yr = (yn + bonus_ref[0]) * g_ref[0]
    mix = jnp.concatenate([yr, yg_ref[0]], axis=1)
    x1 = x_ref[0] + gt_ref[0] * _mm1(mix, wout_ref[...])
    x1_ref[0] = x1
    h2 = _rms_mod(x1, g2_ref[...], sc_ref[0], sh_ref[0])
    h2_ref[0] = h2.astype(BF16)
    comb_ref[0] = _route(_mm3(h2, wr_ref[...]) + br_ref[...])


def _out_proj(yr, bonus, g, yg, x, gt, sc, sh, lp, consts, tm):
    bsz, seq, d = x.shape
    half = pl.BlockSpec((1, tm, RWKV_WIDTH), lambda b, i: (b, i, 0))
    tokd = pl.BlockSpec((1, tm, d), lambda b, i: (b, i, 0))
    vec = pl.BlockSpec((1, 1, d), lambda b, i: (b, 0, 0))
    full = lambda a: pl.BlockSpec(a.shape, lambda b, i: (0,) * a.ndim)
    args = (lp["lnx_w"], lp["lnx_b"], consts["seg"], lp["w_out"], lp["norm2_g"], lp["w_router"], lp["b_router"])
    return pl.pallas_call(
        _outproj_kernel,
        grid=(bsz, seq // tm),
        in_specs=[half, half, half, half, tokd, vec, vec, vec] + [full(a) for a in args],
        out_specs=[tokd, tokd, pl.BlockSpec((1, tm, LANES), lambda b, i: (b, i, 0))],
        out_shape=[jax.ShapeDtypeStruct((bsz, seq, d), F32),
                   jax.ShapeDtypeStruct((bsz, seq, d), BF16),
                   jax.ShapeDtypeStruct((bsz, seq, LANES), F32)],
        compiler_params=_params("parallel", "parallel"),
        name="out_proj_router",
    )(yr, bonus, g, yg, x, gt, sc, sh, *args)


def _moe_kernel(h_ref, comb_ref, x1_ref, gt_ref, sc_ref, sh_ref, gf_ref, wg_ref, wu_ref, wd_ref,
                y_ref, acc_ref):
    e = pl.program_id(2)
    h = h_ref[0]
    hid = _silu(_dg(h, wg_ref[0])) * _dg(h, wu_ref[0])
    lane = lax.broadcasted_iota(jnp.int32, comb_ref.shape[1:], 1)
    ce = jnp.sum(jnp.where(lane == e + N_GROUPS, comb_ref[0], 0.0), axis=-1, keepdims=True)
    contrib = ce * _dg(hid.astype(BF16), wd_ref[0])

    @pl.when(e == 0)
    def _():
        acc_ref[...] = contrib

    @pl.when(e != 0)
    def _():
        acc_ref[...] += contrib

    @pl.when(e == N_EXPERTS - 1)
    def _():
        x2 = x1_ref[0] + gt_ref[0] * acc_ref[...]
        y_ref[0] = _rms_mod(x2, gf_ref[...], sc_ref[0], sh_ref[0])


def _moe(h2, comb, x1, gt, sc, sh, gf, wg, wu, wd, tm):
    bsz, seq, d = x1.shape
    tokd = pl.BlockSpec((1, tm, d), lambda b, i, e: (b, i, 0))
    vec = pl.BlockSpec((1, 1, d), lambda b, i, e: (b, 0, 0))
    return pl.pallas_call(
        _moe_kernel,
        grid=(bsz, seq // tm, N_EXPERTS),
        in_specs=[tokd, pl.BlockSpec((1, tm, LANES), lambda b, i, e: (b, i, 0)), tokd, vec, vec, vec,
                  pl.BlockSpec((1, d), lambda b, i, e: (0, 0)),
                  pl.BlockSpec((1, d, D_EXPERT), lambda b, i, e: (e, 0, 0)),
                  pl.BlockSpec((1, d, D_EXPERT), lambda b, i, e: (e, 0, 0)),
                  pl.BlockSpec((1, D_EXPERT, d), lambda b, i, e: (e, 0, 0))],
        out_specs=tokd,
        out_shape=jax.ShapeDtypeStruct((bsz, seq, d), F32),
        scratch_shapes=[pltpu.VMEM((tm, d), F32)],
        compiler_params=_params("parallel", "parallel", "arbitrary"),
        name="moe_final_norm",
    )(h2, comb, x1, gt, sc, sh, gf, wg, wu, wd)


def _block_ones(n, blk, lower):
    i = jnp.arange(n)
    m = (i[:, None] // blk) == (i[None, :] // blk)
    if lower:
        m = m & (i[None, :] <= i[:, None])
    return m.astype(BF16)


def _consts(tr):
    sel = ((jnp.arange(tr)[None, :] // CHUNK) == jnp.arange(max(tr // CHUNK, 1))[:, None]).astype(BF16)
    return dict(seg=_block_ones(RWKV_WIDTH, RWKV_HEAD, False),
                tri=_block_ones(tr, CHUNK, True), ones=_block_ones(tr, CHUNK, False), sel=sel,
                tri64=_block_ones(CHUNK, CHUNK, True))


def _pad_rows(w, first_row):
    out = jnp.zeros((LORA_PAD, w.shape[1]), F32)
    return lax.dynamic_update_slice(out, w, (first_row, 0)).astype(BF16)


def _layer_params(l, w_in, mu_shift, w0, w_decay_up, a0, w_a_up, w_g_up, k_k, k_a, r_k, lnx_w, lnx_b,
                  w_gla_gate_up, b_gla_gate, gla_norm_g, w_out, norm2_g,
                  w_router_group, b_router_group, w_router_expert, b_router_expert):
    wi = w_in[l]
    wg_ = wi[:, RWKV_PROJ:]
    qkv = 2 * GLA_KEY_WIDTH + GLA_WIDTH
    w_in_p = jnp.concatenate(
        [wi[:, :RWKV_PROJ], wg_[:, :qkv], wg_[:, qkv + GLA_GATE_RANK:], wg_[:, qkv:qkv + GLA_GATE_RANK],
         jnp.zeros((D_MODEL, GLA_PROJ_PAD - GLA_PROJ), F32)], axis=1).astype(BF16)
    n_route = N_GROUPS + N_EXPERTS
    w_router = jnp.concatenate([w_router_group[l], w_router_expert[l],
                                jnp.zeros((D_MODEL, LANES - n_route), F32)], axis=1)
    b_router = jnp.concatenate([b_router_group[l], b_router_expert[l],
                                jnp.zeros((LANES - n_route,), F32)]).reshape(1, LANES)
    wgate = jnp.zeros((LANES, GLA_KEY_WIDTH), F32).at[:GLA_GATE_RANK].set(w_gla_gate_up[l]).astype(BF16)
    r1 = lambda a: a.reshape(1, -1)
    return dict(
        w_in=w_in_p, mu=r1(mu_shift[l]), w0=r1(w0[l]), a0=r1(a0[l]), k_k=r1(k_k[l]), k_a=r1(k_a[l]),
        r_k=r1(r_k[l]), wd=_pad_rows(w_decay_up[l], 0), wa=_pad_rows(w_a_up[l], DECAY_LORA),
        wg=_pad_rows(w_g_up[l], DECAY_LORA + AAA_LORA), lnx_w=r1(lnx_w[l]), lnx_b=r1(lnx_b[l]),
        wgate=wgate, bgate=r1(b_gla_gate[l]), gla_g=r1(gla_norm_g[l]),
        w_out=w_out[l].astype(BF16), norm2_g=r1(norm2_g[l]), w_router=w_router, b_router=b_router)


def _run_layer(x, mod, shift0, wkv0, gla0, lp, experts, final):
    bsz, seq, d = x.shape
    tm = min(seq, 512)
    tr = min(seq, 512)
    bb = 2
    consts = _consts(tr)
    m = lambda j: mod[:, j:j + 1, :]
    sh1, sc1, gt1, sh2, sc2, gt2 = (m(j) for j in range(6))
    proj = _in_proj(x, sc1, sh1, lp["norm1_g"], lp["w_in"], tm)
    new_shift = proj[:, -1, :RWKV_PROJ]
    tails = proj[:, tr - 1::tr, :RWKV_PROJ][:, :-1]
    prev = jnp.concatenate([shift0[:, None, :], tails], axis=1)[:, :, None, :]
    *prep, gl = _rwkv_prep(proj, prev, lp, consts, tr)
    at, rt, bt, kt, be, ke, v, bonus, g = prep
    s0 = jnp.zeros((bsz, RWKV_PAIRS, 2, RWKV_HEAD, 2, RWKV_HEAD), F32)
    w5 = wkv0.reshape(bsz, RWKV_PAIRS, 2, RWKV_HEAD, RWKV_HEAD)
    s0 = s0.at[:, :, 0, :, 0, :].set(w5[:, :, 0]).at[:, :, 1, :, 1, :].set(w5[:, :, 1])
    s0 = s0.reshape(bsz, RWKV_PAIRS, LANES, LANES)
    yr, s_bd = _rwkv_chunk((at, rt, bt, kt, be, ke, v), gl[:, :, None, :], s0, bb)
    s6 = s_bd.reshape(bsz, RWKV_PAIRS, 2, RWKV_HEAD, 2, RWKV_HEAD)
    new_wkv = jnp.stack([s6[:, :, 0, :, 0, :], s6[:, :, 1, :, 1, :]], axis=2).reshape(
        bsz, RWKV_HEADS, RWKV_HEAD, RWKV_HEAD)
    g5 = jnp.swapaxes(gla0, -1, -2).reshape(bsz, GLA_PAIRS, 2, GLA_DV, GLA_DK)
    t0 = jnp.zeros((bsz, GLA_PAIRS, 2, GLA_DV, 2, GLA_DK), F32)
    t0 = t0.at[:, :, 0, :, 0, :].set(g5[:, :, 0]).at[:, :, 1, :, 1, :].set(g5[:, :, 1])
    yg, t_bd = _gla(proj, lp, consts, t0.reshape(bsz, GLA_PAIRS, 2 * GLA_DV, LANES), bb)
    t6 = t_bd.reshape(bsz, GLA_PAIRS, 2, GLA_DV, 2, GLA_DK)
    new_gla = jnp.swapaxes(
        jnp.stack([t6[:, :, 0, :, 0, :], t6[:, :, 1, :, 1, :]], axis=2).reshape(
            bsz, GLA_HEADS, GLA_DV, GLA_DK), -1, -2)
    x1, h2, comb = _out_proj(yr, bonus, g, yg, x, gt1, sc2, sh2, lp, consts, tm)
    out = _moe(h2, comb, x1, gt2, *final, *experts, min(seq, 1024))
    return out, new_shift, new_wkv, new_gla


def kernel(x_prompt, x_sample, c_prompt, c_sample, state_rwkv_shift, state_rwkv_wkv, state_gla_kv, w_ada, b_ada, norm1_g, norm2_g, w_in, mu_shift, w0, w_decay_up, a0, w_a_up, w_g_up, k_k, k_a, r_k, lnx_w, lnx_b, w_gla_gate_up, b_gla_gate, gla_norm_g, w_out, w_router_group, b_router_group, w_router_expert, b_router_expert, w_expert_gate, w_expert_up, w_expert_down, w_ada_final, b_ada_final, normf_g):
    assert w_ada.shape[0] == 1, "the final norm is fused into the single layer's MoE kernel"
    bp, bs = x_prompt.shape[0], x_sample.shape[0]
    d = D_MODEL
    n_rows = -(-(bp + bs) // 8) * 8
    c_all = jnp.zeros((n_rows, d), F32).at[:bp].set(c_prompt).at[bp:bp + bs].set(c_sample)
    modf = _modulation(c_all, w_ada_final, b_ada_final, 1024).reshape(n_rows, 2, d)
    mod = _modulation(c_all, w_ada[0], b_ada[0], 1536).reshape(n_rows, 6, d)
    lp = _layer_params(0, w_in, mu_shift, w0, w_decay_up, a0, w_a_up, w_g_up, k_k, k_a, r_k, lnx_w,
                       lnx_b, w_gla_gate_up, b_gla_gate, gla_norm_g, w_out, norm2_g,
                       w_router_group, b_router_group, w_router_expert, b_router_expert)
    lp["norm1_g"] = norm1_g[0].reshape(1, d)
    experts = (w_expert_gate[0].astype(BF16), w_expert_up[0].astype(BF16), w_expert_down[0].astype(BF16))
    groups = [
        (x_prompt, 0, bp, jnp.zeros((bp, RWKV_PROJ), F32),
         jnp.zeros((bp, RWKV_HEADS, RWKV_HEAD, RWKV_HEAD), F32), jnp.zeros((bp, GLA_HEADS, GLA_DK, GLA_DV), F32)),
        (x_sample, bp, bp + bs, state_rwkv_shift[0], state_rwkv_wkv[0], state_gla_kv[0]),
    ]
    ys, states = [], []
    for x, lo, hi, shift0, wkv0, gla0 in groups:
        final = (modf[lo:hi, 1:2], modf[lo:hi, 0:1], normf_g.reshape(1, d))
        y, *st = _run_layer(x, mod[lo:hi], shift0, wkv0, gla0, lp, experts, final)
        ys.append(y)
        states.extend(s[None] for s in st)
    return tuple(ys + states)
```

```python
import functools

import jax
import jax.numpy as jnp
from jax import lax
from jax.experimental import pallas as pl
from jax.experimental.pallas import tpu as pltpu

F32 = jnp.float32
BF16 = jnp.bfloat16

LANES = 128
VMEM_LIMIT_BYTES = 56 * 1024 * 1024
TOKEN_TILE_ROWS = 512
MOE_TILE_ROWS = 1024
GLA_STEP_ROWS = 256

D_MODEL = 1024
CHUNK = 64
RWKV_WIDTH = 512
RWKV_HEAD = 64
RWKV_HEADS = RWKV_WIDTH // RWKV_HEAD
RWKV_PAIRS = RWKV_HEADS // 2
DECAY_LORA = 32
AAA_LORA = 32
GATE_LORA = 64
LORA_PAD = DECAY_LORA + AAA_LORA + GATE_LORA
RWKV_PROJ = 3 * RWKV_WIDTH + LORA_PAD
GLA_WIDTH = 512
GLA_HEADS = 4
GLA_PAIRS = GLA_HEADS // 2
GLA_DV = GLA_WIDTH // GLA_HEADS
GLA_DK = GLA_DV // 2
GLA_KEY_WIDTH = GLA_HEADS * GLA_DK
GLA_GATE_RANK = 16
GLA_TAU = 16.0
GLA_PROJ = 2 * GLA_KEY_WIDTH + 2 * GLA_WIDTH + GLA_GATE_RANK
GLA_PROJ_PAD = RWKV_PROJ
IN_PROJ_PAD = RWKV_PROJ + GLA_PROJ_PAD
N_GROUPS = 4
EXPERTS_PER_GROUP = 4
N_EXPERTS = N_GROUPS * EXPERTS_PER_GROUP
D_EXPERT = 512
RMS_EPS = 1e-6
LNX_EPS = 64e-5

_NN = (((1,), (0,)), ((), ()))
_NT = (((1,), (1,)), ((), ()))
_TN = (((0,), (0,)), ((), ()))


def _dg(a, b, dims=_NN):
    return lax.dot_general(a, b, dims, preferred_element_type=F32)


def _split2(x):
    hi = x.astype(BF16)
    lo = (x - hi.astype(F32)).astype(BF16)
    return hi, lo


def _split3(x):
    hi = x.astype(BF16)
    r1 = x - hi.astype(F32)
    mid = r1.astype(BF16)
    lo = (r1 - mid.astype(F32)).astype(BF16)
    return hi, mid, lo


def _mm1(a, b, dims=_NN):
    return _dg(a.astype(BF16), b.astype(BF16), dims)


def _mm3(a, b, dims=_NN):
    ah, al = _split2(a)
    bh, bl = _split2(b)
    return _dg(ah, bh, dims) + (_dg(ah, bl, dims) + _dg(al, bh, dims))


def _mm_exact_lhs(e, x, dims=_NN):
    h, m, l = _split3(x)
    return _dg(e, h, dims) + (_dg(e, m, dims) + _dg(e, l, dims))


def _mm_exact_rhs(x, e, dims=_NN):
    h, m, l = _split3(x)
    return _dg(h, e, dims) + (_dg(m, e, dims) + _dg(l, e, dims))


def _softplus(z):
    return jnp.maximum(z, 0.0) + jnp.log(1.0 + jnp.exp(-jnp.abs(z)))


def _sigmoid(z):
    return 1.0 / (1.0 + jnp.exp(-z))


def _silu(z):
    return z * _sigmoid(z)


def _params(*sem):
    return pltpu.CompilerParams(dimension_semantics=sem, vmem_limit_bytes=VMEM_LIMIT_BYTES)


def _mod_kernel(c_ref, w_ref, b_ref, o_ref):
    o_ref[...] = _mm1(_silu(c_ref[...]), w_ref[...]) + b_ref[...]


def _modulation(c, w, b, tn):
    rows, d = c.shape
    n = w.shape[1]
    return pl.pallas_call(
        _mod_kernel,
        grid=(n // tn,),
        in_specs=[pl.BlockSpec((rows, d), lambda j: (0, 0)),
                  pl.BlockSpec((d, tn), lambda j: (0, j)),
                  pl.BlockSpec((1, tn), lambda j: (0, j))],
        out_specs=pl.BlockSpec((rows, tn), lambda j: (0, j)),
        out_shape=jax.ShapeDtypeStruct((rows, n), F32),
        compiler_params=_params("parallel"),
        name="modulation",
    )(c, w, b.reshape(1, n))


def _rms_mod(x, g, sc, sh):
    ms = jnp.mean(x * x, axis=-1, keepdims=True)
    return (x * lax.rsqrt(ms + RMS_EPS) * g) * (1.0 + sc) + sh


def _token_tile(bsz, seq, rows):
    if seq >= rows:
        assert seq % rows == 0
        return 1, rows
    nb = min(bsz, rows // seq)
    assert bsz % nb == 0
    return nb, seq


def _inproj_kernel(x_ref, sc_ref, sh_ref, g_ref, w_ref, o_ref, *, n_step):
    nb, rb, d = x_ref.shape
    h = _rms_mod(x_ref[...], g_ref[...], sc_ref[...], sh_ref[...])
    hb = h.reshape(nb * rb, d).astype(BF16)
    for j in range(IN_PROJ_PAD // n_step):
        cols = slice(j * n_step, (j + 1) * n_step)
        o_ref[:, :, cols] = _dg(hb, w_ref[:, cols]).reshape(nb, rb, n_step)


def _in_proj(x, sc, sh, g, w, rows):
    bsz, seq, d = x.shape
    nb, rb = _token_tile(bsz, seq, rows)
    vec = pl.BlockSpec((nb, 1, d), lambda b, i: (b, 0, 0))
    return pl.pallas_call(
        functools.partial(_inproj_kernel, n_step=2 * LANES),
        grid=(bsz // nb, seq // rb),
        in_specs=[pl.BlockSpec((nb, rb, d), lambda b, i: (b, i, 0)), vec, vec,
                  pl.BlockSpec((1, d), lambda b, i: (0, 0)),
                  pl.BlockSpec((d, IN_PROJ_PAD), lambda b, i: (0, 0))],
        out_specs=pl.BlockSpec((nb, rb, IN_PROJ_PAD), lambda b, i: (b, i, 0)),
        out_shape=jax.ShapeDtypeStruct((bsz, seq, IN_PROJ_PAD), F32),
        compiler_params=_params("parallel", "parallel"),
        name="norm1_in_proj",
    )(x, sc, sh, g, w)


def _rwkv_prep_kernel(p_ref, prev_ref, mu_ref, w0_ref, a0_ref, kk_ref, ka_ref, rk_ref,
                      wd_ref, wa_ref, wg_ref, seg_ref, tri_ref, ones_ref, sel_ref,
                      at_ref, rt_ref, bt_ref, kt_ref, be_ref, ke_ref, v_ref, bonus_ref, g_ref, gl_ref):
    nb, rb, wp = p_ref.shape
    tr = nb * rb
    p = p_ref[...].reshape(tr, wp)
    row = lax.broadcasted_iota(jnp.int32, (nb, rb, wp), 1)
    xx = jnp.where(row == 0, prev_ref[...], pltpu.roll(p, 1, 0).reshape(nb, rb, wp)).reshape(tr, wp)
    ps = p + (xx - p) * mu_ref[...]
    w = RWKV_WIDTH
    r, k, v, lora = ps[:, :w], ps[:, w:2 * w], ps[:, 2 * w:3 * w], ps[:, 3 * w:]
    logw = -_softplus(-(w0_ref[...] + _mm1(jnp.tanh(lora), wd_ref[...]))) - 0.5
    lw = -jnp.exp(logw)
    a = _sigmoid(a0_ref[...] + _mm1(lora, wa_ref[...]))
    g = _mm1(_sigmoid(lora), wg_ref[...])
    seg = seg_ref[...]
    kk = k * kk_ref[...]
    kk = kk / jnp.maximum(jnp.sqrt(_mm_exact_rhs(kk * kk, seg)), 1e-12)
    k2 = k * (1.0 + (a - 1.0) * ka_ref[...])
    cum = _mm_exact_lhs(tri_ref[...], lw)
    tot = _mm_exact_lhs(ones_ref[...], lw)
    kka = kk * a
    ginv = jnp.exp(-cum)
    gend = jnp.exp(tot - cum)
    def put(ref, val):
        ref[...] = val.reshape(nb, rb, RWKV_WIDTH)

    put(at_ref, -kk * jnp.exp(cum - lw))
    put(rt_ref, r * jnp.exp(cum))
    put(bt_ref, kka * ginv)
    put(kt_ref, k2 * ginv)
    put(be_ref, kka * gend)
    put(ke_ref, k2 * gend)
    put(v_ref, v)
    put(bonus_ref, _mm_exact_rhs(r * k2 * rk_ref[...], seg) * v)
    put(g_ref, g)
    gl_ref[...] = jnp.exp(_mm_exact_lhs(sel_ref[...], lw))


def _rwkv_prep(proj, prev, lp, consts, rows):
    bsz, seq, _ = proj.shape
    w = RWKV_WIDTH
    nb, rb = _token_tile(bsz, seq, rows)
    n_seq_tiles = seq // rb
    flat = lambda b, i: (b * n_seq_tiles + i, 0)
    row = lambda n: pl.BlockSpec((1, n), lambda b, i: (0, 0))
    full = lambda a: pl.BlockSpec(a.shape, lambda b, i: (0,) * a.ndim)
    tok = pl.BlockSpec((nb, rb, w), lambda b, i: (b, i, 0))
    out_tok = jax.ShapeDtypeStruct((bsz, seq, w), F32)
    tri, ones, sel = consts["tri"], consts["ones"], consts["sel"]
    return pl.pallas_call(
        _rwkv_prep_kernel,
        grid=(bsz // nb, n_seq_tiles),
        in_specs=[pl.BlockSpec((nb, rb, RWKV_PROJ), lambda b, i: (b, i, 0)),
                  pl.BlockSpec((nb, 1, RWKV_PROJ), lambda b, i: flat(b, i) + (0,)),
                  row(RWKV_PROJ), row(w), row(w), row(w), row(w), row(w),
                  full(lp["wd"]), full(lp["wa"]), full(lp["wg"]), full(consts["seg"]),
                  full(tri), full(ones), full(sel)],
        out_specs=[tok] * 9 + [pl.BlockSpec((nb * rb // CHUNK, w), flat)],
        out_shape=[out_tok] * 9 + [jax.ShapeDtypeStruct((bsz * seq // CHUNK, w), F32)],
        compiler_params=_params("parallel", "parallel"),
        name="rwkv_prep",
    )(proj, prev, lp["mu"], lp["w0"], lp["a0"], lp["k_k"], lp["k_a"], lp["r_k"],
      lp["wd"], lp["wa"], lp["wg"], consts["seg"], tri, ones, sel)


def _pair_masks():
    lane = lax.broadcasted_iota(jnp.int32, (CHUNK, LANES), 1)
    row = lax.broadcasted_iota(jnp.int32, (CHUNK, LANES), 0)
    first = lane < RWKV_HEAD
    col = jnp.where(first, lane, lane - RWKV_HEAD)
    return first, row, col


def _block_diag(x, first):
    z = jnp.zeros_like(x)
    return jnp.concatenate([jnp.where(first, x, z), jnp.where(first, z, x)], axis=0)


def _rwkv_chunk_kernel(at_ref, rt_ref, bt_ref, kt_ref, be_ref, ke_ref, v_ref, gl_ref, s0_ref,
                       y_ref, s_ref, *, bb):
    @pl.when(pl.program_id(1) == 0)
    def _():
        s_ref[...] = s0_ref[...]

    first, row, col = _pair_masks()
    strict = col < row
    incl = col <= row
    eye = jnp.where(col == row, 1.0, 0.0).astype(F32)
    same8 = (col >> 3) == (row >> 3)
    lane2 = lax.broadcasted_iota(jnp.int32, (LANES, LANES), 1)
    row2 = lax.broadcasted_iota(jnp.int32, (LANES, LANES), 0)
    same_head = (lane2 < RWKV_HEAD) == (row2 < RWKV_HEAD)
    bd = functools.partial(_block_diag, first=first)
    c = CHUNK

    def pmm(p, q):
        return _mm3(p, bd(q))

    pairs = [(b, slice(pr * LANES, (pr + 1) * LANES), pr) for b in range(bb) for pr in range(RWKV_PAIRS)]
    n = range(len(pairs))
    ld = lambda ref: [ref[b, :, sl] for b, sl, _ in pairs]
    cat0 = lambda *xs: jnp.concatenate(xs, axis=0)
    cat1 = lambda *xs: jnp.concatenate(xs, axis=1)
    at, rt, bt, kt, be, ke, v = (ld(r) for r in (at_ref, rt_ref, bt_ref, kt_ref, be_ref, ke_ref, v_ref))
    zero = jnp.zeros((c, LANES), F32)
    g_all = [_mm3(cat0(at[i], rt[i]), cat0(bd(kt[i]), bd(bt[i])), _NT) for i in n]
    aak = [jnp.where(strict, g_all[i][:c, :LANES], zero) for i in n]
    aab = [jnp.where(strict, g_all[i][:c, LANES:], zero) for i in n]
    ark = [jnp.where(incl, g_all[i][c:, :LANES], zero) for i in n]
    arb = [jnp.where(incl, g_all[i][c:, LANES:], zero) for i in n]
    a8 = [jnp.where(same8, aab[i], zero) for i in n]
    p2 = [pmm(a8[i], a8[i]) for i in n]
    z = [pmm(aak[i], v[i]) for i in n]
    p4 = [pmm(p2[i], p2[i]) for i in n]
    t = [eye + a8[i] for i in n]
    t = [t[i] + pmm(p2[i], t[i]) for i in n]
    t = [t[i] + pmm(p4[i], t[i]) for i in n]
    for lvl in (3, 4, 5):
        joins = ((col >> (lvl + 1)) == (row >> (lvl + 1))) & ((col >> lvl) != (row >> lvl))
        te = [pmm(t[i], jnp.where(joins, aab[i], zero)) for i in n]
        t = [t[i] + pmm(te[i], t[i]) for i in n]
    wu = [_mm3(t[i], cat1(bd(at[i]), bd(z[i]))) for i in n]
    s = [s_ref[b, pr] for b, _, pr in pairs]
    x = [_mm3(cat0(wu[i][:, :LANES], rt[i]), s[i], _NT) for i in n]
    u = [x[i][:c] + wu[i][:, LANES:] for i in n]
    y = [x[i][c:] + _mm3(cat1(arb[i], ark[i]), cat0(bd(u[i]), bd(v[i]))) for i in n]
    upd = [_mm3(cat0(u[i], v[i]), cat0(be[i], ke[i]), _TN) for i in n]
    for i, (b, sl, pr) in enumerate(pairs):
        y_ref[b, :, sl] = y[i]
        s_ref[b, pr] = s[i] * gl_ref[b, 0, :, sl] + jnp.where(same_head, upd[i], jnp.zeros_like(upd[i]))


def _rwkv_chunk(prep, gl, s0, bb):
    bsz, seq, w = prep[0].shape
    tok = pl.BlockSpec((bb, CHUNK, w), lambda i, c: (i, c, 0))
    st = pl.BlockSpec((bb, RWKV_PAIRS, LANES, LANES), lambda i, c: (i, 0, 0, 0))
    return pl.pallas_call(
        functools.partial(_rwkv_chunk_kernel, bb=bb),
        grid=(bsz // bb, seq // CHUNK),
        in_specs=[tok] * 7 + [pl.BlockSpec((bb, 1, 1, w), lambda i, c: (i, c, 0, 0)), st],
        out_specs=[tok, st],
        out_shape=[jax.ShapeDtypeStruct((bsz, seq, w), F32),
                   jax.ShapeDtypeStruct((bsz, RWKV_PAIRS, LANES, LANES), F32)],
        compiler_params=_params("parallel", "arbitrary"),
        name="rwkv_chunk",
    )(*prep, gl, s0)


def _gla_kernel(p_ref, wgate_ref, bgate_ref, tri_ref, ng_ref, s0_ref, y_ref, s_ref, *, bb, nck):
    @pl.when(pl.program_id(1) == 0)
    def _():
        s_ref[...] = s0_ref[...]

    first, row, col = _pair_masks()
    incl = col <= row
    bd = functools.partial(_block_diag, first=first)
    lane2 = lax.broadcasted_iota(jnp.int32, (2 * GLA_DV, LANES), 1)
    row2 = lax.broadcasted_iota(jnp.int32, (2 * GLA_DV, LANES), 0)
    same_head = (lane2 < GLA_DK) == (row2 < GLA_DV)
    kw, gw = GLA_KEY_WIDTH, GLA_WIDTH
    zero = jnp.zeros((CHUNK, LANES), F32)
    cat0 = lambda xs: jnp.concatenate(xs, axis=0)
    cat1 = lambda xs: jnp.concatenate(xs, axis=1)
    rows = lambda c: slice(c * CHUNK, (c + 1) * CHUNK)
    lanes = lambda pr: slice(pr * LANES, (pr + 1) * LANES)
    bs, cs = range(bb), range(nck)
    x = [p_ref[b] for b in bs]
    gate = [_mm1(x[b][:, 2 * kw + 2 * gw:], wgate_ref[...]) + bgate_ref[...] for b in bs]
    la = [(jnp.minimum(g, 0.0) - jnp.log(1.0 + jnp.exp(-jnp.abs(g)))) / GLA_TAU for g in gate]
    cum = [_mm_exact_lhs(tri_ref[...], la[b]) for b in bs]
    q_dec = [(x[b][:, :kw] * (GLA_DK ** -0.5)) * jnp.exp(cum[b]) for b in bs]
    k_inv = [x[b][:, kw:2 * kw] * jnp.exp(-cum[b]) for b in bs]
    last = [[cum[b][c * CHUNK + CHUNK - 1:(c + 1) * CHUNK, :] for c in cs] for b in bs]
    k_end = [[x[b][rows(c), kw:2 * kw] * jnp.exp(last[b][c] - cum[b][rows(c)]) for c in cs] for b in bs]
    dec = [[jnp.exp(last[b][c]) for c in cs] for b in bs]
    items = [(b, c, pr) for b in bs for c in cs for pr in range(GLA_PAIRS)]
    vp = {(b, c, pr): x[b][rows(c), 2 * kw + pr * 2 * GLA_DV:2 * kw + (pr + 1) * 2 * GLA_DV]
          for b, c, pr in items}
    att = {(b, c, pr): jnp.where(incl, _mm1(q_dec[b][rows(c), lanes(pr)],
                                            bd(k_inv[b][rows(c), lanes(pr)]), _NT), zero)
           for b, c, pr in items}
    upd = {it: _mm1(vp[it], k_end[it[0]][it[1]][:, lanes(it[2])], _TN) for it in items}
    intra = {it: _mm1(att[it], cat0([cat1([vp[it][:, :GLA_DV], zero]), cat1([zero, vp[it][:, GLA_DV:]])]))
             for it in items}
    st = {}
    for b in bs:
        for pr in range(GLA_PAIRS):
            cur = s_ref[b, pr]
            for c in cs:
                st[b, c, pr] = cur
                cur = cur * dec[b][c][:, lanes(pr)] + jnp.where(same_head, upd[b, c, pr], jnp.zeros_like(cur))
            s_ref[b, pr] = cur
    o = {(b, c, pr): intra[b, c, pr] + _mm1(q_dec[b][rows(c), lanes(pr)], st[b, c, pr], _NT)
         for b, c, pr in items}
    for b in bs:
        ob = cat0([cat1([o[b, c, pr] for pr in range(GLA_PAIRS)]) for c in cs])
        heads = [ob[:, h * GLA_DV:(h + 1) * GLA_DV] for h in range(GLA_HEADS)]
        normed = [oh * lax.rsqrt(jnp.mean(oh * oh, axis=-1, keepdims=True) + RMS_EPS) for oh in heads]
        y_ref[b] = cat1(normed) * ng_ref[...] * _silu(x[b][:, 2 * kw + gw:2 * kw + 2 * gw])


def _gla(proj, lp, tri, s0, bb, tg):
    bsz, seq, _ = proj.shape
    full = lambda a: pl.BlockSpec(a.shape, lambda i, c: (0,) * a.ndim)
    st = pl.BlockSpec((bb, GLA_PAIRS, 2 * GLA_DV, LANES), lambda i, c: (i, 0, 0, 0))
    return pl.pallas_call(
        functools.partial(_gla_kernel, bb=bb, nck=tg // CHUNK),
        grid=(bsz // bb, seq // tg),
        in_specs=[pl.BlockSpec((bb, tg, GLA_PROJ_PAD), lambda i, c: (i, c, 1)),
                  full(lp["wgate"]), full(lp["bgate"]), full(tri), full(lp["gla_g"]), st],
        out_specs=[pl.BlockSpec((bb, tg, GLA_WIDTH), lambda i, c: (i, c, 0)), st],
        out_shape=[jax.ShapeDtypeStruct((bsz, seq, GLA_WIDTH), F32),
                   jax.ShapeDtypeStruct((bsz, GLA_PAIRS, 2 * GLA_DV, LANES), F32)],
        compiler_params=_params("parallel", "arbitrary"),
        name="gla_chunk",
    )(proj, lp["wgate"], lp["bgate"], tri, lp["gla_g"], s0)


def _route(lg):
    lane = lax.broadcasted_iota(jnp.int32, lg.shape, 1)
    lane_f = lane.astype(F32)
    neg = jnp.full_like(lg, -jnp.inf)
    big = jnp.full_like(lg, float(LANES))

    def first_argmax(vals, mx):
        return jnp.min(jnp.where(vals == mx, lane_f, big), axis=-1, keepdims=True)

    is_group = lane < N_GROUPS
    lgm = jnp.where(is_group, lg, neg)
    gmax = jnp.max(lgm, axis=-1, keepdims=True)
    gi = first_argmax(lgm, gmax)
    pg_top = 1.0 / jnp.sum(jnp.where(is_group, jnp.exp(lg - gmax), 0.0), axis=-1, keepdims=True)
    expert = lane - N_GROUPS
    in_group = (expert >= 0) & (expert < N_EXPERTS) & ((expert >> 2).astype(F32) == gi)
    le = jnp.where(in_group, lg, neg)
    m1 = jnp.max(le, axis=-1, keepdims=True)
    i1 = first_argmax(le, m1)
    le2 = jnp.where(lane_f == i1, neg, le)
    m2 = jnp.max(le2, axis=-1, keepdims=True)
    i2 = first_argmax(le2, m2)
    p2 = jnp.exp(m2 - m1)
    w1 = pg_top / (1.0 + p2)
    w2 = pg_top * p2 / (1.0 + p2)
    return jnp.where(lane_f == i1, w1, 0.0) + jnp.where(lane_f == i2, w2, 0.0)


def _outproj_kernel(yr_ref, bonus_ref, g_ref, yg_ref, x_ref, gt_ref, sc_ref, sh_ref,
                    lnw_ref, lnb_ref, seg_ref, wout_ref, g2_ref, wr_ref, br_ref,
                    x1_ref, h2_ref, comb_ref):
    nb, rb, d_model = x_ref.shape
    tm = nb * rb
    flat = lambda ref: ref[...].reshape(tm, ref.shape[-1])
    seg = seg_ref[...]
    y = flat(yr_ref)
    inv_n = 1.0 / RWKV_HEAD
    d = y - _mm_exact_rhs(y, seg) * inv_n
    var = _mm_exact_rhs(d * d, seg) * inv_n
    yn = d * lax.rsqrt(var + LNX_EPS) * lnw_ref[...] + lnb_ref[...]
    yr = (yn + flat(bonus_ref)) * flat(g_ref)
    mix = jnp.concatenate([yr, flat(yg_ref)], axis=1)
    x1 = x_ref[...] + gt_ref[...] * _mm1(mix, wout_ref[...]).reshape(nb, rb, d_model)
    x1_ref[...] = x1
    h2 = _rms_mod(x1, g2_ref[...], sc_ref[...], sh_ref[...])
    h2_ref[...] = h2.astype(BF16)
    comb = _route(_mm3(h2.reshape(tm, d_model), wr_ref[...]) + br_ref[...])
    comb_ref[...] = comb.reshape(nb, rb, LANES)


def _out_proj(yr, bonus, g, yg, x, gt, sc, sh, lp, consts, rows):
    bsz, seq, d = x.shape
    nb, rb = _token_tile(bsz, seq, rows)
    half = pl.BlockSpec((nb, rb, RWKV_WIDTH), lambda b, i: (b, i, 0))
    tokd = pl.BlockSpec((nb, rb, d), lambda b, i: (b, i, 0))
    vec = pl.BlockSpec((nb, 1, d), lambda b, i: (b, 0, 0))
    full = lambda a: pl.BlockSpec(a.shape, lambda b, i: (0,) * a.ndim)
    args = (lp["lnx_w"], lp["lnx_b"], consts["seg"], lp["w_out"], lp["norm2_g"], lp["w_router"], lp["b_router"])
    return pl.pallas_call(
        _outproj_kernel,
        grid=(bsz // nb, seq // rb),
        in_specs=[half, half, half, half, tokd, vec, vec, vec] + [full(a) for a in args],
        out_specs=[tokd, tokd, pl.BlockSpec((nb, rb, LANES), lambda b, i: (b, i, 0))],
        out_shape=[jax.ShapeDtypeStruct((bsz, seq, d), F32),
                   jax.ShapeDtypeStruct((bsz, seq, d), BF16),
                   jax.ShapeDtypeStruct((bsz, seq, LANES), F32)],
        compiler_params=_params("parallel", "parallel"),
        name="out_proj_router",
    )(yr, bonus, g, yg, x, gt, sc, sh, *args)


def _moe_kernel(h_ref, comb_ref, x1_ref, gt_ref, sc_ref, sh_ref, gf_ref, wg_ref, wu_ref, wd_ref,
                y_ref, acc_ref):
    e = pl.program_id(2)
    nb, rb, d = h_ref.shape
    tm = nb * rb
    h = h_ref[...].reshape(tm, d)
    hid = _silu(_dg(h, wg_ref[0])) * _dg(h, wu_ref[0])
    comb = comb_ref[...].reshape(tm, LANES)
    lane = lax.broadcasted_iota(jnp.int32, comb.shape, 1)
    ce = jnp.sum(jnp.where(lane == e + N_GROUPS, comb, 0.0), axis=-1, keepdims=True)
    contrib = ce * _dg(hid.astype(BF16), wd_ref[0])

    @pl.when(e == 0)
    def _():
        acc_ref[...] = contrib

    @pl.when(e != 0)
    def _():
        acc_ref[...] += contrib

    @pl.when(e == N_EXPERTS - 1)
    def _():
        x2 = x1_ref[...] + gt_ref[...] * acc_ref[...].reshape(nb, rb, d)
        y_ref[...] = _rms_mod(x2, gf_ref[...], sc_ref[...], sh_ref[...])


def _moe(h2, comb, x1, gt, sc, sh, gf, wg, wu, wd, rows):
    bsz, seq, d = x1.shape
    nb, rb = _token_tile(bsz, seq, rows)
    tm = nb * rb
    tokd = pl.BlockSpec((nb, rb, d), lambda b, i, e: (b, i, 0))
    vec = pl.BlockSpec((nb, 1, d), lambda b, i, e: (b, 0, 0))
    return pl.pallas_call(
        _moe_kernel,
        grid=(bsz // nb, seq // rb, N_EXPERTS),
        in_specs=[tokd, pl.BlockSpec((nb, rb, LANES), lambda b, i, e: (b, i, 0)), tokd, vec, vec, vec,
                  pl.BlockSpec((1, d), lambda b, i, e: (0, 0)),
                  pl.BlockSpec((1, d, D_EXPERT), lambda b, i, e: (e, 0, 0)),
                  pl.BlockSpec((1, d, D_EXPERT), lambda b, i, e: (e, 0, 0)),
                  pl.BlockSpec((1, D_EXPERT, d), lambda b, i, e: (e, 0, 0))],
        out_specs=tokd,
        out_shape=jax.ShapeDtypeStruct((bsz, seq, d), F32),
        scratch_shapes=[pltpu.VMEM((tm, d), F32)],
        compiler_params=_params("parallel", "parallel", "arbitrary"),
        name="moe_final_norm",
    )(h2, comb, x1, gt, sc, sh, gf, wg, wu, wd)


def _block_ones(n, blk, lower):
    i = jnp.arange(n)
    m = (i[:, None] // blk) == (i[None, :] // blk)
    if lower:
        m = m & (i[None, :] <= i[:, None])
    return m.astype(BF16)


def _consts(tr, tg):
    sel = ((jnp.arange(tr)[None, :] // CHUNK) == jnp.arange(tr // CHUNK)[:, None]).astype(BF16)
    return dict(seg=_block_ones(RWKV_WIDTH, RWKV_HEAD, False),
                tri=_block_ones(tr, CHUNK, True), ones=_block_ones(tr, CHUNK, False), sel=sel,
                tri_gla=_block_ones(tg, CHUNK, True))


def _pad_rows(w, first_row):
    out = jnp.zeros((LORA_PAD, w.shape[1]), F32)
    return lax.dynamic_update_slice(out, w, (first_row, 0)).astype(BF16)


def _layer_params(l, w_in, mu_shift, w0, w_decay_up, a0, w_a_up, w_g_up, k_k, k_a, r_k, lnx_w, lnx_b,
                  w_gla_gate_up, b_gla_gate, gla_norm_g, w_out, norm2_g,
                  w_router_group, b_router_group, w_router_expert, b_router_expert):
    wi = w_in[l]
    wg_ = wi[:, RWKV_PROJ:]
    qkv = 2 * GLA_KEY_WIDTH + GLA_WIDTH
    w_in_p = jnp.concatenate(
        [wi[:, :RWKV_PROJ], wg_[:, :qkv], wg_[:, qkv + GLA_GATE_RANK:], wg_[:, qkv:qkv + GLA_GATE_RANK],
         jnp.zeros((D_MODEL, GLA_PROJ_PAD - GLA_PROJ), F32)], axis=1).astype(BF16)
    n_route = N_GROUPS + N_EXPERTS
    w_router = jnp.concatenate([w_router_group[l], w_router_expert[l],
                                jnp.zeros((D_MODEL, LANES - n_route), F32)], axis=1)
    b_router = jnp.concatenate([b_router_group[l], b_router_expert[l],
                                jnp.zeros((LANES - n_route,), F32)]).reshape(1, LANES)
    wgate = jnp.zeros((LANES, GLA_KEY_WIDTH), F32).at[:GLA_GATE_RANK].set(w_gla_gate_up[l]).astype(BF16)
    r1 = lambda a: a.reshape(1, -1)
    return dict(
        w_in=w_in_p, mu=r1(mu_shift[l]), w0=r1(w0[l]), a0=r1(a0[l]), k_k=r1(k_k[l]), k_a=r1(k_a[l]),
        r_k=r1(r_k[l]), wd=_pad_rows(w_decay_up[l], 0), wa=_pad_rows(w_a_up[l], DECAY_LORA),
        wg=_pad_rows(w_g_up[l], DECAY_LORA + AAA_LORA), lnx_w=r1(lnx_w[l]), lnx_b=r1(lnx_b[l]),
        wgate=wgate, bgate=r1(b_gla_gate[l]), gla_g=r1(gla_norm_g[l]),
        w_out=w_out[l].astype(BF16), norm2_g=r1(norm2_g[l]), w_router=w_router, b_router=b_router)


def _run_layer(x, mod, shift0, wkv0, gla0, lp, experts, final):
    bsz, seq, d = x.shape
    assert seq % CHUNK == 0
    bb = 2
    tg = min(seq, GLA_STEP_ROWS)
    consts = _consts(TOKEN_TILE_ROWS, tg)
    m = lambda j: mod[:, j:j + 1, :]
    sh1, sc1, gt1, sh2, sc2, gt2 = (m(j) for j in range(6))
    proj = _in_proj(x, sc1, sh1, lp["norm1_g"], lp["w_in"], TOKEN_TILE_ROWS)
    new_shift = proj[:, -1, :RWKV_PROJ]
    rb = _token_tile(bsz, seq, TOKEN_TILE_ROWS)[1]
    tails = proj[:, rb - 1::rb, :RWKV_PROJ][:, :-1]
    prev = jnp.concatenate([shift0[:, None, :], tails], axis=1).reshape(bsz * (seq // rb), 1, RWKV_PROJ)
    *prep, gl = _rwkv_prep(proj, prev, lp, consts, TOKEN_TILE_ROWS)
    gl = gl.reshape(bsz, seq // CHUNK, RWKV_WIDTH)
    at, rt, bt, kt, be, ke, v, bonus, g = prep
    s0 = jnp.zeros((bsz, RWKV_PAIRS, 2, RWKV_HEAD, 2, RWKV_HEAD), F32)
    w5 = wkv0.reshape(bsz, RWKV_PAIRS, 2, RWKV_HEAD, RWKV_HEAD)
    s0 = s0.at[:, :, 0, :, 0, :].set(w5[:, :, 0]).at[:, :, 1, :, 1, :].set(w5[:, :, 1])
    s0 = s0.reshape(bsz, RWKV_PAIRS, LANES, LANES)
    yr, s_bd = _rwkv_chunk((at, rt, bt, kt, be, ke, v), gl[:, :, None, :], s0, bb)
    s6 = s_bd.reshape(bsz, RWKV_PAIRS, 2, RWKV_HEAD, 2, RWKV_HEAD)
    new_wkv = jnp.stack([s6[:, :, 0, :, 0, :], s6[:, :, 1, :, 1, :]], axis=2).reshape(
        bsz, RWKV_HEADS, RWKV_HEAD, RWKV_HEAD)
    g5 = jnp.swapaxes(gla0, -1, -2).reshape(bsz, GLA_PAIRS, 2, GLA_DV, GLA_DK)
    t0 = jnp.zeros((bsz, GLA_PAIRS, 2, GLA_DV, 2, GLA_DK), F32)
    t0 = t0.at[:, :, 0, :, 0, :].set(g5[:, :, 0]).at[:, :, 1, :, 1, :].set(g5[:, :, 1])
    yg, t_bd = _gla(proj, lp, consts["tri_gla"], t0.reshape(bsz, GLA_PAIRS, 2 * GLA_DV, LANES), bb, tg)
    t6 = t_bd.reshape(bsz, GLA_PAIRS, 2, GLA_DV, 2, GLA_DK)
    new_gla = jnp.swapaxes(
        jnp.stack([t6[:, :, 0, :, 0, :], t6[:, :, 1, :, 1, :]], axis=2).reshape(
            bsz, GLA_HEADS, GLA_DV, GLA_DK), -1, -2)
    x1, h2, comb = _out_proj(yr, bonus, g, yg, x, gt1, sc2, sh2, lp, consts, TOKEN_TILE_ROWS)
    out = _moe(h2, comb, x1, gt2, *final, *experts, MOE_TILE_ROWS)
    return out, new_shift, new_wkv, new_gla


def kernel(x_prompt, x_sample, c_prompt, c_sample, state_rwkv_shift, state_rwkv_wkv, state_gla_kv, w_ada, b_ada, norm1_g, norm2_g, w_in, mu_shift, w0, w_decay_up, a0, w_a_up, w_g_up, k_k, k_a, r_k, lnx_w, lnx_b, w_gla_gate_up, b_gla_gate, gla_norm_g, w_out, w_router_group, b_router_group, w_router_expert, b_router_expert, w_expert_gate, w_expert_up, w_expert_down, w_ada_final, b_ada_final, normf_g):
    assert w_ada.shape[0] == 1, "the final norm is fused into the single layer's MoE kernel"
    bp, bs = x_prompt.shape[0], x_sample.shape[0]
    d = D_MODEL
    n_rows = -(-(bp + bs) // 8) * 8
    c_all = jnp.zeros((n_rows, d), F32).at[:bp].set(c_prompt).at[bp:bp + bs].set(c_sample)
    modf = _modulation(c_all, w_ada_final, b_ada_final, 1024).reshape(n_rows, 2, d)
    mod = _modulation(c_all, w_ada[0], b_ada[0], 1536).reshape(n_rows, 6, d)
    lp = _layer_params(0, w_in, mu_shift, w0, w_decay_up, a0, w_a_up, w_g_up, k_k, k_a, r_k, lnx_w,
                       lnx_b, w_gla_gate_up, b_gla_gate, gla_norm_g, w_out, norm2_g,
                       w_router_group, b_router_group, w_router_expert, b_router_expert)
    lp["norm1_g"] = norm1_g[0].reshape(1, d)
    experts = (w_expert_gate[0].astype(BF16), w_expert_up[0].astype(BF16), w_expert_down[0].astype(BF16))
    groups = [
        (x_prompt, 0, bp, jnp.zeros((bp, RWKV_PROJ), F32),
         jnp.zeros((bp, RWKV_HEADS, RWKV_HEAD, RWKV_HEAD), F32), jnp.zeros((bp, GLA_HEADS, GLA_DK, GLA_DV), F32)),
        (x_sample, bp, bp + bs, state_rwkv_shift[0], state_rwkv_wkv[0], state_gla_kv[0]),
    ]
    ys, states = [], []
    for x, lo, hi, shift0, wkv0, gla0 in groups:
        final = (modf[lo:hi, 1:2], modf[lo:hi, 0:1], normf_g.reshape(1, d))
        y, *st = _run_layer(x, mod[lo:hi], shift0, wkv0, gla0, lp, experts, final)
        ys.append(y)
        states.extend(s[None] for s in st)
    return tuple(ys + states)
```

```python
import functools

import jax
import jax.numpy as jnp
from jax import lax
from jax.experimental import pallas as pl
from jax.experimental.pallas import tpu as pltpu

F32 = jnp.float32
BF16 = jnp.bfloat16

LANES = 128
VMEM_LIMIT_BYTES = 56 * 1024 * 1024
TOKEN_TILE_ROWS = 512
MOE_TILE_ROWS = 1024
GLA_STEP_ROWS = 256
RWKV_STEP_ROWS = 128

D_MODEL = 1024
CHUNK = 64
RWKV_WIDTH = 512
RWKV_HEAD = 64
RWKV_HEADS = RWKV_WIDTH // RWKV_HEAD
RWKV_PAIRS = RWKV_HEADS // 2
DECAY_LORA = 32
AAA_LORA = 32
GATE_LORA = 64
LORA_PAD = DECAY_LORA + AAA_LORA + GATE_LORA
RWKV_PROJ = 3 * RWKV_WIDTH + LORA_PAD
GLA_WIDTH = 512
GLA_HEADS = 4
GLA_PAIRS = GLA_HEADS // 2
GLA_DV = GLA_WIDTH // GLA_HEADS
GLA_DK = GLA_DV // 2
GLA_KEY_WIDTH = GLA_HEADS * GLA_DK
GLA_GATE_RANK = 16
GLA_TAU = 16.0
GLA_PROJ = 2 * GLA_KEY_WIDTH + 2 * GLA_WIDTH + GLA_GATE_RANK
GLA_PROJ_PAD = RWKV_PROJ
IN_PROJ_PAD = RWKV_PROJ + GLA_PROJ_PAD
N_GROUPS = 4
EXPERTS_PER_GROUP = 4
N_EXPERTS = N_GROUPS * EXPERTS_PER_GROUP
D_EXPERT = 512
RMS_EPS = 1e-6
LNX_EPS = 64e-5

_NN = (((1,), (0,)), ((), ()))
_NT = (((1,), (1,)), ((), ()))
_TN = (((0,), (0,)), ((), ()))


def _dg(a, b, dims=_NN):
    return lax.dot_general(a, b, dims, preferred_element_type=F32)


def _split2(x):
    hi = x.astype(BF16)
    lo = (x - hi.astype(F32)).astype(BF16)
    return hi, lo


def _split3(x):
    hi = x.astype(BF16)
    r1 = x - hi.astype(F32)
    mid = r1.astype(BF16)
    lo = (r1 - mid.astype(F32)).astype(BF16)
    return hi, mid, lo


def _mm1(a, b, dims=_NN):
    return _dg(a.astype(BF16), b.astype(BF16), dims)


def _mm3(a, b, dims=_NN):
    ah, al = _split2(a)
    bh, bl = _split2(b)
    return _dg(ah, bh, dims) + (_dg(ah, bl, dims) + _dg(al, bh, dims))


def _mm_exact_lhs(e, x, dims=_NN):
    h, m, l = _split3(x)
    return _dg(e, h, dims) + (_dg(e, m, dims) + _dg(e, l, dims))


def _mm2_exact_rhs(x, e, dims=_NN):
    h, l = _split2(x)
    return _dg(h, e, dims) + _dg(l, e, dims)


def _softplus(z):
    return jnp.maximum(z, 0.0) + jnp.log(1.0 + jnp.exp(-jnp.abs(z)))


def _sigmoid(z):
    return 1.0 / (1.0 + jnp.exp(-z))


def _silu(z):
    return z * _sigmoid(z)


def _params(*sem):
    return pltpu.CompilerParams(dimension_semantics=sem, vmem_limit_bytes=VMEM_LIMIT_BYTES)


def _mod_kernel(c_ref, w_ref, b_ref, o_ref):
    o_ref[...] = _mm1(_silu(c_ref[...]), w_ref[...]) + b_ref[...]


def _modulation(c, w, b, tn):
    rows, d = c.shape
    n = w.shape[1]
    return pl.pallas_call(
        _mod_kernel,
        grid=(n // tn,),
        in_specs=[pl.BlockSpec((rows, d), lambda j: (0, 0)),
                  pl.BlockSpec((d, tn), lambda j: (0, j)),
                  pl.BlockSpec((1, tn), lambda j: (0, j))],
        out_specs=pl.BlockSpec((rows, tn), lambda j: (0, j)),
        out_shape=jax.ShapeDtypeStruct((rows, n), F32),
        compiler_params=_params("parallel"),
        name="modulation",
    )(c, w, b.reshape(1, n))


def _rms_mod(x, g, sc, sh):
    ms = jnp.mean(x * x, axis=-1, keepdims=True)
    return (x * lax.rsqrt(ms + RMS_EPS) * g) * (1.0 + sc) + sh


def _token_tile(bsz, seq, rows):
    if seq >= rows:
        assert seq % rows == 0
        return 1, rows
    nb = min(bsz, rows // seq)
    assert bsz % nb == 0
    return nb, seq


def _inproj_kernel(x_ref, sc_ref, sh_ref, g_ref, w_ref, o_ref, last_ref, *, n_step):
    nb, rb, d = x_ref.shape
    h = _rms_mod(x_ref[...], g_ref[...], sc_ref[...], sh_ref[...])
    hb = h.reshape(nb * rb, d).astype(BF16)
    for j in range(IN_PROJ_PAD // n_step):
        cols = slice(j * n_step, (j + 1) * n_step)
        o_ref[:, :, cols] = _dg(hb, w_ref[:, cols]).reshape(nb, rb, n_step)
    last_ref[...] = o_ref[:, rb - 1:rb, :RWKV_PROJ]


def _in_proj(x, sc, sh, g, w, rows):
    bsz, seq, d = x.shape
    nb, rb = _token_tile(bsz, seq, rows)
    n_seq_tiles = seq // rb
    vec = pl.BlockSpec((nb, 1, d), lambda b, i: (b, 0, 0))
    return pl.pallas_call(
        functools.partial(_inproj_kernel, n_step=2 * LANES),
        grid=(bsz // nb, n_seq_tiles),
        in_specs=[pl.BlockSpec((nb, rb, d), lambda b, i: (b, i, 0)), vec, vec,
                  pl.BlockSpec((1, d), lambda b, i: (0, 0)),
                  pl.BlockSpec((d, IN_PROJ_PAD), lambda b, i: (0, 0))],
        out_specs=[pl.BlockSpec((nb, rb, IN_PROJ_PAD), lambda b, i: (b, i, 0)),
                   pl.BlockSpec((nb, 1, RWKV_PROJ), lambda b, i: (b * n_seq_tiles + i, 0, 0))],
        out_shape=[jax.ShapeDtypeStruct((bsz, seq, IN_PROJ_PAD), F32),
                   jax.ShapeDtypeStruct((bsz * n_seq_tiles, 1, RWKV_PROJ), F32)],
        compiler_params=_params("parallel", "parallel"),
        name="norm1_in_proj",
    )(x, sc, sh, g, w)


def _rwkv_prep_kernel(p_ref, prev_ref, mu_ref, w0_ref, a0_ref, kk_ref, ka_ref, rk_ref,
                      wd_ref, wa_ref, wg_ref, seg_ref, tri_ref,
                      at_ref, bt_ref, rt_ref, kt_ref, be_ref, ke_ref, v_ref, bonus_ref, g_ref, gl_ref):
    nb, rb, wp = p_ref.shape
    tr = nb * rb
    p = p_ref[...].reshape(tr, wp)
    row = lax.broadcasted_iota(jnp.int32, (nb, rb, wp), 1)
    xx = jnp.where(row == 0, prev_ref[...], pltpu.roll(p, 1, 0).reshape(nb, rb, wp)).reshape(tr, wp)
    ps = p + (xx - p) * mu_ref[...]
    w = RWKV_WIDTH
    r, k, v, lora = ps[:, :w], ps[:, w:2 * w], ps[:, 2 * w:3 * w], ps[:, 3 * w:]
    logw = -_softplus(-(w0_ref[...] + _mm1(jnp.tanh(lora), wd_ref[...]))) - 0.5
    lw = -jnp.exp(logw)
    a = _sigmoid(a0_ref[...] + _mm1(lora, wa_ref[...]))
    g = _mm1(_sigmoid(lora), wg_ref[...])
    seg = seg_ref[...]
    kk = k * kk_ref[...]
    kk = kk / jnp.maximum(jnp.sqrt(_mm2_exact_rhs(kk * kk, seg)), 1e-12)
    k2 = k * (1.0 + (a - 1.0) * ka_ref[...])
    cum = _mm_exact_lhs(tri_ref[...], lw)
    lasts = [cum[c * CHUNK + CHUNK - 1:(c + 1) * CHUNK, :] for c in range(tr // CHUNK)]
    tot = jnp.concatenate([jnp.broadcast_to(l, (CHUNK, w)) for l in lasts], axis=0)
    for c, l in enumerate(lasts):
        gl_ref[c:c + 1, :] = jnp.exp(l)
    kka = kk * a
    ginv = jnp.exp(-cum)
    gend = jnp.exp(tot - cum)

    def put(ref, val):
        ref[...] = val.reshape(nb, rb, RWKV_WIDTH).astype(ref.dtype)

    put(at_ref, -kk * jnp.exp(cum - lw))
    put(bt_ref, kka * ginv)
    put(rt_ref, r * jnp.exp(cum))
    put(kt_ref, k2 * ginv)
    put(be_ref, kka * gend)
    put(ke_ref, k2 * gend)
    put(v_ref, v)
    put(bonus_ref, _mm2_exact_rhs(r * k2 * rk_ref[...], seg) * v)
    put(g_ref, g)


def _rwkv_prep(proj, prev, lp, consts, rows):
    bsz, seq, _ = proj.shape
    w = RWKV_WIDTH
    nb, rb = _token_tile(bsz, seq, rows)
    n_seq_tiles = seq // rb
    flat = lambda b, i: (b * n_seq_tiles + i, 0)
    row = lambda n: pl.BlockSpec((1, n), lambda b, i: (0, 0))
    full = lambda a: pl.BlockSpec(a.shape, lambda b, i: (0,) * a.ndim)
    tok = pl.BlockSpec((nb, rb, w), lambda b, i: (b, i, 0))
    out_tok = lambda dt: jax.ShapeDtypeStruct((bsz, seq, w), dt)
    dtypes = [F32, F32, BF16, BF16, BF16, BF16, BF16, F32, F32]
    return pl.pallas_call(
        _rwkv_prep_kernel,
        grid=(bsz // nb, n_seq_tiles),
        in_specs=[pl.BlockSpec((nb, rb, RWKV_PROJ), lambda b, i: (b, i, 0)),
                  pl.BlockSpec((nb, 1, RWKV_PROJ), lambda b, i: flat(b, i) + (0,)),
                  row(RWKV_PROJ), row(w), row(w), row(w), row(w), row(w),
                  full(lp["wd"]), full(lp["wa"]), full(lp["wg"]), full(consts["seg"]), full(consts["tri"])],
        out_specs=[tok] * 9 + [pl.BlockSpec((nb * rb // CHUNK, w), flat)],
        out_shape=[out_tok(dt) for dt in dtypes] + [jax.ShapeDtypeStruct((bsz * seq // CHUNK, w), F32)],
        compiler_params=_params("parallel", "parallel"),
        name="rwkv_prep",
    )(proj, prev, lp["mu"], lp["w0"], lp["a0"], lp["k_k"], lp["k_a"], lp["r_k"],
      lp["wd"], lp["wa"], lp["wg"], consts["seg"], consts["tri"])


def _pair_masks():
    lane = lax.broadcasted_iota(jnp.int32, (CHUNK, LANES), 1)
    row = lax.broadcasted_iota(jnp.int32, (CHUNK, LANES), 0)
    first = lane < RWKV_HEAD
    col = jnp.where(first, lane, lane - RWKV_HEAD)
    return first, row, col


def _block_diag(x, first):
    z = jnp.zeros_like(x)
    return jnp.concatenate([jnp.where(first, x, z), jnp.where(first, z, x)], axis=0)


def _rwkv_chunk_kernel(at_ref, bt_ref, rt_ref, kt_ref, be_ref, ke_ref, v_ref, gl_ref, s0_ref,
                       y_ref, s_ref, *, bb, nck):
    @pl.when(pl.program_id(1) == 0)
    def _():
        s_ref[...] = s0_ref[...]

    first, row, col = _pair_masks()
    strict = col < row
    incl = col <= row
    same8 = (col >> 3) == (row >> 3)
    lane2 = lax.broadcasted_iota(jnp.int32, (LANES, LANES), 1)
    row2 = lax.broadcasted_iota(jnp.int32, (LANES, LANES), 0)
    same_head = (lane2 < RWKV_HEAD) == (row2 < RWKV_HEAD)
    bd = functools.partial(_block_diag, first=first)
    c = CHUNK

    def pmm(p, q):
        return _dg(p.astype(BF16), bd(q.astype(BF16)))

    items = [(b, ck, pr) for b in range(bb) for ck in range(nck) for pr in range(RWKV_PAIRS)]
    n = range(len(items))
    rows = lambda ck: slice(ck * c, (ck + 1) * c)
    lanes = lambda pr: slice(pr * LANES, (pr + 1) * LANES)
    ld = lambda ref: [ref[b, rows(ck), lanes(pr)] for b, ck, pr in items]
    cat0 = lambda *xs: jnp.concatenate(xs, axis=0)
    cat1 = lambda *xs: jnp.concatenate(xs, axis=1)
    at, bt = ld(at_ref), ld(bt_ref)
    rt, kt, be, ke, v = (ld(r) for r in (rt_ref, kt_ref, be_ref, ke_ref, v_ref))
    atb = [a.astype(BF16) for a in at]
    zero = jnp.zeros((c, LANES), F32)
    gk = [_dg(cat0(atb[i], rt[i]), bd(kt[i]), _NT) for i in n]
    aak = [jnp.where(strict, gk[i][:c], zero) for i in n]
    ark = [jnp.where(incl, gk[i][c:], zero) for i in n]
    arb = [jnp.where(incl, _dg(rt[i], bd(bt[i].astype(BF16)), _NT), zero) for i in n]
    aab = [jnp.where(strict, _mm3(at[i], bd(bt[i]), _NT), zero) for i in n]
    z = [pmm(aak[i], v[i]) for i in n]
    a8 = [jnp.where(same8, aab[i], zero) for i in n]
    p2 = [pmm(a8[i], a8[i]) for i in n]
    p4 = [pmm(p2[i], p2[i]) for i in n]
    nn = [a8[i] + p2[i] + pmm(p2[i], a8[i]) for i in n]
    nn = [nn[i] + p4[i] + pmm(p4[i], nn[i]) for i in n]
    for lvl in (3, 4, 5):
        joins = ((col >> (lvl + 1)) == (row >> (lvl + 1))) & ((col >> lvl) != (row >> lvl))
        e = [jnp.where(joins, aab[i], zero) for i in n]
        te = [e[i] + pmm(nn[i], e[i]) for i in n]
        nn = [nn[i] + te[i] + pmm(te[i], nn[i]) for i in n]
    wu = [cat1(at[i], z[i]) + _dg(nn[i].astype(BF16), cat1(bd(atb[i]), bd(z[i].astype(BF16)))) for i in n]
    abk = [cat1(arb[i], ark[i]).astype(BF16) for i in n]
    s = {(b, pr): s_ref[b, pr] for b in range(bb) for pr in range(RWKV_PAIRS)}
    for ck in range(nck):
        cur = [i for i in n if items[i][1] == ck]
        key = lambda i: (items[i][0], items[i][2])
        x = {i: _dg(cat0(wu[i][:, :LANES].astype(BF16), rt[i]), s[key(i)].astype(BF16), _NT) for i in cur}
        ub = {i: (x[i][:c] + wu[i][:, LANES:]).astype(BF16) for i in cur}
        upd = {i: _dg(cat0(ub[i], v[i]), cat0(be[i], ke[i]), _TN) for i in cur}
        for i in cur:
            b, _, pr = items[i]
            y_ref[b, rows(ck), lanes(pr)] = x[i][c:] + _dg(abk[i], cat0(bd(ub[i]), bd(v[i])))
            s[b, pr] = (s[b, pr] * gl_ref[b, ck, :, lanes(pr)]
                        + jnp.where(same_head, upd[i], jnp.zeros_like(upd[i])))
    for (b, pr), val in s.items():
        s_ref[b, pr] = val


def _rwkv_chunk(prep, gl, s0, bb, nck):
    bsz, seq, w = prep[0].shape
    tok = pl.BlockSpec((bb, nck * CHUNK, w), lambda i, c: (i, c, 0))
    st = pl.BlockSpec((bb, RWKV_PAIRS, LANES, LANES), lambda i, c: (i, 0, 0, 0))
    return pl.pallas_call(
        functools.partial(_rwkv_chunk_kernel, bb=bb, nck=nck),
        grid=(bsz // bb, seq // (nck * CHUNK)),
        in_specs=[tok] * 7 + [pl.BlockSpec((bb, nck, 1, w), lambda i, c: (i, c, 0, 0)), st],
        out_specs=[tok, st],
        out_shape=[jax.ShapeDtypeStruct((bsz, seq, w), F32),
                   jax.ShapeDtypeStruct((bsz, RWKV_PAIRS, LANES, LANES), F32)],
        compiler_params=_params("parallel", "arbitrary"),
        name="rwkv_chunk",
    )(*prep, gl, s0)


def _gla_kernel(p_ref, wgate_ref, bgate_ref, tri_ref, ng_ref, s0_ref, y_ref, s_ref, *, bb, nck):
    @pl.when(pl.program_id(1) == 0)
    def _():
        s_ref[...] = s0_ref[...]

    first, row, col = _pair_masks()
    incl = col <= row
    bd = functools.partial(_block_diag, first=first)
    lane2 = lax.broadcasted_iota(jnp.int32, (2 * GLA_DV, LANES), 1)
    row2 = lax.broadcasted_iota(jnp.int32, (2 * GLA_DV, LANES), 0)
    same_head = (lane2 < GLA_DK) == (row2 < GLA_DV)
    kw, gw = GLA_KEY_WIDTH, GLA_WIDTH
    zero = jnp.zeros((CHUNK, LANES), F32)
    cat0 = lambda xs: jnp.concatenate(xs, axis=0)
    cat1 = lambda xs: jnp.concatenate(xs, axis=1)
    rows = lambda c: slice(c * CHUNK, (c + 1) * CHUNK)
    lanes = lambda pr: slice(pr * LANES, (pr + 1) * LANES)
    bs, cs = range(bb), range(nck)
    x = [p_ref[b] for b in bs]
    gate = [_mm1(x[b][:, 2 * kw + 2 * gw:], wgate_ref[...]) + bgate_ref[...] for b in bs]
    la = [(jnp.minimum(g, 0.0) - jnp.log(1.0 + jnp.exp(-jnp.abs(g)))) / GLA_TAU for g in gate]
    cum = [_mm_exact_lhs(tri_ref[...], la[b]) for b in bs]
    q_dec = [(x[b][:, :kw] * (GLA_DK ** -0.5)) * jnp.exp(cum[b]) for b in bs]
    k_inv = [x[b][:, kw:2 * kw] * jnp.exp(-cum[b]) for b in bs]
    last = [[cum[b][c * CHUNK + CHUNK - 1:(c + 1) * CHUNK, :] for c in cs] for b in bs]
    k_end = [[x[b][rows(c), kw:2 * kw] * jnp.exp(last[b][c] - cum[b][rows(c)]) for c in cs] for b in bs]
    dec = [[jnp.exp(last[b][c]) for c in cs] for b in bs]
    items = [(b, c, pr) for b in bs for c in cs for pr in range(GLA_PAIRS)]
    vp = {(b, c, pr): x[b][rows(c), 2 * kw + pr * 2 * GLA_DV:2 * kw + (pr + 1) * 2 * GLA_DV]
          for b, c, pr in items}
    att = {(b, c, pr): jnp.where(incl, _mm1(q_dec[b][rows(c), lanes(pr)],
                                            bd(k_inv[b][rows(c), lanes(pr)]), _NT), zero)
           for b, c, pr in items}
    upd = {it: _mm1(vp[it], k_end[it[0]][it[1]][:, lanes(it[2])], _TN) for it in items}
    intra = {it: _mm1(att[it], cat0([cat1([vp[it][:, :GLA_DV], zero]), cat1([zero, vp[it][:, GLA_DV:]])]))
             for it in items}
    st = {}
    for b in bs:
        for pr in range(GLA_PAIRS):
            cur = s_ref[b, pr]
            for c in cs:
                st[b, c, pr] = cur
                cur = cur * dec[b][c][:, lanes(pr)] + jnp.where(same_head, upd[b, c, pr], jnp.zeros_like(cur))
            s_ref[b, pr] = cur
    o = {(b, c, pr): intra[b, c, pr] + _mm1(q_dec[b][rows(c), lanes(pr)], st[b, c, pr], _NT)
         for b, c, pr in items}
    for b in bs:
        ob = cat0([cat1([o[b, c, pr] for pr in range(GLA_PAIRS)]) for c in cs])
        heads = [ob[:, h * GLA_DV:(h + 1) * GLA_DV] for h in range(GLA_HEADS)]
        normed = [oh * lax.rsqrt(jnp.mean(oh * oh, axis=-1, keepdims=True) + RMS_EPS) for oh in heads]
        y_ref[b] = cat1(normed) * ng_ref[...] * _silu(x[b][:, 2 * kw + gw:2 * kw + 2 * gw])


def _gla(proj, lp, tri, s0, bb, tg):
    bsz, seq, _ = proj.shape
    full = lambda a: pl.BlockSpec(a.shape, lambda i, c: (0,) * a.ndim)
    st = pl.BlockSpec((bb, GLA_PAIRS, 2 * GLA_DV, LANES), lambda i, c: (i, 0, 0, 0))
    return pl.pallas_call(
        functools.partial(_gla_kernel, bb=bb, nck=tg // CHUNK),
        grid=(bsz // bb, seq // tg),
        in_specs=[pl.BlockSpec((bb, tg, GLA_PROJ_PAD), lambda i, c: (i, c, 1)),
                  full(lp["wgate"]), full(lp["bgate"]), full(tri), full(lp["gla_g"]), st],
        out_specs=[pl.BlockSpec((bb, tg, GLA_WIDTH), lambda i, c: (i, c, 0)), st],
        out_shape=[jax.ShapeDtypeStruct((bsz, seq, GLA_WIDTH), F32),
                   jax.ShapeDtypeStruct((bsz, GLA_PAIRS, 2 * GLA_DV, LANES), F32)],
        compiler_params=_params("parallel", "arbitrary"),
        name="gla_chunk",
    )(proj, lp["wgate"], lp["bgate"], tri, lp["gla_g"], s0)


def _route(lg):
    lane = lax.broadcasted_iota(jnp.int32, lg.shape, 1)
    lane_f = lane.astype(F32)
    neg = jnp.full_like(lg, -jnp.inf)
    big = jnp.full_like(lg, float(LANES))

    def first_argmax(vals, mx):
        return jnp.min(jnp.where(vals == mx, lane_f, big), axis=-1, keepdims=True)

    is_group = lane < N_GROUPS
    lgm = jnp.where(is_group, lg, neg)
    gmax = jnp.max(lgm, axis=-1, keepdims=True)
    gi = first_argmax(lgm, gmax)
    pg_top = 1.0 / jnp.sum(jnp.where(is_group, jnp.exp(lg - gmax), 0.0), axis=-1, keepdims=True)
    expert = lane - N_GROUPS
    in_group = (expert >= 0) & (expert < N_EXPERTS) & ((expert >> 2).astype(F32) == gi)
    le = jnp.where(in_group, lg, neg)
    m1 = jnp.max(le, axis=-1, keepdims=True)
    i1 = first_argmax(le, m1)
    le2 = jnp.where(lane_f == i1, neg, le)
    m2 = jnp.max(le2, axis=-1, keepdims=True)
    i2 = first_argmax(le2, m2)
    p2 = jnp.exp(m2 - m1)
    w1 = pg_top / (1.0 + p2)
    w2 = pg_top * p2 / (1.0 + p2)
    return jnp.where(lane_f == i1, w1, 0.0) + jnp.where(lane_f == i2, w2, 0.0)


def _outproj_kernel(yr_ref, bonus_ref, g_ref, yg_ref, x_ref, gt_ref, sc_ref, sh_ref,
                    lnw_ref, lnb_ref, seg_ref, wout_ref, g2_ref, wr_ref, br_ref,
                    x1_ref, h2_ref, comb_ref):
    nb, rb, d_model = x_ref.shape
    tm = nb * rb
    flat = lambda ref: ref[...].reshape(tm, ref.shape[-1])
    seg = seg_ref[...]
    y = flat(yr_ref)
    inv_n = 1.0 / RWKV_HEAD
    d = y - _mm2_exact_rhs(y, seg) * inv_n
    var = _mm2_exact_rhs(d * d, seg) * inv_n
    yn = d * lax.rsqrt(var + LNX_EPS) * lnw_ref[...] + lnb_ref[...]
    yr = (yn + flat(bonus_ref)) * flat(g_ref)
    mix = jnp.concatenate([yr, flat(yg_ref)], axis=1)
    x1 = x_ref[...] + gt_ref[...] * _mm1(mix, wout_ref[...]).reshape(nb, rb, d_model)
    x1_ref[...] = x1
    h2 = _rms_mod(x1, g2_ref[...], sc_ref[...], sh_ref[...])
    h2_ref[...] = h2.astype(BF16)
    comb = _route(_mm3(h2.reshape(tm, d_model), wr_ref[...]) + br_ref[...])
    comb_ref[...] = comb.reshape(nb, rb, LANES)


def _out_proj(yr, bonus, g, yg, x, gt, sc, sh, lp, consts, rows):
    bsz, seq, d = x.shape
    nb, rb = _token_tile(bsz, seq, rows)
    half = pl.BlockSpec((nb, rb, RWKV_WIDTH), lambda b, i: (b, i, 0))
    tokd = pl.BlockSpec((nb, rb, d), lambda b, i: (b, i, 0))
    vec = pl.BlockSpec((nb, 1, d), lambda b, i: (b, 0, 0))
    full = lambda a: pl.BlockSpec(a.shape, lambda b, i: (0,) * a.ndim)
    args = (lp["lnx_w"], lp["lnx_b"], consts["seg"], lp["w_out"], lp["norm2_g"], lp["w_router"], lp["b_router"])
    return pl.pallas_call(
        _outproj_kernel,
        grid=(bsz // nb, seq // rb),
        in_specs=[half, half, half, half, tokd, vec, vec, vec] + [full(a) for a in args],
        out_specs=[tokd, tokd, pl.BlockSpec((nb, rb, LANES), lambda b, i: (b, i, 0))],
        out_shape=[jax.ShapeDtypeStruct((bsz, seq, d), F32),
                   jax.ShapeDtypeStruct((bsz, seq, d), BF16),
                   jax.ShapeDtypeStruct((bsz, seq, LANES), F32)],
        compiler_params=_params("parallel", "parallel"),
        name="out_proj_router",
    )(yr, bonus, g, yg, x, gt, sc, sh, *args)


def _moe_kernel(h_ref, comb_ref, x1_ref, gt_ref, sc_ref, sh_ref, gf_ref, wg_ref, wu_ref, wd_ref,
                y_ref, acc_ref):
    e = pl.program_id(2)
    nb, rb, d = h_ref.shape
    tm = nb * rb
    h = h_ref[...].reshape(tm, d)
    hid = _silu(_dg(h, wg_ref[0])) * _dg(h, wu_ref[0])
    comb = comb_ref[...].reshape(tm, LANES)
    lane = lax.broadcasted_iota(jnp.int32, comb.shape, 1)
    ce = jnp.sum(jnp.where(lane == e + N_GROUPS, comb, 0.0), axis=-1, keepdims=True)
    contrib = ce * _dg(hid.astype(BF16), wd_ref[0])

    @pl.when(e == 0)
    def _():
        acc_ref[...] = contrib

    @pl.when(e != 0)
    def _():
        acc_ref[...] += contrib

    @pl.when(e == N_EXPERTS - 1)
    def _():
        x2 = x1_ref[...] + gt_ref[...] * acc_ref[...].reshape(nb, rb, d)
        y_ref[...] = _rms_mod(x2, gf_ref[...], sc_ref[...], sh_ref[...])


def _moe(h2, comb, x1, gt, sc, sh, gf, wg, wu, wd, rows):
    bsz, seq, d = x1.shape
    nb, rb = _token_tile(bsz, seq, rows)
    tm = nb * rb
    tokd = pl.BlockSpec((nb, rb, d), lambda b, i, e: (b, i, 0))
    vec = pl.BlockSpec((nb, 1, d), lambda b, i, e: (b, 0, 0))
    return pl.pallas_call(
        _moe_kernel,
        grid=(bsz // nb, seq // rb, N_EXPERTS),
        in_specs=[tokd, pl.BlockSpec((nb, rb, LANES), lambda b, i, e: (b, i, 0)), tokd, vec, vec, vec,
                  pl.BlockSpec((1, d), lambda b, i, e: (0, 0)),
                  pl.BlockSpec((1, d, D_EXPERT), lambda b, i, e: (e, 0, 0)),
                  pl.BlockSpec((1, d, D_EXPERT), lambda b, i, e: (e, 0, 0)),
                  pl.BlockSpec((1, D_EXPERT, d), lambda b, i, e: (e, 0, 0))],
        out_specs=tokd,
        out_shape=jax.ShapeDtypeStruct((bsz, seq, d), F32),
        scratch_shapes=[pltpu.VMEM((tm, d), F32)],
        compiler_params=_params("parallel", "parallel", "arbitrary"),
        name="moe_final_norm",
    )(h2, comb, x1, gt, sc, sh, gf, wg, wu, wd)


def _block_ones(n, blk, lower):
    i = jnp.arange(n)
    m = (i[:, None] // blk) == (i[None, :] // blk)
    if lower:
        m = m & (i[None, :] <= i[:, None])
    return m.astype(BF16)


def _consts(tr, tg):
    return dict(seg=_block_ones(RWKV_WIDTH, RWKV_HEAD, False), tri=_block_ones(tr, CHUNK, True),
                tri_gla=_block_ones(tg, CHUNK, True))


def _pad_rows(w, first_row):
    out = jnp.zeros((LORA_PAD, w.shape[1]), F32)
    return lax.dynamic_update_slice(out, w, (first_row, 0)).astype(BF16)


def _layer_params(l, w_in, mu_shift, w0, w_decay_up, a0, w_a_up, w_g_up, k_k, k_a, r_k, lnx_w, lnx_b,
                  w_gla_gate_up, b_gla_gate, gla_norm_g, w_out, norm2_g,
                  w_router_group, b_router_group, w_router_expert, b_router_expert):
    wi = w_in[l]
    wg_ = wi[:, RWKV_PROJ:]
    qkv = 2 * GLA_KEY_WIDTH + GLA_WIDTH
    w_in_p = jnp.concatenate(
        [wi[:, :RWKV_PROJ], wg_[:, :qkv], wg_[:, qkv + GLA_GATE_RANK:], wg_[:, qkv:qkv + GLA_GATE_RANK],
         jnp.zeros((D_MODEL, GLA_PROJ_PAD - GLA_PROJ), F32)], axis=1).astype(BF16)
    n_route = N_GROUPS + N_EXPERTS
    w_router = jnp.concatenate([w_router_group[l], w_router_expert[l],
                                jnp.zeros((D_MODEL, LANES - n_route), F32)], axis=1)
    b_router = jnp.concatenate([b_router_group[l], b_router_expert[l],
                                jnp.zeros((LANES - n_route,), F32)]).reshape(1, LANES)
    wgate = jnp.zeros((LANES, GLA_KEY_WIDTH), F32).at[:GLA_GATE_RANK].set(w_gla_gate_up[l]).astype(BF16)
    r1 = lambda a: a.reshape(1, -1)
    return dict(
        w_in=w_in_p, mu=r1(mu_shift[l]), w0=r1(w0[l]), a0=r1(a0[l]), k_k=r1(k_k[l]), k_a=r1(k_a[l]),
        r_k=r1(r_k[l]), wd=_pad_rows(w_decay_up[l], 0), wa=_pad_rows(w_a_up[l], DECAY_LORA),
        wg=_pad_rows(w_g_up[l], DECAY_LORA + AAA_LORA), lnx_w=r1(lnx_w[l]), lnx_b=r1(lnx_b[l]),
        wgate=wgate, bgate=r1(b_gla_gate[l]), gla_g=r1(gla_norm_g[l]),
        w_out=w_out[l].astype(BF16), norm2_g=r1(norm2_g[l]), w_router=w_router, b_router=b_router)


def _run_layer(x, mod, shift0, wkv0, gla0, lp, experts, final):
    bsz, seq, d = x.shape
    assert seq % CHUNK == 0
    bb = 2
    tg = min(seq, GLA_STEP_ROWS)
    consts = _consts(TOKEN_TILE_ROWS, tg)
    m = lambda j: mod[:, j:j + 1, :]
    sh1, sc1, gt1, sh2, sc2, gt2 = (m(j) for j in range(6))
    proj, tails = _in_proj(x, sc1, sh1, lp["norm1_g"], lp["w_in"], TOKEN_TILE_ROWS)
    tails = tails.reshape(bsz, -1, RWKV_PROJ)
    new_shift = tails[:, -1]
    prev = jnp.concatenate([shift0[:, None, :], tails[:, :-1]], axis=1).reshape(-1, 1, RWKV_PROJ)
    *prep, gl = _rwkv_prep(proj, prev, lp, consts, TOKEN_TILE_ROWS)
    gl = gl.reshape(bsz, seq // CHUNK, RWKV_WIDTH)
    *chunk_in, bonus, g = prep
    s0 = jnp.zeros((bsz, RWKV_PAIRS, 2, RWKV_HEAD, 2, RWKV_HEAD), F32)
    w5 = wkv0.reshape(bsz, RWKV_PAIRS, 2, RWKV_HEAD, RWKV_HEAD)
    s0 = s0.at[:, :, 0, :, 0, :].set(w5[:, :, 0]).at[:, :, 1, :, 1, :].set(w5[:, :, 1])
    s0 = s0.reshape(bsz, RWKV_PAIRS, LANES, LANES)
    yr, s_bd = _rwkv_chunk(chunk_in, gl[:, :, None, :], s0, bb, min(seq, RWKV_STEP_ROWS) // CHUNK)
    s6 = s_bd.reshape(bsz, RWKV_PAIRS, 2, RWKV_HEAD, 2, RWKV_HEAD)
    new_wkv = jnp.stack([s6[:, :, 0, :, 0, :], s6[:, :, 1, :, 1, :]], axis=2).reshape(
        bsz, RWKV_HEADS, RWKV_HEAD, RWKV_HEAD)
    g5 = jnp.swapaxes(gla0, -1, -2).reshape(bsz, GLA_PAIRS, 2, GLA_DV, GLA_DK)
    t0 = jnp.zeros((bsz, GLA_PAIRS, 2, GLA_DV, 2, GLA_DK), F32)
    t0 = t0.at[:, :, 0, :, 0, :].set(g5[:, :, 0]).at[:, :, 1, :, 1, :].set(g5[:, :, 1])
    yg, t_bd = _gla(proj, lp, consts["tri_gla"], t0.reshape(bsz, GLA_PAIRS, 2 * GLA_DV, LANES), bb, tg)
    t6 = t_bd.reshape(bsz, GLA_PAIRS, 2, GLA_DV, 2, GLA_DK)
    new_gla = jnp.swapaxes(
        jnp.stack([t6[:, :, 0, :, 0, :], t6[:, :, 1, :, 1, :]], axis=2).reshape(
            bsz, GLA_HEADS, GLA_DV, GLA_DK), -1, -2)
    x1, h2, comb = _out_proj(yr, bonus, g, yg, x, gt1, sc2, sh2, lp, consts, TOKEN_TILE_ROWS)
    out = _moe(h2, comb, x1, gt2, *final, *experts, MOE_TILE_ROWS)
    return out, new_shift, new_wkv, new_gla


def kernel(x_prompt, x_sample, c_prompt, c_sample, state_rwkv_shift, state_rwkv_wkv, state_gla_kv, w_ada, b_ada, norm1_g, norm2_g, w_in, mu_shift, w0, w_decay_up, a0, w_a_up, w_g_up, k_k, k_a, r_k, lnx_w, lnx_b, w_gla_gate_up, b_gla_gate, gla_norm_g, w_out, w_router_group, b_router_group, w_router_expert, b_router_expert, w_expert_gate, w_expert_up, w_expert_down, w_ada_final, b_ada_final, normf_g):
    assert w_ada.shape[0] == 1, "the final norm is fused into the single layer's MoE kernel"
    bp, bs = x_prompt.shape[0], x_sample.shape[0]
    d = D_MODEL
    n_rows = -(-(bp + bs) // 8) * 8
    c_all = jnp.zeros((n_rows, d), F32).at[:bp].set(c_prompt).at[bp:bp + bs].set(c_sample)
    modf = _modulation(c_all, w_ada_final, b_ada_final, 1024).reshape(n_rows, 2, d)
    mod = _modulation(c_all, w_ada[0], b_ada[0], 1536).reshape(n_rows, 6, d)
    lp = _layer_params(0, w_in, mu_shift, w0, w_decay_up, a0, w_a_up, w_g_up, k_k, k_a, r_k, lnx_w,
                       lnx_b, w_gla_gate_up, b_gla_gate, gla_norm_g, w_out, norm2_g,
                       w_router_group, b_router_group, w_router_expert, b_router_expert)
    lp["norm1_g"] = norm1_g[0].reshape(1, d)
    experts = (w_expert_gate[0].astype(BF16), w_expert_up[0].astype(BF16), w_expert_down[0].astype(BF16))
    groups = [
        (x_prompt, 0, bp, jnp.zeros((bp, RWKV_PROJ), F32),
         jnp.zeros((bp, RWKV_HEADS, RWKV_HEAD, RWKV_HEAD), F32), jnp.zeros((bp, GLA_HEADS, GLA_DK, GLA_DV), F32)),
        (x_sample, bp, bp + bs, state_rwkv_shift[0], state_rwkv_wkv[0], state_gla_kv[0]),
    ]
    ys, states = [], []
    for x, lo, hi, shift0, wkv0, gla0 in groups:
        final = (modf[lo:hi, 1:2], modf[lo:hi, 0:1], normf_g.reshape(1, d))
        y, *st = _run_layer(x, mod[lo:hi], shift0, wkv0, gla0, lp, experts, final)
        ys.append(y)
        states.extend(s[None] for s in st)
    return tuple(ys + states)
```

```python
import functools

import jax
import jax.numpy as jnp
from jax import lax
from jax.experimental import pallas as pl
from jax.experimental.pallas import tpu as pltpu

F32 = jnp.float32
BF16 = jnp.bfloat16

LANES = 128
VMEM_LIMIT_BYTES = 56 * 1024 * 1024
TOKEN_TILE_ROWS = 512
MOE_TILE_ROWS = 1024
MOE_SUB_ROWS = 128
GLA_STEP_ROWS = 256
RWKV_STEP_ROWS = 128

D_MODEL = 1024
CHUNK = 64
RWKV_WIDTH = 512
RWKV_HEAD = 64
RWKV_HEADS = RWKV_WIDTH // RWKV_HEAD
RWKV_PAIRS = RWKV_HEADS // 2
DECAY_LORA = 32
AAA_LORA = 32
GATE_LORA = 64
LORA_PAD = DECAY_LORA + AAA_LORA + GATE_LORA
RWKV_PROJ = 3 * RWKV_WIDTH + LORA_PAD
GLA_WIDTH = 512
GLA_HEADS = 4
GLA_PAIRS = GLA_HEADS // 2
GLA_DV = GLA_WIDTH // GLA_HEADS
GLA_DK = GLA_DV // 2
GLA_KEY_WIDTH = GLA_HEADS * GLA_DK
GLA_GATE_RANK = 16
GLA_TAU = 16.0
GLA_PROJ = 2 * GLA_KEY_WIDTH + 2 * GLA_WIDTH + GLA_GATE_RANK
GLA_PROJ_PAD = RWKV_PROJ
IN_PROJ_PAD = RWKV_PROJ + GLA_PROJ_PAD
N_GROUPS = 4
EXPERTS_PER_GROUP = 4
N_EXPERTS = N_GROUPS * EXPERTS_PER_GROUP
GROUP_SHIFT = EXPERTS_PER_GROUP.bit_length() - 1
assert 1 << GROUP_SHIFT == EXPERTS_PER_GROUP
ROUTE_ROWS = 24
D_EXPERT = 512
RMS_EPS = 1e-6
LNX_EPS = 64e-5

_NN = (((1,), (0,)), ((), ()))
_NT = (((1,), (1,)), ((), ()))
_TN = (((0,), (0,)), ((), ()))


def _dg(a, b, dims=_NN):
    return lax.dot_general(a, b, dims, preferred_element_type=F32)


def _split2(x):
    hi = x.astype(BF16)
    lo = (x - hi.astype(F32)).astype(BF16)
    return hi, lo


def _split3(x):
    hi = x.astype(BF16)
    r1 = x - hi.astype(F32)
    mid = r1.astype(BF16)
    lo = (r1 - mid.astype(F32)).astype(BF16)
    return hi, mid, lo


def _mm1(a, b, dims=_NN):
    return _dg(a.astype(BF16), b.astype(BF16), dims)


def _mm3(a, b, dims=_NN):
    ah, al = _split2(a)
    bh, bl = _split2(b)
    return _dg(ah, bh, dims) + (_dg(ah, bl, dims) + _dg(al, bh, dims))


def _mm_exact_lhs(e, x, dims=_NN):
    h, m, l = _split3(x)
    return _dg(e, h, dims) + (_dg(e, m, dims) + _dg(e, l, dims))


def _mm2_exact_rhs(x, e, dims=_NN):
    h, l = _split2(x)
    return _dg(h, e, dims) + _dg(l, e, dims)


def _softplus(z):
    return jnp.maximum(z, 0.0) + jnp.log(1.0 + jnp.exp(-jnp.abs(z)))


def _sigmoid(z):
    return 1.0 / (1.0 + jnp.exp(-z))


def _silu(z):
    return z * _sigmoid(z)


def _params(*sem):
    return pltpu.CompilerParams(dimension_semantics=sem, vmem_limit_bytes=VMEM_LIMIT_BYTES)


def _mod_kernel(c_ref, w_ref, b_ref, o_ref):
    o_ref[...] = _mm1(_silu(c_ref[...]), w_ref[...]) + b_ref[...]


def _modulation(c, w, b, tn):
    rows, d = c.shape
    n = w.shape[1]
    return pl.pallas_call(
        _mod_kernel,
        grid=(n // tn,),
        in_specs=[pl.BlockSpec((rows, d), lambda j: (0, 0)),
                  pl.BlockSpec((d, tn), lambda j: (0, j)),
                  pl.BlockSpec((1, tn), lambda j: (0, j))],
        out_specs=pl.BlockSpec((rows, tn), lambda j: (0, j)),
        out_shape=jax.ShapeDtypeStruct((rows, n), F32),
        compiler_params=_params("parallel"),
        name="modulation",
    )(c, w, b.reshape(1, n))


def _rms_mod(x, g, sc, sh):
    ms = jnp.mean(x * x, axis=-1, keepdims=True)
    return (x * lax.rsqrt(ms + RMS_EPS) * g) * (1.0 + sc) + sh


def _token_tile(bsz, seq, rows):
    if seq >= rows:
        assert seq % rows == 0
        return 1, rows
    nb = min(bsz, rows // seq)
    assert bsz % nb == 0
    return nb, seq


def _inproj_kernel(x_ref, sc_ref, sh_ref, g_ref, w_ref, o_ref, last_ref, *, n_step):
    nb, rb, d = x_ref.shape
    h = _rms_mod(x_ref[...], g_ref[...], sc_ref[...], sh_ref[...])
    hb = h.reshape(nb * rb, d).astype(BF16)
    for j in range(IN_PROJ_PAD // n_step):
        cols = slice(j * n_step, (j + 1) * n_step)
        o_ref[:, :, cols] = _dg(hb, w_ref[:, cols]).reshape(nb, rb, n_step)
    last_ref[...] = o_ref[:, rb - 1:rb, :RWKV_PROJ]


def _in_proj(x, sc, sh, g, w, rows):
    bsz, seq, d = x.shape
    nb, rb = _token_tile(bsz, seq, rows)
    n_seq_tiles = seq // rb
    vec = pl.BlockSpec((nb, 1, d), lambda b, i: (b, 0, 0))
    return pl.pallas_call(
        functools.partial(_inproj_kernel, n_step=2 * LANES),
        grid=(bsz // nb, n_seq_tiles),
        in_specs=[pl.BlockSpec((nb, rb, d), lambda b, i: (b, i, 0)), vec, vec,
                  pl.BlockSpec((1, d), lambda b, i: (0, 0)),
                  pl.BlockSpec((d, IN_PROJ_PAD), lambda b, i: (0, 0))],
        out_specs=[pl.BlockSpec((nb, rb, IN_PROJ_PAD), lambda b, i: (b, i, 0)),
                   pl.BlockSpec((nb, 1, RWKV_PROJ), lambda b, i: (b * n_seq_tiles + i, 0, 0))],
        out_shape=[jax.ShapeDtypeStruct((bsz, seq, IN_PROJ_PAD), F32),
                   jax.ShapeDtypeStruct((bsz * n_seq_tiles, 1, RWKV_PROJ), F32)],
        compiler_params=_params("parallel", "parallel"),
        name="norm1_in_proj",
    )(x, sc, sh, g, w)


def _rwkv_prep_kernel(p_ref, prev_ref, mu_ref, w0_ref, a0_ref, kk_ref, ka_ref, rk_ref,
                      wd_ref, wa_ref, wg_ref, seg_ref, tri_ref,
                      at_ref, bt_ref, rt_ref, kt_ref, be_ref, ke_ref, v_ref, bonus_ref, g_ref, gl_ref):
    nb, rb, wp = p_ref.shape
    tr = nb * rb
    p = p_ref[...].reshape(tr, wp)
    row = lax.broadcasted_iota(jnp.int32, (nb, rb, wp), 1)
    xx = jnp.where(row == 0, prev_ref[...], pltpu.roll(p, 1, 0).reshape(nb, rb, wp)).reshape(tr, wp)
    ps = p + (xx - p) * mu_ref[...]
    w = RWKV_WIDTH
    r, k, v, lora = ps[:, :w], ps[:, w:2 * w], ps[:, 2 * w:3 * w], ps[:, 3 * w:]
    logw = -_softplus(-(w0_ref[...] + _mm1(jnp.tanh(lora), wd_ref[...]))) - 0.5
    lw = -jnp.exp(logw)
    a = _sigmoid(a0_ref[...] + _mm1(lora, wa_ref[...]))
    g = _mm1(_sigmoid(lora), wg_ref[...])
    seg = seg_ref[...]
    kk = k * kk_ref[...]
    kk = kk / jnp.maximum(jnp.sqrt(_mm2_exact_rhs(kk * kk, seg)), 1e-12)
    k2 = k * (1.0 + (a - 1.0) * ka_ref[...])
    cum = _mm_exact_lhs(tri_ref[...], lw)
    lasts = [cum[c * CHUNK + CHUNK - 1:(c + 1) * CHUNK, :] for c in range(tr // CHUNK)]
    tot = jnp.concatenate([jnp.broadcast_to(l, (CHUNK, w)) for l in lasts], axis=0)
    for c, l in enumerate(lasts):
        gl_ref[c:c + 1, :] = jnp.exp(l)
    kka = kk * a
    ginv = jnp.exp(-cum)
    gend = jnp.exp(tot - cum)

    def put(ref, val):
        ref[...] = val.reshape(nb, rb, RWKV_WIDTH).astype(ref.dtype)

    put(at_ref, -kk * jnp.exp(cum - lw))
    put(bt_ref, kka * ginv)
    put(rt_ref, r * jnp.exp(cum))
    put(kt_ref, k2 * ginv)
    put(be_ref, kka * gend)
    put(ke_ref, k2 * gend)
    put(v_ref, v)
    put(bonus_ref, _mm2_exact_rhs(r * k2 * rk_ref[...], seg) * v)
    put(g_ref, g)


def _rwkv_prep(proj, prev, lp, consts, rows):
    bsz, seq, _ = proj.shape
    w = RWKV_WIDTH
    nb, rb = _token_tile(bsz, seq, rows)
    n_seq_tiles = seq // rb
    flat = lambda b, i: (b * n_seq_tiles + i, 0)
    row = lambda n: pl.BlockSpec((1, n), lambda b, i: (0, 0))
    full = lambda a: pl.BlockSpec(a.shape, lambda b, i: (0,) * a.ndim)
    tok = pl.BlockSpec((nb, rb, w), lambda b, i: (b, i, 0))
    out_tok = lambda dt: jax.ShapeDtypeStruct((bsz, seq, w), dt)
    dtypes = [F32, F32, BF16, BF16, BF16, BF16, BF16, F32, F32]
    return pl.pallas_call(
        _rwkv_prep_kernel,
        grid=(bsz // nb, n_seq_tiles),
        in_specs=[pl.BlockSpec((nb, rb, RWKV_PROJ), lambda b, i: (b, i, 0)),
                  pl.BlockSpec((nb, 1, RWKV_PROJ), lambda b, i: flat(b, i) + (0,)),
                  row(RWKV_PROJ), row(w), row(w), row(w), row(w), row(w),
                  full(lp["wd"]), full(lp["wa"]), full(lp["wg"]), full(consts["seg"]), full(consts["tri"])],
        out_specs=[tok] * 9 + [pl.BlockSpec((nb * rb // CHUNK, w), flat)],
        out_shape=[out_tok(dt) for dt in dtypes] + [jax.ShapeDtypeStruct((bsz * seq // CHUNK, w), F32)],
        compiler_params=_params("parallel", "parallel"),
        name="rwkv_prep",
    )(proj, prev, lp["mu"], lp["w0"], lp["a0"], lp["k_k"], lp["k_a"], lp["r_k"],
      lp["wd"], lp["wa"], lp["wg"], consts["seg"], consts["tri"])


def _pair_masks():
    lane = lax.broadcasted_iota(jnp.int32, (CHUNK, LANES), 1)
    row = lax.broadcasted_iota(jnp.int32, (CHUNK, LANES), 0)
    first = lane < RWKV_HEAD
    col = jnp.where(first, lane, lane - RWKV_HEAD)
    return first, row, col


def _block_diag(x, first):
    z = jnp.zeros_like(x)
    return jnp.concatenate([jnp.where(first, x, z), jnp.where(first, z, x)], axis=0)


def _rwkv_chunk_kernel(at_ref, bt_ref, rt_ref, kt_ref, be_ref, ke_ref, v_ref, gl_ref, s0_ref,
                       y_ref, s_ref, *, bb, nck):
    @pl.when(pl.program_id(1) == 0)
    def _():
        s_ref[...] = s0_ref[...]

    first, row, col = _pair_masks()
    strict = col < row
    incl = col <= row
    same8 = (col >> 3) == (row >> 3)
    lane2 = lax.broadcasted_iota(jnp.int32, (LANES, LANES), 1)
    row2 = lax.broadcasted_iota(jnp.int32, (LANES, LANES), 0)
    same_head = (lane2 < RWKV_HEAD) == (row2 < RWKV_HEAD)
    bd = functools.partial(_block_diag, first=first)
    c = CHUNK

    def pmm(p, q):
        return _dg(p.astype(BF16), bd(q.astype(BF16)))

    items = [(b, ck, pr) for b in range(bb) for ck in range(nck) for pr in range(RWKV_PAIRS)]
    n = range(len(items))
    rows = lambda ck: slice(ck * c, (ck + 1) * c)
    lanes = lambda pr: slice(pr * LANES, (pr + 1) * LANES)
    ld = lambda ref: [ref[b, rows(ck), lanes(pr)] for b, ck, pr in items]
    cat0 = lambda *xs: jnp.concatenate(xs, axis=0)
    cat1 = lambda *xs: jnp.concatenate(xs, axis=1)
    at, bt = ld(at_ref), ld(bt_ref)
    rt, kt, be, ke, v = (ld(r) for r in (rt_ref, kt_ref, be_ref, ke_ref, v_ref))
    atb = [a.astype(BF16) for a in at]
    zero = jnp.zeros((c, LANES), F32)
    gk = [_dg(cat0(atb[i], rt[i]), bd(kt[i]), _NT) for i in n]
    aak = [jnp.where(strict, gk[i][:c], zero) for i in n]
    ark = [jnp.where(incl, gk[i][c:], zero) for i in n]
    arb = [jnp.where(incl, _dg(rt[i], bd(bt[i].astype(BF16)), _NT), zero) for i in n]
    aab = [jnp.where(strict, _mm3(at[i], bd(bt[i]), _NT), zero) for i in n]
    z = [pmm(aak[i], v[i]) for i in n]
    a8 = [jnp.where(same8, aab[i], zero) for i in n]
    p2 = [pmm(a8[i], a8[i]) for i in n]
    p4 = [pmm(p2[i], p2[i]) for i in n]
    nn = [a8[i] + p2[i] + pmm(p2[i], a8[i]) for i in n]
    nn = [nn[i] + p4[i] + pmm(p4[i], nn[i]) for i in n]
    for lvl in (3, 4, 5):
        joins = ((col >> (lvl + 1)) == (row >> (lvl + 1))) & ((col >> lvl) != (row >> lvl))
        e = [jnp.where(joins, aab[i], zero) for i in n]
        te = [e[i] + pmm(nn[i], e[i]) for i in n]
        nn = [nn[i] + te[i] + pmm(te[i], nn[i]) for i in n]
    wu = [cat1(at[i], z[i]) + _dg(nn[i].astype(BF16), cat1(bd(atb[i]), bd(z[i].astype(BF16)))) for i in n]
    abk = [cat1(arb[i], ark[i]).astype(BF16) for i in n]
    s = {(b, pr): s_ref[b, pr] for b in range(bb) for pr in range(RWKV_PAIRS)}
    for ck in range(nck):
        cur = [i for i in n if items[i][1] == ck]
        key = lambda i: (items[i][0], items[i][2])
        x = {i: _dg(cat0(wu[i][:, :LANES].astype(BF16), rt[i]), s[key(i)].astype(BF16), _NT) for i in cur}
        ub = {i: (x[i][:c] + wu[i][:, LANES:]).astype(BF16) for i in cur}
        upd = {i: _dg(cat0(ub[i], v[i]), cat0(be[i], ke[i]), _TN) for i in cur}
        for i in cur:
            b, _, pr = items[i]
            y_ref[b, rows(ck), lanes(pr)] = x[i][c:] + _dg(abk[i], cat0(bd(ub[i]), bd(v[i])))
            s[b, pr] = (s[b, pr] * gl_ref[b, ck, :, lanes(pr)]
                        + jnp.where(same_head, upd[i], jnp.zeros_like(upd[i])))
    for (b, pr), val in s.items():
        s_ref[b, pr] = val


def _rwkv_chunk(prep, gl, s0, bb, nck):
    bsz, seq, w = prep[0].shape
    tok = pl.BlockSpec((bb, nck * CHUNK, w), lambda i, c: (i, c, 0))
    st = pl.BlockSpec((bb, RWKV_PAIRS, LANES, LANES), lambda i, c: (i, 0, 0, 0))
    return pl.pallas_call(
        functools.partial(_rwkv_chunk_kernel, bb=bb, nck=nck),
        grid=(bsz // bb, seq // (nck * CHUNK)),
        in_specs=[tok] * 7 + [pl.BlockSpec((bb, nck, 1, w), lambda i, c: (i, c, 0, 0)), st],
        out_specs=[tok, st],
        out_shape=[jax.ShapeDtypeStruct((bsz, seq, w), F32),
                   jax.ShapeDtypeStruct((bsz, RWKV_PAIRS, LANES, LANES), F32)],
        compiler_params=_params("parallel", "arbitrary"),
        name="rwkv_chunk",
    )(*prep, gl, s0)


def _gla_kernel(p_ref, wgate_ref, bgate_ref, tri_ref, ng_ref, s0_ref, y_ref, s_ref, *, bb, nck):
    @pl.when(pl.program_id(1) == 0)
    def _():
        s_ref[...] = s0_ref[...]

    first, row, col = _pair_masks()
    incl = col <= row
    bd = functools.partial(_block_diag, first=first)
    lane2 = lax.broadcasted_iota(jnp.int32, (2 * GLA_DV, LANES), 1)
    row2 = lax.broadcasted_iota(jnp.int32, (2 * GLA_DV, LANES), 0)
    same_head = (lane2 < GLA_DK) == (row2 < GLA_DV)
    kw, gw = GLA_KEY_WIDTH, GLA_WIDTH
    zero = jnp.zeros((CHUNK, LANES), F32)
    cat0 = lambda xs: jnp.concatenate(xs, axis=0)
    cat1 = lambda xs: jnp.concatenate(xs, axis=1)
    rows = lambda c: slice(c * CHUNK, (c + 1) * CHUNK)
    lanes = lambda pr: slice(pr * LANES, (pr + 1) * LANES)
    bs, cs = range(bb), range(nck)
    x = [p_ref[b] for b in bs]
    gate = [_mm1(x[b][:, 2 * kw + 2 * gw:], wgate_ref[...]) + bgate_ref[...] for b in bs]
    la = [(jnp.minimum(g, 0.0) - jnp.log(1.0 + jnp.exp(-jnp.abs(g)))) / GLA_TAU for g in gate]
    cum = [_mm_exact_lhs(tri_ref[...], la[b]) for b in bs]
    q_dec = [(x[b][:, :kw] * (GLA_DK ** -0.5)) * jnp.exp(cum[b]) for b in bs]
    k_inv = [x[b][:, kw:2 * kw] * jnp.exp(-cum[b]) for b in bs]
    last = [[cum[b][c * CHUNK + CHUNK - 1:(c + 1) * CHUNK, :] for c in cs] for b in bs]
    k_end = [[x[b][rows(c), kw:2 * kw] * jnp.exp(last[b][c] - cum[b][rows(c)]) for c in cs] for b in bs]
    dec = [[jnp.exp(last[b][c]) for c in cs] for b in bs]
    items = [(b, c, pr) for b in bs for c in cs for pr in range(GLA_PAIRS)]
    vp = {(b, c, pr): x[b][rows(c), 2 * kw + pr * 2 * GLA_DV:2 * kw + (pr + 1) * 2 * GLA_DV]
          for b, c, pr in items}
    att = {(b, c, pr): jnp.where(incl, _mm1(q_dec[b][rows(c), lanes(pr)],
                                            bd(k_inv[b][rows(c), lanes(pr)]), _NT), zero)
           for b, c, pr in items}
    upd = {it: _mm1(vp[it], k_end[it[0]][it[1]][:, lanes(it[2])], _TN) for it in items}
    intra = {it: _mm1(att[it], cat0([cat1([vp[it][:, :GLA_DV], zero]), cat1([zero, vp[it][:, GLA_DV:]])]))
             for it in items}
    st = {}
    for b in bs:
        for pr in range(GLA_PAIRS):
            cur = s_ref[b, pr]
            for c in cs:
                st[b, c, pr] = cur
                cur = cur * dec[b][c][:, lanes(pr)] + jnp.where(same_head, upd[b, c, pr], jnp.zeros_like(cur))
            s_ref[b, pr] = cur
    o = {(b, c, pr): intra[b, c, pr] + _mm1(q_dec[b][rows(c), lanes(pr)], st[b, c, pr], _NT)
         for b, c, pr in items}
    for b in bs:
        ob = cat0([cat1([o[b, c, pr] for pr in range(GLA_PAIRS)]) for c in cs])
        heads = [ob[:, h * GLA_DV:(h + 1) * GLA_DV] for h in range(GLA_HEADS)]
        normed = [oh * lax.rsqrt(jnp.mean(oh * oh, axis=-1, keepdims=True) + RMS_EPS) for oh in heads]
        y_ref[b] = cat1(normed) * ng_ref[...] * _silu(x[b][:, 2 * kw + gw:2 * kw + 2 * gw])


def _gla(proj, lp, tri, s0, bb, tg):
    bsz, seq, _ = proj.shape
    full = lambda a: pl.BlockSpec(a.shape, lambda i, c: (0,) * a.ndim)
    st = pl.BlockSpec((bb, GLA_PAIRS, 2 * GLA_DV, LANES), lambda i, c: (i, 0, 0, 0))
    return pl.pallas_call(
        functools.partial(_gla_kernel, bb=bb, nck=tg // CHUNK),
        grid=(bsz // bb, seq // tg),
        in_specs=[pl.BlockSpec((bb, tg, GLA_PROJ_PAD), lambda i, c: (i, c, 1)),
                  full(lp["wgate"]), full(lp["bgate"]), full(tri), full(lp["gla_g"]), st],
        out_specs=[pl.BlockSpec((bb, tg, GLA_WIDTH), lambda i, c: (i, c, 0)), st],
        out_shape=[jax.ShapeDtypeStruct((bsz, seq, GLA_WIDTH), F32),
                   jax.ShapeDtypeStruct((bsz, GLA_PAIRS, 2 * GLA_DV, LANES), F32)],
        compiler_params=_params("parallel", "arbitrary"),
        name="gla_chunk",
    )(proj, lp["wgate"], lp["bgate"], tri, lp["gla_g"], s0)


def _route(lgt):
    tm = lgt.shape[1]
    n_pad = ROUTE_ROWS - N_EXPERTS
    rowg = lax.broadcasted_iota(jnp.int32, (n_pad, tm), 0)
    rowe = lax.broadcasted_iota(jnp.int32, (N_EXPERTS, tm), 0)
    rowg_f, rowe_f = rowg.astype(F32), rowe.astype(F32)

    def first_argmax(vals, mx, rows_f):
        return jnp.min(jnp.where(vals == mx, rows_f, float(ROUTE_ROWS)), axis=0, keepdims=True)

    is_group = rowg < N_GROUPS
    lg = jnp.where(is_group, lgt[N_EXPERTS:], -jnp.inf)
    gmax = jnp.max(lg, axis=0, keepdims=True)
    gi = first_argmax(lg, gmax, rowg_f)
    pg_top = 1.0 / jnp.sum(jnp.where(is_group, jnp.exp(lg - gmax), 0.0), axis=0, keepdims=True)
    in_group = (rowe >> GROUP_SHIFT).astype(F32) == gi
    le = jnp.where(in_group, lgt[:N_EXPERTS], -jnp.inf)
    m1 = jnp.max(le, axis=0, keepdims=True)
    i1 = first_argmax(le, m1, rowe_f)
    le2 = jnp.where(rowe_f == i1, -jnp.inf, le)
    m2 = jnp.max(le2, axis=0, keepdims=True)
    i2 = first_argmax(le2, m2, rowe_f)
    p2 = jnp.exp(m2 - m1)
    w1 = pg_top / (1.0 + p2)
    w2 = pg_top * p2 / (1.0 + p2)
    comb = jnp.where(rowe_f == i1, w1, 0.0) + jnp.where(rowe_f == i2, w2, 0.0)
    return jnp.concatenate([comb, jnp.where(rowg == 0, gi, 0.0)], axis=0)


def _outproj_kernel(yr_ref, bonus_ref, g_ref, yg_ref, x_ref, gt_ref, sc_ref, sh_ref,
                    lnw_ref, lnb_ref, seg_ref, wout_ref, g2_ref, wr_ref, br_ref,
                    x1_ref, h2_ref, route_ref):
    nb, rb, d_model = x_ref.shape
    tm = nb * rb
    flat = lambda ref: ref[...].reshape(tm, ref.shape[-1])
    seg = seg_ref[...]
    y = flat(yr_ref)
    inv_n = 1.0 / RWKV_HEAD
    d = y - _mm2_exact_rhs(y, seg) * inv_n
    var = _mm2_exact_rhs(d * d, seg) * inv_n
    yn = d * lax.rsqrt(var + LNX_EPS) * lnw_ref[...] + lnb_ref[...]
    yr = (yn + flat(bonus_ref)) * flat(g_ref)
    mix = jnp.concatenate([yr, flat(yg_ref)], axis=1)
    x1 = x_ref[...] + gt_ref[...] * _mm1(mix, wout_ref[...]).reshape(nb, rb, d_model)
    x1_ref[...] = x1
    h2 = _rms_mod(x1, g2_ref[...], sc_ref[...], sh_ref[...])
    h2_ref[...] = h2.astype(BF16)
    route_ref[...] = _route(_mm3(wr_ref[...], h2.reshape(tm, d_model), _NT) + br_ref[...])


def _out_proj(yr, bonus, g, yg, x, gt, sc, sh, lp, consts, rows):
    bsz, seq, d = x.shape
    nb, rb = _token_tile(bsz, seq, rows)
    n_seq_tiles = seq // rb
    half = pl.BlockSpec((nb, rb, RWKV_WIDTH), lambda b, i: (b, i, 0))
    tokd = pl.BlockSpec((nb, rb, d), lambda b, i: (b, i, 0))
    vec = pl.BlockSpec((nb, 1, d), lambda b, i: (b, 0, 0))
    full = lambda a: pl.BlockSpec(a.shape, lambda b, i: (0,) * a.ndim)
    args = (lp["lnx_w"], lp["lnx_b"], consts["seg"], lp["w_out"], lp["norm2_g"], lp["w_router"], lp["b_router"])
    return pl.pallas_call(
        _outproj_kernel,
        grid=(bsz // nb, seq // rb),
        in_specs=[half, half, half, half, tokd, vec, vec, vec] + [full(a) for a in args],
        out_specs=[tokd, tokd, pl.BlockSpec((ROUTE_ROWS, nb * rb), lambda b, i: (0, b * n_seq_tiles + i))],
        out_shape=[jax.ShapeDtypeStruct((bsz, seq, d), F32),
                   jax.ShapeDtypeStruct((bsz, seq, d), BF16),
                   jax.ShapeDtypeStruct((ROUTE_ROWS, bsz * seq), F32)],
        compiler_params=_params("parallel", "parallel"),
        name="out_proj_router",
    )(yr, bonus, g, yg, x, gt, sc, sh, *args)


def _moe_kernel(h_ref, route_ref, x1_ref, gt_ref, sc_ref, sh_ref, gf_ref, tri_ref, wg_ref, wu_ref, wd_ref,
                y_ref, hs_ref, cs_ref, ys_ref, pos_ref, meta_ref):
    g = pl.program_id(2)
    nb, rb, d = h_ref.shape
    tm = nb * rb
    sub = MOE_SUB_ROWS
    cap = hs_ref.shape[0]
    n_lane_tiles = tm // LANES
    lane_tile = lambda k: slice(k * LANES, (k + 1) * LANES)

    @pl.when(g == 0)
    def _sort():
        row8 = lax.broadcasted_iota(jnp.int32, (8, LANES), 0).astype(F32)
        carry = jnp.zeros((8, 1), F32)
        members, ranks = [], []
        for k in range(n_lane_tiles):
            blk = jnp.where(row8 == route_ref[N_EXPERTS:N_EXPERTS + 1, lane_tile(k)], 1.0, 0.0)
            members.append(blk)
            ranks.append(_dg(blk.astype(BF16), tri_ref[...]) - blk + carry)
            carry = carry + jnp.sum(blk, axis=1, keepdims=True)
        rowc = lax.broadcasted_iota(jnp.int32, (8, 1), 0)
        lane = lax.broadcasted_iota(jnp.int32, (1, LANES), 1)
        first = jnp.zeros((1, 1), F32)
        off_col = jnp.zeros((8, 1), F32)
        meta = jnp.zeros((1, LANES), F32)
        for grp in range(N_GROUPS):
            count = jnp.sum(jnp.where(rowc == grp, carry, 0.0), axis=0, keepdims=True)
            n_sub = jnp.ceil(count * (1.0 / sub))
            off_col = off_col + jnp.where(rowc == grp, first, 0.0)
            meta = meta + jnp.where(lane == grp, first, 0.0) + jnp.where(lane == N_GROUPS + grp, n_sub, 0.0)
            first = first + n_sub * sub
        meta = (meta + jnp.where(lane == 2 * N_GROUPS, first * (1.0 / sub), 0.0)).astype(jnp.int32)
        for i in range(2 * N_GROUPS + 1):
            meta_ref[i] = meta[0, i]
        for k in range(n_lane_tiles):
            pos = jnp.sum(members[k] * (ranks[k] + off_col), axis=0, keepdims=True)
            pos_ref[:, lane_tile(k)] = pos.astype(jnp.int32)
        ys_ref[...] = jnp.zeros_like(ys_ref)
        h = h_ref[...].reshape(tm, d)
        comb3 = _split3(route_ref[:N_EXPERTS, :])

        def gather(j, carry_):
            base = pl.multiple_of(j * sub, sub)
            prow = lax.broadcasted_iota(jnp.int32, (sub, tm), 0) + base
            onehot = jnp.where(prow == pos_ref[...], 1.0, 0.0).astype(BF16)
            hs_ref[pl.ds(base, sub), :] = _dg(onehot, h).astype(BF16)
            cs_ref[pl.ds(base, sub), :] = (_dg(onehot, comb3[0], _NT)
                                           + (_dg(onehot, comb3[1], _NT) + _dg(onehot, comb3[2], _NT)))
            return carry_

        lax.fori_loop(0, meta_ref[2 * N_GROUPS], gather, 0)

    first_row = meta_ref[g]

    def experts(j, carry_):
        base = pl.multiple_of(first_row + j * sub, sub)
        hs = hs_ref[pl.ds(base, sub), :]
        cs = cs_ref[pl.ds(base, sub), :]
        lane = lax.broadcasted_iota(jnp.int32, cs.shape, 1)
        ys = jnp.zeros((sub, d), F32)
        for e in range(EXPERTS_PER_GROUP):
            hid = _silu(_dg(hs, wg_ref[e])) * _dg(hs, wu_ref[e])
            ce = jnp.sum(jnp.where(lane == g * EXPERTS_PER_GROUP + e, cs, 0.0), axis=-1, keepdims=True)
            ys = ys + ce * _dg(hid.astype(BF16), wd_ref[e])
        ys_ref[pl.ds(base, sub), :] = ys.astype(BF16)
        return carry_

    lax.fori_loop(0, meta_ref[N_GROUPS + g], experts, 0)

    @pl.when(g == N_GROUPS - 1)
    def _unsort():
        ys_all = ys_ref[...]
        prow = lax.broadcasted_iota(jnp.int32, (cap, LANES), 0)
        seqs = max(LANES // rb, 1)
        for k in range(n_lane_tiles):
            onehot = jnp.where(prow == pos_ref[:, lane_tile(k)], 1.0, 0.0).astype(BF16)
            moe = _dg(onehot, ys_all, _TN).reshape(seqs, LANES // seqs, d)
            b0 = k * LANES // rb
            r0 = k * LANES - b0 * rb
            bs, rs = slice(b0, b0 + seqs), slice(r0, r0 + LANES // seqs)
            x2 = x1_ref[bs, rs] + gt_ref[bs] * moe
            y_ref[bs, rs] = _rms_mod(x2, gf_ref[...], sc_ref[bs], sh_ref[bs])


def _moe(h2, route, x1, gt, sc, sh, gf, wg, wu, wd, tri, rows):
    bsz, seq, d = x1.shape
    nb, rb = _token_tile(bsz, seq, rows)
    tm = nb * rb
    assert tm % LANES == 0 and (rb % LANES == 0 or LANES % rb == 0)
    n_seq_tiles = seq // rb
    cap = tm + N_GROUPS * MOE_SUB_ROWS
    tokd = pl.BlockSpec((nb, rb, d), lambda b, i, g: (b, i, 0))
    vec = pl.BlockSpec((nb, 1, d), lambda b, i, g: (b, 0, 0))
    group_w = lambda shape: pl.BlockSpec((EXPERTS_PER_GROUP,) + shape, lambda b, i, g: (g, 0, 0))
    return pl.pallas_call(
        _moe_kernel,
        grid=(bsz // nb, n_seq_tiles, N_GROUPS),
        in_specs=[tokd, pl.BlockSpec((ROUTE_ROWS, tm), lambda b, i, g: (0, b * n_seq_tiles + i)),
                  pl.BlockSpec((nb, rb, d), lambda b, i, g: (b, i, 0), pipeline_mode=pl.Buffered(1)),
                  vec, vec, vec,
                  pl.BlockSpec((1, d), lambda b, i, g: (0, 0)),
                  pl.BlockSpec(tri.shape, lambda b, i, g: (0, 0)),
                  group_w((d, D_EXPERT)), group_w((d, D_EXPERT)), group_w((D_EXPERT, d))],
        out_specs=tokd,
        out_shape=jax.ShapeDtypeStruct((bsz, seq, d), F32),
        scratch_shapes=[pltpu.VMEM((cap, d), BF16), pltpu.VMEM((cap, N_EXPERTS), F32),
                        pltpu.VMEM((cap, d), BF16), pltpu.VMEM((1, tm), jnp.int32),
                        pltpu.SMEM((2 * N_GROUPS + 1,), jnp.int32)],
        compiler_params=_params("parallel", "parallel", "arbitrary"),
        name="moe_final_norm",
    )(h2, route, x1, gt, sc, sh, gf, tri, wg, wu, wd)


def _block_ones(n, blk, lower):
    i = jnp.arange(n)
    m = (i[:, None] // blk) == (i[None, :] // blk)
    if lower:
        m = m & (i[None, :] <= i[:, None])
    return m.astype(BF16)


def _consts(tr, tg):
    return dict(seg=_block_ones(RWKV_WIDTH, RWKV_HEAD, False), tri=_block_ones(tr, CHUNK, True),
                tri_gla=_block_ones(tg, CHUNK, True), tri_up=_block_ones(LANES, LANES, True).T)


def _pad_rows(w, first_row):
    out = jnp.zeros((LORA_PAD, w.shape[1]), F32)
    return lax.dynamic_update_slice(out, w, (first_row, 0)).astype(BF16)


def _layer_params(l, w_in, mu_shift, w0, w_decay_up, a0, w_a_up, w_g_up, k_k, k_a, r_k, lnx_w, lnx_b,
                  w_gla_gate_up, b_gla_gate, gla_norm_g, w_out, norm2_g,
                  w_router_group, b_router_group, w_router_expert, b_router_expert):
    wi = w_in[l]
    wg_ = wi[:, RWKV_PROJ:]
    qkv = 2 * GLA_KEY_WIDTH + GLA_WIDTH
    w_in_p = jnp.concatenate(
        [wi[:, :RWKV_PROJ], wg_[:, :qkv], wg_[:, qkv + GLA_GATE_RANK:], wg_[:, qkv:qkv + GLA_GATE_RANK],
         jnp.zeros((D_MODEL, GLA_PROJ_PAD - GLA_PROJ), F32)], axis=1).astype(BF16)
    n_pad = ROUTE_ROWS - N_EXPERTS - N_GROUPS
    w_router = jnp.concatenate([w_router_expert[l].T, w_router_group[l].T, jnp.zeros((n_pad, D_MODEL), F32)])
    b_router = jnp.concatenate([b_router_expert[l], b_router_group[l],
                                jnp.zeros((n_pad,), F32)]).reshape(ROUTE_ROWS, 1)
    wgate = jnp.zeros((LANES, GLA_KEY_WIDTH), F32).at[:GLA_GATE_RANK].set(w_gla_gate_up[l]).astype(BF16)
    r1 = lambda a: a.reshape(1, -1)
    return dict(
        w_in=w_in_p, mu=r1(mu_shift[l]), w0=r1(w0[l]), a0=r1(a0[l]), k_k=r1(k_k[l]), k_a=r1(k_a[l]),
        r_k=r1(r_k[l]), wd=_pad_rows(w_decay_up[l], 0), wa=_pad_rows(w_a_up[l], DECAY_LORA),
        wg=_pad_rows(w_g_up[l], DECAY_LORA + AAA_LORA), lnx_w=r1(lnx_w[l]), lnx_b=r1(lnx_b[l]),
        wgate=wgate, bgate=r1(b_gla_gate[l]), gla_g=r1(gla_norm_g[l]),
        w_out=w_out[l].astype(BF16), norm2_g=r1(norm2_g[l]), w_router=w_router, b_router=b_router)


def _run_layer(x, mod, shift0, wkv0, gla0, lp, experts, final):
    bsz, seq, d = x.shape
    assert seq % CHUNK == 0
    bb = 2
    tg = min(seq, GLA_STEP_ROWS)
    consts = _consts(TOKEN_TILE_ROWS, tg)
    m = lambda j: mod[:, j:j + 1, :]
    sh1, sc1, gt1, sh2, sc2, gt2 = (m(j) for j in range(6))
    proj, tails = _in_proj(x, sc1, sh1, lp["norm1_g"], lp["w_in"], TOKEN_TILE_ROWS)
    tails = tails.reshape(bsz, -1, RWKV_PROJ)
    new_shift = tails[:, -1]
    prev = jnp.concatenate([shift0[:, None, :], tails[:, :-1]], axis=1).reshape(-1, 1, RWKV_PROJ)
    *prep, gl = _rwkv_prep(proj, prev, lp, consts, TOKEN_TILE_ROWS)
    gl = gl.reshape(bsz, seq // CHUNK, RWKV_WIDTH)
    *chunk_in, bonus, g = prep
    s0 = jnp.zeros((bsz, RWKV_PAIRS, 2, RWKV_HEAD, 2, RWKV_HEAD), F32)
    w5 = wkv0.reshape(bsz, RWKV_PAIRS, 2, RWKV_HEAD, RWKV_HEAD)
    s0 = s0.at[:, :, 0, :, 0, :].set(w5[:, :, 0]).at[:, :, 1, :, 1, :].set(w5[:, :, 1])
    s0 = s0.reshape(bsz, RWKV_PAIRS, LANES, LANES)
    yr, s_bd = _rwkv_chunk(chunk_in, gl[:, :, None, :], s0, bb, min(seq, RWKV_STEP_ROWS) // CHUNK)
    s6 = s_bd.reshape(bsz, RWKV_PAIRS, 2, RWKV_HEAD, 2, RWKV_HEAD)
    new_wkv = jnp.stack([s6[:, :, 0, :, 0, :], s6[:, :, 1, :, 1, :]], axis=2).reshape(
        bsz, RWKV_HEADS, RWKV_HEAD, RWKV_HEAD)
    g5 = jnp.swapaxes(gla0, -1, -2).reshape(bsz, GLA_PAIRS, 2, GLA_DV, GLA_DK)
    t0 = jnp.zeros((bsz, GLA_PAIRS, 2, GLA_DV, 2, GLA_DK), F32)
    t0 = t0.at[:, :, 0, :, 0, :].set(g5[:, :, 0]).at[:, :, 1, :, 1, :].set(g5[:, :, 1])
    yg, t_bd = _gla(proj, lp, consts["tri_gla"], t0.reshape(bsz, GLA_PAIRS, 2 * GLA_DV, LANES), bb, tg)
    t6 = t_bd.reshape(bsz, GLA_PAIRS, 2, GLA_DV, 2, GLA_DK)
    new_gla = jnp.swapaxes(
        jnp.stack([t6[:, :, 0, :, 0, :], t6[:, :, 1, :, 1, :]], axis=2).reshape(
            bsz, GLA_HEADS, GLA_DV, GLA_DK), -1, -2)
    x1, h2, route = _out_proj(yr, bonus, g, yg, x, gt1, sc2, sh2, lp, consts, TOKEN_TILE_ROWS)
    out = _moe(h2, route, x1, gt2, *final, *experts, consts["tri_up"], MOE_TILE_ROWS)
    return out, new_shift, new_wkv, new_gla


def kernel(x_prompt, x_sample, c_prompt, c_sample, state_rwkv_shift, state_rwkv_wkv, state_gla_kv, w_ada, b_ada, norm1_g, norm2_g, w_in, mu_shift, w0, w_decay_up, a0, w_a_up, w_g_up, k_k, k_a, r_k, lnx_w, lnx_b, w_gla_gate_up, b_gla_gate, gla_norm_g, w_out, w_router_group, b_router_group, w_router_expert, b_router_expert, w_expert_gate, w_expert_up, w_expert_down, w_ada_final, b_ada_final, normf_g):
    assert w_ada.shape[0] == 1, "the final norm is fused into the single layer's MoE kernel"
    bp, bs = x_prompt.shape[0], x_sample.shape[0]
    d = D_MODEL
    n_rows = -(-(bp + bs) // 8) * 8
    c_all = jnp.zeros((n_rows, d), F32).at[:bp].set(c_prompt).at[bp:bp + bs].set(c_sample)
    modf = _modulation(c_all, w_ada_final, b_ada_final, 1024).reshape(n_rows, 2, d)
    mod = _modulation(c_all, w_ada[0], b_ada[0], 1536).reshape(n_rows, 6, d)
    lp = _layer_params(0, w_in, mu_shift, w0, w_decay_up, a0, w_a_up, w_g_up, k_k, k_a, r_k, lnx_w,
                       lnx_b, w_gla_gate_up, b_gla_gate, gla_norm_g, w_out, norm2_g,
                       w_router_group, b_router_group, w_router_expert, b_router_expert)
    lp["norm1_g"] = norm1_g[0].reshape(1, d)
    experts = (w_expert_gate[0].astype(BF16), w_expert_up[0].astype(BF16), w_expert_down[0].astype(BF16))
    groups = [
        (x_prompt, 0, bp, jnp.zeros((bp, RWKV_PROJ), F32),
         jnp.zeros((bp, RWKV_HEADS, RWKV_HEAD, RWKV_HEAD), F32), jnp.zeros((bp, GLA_HEADS, GLA_DK, GLA_DV), F32)),
        (x_sample, bp, bp + bs, state_rwkv_shift[0], state_rwkv_wkv[0], state_gla_kv[0]),
    ]
    ys, states = [], []
    for x, lo, hi, shift0, wkv0, gla0 in groups:
        final = (modf[lo:hi, 1:2], modf[lo:hi, 0:1], normf_g.reshape(1, d))
        y, *st = _run_layer(x, mod[lo:hi], shift0, wkv0, gla0, lp, experts, final)
        ys.append(y)
        states.extend(s[None] for s in st)
    return tuple(ys + states)
```

```python
import functools

import jax
import jax.numpy as jnp
from jax import lax
from jax.experimental import pallas as pl
from jax.experimental.pallas import tpu as pltpu

F32 = jnp.float32
BF16 = jnp.bfloat16

LANES = 128
VMEM_LIMIT_BYTES = 56 * 1024 * 1024
TOKEN_TILE_ROWS = 512
MOE_TILE_ROWS = 1024
MOE_SUB_ROWS = 128
MOE_GATHER_ROWS = 256
GLA_STEP_ROWS = 256
RWKV_STEP_ROWS = 128

D_MODEL = 1024
CHUNK = 64
RWKV_WIDTH = 512
RWKV_HEAD = 64
RWKV_HEADS = RWKV_WIDTH // RWKV_HEAD
RWKV_PAIRS = RWKV_HEADS // 2
DECAY_LORA = 32
AAA_LORA = 32
GATE_LORA = 64
LORA_PAD = DECAY_LORA + AAA_LORA + GATE_LORA
RWKV_PROJ = 3 * RWKV_WIDTH + LORA_PAD
GLA_WIDTH = 512
GLA_HEADS = 4
GLA_PAIRS = GLA_HEADS // 2
GLA_DV = GLA_WIDTH // GLA_HEADS
GLA_DK = GLA_DV // 2
GLA_KEY_WIDTH = GLA_HEADS * GLA_DK
GLA_GATE_RANK = 16
GLA_TAU = 16.0
GLA_PROJ = 2 * GLA_KEY_WIDTH + 2 * GLA_WIDTH + GLA_GATE_RANK
GLA_PROJ_PAD = RWKV_PROJ
IN_PROJ_PAD = RWKV_PROJ + GLA_PROJ_PAD
N_GROUPS = 4
EXPERTS_PER_GROUP = 4
N_EXPERTS = N_GROUPS * EXPERTS_PER_GROUP
GROUP_SHIFT = EXPERTS_PER_GROUP.bit_length() - 1
assert 1 << GROUP_SHIFT == EXPERTS_PER_GROUP
ROUTE_ROWS = 24
D_EXPERT = 512
RMS_EPS = 1e-6
LNX_EPS = 64e-5

_NN = (((1,), (0,)), ((), ()))
_NT = (((1,), (1,)), ((), ()))
_TN = (((0,), (0,)), ((), ()))


def _dg(a, b, dims=_NN):
    return lax.dot_general(a, b, dims, preferred_element_type=F32)


def _split2(x):
    hi = x.astype(BF16)
    lo = (x - hi.astype(F32)).astype(BF16)
    return hi, lo


def _split3(x):
    hi = x.astype(BF16)
    r1 = x - hi.astype(F32)
    mid = r1.astype(BF16)
    lo = (r1 - mid.astype(F32)).astype(BF16)
    return hi, mid, lo


def _mm1(a, b, dims=_NN):
    return _dg(a.astype(BF16), b.astype(BF16), dims)


def _mm3(a, b, dims=_NN):
    ah, al = _split2(a)
    bh, bl = _split2(b)
    return _dg(ah, bh, dims) + (_dg(ah, bl, dims) + _dg(al, bh, dims))


def _mm_exact_lhs(e, x, dims=_NN):
    h, m, l = _split3(x)
    return _dg(e, h, dims) + (_dg(e, m, dims) + _dg(e, l, dims))


def _mm2_exact_rhs(x, e, dims=_NN):
    h, l = _split2(x)
    return _dg(h, e, dims) + _dg(l, e, dims)


def _softplus(z):
    return jnp.maximum(z, 0.0) + jnp.log(1.0 + jnp.exp(-jnp.abs(z)))


def _sigmoid(z):
    return 1.0 / (1.0 + jnp.exp(-z))


def _silu(z):
    return z * _sigmoid(z)


def _params(*sem):
    return pltpu.CompilerParams(dimension_semantics=sem, vmem_limit_bytes=VMEM_LIMIT_BYTES)


def _mod_kernel(c_ref, w_ref, b_ref, o_ref):
    o_ref[...] = _mm1(_silu(c_ref[...]), w_ref[...]) + b_ref[...]


def _modulation(c, w, b, tn):
    rows, d = c.shape
    n = w.shape[1]
    return pl.pallas_call(
        _mod_kernel,
        grid=(n // tn,),
        in_specs=[pl.BlockSpec((rows, d), lambda j: (0, 0)),
                  pl.BlockSpec((d, tn), lambda j: (0, j)),
                  pl.BlockSpec((1, tn), lambda j: (0, j))],
        out_specs=pl.BlockSpec((rows, tn), lambda j: (0, j)),
        out_shape=jax.ShapeDtypeStruct((rows, n), F32),
        compiler_params=_params("parallel"),
        name="modulation",
    )(c, w, b.reshape(1, n))


def _win_layout_kernel(w_ref, o_ref):
    qkv_end = RWKV_PROJ + 2 * GLA_KEY_WIDTH + GLA_WIDTH
    gate_end = qkv_end + GLA_GATE_RANK
    o_ref[:, :qkv_end] = w_ref[:, :qkv_end].astype(BF16)
    o_ref[:, qkv_end:qkv_end + GLA_WIDTH] = w_ref[:, gate_end:gate_end + GLA_WIDTH].astype(BF16)
    pad = jnp.zeros((w_ref.shape[0], IN_PROJ_PAD - RWKV_PROJ - GLA_PROJ), F32)
    o_ref[:, qkv_end + GLA_WIDTH:] = jnp.concatenate([w_ref[:, qkv_end:gate_end], pad], axis=1).astype(BF16)


def _win_layout(w, rows):
    d, n = w.shape
    return pl.pallas_call(
        _win_layout_kernel,
        grid=(d // rows,),
        in_specs=[pl.BlockSpec((rows, n), lambda i: (i, 0))],
        out_specs=pl.BlockSpec((rows, IN_PROJ_PAD), lambda i: (i, 0)),
        out_shape=jax.ShapeDtypeStruct((d, IN_PROJ_PAD), BF16),
        compiler_params=_params("parallel"),
        name="w_in_layout",
    )(w)


def _rms_mod(x, g, sc, sh):
    ms = jnp.mean(x * x, axis=-1, keepdims=True)
    return (x * lax.rsqrt(ms + RMS_EPS) * g) * (1.0 + sc) + sh


def _token_tile(bsz, seq, rows):
    if seq >= rows:
        assert seq % rows == 0
        return 1, rows
    nb = min(bsz, rows // seq)
    assert bsz % nb == 0
    return nb, seq


def _inproj_kernel(x_ref, sc_ref, sh_ref, g_ref, w_ref, o_ref, last_ref, *, n_step):
    nb, rb, d = x_ref.shape
    h = _rms_mod(x_ref[...], g_ref[...], sc_ref[...], sh_ref[...])
    hb = h.reshape(nb * rb, d).astype(BF16)
    for j in range(IN_PROJ_PAD // n_step):
        cols = slice(j * n_step, (j + 1) * n_step)
        o_ref[:, :, cols] = _dg(hb, w_ref[:, cols]).reshape(nb, rb, n_step)
    last_ref[...] = o_ref[:, rb - 1:rb, :RWKV_PROJ]


def _in_proj(x, sc, sh, g, w, rows):
    bsz, seq, d = x.shape
    nb, rb = _token_tile(bsz, seq, rows)
    n_seq_tiles = seq // rb
    vec = pl.BlockSpec((nb, 1, d), lambda b, i: (b, 0, 0))
    return pl.pallas_call(
        functools.partial(_inproj_kernel, n_step=2 * LANES),
        grid=(bsz // nb, n_seq_tiles),
        in_specs=[pl.BlockSpec((nb, rb, d), lambda b, i: (b, i, 0)), vec, vec,
                  pl.BlockSpec((1, d), lambda b, i: (0, 0)),
                  pl.BlockSpec((d, IN_PROJ_PAD), lambda b, i: (0, 0))],
        out_specs=[pl.BlockSpec((nb, rb, IN_PROJ_PAD), lambda b, i: (b, i, 0)),
                   pl.BlockSpec((nb, 1, RWKV_PROJ), lambda b, i: (b * n_seq_tiles + i, 0, 0))],
        out_shape=[jax.ShapeDtypeStruct((bsz, seq, IN_PROJ_PAD), F32),
                   jax.ShapeDtypeStruct((bsz * n_seq_tiles, 1, RWKV_PROJ), F32)],
        compiler_params=_params("parallel", "parallel"),
        name="norm1_in_proj",
    )(x, sc, sh, g, w)


def _rwkv_prep_kernel(p_ref, prev_ref, mu_ref, w0_ref, a0_ref, kk_ref, ka_ref, rk_ref,
                      wd_ref, wa_ref, wg_ref, seg_ref, tri_ref,
                      at_ref, bt_ref, rt_ref, kt_ref, be_ref, ke_ref, v_ref, bonus_ref, g_ref, gl_ref):
    nb, rb, wp = p_ref.shape
    tr = nb * rb
    p = p_ref[...].reshape(tr, wp)
    row = lax.broadcasted_iota(jnp.int32, (nb, rb, wp), 1)
    xx = jnp.where(row == 0, prev_ref[...], pltpu.roll(p, 1, 0).reshape(nb, rb, wp)).reshape(tr, wp)
    ps = p + (xx - p) * mu_ref[...]
    w = RWKV_WIDTH
    r, k, v, lora = ps[:, :w], ps[:, w:2 * w], ps[:, 2 * w:3 * w], ps[:, 3 * w:]
    logw = -_softplus(-(w0_ref[...] + _mm1(jnp.tanh(lora), wd_ref[...]))) - 0.5
    lw = -jnp.exp(logw)
    a = _sigmoid(a0_ref[...] + _mm1(lora, wa_ref[...]))
    g = _mm1(_sigmoid(lora), wg_ref[...])
    seg = seg_ref[...]
    kk = k * kk_ref[...]
    kk = kk / jnp.maximum(jnp.sqrt(_mm2_exact_rhs(kk * kk, seg)), 1e-12)
    k2 = k * (1.0 + (a - 1.0) * ka_ref[...])
    cum = _mm_exact_lhs(tri_ref[...], lw)
    lasts = [cum[c * CHUNK + CHUNK - 1:(c + 1) * CHUNK, :] for c in range(tr // CHUNK)]
    tot = jnp.concatenate([jnp.broadcast_to(l, (CHUNK, w)) for l in lasts], axis=0)
    for c, l in enumerate(lasts):
        gl_ref[c:c + 1, :] = jnp.exp(l)
    kka = kk * a
    ginv = jnp.exp(-cum)
    gend = jnp.exp(tot - cum)

    def put(ref, val):
        ref[...] = val.reshape(nb, rb, RWKV_WIDTH).astype(ref.dtype)

    put(at_ref, -kk * jnp.exp(cum - lw))
    put(bt_ref, kka * ginv)
    put(rt_ref, r * jnp.exp(cum))
    put(kt_ref, k2 * ginv)
    put(be_ref, kka * gend)
    put(ke_ref, k2 * gend)
    put(v_ref, v)
    put(bonus_ref, _mm2_exact_rhs(r * k2 * rk_ref[...], seg) * v)
    put(g_ref, g)


def _rwkv_prep(proj, prev, lp, consts, rows):
    bsz, seq, _ = proj.shape
    w = RWKV_WIDTH
    nb, rb = _token_tile(bsz, seq, rows)
    n_seq_tiles = seq // rb
    flat = lambda b, i: (b * n_seq_tiles + i, 0)
    row = lambda n: pl.BlockSpec((1, n), lambda b, i: (0, 0))
    full = lambda a: pl.BlockSpec(a.shape, lambda b, i: (0,) * a.ndim)
    tok = pl.BlockSpec((nb, rb, w), lambda b, i: (b, i, 0))
    out_tok = lambda dt: jax.ShapeDtypeStruct((bsz, seq, w), dt)
    dtypes = [F32, F32, BF16, BF16, BF16, BF16, BF16, F32, F32]
    return pl.pallas_call(
        _rwkv_prep_kernel,
        grid=(bsz // nb, n_seq_tiles),
        in_specs=[pl.BlockSpec((nb, rb, RWKV_PROJ), lambda b, i: (b, i, 0)),
                  pl.BlockSpec((nb, 1, RWKV_PROJ), lambda b, i: flat(b, i) + (0,)),
                  row(RWKV_PROJ), row(w), row(w), row(w), row(w), row(w),
                  full(lp["wd"]), full(lp["wa"]), full(lp["wg"]), full(consts["seg"]), full(consts["tri"])],
        out_specs=[tok] * 9 + [pl.BlockSpec((nb * rb // CHUNK, w), flat)],
        out_shape=[out_tok(dt) for dt in dtypes] + [jax.ShapeDtypeStruct((bsz * seq // CHUNK, w), F32)],
        compiler_params=_params("parallel", "parallel"),
        name="rwkv_prep",
    )(proj, prev, lp["mu"], lp["w0"], lp["a0"], lp["k_k"], lp["k_a"], lp["r_k"],
      lp["wd"], lp["wa"], lp["wg"], consts["seg"], consts["tri"])


def _pair_masks():
    lane = lax.broadcasted_iota(jnp.int32, (CHUNK, LANES), 1)
    row = lax.broadcasted_iota(jnp.int32, (CHUNK, LANES), 0)
    first = lane < RWKV_HEAD
    col = jnp.where(first, lane, lane - RWKV_HEAD)
    return first, row, col


def _block_diag(x, first):
    z = jnp.zeros_like(x)
    return jnp.concatenate([jnp.where(first, x, z), jnp.where(first, z, x)], axis=0)


def _rwkv_chunk_kernel(at_ref, bt_ref, rt_ref, kt_ref, be_ref, ke_ref, v_ref, gl_ref, s0_ref,
                       y_ref, s_ref, *, bb, nck):
    @pl.when(pl.program_id(1) == 0)
    def _():
        s_ref[...] = s0_ref[...]

    first, row, col = _pair_masks()
    strict = col < row
    incl = col <= row
    same8 = (col >> 3) == (row >> 3)
    lane2 = lax.broadcasted_iota(jnp.int32, (LANES, LANES), 1)
    row2 = lax.broadcasted_iota(jnp.int32, (LANES, LANES), 0)
    same_head = (lane2 < RWKV_HEAD) == (row2 < RWKV_HEAD)
    bd = functools.partial(_block_diag, first=first)
    c = CHUNK

    def pmm(p, q):
        return _dg(p.astype(BF16), bd(q.astype(BF16)))

    items = [(b, ck, pr) for b in range(bb) for ck in range(nck) for pr in range(RWKV_PAIRS)]
    n = range(len(items))
    rows = lambda ck: slice(ck * c, (ck + 1) * c)
    lanes = lambda pr: slice(pr * LANES, (pr + 1) * LANES)
    ld = lambda ref: [ref[b, rows(ck), lanes(pr)] for b, ck, pr in items]
    cat0 = lambda *xs: jnp.concatenate(xs, axis=0)
    cat1 = lambda *xs: jnp.concatenate(xs, axis=1)
    at, bt = ld(at_ref), ld(bt_ref)
    rt, kt, be, ke, v = (ld(r) for r in (rt_ref, kt_ref, be_ref, ke_ref, v_ref))
    atb = [a.astype(BF16) for a in at]
    zero = jnp.zeros((c, LANES), F32)
    gk = [_dg(cat0(atb[i], rt[i]), bd(kt[i]), _NT) for i in n]
    aak = [jnp.where(strict, gk[i][:c], zero) for i in n]
    ark = [jnp.where(incl, gk[i][c:], zero) for i in n]
    arb = [jnp.where(incl, _dg(rt[i], bd(bt[i].astype(BF16)), _NT), zero) for i in n]
    aab = [jnp.where(strict, _mm3(at[i], bd(bt[i]), _NT), zero) for i in n]
    z = [pmm(aak[i], v[i]) for i in n]
    a8 = [jnp.where(same8, aab[i], zero) for i in n]
    p2 = [pmm(a8[i], a8[i]) for i in n]
    p4 = [pmm(p2[i], p2[i]) for i in n]
    nn = [a8[i] + p2[i] + pmm(p2[i], a8[i]) for i in n]
    nn = [nn[i] + p4[i] + pmm(p4[i], nn[i]) for i in n]
    for lvl in (3, 4, 5):
        joins = ((col >> (lvl + 1)) == (row >> (lvl + 1))) & ((col >> lvl) != (row >> lvl))
        e = [jnp.where(joins, aab[i], zero) for i in n]
        te = [e[i] + pmm(nn[i], e[i]) for i in n]
        nn = [nn[i] + te[i] + pmm(te[i], nn[i]) for i in n]
    wu = [cat1(at[i], z[i]) + _dg(nn[i].astype(BF16), cat1(bd(atb[i]), bd(z[i].astype(BF16)))) for i in n]
    abk = [cat1(arb[i], ark[i]).astype(BF16) for i in n]
    s = {(b, pr): s_ref[b, pr] for b in range(bb) for pr in range(RWKV_PAIRS)}
    for ck in range(nck):
        cur = [i for i in n if items[i][1] == ck]
        key = lambda i: (items[i][0], items[i][2])
        x = {i: _dg(cat0(wu[i][:, :LANES].astype(BF16), rt[i]), s[key(i)].astype(BF16), _NT) for i in cur}
        ub = {i: (x[i][:c] + wu[i][:, LANES:]).astype(BF16) for i in cur}
        upd = {i: _dg(cat0(ub[i], v[i]), cat0(be[i], ke[i]), _TN) for i in cur}
        for i in cur:
            b, _, pr = items[i]
            y_ref[b, rows(ck), lanes(pr)] = x[i][c:] + _dg(abk[i], cat0(bd(ub[i]), bd(v[i])))
            s[b, pr] = (s[b, pr] * gl_ref[b, ck, :, lanes(pr)]
                        + jnp.where(same_head, upd[i], jnp.zeros_like(upd[i])))
    for (b, pr), val in s.items():
        s_ref[b, pr] = val


def _rwkv_chunk(prep, gl, s0, bb, nck):
    bsz, seq, w = prep[0].shape
    tok = pl.BlockSpec((bb, nck * CHUNK, w), lambda i, c: (i, c, 0))
    st = pl.BlockSpec((bb, RWKV_PAIRS, LANES, LANES), lambda i, c: (i, 0, 0, 0))
    return pl.pallas_call(
        functools.partial(_rwkv_chunk_kernel, bb=bb, nck=nck),
        grid=(bsz // bb, seq // (nck * CHUNK)),
        in_specs=[tok] * 7 + [pl.BlockSpec((bb, nck, 1, w), lambda i, c: (i, c, 0, 0)), st],
        out_specs=[tok, st],
        out_shape=[jax.ShapeDtypeStruct((bsz, seq, w), F32),
                   jax.ShapeDtypeStruct((bsz, RWKV_PAIRS, LANES, LANES), F32)],
        compiler_params=_params("parallel", "arbitrary"),
        name="rwkv_chunk",
    )(*prep, gl, s0)


def _gla_kernel(p_ref, wgate_ref, bgate_ref, tri_ref, ng_ref, s0_ref, y_ref, s_ref, *, bb, nck):
    @pl.when(pl.program_id(1) == 0)
    def _():
        s_ref[...] = s0_ref[...]

    first, row, col = _pair_masks()
    incl = col <= row
    bd = functools.partial(_block_diag, first=first)
    lane2 = lax.broadcasted_iota(jnp.int32, (2 * GLA_DV, LANES), 1)
    row2 = lax.broadcasted_iota(jnp.int32, (2 * GLA_DV, LANES), 0)
    same_head = (lane2 < GLA_DK) == (row2 < GLA_DV)
    kw, gw = GLA_KEY_WIDTH, GLA_WIDTH
    zero = jnp.zeros((CHUNK, LANES), F32)
    cat0 = lambda xs: jnp.concatenate(xs, axis=0)
    cat1 = lambda xs: jnp.concatenate(xs, axis=1)
    rows = lambda c: slice(c * CHUNK, (c + 1) * CHUNK)
    lanes = lambda pr: slice(pr * LANES, (pr + 1) * LANES)
    bs, cs = range(bb), range(nck)
    x = [p_ref[b] for b in bs]
    gate = [_mm1(x[b][:, 2 * kw + 2 * gw:], wgate_ref[...]) + bgate_ref[...] for b in bs]
    la = [(jnp.minimum(g, 0.0) - jnp.log(1.0 + jnp.exp(-jnp.abs(g)))) / GLA_TAU for g in gate]
    cum = [_mm_exact_lhs(tri_ref[...], la[b]) for b in bs]
    q_dec = [(x[b][:, :kw] * (GLA_DK ** -0.5)) * jnp.exp(cum[b]) for b in bs]
    k_inv = [x[b][:, kw:2 * kw] * jnp.exp(-cum[b]) for b in bs]
    last = [[cum[b][c * CHUNK + CHUNK - 1:(c + 1) * CHUNK, :] for c in cs] for b in bs]
    k_end = [[x[b][rows(c), kw:2 * kw] * jnp.exp(last[b][c] - cum[b][rows(c)]) for c in cs] for b in bs]
    dec = [[jnp.exp(last[b][c]) for c in cs] for b in bs]
    items = [(b, c, pr) for b in bs for c in cs for pr in range(GLA_PAIRS)]
    vp = {(b, c, pr): x[b][rows(c), 2 * kw + pr * 2 * GLA_DV:2 * kw + (pr + 1) * 2 * GLA_DV]
          for b, c, pr in items}
    att = {(b, c, pr): jnp.where(incl, _mm1(q_dec[b][rows(c), lanes(pr)],
                                            bd(k_inv[b][rows(c), lanes(pr)]), _NT), zero)
           for b, c, pr in items}
    upd = {it: _mm1(vp[it], k_end[it[0]][it[1]][:, lanes(it[2])], _TN) for it in items}
    intra = {it: _mm1(att[it], cat0([cat1([vp[it][:, :GLA_DV], zero]), cat1([zero, vp[it][:, GLA_DV:]])]))
             for it in items}
    st = {}
    for b in bs:
        for pr in range(GLA_PAIRS):
            cur = s_ref[b, pr]
            for c in cs:
                st[b, c, pr] = cur
                cur = cur * dec[b][c][:, lanes(pr)] + jnp.where(same_head, upd[b, c, pr], jnp.zeros_like(cur))
            s_ref[b, pr] = cur
    o = {(b, c, pr): intra[b, c, pr] + _mm1(q_dec[b][rows(c), lanes(pr)], st[b, c, pr], _NT)
         for b, c, pr in items}
    for b in bs:
        ob = cat0([cat1([o[b, c, pr] for pr in range(GLA_PAIRS)]) for c in cs])
        heads = [ob[:, h * GLA_DV:(h + 1) * GLA_DV] for h in range(GLA_HEADS)]
        normed = [oh * lax.rsqrt(jnp.mean(oh * oh, axis=-1, keepdims=True) + RMS_EPS) for oh in heads]
        y_ref[b] = cat1(normed) * ng_ref[...] * _silu(x[b][:, 2 * kw + gw:2 * kw + 2 * gw])


def _gla(proj, lp, tri, s0, bb, tg):
    bsz, seq, _ = proj.shape
    full = lambda a: pl.BlockSpec(a.shape, lambda i, c: (0,) * a.ndim)
    st = pl.BlockSpec((bb, GLA_PAIRS, 2 * GLA_DV, LANES), lambda i, c: (i, 0, 0, 0))
    return pl.pallas_call(
        functools.partial(_gla_kernel, bb=bb, nck=tg // CHUNK),
        grid=(bsz // bb, seq // tg),
        in_specs=[pl.BlockSpec((bb, tg, GLA_PROJ_PAD), lambda i, c: (i, c, 1)),
                  full(lp["wgate"]), full(lp["bgate"]), full(tri), full(lp["gla_g"]), st],
        out_specs=[pl.BlockSpec((bb, tg, GLA_WIDTH), lambda i, c: (i, c, 0)), st],
        out_shape=[jax.ShapeDtypeStruct((bsz, seq, GLA_WIDTH), F32),
                   jax.ShapeDtypeStruct((bsz, GLA_PAIRS, 2 * GLA_DV, LANES), F32)],
        compiler_params=_params("parallel", "arbitrary"),
        name="gla_chunk",
    )(proj, lp["wgate"], lp["bgate"], tri, lp["gla_g"], s0)


def _route(lgt):
    tm = lgt.shape[1]
    n_pad = ROUTE_ROWS - N_EXPERTS
    rowg = lax.broadcasted_iota(jnp.int32, (n_pad, tm), 0)
    rowe = lax.broadcasted_iota(jnp.int32, (N_EXPERTS, tm), 0)
    rowg_f, rowe_f = rowg.astype(F32), rowe.astype(F32)

    def first_argmax(vals, mx, rows_f):
        return jnp.min(jnp.where(vals == mx, rows_f, float(ROUTE_ROWS)), axis=0, keepdims=True)

    is_group = rowg < N_GROUPS
    lg = jnp.where(is_group, lgt[N_EXPERTS:], -jnp.inf)
    gmax = jnp.max(lg, axis=0, keepdims=True)
    gi = first_argmax(lg, gmax, rowg_f)
    pg_top = 1.0 / jnp.sum(jnp.where(is_group, jnp.exp(lg - gmax), 0.0), axis=0, keepdims=True)
    in_group = (rowe >> GROUP_SHIFT).astype(F32) == gi
    le = jnp.where(in_group, lgt[:N_EXPERTS], -jnp.inf)
    m1 = jnp.max(le, axis=0, keepdims=True)
    i1 = first_argmax(le, m1, rowe_f)
    le2 = jnp.where(rowe_f == i1, -jnp.inf, le)
    m2 = jnp.max(le2, axis=0, keepdims=True)
    i2 = first_argmax(le2, m2, rowe_f)
    p2 = jnp.exp(m2 - m1)
    w1 = pg_top / (1.0 + p2)
    w2 = pg_top * p2 / (1.0 + p2)
    comb = jnp.where(rowe_f == i1, w1, 0.0) + jnp.where(rowe_f == i2, w2, 0.0)
    return jnp.concatenate([comb, jnp.where(rowg == 0, gi, 0.0)], axis=0)


def _outproj_kernel(yr_ref, bonus_ref, g_ref, yg_ref, x_ref, gt_ref, sc_ref, sh_ref,
                    lnw_ref, lnb_ref, seg_ref, wout_ref, g2_ref, wr_ref, br_ref,
                    x1_ref, h2_ref, route_ref):
    nb, rb, d_model = x_ref.shape
    tm = nb * rb
    flat = lambda ref: ref[...].reshape(tm, ref.shape[-1])
    seg = seg_ref[...]
    y = flat(yr_ref)
    inv_n = 1.0 / RWKV_HEAD
    d = y - _mm2_exact_rhs(y, seg) * inv_n
    var = _mm2_exact_rhs(d * d, seg) * inv_n
    yn = d * lax.rsqrt(var + LNX_EPS) * lnw_ref[...] + lnb_ref[...]
    yr = (yn + flat(bonus_ref)) * flat(g_ref)
    mix = jnp.concatenate([yr, flat(yg_ref)], axis=1)
    x1 = x_ref[...] + gt_ref[...] * _mm1(mix, wout_ref[...]).reshape(nb, rb, d_model)
    x1_ref[...] = x1
    h2 = _rms_mod(x1, g2_ref[...], sc_ref[...], sh_ref[...])
    h2_ref[...] = h2.astype(BF16)
    route_ref[...] = _route(_mm3(wr_ref[...], h2.reshape(tm, d_model), _NT) + br_ref[...])


def _out_proj(yr, bonus, g, yg, x, gt, sc, sh, lp, consts, rows):
    bsz, seq, d = x.shape
    nb, rb = _token_tile(bsz, seq, rows)
    n_seq_tiles = seq // rb
    half = pl.BlockSpec((nb, rb, RWKV_WIDTH), lambda b, i: (b, i, 0))
    tokd = pl.BlockSpec((nb, rb, d), lambda b, i: (b, i, 0))
    vec = pl.BlockSpec((nb, 1, d), lambda b, i: (b, 0, 0))
    full = lambda a: pl.BlockSpec(a.shape, lambda b, i: (0,) * a.ndim)
    args = (lp["lnx_w"], lp["lnx_b"], consts["seg"], lp["w_out"], lp["norm2_g"], lp["w_router"], lp["b_router"])
    return pl.pallas_call(
        _outproj_kernel,
        grid=(bsz // nb, seq // rb),
        in_specs=[half, half, half, half, tokd, vec, vec, vec] + [full(a) for a in args],
        out_specs=[tokd, tokd, pl.BlockSpec((ROUTE_ROWS, nb * rb), lambda b, i: (0, b * n_seq_tiles + i))],
        out_shape=[jax.ShapeDtypeStruct((bsz, seq, d), F32),
                   jax.ShapeDtypeStruct((bsz, seq, d), BF16),
                   jax.ShapeDtypeStruct((ROUTE_ROWS, bsz * seq), F32)],
        compiler_params=_params("parallel", "parallel"),
        name="out_proj_router",
    )(yr, bonus, g, yg, x, gt, sc, sh, *args)


def _moe_kernel(h_ref, route_ref, x1_ref, gt_ref, sc_ref, sh_ref, gf_ref, tri_ref, wg_ref, wu_ref, wd_ref,
                y_ref, hs_ref, cs_ref, ys_ref, pos_ref, meta_ref):
    g = pl.program_id(2)
    nb, rb, d = h_ref.shape
    tm = nb * rb
    sub, tail = MOE_SUB_ROWS, MOE_SUB_ROWS // 2
    cap = hs_ref.shape[0]
    n_lane_tiles = tm // LANES
    lane_tile = lambda k: slice(k * LANES, (k + 1) * LANES)

    @pl.when(g == 0)
    def _sort():
        row8 = lax.broadcasted_iota(jnp.int32, (8, LANES), 0).astype(F32)
        carry = jnp.zeros((8, 1), F32)
        members, ranks = [], []
        for k in range(n_lane_tiles):
            blk = jnp.where(row8 == route_ref[N_EXPERTS:N_EXPERTS + 1, lane_tile(k)], 1.0, 0.0)
            members.append(blk)
            ranks.append(_dg(blk.astype(BF16), tri_ref[...]) - blk + carry)
            carry = carry + jnp.sum(blk, axis=1, keepdims=True)
        rowc = lax.broadcasted_iota(jnp.int32, (8, 1), 0)
        lane = lax.broadcasted_iota(jnp.int32, (1, LANES), 1)
        first = jnp.zeros((1, 1), F32)
        off_col = jnp.zeros((8, 1), F32)
        meta = jnp.zeros((1, LANES), F32)
        for grp in range(N_GROUPS):
            count = jnp.sum(jnp.where(rowc == grp, carry, 0.0), axis=0, keepdims=True)
            full = jnp.floor(count * (1.0 / sub))
            rest = count - full * sub
            has_tail = jnp.where((rest > 0.0) & (rest <= float(tail)), 1.0, 0.0)
            n_full = full + jnp.where(rest > float(tail), 1.0, 0.0)
            off_col = off_col + jnp.where(rowc == grp, first, 0.0)
            meta = (meta + jnp.where(lane == grp, first, 0.0) + jnp.where(lane == N_GROUPS + grp, n_full, 0.0)
                    + jnp.where(lane == 2 * N_GROUPS + grp, has_tail, 0.0))
            first = first + n_full * sub + has_tail * tail
        meta = meta.astype(jnp.int32)
        for i in range(3 * N_GROUPS):
            meta_ref[i] = meta[0, i]
        for k in range(n_lane_tiles):
            pos = jnp.sum(members[k] * (ranks[k] + off_col), axis=0, keepdims=True)
            pos_ref[:, lane_tile(k)] = pos.astype(jnp.int32)
        ys_ref[...] = jnp.zeros_like(ys_ref)
        h = h_ref[...].reshape(tm, d)
        comb3 = _split3(route_ref[:N_EXPERTS, :])
        for j in range(cap // MOE_GATHER_ROWS):
            rows = slice(j * MOE_GATHER_ROWS, (j + 1) * MOE_GATHER_ROWS)
            prow = lax.broadcasted_iota(jnp.int32, (MOE_GATHER_ROWS, tm), 0) + j * MOE_GATHER_ROWS
            onehot = jnp.where(prow == pos_ref[...], 1.0, 0.0).astype(BF16)
            hs_ref[rows, :] = _dg(onehot, h).astype(BF16)
            cs_ref[rows, :] = (_dg(onehot, comb3[0], _NT)
                               + (_dg(onehot, comb3[1], _NT) + _dg(onehot, comb3[2], _NT)))

    def experts(base, n_rows):
        hs = hs_ref[pl.ds(base, n_rows), :]
        cs = cs_ref[pl.ds(base, n_rows), :]
        lane = lax.broadcasted_iota(jnp.int32, cs.shape, 1)
        ys = jnp.zeros((n_rows, d), F32)
        for e in range(EXPERTS_PER_GROUP):
            hid = _silu(_dg(hs, wg_ref[e])) * _dg(hs, wu_ref[e])
            ce = jnp.sum(jnp.where(lane == g * EXPERTS_PER_GROUP + e, cs, 0.0), axis=-1, keepdims=True)
            ys = ys + ce * _dg(hid.astype(BF16), wd_ref[e])
        ys_ref[pl.ds(base, n_rows), :] = ys.astype(BF16)

    first_row = meta_ref[g]
    n_full = meta_ref[N_GROUPS + g]

    def full_sub_tile(j, carry_):
        experts(pl.multiple_of(first_row + j * sub, tail), sub)
        return carry_

    lax.fori_loop(0, n_full, full_sub_tile, 0)

    @pl.when(meta_ref[2 * N_GROUPS + g] == 1)
    def _tail():
        experts(pl.multiple_of(first_row + n_full * sub, tail), tail)

    @pl.when(g == N_GROUPS - 1)
    def _unsort():
        ys_all = ys_ref[...]
        prow = lax.broadcasted_iota(jnp.int32, (cap, LANES), 0)
        seqs = max(LANES // rb, 1)
        for k in range(n_lane_tiles):
            onehot = jnp.where(prow == pos_ref[:, lane_tile(k)], 1.0, 0.0).astype(BF16)
            moe = _dg(onehot, ys_all, _TN).reshape(seqs, LANES // seqs, d)
            b0 = k * LANES // rb
            r0 = k * LANES - b0 * rb
            bs, rs = slice(b0, b0 + seqs), slice(r0, r0 + LANES // seqs)
            x2 = x1_ref[bs, rs] + gt_ref[bs] * moe
            y_ref[bs, rs] = _rms_mod(x2, gf_ref[...], sc_ref[bs], sh_ref[bs])


def _moe(h2, route, x1, gt, sc, sh, gf, wg, wu, wd, tri, rows):
    bsz, seq, d = x1.shape
    nb, rb = _token_tile(bsz, seq, rows)
    tm = nb * rb
    assert tm % LANES == 0 and (rb % LANES == 0 or LANES % rb == 0)
    n_seq_tiles = seq // rb
    cap = tm + N_GROUPS * (MOE_SUB_ROWS // 2)
    assert cap % MOE_GATHER_ROWS == 0
    tokd = pl.BlockSpec((nb, rb, d), lambda b, i, g: (b, i, 0))
    vec = pl.BlockSpec((nb, 1, d), lambda b, i, g: (b, 0, 0))
    group_w = lambda shape: pl.BlockSpec((EXPERTS_PER_GROUP,) + shape, lambda b, i, g: (g, 0, 0))
    return pl.pallas_call(
        _moe_kernel,
        grid=(bsz // nb, n_seq_tiles, N_GROUPS),
        in_specs=[tokd, pl.BlockSpec((ROUTE_ROWS, tm), lambda b, i, g: (0, b * n_seq_tiles + i)),
                  pl.BlockSpec((nb, rb, d), lambda b, i, g: (b, i, 0), pipeline_mode=pl.Buffered(1)),
                  vec, vec, vec,
                  pl.BlockSpec((1, d), lambda b, i, g: (0, 0)),
                  pl.BlockSpec(tri.shape, lambda b, i, g: (0, 0)),
                  group_w((d, D_EXPERT)), group_w((d, D_EXPERT)), group_w((D_EXPERT, d))],
        out_specs=tokd,
        out_shape=jax.ShapeDtypeStruct((bsz, seq, d), F32),
        scratch_shapes=[pltpu.VMEM((cap, d), BF16), pltpu.VMEM((cap, N_EXPERTS), F32),
                        pltpu.VMEM((cap, d), BF16), pltpu.VMEM((1, tm), jnp.int32),
                        pltpu.SMEM((3 * N_GROUPS,), jnp.int32)],
        compiler_params=_params("parallel", "parallel", "arbitrary"),
        name="moe_final_norm",
    )(h2, route, x1, gt, sc, sh, gf, tri, wg, wu, wd)


def _block_ones(n, blk, lower):
    i = jnp.arange(n)
    m = (i[:, None] // blk) == (i[None, :] // blk)
    if lower:
        m = m & (i[None, :] <= i[:, None])
    return m.astype(BF16)


def _consts(tr, tg):
    return dict(seg=_block_ones(RWKV_WIDTH, RWKV_HEAD, False), tri=_block_ones(tr, CHUNK, True),
                tri_gla=_block_ones(tg, CHUNK, True), tri_up=_block_ones(LANES, LANES, True).T)


def _pad_rows(w, first_row):
    out = jnp.zeros((LORA_PAD, w.shape[1]), F32)
    return lax.dynamic_update_slice(out, w, (first_row, 0)).astype(BF16)


def _layer_params(l, w_in, mu_shift, w0, w_decay_up, a0, w_a_up, w_g_up, k_k, k_a, r_k, lnx_w, lnx_b,
                  w_gla_gate_up, b_gla_gate, gla_norm_g, w_out, norm2_g,
                  w_router_group, b_router_group, w_router_expert, b_router_expert):
    w_in_p = _win_layout(w_in[l], 256)
    n_pad = ROUTE_ROWS - N_EXPERTS - N_GROUPS
    w_router = jnp.concatenate([w_router_expert[l].T, w_router_group[l].T, jnp.zeros((n_pad, D_MODEL), F32)])
    b_router = jnp.concatenate([b_router_expert[l], b_router_group[l],
                                jnp.zeros((n_pad,), F32)]).reshape(ROUTE_ROWS, 1)
    wgate = jnp.zeros((LANES, GLA_KEY_WIDTH), F32).at[:GLA_GATE_RANK].set(w_gla_gate_up[l]).astype(BF16)
    r1 = lambda a: a.reshape(1, -1)
    return dict(
        w_in=w_in_p, mu=r1(mu_shift[l]), w0=r1(w0[l]), a0=r1(a0[l]), k_k=r1(k_k[l]), k_a=r1(k_a[l]),
        r_k=r1(r_k[l]), wd=_pad_rows(w_decay_up[l], 0), wa=_pad_rows(w_a_up[l], DECAY_LORA),
        wg=_pad_rows(w_g_up[l], DECAY_LORA + AAA_LORA), lnx_w=r1(lnx_w[l]), lnx_b=r1(lnx_b[l]),
        wgate=wgate, bgate=r1(b_gla_gate[l]), gla_g=r1(gla_norm_g[l]),
        w_out=w_out[l].astype(BF16), norm2_g=r1(norm2_g[l]), w_router=w_router, b_router=b_router)


def _run_layer(x, mod, shift0, wkv0, gla0, lp, experts, final):
    bsz, seq, d = x.shape
    assert seq % CHUNK == 0
    bb = 2
    tg = min(seq, GLA_STEP_ROWS)
    consts = _consts(TOKEN_TILE_ROWS, tg)
    m = lambda j: mod[:, j:j + 1, :]
    sh1, sc1, gt1, sh2, sc2, gt2 = (m(j) for j in range(6))
    proj, tails = _in_proj(x, sc1, sh1, lp["norm1_g"], lp["w_in"], TOKEN_TILE_ROWS)
    tails = tails.reshape(bsz, -1, RWKV_PROJ)
    new_shift = tails[:, -1]
    prev = jnp.concatenate([shift0[:, None, :], tails[:, :-1]], axis=1).reshape(-1, 1, RWKV_PROJ)
    *prep, gl = _rwkv_prep(proj, prev, lp, consts, TOKEN_TILE_ROWS)
    gl = gl.reshape(bsz, seq // CHUNK, RWKV_WIDTH)
    *chunk_in, bonus, g = prep
    s0 = jnp.zeros((bsz, RWKV_PAIRS, 2, RWKV_HEAD, 2, RWKV_HEAD), F32)
    w5 = wkv0.reshape(bsz, RWKV_PAIRS, 2, RWKV_HEAD, RWKV_HEAD)
    s0 = s0.at[:, :, 0, :, 0, :].set(w5[:, :, 0]).at[:, :, 1, :, 1, :].set(w5[:, :, 1])
    s0 = s0.reshape(bsz, RWKV_PAIRS, LANES, LANES)
    yr, s_bd = _rwkv_chunk(chunk_in, gl[:, :, None, :], s0, bb, min(seq, RWKV_STEP_ROWS) // CHUNK)
    s6 = s_bd.reshape(bsz, RWKV_PAIRS, 2, RWKV_HEAD, 2, RWKV_HEAD)
    new_wkv = jnp.stack([s6[:, :, 0, :, 0, :], s6[:, :, 1, :, 1, :]], axis=2).reshape(
        bsz, RWKV_HEADS, RWKV_HEAD, RWKV_HEAD)
    g5 = jnp.swapaxes(gla0, -1, -2).reshape(bsz, GLA_PAIRS, 2, GLA_DV, GLA_DK)
    t0 = jnp.zeros((bsz, GLA_PAIRS, 2, GLA_DV, 2, GLA_DK), F32)
    t0 = t0.at[:, :, 0, :, 0, :].set(g5[:, :, 0]).at[:, :, 1, :, 1, :].set(g5[:, :, 1])
    yg, t_bd = _gla(proj, lp, consts["tri_gla"], t0.reshape(bsz, GLA_PAIRS, 2 * GLA_DV, LANES), bb, tg)
    t6 = t_bd.reshape(bsz, GLA_PAIRS, 2, GLA_DV, 2, GLA_DK)
    new_gla = jnp.swapaxes(
        jnp.stack([t6[:, :, 0, :, 0, :], t6[:, :, 1, :, 1, :]], axis=2).reshape(
            bsz, GLA_HEADS, GLA_DV, GLA_DK), -1, -2)
    x1, h2, route = _out_proj(yr, bonus, g, yg, x, gt1, sc2, sh2, lp, consts, TOKEN_TILE_ROWS)
    out = _moe(h2, route, x1, gt2, *final, *experts, consts["tri_up"], MOE_TILE_ROWS)
    return out, new_shift, new_wkv, new_gla


def kernel(x_prompt, x_sample, c_prompt, c_sample, state_rwkv_shift, state_rwkv_wkv, state_gla_kv, w_ada, b_ada, norm1_g, norm2_g, w_in, mu_shift, w0, w_decay_up, a0, w_a_up, w_g_up, k_k, k_a, r_k, lnx_w, lnx_b, w_gla_gate_up, b_gla_gate, gla_norm_g, w_out, w_router_group, b_router_group, w_router_expert, b_router_expert, w_expert_gate, w_expert_up, w_expert_down, w_ada_final, b_ada_final, normf_g):
    assert w_ada.shape[0] == 1, "the final norm is fused into the single layer's MoE kernel"
    bp, bs = x_prompt.shape[0], x_sample.shape[0]
    d = D_MODEL
    n_rows = -(-(bp + bs) // 8) * 8
    c_all = jnp.zeros((n_rows, d), F32).at[:bp].set(c_prompt).at[bp:bp + bs].set(c_sample)
    modf = _modulation(c_all, w_ada_final, b_ada_final, 1024).reshape(n_rows, 2, d)
    mod = _modulation(c_all, w_ada[0], b_ada[0], 1536).reshape(n_rows, 6, d)
    lp = _layer_params(0, w_in, mu_shift, w0, w_decay_up, a0, w_a_up, w_g_up, k_k, k_a, r_k, lnx_w,
                       lnx_b, w_gla_gate_up, b_gla_gate, gla_norm_g, w_out, norm2_g,
                       w_router_group, b_router_group, w_router_expert, b_router_expert)
    lp["norm1_g"] = norm1_g[0].reshape(1, d)
    experts = (w_expert_gate[0].astype(BF16), w_expert_up[0].astype(BF16), w_expert_down[0].astype(BF16))
    groups = [
        (x_prompt, 0, bp, jnp.zeros((bp, RWKV_PROJ), F32),
         jnp.zeros((bp, RWKV_HEADS, RWKV_HEAD, RWKV_HEAD), F32), jnp.zeros((bp, GLA_HEADS, GLA_DK, GLA_DV), F32)),
        (x_sample, bp, bp + bs, state_rwkv_shift[0], state_rwkv_wkv[0], state_gla_kv[0]),
    ]
    ys, states = [], []
    for x, lo, hi, shift0, wkv0, gla0 in groups:
        final = (modf[lo:hi, 1:2], modf[lo:hi, 0:1], normf_g.reshape(1, d))
        y, *st = _run_layer(x, mod[lo:hi], shift0, wkv0, gla0, lp, experts, final)
        ys.append(y)
        states.extend(s[None] for s in st)
    return tuple(ys + states)
```

```python
import functools

import jax
import jax.numpy as jnp
from jax import lax
from jax.experimental import pallas as pl
from jax.experimental.pallas import tpu as pltpu

F32 = jnp.float32
BF16 = jnp.bfloat16

LANES = 128
VMEM_LIMIT_BYTES = 56 * 1024 * 1024
TOKEN_TILE_ROWS = 512
MOE_TILE_ROWS = 1024
MOE_SUB_ROWS = 256
MOE_GATHER_ROWS = 256
GLA_STEP_ROWS = 256
RWKV_STEP_ROWS = 128

D_MODEL = 1024
CHUNK = 64
RWKV_WIDTH = 512
RWKV_HEAD = 64
RWKV_HEADS = RWKV_WIDTH // RWKV_HEAD
HEAD_LANES = RWKV_HEAD
HEAD_SHIFT = HEAD_LANES.bit_length() - 1
assert 1 << HEAD_SHIFT == HEAD_LANES
RWKV_TILE_LANES = 128
RWKV_TILE_HEADS = RWKV_TILE_LANES // RWKV_HEAD
RWKV_TILES = RWKV_WIDTH // RWKV_TILE_LANES
DECAY_LORA = 32
AAA_LORA = 32
GATE_LORA = 64
LORA_PAD = DECAY_LORA + AAA_LORA + GATE_LORA
RWKV_PROJ = 3 * RWKV_WIDTH + LORA_PAD
GLA_WIDTH = 512
GLA_HEADS = 4
GLA_PAIRS = GLA_HEADS // 2
GLA_DV = GLA_WIDTH // GLA_HEADS
GLA_DK = GLA_DV // 2
GLA_KEY_WIDTH = GLA_HEADS * GLA_DK
GLA_GATE_RANK = 16
GLA_TAU = 16.0
GLA_PROJ = 2 * GLA_KEY_WIDTH + 2 * GLA_WIDTH + GLA_GATE_RANK
GLA_PROJ_PAD = RWKV_PROJ
IN_PROJ_PAD = RWKV_PROJ + GLA_PROJ_PAD
N_GROUPS = 4
EXPERTS_PER_GROUP = 4
N_EXPERTS = N_GROUPS * EXPERTS_PER_GROUP
GROUP_SHIFT = EXPERTS_PER_GROUP.bit_length() - 1
assert 1 << GROUP_SHIFT == EXPERTS_PER_GROUP
ROUTE_ROWS = 24
D_EXPERT = 512
RMS_EPS = 1e-6
LNX_EPS = 64e-5

_NN = (((1,), (0,)), ((), ()))
_NT = (((1,), (1,)), ((), ()))
_TN = (((0,), (0,)), ((), ()))


def _dg(a, b, dims=_NN):
    return lax.dot_general(a, b, dims, preferred_element_type=F32)


def _split2(x):
    hi = x.astype(BF16)
    lo = (x - hi.astype(F32)).astype(BF16)
    return hi, lo


def _split3(x):
    hi = x.astype(BF16)
    r1 = x - hi.astype(F32)
    mid = r1.astype(BF16)
    lo = (r1 - mid.astype(F32)).astype(BF16)
    return hi, mid, lo


def _mm1(a, b, dims=_NN):
    return _dg(a.astype(BF16), b.astype(BF16), dims)


def _mm3(a, b, dims=_NN):
    ah, al = _split2(a)
    bh, bl = _split2(b)
    return _dg(ah, bh, dims) + (_dg(ah, bl, dims) + _dg(al, bh, dims))


def _mm_exact_lhs(e, x, dims=_NN):
    h, m, l = _split3(x)
    return _dg(e, h, dims) + (_dg(e, m, dims) + _dg(e, l, dims))


def _mm2_exact_rhs(x, e, dims=_NN):
    h, l = _split2(x)
    return _dg(h, e, dims) + _dg(l, e, dims)


def _softplus(z):
    return jnp.maximum(z, 0.0) + jnp.log(1.0 + jnp.exp(-jnp.abs(z)))


def _sigmoid(z):
    return 1.0 / (1.0 + jnp.exp(-z))


def _silu(z):
    return z * _sigmoid(z)


def _params(*sem):
    return pltpu.CompilerParams(dimension_semantics=sem, vmem_limit_bytes=VMEM_LIMIT_BYTES)


def _mod_kernel(c_ref, w_ref, b_ref, o_ref):
    o_ref[...] = _mm1(_silu(c_ref[...]), w_ref[...]) + b_ref[...]


def _modulation(c, w, b, tn):
    rows, d = c.shape
    n = w.shape[1]
    return pl.pallas_call(
        _mod_kernel,
        grid=(n // tn,),
        in_specs=[pl.BlockSpec((rows, d), lambda j: (0, 0)),
                  pl.BlockSpec((d, tn), lambda j: (0, j)),
                  pl.BlockSpec((1, tn), lambda j: (0, j))],
        out_specs=pl.BlockSpec((rows, tn), lambda j: (0, j)),
        out_shape=jax.ShapeDtypeStruct((rows, n), F32),
        compiler_params=_params("parallel"),
        name="modulation",
    )(c, w, b.reshape(1, n))


def _win_layout_kernel(w_ref, o_ref):
    qkv_end = RWKV_PROJ + 2 * GLA_KEY_WIDTH + GLA_WIDTH
    gate_end = qkv_end + GLA_GATE_RANK
    o_ref[:, :qkv_end] = w_ref[:, :qkv_end].astype(BF16)
    o_ref[:, qkv_end:qkv_end + GLA_WIDTH] = w_ref[:, gate_end:gate_end + GLA_WIDTH].astype(BF16)
    pad = jnp.zeros((w_ref.shape[0], IN_PROJ_PAD - RWKV_PROJ - GLA_PROJ), F32)
    o_ref[:, qkv_end + GLA_WIDTH:] = jnp.concatenate([w_ref[:, qkv_end:gate_end], pad], axis=1).astype(BF16)


def _win_layout(w, rows):
    d, n = w.shape
    return pl.pallas_call(
        _win_layout_kernel,
        grid=(d // rows,),
        in_specs=[pl.BlockSpec((rows, n), lambda i: (i, 0))],
        out_specs=pl.BlockSpec((rows, IN_PROJ_PAD), lambda i: (i, 0)),
        out_shape=jax.ShapeDtypeStruct((d, IN_PROJ_PAD), BF16),
        compiler_params=_params("parallel"),
        name="w_in_layout",
    )(w)


def _rms_mod(x, g, sc, sh):
    ms = jnp.mean(x * x, axis=-1, keepdims=True)
    return (x * lax.rsqrt(ms + RMS_EPS) * g) * (1.0 + sc) + sh


def _token_tile(bsz, seq, rows):
    if seq >= rows:
        assert seq % rows == 0
        return 1, rows
    nb = min(bsz, rows // seq)
    assert bsz % nb == 0
    return nb, seq


def _inproj_kernel(x_ref, sc_ref, sh_ref, g_ref, w_ref, o_ref, last_ref, *, n_step):
    nb, rb, d = x_ref.shape
    h = _rms_mod(x_ref[...], g_ref[...], sc_ref[...], sh_ref[...])
    hb = h.reshape(nb * rb, d).astype(BF16)
    for j in range(IN_PROJ_PAD // n_step):
        cols = slice(j * n_step, (j + 1) * n_step)
        o_ref[:, :, cols] = _dg(hb, w_ref[:, cols]).reshape(nb, rb, n_step)
    last_ref[...] = o_ref[:, rb - 1:rb, :RWKV_PROJ]


def _in_proj(x, sc, sh, g, w, rows):
    bsz, seq, d = x.shape
    nb, rb = _token_tile(bsz, seq, rows)
    n_seq_tiles = seq // rb
    vec = pl.BlockSpec((nb, 1, d), lambda b, i: (b, 0, 0))
    return pl.pallas_call(
        functools.partial(_inproj_kernel, n_step=2 * LANES),
        grid=(bsz // nb, n_seq_tiles),
        in_specs=[pl.BlockSpec((nb, rb, d), lambda b, i: (b, i, 0)), vec, vec,
                  pl.BlockSpec((1, d), lambda b, i: (0, 0)),
                  pl.BlockSpec((d, IN_PROJ_PAD), lambda b, i: (0, 0))],
        out_specs=[pl.BlockSpec((nb, rb, IN_PROJ_PAD), lambda b, i: (b, i, 0)),
                   pl.BlockSpec((nb, 1, RWKV_PROJ), lambda b, i: (b * n_seq_tiles + i, 0, 0))],
        out_shape=[jax.ShapeDtypeStruct((bsz, seq, IN_PROJ_PAD), F32),
                   jax.ShapeDtypeStruct((bsz * n_seq_tiles, 1, RWKV_PROJ), F32)],
        compiler_params=_params("parallel", "parallel"),
        name="norm1_in_proj",
    )(x, sc, sh, g, w)


def _rwkv_prep_kernel(p_ref, prev_ref, mu_ref, w0_ref, a0_ref, kk_ref, ka_ref, rk_ref,
                      wd_ref, wa_ref, wg_ref, seg_ref, tri_ref,
                      at_ref, bt_ref, rt_ref, kt_ref, be_ref, ke_ref, v_ref, bonus_ref, g_ref, gl_ref):
    nb, rb, wp = p_ref.shape
    tr = nb * rb
    p = p_ref[...].reshape(tr, wp)
    row = lax.broadcasted_iota(jnp.int32, (nb, rb, wp), 1)
    xx = jnp.where(row == 0, prev_ref[...], pltpu.roll(p, 1, 0).reshape(nb, rb, wp)).reshape(tr, wp)
    ps = p + (xx - p) * mu_ref[...]
    w = RWKV_WIDTH
    r, k, v, lora = ps[:, :w], ps[:, w:2 * w], ps[:, 2 * w:3 * w], ps[:, 3 * w:]
    logw = -_softplus(-(w0_ref[...] + _mm1(jnp.tanh(lora), wd_ref[...]))) - 0.5
    lw = -jnp.exp(logw)
    a = _sigmoid(a0_ref[...] + _mm1(lora, wa_ref[...]))
    g = _mm1(_sigmoid(lora), wg_ref[...])
    seg = seg_ref[...]
    kk = k * kk_ref[...]
    kk = kk / jnp.maximum(jnp.sqrt(_mm2_exact_rhs(kk * kk, seg)), 1e-12)
    k2 = k * (1.0 + (a - 1.0) * ka_ref[...])
    cum = _mm_exact_lhs(tri_ref[...], lw)
    lasts = [cum[c * CHUNK + CHUNK - 1:(c + 1) * CHUNK, :] for c in range(tr // CHUNK)]
    tot = jnp.concatenate([jnp.broadcast_to(l, (CHUNK, w)) for l in lasts], axis=0)
    for c, l in enumerate(lasts):
        gl_ref[c:c + 1, :] = jnp.exp(l)
    kka = kk * a
    ginv = jnp.exp(-cum)
    gend = jnp.exp(tot - cum)

    def put(ref, val):
        ref[...] = val.reshape(nb, rb, RWKV_WIDTH).astype(ref.dtype)

    put(at_ref, -kk * jnp.exp(cum - lw))
    put(bt_ref, kka * ginv)
    put(rt_ref, r * jnp.exp(cum))
    put(kt_ref, k2 * ginv)
    put(be_ref, kka * gend)
    put(ke_ref, k2 * gend)
    put(v_ref, v)
    put(bonus_ref, _mm2_exact_rhs(r * k2 * rk_ref[...], seg) * v)
    put(g_ref, g)


def _rwkv_prep(proj, prev, lp, consts, rows):
    bsz, seq, _ = proj.shape
    w = RWKV_WIDTH
    nb, rb = _token_tile(bsz, seq, rows)
    n_seq_tiles = seq // rb
    flat = lambda b, i: (b * n_seq_tiles + i, 0)
    row = lambda n: pl.BlockSpec((1, n), lambda b, i: (0, 0))
    full = lambda a: pl.BlockSpec(a.shape, lambda b, i: (0,) * a.ndim)
    tok = pl.BlockSpec((nb, rb, w), lambda b, i: (b, i, 0))
    out_tok = lambda dt: jax.ShapeDtypeStruct((bsz, seq, w), dt)
    dtypes = [F32, F32, BF16, BF16, BF16, BF16, BF16, F32, F32]
    return pl.pallas_call(
        _rwkv_prep_kernel,
        grid=(bsz // nb, n_seq_tiles),
        in_specs=[pl.BlockSpec((nb, rb, RWKV_PROJ), lambda b, i: (b, i, 0)),
                  pl.BlockSpec((nb, 1, RWKV_PROJ), lambda b, i: flat(b, i) + (0,)),
                  row(RWKV_PROJ), row(w), row(w), row(w), row(w), row(w),
                  full(lp["wd"]), full(lp["wa"]), full(lp["wg"]), full(consts["seg"]), full(consts["tri"])],
        out_specs=[tok] * 9 + [pl.BlockSpec((nb * rb // CHUNK, w), flat)],
        out_shape=[out_tok(dt) for dt in dtypes] + [jax.ShapeDtypeStruct((bsz * seq // CHUNK, w), F32)],
        compiler_params=_params("parallel", "parallel"),
        name="rwkv_prep",
    )(proj, prev, lp["mu"], lp["w0"], lp["a0"], lp["k_k"], lp["k_a"], lp["r_k"],
      lp["wd"], lp["wa"], lp["wg"], consts["seg"], consts["tri"])


def _head_masks(width):
    lane = lax.broadcasted_iota(jnp.int32, (CHUNK, width), 1)
    row = lax.broadcasted_iota(jnp.int32, (CHUNK, width), 0)
    return lane >> HEAD_SHIFT, row, lane & (HEAD_LANES - 1)


def _block_diag(x, head):
    z = jnp.zeros_like(x)
    return jnp.concatenate([jnp.where(head == j, x, z) for j in range(x.shape[1] // HEAD_LANES)], axis=0)


def _rwkv_chunk_kernel(at_ref, bt_ref, rt_ref, kt_ref, be_ref, ke_ref, v_ref, gl_ref, s0_ref,
                       y_ref, s_ref, *, bb, nck):
    @pl.when(pl.program_id(1) == 0)
    def _():
        s_ref[...] = s0_ref[...]

    tw = RWKV_TILE_LANES
    head, row, col = _head_masks(tw)
    strict = col < row
    incl = col <= row
    same8 = (col >> 3) == (row >> 3)
    lane2 = lax.broadcasted_iota(jnp.int32, (tw, tw), 1)
    row2 = lax.broadcasted_iota(jnp.int32, (tw, tw), 0)
    same_head = (lane2 >> HEAD_SHIFT) == (row2 >> HEAD_SHIFT)
    bd = functools.partial(_block_diag, head=head)
    c = CHUNK

    def pmm(p, q):
        return _dg(p.astype(BF16), bd(q.astype(BF16)))

    items = [(b, ck, pr) for b in range(bb) for ck in range(nck) for pr in range(RWKV_TILES)]
    n = range(len(items))
    rows = lambda ck: slice(ck * c, (ck + 1) * c)
    lanes = lambda pr: slice(pr * tw, (pr + 1) * tw)
    ld = lambda ref: [ref[b, rows(ck), lanes(pr)] for b, ck, pr in items]
    cat0 = lambda *xs: jnp.concatenate(xs, axis=0)
    cat1 = lambda *xs: jnp.concatenate(xs, axis=1)
    at, bt = ld(at_ref), ld(bt_ref)
    rt, kt, be, ke, v = (ld(r) for r in (rt_ref, kt_ref, be_ref, ke_ref, v_ref))
    ats, bts = [_split2(a) for a in at], [_split2(b) for b in bt]
    atb = [hi for hi, _ in ats]
    zero = jnp.zeros((c, tw), F32)
    gk = [_dg(cat0(atb[i], rt[i]), bd(kt[i]), _NT) for i in n]
    aak = [jnp.where(strict, gk[i][:c], zero) for i in n]
    ark = [jnp.where(incl, gk[i][c:], zero) for i in n]
    gb = [_dg(cat0(ats[i][0], ats[i][1], rt[i]), bd(bts[i][0]), _NT) for i in n]
    arb = [jnp.where(incl, gb[i][2 * c:], zero) for i in n]
    aab = [jnp.where(strict, gb[i][:c] + (gb[i][c:2 * c] + _dg(atb[i], bd(bts[i][1]), _NT)), zero) for i in n]
    z = [pmm(aak[i], v[i]) for i in n]
    a8 = [jnp.where(same8, aab[i], zero) for i in n]
    p2 = [pmm(a8[i], a8[i]) for i in n]
    p4 = [pmm(p2[i], p2[i]) for i in n]
    nn = [a8[i] + p2[i] + pmm(p2[i], a8[i]) for i in n]
    nn = [nn[i] + p4[i] + pmm(p4[i], nn[i]) for i in n]
    for lvl in (3, 4, 5):
        joins = ((col >> (lvl + 1)) == (row >> (lvl + 1))) & ((col >> lvl) != (row >> lvl))
        e = [jnp.where(joins, aab[i], zero) for i in n]
        te = [e[i] + pmm(nn[i], e[i]) for i in n]
        nn = [nn[i] + te[i] + pmm(te[i], nn[i]) for i in n]
    wu = [cat1(at[i], z[i]) + _dg(nn[i].astype(BF16), cat1(bd(atb[i]), bd(z[i].astype(BF16)))) for i in n]
    abk = [cat1(arb[i], ark[i]).astype(BF16) for i in n]
    s = {(b, pr): s_ref[b, pr] for b in range(bb) for pr in range(RWKV_TILES)}
    for ck in range(nck):
        cur = [i for i in n if items[i][1] == ck]
        key = lambda i: (items[i][0], items[i][2])
        x = {i: _dg(cat0(wu[i][:, :tw].astype(BF16), rt[i]), s[key(i)].astype(BF16), _NT) for i in cur}
        ub = {i: (x[i][:c] + wu[i][:, tw:]).astype(BF16) for i in cur}
        upd = {i: _dg(cat0(ub[i], v[i]), cat0(be[i], ke[i]), _TN) for i in cur}
        for i in cur:
            b, _, pr = items[i]
            y_ref[b, rows(ck), lanes(pr)] = x[i][c:] + _dg(abk[i], cat0(bd(ub[i]), bd(v[i])))
            s[b, pr] = (s[b, pr] * gl_ref[b, ck, :, lanes(pr)]
                        + jnp.where(same_head, upd[i], jnp.zeros_like(upd[i])))
    for (b, pr), val in s.items():
        s_ref[b, pr] = val


def _rwkv_chunk(prep, gl, s0, bb, nck):
    bsz, seq, w = prep[0].shape
    tok = pl.BlockSpec((bb, nck * CHUNK, w), lambda i, c: (i, c, 0))
    st = pl.BlockSpec((bb, RWKV_TILES, RWKV_TILE_LANES, RWKV_TILE_LANES), lambda i, c: (i, 0, 0, 0))
    return pl.pallas_call(
        functools.partial(_rwkv_chunk_kernel, bb=bb, nck=nck),
        grid=(bsz // bb, seq // (nck * CHUNK)),
        in_specs=[tok] * 7 + [pl.BlockSpec((bb, nck, 1, w), lambda i, c: (i, c, 0, 0)), st],
        out_specs=[tok, st],
        out_shape=[jax.ShapeDtypeStruct((bsz, seq, w), F32),
                   jax.ShapeDtypeStruct((bsz, RWKV_TILES, RWKV_TILE_LANES, RWKV_TILE_LANES), F32)],
        compiler_params=_params("parallel", "arbitrary"),
        name="rwkv_chunk",
    )(*prep, gl, s0)


def _gla_kernel(p_ref, wgate_ref, bgate_ref, tri_ref, ng_ref, s0_ref, y_ref, s_ref, *, bb, nck):
    @pl.when(pl.program_id(1) == 0)
    def _():
        s_ref[...] = s0_ref[...]

    head, row, col = _head_masks(LANES)
    incl = col <= row
    bd = functools.partial(_block_diag, head=head)
    lane2 = lax.broadcasted_iota(jnp.int32, (2 * GLA_DV, LANES), 1)
    row2 = lax.broadcasted_iota(jnp.int32, (2 * GLA_DV, LANES), 0)
    same_head = (lane2 < GLA_DK) == (row2 < GLA_DV)
    kw, gw = GLA_KEY_WIDTH, GLA_WIDTH
    zero = jnp.zeros((CHUNK, LANES), F32)
    cat0 = lambda xs: jnp.concatenate(xs, axis=0)
    cat1 = lambda xs: jnp.concatenate(xs, axis=1)
    rows = lambda c: slice(c * CHUNK, (c + 1) * CHUNK)
    lanes = lambda pr: slice(pr * LANES, (pr + 1) * LANES)
    bs, cs = range(bb), range(nck)
    x = [p_ref[b] for b in bs]
    gate = [_mm1(x[b][:, 2 * kw + 2 * gw:], wgate_ref[...]) + bgate_ref[...] for b in bs]
    la = [(jnp.minimum(g, 0.0) - jnp.log(1.0 + jnp.exp(-jnp.abs(g)))) / GLA_TAU for g in gate]
    cum = [_mm_exact_lhs(tri_ref[...], la[b]) for b in bs]
    q_dec = [(x[b][:, :kw] * (GLA_DK ** -0.5)) * jnp.exp(cum[b]) for b in bs]
    k_inv = [x[b][:, kw:2 * kw] * jnp.exp(-cum[b]) for b in bs]
    last = [[cum[b][c * CHUNK + CHUNK - 1:(c + 1) * CHUNK, :] for c in cs] for b in bs]
    k_end = [[x[b][rows(c), kw:2 * kw] * jnp.exp(last[b][c] - cum[b][rows(c)]) for c in cs] for b in bs]
    dec = [[jnp.exp(last[b][c]) for c in cs] for b in bs]
    items = [(b, c, pr) for b in bs for c in cs for pr in range(GLA_PAIRS)]
    vp = {(b, c, pr): x[b][rows(c), 2 * kw + pr * 2 * GLA_DV:2 * kw + (pr + 1) * 2 * GLA_DV]
          for b, c, pr in items}
    att = {(b, c, pr): jnp.where(incl, _mm1(q_dec[b][rows(c), lanes(pr)],
                                            bd(k_inv[b][rows(c), lanes(pr)]), _NT), zero)
           for b, c, pr in items}
    upd = {it: _mm1(vp[it], k_end[it[0]][it[1]][:, lanes(it[2])], _TN) for it in items}
    intra = {it: _mm1(att[it], cat0([cat1([vp[it][:, :GLA_DV], zero]), cat1([zero, vp[it][:, GLA_DV:]])]))
             for it in items}
    st = {}
    for b in bs:
        for pr in range(GLA_PAIRS):
            cur = s_ref[b, pr]
            for c in cs:
                st[b, c, pr] = cur
                cur = cur * dec[b][c][:, lanes(pr)] + jnp.where(same_head, upd[b, c, pr], jnp.zeros_like(cur))
            s_ref[b, pr] = cur
    o = {(b, c, pr): intra[b, c, pr] + _mm1(q_dec[b][rows(c), lanes(pr)], st[b, c, pr], _NT)
         for b, c, pr in items}
    for b in bs:
        ob = cat0([cat1([o[b, c, pr] for pr in range(GLA_PAIRS)]) for c in cs])
        heads = [ob[:, h * GLA_DV:(h + 1) * GLA_DV] for h in range(GLA_HEADS)]
        normed = [oh * lax.rsqrt(jnp.mean(oh * oh, axis=-1, keepdims=True) + RMS_EPS) for oh in heads]
        y_ref[b] = cat1(normed) * ng_ref[...] * _silu(x[b][:, 2 * kw + gw:2 * kw + 2 * gw])


def _gla(proj, lp, tri, s0, bb, tg):
    bsz, seq, _ = proj.shape
    full = lambda a: pl.BlockSpec(a.shape, lambda i, c: (0,) * a.ndim)
    st = pl.BlockSpec((bb, GLA_PAIRS, 2 * GLA_DV, LANES), lambda i, c: (i, 0, 0, 0))
    return pl.pallas_call(
        functools.partial(_gla_kernel, bb=bb, nck=tg // CHUNK),
        grid=(bsz // bb, seq // tg),
        in_specs=[pl.BlockSpec((bb, tg, GLA_PROJ_PAD), lambda i, c: (i, c, 1)),
                  full(lp["wgate"]), full(lp["bgate"]), full(tri), full(lp["gla_g"]), st],
        out_specs=[pl.BlockSpec((bb, tg, GLA_WIDTH), lambda i, c: (i, c, 0)), st],
        out_shape=[jax.ShapeDtypeStruct((bsz, seq, GLA_WIDTH), F32),
                   jax.ShapeDtypeStruct((bsz, GLA_PAIRS, 2 * GLA_DV, LANES), F32)],
        compiler_params=_params("parallel", "arbitrary"),
        name="gla_chunk",
    )(proj, lp["wgate"], lp["bgate"], tri, lp["gla_g"], s0)


def _route(lgt):
    tm = lgt.shape[1]
    n_pad = ROUTE_ROWS - N_EXPERTS
    rowg = lax.broadcasted_iota(jnp.int32, (n_pad, tm), 0)
    rowe = lax.broadcasted_iota(jnp.int32, (N_EXPERTS, tm), 0)
    rowg_f, rowe_f = rowg.astype(F32), rowe.astype(F32)

    def first_argmax(vals, mx, rows_f):
        return jnp.min(jnp.where(vals == mx, rows_f, float(ROUTE_ROWS)), axis=0, keepdims=True)

    is_group = rowg < N_GROUPS
    lg = jnp.where(is_group, lgt[N_EXPERTS:], -jnp.inf)
    gmax = jnp.max(lg, axis=0, keepdims=True)
    gi = first_argmax(lg, gmax, rowg_f)
    pg_top = 1.0 / jnp.sum(jnp.where(is_group, jnp.exp(lg - gmax), 0.0), axis=0, keepdims=True)
    in_group = (rowe >> GROUP_SHIFT).astype(F32) == gi
    le = jnp.where(in_group, lgt[:N_EXPERTS], -jnp.inf)
    m1 = jnp.max(le, axis=0, keepdims=True)
    i1 = first_argmax(le, m1, rowe_f)
    le2 = jnp.where(rowe_f == i1, -jnp.inf, le)
    m2 = jnp.max(le2, axis=0, keepdims=True)
    i2 = first_argmax(le2, m2, rowe_f)
    p2 = jnp.exp(m2 - m1)
    w1 = pg_top / (1.0 + p2)
    w2 = pg_top * p2 / (1.0 + p2)
    comb = jnp.where(rowe_f == i1, w1, 0.0) + jnp.where(rowe_f == i2, w2, 0.0)
    return jnp.concatenate([comb, jnp.where(rowg == 0, gi, 0.0)], axis=0)


def _outproj_kernel(yr_ref, bonus_ref, g_ref, yg_ref, x_ref, gt_ref, sc_ref, sh_ref,
                    lnw_ref, lnb_ref, seg_ref, wout_ref, g2_ref, wr_ref, br_ref,
                    x1_ref, h2_ref, route_ref):
    nb, rb, d_model = x_ref.shape
    tm = nb * rb
    flat = lambda ref: ref[...].reshape(tm, ref.shape[-1])
    seg = seg_ref[...]
    y = flat(yr_ref)
    inv_n = 1.0 / RWKV_HEAD
    d = y - _mm2_exact_rhs(y, seg) * inv_n
    var = _mm2_exact_rhs(d * d, seg) * inv_n
    yn = d * lax.rsqrt(var + LNX_EPS) * lnw_ref[...] + lnb_ref[...]
    yr = (yn + flat(bonus_ref)) * flat(g_ref)
    mix = jnp.concatenate([yr, flat(yg_ref)], axis=1)
    x1 = x_ref[...] + gt_ref[...] * _mm1(mix, wout_ref[...]).reshape(nb, rb, d_model)
    x1_ref[...] = x1
    h2 = _rms_mod(x1, g2_ref[...], sc_ref[...], sh_ref[...])
    h2_ref[...] = h2.astype(BF16)
    route_ref[...] = _route(_mm3(wr_ref[...], h2.reshape(tm, d_model), _NT) + br_ref[...])


def _out_proj(yr, bonus, g, yg, x, gt, sc, sh, lp, consts, rows):
    bsz, seq, d = x.shape
    nb, rb = _token_tile(bsz, seq, rows)
    n_seq_tiles = seq // rb
    half = pl.BlockSpec((nb, rb, RWKV_WIDTH), lambda b, i: (b, i, 0))
    tokd = pl.BlockSpec((nb, rb, d), lambda b, i: (b, i, 0))
    vec = pl.BlockSpec((nb, 1, d), lambda b, i: (b, 0, 0))
    full = lambda a: pl.BlockSpec(a.shape, lambda b, i: (0,) * a.ndim)
    args = (lp["lnx_w"], lp["lnx_b"], consts["seg"], lp["w_out"], lp["norm2_g"], lp["w_router"], lp["b_router"])
    return pl.pallas_call(
        _outproj_kernel,
        grid=(bsz // nb, seq // rb),
        in_specs=[half, half, half, half, tokd, vec, vec, vec] + [full(a) for a in args],
        out_specs=[tokd, tokd, pl.BlockSpec((ROUTE_ROWS, nb * rb), lambda b, i: (0, b * n_seq_tiles + i))],
        out_shape=[jax.ShapeDtypeStruct((bsz, seq, d), F32),
                   jax.ShapeDtypeStruct((bsz, seq, d), BF16),
                   jax.ShapeDtypeStruct((ROUTE_ROWS, bsz * seq), F32)],
        compiler_params=_params("parallel", "parallel"),
        name="out_proj_router",
    )(yr, bonus, g, yg, x, gt, sc, sh, *args)


def _moe_kernel(h_ref, route_ref, x1_ref, gt_ref, sc_ref, sh_ref, gf_ref, tri_ref, wg_ref, wu_ref, wd_ref,
                y_ref, hs_ref, cs_ref, ys_ref, pos_ref, meta_ref):
    g = pl.program_id(2)
    nb, rb, d = h_ref.shape
    tm = nb * rb
    sub, tail = MOE_SUB_ROWS, MOE_SUB_ROWS // 2
    cap = hs_ref.shape[0]
    n_lane_tiles = tm // LANES
    lane_tile = lambda k: slice(k * LANES, (k + 1) * LANES)

    @pl.when(g == 0)
    def _sort():
        row8 = lax.broadcasted_iota(jnp.int32, (8, LANES), 0).astype(F32)
        carry = jnp.zeros((8, 1), F32)
        members, ranks = [], []
        for k in range(n_lane_tiles):
            blk = jnp.where(row8 == route_ref[N_EXPERTS:N_EXPERTS + 1, lane_tile(k)], 1.0, 0.0)
            members.append(blk)
            ranks.append(_dg(blk.astype(BF16), tri_ref[...]) - blk + carry)
            carry = carry + jnp.sum(blk, axis=1, keepdims=True)
        rowc = lax.broadcasted_iota(jnp.int32, (8, 1), 0)
        lane = lax.broadcasted_iota(jnp.int32, (1, LANES), 1)
        first = jnp.zeros((1, 1), F32)
        off_col = jnp.zeros((8, 1), F32)
        meta = jnp.zeros((1, LANES), F32)
        for grp in range(N_GROUPS):
            count = jnp.sum(jnp.where(rowc == grp, carry, 0.0), axis=0, keepdims=True)
            full = jnp.floor(count * (1.0 / sub))
            rest = count - full * sub
            has_tail = jnp.where((rest > 0.0) & (rest <= float(tail)), 1.0, 0.0)
            n_full = full + jnp.where(rest > float(tail), 1.0, 0.0)
            off_col = off_col + jnp.where(rowc == grp, first, 0.0)
            meta = (meta + jnp.where(lane == grp, first, 0.0) + jnp.where(lane == N_GROUPS + grp, n_full, 0.0)
                    + jnp.where(lane == 2 * N_GROUPS + grp, has_tail, 0.0))
            first = first + n_full * sub + has_tail * tail
        meta = (meta + jnp.where(lane == 3 * N_GROUPS, first, 0.0)).astype(jnp.int32)
        for i in range(3 * N_GROUPS + 1):
            meta_ref[i] = meta[0, i]
        for k in range(n_lane_tiles):
            pos = jnp.sum(members[k] * (ranks[k] + off_col), axis=0, keepdims=True)
            pos_ref[:, lane_tile(k)] = pos.astype(jnp.int32)
        ys_ref[...] = jnp.zeros_like(ys_ref)
        h = h_ref[...].reshape(tm, d)
        comb3 = _split3(route_ref[:N_EXPERTS, :])
        for j in range(cap // MOE_GATHER_ROWS):
            @pl.when(j * MOE_GATHER_ROWS < meta_ref[3 * N_GROUPS])
            def _gather_rows(j=j):
                rows = slice(j * MOE_GATHER_ROWS, (j + 1) * MOE_GATHER_ROWS)
                prow = lax.broadcasted_iota(jnp.int32, (MOE_GATHER_ROWS, tm), 0) + j * MOE_GATHER_ROWS
                onehot = jnp.where(prow == pos_ref[...], 1.0, 0.0).astype(BF16)
                hs_ref[rows, :] = _dg(onehot, h).astype(BF16)
                cs_ref[rows, :] = (_dg(onehot, comb3[0], _NT)
                                   + (_dg(onehot, comb3[1], _NT) + _dg(onehot, comb3[2], _NT)))

    def experts(base, n_rows):
        hs = hs_ref[pl.ds(base, n_rows), :]
        cs = cs_ref[pl.ds(base, n_rows), :]
        lane = lax.broadcasted_iota(jnp.int32, cs.shape, 1)
        ys = jnp.zeros((n_rows, d), F32)
        for e in range(EXPERTS_PER_GROUP):
            hid = _silu(_dg(hs, wg_ref[e])) * _dg(hs, wu_ref[e])
            ce = jnp.sum(jnp.where(lane == g * EXPERTS_PER_GROUP + e, cs, 0.0), axis=-1, keepdims=True)
            ys = ys + ce * _dg(hid.astype(BF16), wd_ref[e])
        ys_ref[pl.ds(base, n_rows), :] = ys.astype(BF16)

    first_row = meta_ref[g]
    n_full = meta_ref[N_GROUPS + g]

    def full_sub_tile(j, carry_):
        experts(pl.multiple_of(first_row + j * sub, tail), sub)
        return carry_

    lax.fori_loop(0, n_full, full_sub_tile, 0)

    @pl.when(meta_ref[2 * N_GROUPS + g] == 1)
    def _tail():
        experts(pl.multiple_of(first_row + n_full * sub, tail), tail)

    @pl.when(g == N_GROUPS - 1)
    def _unsort():
        ys_all = ys_ref[...]
        prow = lax.broadcasted_iota(jnp.int32, (cap, LANES), 0)
        seqs = max(LANES // rb, 1)
        for k in range(n_lane_tiles):
            onehot = jnp.where(prow == pos_ref[:, lane_tile(k)], 1.0, 0.0).astype(BF16)
            moe = _dg(onehot, ys_all, _TN).reshape(seqs, LANES // seqs, d)
            b0 = k * LANES // rb
            r0 = k * LANES - b0 * rb
            bs, rs = slice(b0, b0 + seqs), slice(r0, r0 + LANES // seqs)
            x2 = x1_ref[bs, rs] + gt_ref[bs] * moe
            y_ref[bs, rs] = _rms_mod(x2, gf_ref[...], sc_ref[bs], sh_ref[bs])


def _moe(h2, route, x1, gt, sc, sh, gf, wg, wu, wd, tri, rows):
    bsz, seq, d = x1.shape
    nb, rb = _token_tile(bsz, seq, rows)
    tm = nb * rb
    assert tm % LANES == 0 and (rb % LANES == 0 or LANES % rb == 0)
    n_seq_tiles = seq // rb
    cap = tm + N_GROUPS * (MOE_SUB_ROWS // 2)
    assert cap % MOE_GATHER_ROWS == 0
    tokd = pl.BlockSpec((nb, rb, d), lambda b, i, g: (b, i, 0))
    vec = pl.BlockSpec((nb, 1, d), lambda b, i, g: (b, 0, 0))
    group_w = lambda shape: pl.BlockSpec((EXPERTS_PER_GROUP,) + shape, lambda b, i, g: (g, 0, 0))
    return pl.pallas_call(
        _moe_kernel,
        grid=(bsz // nb, n_seq_tiles, N_GROUPS),
        in_specs=[tokd, pl.BlockSpec((ROUTE_ROWS, tm), lambda b, i, g: (0, b * n_seq_tiles + i)),
                  pl.BlockSpec((nb, rb, d), lambda b, i, g: (b, i, 0), pipeline_mode=pl.Buffered(1)),
                  vec, vec, vec,
                  pl.BlockSpec((1, d), lambda b, i, g: (0, 0)),
                  pl.BlockSpec(tri.shape, lambda b, i, g: (0, 0)),
                  group_w((d, D_EXPERT)), group_w((d, D_EXPERT)), group_w((D_EXPERT, d))],
        out_specs=tokd,
        out_shape=jax.ShapeDtypeStruct((bsz, seq, d), F32),
        scratch_shapes=[pltpu.VMEM((cap, d), BF16), pltpu.VMEM((cap, N_EXPERTS), F32),
                        pltpu.VMEM((cap, d), BF16), pltpu.VMEM((1, tm), jnp.int32),
                        pltpu.SMEM((3 * N_GROUPS + 1,), jnp.int32)],
        compiler_params=_params("parallel", "parallel", "arbitrary"),
        name="moe_final_norm",
    )(h2, route, x1, gt, sc, sh, gf, tri, wg, wu, wd)


def _block_ones(n, blk, lower):
    i = jnp.arange(n)
    m = (i[:, None] // blk) == (i[None, :] // blk)
    if lower:
        m = m & (i[None, :] <= i[:, None])
    return m.astype(BF16)


def _consts(tr, tg):
    return dict(seg=_block_ones(RWKV_WIDTH, RWKV_HEAD, False), tri=_block_ones(tr, CHUNK, True),
                tri_gla=_block_ones(tg, CHUNK, True), tri_up=_block_ones(LANES, LANES, True).T)


def _heads_block_diag(s):
    n, c = s.shape[2], s.shape[-1]
    rows = [jnp.pad(s[:, :, j], ((0, 0), (0, 0), (0, 0), (j * c, (n - 1 - j) * c))) for j in range(n)]
    return jnp.concatenate(rows, axis=2)


def _diag_blocks(t, n):
    r, c = t.shape[-2] // n, t.shape[-1] // n
    return jnp.stack([t[:, :, j * r:(j + 1) * r, j * c:(j + 1) * c] for j in range(n)], axis=2)


def _pad_rows(w, first_row):
    out = jnp.zeros((LORA_PAD, w.shape[1]), F32)
    return lax.dynamic_update_slice(out, w, (first_row, 0)).astype(BF16)


def _layer_params(l, w_in, mu_shift, w0, w_decay_up, a0, w_a_up, w_g_up, k_k, k_a, r_k, lnx_w, lnx_b,
                  w_gla_gate_up, b_gla_gate, gla_norm_g, w_out, norm2_g,
                  w_router_group, b_router_group, w_router_expert, b_router_expert):
    w_in_p = _win_layout(w_in[l], 256)
    n_pad = ROUTE_ROWS - N_EXPERTS - N_GROUPS
    w_router = jnp.concatenate([w_router_expert[l].T, w_router_group[l].T, jnp.zeros((n_pad, D_MODEL), F32)])
    b_router = jnp.concatenate([b_router_expert[l], b_router_group[l],
                                jnp.zeros((n_pad,), F32)]).reshape(ROUTE_ROWS, 1)
    wgate = jnp.zeros((LANES, GLA_KEY_WIDTH), F32).at[:GLA_GATE_RANK].set(w_gla_gate_up[l]).astype(BF16)
    r1 = lambda a: a.reshape(1, -1)
    return dict(
        w_in=w_in_p, mu=r1(mu_shift[l]), w0=r1(w0[l]), a0=r1(a0[l]), k_k=r1(k_k[l]), k_a=r1(k_a[l]),
        r_k=r1(r_k[l]), wd=_pad_rows(w_decay_up[l], 0), wa=_pad_rows(w_a_up[l], DECAY_LORA),
        wg=_pad_rows(w_g_up[l], DECAY_LORA + AAA_LORA), lnx_w=r1(lnx_w[l]), lnx_b=r1(lnx_b[l]),
        wgate=wgate, bgate=r1(b_gla_gate[l]), gla_g=r1(gla_norm_g[l]),
        w_out=w_out[l].astype(BF16), norm2_g=r1(norm2_g[l]), w_router=w_router, b_router=b_router)


def _run_layer(x, mod, shift0, wkv0, gla0, lp, experts, final):
    bsz, seq, d = x.shape
    assert seq % CHUNK == 0
    bb = 2
    tg = min(seq, GLA_STEP_ROWS)
    consts = _consts(TOKEN_TILE_ROWS, tg)
    m = lambda j: mod[:, j:j + 1, :]
    sh1, sc1, gt1, sh2, sc2, gt2 = (m(j) for j in range(6))
    proj, tails = _in_proj(x, sc1, sh1, lp["norm1_g"], lp["w_in"], TOKEN_TILE_ROWS)
    tails = tails.reshape(bsz, -1, RWKV_PROJ)
    new_shift = tails[:, -1]
    prev = jnp.concatenate([shift0[:, None, :], tails[:, :-1]], axis=1).reshape(-1, 1, RWKV_PROJ)
    *prep, gl = _rwkv_prep(proj, prev, lp, consts, TOKEN_TILE_ROWS)
    gl = gl.reshape(bsz, seq // CHUNK, RWKV_WIDTH)
    *chunk_in, bonus, g = prep
    s0 = _heads_block_diag(wkv0.reshape(bsz, RWKV_TILES, RWKV_TILE_HEADS, RWKV_HEAD, RWKV_HEAD))
    yr, s_bd = _rwkv_chunk(chunk_in, gl[:, :, None, :], s0, bb, min(seq, RWKV_STEP_ROWS) // CHUNK)
    new_wkv = _diag_blocks(s_bd, RWKV_TILE_HEADS).reshape(bsz, RWKV_HEADS, RWKV_HEAD, RWKV_HEAD)
    t0 = _heads_block_diag(jnp.swapaxes(gla0, -1, -2).reshape(bsz, GLA_PAIRS, 2, GLA_DV, GLA_DK))
    yg, t_bd = _gla(proj, lp, consts["tri_gla"], t0, bb, tg)
    new_gla = jnp.swapaxes(_diag_blocks(t_bd, 2).reshape(bsz, GLA_HEADS, GLA_DV, GLA_DK), -1, -2)
    x1, h2, route = _out_proj(yr, bonus, g, yg, x, gt1, sc2, sh2, lp, consts, TOKEN_TILE_ROWS)
    out = _moe(h2, route, x1, gt2, *final, *experts, consts["tri_up"], MOE_TILE_ROWS)
    return out, new_shift, new_wkv, new_gla


def kernel(x_prompt, x_sample, c_prompt, c_sample, state_rwkv_shift, state_rwkv_wkv, state_gla_kv, w_ada, b_ada, norm1_g, norm2_g, w_in, mu_shift, w0, w_decay_up, a0, w_a_up, w_g_up, k_k, k_a, r_k, lnx_w, lnx_b, w_gla_gate_up, b_gla_gate, gla_norm_g, w_out, w_router_group, b_router_group, w_router_expert, b_router_expert, w_expert_gate, w_expert_up, w_expert_down, w_ada_final, b_ada_final, normf_g):
    assert w_ada.shape[0] == 1, "the final norm is fused into the single layer's MoE kernel"
    bp, bs = x_prompt.shape[0], x_sample.shape[0]
    d = D_MODEL
    n_rows = -(-(bp + bs) // 8) * 8
    c_all = jnp.zeros((n_rows, d), F32).at[:bp].set(c_prompt).at[bp:bp + bs].set(c_sample)
    modf = _modulation(c_all, w_ada_final, b_ada_final, 1024).reshape(n_rows, 2, d)
    mod = _modulation(c_all, w_ada[0], b_ada[0], 1536).reshape(n_rows, 6, d)
    lp = _layer_params(0, w_in, mu_shift, w0, w_decay_up, a0, w_a_up, w_g_up, k_k, k_a, r_k, lnx_w,
                       lnx_b, w_gla_gate_up, b_gla_gate, gla_norm_g, w_out, norm2_g,
                       w_router_group, b_router_group, w_router_expert, b_router_expert)
    lp["norm1_g"] = norm1_g[0].reshape(1, d)
    experts = (w_expert_gate[0].astype(BF16), w_expert_up[0].astype(BF16), w_expert_down[0].astype(BF16))
    groups = [
        (x_prompt, 0, bp, jnp.zeros((bp, RWKV_PROJ), F32),
         jnp.zeros((bp, RWKV_HEADS, RWKV_HEAD, RWKV_HEAD), F32), jnp.zeros((bp, GLA_HEADS, GLA_DK, GLA_DV), F32)),
        (x_sample, bp, bp + bs, state_rwkv_shift[0], state_rwkv_wkv[0], state_gla_kv[0]),
    ]
    ys, states = [], []
    for x, lo, hi, shift0, wkv0, gla0 in groups:
        final = (modf[lo:hi, 1:2], modf[lo:hi, 0:1], normf_g.reshape(1, d))
        y, *st = _run_layer(x, mod[lo:hi], shift0, wkv0, gla0, lp, experts, final)
        ys.append(y)
        states.extend(s[None] for s in st)
    return tuple(ys + states)
```

```python
import functools

import jax
import jax.numpy as jnp
from jax import lax
from jax.experimental import pallas as pl
from jax.experimental.pallas import tpu as pltpu

F32 = jnp.float32
BF16 = jnp.bfloat16

LANES = 128
VMEM_LIMIT_BYTES = 56 * 1024 * 1024
TOKEN_TILE_ROWS = 512
MOE_TILE_ROWS = 1024
MOE_SUB_ROWS = 256
MOE_GATHER_ROWS = 256
GLA_STEP_ROWS = 256
RWKV_STEP_ROWS = 128

D_MODEL = 1024
CHUNK = 64
RWKV_WIDTH = 512
RWKV_HEAD = 64
RWKV_HEADS = RWKV_WIDTH // RWKV_HEAD
HEAD_LANES = RWKV_HEAD
HEAD_SHIFT = HEAD_LANES.bit_length() - 1
assert 1 << HEAD_SHIFT == HEAD_LANES
RWKV_TILE_LANES = 128
RWKV_TILE_HEADS = RWKV_TILE_LANES // RWKV_HEAD
RWKV_TILES = RWKV_WIDTH // RWKV_TILE_LANES
DECAY_LORA = 32
AAA_LORA = 32
GATE_LORA = 64
LORA_PAD = DECAY_LORA + AAA_LORA + GATE_LORA
RWKV_PROJ = 3 * RWKV_WIDTH + LORA_PAD
GLA_WIDTH = 512
GLA_HEADS = 4
GLA_PAIRS = GLA_HEADS // 2
GLA_DV = GLA_WIDTH // GLA_HEADS
GLA_DK = GLA_DV // 2
GLA_KEY_WIDTH = GLA_HEADS * GLA_DK
GLA_GATE_RANK = 16
GLA_TAU = 16.0
GLA_PROJ = 2 * GLA_KEY_WIDTH + 2 * GLA_WIDTH + GLA_GATE_RANK
GLA_PROJ_PAD = RWKV_PROJ
IN_PROJ_PAD = RWKV_PROJ + GLA_PROJ_PAD
N_GROUPS = 4
EXPERTS_PER_GROUP = 4
N_EXPERTS = N_GROUPS * EXPERTS_PER_GROUP
GROUP_SHIFT = EXPERTS_PER_GROUP.bit_length() - 1
assert 1 << GROUP_SHIFT == EXPERTS_PER_GROUP
ROUTE_ROWS = 24
D_EXPERT = 512
RMS_EPS = 1e-6
LNX_EPS = 64e-5
LOG2_E = 1.4426950408889634

_NN = (((1,), (0,)), ((), ()))
_NT = (((1,), (1,)), ((), ()))
_TN = (((0,), (0,)), ((), ()))


def _dg(a, b, dims=_NN):
    return lax.dot_general(a, b, dims, preferred_element_type=F32)


def _split2(x):
    hi = x.astype(BF16)
    lo = (x - hi.astype(F32)).astype(BF16)
    return hi, lo


def _split3(x):
    hi = x.astype(BF16)
    r1 = x - hi.astype(F32)
    mid = r1.astype(BF16)
    lo = (r1 - mid.astype(F32)).astype(BF16)
    return hi, mid, lo


def _mm1(a, b, dims=_NN):
    return _dg(a.astype(BF16), b.astype(BF16), dims)


def _mm3(a, b, dims=_NN):
    ah, al = _split2(a)
    bh, bl = _split2(b)
    return _dg(ah, bh, dims) + (_dg(ah, bl, dims) + _dg(al, bh, dims))


def _mm_exact_lhs(e, x, dims=_NN):
    h, m, l = _split3(x)
    return _dg(e, h, dims) + (_dg(e, m, dims) + _dg(e, l, dims))


def _mm2_exact_rhs(x, e, dims=_NN):
    h, l = _split2(x)
    return _dg(h, e, dims) + _dg(l, e, dims)


def _softplus(z):
    return jnp.maximum(z, 0.0) + jnp.log(1.0 + jnp.exp(-jnp.abs(z)))


def _sigmoid(z):
    return 1.0 / (1.0 + jnp.exp(-z))


def _silu(z):
    return z * _sigmoid(z)


def _params(*sem):
    return pltpu.CompilerParams(dimension_semantics=sem, vmem_limit_bytes=VMEM_LIMIT_BYTES)


def _mod_kernel(c_ref, w_ref, b_ref, o_ref):
    o_ref[...] = _mm1(_silu(c_ref[...]), w_ref[...]) + b_ref[...]


def _modulation(c, w, b, tn, layer=None):
    rows, d = c.shape
    n = w.shape[-1]
    if layer is None:
        w_spec = pl.BlockSpec((d, tn), lambda j: (0, j))
    else:
        w_spec = pl.BlockSpec((None, d, tn), lambda j: (layer, 0, j))
    return pl.pallas_call(
        _mod_kernel,
        grid=(n // tn,),
        in_specs=[pl.BlockSpec((rows, d), lambda j: (0, 0)), w_spec,
                  pl.BlockSpec((1, tn), lambda j: (0, j))],
        out_specs=pl.BlockSpec((rows, tn), lambda j: (0, j)),
        out_shape=jax.ShapeDtypeStruct((rows, n), F32),
        compiler_params=_params("parallel"),
        name="modulation",
    )(c, w, b.reshape(1, n))


def _win_layout_kernel(w_ref, o_ref):
    qkv_end = RWKV_PROJ + 2 * GLA_KEY_WIDTH + GLA_WIDTH
    gate_end = qkv_end + GLA_GATE_RANK
    o_ref[:, :qkv_end] = w_ref[:, :qkv_end].astype(BF16)
    o_ref[:, qkv_end:qkv_end + GLA_WIDTH] = w_ref[:, gate_end:gate_end + GLA_WIDTH].astype(BF16)
    pad = jnp.zeros((w_ref.shape[0], IN_PROJ_PAD - RWKV_PROJ - GLA_PROJ), F32)
    o_ref[:, qkv_end + GLA_WIDTH:] = jnp.concatenate([w_ref[:, qkv_end:gate_end], pad], axis=1).astype(BF16)


def _win_layout(w, rows):
    d, n = w.shape
    return pl.pallas_call(
        _win_layout_kernel,
        grid=(d // rows,),
        in_specs=[pl.BlockSpec((rows, n), lambda i: (i, 0))],
        out_specs=pl.BlockSpec((rows, IN_PROJ_PAD), lambda i: (i, 0)),
        out_shape=jax.ShapeDtypeStruct((d, IN_PROJ_PAD), BF16),
        compiler_params=_params("parallel"),
        name="w_in_layout",
    )(w)


def _rms_mod(x, g, sc, sh):
    ms = jnp.mean(x * x, axis=-1, keepdims=True)
    return (x * lax.rsqrt(ms + RMS_EPS) * g) * (1.0 + sc) + sh


def _token_tile(bsz, seq, rows):
    if seq >= rows:
        assert seq % rows == 0
        return 1, rows
    nb = min(bsz, rows // seq)
    assert bsz % nb == 0
    return nb, seq


def _inproj_kernel(x_ref, sc_ref, sh_ref, g_ref, w_ref, o_ref, last_ref, *, n_step):
    nb, rb, d = x_ref.shape
    h = _rms_mod(x_ref[...], g_ref[...], sc_ref[...], sh_ref[...])
    hb = h.reshape(nb * rb, d).astype(BF16)
    for j in range(IN_PROJ_PAD // n_step):
        cols = slice(j * n_step, (j + 1) * n_step)
        o_ref[:, :, cols] = _dg(hb, w_ref[:, cols]).reshape(nb, rb, n_step)
    last_ref[...] = o_ref[:, rb - 1:rb, :RWKV_PROJ]


def _in_proj(x, sc, sh, g, w, rows):
    bsz, seq, d = x.shape
    nb, rb = _token_tile(bsz, seq, rows)
    n_seq_tiles = seq // rb
    vec = pl.BlockSpec((nb, 1, d), lambda b, i: (b, 0, 0))
    return pl.pallas_call(
        functools.partial(_inproj_kernel, n_step=2 * LANES),
        grid=(bsz // nb, n_seq_tiles),
        in_specs=[pl.BlockSpec((nb, rb, d), lambda b, i: (b, i, 0)), vec, vec,
                  pl.BlockSpec((1, d), lambda b, i: (0, 0)),
                  pl.BlockSpec((d, IN_PROJ_PAD), lambda b, i: (0, 0))],
        out_specs=[pl.BlockSpec((nb, rb, IN_PROJ_PAD), lambda b, i: (b, i, 0)),
                   pl.BlockSpec((nb, 1, RWKV_PROJ), lambda b, i: (b * n_seq_tiles + i, 0, 0))],
        out_shape=[jax.ShapeDtypeStruct((bsz, seq, IN_PROJ_PAD), F32),
                   jax.ShapeDtypeStruct((bsz * n_seq_tiles, 1, RWKV_PROJ), F32)],
        compiler_params=_params("parallel", "parallel"),
        name="norm1_in_proj",
    )(x, sc, sh, g, w)


def _rwkv_prep_kernel(p_ref, prev_ref, mu_ref, w0_ref, a0_ref, kk_ref, ka_ref, rk_ref,
                      wd_ref, wa_ref, wg_ref, seg_ref, tri_ref,
                      at_ref, bt_ref, rt_ref, kt_ref, be_ref, ke_ref, v_ref, bonus_ref, g_ref, gl_ref):
    nb, rb, wp = p_ref.shape
    tr = nb * rb
    p = p_ref[...].reshape(tr, wp)
    row = lax.broadcasted_iota(jnp.int32, (nb, rb, wp), 1)
    xx = jnp.where(row == 0, prev_ref[...], pltpu.roll(p, 1, 0).reshape(nb, rb, wp)).reshape(tr, wp)
    ps = p + (xx - p) * mu_ref[...]
    w = RWKV_WIDTH
    r, k, v, lora = ps[:, :w], ps[:, w:2 * w], ps[:, 2 * w:3 * w], ps[:, 3 * w:]
    logw = -_softplus(-(w0_ref[...] + _mm1(jnp.tanh(lora), wd_ref[...]))) - 0.5
    lw = jnp.exp(logw) * (-LOG2_E)
    a = _sigmoid(a0_ref[...] + _mm1(lora, wa_ref[...]))
    g = _mm1(_sigmoid(lora), wg_ref[...])
    seg = seg_ref[...]
    kk = k * kk_ref[...]
    kk = kk / jnp.maximum(jnp.sqrt(_mm2_exact_rhs(kk * kk, seg)), 1e-12)
    k2 = k * (1.0 + (a - 1.0) * ka_ref[...])
    cum = _mm_exact_lhs(tri_ref[...], lw)
    lasts = [jnp.exp2(cum[c * CHUNK + CHUNK - 1:(c + 1) * CHUNK, :]) for c in range(tr // CHUNK)]
    tot = jnp.concatenate([jnp.broadcast_to(l, (CHUNK, w)) for l in lasts], axis=0)
    for c, l in enumerate(lasts):
        gl_ref[c:c + 1, :] = l
    kka = kk * a
    ginv = jnp.exp2(-cum)
    gend = tot * ginv

    def put(ref, val):
        ref[...] = val.reshape(nb, rb, RWKV_WIDTH).astype(ref.dtype)

    put(at_ref, -kk * jnp.exp2(cum - lw))
    put(bt_ref, kka * ginv)
    put(rt_ref, r * jnp.exp2(cum))
    put(kt_ref, k2 * ginv)
    put(be_ref, kka * gend)
    put(ke_ref, k2 * gend)
    put(v_ref, v)
    put(bonus_ref, _mm2_exact_rhs(r * k2 * rk_ref[...], seg) * v)
    put(g_ref, g)


def _rwkv_prep(proj, prev, lp, consts, rows):
    bsz, seq, _ = proj.shape
    w = RWKV_WIDTH
    nb, rb = _token_tile(bsz, seq, rows)
    n_seq_tiles = seq // rb
    flat = lambda b, i: (b * n_seq_tiles + i, 0)
    row = lambda n: pl.BlockSpec((1, n), lambda b, i: (0, 0))
    full = lambda a: pl.BlockSpec(a.shape, lambda b, i: (0,) * a.ndim)
    tok = pl.BlockSpec((nb, rb, w), lambda b, i: (b, i, 0))
    out_tok = lambda dt: jax.ShapeDtypeStruct((bsz, seq, w), dt)
    dtypes = [F32, F32, BF16, BF16, BF16, BF16, BF16, F32, F32]
    return pl.pallas_call(
        _rwkv_prep_kernel,
        grid=(bsz // nb, n_seq_tiles),
        in_specs=[pl.BlockSpec((nb, rb, RWKV_PROJ), lambda b, i: (b, i, 0)),
                  pl.BlockSpec((nb, 1, RWKV_PROJ), lambda b, i: flat(b, i) + (0,)),
                  row(RWKV_PROJ), row(w), row(w), row(w), row(w), row(w),
                  full(lp["wd"]), full(lp["wa"]), full(lp["wg"]), full(consts["seg"]), full(consts["tri"])],
        out_specs=[tok] * 9 + [pl.BlockSpec((nb * rb // CHUNK, w), flat)],
        out_shape=[out_tok(dt) for dt in dtypes] + [jax.ShapeDtypeStruct((bsz * seq // CHUNK, w), F32)],
        compiler_params=_params("parallel", "parallel"),
        name="rwkv_prep",
    )(proj, prev, lp["mu"], lp["w0"], lp["a0"], lp["k_k"], lp["k_a"], lp["r_k"],
      lp["wd"], lp["wa"], lp["wg"], consts["seg"], consts["tri"])


def _head_masks(width):
    lane = lax.broadcasted_iota(jnp.int32, (CHUNK, width), 1)
    row = lax.broadcasted_iota(jnp.int32, (CHUNK, width), 0)
    return lane >> HEAD_SHIFT, row, lane & (HEAD_LANES - 1)


def _block_diag(x, head):
    z = jnp.zeros_like(x)
    return jnp.concatenate([jnp.where(head == j, x, z) for j in range(x.shape[1] // HEAD_LANES)], axis=0)


def _rwkv_chunk_kernel(at_ref, bt_ref, rt_ref, kt_ref, be_ref, ke_ref, v_ref, gl_ref, s0_ref,
                       y_ref, s_ref, *, bb, nck):
    @pl.when(pl.program_id(1) == 0)
    def _():
        s_ref[...] = s0_ref[...]

    tw = RWKV_TILE_LANES
    head, row, col = _head_masks(tw)
    strict = col < row
    incl = col <= row
    same8 = (col >> 3) == (row >> 3)
    lane2 = lax.broadcasted_iota(jnp.int32, (tw, tw), 1)
    row2 = lax.broadcasted_iota(jnp.int32, (tw, tw), 0)
    same_head = (lane2 >> HEAD_SHIFT) == (row2 >> HEAD_SHIFT)
    bd = functools.partial(_block_diag, head=head)
    c = CHUNK

    def pmm(p, q):
        return _dg(p.astype(BF16), bd(q.astype(BF16)))

    items = [(b, ck, pr) for b in range(bb) for ck in range(nck) for pr in range(RWKV_TILES)]
    n = range(len(items))
    rows = lambda ck: slice(ck * c, (ck + 1) * c)
    lanes = lambda pr: slice(pr * tw, (pr + 1) * tw)
    ld = lambda ref: [ref[b, rows(ck), lanes(pr)] for b, ck, pr in items]
    cat0 = lambda *xs: jnp.concatenate(xs, axis=0)
    cat1 = lambda *xs: jnp.concatenate(xs, axis=1)
    at, bt = ld(at_ref), ld(bt_ref)
    rt, kt, be, ke, v = (ld(r) for r in (rt_ref, kt_ref, be_ref, ke_ref, v_ref))
    ats, bts = [_split2(a) for a in at], [_split2(b) for b in bt]
    atb = [hi for hi, _ in ats]
    zero = jnp.zeros((c, tw), F32)
    gk = [_dg(cat0(atb[i], rt[i]), bd(kt[i]), _NT) for i in n]
    aak = [jnp.where(strict, gk[i][:c], zero) for i in n]
    ark = [jnp.where(incl, gk[i][c:], zero) for i in n]
    gb = [_dg(cat0(ats[i][0], ats[i][1], rt[i]), bd(bts[i][0]), _NT) for i in n]
    arb = [jnp.where(incl, gb[i][2 * c:], zero) for i in n]
    aab = [jnp.where(strict, gb[i][:c] + (gb[i][c:2 * c] + _dg(atb[i], bd(bts[i][1]), _NT)), zero) for i in n]
    z = [pmm(aak[i], v[i]) for i in n]
    a8 = [jnp.where(same8, aab[i], zero) for i in n]
    p2 = [pmm(a8[i], a8[i]) for i in n]
    p4 = [pmm(p2[i], p2[i]) for i in n]
    nn = [a8[i] + p2[i] + pmm(p2[i], a8[i]) for i in n]
    nn = [nn[i] + p4[i] + pmm(p4[i], nn[i]) for i in n]
    for lvl in (3, 4, 5):
        joins = ((col >> (lvl + 1)) == (row >> (lvl + 1))) & ((col >> lvl) != (row >> lvl))
        e = [jnp.where(joins, aab[i], zero) for i in n]
        te = [e[i] + pmm(nn[i], e[i]) for i in n]
        nn = [nn[i] + te[i] + pmm(te[i], nn[i]) for i in n]
    wu = [cat1(at[i], z[i]) + _dg(nn[i].astype(BF16), cat1(bd(atb[i]), bd(z[i].astype(BF16)))) for i in n]
    abk = [cat1(arb[i], ark[i]).astype(BF16) for i in n]
    s = {(b, pr): s_ref[b, pr] for b in range(bb) for pr in range(RWKV_TILES)}
    for ck in range(nck):
        cur = [i for i in n if items[i][1] == ck]
        key = lambda i: (items[i][0], items[i][2])
        x = {i: _dg(cat0(wu[i][:, :tw].astype(BF16), rt[i]), s[key(i)].astype(BF16), _NT) for i in cur}
        ub = {i: (x[i][:c] + wu[i][:, tw:]).astype(BF16) for i in cur}
        upd = {i: _dg(cat0(ub[i], v[i]), cat0(be[i], ke[i]), _TN) for i in cur}
        for i in cur:
            b, _, pr = items[i]
            y_ref[b, rows(ck), lanes(pr)] = x[i][c:] + _dg(abk[i], cat0(bd(ub[i]), bd(v[i])))
            s[b, pr] = (s[b, pr] * gl_ref[b, ck, :, lanes(pr)]
                        + jnp.where(same_head, upd[i], jnp.zeros_like(upd[i])))
    for (b, pr), val in s.items():
        s_ref[b, pr] = val


def _rwkv_chunk(prep, gl, s0, bb, nck):
    bsz, seq, w = prep[0].shape
    tok = pl.BlockSpec((bb, nck * CHUNK, w), lambda i, c: (i, c, 0))
    st = pl.BlockSpec((bb, RWKV_TILES, RWKV_TILE_LANES, RWKV_TILE_LANES), lambda i, c: (i, 0, 0, 0))
    return pl.pallas_call(
        functools.partial(_rwkv_chunk_kernel, bb=bb, nck=nck),
        grid=(bsz // bb, seq // (nck * CHUNK)),
        in_specs=[tok] * 7 + [pl.BlockSpec((bb, nck, 1, w), lambda i, c: (i, c, 0, 0)), st],
        out_specs=[tok, st],
        out_shape=[jax.ShapeDtypeStruct((bsz, seq, w), F32),
                   jax.ShapeDtypeStruct((bsz, RWKV_TILES, RWKV_TILE_LANES, RWKV_TILE_LANES), F32)],
        compiler_params=_params("parallel", "arbitrary"),
        name="rwkv_chunk",
    )(*prep, gl, s0)


def _gla_kernel(p_ref, wgate_ref, bgate_ref, tri_ref, ng_ref, s0_ref, y_ref, s_ref, *, bb, nck):
    @pl.when(pl.program_id(1) == 0)
    def _():
        s_ref[...] = s0_ref[...]

    head, row, col = _head_masks(LANES)
    incl = col <= row
    bd = functools.partial(_block_diag, head=head)
    lane2 = lax.broadcasted_iota(jnp.int32, (2 * GLA_DV, LANES), 1)
    row2 = lax.broadcasted_iota(jnp.int32, (2 * GLA_DV, LANES), 0)
    same_head = (lane2 < GLA_DK) == (row2 < GLA_DV)
    kw, gw = GLA_KEY_WIDTH, GLA_WIDTH
    zero = jnp.zeros((CHUNK, LANES), F32)
    cat0 = lambda xs: jnp.concatenate(xs, axis=0)
    cat1 = lambda xs: jnp.concatenate(xs, axis=1)
    rows = lambda c: slice(c * CHUNK, (c + 1) * CHUNK)
    lanes = lambda pr: slice(pr * LANES, (pr + 1) * LANES)
    bs, cs = range(bb), range(nck)
    x = [p_ref[b] for b in bs]
    gate = [_mm1(x[b][:, 2 * kw + 2 * gw:], wgate_ref[...]) + bgate_ref[...] for b in bs]
    la = [(jnp.minimum(g, 0.0) - jnp.log(1.0 + jnp.exp(-jnp.abs(g)))) / GLA_TAU for g in gate]
    cum = [_mm_exact_lhs(tri_ref[...], la[b]) for b in bs]
    q_dec = [(x[b][:, :kw] * (GLA_DK ** -0.5)) * jnp.exp(cum[b]) for b in bs]
    k_inv = [x[b][:, kw:2 * kw] * jnp.exp(-cum[b]) for b in bs]
    last = [[cum[b][c * CHUNK + CHUNK - 1:(c + 1) * CHUNK, :] for c in cs] for b in bs]
    k_end = [[x[b][rows(c), kw:2 * kw] * jnp.exp(last[b][c] - cum[b][rows(c)]) for c in cs] for b in bs]
    dec = [[jnp.exp(last[b][c]) for c in cs] for b in bs]
    items = [(b, c, pr) for b in bs for c in cs for pr in range(GLA_PAIRS)]
    vp = {(b, c, pr): x[b][rows(c), 2 * kw + pr * 2 * GLA_DV:2 * kw + (pr + 1) * 2 * GLA_DV]
          for b, c, pr in items}
    att = {(b, c, pr): jnp.where(incl, _mm1(q_dec[b][rows(c), lanes(pr)],
                                            bd(k_inv[b][rows(c), lanes(pr)]), _NT), zero)
           for b, c, pr in items}
    upd = {it: _mm1(vp[it], k_end[it[0]][it[1]][:, lanes(it[2])], _TN) for it in items}
    intra = {it: _mm1(att[it], cat0([cat1([vp[it][:, :GLA_DV], zero]), cat1([zero, vp[it][:, GLA_DV:]])]))
             for it in items}
    st = {}
    for b in bs:
        for pr in range(GLA_PAIRS):
            cur = s_ref[b, pr]
            for c in cs:
                st[b, c, pr] = cur
                cur = cur * dec[b][c][:, lanes(pr)] + jnp.where(same_head, upd[b, c, pr], jnp.zeros_like(cur))
            s_ref[b, pr] = cur
    o = {(b, c, pr): intra[b, c, pr] + _mm1(q_dec[b][rows(c), lanes(pr)], st[b, c, pr], _NT)
         for b, c, pr in items}
    for b in bs:
        ob = cat0([cat1([o[b, c, pr] for pr in range(GLA_PAIRS)]) for c in cs])
        heads = [ob[:, h * GLA_DV:(h + 1) * GLA_DV] for h in range(GLA_HEADS)]
        normed = [oh * lax.rsqrt(jnp.mean(oh * oh, axis=-1, keepdims=True) + RMS_EPS) for oh in heads]
        y_ref[b] = cat1(normed) * ng_ref[...] * _silu(x[b][:, 2 * kw + gw:2 * kw + 2 * gw])


def _gla(proj, lp, tri, s0, bb, tg):
    bsz, seq, _ = proj.shape
    full = lambda a: pl.BlockSpec(a.shape, lambda i, c: (0,) * a.ndim)
    st = pl.BlockSpec((bb, GLA_PAIRS, 2 * GLA_DV, LANES), lambda i, c: (i, 0, 0, 0))
    return pl.pallas_call(
        functools.partial(_gla_kernel, bb=bb, nck=tg // CHUNK),
        grid=(bsz // bb, seq // tg),
        in_specs=[pl.BlockSpec((bb, tg, GLA_PROJ_PAD), lambda i, c: (i, c, 1)),
                  full(lp["wgate"]), full(lp["bgate"]), full(tri), full(lp["gla_g"]), st],
        out_specs=[pl.BlockSpec((bb, tg, GLA_WIDTH), lambda i, c: (i, c, 0)), st],
        out_shape=[jax.ShapeDtypeStruct((bsz, seq, GLA_WIDTH), F32),
                   jax.ShapeDtypeStruct((bsz, GLA_PAIRS, 2 * GLA_DV, LANES), F32)],
        compiler_params=_params("parallel", "arbitrary"),
        name="gla_chunk",
    )(proj, lp["wgate"], lp["bgate"], tri, lp["gla_g"], s0)


def _route(lgt):
    tm = lgt.shape[1]
    n_pad = ROUTE_ROWS - N_EXPERTS
    rowg = lax.broadcasted_iota(jnp.int32, (n_pad, tm), 0)
    rowe = lax.broadcasted_iota(jnp.int32, (N_EXPERTS, tm), 0)
    rowg_f, rowe_f = rowg.astype(F32), rowe.astype(F32)

    def first_argmax(vals, mx, rows_f):
        return jnp.min(jnp.where(vals == mx, rows_f, float(ROUTE_ROWS)), axis=0, keepdims=True)

    is_group = rowg < N_GROUPS
    lg = jnp.where(is_group, lgt[N_EXPERTS:], -jnp.inf)
    gmax = jnp.max(lg, axis=0, keepdims=True)
    gi = first_argmax(lg, gmax, rowg_f)
    pg_top = 1.0 / jnp.sum(jnp.where(is_group, jnp.exp(lg - gmax), 0.0), axis=0, keepdims=True)
    in_group = (rowe >> GROUP_SHIFT).astype(F32) == gi
    le = jnp.where(in_group, lgt[:N_EXPERTS], -jnp.inf)
    m1 = jnp.max(le, axis=0, keepdims=True)
    i1 = first_argmax(le, m1, rowe_f)
    le2 = jnp.where(rowe_f == i1, -jnp.inf, le)
    m2 = jnp.max(le2, axis=0, keepdims=True)
    i2 = first_argmax(le2, m2, rowe_f)
    p2 = jnp.exp(m2 - m1)
    w1 = pg_top / (1.0 + p2)
    w2 = pg_top * p2 / (1.0 + p2)
    comb = jnp.where(rowe_f == i1, w1, 0.0) + jnp.where(rowe_f == i2, w2, 0.0)
    return jnp.concatenate([comb, jnp.where(rowg == 0, gi, 0.0)], axis=0)


def _outproj_kernel(yr_ref, bonus_ref, g_ref, yg_ref, x_ref, gt_ref, sc_ref, sh_ref,
                    lnw_ref, lnb_ref, seg_ref, wout_ref, g2_ref, wr_ref, br_ref,
                    x1_ref, h2_ref, route_ref):
    nb, rb, d_model = x_ref.shape
    tm = nb * rb
    flat = lambda ref: ref[...].reshape(tm, ref.shape[-1])
    seg = seg_ref[...]
    y = flat(yr_ref)
    inv_n = 1.0 / RWKV_HEAD
    d = y - _mm2_exact_rhs(y, seg) * inv_n
    var = _mm2_exact_rhs(d * d, seg) * inv_n
    yn = d * lax.rsqrt(var + LNX_EPS) * lnw_ref[...] + lnb_ref[...]
    yr = (yn + flat(bonus_ref)) * flat(g_ref)
    mix = jnp.concatenate([yr, flat(yg_ref)], axis=1)
    x1 = x_ref[...] + gt_ref[...] * _mm1(mix, wout_ref[...]).reshape(nb, rb, d_model)
    x1_ref[...] = x1
    h2 = _rms_mod(x1, g2_ref[...], sc_ref[...], sh_ref[...])
    h2_ref[...] = h2.astype(BF16)
    route_ref[...] = _route(_mm3(wr_ref[...], h2.reshape(tm, d_model), _NT) + br_ref[...])


def _out_proj(yr, bonus, g, yg, x, gt, sc, sh, lp, consts, rows):
    bsz, seq, d = x.shape
    nb, rb = _token_tile(bsz, seq, rows)
    n_seq_tiles = seq // rb
    half = pl.BlockSpec((nb, rb, RWKV_WIDTH), lambda b, i: (b, i, 0))
    tokd = pl.BlockSpec((nb, rb, d), lambda b, i: (b, i, 0))
    vec = pl.BlockSpec((nb, 1, d), lambda b, i: (b, 0, 0))
    full = lambda a: pl.BlockSpec(a.shape, lambda b, i: (0,) * a.ndim)
    args = (lp["lnx_w"], lp["lnx_b"], consts["seg"], lp["w_out"], lp["norm2_g"], lp["w_router"], lp["b_router"])
    return pl.pallas_call(
        _outproj_kernel,
        grid=(bsz // nb, seq // rb),
        in_specs=[half, half, half, half, tokd, vec, vec, vec] + [full(a) for a in args],
        out_specs=[tokd, tokd, pl.BlockSpec((ROUTE_ROWS, nb * rb), lambda b, i: (0, b * n_seq_tiles + i))],
        out_shape=[jax.ShapeDtypeStruct((bsz, seq, d), F32),
                   jax.ShapeDtypeStruct((bsz, seq, d), BF16),
                   jax.ShapeDtypeStruct((ROUTE_ROWS, bsz * seq), F32)],
        compiler_params=_params("parallel", "parallel"),
        name="out_proj_router",
    )(yr, bonus, g, yg, x, gt, sc, sh, *args)


def _moe_kernel(h_ref, route_ref, x1_ref, gt_ref, sc_ref, sh_ref, gf_ref, tri_ref, wg_ref, wu_ref, wd_ref,
                y_ref, hs_ref, cs_ref, ys_ref, pos_ref, meta_ref):
    g = pl.program_id(2)
    nb, rb, d = h_ref.shape
    tm = nb * rb
    sub, tail = MOE_SUB_ROWS, MOE_SUB_ROWS // 2
    cap = hs_ref.shape[0]
    n_lane_tiles = tm // LANES
    lane_tile = lambda k: slice(k * LANES, (k + 1) * LANES)

    @pl.when(g == 0)
    def _sort():
        row8 = lax.broadcasted_iota(jnp.int32, (8, LANES), 0).astype(F32)
        carry = jnp.zeros((8, 1), F32)
        members, ranks = [], []
        for k in range(n_lane_tiles):
            blk = jnp.where(row8 == route_ref[N_EXPERTS:N_EXPERTS + 1, lane_tile(k)], 1.0, 0.0)
            members.append(blk)
            ranks.append(_dg(blk.astype(BF16), tri_ref[...]) - blk + carry)
            carry = carry + jnp.sum(blk, axis=1, keepdims=True)
        rowc = lax.broadcasted_iota(jnp.int32, (8, 1), 0)
        lane = lax.broadcasted_iota(jnp.int32, (1, LANES), 1)
        first = jnp.zeros((1, 1), F32)
        off_col = jnp.zeros((8, 1), F32)
        meta = jnp.zeros((1, LANES), F32)
        for grp in range(N_GROUPS):
            count = jnp.sum(jnp.where(rowc == grp, carry, 0.0), axis=0, keepdims=True)
            full = jnp.floor(count * (1.0 / sub))
            rest = count - full * sub
            has_tail = jnp.where((rest > 0.0) & (rest <= float(tail)), 1.0, 0.0)
            n_full = full + jnp.where(rest > float(tail), 1.0, 0.0)
            off_col = off_col + jnp.where(rowc == grp, first, 0.0)
            meta = (meta + jnp.where(lane == grp, first, 0.0) + jnp.where(lane == N_GROUPS + grp, n_full, 0.0)
                    + jnp.where(lane == 2 * N_GROUPS + grp, has_tail, 0.0))
            first = first + n_full * sub + has_tail * tail
        meta = (meta + jnp.where(lane == 3 * N_GROUPS, first, 0.0)).astype(jnp.int32)
        for i in range(3 * N_GROUPS + 1):
            meta_ref[i] = meta[0, i]
        for k in range(n_lane_tiles):
            pos = jnp.sum(members[k] * (ranks[k] + off_col), axis=0, keepdims=True)
            pos_ref[:, lane_tile(k)] = pos.astype(jnp.int32)
        ys_ref[...] = jnp.zeros_like(ys_ref)
        h = h_ref[...].reshape(tm, d)
        comb3 = _split3(route_ref[:N_EXPERTS, :])
        for j in range(cap // MOE_GATHER_ROWS):
            @pl.when(j * MOE_GATHER_ROWS < meta_ref[3 * N_GROUPS])
            def _gather_rows(j=j):
                rows = slice(j * MOE_GATHER_ROWS, (j + 1) * MOE_GATHER_ROWS)
                prow = lax.broadcasted_iota(jnp.int32, (MOE_GATHER_ROWS, tm), 0) + j * MOE_GATHER_ROWS
                onehot = jnp.where(prow == pos_ref[...], 1.0, 0.0).astype(BF16)
                hs_ref[rows, :] = _dg(onehot, h).astype(BF16)
                cs_ref[rows, :] = (_dg(onehot, comb3[0], _NT)
                                   + (_dg(onehot, comb3[1], _NT) + _dg(onehot, comb3[2], _NT)))

    def experts(base, n_rows):
        hs = hs_ref[pl.ds(base, n_rows), :]
        cs = cs_ref[pl.ds(base, n_rows), :]
        lane = lax.broadcasted_iota(jnp.int32, cs.shape, 1)
        ys = jnp.zeros((n_rows, d), F32)
        for e in range(EXPERTS_PER_GROUP):
            hid = _silu(_dg(hs, wg_ref[e])) * _dg(hs, wu_ref[e])
            ce = jnp.sum(jnp.where(lane == g * EXPERTS_PER_GROUP + e, cs, 0.0), axis=-1, keepdims=True)
            ys = ys + ce * _dg(hid.astype(BF16), wd_ref[e])
        ys_ref[pl.ds(base, n_rows), :] = ys.astype(BF16)

    first_row = meta_ref[g]
    n_full = meta_ref[N_GROUPS + g]

    def full_sub_tile(j, carry_):
        experts(pl.multiple_of(first_row + j * sub, tail), sub)
        return carry_

    lax.fori_loop(0, n_full, full_sub_tile, 0)

    @pl.when(meta_ref[2 * N_GROUPS + g] == 1)
    def _tail():
        experts(pl.multiple_of(first_row + n_full * sub, tail), tail)

    @pl.when(g == N_GROUPS - 1)
    def _unsort():
        ys_all = ys_ref[...]
        prow = lax.broadcasted_iota(jnp.int32, (cap, LANES), 0)
        seqs = max(LANES // rb, 1)
        for k in range(n_lane_tiles):
            onehot = jnp.where(prow == pos_ref[:, lane_tile(k)], 1.0, 0.0).astype(BF16)
            moe = _dg(onehot, ys_all, _TN).reshape(seqs, LANES // seqs, d)
            b0 = k * LANES // rb
            r0 = k * LANES - b0 * rb
            bs, rs = slice(b0, b0 + seqs), slice(r0, r0 + LANES // seqs)
            x2 = x1_ref[bs, rs] + gt_ref[bs] * moe
            y_ref[bs, rs] = _rms_mod(x2, gf_ref[...], sc_ref[bs], sh_ref[bs])


def _moe(h2, route, x1, gt, sc, sh, gf, wg, wu, wd, tri, rows):
    bsz, seq, d = x1.shape
    nb, rb = _token_tile(bsz, seq, rows)
    tm = nb * rb
    assert tm % LANES == 0 and (rb % LANES == 0 or LANES % rb == 0)
    n_seq_tiles = seq // rb
    cap = tm + N_GROUPS * (MOE_SUB_ROWS // 2)
    assert cap % MOE_GATHER_ROWS == 0
    tokd = pl.BlockSpec((nb, rb, d), lambda b, i, g: (b, i, 0))
    vec = pl.BlockSpec((nb, 1, d), lambda b, i, g: (b, 0, 0))
    group_w = lambda shape: pl.BlockSpec((EXPERTS_PER_GROUP,) + shape, lambda b, i, g: (g, 0, 0))
    return pl.pallas_call(
        _moe_kernel,
        grid=(bsz // nb, n_seq_tiles, N_GROUPS),
        in_specs=[tokd, pl.BlockSpec((ROUTE_ROWS, tm), lambda b, i, g: (0, b * n_seq_tiles + i)),
                  pl.BlockSpec((nb, rb, d), lambda b, i, g: (b, i, 0), pipeline_mode=pl.Buffered(1)),
                  vec, vec, vec,
                  pl.BlockSpec((1, d), lambda b, i, g: (0, 0)),
                  pl.BlockSpec(tri.shape, lambda b, i, g: (0, 0)),
                  group_w((d, D_EXPERT)), group_w((d, D_EXPERT)), group_w((D_EXPERT, d))],
        out_specs=tokd,
        out_shape=jax.ShapeDtypeStruct((bsz, seq, d), F32),
        scratch_shapes=[pltpu.VMEM((cap, d), BF16), pltpu.VMEM((cap, N_EXPERTS), F32),
                        pltpu.VMEM((cap, d), BF16), pltpu.VMEM((1, tm), jnp.int32),
                        pltpu.SMEM((3 * N_GROUPS + 1,), jnp.int32)],
        compiler_params=_params("parallel", "parallel", "arbitrary"),
        name="moe_final_norm",
    )(h2, route, x1, gt, sc, sh, gf, tri, wg, wu, wd)


def _block_ones(n, blk, lower):
    i = jnp.arange(n)
    m = (i[:, None] // blk) == (i[None, :] // blk)
    if lower:
        m = m & (i[None, :] <= i[:, None])
    return m.astype(BF16)


def _consts(tr, tg):
    return dict(seg=_block_ones(RWKV_WIDTH, RWKV_HEAD, False), tri=_block_ones(tr, CHUNK, True),
                tri_gla=_block_ones(tg, CHUNK, True), tri_up=_block_ones(LANES, LANES, True).T)


def _heads_block_diag(s):
    n, c = s.shape[2], s.shape[-1]
    rows = [jnp.pad(s[:, :, j], ((0, 0), (0, 0), (0, 0), (j * c, (n - 1 - j) * c))) for j in range(n)]
    return jnp.concatenate(rows, axis=2)


def _diag_blocks(t, n):
    r, c = t.shape[-2] // n, t.shape[-1] // n
    return jnp.stack([t[:, :, j * r:(j + 1) * r, j * c:(j + 1) * c] for j in range(n)], axis=2)


def _pad_rows(w, first_row):
    out = jnp.zeros((LORA_PAD, w.shape[1]), F32)
    return lax.dynamic_update_slice(out, w, (first_row, 0)).astype(BF16)


def _layer_params(l, w_in, mu_shift, w0, w_decay_up, a0, w_a_up, w_g_up, k_k, k_a, r_k, lnx_w, lnx_b,
                  w_gla_gate_up, b_gla_gate, gla_norm_g, w_out, norm2_g,
                  w_router_group, b_router_group, w_router_expert, b_router_expert):
    w_in_p = _win_layout(w_in[l], 256)
    n_pad = ROUTE_ROWS - N_EXPERTS - N_GROUPS
    w_router = jnp.concatenate([w_router_expert[l].T, w_router_group[l].T, jnp.zeros((n_pad, D_MODEL), F32)])
    b_router = jnp.concatenate([b_router_expert[l], b_router_group[l],
                                jnp.zeros((n_pad,), F32)]).reshape(ROUTE_ROWS, 1)
    wgate = jnp.zeros((LANES, GLA_KEY_WIDTH), F32).at[:GLA_GATE_RANK].set(w_gla_gate_up[l]).astype(BF16)
    r1 = lambda a: a.reshape(1, -1)
    return dict(
        w_in=w_in_p, mu=r1(mu_shift[l]), w0=r1(w0[l]), a0=r1(a0[l]), k_k=r1(k_k[l]), k_a=r1(k_a[l]),
        r_k=r1(r_k[l]), wd=_pad_rows(w_decay_up[l], 0), wa=_pad_rows(w_a_up[l], DECAY_LORA),
        wg=_pad_rows(w_g_up[l], DECAY_LORA + AAA_LORA), lnx_w=r1(lnx_w[l]), lnx_b=r1(lnx_b[l]),
        wgate=wgate, bgate=r1(b_gla_gate[l]), gla_g=r1(gla_norm_g[l]),
        w_out=w_out[l].astype(BF16), norm2_g=r1(norm2_g[l]), w_router=w_router, b_router=b_router)


def _run_layer(x, mod, shift0, wkv0, gla0, lp, experts, final):
    bsz, seq, d = x.shape
    assert seq % CHUNK == 0
    bb = 2
    tg = min(seq, GLA_STEP_ROWS)
    consts = _consts(TOKEN_TILE_ROWS, tg)
    m = lambda j: mod[:, j:j + 1, :]
    sh1, sc1, gt1, sh2, sc2, gt2 = (m(j) for j in range(6))
    proj, tails = _in_proj(x, sc1, sh1, lp["norm1_g"], lp["w_in"], TOKEN_TILE_ROWS)
    tails = tails.reshape(bsz, -1, RWKV_PROJ)
    new_shift = tails[:, -1]
    prev = jnp.concatenate([shift0[:, None, :], tails[:, :-1]], axis=1).reshape(-1, 1, RWKV_PROJ)
    *prep, gl = _rwkv_prep(proj, prev, lp, consts, TOKEN_TILE_ROWS)
    gl = gl.reshape(bsz, seq // CHUNK, RWKV_WIDTH)
    *chunk_in, bonus, g = prep
    s0 = _heads_block_diag(wkv0.reshape(bsz, RWKV_TILES, RWKV_TILE_HEADS, RWKV_HEAD, RWKV_HEAD))
    yr, s_bd = _rwkv_chunk(chunk_in, gl[:, :, None, :], s0, bb, min(seq, RWKV_STEP_ROWS) // CHUNK)
    new_wkv = _diag_blocks(s_bd, RWKV_TILE_HEADS).reshape(bsz, RWKV_HEADS, RWKV_HEAD, RWKV_HEAD)
    t0 = _heads_block_diag(jnp.swapaxes(gla0, -1, -2).reshape(bsz, GLA_PAIRS, 2, GLA_DV, GLA_DK))
    yg, t_bd = _gla(proj, lp, consts["tri_gla"], t0, bb, tg)
    new_gla = jnp.swapaxes(_diag_blocks(t_bd, 2).reshape(bsz, GLA_HEADS, GLA_DV, GLA_DK), -1, -2)
    x1, h2, route = _out_proj(yr, bonus, g, yg, x, gt1, sc2, sh2, lp, consts, TOKEN_TILE_ROWS)
    out = _moe(h2, route, x1, gt2, *final, *experts, consts["tri_up"], MOE_TILE_ROWS)
    return out, new_shift, new_wkv, new_gla


def kernel(x_prompt, x_sample, c_prompt, c_sample, state_rwkv_shift, state_rwkv_wkv, state_gla_kv, w_ada, b_ada, norm1_g, norm2_g, w_in, mu_shift, w0, w_decay_up, a0, w_a_up, w_g_up, k_k, k_a, r_k, lnx_w, lnx_b, w_gla_gate_up, b_gla_gate, gla_norm_g, w_out, w_router_group, b_router_group, w_router_expert, b_router_expert, w_expert_gate, w_expert_up, w_expert_down, w_ada_final, b_ada_final, normf_g):
    assert w_ada.shape[0] == 1, "the final norm is fused into the single layer's MoE kernel"
    bp, bs = x_prompt.shape[0], x_sample.shape[0]
    d = D_MODEL
    n_rows = -(-(bp + bs) // 8) * 8
    c_all = jnp.zeros((n_rows, d), F32).at[:bp].set(c_prompt).at[bp:bp + bs].set(c_sample)
    modf = _modulation(c_all, w_ada_final, b_ada_final, 1024).reshape(n_rows, 2, d)
    mod = _modulation(c_all, w_ada, b_ada[0], 1536, layer=0).reshape(n_rows, 6, d)
    lp = _layer_params(0, w_in, mu_shift, w0, w_decay_up, a0, w_a_up, w_g_up, k_k, k_a, r_k, lnx_w,
                       lnx_b, w_gla_gate_up, b_gla_gate, gla_norm_g, w_out, norm2_g,
                       w_router_group, b_router_group, w_router_expert, b_router_expert)
    lp["norm1_g"] = norm1_g[0].reshape(1, d)
    experts = (w_expert_gate[0].astype(BF16), w_expert_up[0].astype(BF16), w_expert_down[0].astype(BF16))
    groups = [
        (x_prompt, 0, bp, jnp.zeros((bp, RWKV_PROJ), F32),
         jnp.zeros((bp, RWKV_HEADS, RWKV_HEAD, RWKV_HEAD), F32), jnp.zeros((bp, GLA_HEADS, GLA_DK, GLA_DV), F32)),
        (x_sample, bp, bp + bs, state_rwkv_shift[0], state_rwkv_wkv[0], state_gla_kv[0]),
    ]
    ys, states = [], []
    for x, lo, hi, shift0, wkv0, gla0 in groups:
        final = (modf[lo:hi, 1:2], modf[lo:hi, 0:1], normf_g.reshape(1, d))
        y, *st = _run_layer(x, mod[lo:hi], shift0, wkv0, gla0, lp, experts, final)
        ys.append(y)
        states.extend(s[None] for s in st)
    return tuple(ys + states)
```

```python
import functools

import jax
import jax.numpy as jnp
from jax import lax
from jax.experimental import pallas as pl
from jax.experimental.pallas import tpu as pltpu

F32 = jnp.float32
BF16 = jnp.bfloat16

LANES = 128
VMEM_LIMIT_BYTES = 56 * 1024 * 1024
TOKEN_TILE_ROWS = 512
MOE_TILE_ROWS = 1024
MOE_SUB_ROWS = 256
MOE_GATHER_ROWS = 256
GLA_STEP_ROWS = 256
RWKV_STEP_ROWS = 256

D_MODEL = 1024
CHUNK = 64
RWKV_WIDTH = 512
RWKV_HEAD = 64
RWKV_HEADS = RWKV_WIDTH // RWKV_HEAD
HEAD_LANES = RWKV_HEAD
HEAD_SHIFT = HEAD_LANES.bit_length() - 1
assert 1 << HEAD_SHIFT == HEAD_LANES
RWKV_TILE_LANES = 128
RWKV_TILE_HEADS = RWKV_TILE_LANES // RWKV_HEAD
RWKV_TILES = RWKV_WIDTH // RWKV_TILE_LANES
DECAY_LORA = 32
AAA_LORA = 32
GATE_LORA = 64
LORA_PAD = DECAY_LORA + AAA_LORA + GATE_LORA
RWKV_PROJ = 3 * RWKV_WIDTH + LORA_PAD
GLA_WIDTH = 512
GLA_HEADS = 4
GLA_PAIRS = GLA_HEADS // 2
GLA_DV = GLA_WIDTH // GLA_HEADS
GLA_DK = GLA_DV // 2
GLA_KEY_WIDTH = GLA_HEADS * GLA_DK
GLA_GATE_RANK = 16
GLA_TAU = 16.0
GLA_PROJ = 2 * GLA_KEY_WIDTH + 2 * GLA_WIDTH + GLA_GATE_RANK
GLA_PROJ_PAD = RWKV_PROJ
IN_PROJ_PAD = RWKV_PROJ + GLA_PROJ_PAD
N_GROUPS = 4
EXPERTS_PER_GROUP = 4
N_EXPERTS = N_GROUPS * EXPERTS_PER_GROUP
GROUP_SHIFT = EXPERTS_PER_GROUP.bit_length() - 1
assert 1 << GROUP_SHIFT == EXPERTS_PER_GROUP
ROUTE_ROWS = 24
D_EXPERT = 512
RMS_EPS = 1e-6
LNX_EPS = 64e-5
LOG2_E = 1.4426950408889634

_NN = (((1,), (0,)), ((), ()))
_NT = (((1,), (1,)), ((), ()))
_TN = (((0,), (0,)), ((), ()))


def _dg(a, b, dims=_NN):
    return lax.dot_general(a, b, dims, preferred_element_type=F32)


def _split2(x):
    hi = x.astype(BF16)
    lo = (x - hi.astype(F32)).astype(BF16)
    return hi, lo


def _split3(x):
    hi = x.astype(BF16)
    r1 = x - hi.astype(F32)
    mid = r1.astype(BF16)
    lo = (r1 - mid.astype(F32)).astype(BF16)
    return hi, mid, lo


def _mm1(a, b, dims=_NN):
    return _dg(a.astype(BF16), b.astype(BF16), dims)


def _mm3(a, b, dims=_NN):
    ah, al = _split2(a)
    bh, bl = _split2(b)
    return _dg(ah, bh, dims) + (_dg(ah, bl, dims) + _dg(al, bh, dims))


def _mm_exact_lhs(e, x, dims=_NN):
    h, m, l = _split3(x)
    return _dg(e, h, dims) + (_dg(e, m, dims) + _dg(e, l, dims))


def _mm2_exact_rhs(x, e, dims=_NN):
    h, l = _split2(x)
    return _dg(h, e, dims) + _dg(l, e, dims)


def _softplus(z):
    return jnp.maximum(z, 0.0) + jnp.log(1.0 + jnp.exp(-jnp.abs(z)))


def _sigmoid(z):
    return 1.0 / (1.0 + jnp.exp(-z))


def _silu(z):
    return z * _sigmoid(z)


def _params(*sem):
    return pltpu.CompilerParams(dimension_semantics=sem, vmem_limit_bytes=VMEM_LIMIT_BYTES)


def _mod_kernel(c_ref, w_ref, b_ref, o_ref):
    o_ref[...] = _mm1(_silu(c_ref[...]), w_ref[...]) + b_ref[...]


def _modulation(c, w, b, tn, layer=None):
    rows, d = c.shape
    n = w.shape[-1]
    if layer is None:
        w_spec = pl.BlockSpec((d, tn), lambda j: (0, j))
    else:
        w_spec = pl.BlockSpec((None, d, tn), lambda j: (layer, 0, j))
    return pl.pallas_call(
        _mod_kernel,
        grid=(n // tn,),
        in_specs=[pl.BlockSpec((rows, d), lambda j: (0, 0)), w_spec,
                  pl.BlockSpec((1, tn), lambda j: (0, j))],
        out_specs=pl.BlockSpec((rows, tn), lambda j: (0, j)),
        out_shape=jax.ShapeDtypeStruct((rows, n), F32),
        compiler_params=_params("parallel"),
        name="modulation",
    )(c, w, b.reshape(1, n))


def _win_layout_kernel(w_ref, o_ref):
    qkv_end = RWKV_PROJ + 2 * GLA_KEY_WIDTH + GLA_WIDTH
    gate_end = qkv_end + GLA_GATE_RANK
    o_ref[:qkv_end] = w_ref[:qkv_end].astype(BF16)
    o_ref[qkv_end:qkv_end + GLA_WIDTH] = w_ref[gate_end:gate_end + GLA_WIDTH].astype(BF16)
    o_ref[qkv_end + GLA_WIDTH:RWKV_PROJ + GLA_PROJ] = w_ref[qkv_end:gate_end].astype(BF16)
    o_ref[RWKV_PROJ + GLA_PROJ:] = jnp.zeros((IN_PROJ_PAD - RWKV_PROJ - GLA_PROJ, o_ref.shape[1]), BF16)


def _win_layout(wt, cols):
    n, d = wt.shape
    return pl.pallas_call(
        _win_layout_kernel,
        grid=(d // cols,),
        in_specs=[pl.BlockSpec((n, cols), lambda i: (0, i))],
        out_specs=pl.BlockSpec((IN_PROJ_PAD, cols), lambda i: (0, i)),
        out_shape=jax.ShapeDtypeStruct((IN_PROJ_PAD, d), BF16),
        compiler_params=_params("parallel"),
        name="w_in_layout",
    )(wt)


def _rms_mod(x, g, sc, sh):
    ms = jnp.mean(x * x, axis=-1, keepdims=True)
    return (x * lax.rsqrt(ms + RMS_EPS) * g) * (1.0 + sc) + sh


def _token_tile(bsz, seq, rows):
    if seq >= rows:
        assert seq % rows == 0
        return 1, rows
    nb = min(bsz, rows // seq)
    assert bsz % nb == 0
    return nb, seq


def _inproj_kernel(x_ref, sc_ref, sh_ref, g_ref, w_ref, o_ref, last_ref, *, n_step):
    nb, rb, d = x_ref.shape
    h = _rms_mod(x_ref[...], g_ref[...], sc_ref[...], sh_ref[...])
    hb = h.reshape(nb * rb, d).astype(BF16)
    for j in range(IN_PROJ_PAD // n_step):
        cols = slice(j * n_step, (j + 1) * n_step)
        o_ref[:, :, cols] = _dg(hb, w_ref[cols, :], _NT).reshape(nb, rb, n_step)
    last_ref[...] = o_ref[:, rb - 1:rb, :RWKV_PROJ]


def _in_proj(x, sc, sh, g, w, rows):
    bsz, seq, d = x.shape
    nb, rb = _token_tile(bsz, seq, rows)
    n_seq_tiles = seq // rb
    vec = pl.BlockSpec((nb, 1, d), lambda b, i: (b, 0, 0))
    return pl.pallas_call(
        functools.partial(_inproj_kernel, n_step=2 * LANES),
        grid=(bsz // nb, n_seq_tiles),
        in_specs=[pl.BlockSpec((nb, rb, d), lambda b, i: (b, i, 0)), vec, vec,
                  pl.BlockSpec((1, d), lambda b, i: (0, 0)),
                  pl.BlockSpec((IN_PROJ_PAD, d), lambda b, i: (0, 0))],
        out_specs=[pl.BlockSpec((nb, rb, IN_PROJ_PAD), lambda b, i: (b, i, 0)),
                   pl.BlockSpec((nb, 1, RWKV_PROJ), lambda b, i: (b * n_seq_tiles + i, 0, 0))],
        out_shape=[jax.ShapeDtypeStruct((bsz, seq, IN_PROJ_PAD), F32),
                   jax.ShapeDtypeStruct((bsz * n_seq_tiles, 1, RWKV_PROJ), F32)],
        compiler_params=_params("parallel", "parallel"),
        name="norm1_in_proj",
    )(x, sc, sh, g, w)


def _rwkv_prep_kernel(p_ref, prev_ref, mu_ref, w0_ref, a0_ref, kk_ref, ka_ref, rk_ref,
                      wd_ref, wa_ref, wg_ref, seg_ref, tri_ref,
                      ab_ref, rkv_ref, bg_ref, gl_ref):
    nb, rb, wp = p_ref.shape
    tr = nb * rb
    p = p_ref[...].reshape(tr, wp)
    row = lax.broadcasted_iota(jnp.int32, (nb, rb, wp), 1)
    xx = jnp.where(row == 0, prev_ref[...], pltpu.roll(p, 1, 0).reshape(nb, rb, wp)).reshape(tr, wp)
    ps = p + (xx - p) * mu_ref[...]
    w = RWKV_WIDTH
    r, k, v, lora = ps[:, :w], ps[:, w:2 * w], ps[:, 2 * w:3 * w], ps[:, 3 * w:]
    logw = -_softplus(-(w0_ref[...] + _mm1(jnp.tanh(lora), wd_ref[...]))) - 0.5
    lw = jnp.exp(logw) * (-LOG2_E)
    a = _sigmoid(a0_ref[...] + _mm1(lora, wa_ref[...]))
    g = _mm1(_sigmoid(lora), wg_ref[...])
    seg = seg_ref[...]
    kk = k * kk_ref[...]
    kk = kk / jnp.maximum(jnp.sqrt(_mm2_exact_rhs(kk * kk, seg)), 1e-12)
    k2 = k * (1.0 + (a - 1.0) * ka_ref[...])
    cum = _mm_exact_lhs(tri_ref[...], lw)
    lasts = [jnp.exp2(cum[c * CHUNK + CHUNK - 1:(c + 1) * CHUNK, :]) for c in range(tr // CHUNK)]
    tot = jnp.concatenate([jnp.broadcast_to(l, (CHUNK, w)) for l in lasts], axis=0)
    for c, l in enumerate(lasts):
        gl_ref[c:c + 1, :] = l
    kka = kk * a
    ginv = jnp.exp2(-cum)
    gend = tot * ginv

    def put(ref, slot, val):
        ref[:, :, slot * w:(slot + 1) * w] = val.reshape(nb, rb, w).astype(ref.dtype)

    put(ab_ref, 0, -kk * jnp.exp2(cum - lw))
    put(ab_ref, 1, kka * ginv)
    put(rkv_ref, 0, r * jnp.exp2(cum))
    put(rkv_ref, 1, k2 * ginv)
    put(rkv_ref, 2, kka * gend)
    put(rkv_ref, 3, k2 * gend)
    put(rkv_ref, 4, v)
    put(bg_ref, 0, _mm2_exact_rhs(r * k2 * rk_ref[...], seg) * v)
    put(bg_ref, 1, g)


def _rwkv_prep(proj, prev, lp, consts, rows):
    bsz, seq, _ = proj.shape
    w = RWKV_WIDTH
    nb, rb = _token_tile(bsz, seq, rows)
    n_seq_tiles = seq // rb
    flat = lambda b, i: (b * n_seq_tiles + i, 0)
    row = lambda n: pl.BlockSpec((1, n), lambda b, i: (0, 0))
    full = lambda a: pl.BlockSpec(a.shape, lambda b, i: (0,) * a.ndim)
    packed = [(2, F32), (5, BF16), (2, F32)]
    return pl.pallas_call(
        _rwkv_prep_kernel,
        grid=(bsz // nb, n_seq_tiles),
        in_specs=[pl.BlockSpec((nb, rb, RWKV_PROJ), lambda b, i: (b, i, 0)),
                  pl.BlockSpec((nb, 1, RWKV_PROJ), lambda b, i: flat(b, i) + (0,)),
                  row(RWKV_PROJ), row(w), row(w), row(w), row(w), row(w),
                  full(lp["wd"]), full(lp["wa"]), full(lp["wg"]), full(consts["seg"]), full(consts["tri"])],
        out_specs=[pl.BlockSpec((nb, rb, k * w), lambda b, i: (b, i, 0)) for k, _ in packed]
        + [pl.BlockSpec((nb * rb // CHUNK, w), flat)],
        out_shape=[jax.ShapeDtypeStruct((bsz, seq, k * w), dt) for k, dt in packed]
        + [jax.ShapeDtypeStruct((bsz * seq // CHUNK, w), F32)],
        compiler_params=_params("parallel", "parallel"),
        name="rwkv_prep",
    )(proj, prev, lp["mu"], lp["w0"], lp["a0"], lp["k_k"], lp["k_a"], lp["r_k"],
      lp["wd"], lp["wa"], lp["wg"], consts["seg"], consts["tri"])


def _head_masks(width):
    lane = lax.broadcasted_iota(jnp.int32, (CHUNK, width), 1)
    row = lax.broadcasted_iota(jnp.int32, (CHUNK, width), 0)
    return lane >> HEAD_SHIFT, row, lane & (HEAD_LANES - 1)


def _block_diag(x, head):
    z = jnp.zeros_like(x)
    return jnp.concatenate([jnp.where(head == j, x, z) for j in range(x.shape[1] // HEAD_LANES)], axis=0)


def _rwkv_chunk_kernel(ab_ref, rkv_ref, gl_ref, s0_ref, y_ref, s_ref, *, bb, nck):
    @pl.when(pl.program_id(1) == 0)
    def _():
        s_ref[...] = s0_ref[...]

    tw = RWKV_TILE_LANES
    head, row, col = _head_masks(tw)
    strict = col < row
    incl = col <= row
    same8 = (col >> 3) == (row >> 3)
    lane2 = lax.broadcasted_iota(jnp.int32, (tw, tw), 1)
    row2 = lax.broadcasted_iota(jnp.int32, (tw, tw), 0)
    same_head = (lane2 >> HEAD_SHIFT) == (row2 >> HEAD_SHIFT)
    bd = functools.partial(_block_diag, head=head)
    c = CHUNK

    def pmm(p, q):
        return _dg(p.astype(BF16), bd(q.astype(BF16)))

    items = [(b, ck, pr) for b in range(bb) for ck in range(nck) for pr in range(RWKV_TILES)]
    n = range(len(items))
    rows = lambda ck: slice(ck * c, (ck + 1) * c)
    lanes = lambda pr: slice(pr * tw, (pr + 1) * tw)
    w = RWKV_WIDTH

    def ld(ref, slot):
        return [ref[b, rows(ck), slot * w + pr * tw:slot * w + (pr + 1) * tw] for b, ck, pr in items]

    cat0 = lambda *xs: jnp.concatenate(xs, axis=0)
    cat1 = lambda *xs: jnp.concatenate(xs, axis=1)
    at, bt = ld(ab_ref, 0), ld(ab_ref, 1)
    rt, kt, be, ke, v = (ld(rkv_ref, slot) for slot in range(5))
    ats, bts = [_split2(a) for a in at], [_split2(b) for b in bt]
    atb = [hi for hi, _ in ats]
    zero = jnp.zeros((c, tw), F32)
    gk = [_dg(cat0(atb[i], rt[i]), bd(kt[i]), _NT) for i in n]
    aak = [jnp.where(strict, gk[i][:c], zero) for i in n]
    ark = [jnp.where(incl, gk[i][c:], zero) for i in n]
    gb = [_dg(cat0(ats[i][0], ats[i][1], rt[i]), bd(bts[i][0]), _NT) for i in n]
    arb = [jnp.where(incl, gb[i][2 * c:], zero) for i in n]
    aab = [jnp.where(strict, gb[i][:c] + (gb[i][c:2 * c] + _dg(atb[i], bd(bts[i][1]), _NT)), zero) for i in n]
    z = [pmm(aak[i], v[i]) for i in n]
    a8 = [jnp.where(same8, aab[i], zero) for i in n]
    p2 = [pmm(a8[i], a8[i]) for i in n]
    p4 = [pmm(p2[i], p2[i]) for i in n]
    nn = [a8[i] + p2[i] + pmm(p2[i], a8[i]) for i in n]
    nn = [nn[i] + p4[i] + pmm(p4[i], nn[i]) for i in n]
    for lvl in (3, 4, 5):
        joins = ((col >> (lvl + 1)) == (row >> (lvl + 1))) & ((col >> lvl) != (row >> lvl))
        e = [jnp.where(joins, aab[i], zero) for i in n]
        te = [e[i] + pmm(nn[i], e[i]) for i in n]
        nn = [nn[i] + te[i] + pmm(te[i], nn[i]) for i in n]
    wu = [cat1(at[i], z[i]) + _dg(nn[i].astype(BF16), cat1(bd(atb[i]), bd(z[i].astype(BF16)))) for i in n]
    abk = [cat1(arb[i], ark[i]).astype(BF16) for i in n]
    s = {(b, pr): s_ref[b, pr] for b in range(bb) for pr in range(RWKV_TILES)}
    for ck in range(nck):
        cur = [i for i in n if items[i][1] == ck]
        key = lambda i: (items[i][0], items[i][2])
        x = {i: _dg(cat0(wu[i][:, :tw].astype(BF16), rt[i]), s[key(i)].astype(BF16), _NT) for i in cur}
        ub = {i: (x[i][:c] + wu[i][:, tw:]).astype(BF16) for i in cur}
        upd = {i: _dg(cat0(ub[i], v[i]), cat0(be[i], ke[i]), _TN) for i in cur}
        for i in cur:
            b, _, pr = items[i]
            y_ref[b, rows(ck), lanes(pr)] = x[i][c:] + _dg(abk[i], cat0(bd(ub[i]), bd(v[i])))
            s[b, pr] = (s[b, pr] * gl_ref[b, ck, :, lanes(pr)]
                        + jnp.where(same_head, upd[i], jnp.zeros_like(upd[i])))
    for (b, pr), val in s.items():
        s_ref[b, pr] = val


def _rwkv_chunk(ab, rkv, gl, s0, bb, nck):
    bsz, seq, _ = ab.shape
    w = RWKV_WIDTH
    tok = lambda a: pl.BlockSpec((bb, nck * CHUNK, a.shape[-1]), lambda i, c: (i, c, 0))
    st = pl.BlockSpec((bb, RWKV_TILES, RWKV_TILE_LANES, RWKV_TILE_LANES), lambda i, c: (i, 0, 0, 0))
    return pl.pallas_call(
        functools.partial(_rwkv_chunk_kernel, bb=bb, nck=nck),
        grid=(bsz // bb, seq // (nck * CHUNK)),
        in_specs=[tok(ab), tok(rkv), pl.BlockSpec((bb, nck, 1, w), lambda i, c: (i, c, 0, 0)), st],
        out_specs=[pl.BlockSpec((bb, nck * CHUNK, w), lambda i, c: (i, c, 0)), st],
        out_shape=[jax.ShapeDtypeStruct((bsz, seq, w), F32),
                   jax.ShapeDtypeStruct((bsz, RWKV_TILES, RWKV_TILE_LANES, RWKV_TILE_LANES), F32)],
        compiler_params=_params("parallel", "arbitrary"),
        name="rwkv_chunk",
    )(ab, rkv, gl, s0)


def _gla_kernel(p_ref, wgate_ref, bgate_ref, tri_ref, ng_ref, s0_ref, y_ref, s_ref, *, bb, nck):
    @pl.when(pl.program_id(1) == 0)
    def _():
        s_ref[...] = s0_ref[...]

    head, row, col = _head_masks(LANES)
    incl = col <= row
    bd = functools.partial(_block_diag, head=head)
    lane2 = lax.broadcasted_iota(jnp.int32, (2 * GLA_DV, LANES), 1)
    row2 = lax.broadcasted_iota(jnp.int32, (2 * GLA_DV, LANES), 0)
    same_head = (lane2 < GLA_DK) == (row2 < GLA_DV)
    kw, gw = GLA_KEY_WIDTH, GLA_WIDTH
    zero = jnp.zeros((CHUNK, LANES), F32)
    cat0 = lambda xs: jnp.concatenate(xs, axis=0)
    cat1 = lambda xs: jnp.concatenate(xs, axis=1)
    rows = lambda c: slice(c * CHUNK, (c + 1) * CHUNK)
    lanes = lambda pr: slice(pr * LANES, (pr + 1) * LANES)
    bs, cs = range(bb), range(nck)
    x = [p_ref[b] for b in bs]
    gate = [_mm1(x[b][:, 2 * kw + 2 * gw:], wgate_ref[...]) + bgate_ref[...] for b in bs]
    la = [(jnp.minimum(g, 0.0) - jnp.log(1.0 + jnp.exp(-jnp.abs(g)))) / GLA_TAU for g in gate]
    cum = [_mm_exact_lhs(tri_ref[...], la[b]) for b in bs]
    q_dec = [(x[b][:, :kw] * (GLA_DK ** -0.5)) * jnp.exp(cum[b]) for b in bs]
    k_inv = [x[b][:, kw:2 * kw] * jnp.exp(-cum[b]) for b in bs]
    last = [[cum[b][c * CHUNK + CHUNK - 1:(c + 1) * CHUNK, :] for c in cs] for b in bs]
    k_end = [[x[b][rows(c), kw:2 * kw] * jnp.exp(last[b][c] - cum[b][rows(c)]) for c in cs] for b in bs]
    dec = [[jnp.exp(last[b][c]) for c in cs] for b in bs]
    items = [(b, c, pr) for b in bs for c in cs for pr in range(GLA_PAIRS)]
    vp = {(b, c, pr): x[b][rows(c), 2 * kw + pr * 2 * GLA_DV:2 * kw + (pr + 1) * 2 * GLA_DV]
          for b, c, pr in items}
    att = {(b, c, pr): jnp.where(incl, _mm1(q_dec[b][rows(c), lanes(pr)],
                                            bd(k_inv[b][rows(c), lanes(pr)]), _NT), zero)
           for b, c, pr in items}
    upd = {it: _mm1(vp[it], k_end[it[0]][it[1]][:, lanes(it[2])], _TN) for it in items}
    intra = {it: _mm1(att[it], cat0([cat1([vp[it][:, :GLA_DV], zero]), cat1([zero, vp[it][:, GLA_DV:]])]))
             for it in items}
    st = {}
    for b in bs:
        for pr in range(GLA_PAIRS):
            cur = s_ref[b, pr]
            for c in cs:
                st[b, c, pr] = cur
                cur = cur * dec[b][c][:, lanes(pr)] + jnp.where(same_head, upd[b, c, pr], jnp.zeros_like(cur))
            s_ref[b, pr] = cur
    o = {(b, c, pr): intra[b, c, pr] + _mm1(q_dec[b][rows(c), lanes(pr)], st[b, c, pr], _NT)
         for b, c, pr in items}
    for b in bs:
        ob = cat0([cat1([o[b, c, pr] for pr in range(GLA_PAIRS)]) for c in cs])
        heads = [ob[:, h * GLA_DV:(h + 1) * GLA_DV] for h in range(GLA_HEADS)]
        normed = [oh * lax.rsqrt(jnp.mean(oh * oh, axis=-1, keepdims=True) + RMS_EPS) for oh in heads]
        y_ref[b] = cat1(normed) * ng_ref[...] * _silu(x[b][:, 2 * kw + gw:2 * kw + 2 * gw])


def _gla(proj, lp, tri, s0, bb, tg):
    bsz, seq, _ = proj.shape
    full = lambda a: pl.BlockSpec(a.shape, lambda i, c: (0,) * a.ndim)
    st = pl.BlockSpec((bb, GLA_PAIRS, 2 * GLA_DV, LANES), lambda i, c: (i, 0, 0, 0))
    return pl.pallas_call(
        functools.partial(_gla_kernel, bb=bb, nck=tg // CHUNK),
        grid=(bsz // bb, seq // tg),
        in_specs=[pl.BlockSpec((bb, tg, GLA_PROJ_PAD), lambda i, c: (i, c, 1)),
                  full(lp["wgate"]), full(lp["bgate"]), full(tri), full(lp["gla_g"]), st],
        out_specs=[pl.BlockSpec((bb, tg, GLA_WIDTH), lambda i, c: (i, c, 0)), st],
        out_shape=[jax.ShapeDtypeStruct((bsz, seq, GLA_WIDTH), F32),
                   jax.ShapeDtypeStruct((bsz, GLA_PAIRS, 2 * GLA_DV, LANES), F32)],
        compiler_params=_params("parallel", "arbitrary"),
        name="gla_chunk",
    )(proj, lp["wgate"], lp["bgate"], tri, lp["gla_g"], s0)


def _route(lgt):
    tm = lgt.shape[1]
    n_pad = ROUTE_ROWS - N_EXPERTS
    rowg = lax.broadcasted_iota(jnp.int32, (n_pad, tm), 0)
    rowe = lax.broadcasted_iota(jnp.int32, (N_EXPERTS, tm), 0)
    rowg_f, rowe_f = rowg.astype(F32), rowe.astype(F32)

    def first_argmax(vals, mx, rows_f):
        return jnp.min(jnp.where(vals == mx, rows_f, float(ROUTE_ROWS)), axis=0, keepdims=True)

    is_group = rowg < N_GROUPS
    lg = jnp.where(is_group, lgt[N_EXPERTS:], -jnp.inf)
    gmax = jnp.max(lg, axis=0, keepdims=True)
    gi = first_argmax(lg, gmax, rowg_f)
    pg_top = 1.0 / jnp.sum(jnp.where(is_group, jnp.exp(lg - gmax), 0.0), axis=0, keepdims=True)
    in_group = (rowe >> GROUP_SHIFT).astype(F32) == gi
    le = jnp.where(in_group, lgt[:N_EXPERTS], -jnp.inf)
    m1 = jnp.max(le, axis=0, keepdims=True)
    i1 = first_argmax(le, m1, rowe_f)
    le2 = jnp.where(rowe_f == i1, -jnp.inf, le)
    m2 = jnp.max(le2, axis=0, keepdims=True)
    i2 = first_argmax(le2, m2, rowe_f)
    p2 = jnp.exp(m2 - m1)
    w1 = pg_top / (1.0 + p2)
    w2 = pg_top * p2 / (1.0 + p2)
    comb = jnp.where(rowe_f == i1, w1, 0.0) + jnp.where(rowe_f == i2, w2, 0.0)
    return jnp.concatenate([comb, jnp.where(rowg == 0, gi, 0.0)], axis=0)


def _outproj_kernel(yr_ref, bg_ref, yg_ref, x_ref, gt_ref, sc_ref, sh_ref,
                    lnw_ref, lnb_ref, seg_ref, wout_ref, g2_ref, wr_ref, br_ref,
                    x1_ref, h2_ref, route_ref):
    nb, rb, d_model = x_ref.shape
    tm = nb * rb
    flat = lambda ref: ref[...].reshape(tm, ref.shape[-1])
    seg = seg_ref[...]
    y = flat(yr_ref)
    inv_n = 1.0 / RWKV_HEAD
    d = y - _mm2_exact_rhs(y, seg) * inv_n
    var = _mm2_exact_rhs(d * d, seg) * inv_n
    yn = d * lax.rsqrt(var + LNX_EPS) * lnw_ref[...] + lnb_ref[...]
    bg = flat(bg_ref)
    yr = (yn + bg[:, :RWKV_WIDTH]) * bg[:, RWKV_WIDTH:]
    mix = jnp.concatenate([yr, flat(yg_ref)], axis=1)
    x1 = x_ref[...] + gt_ref[...] * _mm1(mix, wout_ref[...]).reshape(nb, rb, d_model)
    x1_ref[...] = x1
    h2 = _rms_mod(x1, g2_ref[...], sc_ref[...], sh_ref[...])
    h2_ref[...] = h2.astype(BF16)
    route_ref[...] = _route(_mm3(wr_ref[...], h2.reshape(tm, d_model), _NT) + br_ref[...])


def _out_proj(yr, bg, yg, x, gt, sc, sh, lp, consts, rows):
    bsz, seq, d = x.shape
    nb, rb = _token_tile(bsz, seq, rows)
    n_seq_tiles = seq // rb
    half = pl.BlockSpec((nb, rb, RWKV_WIDTH), lambda b, i: (b, i, 0))
    tokd = pl.BlockSpec((nb, rb, d), lambda b, i: (b, i, 0))
    vec = pl.BlockSpec((nb, 1, d), lambda b, i: (b, 0, 0))
    full = lambda a: pl.BlockSpec(a.shape, lambda b, i: (0,) * a.ndim)
    args = (lp["lnx_w"], lp["lnx_b"], consts["seg"], lp["w_out"], lp["norm2_g"], lp["w_router"], lp["b_router"])
    return pl.pallas_call(
        _outproj_kernel,
        grid=(bsz // nb, seq // rb),
        in_specs=[half, pl.BlockSpec((nb, rb, 2 * RWKV_WIDTH), lambda b, i: (b, i, 0)), half, tokd,
                  vec, vec, vec] + [full(a) for a in args],
        out_specs=[tokd, tokd, pl.BlockSpec((ROUTE_ROWS, nb * rb), lambda b, i: (0, b * n_seq_tiles + i))],
        out_shape=[jax.ShapeDtypeStruct((bsz, seq, d), F32),
                   jax.ShapeDtypeStruct((bsz, seq, d), BF16),
                   jax.ShapeDtypeStruct((ROUTE_ROWS, bsz * seq), F32)],
        compiler_params=_params("parallel", "parallel"),
        name="out_proj_router",
    )(yr, bg, yg, x, gt, sc, sh, *args)


def _moe_kernel(h_ref, route_ref, x1_ref, gt_ref, sc_ref, sh_ref, gf_ref, tri_ref, wg_ref, wu_ref, wd_ref,
                y_ref, hs_ref, cs_ref, ys_ref, pos_ref, meta_ref):
    g = pl.program_id(2)
    nb, rb, d = h_ref.shape
    tm = nb * rb
    sub, tail = MOE_SUB_ROWS, MOE_SUB_ROWS // 2
    cap = hs_ref.shape[0]
    n_lane_tiles = tm // LANES
    lane_tile = lambda k: slice(k * LANES, (k + 1) * LANES)

    @pl.when(g == 0)
    def _sort():
        row8 = lax.broadcasted_iota(jnp.int32, (8, LANES), 0).astype(F32)
        carry = jnp.zeros((8, 1), F32)
        members, ranks = [], []
        for k in range(n_lane_tiles):
            blk = jnp.where(row8 == route_ref[N_EXPERTS:N_EXPERTS + 1, lane_tile(k)], 1.0, 0.0)
            members.append(blk)
            ranks.append(_dg(blk.astype(BF16), tri_ref[...]) - blk + carry)
            carry = carry + jnp.sum(blk, axis=1, keepdims=True)
        rowc = lax.broadcasted_iota(jnp.int32, (8, 1), 0)
        lane = lax.broadcasted_iota(jnp.int32, (1, LANES), 1)
        first = jnp.zeros((1, 1), F32)
        off_col = jnp.zeros((8, 1), F32)
        meta = jnp.zeros((1, LANES), F32)
        for grp in range(N_GROUPS):
            count = jnp.sum(jnp.where(rowc == grp, carry, 0.0), axis=0, keepdims=True)
            full = jnp.floor(count * (1.0 / sub))
            rest = count - full * sub
            has_tail = jnp.where((rest > 0.0) & (rest <= float(tail)), 1.0, 0.0)
            n_full = full + jnp.where(rest > float(tail), 1.0, 0.0)
            off_col = off_col + jnp.where(rowc == grp, first, 0.0)
            meta = (meta + jnp.where(lane == grp, first, 0.0) + jnp.where(lane == N_GROUPS + grp, n_full, 0.0)
                    + jnp.where(lane == 2 * N_GROUPS + grp, has_tail, 0.0))
            first = first + n_full * sub + has_tail * tail
        meta = (meta + jnp.where(lane == 3 * N_GROUPS, first, 0.0)).astype(jnp.int32)
        for i in range(3 * N_GROUPS + 1):
            meta_ref[i] = meta[0, i]
        for k in range(n_lane_tiles):
            pos = jnp.sum(members[k] * (ranks[k] + off_col), axis=0, keepdims=True)
            pos_ref[:, lane_tile(k)] = pos.astype(jnp.int32)
        ys_ref[...] = jnp.zeros_like(ys_ref)
        h = h_ref[...].reshape(tm, d)
        comb3 = _split3(route_ref[:N_EXPERTS, :])
        for j in range(cap // MOE_GATHER_ROWS):
            @pl.when(j * MOE_GATHER_ROWS < meta_ref[3 * N_GROUPS])
            def _gather_rows(j=j):
                rows = slice(j * MOE_GATHER_ROWS, (j + 1) * MOE_GATHER_ROWS)
                prow = lax.broadcasted_iota(jnp.int32, (MOE_GATHER_ROWS, tm), 0) + j * MOE_GATHER_ROWS
                onehot = jnp.where(prow == pos_ref[...], 1.0, 0.0).astype(BF16)
                hs_ref[rows, :] = _dg(onehot, h).astype(BF16)
                cs_ref[rows, :] = (_dg(onehot, comb3[0], _NT)
                                   + (_dg(onehot, comb3[1], _NT) + _dg(onehot, comb3[2], _NT)))

    def experts(base, n_rows):
        hs = hs_ref[pl.ds(base, n_rows), :]
        cs = cs_ref[pl.ds(base, n_rows), :]
        lane = lax.broadcasted_iota(jnp.int32, cs.shape, 1)
        ys = jnp.zeros((n_rows, d), F32)
        for e in range(EXPERTS_PER_GROUP):
            hid = _silu(_dg(hs, wg_ref[e])) * _dg(hs, wu_ref[e])
            ce = jnp.sum(jnp.where(lane == g * EXPERTS_PER_GROUP + e, cs, 0.0), axis=-1, keepdims=True)
            ys = ys + ce * _dg(hid.astype(BF16), wd_ref[e])
        ys_ref[pl.ds(base, n_rows), :] = ys.astype(BF16)

    first_row = meta_ref[g]
    n_full = meta_ref[N_GROUPS + g]

    def full_sub_tile(j, carry_):
        experts(pl.multiple_of(first_row + j * sub, tail), sub)
        return carry_

    lax.fori_loop(0, n_full, full_sub_tile, 0)

    @pl.when(meta_ref[2 * N_GROUPS + g] == 1)
    def _tail():
        experts(pl.multiple_of(first_row + n_full * sub, tail), tail)

    @pl.when(g == N_GROUPS - 1)
    def _unsort():
        ys_all = ys_ref[...]
        prow = lax.broadcasted_iota(jnp.int32, (cap, LANES), 0)
        seqs = max(LANES // rb, 1)
        for k in range(n_lane_tiles):
            onehot = jnp.where(prow == pos_ref[:, lane_tile(k)], 1.0, 0.0).astype(BF16)
            moe = _dg(onehot, ys_all, _TN).reshape(seqs, LANES // seqs, d)
            b0 = k * LANES // rb
            r0 = k * LANES - b0 * rb
            bs, rs = slice(b0, b0 + seqs), slice(r0, r0 + LANES // seqs)
            x2 = x1_ref[bs, rs] + gt_ref[bs] * moe
            y_ref[bs, rs] = _rms_mod(x2, gf_ref[...], sc_ref[bs], sh_ref[bs])


def _moe(h2, route, x1, gt, sc, sh, gf, wg, wu, wd, tri, rows):
    bsz, seq, d = x1.shape
    nb, rb = _token_tile(bsz, seq, rows)
    tm = nb * rb
    assert tm % LANES == 0 and (rb % LANES == 0 or LANES % rb == 0)
    n_seq_tiles = seq // rb
    cap = tm + N_GROUPS * (MOE_SUB_ROWS // 2)
    assert cap % MOE_GATHER_ROWS == 0
    tokd = pl.BlockSpec((nb, rb, d), lambda b, i, g: (b, i, 0))
    vec = pl.BlockSpec((nb, 1, d), lambda b, i, g: (b, 0, 0))
    group_w = lambda shape: pl.BlockSpec((EXPERTS_PER_GROUP,) + shape, lambda b, i, g: (g, 0, 0))
    return pl.pallas_call(
        _moe_kernel,
        grid=(bsz // nb, n_seq_tiles, N_GROUPS),
        in_specs=[tokd, pl.BlockSpec((ROUTE_ROWS, tm), lambda b, i, g: (0, b * n_seq_tiles + i)),
                  pl.BlockSpec((nb, rb, d), lambda b, i, g: (b, i, 0), pipeline_mode=pl.Buffered(1)),
                  vec, vec, vec,
                  pl.BlockSpec((1, d), lambda b, i, g: (0, 0)),
                  pl.BlockSpec(tri.shape, lambda b, i, g: (0, 0)),
                  group_w((d, D_EXPERT)), group_w((d, D_EXPERT)), group_w((D_EXPERT, d))],
        out_specs=tokd,
        out_shape=jax.ShapeDtypeStruct((bsz, seq, d), F32),
        scratch_shapes=[pltpu.VMEM((cap, d), BF16), pltpu.VMEM((cap, N_EXPERTS), F32),
                        pltpu.VMEM((cap, d), BF16), pltpu.VMEM((1, tm), jnp.int32),
                        pltpu.SMEM((3 * N_GROUPS + 1,), jnp.int32)],
        compiler_params=_params("parallel", "parallel", "arbitrary"),
        name="moe_final_norm",
    )(h2, route, x1, gt, sc, sh, gf, tri, wg, wu, wd)


def _block_ones(n, blk, lower):
    i = jnp.arange(n)
    m = (i[:, None] // blk) == (i[None, :] // blk)
    if lower:
        m = m & (i[None, :] <= i[:, None])
    return m.astype(BF16)


def _consts(tr, tg):
    return dict(seg=_block_ones(RWKV_WIDTH, RWKV_HEAD, False), tri=_block_ones(tr, CHUNK, True),
                tri_gla=_block_ones(tg, CHUNK, True), tri_up=_block_ones(LANES, LANES, True).T)


def _heads_block_diag(s):
    n, c = s.shape[2], s.shape[-1]
    rows = [jnp.pad(s[:, :, j], ((0, 0), (0, 0), (0, 0), (j * c, (n - 1 - j) * c))) for j in range(n)]
    return jnp.concatenate(rows, axis=2)


def _diag_blocks(t, n):
    r, c = t.shape[-2] // n, t.shape[-1] // n
    return jnp.stack([t[:, :, j * r:(j + 1) * r, j * c:(j + 1) * c] for j in range(n)], axis=2)


def _pad_rows(w, first_row):
    out = jnp.zeros((LORA_PAD, w.shape[1]), F32)
    return lax.dynamic_update_slice(out, w, (first_row, 0)).astype(BF16)


def _layer_params(l, w_in, mu_shift, w0, w_decay_up, a0, w_a_up, w_g_up, k_k, k_a, r_k, lnx_w, lnx_b,
                  w_gla_gate_up, b_gla_gate, gla_norm_g, w_out, norm2_g,
                  w_router_group, b_router_group, w_router_expert, b_router_expert):
    w_in_p = _win_layout(w_in[l].T, 256)
    n_pad = ROUTE_ROWS - N_EXPERTS - N_GROUPS
    w_router = jnp.concatenate([w_router_expert[l].T, w_router_group[l].T, jnp.zeros((n_pad, D_MODEL), F32)])
    b_router = jnp.concatenate([b_router_expert[l], b_router_group[l],
                                jnp.zeros((n_pad,), F32)]).reshape(ROUTE_ROWS, 1)
    wgate = jnp.zeros((LANES, GLA_KEY_WIDTH), F32).at[:GLA_GATE_RANK].set(w_gla_gate_up[l]).astype(BF16)
    r1 = lambda a: a.reshape(1, -1)
    return dict(
        w_in=w_in_p, mu=r1(mu_shift[l]), w0=r1(w0[l]), a0=r1(a0[l]), k_k=r1(k_k[l]), k_a=r1(k_a[l]),
        r_k=r1(r_k[l]), wd=_pad_rows(w_decay_up[l], 0), wa=_pad_rows(w_a_up[l], DECAY_LORA),
        wg=_pad_rows(w_g_up[l], DECAY_LORA + AAA_LORA), lnx_w=r1(lnx_w[l]), lnx_b=r1(lnx_b[l]),
        wgate=wgate, bgate=r1(b_gla_gate[l]), gla_g=r1(gla_norm_g[l]),
        w_out=w_out[l].astype(BF16), norm2_g=r1(norm2_g[l]), w_router=w_router, b_router=b_router)


def _run_layer(x, mod, shift0, wkv0, gla0, lp, experts, final):
    bsz, seq, d = x.shape
    assert seq % CHUNK == 0
    bb = 2
    tg = min(seq, GLA_STEP_ROWS)
    consts = _consts(TOKEN_TILE_ROWS, tg)
    m = lambda j: mod[:, j:j + 1, :]
    sh1, sc1, gt1, sh2, sc2, gt2 = (m(j) for j in range(6))
    proj, tails = _in_proj(x, sc1, sh1, lp["norm1_g"], lp["w_in"], TOKEN_TILE_ROWS)
    tails = tails.reshape(bsz, -1, RWKV_PROJ)
    new_shift = tails[:, -1]
    prev = jnp.concatenate([shift0[:, None, :], tails[:, :-1]], axis=1).reshape(-1, 1, RWKV_PROJ)
    ab, rkv, bg, gl = _rwkv_prep(proj, prev, lp, consts, TOKEN_TILE_ROWS)
    gl = gl.reshape(bsz, seq // CHUNK, RWKV_WIDTH)
    s0 = _heads_block_diag(wkv0.reshape(bsz, RWKV_TILES, RWKV_TILE_HEADS, RWKV_HEAD, RWKV_HEAD))
    yr, s_bd = _rwkv_chunk(ab, rkv, gl[:, :, None, :], s0, bb, min(seq, RWKV_STEP_ROWS) // CHUNK)
    new_wkv = _diag_blocks(s_bd, RWKV_TILE_HEADS).reshape(bsz, RWKV_HEADS, RWKV_HEAD, RWKV_HEAD)
    t0 = _heads_block_diag(jnp.swapaxes(gla0, -1, -2).reshape(bsz, GLA_PAIRS, 2, GLA_DV, GLA_DK))
    yg, t_bd = _gla(proj, lp, consts["tri_gla"], t0, bb, tg)
    new_gla = jnp.swapaxes(_diag_blocks(t_bd, 2).reshape(bsz, GLA_HEADS, GLA_DV, GLA_DK), -1, -2)
    x1, h2, route = _out_proj(yr, bg, yg, x, gt1, sc2, sh2, lp, consts, TOKEN_TILE_ROWS)
    out = _moe(h2, route, x1, gt2, *final, *experts, consts["tri_up"], MOE_TILE_ROWS)
    return out, new_shift, new_wkv, new_gla


def kernel(x_prompt, x_sample, c_prompt, c_sample, state_rwkv_shift, state_rwkv_wkv, state_gla_kv, w_ada, b_ada, norm1_g, norm2_g, w_in, mu_shift, w0, w_decay_up, a0, w_a_up, w_g_up, k_k, k_a, r_k, lnx_w, lnx_b, w_gla_gate_up, b_gla_gate, gla_norm_g, w_out, w_router_group, b_router_group, w_router_expert, b_router_expert, w_expert_gate, w_expert_up, w_expert_down, w_ada_final, b_ada_final, normf_g):
    assert w_ada.shape[0] == 1, "the final norm is fused into the single layer's MoE kernel"
    bp, bs = x_prompt.shape[0], x_sample.shape[0]
    d = D_MODEL
    n_rows = -(-(bp + bs) // 8) * 8
    c_all = jnp.zeros((n_rows, d), F32).at[:bp].set(c_prompt).at[bp:bp + bs].set(c_sample)
    modf = _modulation(c_all, w_ada_final, b_ada_final, 1024).reshape(n_rows, 2, d)
    mod = _modulation(c_all, w_ada, b_ada[0], 1536, layer=0).reshape(n_rows, 6, d)
    lp = _layer_params(0, w_in, mu_shift, w0, w_decay_up, a0, w_a_up, w_g_up, k_k, k_a, r_k, lnx_w,
                       lnx_b, w_gla_gate_up, b_gla_gate, gla_norm_g, w_out, norm2_g,
                       w_router_group, b_router_group, w_router_expert, b_router_expert)
    lp["norm1_g"] = norm1_g[0].reshape(1, d)
    experts = (w_expert_gate[0].astype(BF16), w_expert_up[0].astype(BF16), w_expert_down[0].astype(BF16))
    groups = [
        (x_prompt, 0, bp, jnp.zeros((bp, RWKV_PROJ), F32),
         jnp.zeros((bp, RWKV_HEADS, RWKV_HEAD, RWKV_HEAD), F32), jnp.zeros((bp, GLA_HEADS, GLA_DK, GLA_DV), F32)),
        (x_sample, bp, bp + bs, state_rwkv_shift[0], state_rwkv_wkv[0], state_gla_kv[0]),
    ]
    ys, states = [], []
    for x, lo, hi, shift0, wkv0, gla0 in groups:
        final = (modf[lo:hi, 1:2], modf[lo:hi, 0:1], normf_g.reshape(1, d))
        y, *st = _run_layer(x, mod[lo:hi], shift0, wkv0, gla0, lp, experts, final)
        ys.append(y)
        states.extend(s[None] for s in st)
    return tuple(ys + states)
```

```python
import functools

import jax
import jax.numpy as jnp
import numpy as np
from jax import lax
from jax.experimental import pallas as pl
from jax.experimental.pallas import tpu as pltpu

F32 = jnp.float32
BF16 = jnp.bfloat16

LANES = 128
VMEM_LIMIT_BYTES = 56 * 1024 * 1024
TOKEN_TILE_ROWS = 512
MOE_TILE_ROWS = 1024
MOE_SUB_ROWS = 256
MOE_GATHER_ROWS = 256
GLA_STEP_ROWS = 512
RECURRENT_MAX_SEQS = 8
RWKV_STEP_ROWS = 256

D_MODEL = 1024
CHUNK = 64
RWKV_WIDTH = 512
RWKV_HEAD = 64
RWKV_HEADS = RWKV_WIDTH // RWKV_HEAD
HEAD_LANES = RWKV_HEAD
HEAD_SHIFT = HEAD_LANES.bit_length() - 1
assert 1 << HEAD_SHIFT == HEAD_LANES
RWKV_TILE_LANES = 128
RWKV_TILE_HEADS = RWKV_TILE_LANES // RWKV_HEAD
RWKV_TILES = RWKV_WIDTH // RWKV_TILE_LANES
DECAY_LORA = 32
AAA_LORA = 32
GATE_LORA = 64
LORA_PAD = DECAY_LORA + AAA_LORA + GATE_LORA
RWKV_PROJ = 3 * RWKV_WIDTH + LORA_PAD
GLA_WIDTH = 512
GLA_HEADS = 4
GLA_PAIRS = GLA_HEADS // 2
GLA_DV = GLA_WIDTH // GLA_HEADS
GLA_DK = GLA_DV // 2
GLA_KEY_WIDTH = GLA_HEADS * GLA_DK
GLA_GATE_RANK = 16
GLA_TAU = 16.0
GLA_PROJ = 2 * GLA_KEY_WIDTH + 2 * GLA_WIDTH + GLA_GATE_RANK
GLA_PROJ_PAD = RWKV_PROJ
IN_PROJ_PAD = RWKV_PROJ + GLA_PROJ_PAD
N_GROUPS = 4
EXPERTS_PER_GROUP = 4
N_EXPERTS = N_GROUPS * EXPERTS_PER_GROUP
GROUP_SHIFT = EXPERTS_PER_GROUP.bit_length() - 1
assert 1 << GROUP_SHIFT == EXPERTS_PER_GROUP
ROUTE_ROWS = 24
D_EXPERT = 512
RMS_EPS = 1e-6
LNX_EPS = 64e-5
LOG2_E = 1.4426950408889634

_NN = (((1,), (0,)), ((), ()))
_NT = (((1,), (1,)), ((), ()))
_TN = (((0,), (0,)), ((), ()))


def _dg(a, b, dims=_NN):
    return lax.dot_general(a, b, dims, preferred_element_type=F32)


def _split2(x):
    hi = x.astype(BF16)
    lo = (x - hi.astype(F32)).astype(BF16)
    return hi, lo


def _split3(x):
    hi = x.astype(BF16)
    r1 = x - hi.astype(F32)
    mid = r1.astype(BF16)
    lo = (r1 - mid.astype(F32)).astype(BF16)
    return hi, mid, lo


def _mm1(a, b, dims=_NN):
    return _dg(a.astype(BF16), b.astype(BF16), dims)


def _mm3(a, b, dims=_NN):
    ah, al = _split2(a)
    bh, bl = _split2(b)
    return _dg(ah, bh, dims) + (_dg(ah, bl, dims) + _dg(al, bh, dims))


def _mm_exact_lhs(e, x, dims=_NN):
    h, m, l = _split3(x)
    return _dg(e, h, dims) + (_dg(e, m, dims) + _dg(e, l, dims))


def _mm2_exact_rhs(x, e, dims=_NN):
    h, l = _split2(x)
    return _dg(h, e, dims) + _dg(l, e, dims)


def _softplus(z):
    return jnp.maximum(z, 0.0) + jnp.log(1.0 + jnp.exp(-jnp.abs(z)))


def _sigmoid(z):
    return 1.0 / (1.0 + jnp.exp(-z))


def _silu(z):
    return z * _sigmoid(z)


def _params(*sem):
    return pltpu.CompilerParams(dimension_semantics=sem, vmem_limit_bytes=VMEM_LIMIT_BYTES)


def _mod_kernel(c_ref, w_ref, b_ref, o_ref):
    o_ref[...] = _mm1(_silu(c_ref[...]), w_ref[...]) + b_ref[...]


def _modulation(c, w, b, tn, layer=None):
    rows, d = c.shape
    n = w.shape[-1]
    if layer is None:
        w_spec = pl.BlockSpec((d, tn), lambda j: (0, j))
    else:
        w_spec = pl.BlockSpec((None, d, tn), lambda j: (layer, 0, j))
    return pl.pallas_call(
        _mod_kernel,
        grid=(n // tn,),
        in_specs=[pl.BlockSpec((rows, d), lambda j: (0, 0)), w_spec,
                  pl.BlockSpec((1, tn), lambda j: (0, j))],
        out_specs=pl.BlockSpec((rows, tn), lambda j: (0, j)),
        out_shape=jax.ShapeDtypeStruct((rows, n), F32),
        compiler_params=_params("parallel"),
        name="modulation",
    )(c, w, b.reshape(1, n))


def _win_layout_kernel(w_ref, o_ref):
    qkv_end = RWKV_PROJ + 2 * GLA_KEY_WIDTH + GLA_WIDTH
    gate_end = qkv_end + GLA_GATE_RANK
    o_ref[:qkv_end] = w_ref[:qkv_end].astype(BF16)
    o_ref[qkv_end:qkv_end + GLA_WIDTH] = w_ref[gate_end:gate_end + GLA_WIDTH].astype(BF16)
    o_ref[qkv_end + GLA_WIDTH:RWKV_PROJ + GLA_PROJ] = w_ref[qkv_end:gate_end].astype(BF16)
    o_ref[RWKV_PROJ + GLA_PROJ:] = jnp.zeros((IN_PROJ_PAD - RWKV_PROJ - GLA_PROJ, o_ref.shape[1]), BF16)


def _win_layout(wt, cols):
    n, d = wt.shape
    return pl.pallas_call(
        _win_layout_kernel,
        grid=(d // cols,),
        in_specs=[pl.BlockSpec((n, cols), lambda i: (0, i))],
        out_specs=pl.BlockSpec((IN_PROJ_PAD, cols), lambda i: (0, i)),
        out_shape=jax.ShapeDtypeStruct((IN_PROJ_PAD, d), BF16),
        compiler_params=_params("parallel"),
        name="w_in_layout",
    )(wt)


def _rms_mod(x, g, sc, sh):
    ms = jnp.mean(x * x, axis=-1, keepdims=True)
    return (x * lax.rsqrt(ms + RMS_EPS) * g) * (1.0 + sc) + sh


def _token_tile(bsz, seq, rows):
    if seq >= rows:
        assert seq % rows == 0
        return 1, rows
    nb = min(bsz, rows // seq)
    assert bsz % nb == 0
    return nb, seq


def _inproj_kernel(x_ref, sc_ref, sh_ref, g_ref, w_ref, o_ref, last_ref, *, n_step):
    nb, rb, d = x_ref.shape
    h = _rms_mod(x_ref[...], g_ref[...], sc_ref[...], sh_ref[...])
    hb = h.reshape(nb * rb, d).astype(BF16)
    for j in range(IN_PROJ_PAD // n_step):
        cols = slice(j * n_step, (j + 1) * n_step)
        o_ref[:, :, cols] = _dg(hb, w_ref[cols, :], _NT).reshape(nb, rb, n_step)
    last_ref[...] = o_ref[:, rb - 1:rb, :RWKV_PROJ]


def _in_proj(x, sc, sh, g, w, rows):
    bsz, seq, d = x.shape
    nb, rb = _token_tile(bsz, seq, rows)
    n_seq_tiles = seq // rb
    vec = pl.BlockSpec((nb, 1, d), lambda b, i: (b, 0, 0))
    return pl.pallas_call(
        functools.partial(_inproj_kernel, n_step=2 * LANES),
        grid=(bsz // nb, n_seq_tiles),
        in_specs=[pl.BlockSpec((nb, rb, d), lambda b, i: (b, i, 0)), vec, vec,
                  pl.BlockSpec((1, d), lambda b, i: (0, 0)),
                  pl.BlockSpec((IN_PROJ_PAD, d), lambda b, i: (0, 0))],
        out_specs=[pl.BlockSpec((nb, rb, IN_PROJ_PAD), lambda b, i: (b, i, 0)),
                   pl.BlockSpec((nb, 1, RWKV_PROJ), lambda b, i: (b * n_seq_tiles + i, 0, 0))],
        out_shape=[jax.ShapeDtypeStruct((bsz, seq, IN_PROJ_PAD), F32),
                   jax.ShapeDtypeStruct((bsz * n_seq_tiles, 1, RWKV_PROJ), F32)],
        compiler_params=_params("parallel", "parallel"),
        name="norm1_in_proj",
    )(x, sc, sh, g, w)


def _rwkv_prep_kernel(p_ref, prev_ref, mu_ref, w0_ref, a0_ref, kk_ref, ka_ref, rk_ref,
                      wd_ref, wa_ref, wg_ref, seg_ref, tri_ref,
                      ab_ref, rkv_ref, bg_ref, gl_ref):
    nb, rb, wp = p_ref.shape
    tr = nb * rb
    p = p_ref[...].reshape(tr, wp)
    row = lax.broadcasted_iota(jnp.int32, (nb, rb, wp), 1)
    xx = jnp.where(row == 0, prev_ref[...], pltpu.roll(p, 1, 0).reshape(nb, rb, wp)).reshape(tr, wp)
    ps = p + (xx - p) * mu_ref[...]
    w = RWKV_WIDTH
    r, k, v, lora = ps[:, :w], ps[:, w:2 * w], ps[:, 2 * w:3 * w], ps[:, 3 * w:]
    logw = -_softplus(-(w0_ref[...] + _mm1(jnp.tanh(lora), wd_ref[...]))) - 0.5
    lw = jnp.exp(logw) * (-LOG2_E)
    a = _sigmoid(a0_ref[...] + _mm1(lora, wa_ref[...]))
    g = _mm1(_sigmoid(lora), wg_ref[...])
    seg = seg_ref[...]
    kk = k * kk_ref[...]
    kk = kk / jnp.maximum(jnp.sqrt(_mm2_exact_rhs(kk * kk, seg)), 1e-12)
    k2 = k * (1.0 + (a - 1.0) * ka_ref[...])
    cum = _mm_exact_lhs(tri_ref[...], lw)
    lasts = [jnp.exp2(cum[c * CHUNK + CHUNK - 1:(c + 1) * CHUNK, :]) for c in range(tr // CHUNK)]
    tot = jnp.concatenate([jnp.broadcast_to(l, (CHUNK, w)) for l in lasts], axis=0)
    for c, l in enumerate(lasts):
        gl_ref[c:c + 1, :] = l
    kka = kk * a
    ginv = jnp.exp2(-cum)
    gend = tot * ginv

    def put(ref, slot, val):
        ref[:, :, slot * w:(slot + 1) * w] = val.reshape(nb, rb, w).astype(ref.dtype)

    put(ab_ref, 0, -kk * jnp.exp2(cum - lw))
    put(ab_ref, 1, kka * ginv)
    put(rkv_ref, 0, r * jnp.exp2(cum))
    put(rkv_ref, 1, k2 * ginv)
    put(rkv_ref, 2, kka * gend)
    put(rkv_ref, 3, k2 * gend)
    put(rkv_ref, 4, v)
    put(bg_ref, 0, _mm2_exact_rhs(r * k2 * rk_ref[...], seg) * v)
    put(bg_ref, 1, g)


def _rwkv_prep(proj, prev, lp, consts, rows):
    bsz, seq, _ = proj.shape
    w = RWKV_WIDTH
    nb, rb = _token_tile(bsz, seq, rows)
    n_seq_tiles = seq // rb
    flat = lambda b, i: (b * n_seq_tiles + i, 0)
    row = lambda n: pl.BlockSpec((1, n), lambda b, i: (0, 0))
    full = lambda a: pl.BlockSpec(a.shape, lambda b, i: (0,) * a.ndim)
    packed = [(2, F32), (5, BF16), (2, F32)]
    return pl.pallas_call(
        _rwkv_prep_kernel,
        grid=(bsz // nb, n_seq_tiles),
        in_specs=[pl.BlockSpec((nb, rb, RWKV_PROJ), lambda b, i: (b, i, 0)),
                  pl.BlockSpec((nb, 1, RWKV_PROJ), lambda b, i: flat(b, i) + (0,)),
                  row(RWKV_PROJ), row(w), row(w), row(w), row(w), row(w),
                  full(lp["wd"]), full(lp["wa"]), full(lp["wg"]), full(consts["seg"]), full(consts["tri"])],
        out_specs=[pl.BlockSpec((nb, rb, k * w), lambda b, i: (b, i, 0)) for k, _ in packed]
        + [pl.BlockSpec((nb * rb // CHUNK, w), flat)],
        out_shape=[jax.ShapeDtypeStruct((bsz, seq, k * w), dt) for k, dt in packed]
        + [jax.ShapeDtypeStruct((bsz * seq // CHUNK, w), F32)],
        compiler_params=_params("parallel", "parallel"),
        name="rwkv_prep",
    )(proj, prev, lp["mu"], lp["w0"], lp["a0"], lp["k_k"], lp["k_a"], lp["r_k"],
      lp["wd"], lp["wa"], lp["wg"], consts["seg"], consts["tri"])


def _head_masks(width):
    lane = lax.broadcasted_iota(jnp.int32, (CHUNK, width), 1)
    row = lax.broadcasted_iota(jnp.int32, (CHUNK, width), 0)
    return lane >> HEAD_SHIFT, row, lane & (HEAD_LANES - 1)


def _block_diag(x, head):
    z = jnp.zeros_like(x)
    return jnp.concatenate([jnp.where(head == j, x, z) for j in range(x.shape[1] // HEAD_LANES)], axis=0)


def _rwkv_chunk_kernel(ab_ref, rkv_ref, gl_ref, s0_ref, y_ref, s_ref, *, bb, nck):
    @pl.when(pl.program_id(1) == 0)
    def _():
        s_ref[...] = s0_ref[...]

    tw = RWKV_TILE_LANES
    head, row, col = _head_masks(tw)
    strict = col < row
    incl = col <= row
    same8 = (col >> 3) == (row >> 3)
    lane2 = lax.broadcasted_iota(jnp.int32, (tw, tw), 1)
    row2 = lax.broadcasted_iota(jnp.int32, (tw, tw), 0)
    same_head = (lane2 >> HEAD_SHIFT) == (row2 >> HEAD_SHIFT)
    bd = functools.partial(_block_diag, head=head)
    c = CHUNK

    def pmm(p, q):
        return _dg(p.astype(BF16), bd(q.astype(BF16)))

    items = [(b, ck, pr) for b in range(bb) for ck in range(nck) for pr in range(RWKV_TILES)]
    n = range(len(items))
    rows = lambda ck: slice(ck * c, (ck + 1) * c)
    lanes = lambda pr: slice(pr * tw, (pr + 1) * tw)
    w = RWKV_WIDTH

    def ld(ref, slot):
        return [ref[b, rows(ck), slot * w + pr * tw:slot * w + (pr + 1) * tw] for b, ck, pr in items]

    cat0 = lambda *xs: jnp.concatenate(xs, axis=0)
    cat1 = lambda *xs: jnp.concatenate(xs, axis=1)
    at, bt = ld(ab_ref, 0), ld(ab_ref, 1)
    rt, kt, be, ke, v = (ld(rkv_ref, slot) for slot in range(5))
    ats, bts = [_split2(a) for a in at], [_split2(b) for b in bt]
    atb = [hi for hi, _ in ats]
    zero = jnp.zeros((c, tw), F32)
    gk = [_dg(cat0(atb[i], rt[i]), bd(kt[i]), _NT) for i in n]
    aak = [jnp.where(strict, gk[i][:c], zero) for i in n]
    ark = [jnp.where(incl, gk[i][c:], zero) for i in n]
    gb = [_dg(cat0(ats[i][0], ats[i][1], rt[i]), bd(bts[i][0]), _NT) for i in n]
    arb = [jnp.where(incl, gb[i][2 * c:], zero) for i in n]
    aab = [jnp.where(strict, gb[i][:c] + (gb[i][c:2 * c] + _dg(atb[i], bd(bts[i][1]), _NT)), zero) for i in n]
    z = [pmm(aak[i], v[i]) for i in n]
    a8 = [jnp.where(same8, aab[i], zero) for i in n]
    p2 = [pmm(a8[i], a8[i]) for i in n]
    p4 = [pmm(p2[i], p2[i]) for i in n]
    nn = [a8[i] + p2[i] + pmm(p2[i], a8[i]) for i in n]
    nn = [nn[i] + p4[i] + pmm(p4[i], nn[i]) for i in n]
    for lvl in (3, 4, 5):
        joins = ((col >> (lvl + 1)) == (row >> (lvl + 1))) & ((col >> lvl) != (row >> lvl))
        e = [jnp.where(joins, aab[i], zero) for i in n]
        te = [e[i] + pmm(nn[i], e[i]) for i in n]
        nn = [nn[i] + te[i] + pmm(te[i], nn[i]) for i in n]
    wu = [cat1(at[i], z[i]) + _dg(nn[i].astype(BF16), cat1(bd(atb[i]), bd(z[i].astype(BF16)))) for i in n]
    abk = [cat1(arb[i], ark[i]).astype(BF16) for i in n]
    s = {(b, pr): s_ref[b, pr] for b in range(bb) for pr in range(RWKV_TILES)}
    for ck in range(nck):
        cur = [i for i in n if items[i][1] == ck]
        key = lambda i: (items[i][0], items[i][2])
        x = {i: _dg(cat0(wu[i][:, :tw].astype(BF16), rt[i]), s[key(i)].astype(BF16), _NT) for i in cur}
        ub = {i: (x[i][:c] + wu[i][:, tw:]).astype(BF16) for i in cur}
        upd = {i: _dg(cat0(ub[i], v[i]), cat0(be[i], ke[i]), _TN) for i in cur}
        for i in cur:
            b, _, pr = items[i]
            y_ref[b, rows(ck), lanes(pr)] = x[i][c:] + _dg(abk[i], cat0(bd(ub[i]), bd(v[i])))
            s[b, pr] = (s[b, pr] * gl_ref[b, ck, :, lanes(pr)]
                        + jnp.where(same_head, upd[i], jnp.zeros_like(upd[i])))
    for (b, pr), val in s.items():
        s_ref[b, pr] = val


def _rwkv_chunk(ab, rkv, gl, s0, bb, nck):
    bsz, seq, _ = ab.shape
    w = RWKV_WIDTH
    tok = lambda a: pl.BlockSpec((bb, nck * CHUNK, a.shape[-1]), lambda i, c: (i, c, 0))
    st = pl.BlockSpec((bb, RWKV_TILES, RWKV_TILE_LANES, RWKV_TILE_LANES), lambda i, c: (i, 0, 0, 0))
    return pl.pallas_call(
        functools.partial(_rwkv_chunk_kernel, bb=bb, nck=nck),
        grid=(bsz // bb, seq // (nck * CHUNK)),
        in_specs=[tok(ab), tok(rkv), pl.BlockSpec((bb, nck, 1, w), lambda i, c: (i, c, 0, 0)), st],
        out_specs=[pl.BlockSpec((bb, nck * CHUNK, w), lambda i, c: (i, c, 0)), st],
        out_shape=[jax.ShapeDtypeStruct((bsz, seq, w), F32),
                   jax.ShapeDtypeStruct((bsz, RWKV_TILES, RWKV_TILE_LANES, RWKV_TILE_LANES), F32)],
        compiler_params=_params("parallel", "arbitrary"),
        name="rwkv_chunk",
    )(ab, rkv, gl, s0)


def _gla_kernel(p_ref, wgate_ref, bgate_ref, tri_ref, ng_ref, s0_ref, y_ref, s_ref, *, bb, nck):
    @pl.when(pl.program_id(1) == 0)
    def _():
        s_ref[...] = s0_ref[...]

    head, row, col = _head_masks(LANES)
    incl = col <= row
    bd = functools.partial(_block_diag, head=head)
    lane2 = lax.broadcasted_iota(jnp.int32, (2 * GLA_DV, LANES), 1)
    row2 = lax.broadcasted_iota(jnp.int32, (2 * GLA_DV, LANES), 0)
    same_head = (lane2 < GLA_DK) == (row2 < GLA_DV)
    kw, gw = GLA_KEY_WIDTH, GLA_WIDTH
    zero = jnp.zeros((CHUNK, LANES), F32)
    cat0 = lambda xs: jnp.concatenate(xs, axis=0)
    cat1 = lambda xs: jnp.concatenate(xs, axis=1)
    rows = lambda c: slice(c * CHUNK, (c + 1) * CHUNK)
    lanes = lambda pr: slice(pr * LANES, (pr + 1) * LANES)
    bs, cs = range(bb), range(nck)
    x = [p_ref[b] for b in bs]
    gate = [_mm1(x[b][:, 2 * kw + 2 * gw:], wgate_ref[...]) + bgate_ref[...] for b in bs]
    la = [(jnp.minimum(g, 0.0) - jnp.log(1.0 + jnp.exp(-jnp.abs(g)))) / GLA_TAU for g in gate]
    cum = [_mm_exact_lhs(tri_ref[...], la[b]) for b in bs]
    q_dec = [(x[b][:, :kw] * (GLA_DK ** -0.5)) * jnp.exp(cum[b]) for b in bs]
    k_inv = [x[b][:, kw:2 * kw] * jnp.exp(-cum[b]) for b in bs]
    last = [[cum[b][c * CHUNK + CHUNK - 1:(c + 1) * CHUNK, :] for c in cs] for b in bs]
    k_end = [[x[b][rows(c), kw:2 * kw] * jnp.exp(last[b][c] - cum[b][rows(c)]) for c in cs] for b in bs]
    dec = [[jnp.exp(last[b][c]) for c in cs] for b in bs]
    items = [(b, c, pr) for b in bs for c in cs for pr in range(GLA_PAIRS)]
    vp = {(b, c, pr): x[b][rows(c), 2 * kw + pr * 2 * GLA_DV:2 * kw + (pr + 1) * 2 * GLA_DV]
          for b, c, pr in items}
    att = {(b, c, pr): jnp.where(incl, _mm1(q_dec[b][rows(c), lanes(pr)],
                                            bd(k_inv[b][rows(c), lanes(pr)]), _NT), zero)
           for b, c, pr in items}
    upd = {it: _mm1(vp[it], k_end[it[0]][it[1]][:, lanes(it[2])], _TN) for it in items}
    intra = {it: _mm1(att[it], cat0([cat1([vp[it][:, :GLA_DV], zero]), cat1([zero, vp[it][:, GLA_DV:]])]))
             for it in items}
    st = {}
    for b in bs:
        for pr in range(GLA_PAIRS):
            cur = s_ref[b, pr]
            for c in cs:
                st[b, c, pr] = cur
                cur = cur * dec[b][c][:, lanes(pr)] + jnp.where(same_head, upd[b, c, pr], jnp.zeros_like(cur))
            s_ref[b, pr] = cur
    o = {(b, c, pr): intra[b, c, pr] + _mm1(q_dec[b][rows(c), lanes(pr)], st[b, c, pr], _NT)
         for b, c, pr in items}
    for b in bs:
        ob = cat0([cat1([o[b, c, pr] for pr in range(GLA_PAIRS)]) for c in cs])
        heads = [ob[:, h * GLA_DV:(h + 1) * GLA_DV] for h in range(GLA_HEADS)]
        normed = [oh * lax.rsqrt(jnp.mean(oh * oh, axis=-1, keepdims=True) + RMS_EPS) for oh in heads]
        y_ref[b] = cat1(normed) * ng_ref[...] * _silu(x[b][:, 2 * kw + gw:2 * kw + 2 * gw])


def _gla(proj, lp, tri, s0, bb, tg):
    bsz, seq, _ = proj.shape
    full = lambda a: pl.BlockSpec(a.shape, lambda i, c: (0,) * a.ndim)
    st = pl.BlockSpec((bb, GLA_PAIRS, 2 * GLA_DV, LANES), lambda i, c: (i, 0, 0, 0))
    return pl.pallas_call(
        functools.partial(_gla_kernel, bb=bb, nck=tg // CHUNK),
        grid=(bsz // bb, seq // tg),
        in_specs=[pl.BlockSpec((bb, tg, GLA_PROJ_PAD), lambda i, c: (i, c, 1)),
                  full(lp["wgate"]), full(lp["bgate"]), full(tri), full(lp["gla_g"]), st],
        out_specs=[pl.BlockSpec((bb, tg, GLA_WIDTH), lambda i, c: (i, c, 0)), st],
        out_shape=[jax.ShapeDtypeStruct((bsz, seq, GLA_WIDTH), F32),
                   jax.ShapeDtypeStruct((bsz, GLA_PAIRS, 2 * GLA_DV, LANES), F32)],
        compiler_params=_params("parallel", "arbitrary"),
        name="gla_chunk",
    )(proj, lp["wgate"], lp["bgate"], tri, lp["gla_g"], s0)


def _route(lgt):
    tm = lgt.shape[1]
    n_pad = ROUTE_ROWS - N_EXPERTS
    rowg = lax.broadcasted_iota(jnp.int32, (n_pad, tm), 0)
    rowe = lax.broadcasted_iota(jnp.int32, (N_EXPERTS, tm), 0)
    rowg_f, rowe_f = rowg.astype(F32), rowe.astype(F32)

    def first_argmax(vals, mx, rows_f):
        return jnp.min(jnp.where(vals == mx, rows_f, float(ROUTE_ROWS)), axis=0, keepdims=True)

    is_group = rowg < N_GROUPS
    lg = jnp.where(is_group, lgt[N_EXPERTS:], -jnp.inf)
    gmax = jnp.max(lg, axis=0, keepdims=True)
    gi = first_argmax(lg, gmax, rowg_f)
    pg_top = 1.0 / jnp.sum(jnp.where(is_group, jnp.exp(lg - gmax), 0.0), axis=0, keepdims=True)
    in_group = (rowe >> GROUP_SHIFT).astype(F32) == gi
    le = jnp.where(in_group, lgt[:N_EXPERTS], -jnp.inf)
    m1 = jnp.max(le, axis=0, keepdims=True)
    i1 = first_argmax(le, m1, rowe_f)
    le2 = jnp.where(rowe_f == i1, -jnp.inf, le)
    m2 = jnp.max(le2, axis=0, keepdims=True)
    i2 = first_argmax(le2, m2, rowe_f)
    p2 = jnp.exp(m2 - m1)
    w1 = pg_top / (1.0 + p2)
    w2 = pg_top * p2 / (1.0 + p2)
    comb = jnp.where(rowe_f == i1, w1, 0.0) + jnp.where(rowe_f == i2, w2, 0.0)
    return jnp.concatenate([comb, jnp.where(rowg == 0, gi, 0.0)], axis=0)


def _outproj_kernel(yr_ref, bg_ref, yg_ref, x_ref, gt_ref, sc_ref, sh_ref,
                    lnw_ref, lnb_ref, seg_ref, wout_ref, g2_ref, wr_ref, br_ref,
                    x1_ref, h2_ref, route_ref):
    nb, rb, d_model = x_ref.shape
    tm = nb * rb
    flat = lambda ref: ref[...].reshape(tm, ref.shape[-1])
    seg = seg_ref[...]
    y = flat(yr_ref)
    inv_n = 1.0 / RWKV_HEAD
    d = y - _mm2_exact_rhs(y, seg) * inv_n
    var = _mm2_exact_rhs(d * d, seg) * inv_n
    yn = d * lax.rsqrt(var + LNX_EPS) * lnw_ref[...] + lnb_ref[...]
    bg = flat(bg_ref)
    yr = (yn + bg[:, :RWKV_WIDTH]) * bg[:, RWKV_WIDTH:]
    mix = jnp.concatenate([yr, flat(yg_ref)], axis=1)
    x1 = x_ref[...] + gt_ref[...] * _mm1(mix, wout_ref[...]).reshape(nb, rb, d_model)
    x1_ref[...] = x1
    h2 = _rms_mod(x1, g2_ref[...], sc_ref[...], sh_ref[...])
    h2_ref[...] = h2.astype(BF16)
    route_ref[...] = _route(_mm3(wr_ref[...], h2.reshape(tm, d_model), _NT) + br_ref[...])


def _out_proj(yr, bg, yg, x, gt, sc, sh, lp, consts, rows):
    bsz, seq, d = x.shape
    nb, rb = _token_tile(bsz, seq, rows)
    n_seq_tiles = seq // rb
    half = pl.BlockSpec((nb, rb, RWKV_WIDTH), lambda b, i: (b, i, 0))
    tokd = pl.BlockSpec((nb, rb, d), lambda b, i: (b, i, 0))
    vec = pl.BlockSpec((nb, 1, d), lambda b, i: (b, 0, 0))
    full = lambda a: pl.BlockSpec(a.shape, lambda b, i: (0,) * a.ndim)
    args = (lp["lnx_w"], lp["lnx_b"], consts["seg"], lp["w_out"], lp["norm2_g"], lp["w_router"], lp["b_router"])
    return pl.pallas_call(
        _outproj_kernel,
        grid=(bsz // nb, seq // rb),
        in_specs=[half, pl.BlockSpec((nb, rb, 2 * RWKV_WIDTH), lambda b, i: (b, i, 0)), half, tokd,
                  vec, vec, vec] + [full(a) for a in args],
        out_specs=[tokd, tokd, pl.BlockSpec((ROUTE_ROWS, nb * rb), lambda b, i: (0, b * n_seq_tiles + i))],
        out_shape=[jax.ShapeDtypeStruct((bsz, seq, d), F32),
                   jax.ShapeDtypeStruct((bsz, seq, d), BF16),
                   jax.ShapeDtypeStruct((ROUTE_ROWS, bsz * seq), F32)],
        compiler_params=_params("parallel", "parallel"),
        name="out_proj_router",
    )(yr, bg, yg, x, gt, sc, sh, *args)


def _moe_kernel(h_ref, route_ref, x1_ref, gt_ref, sc_ref, sh_ref, gf_ref, tri_ref, wg_ref, wu_ref, wd_ref,
                y_ref, hs_ref, cs_ref, ys_ref, pos_ref, meta_ref):
    g = pl.program_id(2)
    nb, rb, d = h_ref.shape
    tm = nb * rb
    sub, tail = MOE_SUB_ROWS, MOE_SUB_ROWS // 2
    cap = hs_ref.shape[0]
    n_lane_tiles = tm // LANES
    lane_tile = lambda k: slice(k * LANES, (k + 1) * LANES)

    @pl.when(g == 0)
    def _sort():
        row8 = lax.broadcasted_iota(jnp.int32, (8, LANES), 0).astype(F32)
        carry = jnp.zeros((8, 1), F32)
        members, ranks = [], []
        for k in range(n_lane_tiles):
            blk = jnp.where(row8 == route_ref[N_EXPERTS:N_EXPERTS + 1, lane_tile(k)], 1.0, 0.0)
            members.append(blk)
            ranks.append(_dg(blk.astype(BF16), tri_ref[...]) - blk + carry)
            carry = carry + jnp.sum(blk, axis=1, keepdims=True)
        rowc = lax.broadcasted_iota(jnp.int32, (8, 1), 0)
        lane = lax.broadcasted_iota(jnp.int32, (1, LANES), 1)
        first = jnp.zeros((1, 1), F32)
        off_col = jnp.zeros((8, 1), F32)
        meta = jnp.zeros((1, LANES), F32)
        for grp in range(N_GROUPS):
            count = jnp.sum(jnp.where(rowc == grp, carry, 0.0), axis=0, keepdims=True)
            full = jnp.floor(count * (1.0 / sub))
            rest = count - full * sub
            has_tail = jnp.where((rest > 0.0) & (rest <= float(tail)), 1.0, 0.0)
            n_full = full + jnp.where(rest > float(tail), 1.0, 0.0)
            off_col = off_col + jnp.where(rowc == grp, first, 0.0)
            meta = (meta + jnp.where(lane == grp, first, 0.0) + jnp.where(lane == N_GROUPS + grp, n_full, 0.0)
                    + jnp.where(lane == 2 * N_GROUPS + grp, has_tail, 0.0))
            first = first + n_full * sub + has_tail * tail
        meta = (meta + jnp.where(lane == 3 * N_GROUPS, first, 0.0)).astype(jnp.int32)
        for i in range(3 * N_GROUPS + 1):
            meta_ref[i] = meta[0, i]
        for k in range(n_lane_tiles):
            pos = jnp.sum(members[k] * (ranks[k] + off_col), axis=0, keepdims=True)
            pos_ref[:, lane_tile(k)] = pos.astype(jnp.int32)
        ys_ref[...] = jnp.zeros_like(ys_ref)
        h = h_ref[...].reshape(tm, d)
        comb3 = _split3(route_ref[:N_EXPERTS, :])
        for j in range(cap // MOE_GATHER_ROWS):
            @pl.when(j * MOE_GATHER_ROWS < meta_ref[3 * N_GROUPS])
            def _gather_rows(j=j):
                rows = slice(j * MOE_GATHER_ROWS, (j + 1) * MOE_GATHER_ROWS)
                prow = lax.broadcasted_iota(jnp.int32, (MOE_GATHER_ROWS, tm), 0) + j * MOE_GATHER_ROWS
                onehot = jnp.where(prow == pos_ref[...], 1.0, 0.0).astype(BF16)
                hs_ref[rows, :] = _dg(onehot, h).astype(BF16)
                cs_ref[rows, :] = (_dg(onehot, comb3[0], _NT)
                                   + (_dg(onehot, comb3[1], _NT) + _dg(onehot, comb3[2], _NT)))

    def experts(base, n_rows):
        hs = hs_ref[pl.ds(base, n_rows), :]
        cs = cs_ref[pl.ds(base, n_rows), :]
        lane = lax.broadcasted_iota(jnp.int32, cs.shape, 1)
        ys = jnp.zeros((n_rows, d), F32)
        for e in range(EXPERTS_PER_GROUP):
            hid = _silu(_dg(hs, wg_ref[e])) * _dg(hs, wu_ref[e])
            ce = jnp.sum(jnp.where(lane == g * EXPERTS_PER_GROUP + e, cs, 0.0), axis=-1, keepdims=True)
            ys = ys + ce * _dg(hid.astype(BF16), wd_ref[e])
        ys_ref[pl.ds(base, n_rows), :] = ys.astype(BF16)

    first_row = meta_ref[g]
    n_full = meta_ref[N_GROUPS + g]

    def full_sub_tile(j, carry_):
        experts(pl.multiple_of(first_row + j * sub, tail), sub)
        return carry_

    lax.fori_loop(0, n_full, full_sub_tile, 0)

    @pl.when(meta_ref[2 * N_GROUPS + g] == 1)
    def _tail():
        experts(pl.multiple_of(first_row + n_full * sub, tail), tail)

    @pl.when(g == N_GROUPS - 1)
    def _unsort():
        ys_all = ys_ref[...]
        prow = lax.broadcasted_iota(jnp.int32, (cap, LANES), 0)
        seqs = max(LANES // rb, 1)
        for k in range(n_lane_tiles):
            onehot = jnp.where(prow == pos_ref[:, lane_tile(k)], 1.0, 0.0).astype(BF16)
            moe = _dg(onehot, ys_all, _TN).reshape(seqs, LANES // seqs, d)
            b0 = k * LANES // rb
            r0 = k * LANES - b0 * rb
            bs, rs = slice(b0, b0 + seqs), slice(r0, r0 + LANES // seqs)
            x2 = x1_ref[bs, rs] + gt_ref[bs] * moe
            y_ref[bs, rs] = _rms_mod(x2, gf_ref[...], sc_ref[bs], sh_ref[bs])


def _moe(h2, route, x1, gt, sc, sh, gf, wg, wu, wd, tri, rows):
    bsz, seq, d = x1.shape
    nb, rb = _token_tile(bsz, seq, rows)
    tm = nb * rb
    assert tm % LANES == 0 and (rb % LANES == 0 or LANES % rb == 0)
    n_seq_tiles = seq // rb
    cap = tm + N_GROUPS * (MOE_SUB_ROWS // 2)
    assert cap % MOE_GATHER_ROWS == 0
    tokd = pl.BlockSpec((nb, rb, d), lambda b, i, g: (b, i, 0))
    vec = pl.BlockSpec((nb, 1, d), lambda b, i, g: (b, 0, 0))
    group_w = lambda shape: pl.BlockSpec((EXPERTS_PER_GROUP,) + shape, lambda b, i, g: (g, 0, 0))
    return pl.pallas_call(
        _moe_kernel,
        grid=(bsz // nb, n_seq_tiles, N_GROUPS),
        in_specs=[tokd, pl.BlockSpec((ROUTE_ROWS, tm), lambda b, i, g: (0, b * n_seq_tiles + i)),
                  pl.BlockSpec((nb, rb, d), lambda b, i, g: (b, i, 0), pipeline_mode=pl.Buffered(1)),
                  vec, vec, vec,
                  pl.BlockSpec((1, d), lambda b, i, g: (0, 0)),
                  pl.BlockSpec(tri.shape, lambda b, i, g: (0, 0)),
                  group_w((d, D_EXPERT)), group_w((d, D_EXPERT)), group_w((D_EXPERT, d))],
        out_specs=tokd,
        out_shape=jax.ShapeDtypeStruct((bsz, seq, d), F32),
        scratch_shapes=[pltpu.VMEM((cap, d), BF16), pltpu.VMEM((cap, N_EXPERTS), F32),
                        pltpu.VMEM((cap, d), BF16), pltpu.VMEM((1, tm), jnp.int32),
                        pltpu.SMEM((3 * N_GROUPS + 1,), jnp.int32)],
        compiler_params=_params("parallel", "parallel", "arbitrary"),
        name="moe_final_norm",
    )(h2, route, x1, gt, sc, sh, gf, tri, wg, wu, wd)


def _block_ones(n, blk, lower):
    i = np.arange(n)
    m = (i[:, None] // blk) == (i[None, :] // blk)
    if lower:
        m = m & (i[None, :] <= i[:, None])
    return jnp.asarray(m, dtype=BF16)


def _consts(tr, tg):
    return dict(seg=_block_ones(RWKV_WIDTH, RWKV_HEAD, False), tri=_block_ones(tr, CHUNK, True),
                tri_gla=_block_ones(tg, CHUNK, True), tri_up=_block_ones(LANES, LANES, True).T)


def _heads_block_diag(s):
    n, c = s.shape[2], s.shape[-1]
    rows = [jnp.pad(s[:, :, j], ((0, 0), (0, 0), (0, 0), (j * c, (n - 1 - j) * c))) for j in range(n)]
    return jnp.concatenate(rows, axis=2)


def _diag_blocks(t, n):
    r, c = t.shape[-2] // n, t.shape[-1] // n
    return jnp.stack([t[:, :, j * r:(j + 1) * r, j * c:(j + 1) * c] for j in range(n)], axis=2)


def _pad_rows(w, first_row):
    out = jnp.zeros((LORA_PAD, w.shape[1]), F32)
    return lax.dynamic_update_slice(out, w, (first_row, 0)).astype(BF16)


def _layer_params(l, w_in, mu_shift, w0, w_decay_up, a0, w_a_up, w_g_up, k_k, k_a, r_k, lnx_w, lnx_b,
                  w_gla_gate_up, b_gla_gate, gla_norm_g, w_out, norm2_g,
                  w_router_group, b_router_group, w_router_expert, b_router_expert):
    w_in_p = _win_layout(w_in[l].T, 256)
    n_pad = ROUTE_ROWS - N_EXPERTS - N_GROUPS
    w_router = jnp.concatenate([w_router_expert[l].T, w_router_group[l].T, jnp.zeros((n_pad, D_MODEL), F32)])
    b_router = jnp.concatenate([b_router_expert[l], b_router_group[l],
                                jnp.zeros((n_pad,), F32)]).reshape(ROUTE_ROWS, 1)
    wgate = jnp.zeros((LANES, GLA_KEY_WIDTH), F32).at[:GLA_GATE_RANK].set(w_gla_gate_up[l]).astype(BF16)
    r1 = lambda a: a.reshape(1, -1)
    return dict(
        w_in=w_in_p, mu=r1(mu_shift[l]), w0=r1(w0[l]), a0=r1(a0[l]), k_k=r1(k_k[l]), k_a=r1(k_a[l]),
        r_k=r1(r_k[l]), wd=_pad_rows(w_decay_up[l], 0), wa=_pad_rows(w_a_up[l], DECAY_LORA),
        wg=_pad_rows(w_g_up[l], DECAY_LORA + AAA_LORA), lnx_w=r1(lnx_w[l]), lnx_b=r1(lnx_b[l]),
        wgate=wgate, bgate=r1(b_gla_gate[l]), gla_g=r1(gla_norm_g[l]),
        w_out=w_out[l].astype(BF16), norm2_g=r1(norm2_g[l]), w_router=w_router, b_router=b_router)


def _run_layer(x, mod, shift0, wkv0, gla0, lp, experts, final):
    bsz, seq, d = x.shape
    assert seq % CHUNK == 0
    bb = 2 if seq >= RWKV_STEP_ROWS else min(bsz, RECURRENT_MAX_SEQS)
    assert bsz % bb == 0
    tg = min(seq, GLA_STEP_ROWS)
    consts = _consts(TOKEN_TILE_ROWS, tg)
    m = lambda j: mod[:, j:j + 1, :]
    sh1, sc1, gt1, sh2, sc2, gt2 = (m(j) for j in range(6))
    proj, tails = _in_proj(x, sc1, sh1, lp["norm1_g"], lp["w_in"], TOKEN_TILE_ROWS)
    tails = tails.reshape(bsz, -1, RWKV_PROJ)
    new_shift = tails[:, -1]
    prev = jnp.concatenate([shift0[:, None, :], tails[:, :-1]], axis=1).reshape(-1, 1, RWKV_PROJ)
    ab, rkv, bg, gl = _rwkv_prep(proj, prev, lp, consts, TOKEN_TILE_ROWS)
    gl = gl.reshape(bsz, seq // CHUNK, RWKV_WIDTH)
    s0 = _heads_block_diag(wkv0.reshape(bsz, RWKV_TILES, RWKV_TILE_HEADS, RWKV_HEAD, RWKV_HEAD))
    yr, s_bd = _rwkv_chunk(ab, rkv, gl[:, :, None, :], s0, bb, min(seq, RWKV_STEP_ROWS) // CHUNK)
    new_wkv = _diag_blocks(s_bd, RWKV_TILE_HEADS).reshape(bsz, RWKV_HEADS, RWKV_HEAD, RWKV_HEAD)
    t0 = _heads_block_diag(jnp.swapaxes(gla0, -1, -2).reshape(bsz, GLA_PAIRS, 2, GLA_DV, GLA_DK))
    yg, t_bd = _gla(proj, lp, consts["tri_gla"], t0, bb, tg)
    new_gla = jnp.swapaxes(_diag_blocks(t_bd, 2).reshape(bsz, GLA_HEADS, GLA_DV, GLA_DK), -1, -2)
    x1, h2, route = _out_proj(yr, bg, yg, x, gt1, sc2, sh2, lp, consts, TOKEN_TILE_ROWS)
    out = _moe(h2, route, x1, gt2, *final, *experts, consts["tri_up"], MOE_TILE_ROWS)
    return out, new_shift, new_wkv, new_gla


def kernel(x_prompt, x_sample, c_prompt, c_sample, state_rwkv_shift, state_rwkv_wkv, state_gla_kv, w_ada, b_ada, norm1_g, norm2_g, w_in, mu_shift, w0, w_decay_up, a0, w_a_up, w_g_up, k_k, k_a, r_k, lnx_w, lnx_b, w_gla_gate_up, b_gla_gate, gla_norm_g, w_out, w_router_group, b_router_group, w_router_expert, b_router_expert, w_expert_gate, w_expert_up, w_expert_down, w_ada_final, b_ada_final, normf_g):
    assert w_ada.shape[0] == 1, "the final norm is fused into the single layer's MoE kernel"
    bp, bs = x_prompt.shape[0], x_sample.shape[0]
    d = D_MODEL
    n_rows = -(-(bp + bs) // 8) * 8
    c_all = jnp.zeros((n_rows, d), F32).at[:bp].set(c_prompt).at[bp:bp + bs].set(c_sample)
    modf = _modulation(c_all, w_ada_final, b_ada_final, 1024).reshape(n_rows, 2, d)
    mod = _modulation(c_all, w_ada, b_ada[0], 1536, layer=0).reshape(n_rows, 6, d)
    lp = _layer_params(0, w_in, mu_shift, w0, w_decay_up, a0, w_a_up, w_g_up, k_k, k_a, r_k, lnx_w,
                       lnx_b, w_gla_gate_up, b_gla_gate, gla_norm_g, w_out, norm2_g,
                       w_router_group, b_router_group, w_router_expert, b_router_expert)
    lp["norm1_g"] = norm1_g[0].reshape(1, d)
    experts = (w_expert_gate[0].astype(BF16), w_expert_up[0].astype(BF16), w_expert_down[0].astype(BF16))
    groups = [
        (x_prompt, 0, bp, jnp.zeros((bp, RWKV_PROJ), F32),
         jnp.zeros((bp, RWKV_HEADS, RWKV_HEAD, RWKV_HEAD), F32), jnp.zeros((bp, GLA_HEADS, GLA_DK, GLA_DV), F32)),
        (x_sample, bp, bp + bs, state_rwkv_shift[0], state_rwkv_wkv[0], state_gla_kv[0]),
    ]
    ys, states = [], []
    for x, lo, hi, shift0, wkv0, gla0 in groups:
        final = (modf[lo:hi, 1:2], modf[lo:hi, 0:1], normf_g.reshape(1, d))
        y, *st = _run_layer(x, mod[lo:hi], shift0, wkv0, gla0, lp, experts, final)
        ys.append(y)
        states.extend(s[None] for s in st)
    return tuple(ys + states)
```

```python
import functools

import jax
import jax.numpy as jnp
import numpy as np
from jax import lax
from jax.experimental import pallas as pl
from jax.experimental.pallas import tpu as pltpu

F32 = jnp.float32
BF16 = jnp.bfloat16

LANES = 128
VMEM_LIMIT_BYTES = 56 * 1024 * 1024
TOKEN_TILE_ROWS = 512
MOE_TILE_ROWS = 1024
MOE_SUB_ROWS = 256
MOE_GATHER_ROWS = 256
GLA_STEP_ROWS = 512
RECURRENT_MAX_SEQS = 8
RWKV_STEP_ROWS = 256

D_MODEL = 1024
CHUNK = 64
RWKV_WIDTH = 512
RWKV_HEAD = 64
RWKV_HEADS = RWKV_WIDTH // RWKV_HEAD
HEAD_LANES = RWKV_HEAD
HEAD_SHIFT = HEAD_LANES.bit_length() - 1
assert 1 << HEAD_SHIFT == HEAD_LANES
RWKV_TILE_LANES = 128
RWKV_TILE_HEADS = RWKV_TILE_LANES // RWKV_HEAD
RWKV_TILES = RWKV_WIDTH // RWKV_TILE_LANES
DECAY_LORA = 32
AAA_LORA = 32
GATE_LORA = 64
LORA_PAD = DECAY_LORA + AAA_LORA + GATE_LORA
RWKV_PROJ = 3 * RWKV_WIDTH + LORA_PAD
GLA_WIDTH = 512
GLA_HEADS = 4
GLA_PAIRS = GLA_HEADS // 2
GLA_DV = GLA_WIDTH // GLA_HEADS
GLA_DK = GLA_DV // 2
GLA_KEY_WIDTH = GLA_HEADS * GLA_DK
GLA_GATE_RANK = 16
GLA_TAU = 16.0
GLA_PROJ = 2 * GLA_KEY_WIDTH + 2 * GLA_WIDTH + GLA_GATE_RANK
GLA_PROJ_PAD = RWKV_PROJ
IN_PROJ_PAD = RWKV_PROJ + GLA_PROJ_PAD
N_GROUPS = 4
EXPERTS_PER_GROUP = 4
N_EXPERTS = N_GROUPS * EXPERTS_PER_GROUP
GROUP_SHIFT = EXPERTS_PER_GROUP.bit_length() - 1
assert 1 << GROUP_SHIFT == EXPERTS_PER_GROUP
ROUTE_ROWS = 24
D_EXPERT = 512
RMS_EPS = 1e-6
LNX_EPS = 64e-5
LOG2_E = 1.4426950408889634

_NN = (((1,), (0,)), ((), ()))
_NT = (((1,), (1,)), ((), ()))
_TN = (((0,), (0,)), ((), ()))


def _dg(a, b, dims=_NN):
    return lax.dot_general(a, b, dims, preferred_element_type=F32)


def _split2(x):
    hi = x.astype(BF16)
    lo = (x - hi.astype(F32)).astype(BF16)
    return hi, lo


def _split3(x):
    hi = x.astype(BF16)
    r1 = x - hi.astype(F32)
    mid = r1.astype(BF16)
    lo = (r1 - mid.astype(F32)).astype(BF16)
    return hi, mid, lo


def _mm1(a, b, dims=_NN):
    return _dg(a.astype(BF16), b.astype(BF16), dims)


def _mm3(a, b, dims=_NN):
    ah, al = _split2(a)
    bh, bl = _split2(b)
    return _dg(ah, bh, dims) + (_dg(ah, bl, dims) + _dg(al, bh, dims))


def _mm_exact_lhs(e, x, dims=_NN):
    h, m, l = _split3(x)
    return _dg(e, h, dims) + (_dg(e, m, dims) + _dg(e, l, dims))


def _mm2_exact_rhs(x, e, dims=_NN):
    h, l = _split2(x)
    return _dg(h, e, dims) + _dg(l, e, dims)


def _softplus(z):
    return jnp.maximum(z, 0.0) + jnp.log(1.0 + jnp.exp(-jnp.abs(z)))


def _sigmoid(z):
    return 1.0 / (1.0 + jnp.exp(-z))


def _silu(z):
    return z * _sigmoid(z)


def _params(*sem):
    return pltpu.CompilerParams(dimension_semantics=sem, vmem_limit_bytes=VMEM_LIMIT_BYTES)


def _mod_kernel(c_ref, w_ref, b_ref, o_ref):
    o_ref[...] = _mm1(_silu(c_ref[...]), w_ref[...]) + b_ref[...]


def _modulation(c, w, b, tn, layer=None):
    rows, d = c.shape
    n = w.shape[-1]
    if layer is None:
        w_spec = pl.BlockSpec((d, tn), lambda j: (0, j))
    else:
        w_spec = pl.BlockSpec((None, d, tn), lambda j: (layer, 0, j))
    return pl.pallas_call(
        _mod_kernel,
        grid=(n // tn,),
        in_specs=[pl.BlockSpec((rows, d), lambda j: (0, 0)), w_spec,
                  pl.BlockSpec((1, tn), lambda j: (0, j))],
        out_specs=pl.BlockSpec((rows, tn), lambda j: (0, j)),
        out_shape=jax.ShapeDtypeStruct((rows, n), F32),
        compiler_params=_params("parallel"),
        name="modulation",
    )(c, w, b.reshape(1, n))


def _win_layout_kernel(w_ref, o_ref):
    qkv_end = RWKV_PROJ + 2 * GLA_KEY_WIDTH + GLA_WIDTH
    gate_end = qkv_end + GLA_GATE_RANK
    o_ref[:qkv_end] = w_ref[:qkv_end].astype(BF16)
    o_ref[qkv_end:qkv_end + GLA_WIDTH] = w_ref[gate_end:gate_end + GLA_WIDTH].astype(BF16)
    o_ref[qkv_end + GLA_WIDTH:RWKV_PROJ + GLA_PROJ] = w_ref[qkv_end:gate_end].astype(BF16)
    o_ref[RWKV_PROJ + GLA_PROJ:] = jnp.zeros((IN_PROJ_PAD - RWKV_PROJ - GLA_PROJ, o_ref.shape[1]), BF16)


def _win_layout(wt, cols):
    n, d = wt.shape
    return pl.pallas_call(
        _win_layout_kernel,
        grid=(d // cols,),
        in_specs=[pl.BlockSpec((n, cols), lambda i: (0, i))],
        out_specs=pl.BlockSpec((IN_PROJ_PAD, cols), lambda i: (0, i)),
        out_shape=jax.ShapeDtypeStruct((IN_PROJ_PAD, d), BF16),
        compiler_params=_params("parallel"),
        name="w_in_layout",
    )(wt)


def _rms_mod(x, g, sc, sh):
    ms = jnp.mean(x * x, axis=-1, keepdims=True)
    return (x * lax.rsqrt(ms + RMS_EPS) * g) * (1.0 + sc) + sh


def _token_tile(bsz, seq, rows):
    if seq >= rows:
        assert seq % rows == 0
        return 1, rows
    nb = min(bsz, rows // seq)
    assert bsz % nb == 0
    return nb, seq


def _inproj_kernel(x_ref, sc_ref, sh_ref, g_ref, w_ref, o_ref, last_ref, *, n_step):
    nb, rb, d = x_ref.shape
    h = _rms_mod(x_ref[...], g_ref[...], sc_ref[...], sh_ref[...])
    hb = h.reshape(nb * rb, d).astype(BF16)
    for j in range(IN_PROJ_PAD // n_step):
        cols = slice(j * n_step, (j + 1) * n_step)
        o_ref[:, :, cols] = _dg(hb, w_ref[cols, :], _NT).reshape(nb, rb, n_step)
    last_ref[...] = o_ref[:, rb - 1:rb, :RWKV_PROJ]


def _in_proj(x, sc, sh, g, w, rows):
    bsz, seq, d = x.shape
    nb, rb = _token_tile(bsz, seq, rows)
    n_seq_tiles = seq // rb
    vec = pl.BlockSpec((nb, 1, d), lambda b, i: (b, 0, 0))
    return pl.pallas_call(
        functools.partial(_inproj_kernel, n_step=2 * LANES),
        grid=(bsz // nb, n_seq_tiles),
        in_specs=[pl.BlockSpec((nb, rb, d), lambda b, i: (b, i, 0)), vec, vec,
                  pl.BlockSpec((1, d), lambda b, i: (0, 0)),
                  pl.BlockSpec((IN_PROJ_PAD, d), lambda b, i: (0, 0))],
        out_specs=[pl.BlockSpec((nb, rb, IN_PROJ_PAD), lambda b, i: (b, i, 0)),
                   pl.BlockSpec((nb, 1, RWKV_PROJ), lambda b, i: (b * n_seq_tiles + i, 0, 0))],
        out_shape=[jax.ShapeDtypeStruct((bsz, seq, IN_PROJ_PAD), F32),
                   jax.ShapeDtypeStruct((bsz * n_seq_tiles, 1, RWKV_PROJ), F32)],
        compiler_params=_params("parallel", "parallel"),
        name="norm1_in_proj",
    )(x, sc, sh, g, w)


def _rwkv_prep_kernel(p_ref, prev_ref, mu_ref, w0_ref, a0_ref, kk_ref, ka_ref, rk_ref,
                      wd_ref, wa_ref, wg_ref, seg_ref, tri_ref,
                      ab_ref, rkv_ref, bg_ref, gl_ref):
    nb, rb, wp = p_ref.shape
    tr = nb * rb
    p = p_ref[...].reshape(tr, wp)
    row = lax.broadcasted_iota(jnp.int32, (nb, rb, wp), 1)
    xx = jnp.where(row == 0, prev_ref[...], pltpu.roll(p, 1, 0).reshape(nb, rb, wp)).reshape(tr, wp)
    ps = p + (xx - p) * mu_ref[...]
    w = RWKV_WIDTH
    r, k, v, lora = ps[:, :w], ps[:, w:2 * w], ps[:, 2 * w:3 * w], ps[:, 3 * w:]
    logw = -_softplus(-(w0_ref[...] + _mm1(jnp.tanh(lora), wd_ref[...]))) - 0.5
    lw = jnp.exp(logw) * (-LOG2_E)
    a = _sigmoid(a0_ref[...] + _mm1(lora, wa_ref[...]))
    g = _mm1(_sigmoid(lora), wg_ref[...])
    seg = seg_ref[...]
    kk = k * kk_ref[...]
    kk = kk / jnp.maximum(jnp.sqrt(_mm2_exact_rhs(kk * kk, seg)), 1e-12)
    k2 = k * (1.0 + (a - 1.0) * ka_ref[...])
    cum = _mm_exact_lhs(tri_ref[...], lw)
    lasts = [jnp.exp2(cum[c * CHUNK + CHUNK - 1:(c + 1) * CHUNK, :]) for c in range(tr // CHUNK)]
    for c, l in enumerate(lasts):
        gl_ref[c:c + 1, :] = l

    def to_chunk_end(val):
        return jnp.concatenate([val[c * CHUNK:(c + 1) * CHUNK] * l for c, l in enumerate(lasts)], axis=0)

    ginv = jnp.exp2(-cum)
    bt = (kk * a) * ginv
    kt = k2 * ginv

    def put(ref, slot, val):
        ref[:, :, slot * w:(slot + 1) * w] = val.reshape(nb, rb, w).astype(ref.dtype)

    put(ab_ref, 0, -kk * jnp.exp2(cum - lw))
    put(ab_ref, 1, bt)
    put(rkv_ref, 0, r * jnp.exp2(cum))
    put(rkv_ref, 1, kt)
    put(rkv_ref, 2, to_chunk_end(bt))
    put(rkv_ref, 3, to_chunk_end(kt))
    put(rkv_ref, 4, v)
    put(bg_ref, 0, _mm2_exact_rhs(r * k2 * rk_ref[...], seg) * v)
    put(bg_ref, 1, g)


def _rwkv_prep(proj, prev, lp, consts, rows):
    bsz, seq, _ = proj.shape
    w = RWKV_WIDTH
    nb, rb = _token_tile(bsz, seq, rows)
    n_seq_tiles = seq // rb
    flat = lambda b, i: (b * n_seq_tiles + i, 0)
    row = lambda n: pl.BlockSpec((1, n), lambda b, i: (0, 0))
    full = lambda a: pl.BlockSpec(a.shape, lambda b, i: (0,) * a.ndim)
    packed = [(2, F32), (5, BF16), (2, F32)]
    return pl.pallas_call(
        _rwkv_prep_kernel,
        grid=(bsz // nb, n_seq_tiles),
        in_specs=[pl.BlockSpec((nb, rb, RWKV_PROJ), lambda b, i: (b, i, 0)),
                  pl.BlockSpec((nb, 1, RWKV_PROJ), lambda b, i: flat(b, i) + (0,)),
                  row(RWKV_PROJ), row(w), row(w), row(w), row(w), row(w),
                  full(lp["wd"]), full(lp["wa"]), full(lp["wg"]), full(consts["seg"]), full(consts["tri"])],
        out_specs=[pl.BlockSpec((nb, rb, k * w), lambda b, i: (b, i, 0)) for k, _ in packed]
        + [pl.BlockSpec((nb * rb // CHUNK, w), flat)],
        out_shape=[jax.ShapeDtypeStruct((bsz, seq, k * w), dt) for k, dt in packed]
        + [jax.ShapeDtypeStruct((bsz * seq // CHUNK, w), F32)],
        compiler_params=_params("parallel", "parallel"),
        name="rwkv_prep",
    )(proj, prev, lp["mu"], lp["w0"], lp["a0"], lp["k_k"], lp["k_a"], lp["r_k"],
      lp["wd"], lp["wa"], lp["wg"], consts["seg"], consts["tri"])


def _head_masks(width):
    lane = lax.broadcasted_iota(jnp.int32, (CHUNK, width), 1)
    row = lax.broadcasted_iota(jnp.int32, (CHUNK, width), 0)
    return lane >> HEAD_SHIFT, row, lane & (HEAD_LANES - 1)


def _block_diag(x, head):
    z = jnp.zeros_like(x)
    return jnp.concatenate([jnp.where(head == j, x, z) for j in range(x.shape[1] // HEAD_LANES)], axis=0)


def _rwkv_chunk_kernel(ab_ref, rkv_ref, gl_ref, s0_ref, y_ref, s_ref, *, bb, nck):
    @pl.when(pl.program_id(1) == 0)
    def _():
        s_ref[...] = s0_ref[...]

    tw = RWKV_TILE_LANES
    head, row, col = _head_masks(tw)
    strict = col < row
    incl = col <= row
    same8 = (col >> 3) == (row >> 3)
    lane2 = lax.broadcasted_iota(jnp.int32, (tw, tw), 1)
    row2 = lax.broadcasted_iota(jnp.int32, (tw, tw), 0)
    same_head = (lane2 >> HEAD_SHIFT) == (row2 >> HEAD_SHIFT)
    bd = functools.partial(_block_diag, head=head)
    c = CHUNK

    def pmm(p, q):
        return _dg(p.astype(BF16), bd(q.astype(BF16)))

    items = [(b, ck, pr) for b in range(bb) for ck in range(nck) for pr in range(RWKV_TILES)]
    n = range(len(items))
    rows = lambda ck: slice(ck * c, (ck + 1) * c)
    lanes = lambda pr: slice(pr * tw, (pr + 1) * tw)
    w = RWKV_WIDTH

    def ld(ref, slot):
        return [ref[b, rows(ck), slot * w + pr * tw:slot * w + (pr + 1) * tw] for b, ck, pr in items]

    cat0 = lambda *xs: jnp.concatenate(xs, axis=0)
    cat1 = lambda *xs: jnp.concatenate(xs, axis=1)
    at, bt = ld(ab_ref, 0), ld(ab_ref, 1)
    rt, kt, be, ke, v = (ld(rkv_ref, slot) for slot in range(5))
    ats, bts = [_split2(a) for a in at], [_split2(b) for b in bt]
    atb = [hi for hi, _ in ats]
    zero = jnp.zeros((c, tw), F32)
    gk = [_dg(cat0(atb[i], rt[i]), bd(kt[i]), _NT) for i in n]
    aak = [jnp.where(strict, gk[i][:c], zero) for i in n]
    ark = [jnp.where(incl, gk[i][c:], zero) for i in n]
    gb = [_dg(cat0(ats[i][0], ats[i][1], rt[i]), bd(bts[i][0]), _NT) for i in n]
    arb = [jnp.where(incl, gb[i][2 * c:], zero) for i in n]
    aab = [jnp.where(strict, gb[i][:c] + (gb[i][c:2 * c] + _dg(atb[i], bd(bts[i][1]), _NT)), zero) for i in n]
    z = [pmm(aak[i], v[i]) for i in n]
    a8 = [jnp.where(same8, aab[i], zero) for i in n]
    p2 = [pmm(a8[i], a8[i]) for i in n]
    p4 = [pmm(p2[i], p2[i]) for i in n]
    nn = [a8[i] + p2[i] + pmm(p2[i], a8[i]) for i in n]
    nn = [nn[i] + p4[i] + pmm(p4[i], nn[i]) for i in n]
    for lvl in (3, 4, 5):
        joins = ((col >> (lvl + 1)) == (row >> (lvl + 1))) & ((col >> lvl) != (row >> lvl))
        e = [jnp.where(joins, aab[i], zero) for i in n]
        te = [e[i] + pmm(nn[i], e[i]) for i in n]
        nn = [nn[i] + te[i] + pmm(te[i], nn[i]) for i in n]
    wu = [cat1(at[i], z[i]) + _dg(nn[i].astype(BF16), cat1(bd(atb[i]), bd(z[i].astype(BF16)))) for i in n]
    abk = [cat1(arb[i], ark[i]).astype(BF16) for i in n]
    s = {(b, pr): s_ref[b, pr] for b in range(bb) for pr in range(RWKV_TILES)}
    for ck in range(nck):
        cur = [i for i in n if items[i][1] == ck]
        key = lambda i: (items[i][0], items[i][2])
        x = {i: _dg(cat0(wu[i][:, :tw].astype(BF16), rt[i]), s[key(i)].astype(BF16), _NT) for i in cur}
        ub = {i: (x[i][:c] + wu[i][:, tw:]).astype(BF16) for i in cur}
        upd = {i: _dg(cat0(ub[i], v[i]), cat0(be[i], ke[i]), _TN) for i in cur}
        for i in cur:
            b, _, pr = items[i]
            y_ref[b, rows(ck), lanes(pr)] = x[i][c:] + _dg(abk[i], cat0(bd(ub[i]), bd(v[i])))
            s[b, pr] = (s[b, pr] * gl_ref[b, ck, :, lanes(pr)]
                        + jnp.where(same_head, upd[i], jnp.zeros_like(upd[i])))
    for (b, pr), val in s.items():
        s_ref[b, pr] = val


def _rwkv_chunk(ab, rkv, gl, s0, bb, nck):
    bsz, seq, _ = ab.shape
    w = RWKV_WIDTH
    tok = lambda a: pl.BlockSpec((bb, nck * CHUNK, a.shape[-1]), lambda i, c: (i, c, 0))
    st = pl.BlockSpec((bb, RWKV_TILES, RWKV_TILE_LANES, RWKV_TILE_LANES), lambda i, c: (i, 0, 0, 0))
    return pl.pallas_call(
        functools.partial(_rwkv_chunk_kernel, bb=bb, nck=nck),
        grid=(bsz // bb, seq // (nck * CHUNK)),
        in_specs=[tok(ab), tok(rkv), pl.BlockSpec((bb, nck, 1, w), lambda i, c: (i, c, 0, 0)), st],
        out_specs=[pl.BlockSpec((bb, nck * CHUNK, w), lambda i, c: (i, c, 0)), st],
        out_shape=[jax.ShapeDtypeStruct((bsz, seq, w), F32),
                   jax.ShapeDtypeStruct((bsz, RWKV_TILES, RWKV_TILE_LANES, RWKV_TILE_LANES), F32)],
        compiler_params=_params("parallel", "arbitrary"),
        name="rwkv_chunk",
    )(ab, rkv, gl, s0)


def _gla_kernel(p_ref, wgate_ref, bgate_ref, tri_ref, ng_ref, s0_ref, y_ref, s_ref, *, bb, nck):
    @pl.when(pl.program_id(1) == 0)
    def _():
        s_ref[...] = s0_ref[...]

    head, row, col = _head_masks(LANES)
    incl = col <= row
    bd = functools.partial(_block_diag, head=head)
    lane2 = lax.broadcasted_iota(jnp.int32, (2 * GLA_DV, LANES), 1)
    row2 = lax.broadcasted_iota(jnp.int32, (2 * GLA_DV, LANES), 0)
    same_head = (lane2 < GLA_DK) == (row2 < GLA_DV)
    kw, gw = GLA_KEY_WIDTH, GLA_WIDTH
    zero = jnp.zeros((CHUNK, LANES), F32)
    cat0 = lambda xs: jnp.concatenate(xs, axis=0)
    cat1 = lambda xs: jnp.concatenate(xs, axis=1)
    rows = lambda c: slice(c * CHUNK, (c + 1) * CHUNK)
    lanes = lambda pr: slice(pr * LANES, (pr + 1) * LANES)
    bs, cs = range(bb), range(nck)
    x = [p_ref[b] for b in bs]
    gate = [_mm1(x[b][:, 2 * kw + 2 * gw:], wgate_ref[...]) + bgate_ref[...] for b in bs]
    la = [(jnp.minimum(g, 0.0) - jnp.log(1.0 + jnp.exp(-jnp.abs(g)))) / GLA_TAU for g in gate]
    cum = [_mm_exact_lhs(tri_ref[...], la[b]) for b in bs]
    q_dec = [(x[b][:, :kw] * (GLA_DK ** -0.5)) * jnp.exp(cum[b]) for b in bs]
    k_inv = [x[b][:, kw:2 * kw] * jnp.exp(-cum[b]) for b in bs]
    last = [[cum[b][c * CHUNK + CHUNK - 1:(c + 1) * CHUNK, :] for c in cs] for b in bs]
    k_end = [[x[b][rows(c), kw:2 * kw] * jnp.exp(last[b][c] - cum[b][rows(c)]) for c in cs] for b in bs]
    dec = [[jnp.exp(last[b][c]) for c in cs] for b in bs]
    items = [(b, c, pr) for b in bs for c in cs for pr in range(GLA_PAIRS)]
    vp = {(b, c, pr): x[b][rows(c), 2 * kw + pr * 2 * GLA_DV:2 * kw + (pr + 1) * 2 * GLA_DV]
          for b, c, pr in items}
    att = {(b, c, pr): jnp.where(incl, _mm1(q_dec[b][rows(c), lanes(pr)],
                                            bd(k_inv[b][rows(c), lanes(pr)]), _NT), zero)
           for b, c, pr in items}
    upd = {it: _mm1(vp[it], k_end[it[0]][it[1]][:, lanes(it[2])], _TN) for it in items}
    intra = {it: _mm1(att[it], cat0([cat1([vp[it][:, :GLA_DV], zero]), cat1([zero, vp[it][:, GLA_DV:]])]))
             for it in items}
    st = {}
    for b in bs:
        for pr in range(GLA_PAIRS):
            cur = s_ref[b, pr]
            for c in cs:
                st[b, c, pr] = cur
                cur = cur * dec[b][c][:, lanes(pr)] + jnp.where(same_head, upd[b, c, pr], jnp.zeros_like(cur))
            s_ref[b, pr] = cur
    o = {(b, c, pr): intra[b, c, pr] + _mm1(q_dec[b][rows(c), lanes(pr)], st[b, c, pr], _NT)
         for b, c, pr in items}
    for b in bs:
        ob = cat0([cat1([o[b, c, pr] for pr in range(GLA_PAIRS)]) for c in cs])
        heads = [ob[:, h * GLA_DV:(h + 1) * GLA_DV] for h in range(GLA_HEADS)]
        normed = [oh * lax.rsqrt(jnp.mean(oh * oh, axis=-1, keepdims=True) + RMS_EPS) for oh in heads]
        y_ref[b] = cat1(normed) * ng_ref[...] * _silu(x[b][:, 2 * kw + gw:2 * kw + 2 * gw])


def _gla(proj, lp, tri, s0, bb, tg):
    bsz, seq, _ = proj.shape
    full = lambda a: pl.BlockSpec(a.shape, lambda i, c: (0,) * a.ndim)
    st = pl.BlockSpec((bb, GLA_PAIRS, 2 * GLA_DV, LANES), lambda i, c: (i, 0, 0, 0))
    return pl.pallas_call(
        functools.partial(_gla_kernel, bb=bb, nck=tg // CHUNK),
        grid=(bsz // bb, seq // tg),
        in_specs=[pl.BlockSpec((bb, tg, GLA_PROJ_PAD), lambda i, c: (i, c, 1)),
                  full(lp["wgate"]), full(lp["bgate"]), full(tri), full(lp["gla_g"]), st],
        out_specs=[pl.BlockSpec((bb, tg, GLA_WIDTH), lambda i, c: (i, c, 0)), st],
        out_shape=[jax.ShapeDtypeStruct((bsz, seq, GLA_WIDTH), F32),
                   jax.ShapeDtypeStruct((bsz, GLA_PAIRS, 2 * GLA_DV, LANES), F32)],
        compiler_params=_params("parallel", "arbitrary"),
        name="gla_chunk",
    )(proj, lp["wgate"], lp["bgate"], tri, lp["gla_g"], s0)


def _route(lgt):
    tm = lgt.shape[1]
    n_pad = ROUTE_ROWS - N_EXPERTS
    rowg = lax.broadcasted_iota(jnp.int32, (n_pad, tm), 0)
    rowe = lax.broadcasted_iota(jnp.int32, (N_EXPERTS, tm), 0)
    rowg_f, rowe_f = rowg.astype(F32), rowe.astype(F32)

    def first_argmax(vals, mx, rows_f):
        return jnp.min(jnp.where(vals == mx, rows_f, float(ROUTE_ROWS)), axis=0, keepdims=True)

    is_group = rowg < N_GROUPS
    lg = jnp.where(is_group, lgt[N_EXPERTS:], -jnp.inf)
    gmax = jnp.max(lg, axis=0, keepdims=True)
    gi = first_argmax(lg, gmax, rowg_f)
    pg_top = 1.0 / jnp.sum(jnp.where(is_group, jnp.exp(lg - gmax), 0.0), axis=0, keepdims=True)
    in_group = (rowe >> GROUP_SHIFT).astype(F32) == gi
    le = jnp.where(in_group, lgt[:N_EXPERTS], -jnp.inf)
    m1 = jnp.max(le, axis=0, keepdims=True)
    i1 = first_argmax(le, m1, rowe_f)
    le2 = jnp.where(rowe_f == i1, -jnp.inf, le)
    m2 = jnp.max(le2, axis=0, keepdims=True)
    i2 = first_argmax(le2, m2, rowe_f)
    p2 = jnp.exp(m2 - m1)
    w1 = pg_top / (1.0 + p2)
    w2 = pg_top * p2 / (1.0 + p2)
    comb = jnp.where(rowe_f == i1, w1, 0.0) + jnp.where(rowe_f == i2, w2, 0.0)
    return jnp.concatenate([comb, jnp.where(rowg == 0, gi, 0.0)], axis=0)


def _outproj_kernel(yr_ref, bg_ref, yg_ref, x_ref, gt_ref, sc_ref, sh_ref,
                    lnw_ref, lnb_ref, seg_ref, wout_ref, g2_ref, wr_ref, br_ref,
                    x1_ref, h2_ref, route_ref):
    nb, rb, d_model = x_ref.shape
    tm = nb * rb
    flat = lambda ref: ref[...].reshape(tm, ref.shape[-1])
    seg = seg_ref[...]
    y = flat(yr_ref)
    inv_n = 1.0 / RWKV_HEAD
    d = y - _mm2_exact_rhs(y, seg) * inv_n
    var = _mm2_exact_rhs(d * d, seg) * inv_n
    yn = d * lax.rsqrt(var + LNX_EPS) * lnw_ref[...] + lnb_ref[...]
    bg = flat(bg_ref)
    yr = (yn + bg[:, :RWKV_WIDTH]) * bg[:, RWKV_WIDTH:]
    mix = jnp.concatenate([yr, flat(yg_ref)], axis=1)
    x1 = x_ref[...] + gt_ref[...] * _mm1(mix, wout_ref[...]).reshape(nb, rb, d_model)
    x1_ref[...] = x1
    h2 = _rms_mod(x1, g2_ref[...], sc_ref[...], sh_ref[...])
    h2_ref[...] = h2.astype(BF16)
    route_ref[...] = _route(_mm3(wr_ref[...], h2.reshape(tm, d_model), _NT) + br_ref[...])


def _out_proj(yr, bg, yg, x, gt, sc, sh, lp, consts, rows):
    bsz, seq, d = x.shape
    nb, rb = _token_tile(bsz, seq, rows)
    n_seq_tiles = seq // rb
    half = pl.BlockSpec((nb, rb, RWKV_WIDTH), lambda b, i: (b, i, 0))
    tokd = pl.BlockSpec((nb, rb, d), lambda b, i: (b, i, 0))
    vec = pl.BlockSpec((nb, 1, d), lambda b, i: (b, 0, 0))
    full = lambda a: pl.BlockSpec(a.shape, lambda b, i: (0,) * a.ndim)
    args = (lp["lnx_w"], lp["lnx_b"], consts["seg"], lp["w_out"], lp["norm2_g"], lp["w_router"], lp["b_router"])
    return pl.pallas_call(
        _outproj_kernel,
        grid=(bsz // nb, seq // rb),
        in_specs=[half, pl.BlockSpec((nb, rb, 2 * RWKV_WIDTH), lambda b, i: (b, i, 0)), half, tokd,
                  vec, vec, vec] + [full(a) for a in args],
        out_specs=[tokd, tokd, pl.BlockSpec((ROUTE_ROWS, nb * rb), lambda b, i: (0, b * n_seq_tiles + i))],
        out_shape=[jax.ShapeDtypeStruct((bsz, seq, d), F32),
                   jax.ShapeDtypeStruct((bsz, seq, d), BF16),
                   jax.ShapeDtypeStruct((ROUTE_ROWS, bsz * seq), F32)],
        compiler_params=_params("parallel", "parallel"),
        name="out_proj_router",
    )(yr, bg, yg, x, gt, sc, sh, *args)


def _snake_group(tile, step):
    return jnp.where(tile % 2 == 0, step, N_GROUPS - 1 - step)


def _moe_kernel(h_ref, route_ref, x1_ref, gt_ref, sc_ref, sh_ref, gf_ref, tri_ref, wg_ref, wu_ref, wd_ref,
                y_ref, hs_ref, cs_ref, ys_ref, pos_ref, meta_ref):
    step = pl.program_id(2)
    g = _snake_group(pl.program_id(0) * pl.num_programs(1) + pl.program_id(1), step)
    nb, rb, d = h_ref.shape
    tm = nb * rb
    sub, tail = MOE_SUB_ROWS, MOE_SUB_ROWS // 2
    cap = hs_ref.shape[0]
    n_lane_tiles = tm // LANES
    lane_tile = lambda k: slice(k * LANES, (k + 1) * LANES)

    @pl.when(step == 0)
    def _sort():
        row8 = lax.broadcasted_iota(jnp.int32, (8, LANES), 0).astype(F32)
        carry = jnp.zeros((8, 1), F32)
        members, ranks = [], []
        for k in range(n_lane_tiles):
            blk = jnp.where(row8 == route_ref[N_EXPERTS:N_EXPERTS + 1, lane_tile(k)], 1.0, 0.0)
            members.append(blk)
            ranks.append(_dg(blk.astype(BF16), tri_ref[...]) - blk + carry)
            carry = carry + jnp.sum(blk, axis=1, keepdims=True)
        rowc = lax.broadcasted_iota(jnp.int32, (8, 1), 0)
        lane = lax.broadcasted_iota(jnp.int32, (1, LANES), 1)
        first = jnp.zeros((1, 1), F32)
        off_col = jnp.zeros((8, 1), F32)
        meta = jnp.zeros((1, LANES), F32)
        for grp in range(N_GROUPS):
            count = jnp.sum(jnp.where(rowc == grp, carry, 0.0), axis=0, keepdims=True)
            full = jnp.floor(count * (1.0 / sub))
            rest = count - full * sub
            has_tail = jnp.where((rest > 0.0) & (rest <= float(tail)), 1.0, 0.0)
            n_full = full + jnp.where(rest > float(tail), 1.0, 0.0)
            off_col = off_col + jnp.where(rowc == grp, first, 0.0)
            meta = (meta + jnp.where(lane == grp, first, 0.0) + jnp.where(lane == N_GROUPS + grp, n_full, 0.0)
                    + jnp.where(lane == 2 * N_GROUPS + grp, has_tail, 0.0))
            first = first + n_full * sub + has_tail * tail
        meta = (meta + jnp.where(lane == 3 * N_GROUPS, first, 0.0)).astype(jnp.int32)
        for i in range(3 * N_GROUPS + 1):
            meta_ref[i] = meta[0, i]
        for k in range(n_lane_tiles):
            pos = jnp.sum(members[k] * (ranks[k] + off_col), axis=0, keepdims=True)
            pos_ref[:, lane_tile(k)] = pos.astype(jnp.int32)
        ys_ref[...] = jnp.zeros_like(ys_ref)
        h = h_ref[...].reshape(tm, d)
        comb3 = _split3(route_ref[:N_EXPERTS, :])
        for j in range(cap // MOE_GATHER_ROWS):
            @pl.when(j * MOE_GATHER_ROWS < meta_ref[3 * N_GROUPS])
            def _gather_rows(j=j):
                rows = slice(j * MOE_GATHER_ROWS, (j + 1) * MOE_GATHER_ROWS)
                prow = lax.broadcasted_iota(jnp.int32, (MOE_GATHER_ROWS, tm), 0) + j * MOE_GATHER_ROWS
                onehot = jnp.where(prow == pos_ref[...], 1.0, 0.0).astype(BF16)
                hs_ref[rows, :] = _dg(onehot, h).astype(BF16)
                cs_ref[rows, :] = (_dg(onehot, comb3[0], _NT)
                                   + (_dg(onehot, comb3[1], _NT) + _dg(onehot, comb3[2], _NT)))

    def experts(base, n_rows):
        hs = hs_ref[pl.ds(base, n_rows), :]
        cs = cs_ref[pl.ds(base, n_rows), :]
        lane = lax.broadcasted_iota(jnp.int32, cs.shape, 1)
        ys = jnp.zeros((n_rows, d), F32)
        for e in range(EXPERTS_PER_GROUP):
            hid = _silu(_dg(hs, wg_ref[e])) * _dg(hs, wu_ref[e])
            ce = jnp.sum(jnp.where(lane == g * EXPERTS_PER_GROUP + e, cs, 0.0), axis=-1, keepdims=True)
            ys = ys + ce * _dg(hid.astype(BF16), wd_ref[e])
        ys_ref[pl.ds(base, n_rows), :] = ys.astype(BF16)

    first_row = meta_ref[g]
    n_full = meta_ref[N_GROUPS + g]

    def full_sub_tile(j, carry_):
        experts(pl.multiple_of(first_row + j * sub, tail), sub)
        return carry_

    lax.fori_loop(0, n_full, full_sub_tile, 0)

    @pl.when(meta_ref[2 * N_GROUPS + g] == 1)
    def _tail():
        experts(pl.multiple_of(first_row + n_full * sub, tail), tail)

    @pl.when(step == N_GROUPS - 1)
    def _unsort():
        ys_all = ys_ref[...]
        prow = lax.broadcasted_iota(jnp.int32, (cap, LANES), 0)
        seqs = max(LANES // rb, 1)
        for k in range(n_lane_tiles):
            onehot = jnp.where(prow == pos_ref[:, lane_tile(k)], 1.0, 0.0).astype(BF16)
            moe = _dg(onehot, ys_all, _TN).reshape(seqs, LANES // seqs, d)
            b0 = k * LANES // rb
            r0 = k * LANES - b0 * rb
            bs, rs = slice(b0, b0 + seqs), slice(r0, r0 + LANES // seqs)
            x2 = x1_ref[bs, rs] + gt_ref[bs] * moe
            y_ref[bs, rs] = _rms_mod(x2, gf_ref[...], sc_ref[bs], sh_ref[bs])


def _moe(h2, route, x1, gt, sc, sh, gf, wg, wu, wd, tri, rows):
    bsz, seq, d = x1.shape
    nb, rb = _token_tile(bsz, seq, rows)
    tm = nb * rb
    assert tm % LANES == 0 and (rb % LANES == 0 or LANES % rb == 0)
    n_seq_tiles = seq // rb
    cap = tm + N_GROUPS * (MOE_SUB_ROWS // 2)
    assert cap % MOE_GATHER_ROWS == 0
    tokd = pl.BlockSpec((nb, rb, d), lambda b, i, g: (b, i, 0))
    vec = pl.BlockSpec((nb, 1, d), lambda b, i, g: (b, 0, 0))
    group_w = lambda shape: pl.BlockSpec(
        (EXPERTS_PER_GROUP,) + shape, lambda b, i, s: (_snake_group(b * n_seq_tiles + i, s), 0, 0))
    return pl.pallas_call(
        _moe_kernel,
        grid=(bsz // nb, n_seq_tiles, N_GROUPS),
        in_specs=[tokd, pl.BlockSpec((ROUTE_ROWS, tm), lambda b, i, g: (0, b * n_seq_tiles + i)),
                  pl.BlockSpec((nb, rb, d), lambda b, i, g: (b, i, 0), pipeline_mode=pl.Buffered(1)),
                  vec, vec, vec,
                  pl.BlockSpec((1, d), lambda b, i, g: (0, 0)),
                  pl.BlockSpec(tri.shape, lambda b, i, g: (0, 0)),
                  group_w((d, D_EXPERT)), group_w((d, D_EXPERT)), group_w((D_EXPERT, d))],
        out_specs=tokd,
        out_shape=jax.ShapeDtypeStruct((bsz, seq, d), F32),
        scratch_shapes=[pltpu.VMEM((cap, d), BF16), pltpu.VMEM((cap, N_EXPERTS), F32),
                        pltpu.VMEM((cap, d), BF16), pltpu.VMEM((1, tm), jnp.int32),
                        pltpu.SMEM((3 * N_GROUPS + 1,), jnp.int32)],
        compiler_params=_params("parallel", "parallel", "arbitrary"),
        name="moe_final_norm",
    )(h2, route, x1, gt, sc, sh, gf, tri, wg, wu, wd)


def _block_ones(n, blk, lower):
    i = np.arange(n)
    m = (i[:, None] // blk) == (i[None, :] // blk)
    if lower:
        m = m & (i[None, :] <= i[:, None])
    return jnp.asarray(m, dtype=BF16)


def _consts(tr, tg):
    return dict(seg=_block_ones(RWKV_WIDTH, RWKV_HEAD, False), tri=_block_ones(tr, CHUNK, True),
                tri_gla=_block_ones(tg, CHUNK, True), tri_up=_block_ones(LANES, LANES, True).T)


def _heads_block_diag(s):
    n, c = s.shape[2], s.shape[-1]
    rows = [jnp.pad(s[:, :, j], ((0, 0), (0, 0), (0, 0), (j * c, (n - 1 - j) * c))) for j in range(n)]
    return jnp.concatenate(rows, axis=2)


def _diag_blocks(t, n):
    r, c = t.shape[-2] // n, t.shape[-1] // n
    return jnp.stack([t[:, :, j * r:(j + 1) * r, j * c:(j + 1) * c] for j in range(n)], axis=2)


def _pad_rows(w, first_row):
    out = jnp.zeros((LORA_PAD, w.shape[1]), F32)
    return lax.dynamic_update_slice(out, w, (first_row, 0)).astype(BF16)


def _layer_params(l, w_in, mu_shift, w0, w_decay_up, a0, w_a_up, w_g_up, k_k, k_a, r_k, lnx_w, lnx_b,
                  w_gla_gate_up, b_gla_gate, gla_norm_g, w_out, norm2_g,
                  w_router_group, b_router_group, w_router_expert, b_router_expert):
    w_in_p = _win_layout(w_in[l].T, 256)
    n_pad = ROUTE_ROWS - N_EXPERTS - N_GROUPS
    w_router = jnp.concatenate([w_router_expert[l].T, w_router_group[l].T, jnp.zeros((n_pad, D_MODEL), F32)])
    b_router = jnp.concatenate([b_router_expert[l], b_router_group[l],
                                jnp.zeros((n_pad,), F32)]).reshape(ROUTE_ROWS, 1)
    wgate = jnp.zeros((LANES, GLA_KEY_WIDTH), F32).at[:GLA_GATE_RANK].set(w_gla_gate_up[l]).astype(BF16)
    r1 = lambda a: a.reshape(1, -1)
    return dict(
        w_in=w_in_p, mu=r1(mu_shift[l]), w0=r1(w0[l]), a0=r1(a0[l]), k_k=r1(k_k[l]), k_a=r1(k_a[l]),
        r_k=r1(r_k[l]), wd=_pad_rows(w_decay_up[l], 0), wa=_pad_rows(w_a_up[l], DECAY_LORA),
        wg=_pad_rows(w_g_up[l], DECAY_LORA + AAA_LORA), lnx_w=r1(lnx_w[l]), lnx_b=r1(lnx_b[l]),
        wgate=wgate, bgate=r1(b_gla_gate[l]), gla_g=r1(gla_norm_g[l]),
        w_out=w_out[l].astype(BF16), norm2_g=r1(norm2_g[l]), w_router=w_router, b_router=b_router)


def _run_layer(x, mod, shift0, wkv0, gla0, lp, experts, final):
    bsz, seq, d = x.shape
    assert seq % CHUNK == 0
    bb = 2 if seq >= RWKV_STEP_ROWS else min(bsz, RECURRENT_MAX_SEQS)
    assert bsz % bb == 0
    tg = min(seq, GLA_STEP_ROWS)
    consts = _consts(TOKEN_TILE_ROWS, tg)
    m = lambda j: mod[:, j:j + 1, :]
    sh1, sc1, gt1, sh2, sc2, gt2 = (m(j) for j in range(6))
    proj, tails = _in_proj(x, sc1, sh1, lp["norm1_g"], lp["w_in"], TOKEN_TILE_ROWS)
    tails = tails.reshape(bsz, -1, RWKV_PROJ)
    new_shift = tails[:, -1]
    prev = jnp.concatenate([shift0[:, None, :], tails[:, :-1]], axis=1).reshape(-1, 1, RWKV_PROJ)
    ab, rkv, bg, gl = _rwkv_prep(proj, prev, lp, consts, TOKEN_TILE_ROWS)
    gl = gl.reshape(bsz, seq // CHUNK, RWKV_WIDTH)
    s0 = _heads_block_diag(wkv0.reshape(bsz, RWKV_TILES, RWKV_TILE_HEADS, RWKV_HEAD, RWKV_HEAD))
    yr, s_bd = _rwkv_chunk(ab, rkv, gl[:, :, None, :], s0, bb, min(seq, RWKV_STEP_ROWS) // CHUNK)
    new_wkv = _diag_blocks(s_bd, RWKV_TILE_HEADS).reshape(bsz, RWKV_HEADS, RWKV_HEAD, RWKV_HEAD)
    t0 = _heads_block_diag(jnp.swapaxes(gla0, -1, -2).reshape(bsz, GLA_PAIRS, 2, GLA_DV, GLA_DK))
    yg, t_bd = _gla(proj, lp, consts["tri_gla"], t0, bb, tg)
    new_gla = jnp.swapaxes(_diag_blocks(t_bd, 2).reshape(bsz, GLA_HEADS, GLA_DV, GLA_DK), -1, -2)
    x1, h2, route = _out_proj(yr, bg, yg, x, gt1, sc2, sh2, lp, consts, TOKEN_TILE_ROWS)
    out = _moe(h2, route, x1, gt2, *final, *experts, consts["tri_up"], MOE_TILE_ROWS)
    return out, new_shift, new_wkv, new_gla


def kernel(x_prompt, x_sample, c_prompt, c_sample, state_rwkv_shift, state_rwkv_wkv, state_gla_kv, w_ada, b_ada, norm1_g, norm2_g, w_in, mu_shift, w0, w_decay_up, a0, w_a_up, w_g_up, k_k, k_a, r_k, lnx_w, lnx_b, w_gla_gate_up, b_gla_gate, gla_norm_g, w_out, w_router_group, b_router_group, w_router_expert, b_router_expert, w_expert_gate, w_expert_up, w_expert_down, w_ada_final, b_ada_final, normf_g):
    assert w_ada.shape[0] == 1, "the final norm is fused into the single layer's MoE kernel"
    bp, bs = x_prompt.shape[0], x_sample.shape[0]
    d = D_MODEL
    n_rows = -(-(bp + bs) // 8) * 8
    c_all = jnp.zeros((n_rows, d), F32).at[:bp].set(c_prompt).at[bp:bp + bs].set(c_sample)
    modf = _modulation(c_all, w_ada_final, b_ada_final, 1024).reshape(n_rows, 2, d)
    mod = _modulation(c_all, w_ada, b_ada[0], 1536, layer=0).reshape(n_rows, 6, d)
    lp = _layer_params(0, w_in, mu_shift, w0, w_decay_up, a0, w_a_up, w_g_up, k_k, k_a, r_k, lnx_w,
                       lnx_b, w_gla_gate_up, b_gla_gate, gla_norm_g, w_out, norm2_g,
                       w_router_group, b_router_group, w_router_expert, b_router_expert)
    lp["norm1_g"] = norm1_g[0].reshape(1, d)
    experts = (w_expert_gate[0].astype(BF16), w_expert_up[0].astype(BF16), w_expert_down[0].astype(BF16))
    groups = [
        (x_prompt, 0, bp, jnp.zeros((bp, RWKV_PROJ), F32),
         jnp.zeros((bp, RWKV_HEADS, RWKV_HEAD, RWKV_HEAD), F32), jnp.zeros((bp, GLA_HEADS, GLA_DK, GLA_DV), F32)),
        (x_sample, bp, bp + bs, state_rwkv_shift[0], state_rwkv_wkv[0], state_gla_kv[0]),
    ]
    ys, states = [], []
    for x, lo, hi, shift0, wkv0, gla0 in groups:
        final = (modf[lo:hi, 1:2], modf[lo:hi, 0:1], normf_g.reshape(1, d))
        y, *st = _run_layer(x, mod[lo:hi], shift0, wkv0, gla0, lp, experts, final)
        ys.append(y)
        states.extend(s[None] for s in st)
    return tuple(ys + states)
```

```python
import functools

import jax
import jax.numpy as jnp
import numpy as np
from jax import lax
from jax.experimental import pallas as pl
from jax.experimental.pallas import tpu as pltpu

F32 = jnp.float32
BF16 = jnp.bfloat16

LANES = 128
VMEM_LIMIT_BYTES = 56 * 1024 * 1024
TOKEN_TILE_ROWS = 512
MOE_TILE_ROWS = 1024
MOE_SUB_ROWS = 256
GLA_STEP_ROWS = 512
RECURRENT_MAX_SEQS = 8
RWKV_STEP_ROWS = 256

D_MODEL = 1024
CHUNK = 64
RWKV_WIDTH = 512
RWKV_HEAD = 64
RWKV_HEADS = RWKV_WIDTH // RWKV_HEAD
HEAD_LANES = RWKV_HEAD
HEAD_SHIFT = HEAD_LANES.bit_length() - 1
assert 1 << HEAD_SHIFT == HEAD_LANES
RWKV_TILE_LANES = 128
RWKV_TILE_HEADS = RWKV_TILE_LANES // RWKV_HEAD
RWKV_TILES = RWKV_WIDTH // RWKV_TILE_LANES
DECAY_LORA = 32
AAA_LORA = 32
GATE_LORA = 64
LORA_PAD = DECAY_LORA + AAA_LORA + GATE_LORA
RWKV_PROJ = 3 * RWKV_WIDTH + LORA_PAD
GLA_WIDTH = 512
GLA_HEADS = 4
GLA_PAIRS = GLA_HEADS // 2
GLA_DV = GLA_WIDTH // GLA_HEADS
GLA_DK = GLA_DV // 2
GLA_KEY_WIDTH = GLA_HEADS * GLA_DK
GLA_GATE_RANK = 16
GLA_TAU = 16.0
GLA_PROJ = 2 * GLA_KEY_WIDTH + 2 * GLA_WIDTH + GLA_GATE_RANK
GLA_PROJ_PAD = RWKV_PROJ
IN_PROJ_PAD = RWKV_PROJ + GLA_PROJ_PAD
N_GROUPS = 4
EXPERTS_PER_GROUP = 4
N_EXPERTS = N_GROUPS * EXPERTS_PER_GROUP
GROUP_SHIFT = EXPERTS_PER_GROUP.bit_length() - 1
assert 1 << GROUP_SHIFT == EXPERTS_PER_GROUP
ROUTE_ROWS = 24
D_EXPERT = 512
RMS_EPS = 1e-6
LNX_EPS = 64e-5
LOG2_E = 1.4426950408889634

_NN = (((1,), (0,)), ((), ()))
_NT = (((1,), (1,)), ((), ()))
_TN = (((0,), (0,)), ((), ()))


def _dg(a, b, dims=_NN):
    return lax.dot_general(a, b, dims, preferred_element_type=F32)


def _split2(x):
    hi = x.astype(BF16)
    lo = (x - hi.astype(F32)).astype(BF16)
    return hi, lo


def _split3(x):
    hi = x.astype(BF16)
    r1 = x - hi.astype(F32)
    mid = r1.astype(BF16)
    lo = (r1 - mid.astype(F32)).astype(BF16)
    return hi, mid, lo


def _mm1(a, b, dims=_NN):
    return _dg(a.astype(BF16), b.astype(BF16), dims)


def _mm3(a, b, dims=_NN):
    ah, al = _split2(a)
    bh, bl = _split2(b)
    return _dg(ah, bh, dims) + (_dg(ah, bl, dims) + _dg(al, bh, dims))


def _mm_exact_lhs(e, x, dims=_NN):
    h, m, l = _split3(x)
    return _dg(e, h, dims) + (_dg(e, m, dims) + _dg(e, l, dims))


def _mm2_exact_rhs(x, e, dims=_NN):
    h, l = _split2(x)
    return _dg(h, e, dims) + _dg(l, e, dims)


def _softplus(z):
    return jnp.maximum(z, 0.0) + jnp.log(1.0 + jnp.exp(-jnp.abs(z)))


def _sigmoid(z):
    return 1.0 / (1.0 + jnp.exp(-z))


def _silu(z):
    return z * _sigmoid(z)


def _params(*sem):
    return pltpu.CompilerParams(dimension_semantics=sem, vmem_limit_bytes=VMEM_LIMIT_BYTES)


def _mod_kernel(c_ref, w_ref, b_ref, o_ref):
    o_ref[...] = _mm1(_silu(c_ref[...]), w_ref[...]) + b_ref[...]


def _modulation(c, w, b, tn, layer=None):
    rows, d = c.shape
    n = w.shape[-1]
    if layer is None:
        w_spec = pl.BlockSpec((d, tn), lambda j: (0, j))
    else:
        w_spec = pl.BlockSpec((None, d, tn), lambda j: (layer, 0, j))
    return pl.pallas_call(
        _mod_kernel,
        grid=(n // tn,),
        in_specs=[pl.BlockSpec((rows, d), lambda j: (0, 0)), w_spec,
                  pl.BlockSpec((1, tn), lambda j: (0, j))],
        out_specs=pl.BlockSpec((rows, tn), lambda j: (0, j)),
        out_shape=jax.ShapeDtypeStruct((rows, n), F32),
        compiler_params=_params("parallel"),
        name="modulation",
    )(c, w, b.reshape(1, n))


def _win_layout_kernel(w_ref, o_ref):
    qkv_end = RWKV_PROJ + 2 * GLA_KEY_WIDTH + GLA_WIDTH
    gate_end = qkv_end + GLA_GATE_RANK
    o_ref[:qkv_end] = w_ref[:qkv_end].astype(BF16)
    o_ref[qkv_end:qkv_end + GLA_WIDTH] = w_ref[gate_end:gate_end + GLA_WIDTH].astype(BF16)
    o_ref[qkv_end + GLA_WIDTH:RWKV_PROJ + GLA_PROJ] = w_ref[qkv_end:gate_end].astype(BF16)
    o_ref[RWKV_PROJ + GLA_PROJ:] = jnp.zeros((IN_PROJ_PAD - RWKV_PROJ - GLA_PROJ, o_ref.shape[1]), BF16)


def _win_layout(wt, cols):
    n, d = wt.shape
    return pl.pallas_call(
        _win_layout_kernel,
        grid=(d // cols,),
        in_specs=[pl.BlockSpec((n, cols), lambda i: (0, i))],
        out_specs=pl.BlockSpec((IN_PROJ_PAD, cols), lambda i: (0, i)),
        out_shape=jax.ShapeDtypeStruct((IN_PROJ_PAD, d), BF16),
        compiler_params=_params("parallel"),
        name="w_in_layout",
    )(wt)


def _rms_mod(x, g, sc, sh):
    ms = jnp.mean(x * x, axis=-1, keepdims=True)
    return (x * lax.rsqrt(ms + RMS_EPS) * g) * (1.0 + sc) + sh


def _token_tile(bsz, seq, rows):
    if seq >= rows:
        assert seq % rows == 0
        return 1, rows
    nb = min(bsz, rows // seq)
    assert bsz % nb == 0
    return nb, seq


def _inproj_kernel(x_ref, sc_ref, sh_ref, g_ref, w_ref, o_ref, last_ref, *, n_step):
    nb, rb, d = x_ref.shape
    h = _rms_mod(x_ref[...], g_ref[...], sc_ref[...], sh_ref[...])
    hb = h.reshape(nb * rb, d).astype(BF16)
    for j in range(IN_PROJ_PAD // n_step):
        cols = slice(j * n_step, (j + 1) * n_step)
        o_ref[:, :, cols] = _dg(hb, w_ref[cols, :], _NT).reshape(nb, rb, n_step)
    last_ref[...] = o_ref[:, rb - 1:rb, :RWKV_PROJ]


def _in_proj(x, sc, sh, g, w, rows):
    bsz, seq, d = x.shape
    nb, rb = _token_tile(bsz, seq, rows)
    n_seq_tiles = seq // rb
    vec = pl.BlockSpec((nb, 1, d), lambda b, i: (b, 0, 0))
    return pl.pallas_call(
        functools.partial(_inproj_kernel, n_step=2 * LANES),
        grid=(bsz // nb, n_seq_tiles),
        in_specs=[pl.BlockSpec((nb, rb, d), lambda b, i: (b, i, 0)), vec, vec,
                  pl.BlockSpec((1, d), lambda b, i: (0, 0)),
                  pl.BlockSpec((IN_PROJ_PAD, d), lambda b, i: (0, 0))],
        out_specs=[pl.BlockSpec((nb, rb, IN_PROJ_PAD), lambda b, i: (b, i, 0)),
                   pl.BlockSpec((nb, 1, RWKV_PROJ), lambda b, i: (b * n_seq_tiles + i, 0, 0))],
        out_shape=[jax.ShapeDtypeStruct((bsz, seq, IN_PROJ_PAD), F32),
                   jax.ShapeDtypeStruct((bsz * n_seq_tiles, 1, RWKV_PROJ), F32)],
        compiler_params=_params("parallel", "parallel"),
        name="norm1_in_proj",
    )(x, sc, sh, g, w)


def _rwkv_prep_kernel(p_ref, prev_ref, mu_ref, w0_ref, a0_ref, kk_ref, ka_ref, rk_ref,
                      wd_ref, wa_ref, wg_ref, seg_ref, tri_ref,
                      ab_ref, rkv_ref, bg_ref, gl_ref):
    nb, rb, wp = p_ref.shape
    tr = nb * rb
    p = p_ref[...].reshape(tr, wp)
    row = lax.broadcasted_iota(jnp.int32, (nb, rb, wp), 1)
    xx = jnp.where(row == 0, prev_ref[...], pltpu.roll(p, 1, 0).reshape(nb, rb, wp)).reshape(tr, wp)
    ps = p + (xx - p) * mu_ref[...]
    w = RWKV_WIDTH
    r, k, v, lora = ps[:, :w], ps[:, w:2 * w], ps[:, 2 * w:3 * w], ps[:, 3 * w:]
    logw = -_softplus(-(w0_ref[...] + _mm1(jnp.tanh(lora), wd_ref[...]))) - 0.5
    lw = jnp.exp(logw) * (-LOG2_E)
    a = _sigmoid(a0_ref[...] + _mm1(lora, wa_ref[...]))
    g = _mm1(_sigmoid(lora), wg_ref[...])
    seg = seg_ref[...]
    kk = k * kk_ref[...]
    kk = kk / jnp.maximum(jnp.sqrt(_mm2_exact_rhs(kk * kk, seg)), 1e-12)
    k2 = k * (1.0 + (a - 1.0) * ka_ref[...])
    cum = _mm_exact_lhs(tri_ref[...], lw)
    lasts = [jnp.exp2(cum[c * CHUNK + CHUNK - 1:(c + 1) * CHUNK, :]) for c in range(tr // CHUNK)]
    for c, l in enumerate(lasts):
        gl_ref[c:c + 1, :] = l

    def to_chunk_end(val):
        return jnp.concatenate([val[c * CHUNK:(c + 1) * CHUNK] * l for c, l in enumerate(lasts)], axis=0)

    ginv = jnp.exp2(-cum)
    bt = (kk * a) * ginv
    kt = k2 * ginv

    def put(ref, slot, val):
        ref[:, :, slot * w:(slot + 1) * w] = val.reshape(nb, rb, w).astype(ref.dtype)

    put(ab_ref, 0, -kk * jnp.exp2(cum - lw))
    put(ab_ref, 1, bt)
    put(rkv_ref, 0, r * jnp.exp2(cum))
    put(rkv_ref, 1, kt)
    put(rkv_ref, 2, to_chunk_end(bt))
    put(rkv_ref, 3, to_chunk_end(kt))
    put(rkv_ref, 4, v)
    put(bg_ref, 0, _mm2_exact_rhs(r * k2 * rk_ref[...], seg) * v)
    put(bg_ref, 1, g)


def _rwkv_prep(proj, prev, lp, consts, rows):
    bsz, seq, _ = proj.shape
    w = RWKV_WIDTH
    nb, rb = _token_tile(bsz, seq, rows)
    n_seq_tiles = seq // rb
    flat = lambda b, i: (b * n_seq_tiles + i, 0)
    row = lambda n: pl.BlockSpec((1, n), lambda b, i: (0, 0))
    full = lambda a: pl.BlockSpec(a.shape, lambda b, i: (0,) * a.ndim)
    packed = [(2, F32), (5, BF16), (2, F32)]
    return pl.pallas_call(
        _rwkv_prep_kernel,
        grid=(bsz // nb, n_seq_tiles),
        in_specs=[pl.BlockSpec((nb, rb, RWKV_PROJ), lambda b, i: (b, i, 0)),
                  pl.BlockSpec((nb, 1, RWKV_PROJ), lambda b, i: flat(b, i) + (0,)),
                  row(RWKV_PROJ), row(w), row(w), row(w), row(w), row(w),
                  full(lp["wd"]), full(lp["wa"]), full(lp["wg"]), full(consts["seg"]), full(consts["tri"])],
        out_specs=[pl.BlockSpec((nb, rb, k * w), lambda b, i: (b, i, 0)) for k, _ in packed]
        + [pl.BlockSpec((nb * rb // CHUNK, w), flat)],
        out_shape=[jax.ShapeDtypeStruct((bsz, seq, k * w), dt) for k, dt in packed]
        + [jax.ShapeDtypeStruct((bsz * seq // CHUNK, w), F32)],
        compiler_params=_params("parallel", "parallel"),
        name="rwkv_prep",
    )(proj, prev, lp["mu"], lp["w0"], lp["a0"], lp["k_k"], lp["k_a"], lp["r_k"],
      lp["wd"], lp["wa"], lp["wg"], consts["seg"], consts["tri"])


def _head_masks(width):
    lane = lax.broadcasted_iota(jnp.int32, (CHUNK, width), 1)
    row = lax.broadcasted_iota(jnp.int32, (CHUNK, width), 0)
    return lane >> HEAD_SHIFT, row, lane & (HEAD_LANES - 1)


def _block_diag(x, head):
    z = jnp.zeros_like(x)
    return jnp.concatenate([jnp.where(head == j, x, z) for j in range(x.shape[1] // HEAD_LANES)], axis=0)


def _rwkv_chunk_kernel(ab_ref, rkv_ref, gl_ref, s0_ref, y_ref, s_ref, *, bb, nck):
    @pl.when(pl.program_id(1) == 0)
    def _():
        s_ref[...] = s0_ref[...]

    tw = RWKV_TILE_LANES
    head, row, col = _head_masks(tw)
    strict = col < row
    incl = col <= row
    same8 = (col >> 3) == (row >> 3)
    lane2 = lax.broadcasted_iota(jnp.int32, (tw, tw), 1)
    row2 = lax.broadcasted_iota(jnp.int32, (tw, tw), 0)
    same_head = (lane2 >> HEAD_SHIFT) == (row2 >> HEAD_SHIFT)
    bd = functools.partial(_block_diag, head=head)
    c = CHUNK

    def pmm(p, q):
        return _dg(p.astype(BF16), bd(q.astype(BF16)))

    items = [(b, ck, pr) for b in range(bb) for ck in range(nck) for pr in range(RWKV_TILES)]
    n = range(len(items))
    rows = lambda ck: slice(ck * c, (ck + 1) * c)
    lanes = lambda pr: slice(pr * tw, (pr + 1) * tw)
    w = RWKV_WIDTH

    def ld(ref, slot):
        return [ref[b, rows(ck), slot * w + pr * tw:slot * w + (pr + 1) * tw] for b, ck, pr in items]

    cat0 = lambda *xs: jnp.concatenate(xs, axis=0)
    cat1 = lambda *xs: jnp.concatenate(xs, axis=1)
    at, bt = ld(ab_ref, 0), ld(ab_ref, 1)
    rt, kt, be, ke, v = (ld(rkv_ref, slot) for slot in range(5))
    ats, bts = [_split2(a) for a in at], [_split2(b) for b in bt]
    atb = [hi for hi, _ in ats]
    zero = jnp.zeros((c, tw), F32)
    gk = [_dg(cat0(atb[i], rt[i]), bd(kt[i]), _NT) for i in n]
    aak = [jnp.where(strict, gk[i][:c], zero) for i in n]
    ark = [jnp.where(incl, gk[i][c:], zero) for i in n]
    gb = [_dg(cat0(ats[i][0], ats[i][1], rt[i]), bd(bts[i][0]), _NT) for i in n]
    arb = [jnp.where(incl, gb[i][2 * c:], zero) for i in n]
    aab = [jnp.where(strict, gb[i][:c] + (gb[i][c:2 * c] + _dg(atb[i], bd(bts[i][1]), _NT)), zero) for i in n]
    z = [pmm(aak[i], v[i]) for i in n]
    a8 = [jnp.where(same8, aab[i], zero) for i in n]
    p2 = [pmm(a8[i], a8[i]) for i in n]
    p4 = [pmm(p2[i], p2[i]) for i in n]
    nn = [a8[i] + p2[i] + pmm(p2[i], a8[i]) for i in n]
    nn = [nn[i] + p4[i] + pmm(p4[i], nn[i]) for i in n]
    for lvl in (3, 4, 5):
        joins = ((col >> (lvl + 1)) == (row >> (lvl + 1))) & ((col >> lvl) != (row >> lvl))
        e = [jnp.where(joins, aab[i], zero) for i in n]
        te = [e[i] + pmm(nn[i], e[i]) for i in n]
        nn = [nn[i] + te[i] + pmm(te[i], nn[i]) for i in n]
    wu = [cat1(at[i], z[i]) + _dg(nn[i].astype(BF16), cat1(bd(atb[i]), bd(z[i].astype(BF16)))) for i in n]
    abk = [cat1(arb[i], ark[i]).astype(BF16) for i in n]
    s = {(b, pr): s_ref[b, pr] for b in range(bb) for pr in range(RWKV_TILES)}
    for ck in range(nck):
        cur = [i for i in n if items[i][1] == ck]
        key = lambda i: (items[i][0], items[i][2])
        x = {i: _dg(cat0(wu[i][:, :tw].astype(BF16), rt[i]), s[key(i)].astype(BF16), _NT) for i in cur}
        ub = {i: (x[i][:c] + wu[i][:, tw:]).astype(BF16) for i in cur}
        upd = {i: _dg(cat0(ub[i], v[i]), cat0(be[i], ke[i]), _TN) for i in cur}
        for i in cur:
            b, _, pr = items[i]
            y_ref[b, rows(ck), lanes(pr)] = x[i][c:] + _dg(abk[i], cat0(bd(ub[i]), bd(v[i])))
            s[b, pr] = (s[b, pr] * gl_ref[b, ck, :, lanes(pr)]
                        + jnp.where(same_head, upd[i], jnp.zeros_like(upd[i])))
    for (b, pr), val in s.items():
        s_ref[b, pr] = val


def _rwkv_chunk(ab, rkv, gl, s0, bb, nck):
    bsz, seq, _ = ab.shape
    w = RWKV_WIDTH
    tok = lambda a: pl.BlockSpec((bb, nck * CHUNK, a.shape[-1]), lambda i, c: (i, c, 0))
    st = pl.BlockSpec((bb, RWKV_TILES, RWKV_TILE_LANES, RWKV_TILE_LANES), lambda i, c: (i, 0, 0, 0))
    return pl.pallas_call(
        functools.partial(_rwkv_chunk_kernel, bb=bb, nck=nck),
        grid=(bsz // bb, seq // (nck * CHUNK)),
        in_specs=[tok(ab), tok(rkv), pl.BlockSpec((bb, nck, 1, w), lambda i, c: (i, c, 0, 0)), st],
        out_specs=[pl.BlockSpec((bb, nck * CHUNK, w), lambda i, c: (i, c, 0)), st],
        out_shape=[jax.ShapeDtypeStruct((bsz, seq, w), F32),
                   jax.ShapeDtypeStruct((bsz, RWKV_TILES, RWKV_TILE_LANES, RWKV_TILE_LANES), F32)],
        compiler_params=_params("parallel", "arbitrary"),
        name="rwkv_chunk",
    )(ab, rkv, gl, s0)


def _gla_kernel(p_ref, wgate_ref, bgate_ref, tri_ref, ng_ref, s0_ref, y_ref, s_ref, *, bb, nck):
    @pl.when(pl.program_id(1) == 0)
    def _():
        s_ref[...] = s0_ref[...]

    head, row, col = _head_masks(LANES)
    incl = col <= row
    bd = functools.partial(_block_diag, head=head)
    lane2 = lax.broadcasted_iota(jnp.int32, (2 * GLA_DV, LANES), 1)
    row2 = lax.broadcasted_iota(jnp.int32, (2 * GLA_DV, LANES), 0)
    same_head = (lane2 < GLA_DK) == (row2 < GLA_DV)
    kw, gw = GLA_KEY_WIDTH, GLA_WIDTH
    zero = jnp.zeros((CHUNK, LANES), F32)
    cat0 = lambda xs: jnp.concatenate(xs, axis=0)
    cat1 = lambda xs: jnp.concatenate(xs, axis=1)
    rows = lambda c: slice(c * CHUNK, (c + 1) * CHUNK)
    lanes = lambda pr: slice(pr * LANES, (pr + 1) * LANES)
    bs, cs = range(bb), range(nck)
    x = [p_ref[b] for b in bs]
    gate = [_mm1(x[b][:, 2 * kw + 2 * gw:], wgate_ref[...]) + bgate_ref[...] for b in bs]
    la = [(jnp.minimum(g, 0.0) - jnp.log(1.0 + jnp.exp(-jnp.abs(g)))) / GLA_TAU for g in gate]
    cum = [_mm_exact_lhs(tri_ref[...], la[b]) for b in bs]
    q_dec = [(x[b][:, :kw] * (GLA_DK ** -0.5)) * jnp.exp(cum[b]) for b in bs]
    k_inv = [x[b][:, kw:2 * kw] * jnp.exp(-cum[b]) for b in bs]
    last = [[cum[b][c * CHUNK + CHUNK - 1:(c + 1) * CHUNK, :] for c in cs] for b in bs]
    k_end = [[x[b][rows(c), kw:2 * kw] * jnp.exp(last[b][c] - cum[b][rows(c)]) for c in cs] for b in bs]
    dec = [[jnp.exp(last[b][c]) for c in cs] for b in bs]
    items = [(b, c, pr) for b in bs for c in cs for pr in range(GLA_PAIRS)]
    vp = {(b, c, pr): x[b][rows(c), 2 * kw + pr * 2 * GLA_DV:2 * kw + (pr + 1) * 2 * GLA_DV]
          for b, c, pr in items}
    att = {(b, c, pr): jnp.where(incl, _mm1(q_dec[b][rows(c), lanes(pr)],
                                            bd(k_inv[b][rows(c), lanes(pr)]), _NT), zero)
           for b, c, pr in items}
    upd = {it: _mm1(vp[it], k_end[it[0]][it[1]][:, lanes(it[2])], _TN) for it in items}
    intra = {it: _mm1(att[it], cat0([cat1([vp[it][:, :GLA_DV], zero]), cat1([zero, vp[it][:, GLA_DV:]])]))
             for it in items}
    st = {}
    for b in bs:
        for pr in range(GLA_PAIRS):
            cur = s_ref[b, pr]
            for c in cs:
                st[b, c, pr] = cur
                cur = cur * dec[b][c][:, lanes(pr)] + jnp.where(same_head, upd[b, c, pr], jnp.zeros_like(cur))
            s_ref[b, pr] = cur
    o = {(b, c, pr): intra[b, c, pr] + _mm1(q_dec[b][rows(c), lanes(pr)], st[b, c, pr], _NT)
         for b, c, pr in items}
    for b in bs:
        ob = cat0([cat1([o[b, c, pr] for pr in range(GLA_PAIRS)]) for c in cs])
        heads = [ob[:, h * GLA_DV:(h + 1) * GLA_DV] for h in range(GLA_HEADS)]
        normed = [oh * lax.rsqrt(jnp.mean(oh * oh, axis=-1, keepdims=True) + RMS_EPS) for oh in heads]
        y_ref[b] = cat1(normed) * ng_ref[...] * _silu(x[b][:, 2 * kw + gw:2 * kw + 2 * gw])


def _gla(proj, lp, tri, s0, bb, tg):
    bsz, seq, _ = proj.shape
    full = lambda a: pl.BlockSpec(a.shape, lambda i, c: (0,) * a.ndim)
    st = pl.BlockSpec((bb, GLA_PAIRS, 2 * GLA_DV, LANES), lambda i, c: (i, 0, 0, 0))
    return pl.pallas_call(
        functools.partial(_gla_kernel, bb=bb, nck=tg // CHUNK),
        grid=(bsz // bb, seq // tg),
        in_specs=[pl.BlockSpec((bb, tg, GLA_PROJ_PAD), lambda i, c: (i, c, 1)),
                  full(lp["wgate"]), full(lp["bgate"]), full(tri), full(lp["gla_g"]), st],
        out_specs=[pl.BlockSpec((bb, tg, GLA_WIDTH), lambda i, c: (i, c, 0)), st],
        out_shape=[jax.ShapeDtypeStruct((bsz, seq, GLA_WIDTH), F32),
                   jax.ShapeDtypeStruct((bsz, GLA_PAIRS, 2 * GLA_DV, LANES), F32)],
        compiler_params=_params("parallel", "arbitrary"),
        name="gla_chunk",
    )(proj, lp["wgate"], lp["bgate"], tri, lp["gla_g"], s0)


def _route(lgt):
    tm = lgt.shape[1]
    n_pad = ROUTE_ROWS - N_EXPERTS
    rowg = lax.broadcasted_iota(jnp.int32, (n_pad, tm), 0)
    rowe = lax.broadcasted_iota(jnp.int32, (N_EXPERTS, tm), 0)
    rowg_f, rowe_f = rowg.astype(F32), rowe.astype(F32)

    def first_argmax(vals, mx, rows_f):
        return jnp.min(jnp.where(vals == mx, rows_f, float(ROUTE_ROWS)), axis=0, keepdims=True)

    is_group = rowg < N_GROUPS
    lg = jnp.where(is_group, lgt[N_EXPERTS:], -jnp.inf)
    gmax = jnp.max(lg, axis=0, keepdims=True)
    gi = first_argmax(lg, gmax, rowg_f)
    pg_top = 1.0 / jnp.sum(jnp.where(is_group, jnp.exp(lg - gmax), 0.0), axis=0, keepdims=True)
    in_group = (rowe >> GROUP_SHIFT).astype(F32) == gi
    le = jnp.where(in_group, lgt[:N_EXPERTS], -jnp.inf)
    m1 = jnp.max(le, axis=0, keepdims=True)
    i1 = first_argmax(le, m1, rowe_f)
    le2 = jnp.where(rowe_f == i1, -jnp.inf, le)
    m2 = jnp.max(le2, axis=0, keepdims=True)
    i2 = first_argmax(le2, m2, rowe_f)
    p2 = jnp.exp(m2 - m1)
    w1 = pg_top / (1.0 + p2)
    w2 = pg_top * p2 / (1.0 + p2)
    comb = jnp.where(rowe_f == i1, w1, 0.0) + jnp.where(rowe_f == i2, w2, 0.0)
    return jnp.concatenate([comb, jnp.where(rowg == 0, gi, 0.0)], axis=0)


def _outproj_kernel(yr_ref, bg_ref, yg_ref, x_ref, gt_ref, sc_ref, sh_ref,
                    lnw_ref, lnb_ref, seg_ref, wout_ref, g2_ref, wr_ref, br_ref,
                    x1_ref, h2_ref, route_ref):
    nb, rb, d_model = x_ref.shape
    tm = nb * rb
    flat = lambda ref: ref[...].reshape(tm, ref.shape[-1])
    seg = seg_ref[...]
    y = flat(yr_ref)
    inv_n = 1.0 / RWKV_HEAD
    d = y - _mm2_exact_rhs(y, seg) * inv_n
    var = _mm2_exact_rhs(d * d, seg) * inv_n
    yn = d * lax.rsqrt(var + LNX_EPS) * lnw_ref[...] + lnb_ref[...]
    bg = flat(bg_ref)
    yr = (yn + bg[:, :RWKV_WIDTH]) * bg[:, RWKV_WIDTH:]
    mix = jnp.concatenate([yr, flat(yg_ref)], axis=1)
    x1 = x_ref[...] + gt_ref[...] * _mm1(mix, wout_ref[...]).reshape(nb, rb, d_model)
    x1_ref[...] = x1
    h2 = _rms_mod(x1, g2_ref[...], sc_ref[...], sh_ref[...])
    h2_ref[...] = h2.astype(BF16)
    route_ref[...] = _route(_mm3(wr_ref[...], h2.reshape(tm, d_model), _NT) + br_ref[...])


def _out_proj(yr, bg, yg, x, gt, sc, sh, lp, consts, rows):
    bsz, seq, d = x.shape
    nb, rb = _token_tile(bsz, seq, rows)
    n_seq_tiles = seq // rb
    half = pl.BlockSpec((nb, rb, RWKV_WIDTH), lambda b, i: (b, i, 0))
    tokd = pl.BlockSpec((nb, rb, d), lambda b, i: (b, i, 0))
    vec = pl.BlockSpec((nb, 1, d), lambda b, i: (b, 0, 0))
    full = lambda a: pl.BlockSpec(a.shape, lambda b, i: (0,) * a.ndim)
    args = (lp["lnx_w"], lp["lnx_b"], consts["seg"], lp["w_out"], lp["norm2_g"], lp["w_router"], lp["b_router"])
    return pl.pallas_call(
        _outproj_kernel,
        grid=(bsz // nb, seq // rb),
        in_specs=[half, pl.BlockSpec((nb, rb, 2 * RWKV_WIDTH), lambda b, i: (b, i, 0)), half, tokd,
                  vec, vec, vec] + [full(a) for a in args],
        out_specs=[tokd, tokd, pl.BlockSpec((ROUTE_ROWS, nb * rb), lambda b, i: (0, b * n_seq_tiles + i))],
        out_shape=[jax.ShapeDtypeStruct((bsz, seq, d), F32),
                   jax.ShapeDtypeStruct((bsz, seq, d), BF16),
                   jax.ShapeDtypeStruct((ROUTE_ROWS, bsz * seq), F32)],
        compiler_params=_params("parallel", "parallel"),
        name="out_proj_router",
    )(yr, bg, yg, x, gt, sc, sh, *args)


def _moe_kernel(h_ref, route_ref, x1_ref, gt_ref, sc_ref, sh_ref, gf_ref, tri_ref, wg_ref, wu_ref, wd_ref,
                y_ref, ys_ref, pos_ref, meta_ref):
    g = pl.program_id(2)
    nb, rb, d = h_ref.shape
    tm = nb * rb
    sub, tail = MOE_SUB_ROWS, MOE_SUB_ROWS // 2
    cap = ys_ref.shape[0]
    n_lane_tiles = tm // LANES
    lane_tile = lambda k: slice(k * LANES, (k + 1) * LANES)

    @pl.when(g == 0)
    def _sort():
        row8 = lax.broadcasted_iota(jnp.int32, (8, LANES), 0).astype(F32)
        carry = jnp.zeros((8, 1), F32)
        members, ranks = [], []
        for k in range(n_lane_tiles):
            blk = jnp.where(row8 == route_ref[N_EXPERTS:N_EXPERTS + 1, lane_tile(k)], 1.0, 0.0)
            members.append(blk)
            ranks.append(_dg(blk.astype(BF16), tri_ref[...]) - blk + carry)
            carry = carry + jnp.sum(blk, axis=1, keepdims=True)
        rowc = lax.broadcasted_iota(jnp.int32, (8, 1), 0)
        lane = lax.broadcasted_iota(jnp.int32, (1, LANES), 1)
        first = jnp.zeros((1, 1), F32)
        off_col = jnp.zeros((8, 1), F32)
        meta = jnp.zeros((1, LANES), F32)
        for grp in range(N_GROUPS):
            count = jnp.sum(jnp.where(rowc == grp, carry, 0.0), axis=0, keepdims=True)
            full = jnp.floor(count * (1.0 / sub))
            rest = count - full * sub
            has_tail = jnp.where((rest > 0.0) & (rest <= float(tail)), 1.0, 0.0)
            n_full = full + jnp.where(rest > float(tail), 1.0, 0.0)
            off_col = off_col + jnp.where(rowc == grp, first, 0.0)
            meta = (meta + jnp.where(lane == grp, first, 0.0) + jnp.where(lane == N_GROUPS + grp, n_full, 0.0)
                    + jnp.where(lane == 2 * N_GROUPS + grp, has_tail, 0.0))
            first = first + n_full * sub + has_tail * tail
        meta = meta.astype(jnp.int32)
        for i in range(3 * N_GROUPS):
            meta_ref[i] = meta[0, i]
        for k in range(n_lane_tiles):
            pos = jnp.sum(members[k] * (ranks[k] + off_col), axis=0, keepdims=True)
            pos_ref[:, lane_tile(k)] = pos.astype(jnp.int32)
        ys_ref[...] = jnp.zeros_like(ys_ref)

    def experts(base, n_rows):
        prow = lax.broadcasted_iota(jnp.int32, (n_rows, tm), 0) + base
        onehot = jnp.where(prow == pos_ref[...], 1.0, 0.0).astype(BF16)
        hs = _dg(onehot, h_ref[...].reshape(tm, d)).astype(BF16)
        comb3 = _split3(route_ref[:N_EXPERTS, :])
        cs = _dg(onehot, comb3[0], _NT) + (_dg(onehot, comb3[1], _NT) + _dg(onehot, comb3[2], _NT))
        lane = lax.broadcasted_iota(jnp.int32, cs.shape, 1)
        ys = jnp.zeros((n_rows, d), F32)
        for e in range(EXPERTS_PER_GROUP):
            hid = _silu(_dg(hs, wg_ref[e])) * _dg(hs, wu_ref[e])
            ce = jnp.sum(jnp.where(lane == g * EXPERTS_PER_GROUP + e, cs, 0.0), axis=-1, keepdims=True)
            ys = ys + ce * _dg(hid.astype(BF16), wd_ref[e])
        ys_ref[pl.ds(base, n_rows), :] = ys.astype(BF16)

    first_row = meta_ref[g]
    n_full = meta_ref[N_GROUPS + g]

    def full_sub_tile(j, carry_):
        experts(pl.multiple_of(first_row + j * sub, tail), sub)
        return carry_

    lax.fori_loop(0, n_full, full_sub_tile, 0)

    @pl.when(meta_ref[2 * N_GROUPS + g] == 1)
    def _tail():
        experts(pl.multiple_of(first_row + n_full * sub, tail), tail)

    @pl.when(g == N_GROUPS - 1)
    def _unsort():
        ys_all = ys_ref[...]
        prow = lax.broadcasted_iota(jnp.int32, (cap, LANES), 0)
        seqs = max(LANES // rb, 1)
        for k in range(n_lane_tiles):
            onehot = jnp.where(prow == pos_ref[:, lane_tile(k)], 1.0, 0.0).astype(BF16)
            moe = _dg(onehot, ys_all, _TN).reshape(seqs, LANES // seqs, d)
            b0 = k * LANES // rb
            r0 = k * LANES - b0 * rb
            bs, rs = slice(b0, b0 + seqs), slice(r0, r0 + LANES // seqs)
            x2 = x1_ref[bs, rs] + gt_ref[bs] * moe
            y_ref[bs, rs] = _rms_mod(x2, gf_ref[...], sc_ref[bs], sh_ref[bs])


def _moe(h2, route, x1, gt, sc, sh, gf, wg, wu, wd, tri, rows):
    bsz, seq, d = x1.shape
    nb, rb = _token_tile(bsz, seq, rows)
    tm = nb * rb
    assert tm % LANES == 0 and (rb % LANES == 0 or LANES % rb == 0)
    n_seq_tiles = seq // rb
    cap = tm + N_GROUPS * (MOE_SUB_ROWS // 2)
    tokd = pl.BlockSpec((nb, rb, d), lambda b, i, g: (b, i, 0))
    vec = pl.BlockSpec((nb, 1, d), lambda b, i, g: (b, 0, 0))
    group_w = lambda shape: pl.BlockSpec((EXPERTS_PER_GROUP,) + shape, lambda b, i, g: (g, 0, 0))
    return pl.pallas_call(
        _moe_kernel,
        grid=(bsz // nb, n_seq_tiles, N_GROUPS),
        in_specs=[tokd, pl.BlockSpec((ROUTE_ROWS, tm), lambda b, i, g: (0, b * n_seq_tiles + i)),
                  tokd, vec, vec, vec,
                  pl.BlockSpec((1, d), lambda b, i, g: (0, 0)),
                  pl.BlockSpec(tri.shape, lambda b, i, g: (0, 0)),
                  group_w((d, D_EXPERT)), group_w((d, D_EXPERT)), group_w((D_EXPERT, d))],
        out_specs=tokd,
        out_shape=jax.ShapeDtypeStruct((bsz, seq, d), F32),
        scratch_shapes=[pltpu.VMEM((cap, d), BF16), pltpu.VMEM((1, tm), jnp.int32),
                        pltpu.SMEM((3 * N_GROUPS,), jnp.int32)],
        compiler_params=_params("parallel", "parallel", "arbitrary"),
        name="moe_final_norm",
    )(h2, route, x1, gt, sc, sh, gf, tri, wg, wu, wd)


def _block_ones(n, blk, lower):
    i = np.arange(n)
    m = (i[:, None] // blk) == (i[None, :] // blk)
    if lower:
        m = m & (i[None, :] <= i[:, None])
    return jnp.asarray(m, dtype=BF16)


def _consts(tr, tg):
    return dict(seg=_block_ones(RWKV_WIDTH, RWKV_HEAD, False), tri=_block_ones(tr, CHUNK, True),
                tri_gla=_block_ones(tg, CHUNK, True), tri_up=_block_ones(LANES, LANES, True).T)


def _heads_block_diag(s):
    n, c = s.shape[2], s.shape[-1]
    rows = [jnp.pad(s[:, :, j], ((0, 0), (0, 0), (0, 0), (j * c, (n - 1 - j) * c))) for j in range(n)]
    return jnp.concatenate(rows, axis=2)


def _diag_blocks(t, n):
    r, c = t.shape[-2] // n, t.shape[-1] // n
    return jnp.stack([t[:, :, j * r:(j + 1) * r, j * c:(j + 1) * c] for j in range(n)], axis=2)


def _pad_rows(w, first_row):
    out = jnp.zeros((LORA_PAD, w.shape[1]), F32)
    return lax.dynamic_update_slice(out, w, (first_row, 0)).astype(BF16)


def _layer_params(l, w_in, mu_shift, w0, w_decay_up, a0, w_a_up, w_g_up, k_k, k_a, r_k, lnx_w, lnx_b,
                  w_gla_gate_up, b_gla_gate, gla_norm_g, w_out, norm2_g,
                  w_router_group, b_router_group, w_router_expert, b_router_expert):
    w_in_p = _win_layout(w_in[l].T, 256)
    n_pad = ROUTE_ROWS - N_EXPERTS - N_GROUPS
    w_router = jnp.concatenate([w_router_expert[l].T, w_router_group[l].T, jnp.zeros((n_pad, D_MODEL), F32)])
    b_router = jnp.concatenate([b_router_expert[l], b_router_group[l],
                                jnp.zeros((n_pad,), F32)]).reshape(ROUTE_ROWS, 1)
    wgate = jnp.zeros((LANES, GLA_KEY_WIDTH), F32).at[:GLA_GATE_RANK].set(w_gla_gate_up[l]).astype(BF16)
    r1 = lambda a: a.reshape(1, -1)
    return dict(
        w_in=w_in_p, mu=r1(mu_shift[l]), w0=r1(w0[l]), a0=r1(a0[l]), k_k=r1(k_k[l]), k_a=r1(k_a[l]),
        r_k=r1(r_k[l]), wd=_pad_rows(w_decay_up[l], 0), wa=_pad_rows(w_a_up[l], DECAY_LORA),
        wg=_pad_rows(w_g_up[l], DECAY_LORA + AAA_LORA), lnx_w=r1(lnx_w[l]), lnx_b=r1(lnx_b[l]),
        wgate=wgate, bgate=r1(b_gla_gate[l]), gla_g=r1(gla_norm_g[l]),
        w_out=w_out[l].astype(BF16), norm2_g=r1(norm2_g[l]), w_router=w_router, b_router=b_router)


def _run_layer(x, mod, shift0, wkv0, gla0, lp, experts, final):
    bsz, seq, d = x.shape
    assert seq % CHUNK == 0
    bb = 2 if seq >= RWKV_STEP_ROWS else min(bsz, RECURRENT_MAX_SEQS)
    assert bsz % bb == 0
    tg = min(seq, GLA_STEP_ROWS)
    consts = _consts(TOKEN_TILE_ROWS, tg)
    m = lambda j: mod[:, j:j + 1, :]
    sh1, sc1, gt1, sh2, sc2, gt2 = (m(j) for j in range(6))
    proj, tails = _in_proj(x, sc1, sh1, lp["norm1_g"], lp["w_in"], TOKEN_TILE_ROWS)
    tails = tails.reshape(bsz, -1, RWKV_PROJ)
    new_shift = tails[:, -1]
    prev = jnp.concatenate([shift0[:, None, :], tails[:, :-1]], axis=1).reshape(-1, 1, RWKV_PROJ)
    ab, rkv, bg, gl = _rwkv_prep(proj, prev, lp, consts, TOKEN_TILE_ROWS)
    gl = gl.reshape(bsz, seq // CHUNK, RWKV_WIDTH)
    s0 = _heads_block_diag(wkv0.reshape(bsz, RWKV_TILES, RWKV_TILE_HEADS, RWKV_HEAD, RWKV_HEAD))
    yr, s_bd = _rwkv_chunk(ab, rkv, gl[:, :, None, :], s0, bb, min(seq, RWKV_STEP_ROWS) // CHUNK)
    new_wkv = _diag_blocks(s_bd, RWKV_TILE_HEADS).reshape(bsz, RWKV_HEADS, RWKV_HEAD, RWKV_HEAD)
    t0 = _heads_block_diag(jnp.swapaxes(gla0, -1, -2).reshape(bsz, GLA_PAIRS, 2, GLA_DV, GLA_DK))
    yg, t_bd = _gla(proj, lp, consts["tri_gla"], t0, bb, tg)
    new_gla = jnp.swapaxes(_diag_blocks(t_bd, 2).reshape(bsz, GLA_HEADS, GLA_DV, GLA_DK), -1, -2)
    x1, h2, route = _out_proj(yr, bg, yg, x, gt1, sc2, sh2, lp, consts, TOKEN_TILE_ROWS)
    out = _moe(h2, route, x1, gt2, *final, *experts, consts["tri_up"], MOE_TILE_ROWS)
    return out, new_shift, new_wkv, new_gla


def kernel(x_prompt, x_sample, c_prompt, c_sample, state_rwkv_shift, state_rwkv_wkv, state_gla_kv, w_ada, b_ada, norm1_g, norm2_g, w_in, mu_shift, w0, w_decay_up, a0, w_a_up, w_g_up, k_k, k_a, r_k, lnx_w, lnx_b, w_gla_gate_up, b_gla_gate, gla_norm_g, w_out, w_router_group, b_router_group, w_router_expert, b_router_expert, w_expert_gate, w_expert_up, w_expert_down, w_ada_final, b_ada_final, normf_g):
    assert w_ada.shape[0] == 1, "the final norm is fused into the single layer's MoE kernel"
    bp, bs = x_prompt.shape[0], x_sample.shape[0]
    d = D_MODEL
    n_rows = -(-(bp + bs) // 8) * 8
    c_all = jnp.zeros((n_rows, d), F32).at[:bp].set(c_prompt).at[bp:bp + bs].set(c_sample)
    modf = _modulation(c_all, w_ada_final, b_ada_final, 1024).reshape(n_rows, 2, d)
    mod = _modulation(c_all, w_ada, b_ada[0], 1536, layer=0).reshape(n_rows, 6, d)
    lp = _layer_params(0, w_in, mu_shift, w0, w_decay_up, a0, w_a_up, w_g_up, k_k, k_a, r_k, lnx_w,
                       lnx_b, w_gla_gate_up, b_gla_gate, gla_norm_g, w_out, norm2_g,
                       w_router_group, b_router_group, w_router_expert, b_router_expert)
    lp["norm1_g"] = norm1_g[0].reshape(1, d)
    experts = (w_expert_gate[0].astype(BF16), w_expert_up[0].astype(BF16), w_expert_down[0].astype(BF16))
    groups = [
        (x_prompt, 0, bp, jnp.zeros((bp, RWKV_PROJ), F32),
         jnp.zeros((bp, RWKV_HEADS, RWKV_HEAD, RWKV_HEAD), F32), jnp.zeros((bp, GLA_HEADS, GLA_DK, GLA_DV), F32)),
        (x_sample, bp, bp + bs, state_rwkv_shift[0], state_rwkv_wkv[0], state_gla_kv[0]),
    ]
    ys, states = [], []
    for x, lo, hi, shift0, wkv0, gla0 in groups:
        final = (modf[lo:hi, 1:2], modf[lo:hi, 0:1], normf_g.reshape(1, d))
        y, *st = _run_layer(x, mod[lo:hi], shift0, wkv0, gla0, lp, experts, final)
        ys.append(y)
        states.extend(s[None] for s in st)
    return tuple(ys + states)
```

```python
import functools

import jax
import jax.numpy as jnp
import numpy as np
from jax import lax
from jax.experimental import pallas as pl
from jax.experimental.pallas import tpu as pltpu

F32 = jnp.float32
BF16 = jnp.bfloat16

LANES = 128
VMEM_LIMIT_BYTES = 56 * 1024 * 1024
TOKEN_TILE_ROWS = 512
MOE_TILE_ROWS = 1024
MOE_SUB_ROWS = 256
GLA_STEP_ROWS = 512
RECURRENT_MAX_SEQS = 8
RWKV_STEP_ROWS = 256

D_MODEL = 1024
CHUNK = 64
RWKV_WIDTH = 512
RWKV_HEAD = 64
RWKV_HEADS = RWKV_WIDTH // RWKV_HEAD
HEAD_LANES = RWKV_HEAD
HEAD_SHIFT = HEAD_LANES.bit_length() - 1
assert 1 << HEAD_SHIFT == HEAD_LANES
RWKV_TILE_LANES = 128
RWKV_TILE_HEADS = RWKV_TILE_LANES // RWKV_HEAD
RWKV_TILES = RWKV_WIDTH // RWKV_TILE_LANES
DECAY_LORA = 32
AAA_LORA = 32
GATE_LORA = 64
LORA_PAD = DECAY_LORA + AAA_LORA + GATE_LORA
RWKV_PROJ = 3 * RWKV_WIDTH + LORA_PAD
GLA_WIDTH = 512
GLA_HEADS = 4
GLA_PAIRS = GLA_HEADS // 2
GLA_DV = GLA_WIDTH // GLA_HEADS
GLA_DK = GLA_DV // 2
GLA_KEY_WIDTH = GLA_HEADS * GLA_DK
GLA_GATE_RANK = 16
GLA_TAU = 16.0
GLA_PROJ = 2 * GLA_KEY_WIDTH + 2 * GLA_WIDTH + GLA_GATE_RANK
GLA_PROJ_PAD = RWKV_PROJ
IN_PROJ_PAD = RWKV_PROJ + GLA_PROJ_PAD
N_GROUPS = 4
EXPERTS_PER_GROUP = 4
N_EXPERTS = N_GROUPS * EXPERTS_PER_GROUP
GROUP_SHIFT = EXPERTS_PER_GROUP.bit_length() - 1
assert 1 << GROUP_SHIFT == EXPERTS_PER_GROUP
ROUTE_ROWS = 24
D_EXPERT = 512
RMS_EPS = 1e-6
LNX_EPS = 64e-5
LOG2_E = 1.4426950408889634

_NN = (((1,), (0,)), ((), ()))
_NT = (((1,), (1,)), ((), ()))
_TN = (((0,), (0,)), ((), ()))


def _dg(a, b, dims=_NN):
    return lax.dot_general(a, b, dims, preferred_element_type=F32)


def _split2(x):
    hi = x.astype(BF16)
    lo = (x - hi.astype(F32)).astype(BF16)
    return hi, lo


def _split3(x):
    hi = x.astype(BF16)
    r1 = x - hi.astype(F32)
    mid = r1.astype(BF16)
    lo = (r1 - mid.astype(F32)).astype(BF16)
    return hi, mid, lo


def _mm1(a, b, dims=_NN):
    return _dg(a.astype(BF16), b.astype(BF16), dims)


def _mm3(a, b, dims=_NN):
    ah, al = _split2(a)
    bh, bl = _split2(b)
    return _dg(ah, bh, dims) + (_dg(ah, bl, dims) + _dg(al, bh, dims))


def _mm_exact_lhs(e, x, dims=_NN):
    h, m, l = _split3(x)
    return _dg(e, h, dims) + (_dg(e, m, dims) + _dg(e, l, dims))


def _mm2_exact_rhs(x, e, dims=_NN):
    h, l = _split2(x)
    return _dg(h, e, dims) + _dg(l, e, dims)


def _softplus(z):
    return jnp.maximum(z, 0.0) + jnp.log(1.0 + jnp.exp(-jnp.abs(z)))


def _sigmoid(z):
    return 1.0 / (1.0 + jnp.exp(-z))


def _silu(z):
    return z * _sigmoid(z)


def _params(*sem):
    return pltpu.CompilerParams(dimension_semantics=sem, vmem_limit_bytes=VMEM_LIMIT_BYTES)


def _mod_kernel(c_ref, w_ref, b_ref, o_ref):
    o_ref[...] = _mm1(_silu(c_ref[...]), w_ref[...]) + b_ref[...]


def _modulation(c, w, b, tn, layer=None):
    rows, d = c.shape
    n = w.shape[-1]
    if layer is None:
        w_spec = pl.BlockSpec((d, tn), lambda j: (0, j))
    else:
        w_spec = pl.BlockSpec((None, d, tn), lambda j: (layer, 0, j))
    return pl.pallas_call(
        _mod_kernel,
        grid=(n // tn,),
        in_specs=[pl.BlockSpec((rows, d), lambda j: (0, 0)), w_spec,
                  pl.BlockSpec((1, tn), lambda j: (0, j))],
        out_specs=pl.BlockSpec((rows, tn), lambda j: (0, j)),
        out_shape=jax.ShapeDtypeStruct((rows, n), F32),
        compiler_params=_params("parallel"),
        name="modulation",
    )(c, w, b.reshape(1, n))


def _win_layout_kernel(w_ref, o_ref):
    qkv_end = RWKV_PROJ + 2 * GLA_KEY_WIDTH + GLA_WIDTH
    gate_end = qkv_end + GLA_GATE_RANK
    o_ref[:qkv_end] = w_ref[:qkv_end].astype(BF16)
    o_ref[qkv_end:qkv_end + GLA_WIDTH] = w_ref[gate_end:gate_end + GLA_WIDTH].astype(BF16)
    o_ref[qkv_end + GLA_WIDTH:RWKV_PROJ + GLA_PROJ] = w_ref[qkv_end:gate_end].astype(BF16)
    o_ref[RWKV_PROJ + GLA_PROJ:] = jnp.zeros((IN_PROJ_PAD - RWKV_PROJ - GLA_PROJ, o_ref.shape[1]), BF16)


def _win_layout(wt, cols):
    n, d = wt.shape
    return pl.pallas_call(
        _win_layout_kernel,
        grid=(d // cols,),
        in_specs=[pl.BlockSpec((n, cols), lambda i: (0, i))],
        out_specs=pl.BlockSpec((IN_PROJ_PAD, cols), lambda i: (0, i)),
        out_shape=jax.ShapeDtypeStruct((IN_PROJ_PAD, d), BF16),
        compiler_params=_params("parallel"),
        name="w_in_layout",
    )(wt)


def _rms_mod(x, g, sc, sh):
    ms = jnp.mean(x * x, axis=-1, keepdims=True)
    return (x * lax.rsqrt(ms + RMS_EPS) * g) * (1.0 + sc) + sh


def _token_tile(bsz, seq, rows):
    if seq >= rows:
        assert seq % rows == 0
        return 1, rows
    nb = min(bsz, rows // seq)
    assert bsz % nb == 0
    return nb, seq


def _inproj_kernel(x_ref, sc_ref, sh_ref, g_ref, w_ref, o_ref, last_ref, *, n_step):
    nb, rb, d = x_ref.shape
    h = _rms_mod(x_ref[...], g_ref[...], sc_ref[...], sh_ref[...])
    hb = h.reshape(nb * rb, d).astype(BF16)
    for j in range(IN_PROJ_PAD // n_step):
        cols = slice(j * n_step, (j + 1) * n_step)
        o_ref[:, :, cols] = _dg(hb, w_ref[cols, :], _NT).reshape(nb, rb, n_step)
    last_ref[...] = o_ref[:, rb - 1:rb, :RWKV_PROJ]


def _in_proj(x, sc, sh, g, w, rows):
    bsz, seq, d = x.shape
    nb, rb = _token_tile(bsz, seq, rows)
    n_seq_tiles = seq // rb
    vec = pl.BlockSpec((nb, 1, d), lambda b, i: (b, 0, 0))
    return pl.pallas_call(
        functools.partial(_inproj_kernel, n_step=2 * LANES),
        grid=(bsz // nb, n_seq_tiles),
        in_specs=[pl.BlockSpec((nb, rb, d), lambda b, i: (b, i, 0)), vec, vec,
                  pl.BlockSpec((1, d), lambda b, i: (0, 0)),
                  pl.BlockSpec((IN_PROJ_PAD, d), lambda b, i: (0, 0))],
        out_specs=[pl.BlockSpec((nb, rb, IN_PROJ_PAD), lambda b, i: (b, i, 0)),
                   pl.BlockSpec((nb, 1, RWKV_PROJ), lambda b, i: (b * n_seq_tiles + i, 0, 0))],
        out_shape=[jax.ShapeDtypeStruct((bsz, seq, IN_PROJ_PAD), F32),
                   jax.ShapeDtypeStruct((bsz * n_seq_tiles, 1, RWKV_PROJ), F32)],
        compiler_params=_params("parallel", "parallel"),
        name="norm1_in_proj",
    )(x, sc, sh, g, w)


def _rwkv_prep_kernel(p_ref, prev_ref, mu_ref, w0_ref, a0_ref, kk_ref, ka_ref, rk_ref,
                      wd_ref, wa_ref, wg_ref, seg_ref, tri_ref,
                      ab_ref, rkv_ref, bg_ref, gl_ref):
    nb, rb, wp = p_ref.shape
    tr = nb * rb
    p = p_ref[...].reshape(tr, wp)
    row = lax.broadcasted_iota(jnp.int32, (nb, rb, wp), 1)
    xx = jnp.where(row == 0, prev_ref[...], pltpu.roll(p, 1, 0).reshape(nb, rb, wp)).reshape(tr, wp)
    ps = p + (xx - p) * mu_ref[...]
    w = RWKV_WIDTH
    r, k, v, lora = ps[:, :w], ps[:, w:2 * w], ps[:, 2 * w:3 * w], ps[:, 3 * w:]
    logw = -_softplus(-(w0_ref[...] + _mm1(jnp.tanh(lora), wd_ref[...]))) - 0.5
    lw = jnp.exp(logw) * (-LOG2_E)
    a = _sigmoid(a0_ref[...] + _mm1(lora, wa_ref[...]))
    g = _mm1(_sigmoid(lora), wg_ref[...])
    seg = seg_ref[...]
    kk = k * kk_ref[...]
    kk = kk / jnp.maximum(jnp.sqrt(_mm2_exact_rhs(kk * kk, seg)), 1e-12)
    k2 = k * (1.0 + (a - 1.0) * ka_ref[...])
    cum = _mm_exact_lhs(tri_ref[...], lw)
    lasts = [jnp.exp2(cum[c * CHUNK + CHUNK - 1:(c + 1) * CHUNK, :]) for c in range(tr // CHUNK)]
    for c, l in enumerate(lasts):
        gl_ref[c:c + 1, :] = l

    def to_chunk_end(val):
        return jnp.concatenate([val[c * CHUNK:(c + 1) * CHUNK] * l for c, l in enumerate(lasts)], axis=0)

    ginv = jnp.exp2(-cum)
    bt = (kk * a) * ginv
    kt = k2 * ginv

    def put(ref, slot, val):
        ref[:, :, slot * w:(slot + 1) * w] = val.reshape(nb, rb, w).astype(ref.dtype)

    put(ab_ref, 0, -kk * jnp.exp2(cum - lw))
    put(ab_ref, 1, bt)
    put(rkv_ref, 0, r * jnp.exp2(cum))
    put(rkv_ref, 1, kt)
    put(rkv_ref, 2, to_chunk_end(bt))
    put(rkv_ref, 3, to_chunk_end(kt))
    put(rkv_ref, 4, v)
    put(bg_ref, 0, _mm2_exact_rhs(r * k2 * rk_ref[...], seg) * v)
    put(bg_ref, 1, g)


def _rwkv_prep(proj, prev, lp, consts, rows):
    bsz, seq, _ = proj.shape
    w = RWKV_WIDTH
    nb, rb = _token_tile(bsz, seq, rows)
    n_seq_tiles = seq // rb
    flat = lambda b, i: (b * n_seq_tiles + i, 0)
    row = lambda n: pl.BlockSpec((1, n), lambda b, i: (0, 0))
    full = lambda a: pl.BlockSpec(a.shape, lambda b, i: (0,) * a.ndim)
    packed = [(2, F32), (5, BF16), (2, F32)]
    return pl.pallas_call(
        _rwkv_prep_kernel,
        grid=(bsz // nb, n_seq_tiles),
        in_specs=[pl.BlockSpec((nb, rb, RWKV_PROJ), lambda b, i: (b, i, 0)),
                  pl.BlockSpec((nb, 1, RWKV_PROJ), lambda b, i: flat(b, i) + (0,)),
                  row(RWKV_PROJ), row(w), row(w), row(w), row(w), row(w),
                  full(lp["wd"]), full(lp["wa"]), full(lp["wg"]), full(consts["seg"]), full(consts["tri"])],
        out_specs=[pl.BlockSpec((nb, rb, k * w), lambda b, i: (b, i, 0)) for k, _ in packed]
        + [pl.BlockSpec((nb * rb // CHUNK, w), flat)],
        out_shape=[jax.ShapeDtypeStruct((bsz, seq, k * w), dt) for k, dt in packed]
        + [jax.ShapeDtypeStruct((bsz * seq // CHUNK, w), F32)],
        compiler_params=_params("parallel", "parallel"),
        name="rwkv_prep",
    )(proj, prev, lp["mu"], lp["w0"], lp["a0"], lp["k_k"], lp["k_a"], lp["r_k"],
      lp["wd"], lp["wa"], lp["wg"], consts["seg"], consts["tri"])


def _head_masks(width):
    lane = lax.broadcasted_iota(jnp.int32, (CHUNK, width), 1)
    row = lax.broadcasted_iota(jnp.int32, (CHUNK, width), 0)
    return lane >> HEAD_SHIFT, row, lane & (HEAD_LANES - 1)


def _block_diag(x, head):
    z = jnp.zeros_like(x)
    return jnp.concatenate([jnp.where(head == j, x, z) for j in range(x.shape[1] // HEAD_LANES)], axis=0)


def _rwkv_chunk_kernel(ab_ref, rkv_ref, gl_ref, s0_ref, y_ref, s_ref, *, bb, nck):
    @pl.when(pl.program_id(1) == 0)
    def _():
        s_ref[...] = s0_ref[...]

    tw = RWKV_TILE_LANES
    head, row, col = _head_masks(tw)
    strict = col < row
    incl = col <= row
    same8 = (col >> 3) == (row >> 3)
    lane2 = lax.broadcasted_iota(jnp.int32, (tw, tw), 1)
    row2 = lax.broadcasted_iota(jnp.int32, (tw, tw), 0)
    same_head = (lane2 >> HEAD_SHIFT) == (row2 >> HEAD_SHIFT)
    bd = functools.partial(_block_diag, head=head)
    c = CHUNK

    def pmm(p, q):
        return _dg(p.astype(BF16), bd(q.astype(BF16)))

    items = [(b, ck, pr) for b in range(bb) for ck in range(nck) for pr in range(RWKV_TILES)]
    n = range(len(items))
    rows = lambda ck: slice(ck * c, (ck + 1) * c)
    lanes = lambda pr: slice(pr * tw, (pr + 1) * tw)
    w = RWKV_WIDTH

    def ld(ref, slot):
        return [ref[b, rows(ck), slot * w + pr * tw:slot * w + (pr + 1) * tw] for b, ck, pr in items]

    cat0 = lambda *xs: jnp.concatenate(xs, axis=0)
    cat1 = lambda *xs: jnp.concatenate(xs, axis=1)
    at, bt = ld(ab_ref, 0), ld(ab_ref, 1)
    rt, kt, be, ke, v = (ld(rkv_ref, slot) for slot in range(5))
    ats, bts = [_split2(a) for a in at], [_split2(b) for b in bt]
    atb = [hi for hi, _ in ats]
    zero = jnp.zeros((c, tw), F32)
    gk = [_dg(cat0(atb[i], rt[i]), bd(kt[i]), _NT) for i in n]
    aak = [jnp.where(strict, gk[i][:c], zero) for i in n]
    ark = [jnp.where(incl, gk[i][c:], zero) for i in n]
    gb = [_dg(cat0(ats[i][0], ats[i][1], rt[i]), bd(bts[i][0]), _NT) for i in n]
    arb = [jnp.where(incl, gb[i][2 * c:], zero) for i in n]
    aab = [jnp.where(strict, gb[i][:c] + (gb[i][c:2 * c] + _dg(atb[i], bd(bts[i][1]), _NT)), zero) for i in n]
    z = [pmm(aak[i], v[i]) for i in n]
    a8 = [jnp.where(same8, aab[i], zero) for i in n]
    p2 = [pmm(a8[i], a8[i]) for i in n]
    p4 = [pmm(p2[i], p2[i]) for i in n]
    nn = [a8[i] + p2[i] + pmm(p2[i], a8[i]) for i in n]
    nn = [nn[i] + p4[i] + pmm(p4[i], nn[i]) for i in n]
    for lvl in (3, 4, 5):
        joins = ((col >> (lvl + 1)) == (row >> (lvl + 1))) & ((col >> lvl) != (row >> lvl))
        e = [jnp.where(joins, aab[i], zero) for i in n]
        te = [e[i] + pmm(nn[i], e[i]) for i in n]
        nn = [nn[i] + te[i] + pmm(te[i], nn[i]) for i in n]
    wu = [cat1(at[i], z[i]) + _dg(nn[i].astype(BF16), cat1(bd(atb[i]), bd(z[i].astype(BF16)))) for i in n]
    abk = [cat1(arb[i], ark[i]).astype(BF16) for i in n]
    s = {(b, pr): s_ref[b, pr] for b in range(bb) for pr in range(RWKV_TILES)}
    for ck in range(nck):
        cur = [i for i in n if items[i][1] == ck]
        key = lambda i: (items[i][0], items[i][2])
        x = {i: _dg(cat0(wu[i][:, :tw].astype(BF16), rt[i]), s[key(i)].astype(BF16), _NT) for i in cur}
        ub = {i: (x[i][:c] + wu[i][:, tw:]).astype(BF16) for i in cur}
        upd = {i: _dg(cat0(ub[i], v[i]), cat0(be[i], ke[i]), _TN) for i in cur}
        for i in cur:
            b, _, pr = items[i]
            y_ref[b, rows(ck), lanes(pr)] = x[i][c:] + _dg(abk[i], cat0(bd(ub[i]), bd(v[i])))
            s[b, pr] = (s[b, pr] * gl_ref[b, ck, :, lanes(pr)]
                        + jnp.where(same_head, upd[i], jnp.zeros_like(upd[i])))
    for (b, pr), val in s.items():
        s_ref[b, pr] = val


def _rwkv_chunk(ab, rkv, gl, s0, bb, nck):
    bsz, seq, _ = ab.shape
    w = RWKV_WIDTH
    tok = lambda a: pl.BlockSpec((bb, nck * CHUNK, a.shape[-1]), lambda i, c: (i, c, 0))
    st = pl.BlockSpec((bb, RWKV_TILES, RWKV_TILE_LANES, RWKV_TILE_LANES), lambda i, c: (i, 0, 0, 0))
    return pl.pallas_call(
        functools.partial(_rwkv_chunk_kernel, bb=bb, nck=nck),
        grid=(bsz // bb, seq // (nck * CHUNK)),
        in_specs=[tok(ab), tok(rkv), pl.BlockSpec((bb, nck, 1, w), lambda i, c: (i, c, 0, 0)), st],
        out_specs=[pl.BlockSpec((bb, nck * CHUNK, w), lambda i, c: (i, c, 0)), st],
        out_shape=[jax.ShapeDtypeStruct((bsz, seq, w), F32),
                   jax.ShapeDtypeStruct((bsz, RWKV_TILES, RWKV_TILE_LANES, RWKV_TILE_LANES), F32)],
        compiler_params=_params("parallel", "arbitrary"),
        name="rwkv_chunk",
    )(ab, rkv, gl, s0)


def _gla_kernel(p_ref, wgate_ref, bgate_ref, tri_ref, ng_ref, s0_ref, y_ref, s_ref, *, bb, nck):
    @pl.when(pl.program_id(1) == 0)
    def _():
        s_ref[...] = s0_ref[...]

    head, row, col = _head_masks(LANES)
    incl = col <= row
    bd = functools.partial(_block_diag, head=head)
    lane2 = lax.broadcasted_iota(jnp.int32, (2 * GLA_DV, LANES), 1)
    row2 = lax.broadcasted_iota(jnp.int32, (2 * GLA_DV, LANES), 0)
    same_head = (lane2 < GLA_DK) == (row2 < GLA_DV)
    kw, gw = GLA_KEY_WIDTH, GLA_WIDTH
    zero = jnp.zeros((CHUNK, LANES), F32)
    cat0 = lambda xs: jnp.concatenate(xs, axis=0)
    cat1 = lambda xs: jnp.concatenate(xs, axis=1)
    rows = lambda c: slice(c * CHUNK, (c + 1) * CHUNK)
    lanes = lambda pr: slice(pr * LANES, (pr + 1) * LANES)
    bs, cs = range(bb), range(nck)
    x = [p_ref[b] for b in bs]
    gate = [_mm1(x[b][:, 2 * kw + 2 * gw:], wgate_ref[...]) + bgate_ref[...] for b in bs]
    la = [(jnp.minimum(g, 0.0) - jnp.log(1.0 + jnp.exp(-jnp.abs(g)))) / GLA_TAU for g in gate]
    cum = [_mm_exact_lhs(tri_ref[...], la[b]) for b in bs]
    q_dec = [(x[b][:, :kw] * (GLA_DK ** -0.5)) * jnp.exp(cum[b]) for b in bs]
    k_inv = [x[b][:, kw:2 * kw] * jnp.exp(-cum[b]) for b in bs]
    last = [[cum[b][c * CHUNK + CHUNK - 1:(c + 1) * CHUNK, :] for c in cs] for b in bs]
    k_end = [[x[b][rows(c), kw:2 * kw] * jnp.exp(last[b][c] - cum[b][rows(c)]) for c in cs] for b in bs]
    dec = [[jnp.exp(last[b][c]) for c in cs] for b in bs]
    items = [(b, c, pr) for b in bs for c in cs for pr in range(GLA_PAIRS)]
    vp = {(b, c, pr): x[b][rows(c), 2 * kw + pr * 2 * GLA_DV:2 * kw + (pr + 1) * 2 * GLA_DV]
          for b, c, pr in items}
    att = {(b, c, pr): jnp.where(incl, _mm1(q_dec[b][rows(c), lanes(pr)],
                                            bd(k_inv[b][rows(c), lanes(pr)]), _NT), zero)
           for b, c, pr in items}
    upd = {it: _mm1(vp[it], k_end[it[0]][it[1]][:, lanes(it[2])], _TN) for it in items}
    intra = {it: _mm1(att[it], cat0([cat1([vp[it][:, :GLA_DV], zero]), cat1([zero, vp[it][:, GLA_DV:]])]))
             for it in items}
    st = {}
    for b in bs:
        for pr in range(GLA_PAIRS):
            cur = s_ref[b, pr]
            for c in cs:
                st[b, c, pr] = cur
                cur = cur * dec[b][c][:, lanes(pr)] + jnp.where(same_head, upd[b, c, pr], jnp.zeros_like(cur))
            s_ref[b, pr] = cur
    o = {(b, c, pr): intra[b, c, pr] + _mm1(q_dec[b][rows(c), lanes(pr)], st[b, c, pr], _NT)
         for b, c, pr in items}
    for b in bs:
        ob = cat0([cat1([o[b, c, pr] for pr in range(GLA_PAIRS)]) for c in cs])
        heads = [ob[:, h * GLA_DV:(h + 1) * GLA_DV] for h in range(GLA_HEADS)]
        normed = [oh * lax.rsqrt(jnp.mean(oh * oh, axis=-1, keepdims=True) + RMS_EPS) for oh in heads]
        y_ref[b] = cat1(normed) * ng_ref[...] * _silu(x[b][:, 2 * kw + gw:2 * kw + 2 * gw])


def _gla(proj, lp, tri, s0, bb, tg):
    bsz, seq, _ = proj.shape
    full = lambda a: pl.BlockSpec(a.shape, lambda i, c: (0,) * a.ndim)
    st = pl.BlockSpec((bb, GLA_PAIRS, 2 * GLA_DV, LANES), lambda i, c: (i, 0, 0, 0))
    return pl.pallas_call(
        functools.partial(_gla_kernel, bb=bb, nck=tg // CHUNK),
        grid=(bsz // bb, seq // tg),
        in_specs=[pl.BlockSpec((bb, tg, GLA_PROJ_PAD), lambda i, c: (i, c, 1)),
                  full(lp["wgate"]), full(lp["bgate"]), full(tri), full(lp["gla_g"]), st],
        out_specs=[pl.BlockSpec((bb, tg, GLA_WIDTH), lambda i, c: (i, c, 0)), st],
        out_shape=[jax.ShapeDtypeStruct((bsz, seq, GLA_WIDTH), F32),
                   jax.ShapeDtypeStruct((bsz, GLA_PAIRS, 2 * GLA_DV, LANES), F32)],
        compiler_params=_params("parallel", "arbitrary"),
        name="gla_chunk",
    )(proj, lp["wgate"], lp["bgate"], tri, lp["gla_g"], s0)


def _route(lgt):
    tm = lgt.shape[1]
    n_pad = ROUTE_ROWS - N_EXPERTS
    rowg = lax.broadcasted_iota(jnp.int32, (n_pad, tm), 0)
    rowe = lax.broadcasted_iota(jnp.int32, (N_EXPERTS, tm), 0)
    rowg_f, rowe_f = rowg.astype(F32), rowe.astype(F32)

    def first_argmax(vals, mx, rows_f):
        return jnp.min(jnp.where(vals == mx, rows_f, float(ROUTE_ROWS)), axis=0, keepdims=True)

    is_group = rowg < N_GROUPS
    lg = jnp.where(is_group, lgt[N_EXPERTS:], -jnp.inf)
    gmax = jnp.max(lg, axis=0, keepdims=True)
    gi = first_argmax(lg, gmax, rowg_f)
    pg_top = 1.0 / jnp.sum(jnp.where(is_group, jnp.exp(lg - gmax), 0.0), axis=0, keepdims=True)
    in_group = (rowe >> GROUP_SHIFT).astype(F32) == gi
    le = jnp.where(in_group, lgt[:N_EXPERTS], -jnp.inf)
    m1 = jnp.max(le, axis=0, keepdims=True)
    i1 = first_argmax(le, m1, rowe_f)
    le2 = jnp.where(rowe_f == i1, -jnp.inf, le)
    m2 = jnp.max(le2, axis=0, keepdims=True)
    i2 = first_argmax(le2, m2, rowe_f)
    p2 = jnp.exp(m2 - m1)
    w1 = pg_top / (1.0 + p2)
    w2 = pg_top * p2 / (1.0 + p2)
    comb = jnp.where(rowe_f == i1, w1, 0.0) + jnp.where(rowe_f == i2, w2, 0.0)
    return jnp.concatenate([comb, jnp.where(rowg == 0, gi, 0.0)], axis=0)


def _outproj_kernel(yr_ref, bg_ref, yg_ref, x_ref, gt_ref, sc_ref, sh_ref,
                    lnw_ref, lnb_ref, seg_ref, wout_ref, g2_ref, wr_ref, br_ref,
                    x1_ref, h2_ref, route_ref):
    nb, rb, d_model = x_ref.shape
    tm = nb * rb
    flat = lambda ref: ref[...].reshape(tm, ref.shape[-1])
    seg = seg_ref[...]
    y = flat(yr_ref)
    inv_n = 1.0 / RWKV_HEAD
    d = y - _mm2_exact_rhs(y, seg) * inv_n
    var = _mm2_exact_rhs(d * d, seg) * inv_n
    yn = d * lax.rsqrt(var + LNX_EPS) * lnw_ref[...] + lnb_ref[...]
    bg = flat(bg_ref)
    yr = (yn + bg[:, :RWKV_WIDTH]) * bg[:, RWKV_WIDTH:]
    mix = jnp.concatenate([yr, flat(yg_ref)], axis=1)
    x1 = x_ref[...] + gt_ref[...] * _mm1(mix, wout_ref[...]).reshape(nb, rb, d_model)
    x1_ref[...] = x1
    h2 = _rms_mod(x1, g2_ref[...], sc_ref[...], sh_ref[...])
    h2_ref[...] = h2.astype(BF16)
    route_ref[...] = _route(_mm3(wr_ref[...], h2.reshape(tm, d_model), _NT) + br_ref[...])


def _out_proj(yr, bg, yg, x, gt, sc, sh, lp, consts, rows):
    bsz, seq, d = x.shape
    nb, rb = _token_tile(bsz, seq, rows)
    n_seq_tiles = seq // rb
    half = pl.BlockSpec((nb, rb, RWKV_WIDTH), lambda b, i: (b, i, 0))
    tokd = pl.BlockSpec((nb, rb, d), lambda b, i: (b, i, 0))
    vec = pl.BlockSpec((nb, 1, d), lambda b, i: (b, 0, 0))
    full = lambda a: pl.BlockSpec(a.shape, lambda b, i: (0,) * a.ndim)
    args = (lp["lnx_w"], lp["lnx_b"], consts["seg"], lp["w_out"], lp["norm2_g"], lp["w_router"], lp["b_router"])
    return pl.pallas_call(
        _outproj_kernel,
        grid=(bsz // nb, seq // rb),
        in_specs=[half, pl.BlockSpec((nb, rb, 2 * RWKV_WIDTH), lambda b, i: (b, i, 0)), half, tokd,
                  vec, vec, vec] + [full(a) for a in args],
        out_specs=[tokd, tokd, pl.BlockSpec((ROUTE_ROWS, nb * rb), lambda b, i: (0, b * n_seq_tiles + i))],
        out_shape=[jax.ShapeDtypeStruct((bsz, seq, d), F32),
                   jax.ShapeDtypeStruct((bsz, seq, d), BF16),
                   jax.ShapeDtypeStruct((ROUTE_ROWS, bsz * seq), F32)],
        compiler_params=_params("parallel", "parallel"),
        name="out_proj_router",
    )(yr, bg, yg, x, gt, sc, sh, *args)


def _moe_kernel(h_ref, route_ref, x1_ref, gt_ref, sc_ref, sh_ref, gf_ref, tri_ref, wg_ref, wu_ref, wd_ref,
                y_ref, ys_ref, pos_ref, meta_ref):
    g = pl.program_id(2)
    nb, rb, d = h_ref.shape
    tm = nb * rb
    sub, tail = MOE_SUB_ROWS, MOE_SUB_ROWS // 2
    cap = ys_ref.shape[0]
    n_lane_tiles = tm // LANES
    lane_tile = lambda k: slice(k * LANES, (k + 1) * LANES)

    @pl.when(g == 0)
    def _sort():
        row8 = lax.broadcasted_iota(jnp.int32, (8, LANES), 0).astype(F32)
        carry = jnp.zeros((8, 1), F32)
        members, ranks = [], []
        for k in range(n_lane_tiles):
            blk = jnp.where(row8 == route_ref[N_EXPERTS:N_EXPERTS + 1, lane_tile(k)], 1.0, 0.0)
            members.append(blk)
            ranks.append(_dg(blk.astype(BF16), tri_ref[...]) - blk + carry)
            carry = carry + jnp.sum(blk, axis=1, keepdims=True)
        rowc = lax.broadcasted_iota(jnp.int32, (8, 1), 0)
        lane = lax.broadcasted_iota(jnp.int32, (1, LANES), 1)
        first = jnp.zeros((1, 1), F32)
        off_col = jnp.zeros((8, 1), F32)
        meta = jnp.zeros((1, LANES), F32)
        for grp in range(N_GROUPS):
            count = jnp.sum(jnp.where(rowc == grp, carry, 0.0), axis=0, keepdims=True)
            blocks = jnp.ceil(count * (1.0 / tail))
            pairs = jnp.floor(blocks * 0.5)
            odd = blocks - 2.0 * pairs
            triple = odd * jnp.where(blocks >= 3.0, 1.0, 0.0)
            n_full = pairs - triple
            off_col = off_col + jnp.where(rowc == grp, first, 0.0)
            meta = (meta + jnp.where(lane == grp, first, 0.0) + jnp.where(lane == N_GROUPS + grp, n_full, 0.0)
                    + jnp.where(lane == 2 * N_GROUPS + grp, odd + 2.0 * triple, 0.0))
            first = first + blocks * tail
        meta = meta.astype(jnp.int32)
        for i in range(3 * N_GROUPS):
            meta_ref[i] = meta[0, i]
        for k in range(n_lane_tiles):
            pos = jnp.sum(members[k] * (ranks[k] + off_col), axis=0, keepdims=True)
            pos_ref[:, lane_tile(k)] = pos.astype(jnp.int32)
        ys_ref[...] = jnp.zeros_like(ys_ref)

    def experts(base, n_rows):
        prow = lax.broadcasted_iota(jnp.int32, (n_rows, tm), 0) + base
        onehot = jnp.where(prow == pos_ref[...], 1.0, 0.0).astype(BF16)
        hs = _dg(onehot, h_ref[...].reshape(tm, d)).astype(BF16)
        comb3 = _split3(route_ref[:N_EXPERTS, :])
        cs = _dg(onehot, comb3[0], _NT) + (_dg(onehot, comb3[1], _NT) + _dg(onehot, comb3[2], _NT))
        lane = lax.broadcasted_iota(jnp.int32, cs.shape, 1)
        ys = jnp.zeros((n_rows, d), F32)
        for e in range(EXPERTS_PER_GROUP):
            hid = _silu(_dg(hs, wg_ref[e])) * _dg(hs, wu_ref[e])
            ce = jnp.sum(jnp.where(lane == g * EXPERTS_PER_GROUP + e, cs, 0.0), axis=-1, keepdims=True)
            ys = ys + ce * _dg(hid.astype(BF16), wd_ref[e])
        ys_ref[pl.ds(base, n_rows), :] = ys.astype(BF16)

    first_row = meta_ref[g]
    n_full = meta_ref[N_GROUPS + g]

    def full_sub_tile(j, carry_):
        experts(pl.multiple_of(first_row + j * sub, tail), sub)
        return carry_

    lax.fori_loop(0, n_full, full_sub_tile, 0)

    for last_blocks in (1, 3):
        @pl.when(meta_ref[2 * N_GROUPS + g] == last_blocks)
        def _last(last_blocks=last_blocks):
            experts(pl.multiple_of(first_row + n_full * sub, tail), last_blocks * tail)

    @pl.when(g == N_GROUPS - 1)
    def _unsort():
        ys_all = ys_ref[...]
        prow = lax.broadcasted_iota(jnp.int32, (cap, LANES), 0)
        seqs = max(LANES // rb, 1)
        for k in range(n_lane_tiles):
            onehot = jnp.where(prow == pos_ref[:, lane_tile(k)], 1.0, 0.0).astype(BF16)
            moe = _dg(onehot, ys_all, _TN).reshape(seqs, LANES // seqs, d)
            b0 = k * LANES // rb
            r0 = k * LANES - b0 * rb
            bs, rs = slice(b0, b0 + seqs), slice(r0, r0 + LANES // seqs)
            x2 = x1_ref[bs, rs] + gt_ref[bs] * moe
            y_ref[bs, rs] = _rms_mod(x2, gf_ref[...], sc_ref[bs], sh_ref[bs])


def _moe(h2, route, x1, gt, sc, sh, gf, wg, wu, wd, tri, rows):
    bsz, seq, d = x1.shape
    nb, rb = _token_tile(bsz, seq, rows)
    tm = nb * rb
    assert tm % LANES == 0 and (rb % LANES == 0 or LANES % rb == 0)
    n_seq_tiles = seq // rb
    cap = tm + N_GROUPS * (MOE_SUB_ROWS // 2)
    tokd = pl.BlockSpec((nb, rb, d), lambda b, i, g: (b, i, 0))
    vec = pl.BlockSpec((nb, 1, d), lambda b, i, g: (b, 0, 0))
    group_w = lambda shape: pl.BlockSpec((EXPERTS_PER_GROUP,) + shape, lambda b, i, g: (g, 0, 0))
    return pl.pallas_call(
        _moe_kernel,
        grid=(bsz // nb, n_seq_tiles, N_GROUPS),
        in_specs=[tokd, pl.BlockSpec((ROUTE_ROWS, tm), lambda b, i, g: (0, b * n_seq_tiles + i)),
                  tokd, vec, vec, vec,
                  pl.BlockSpec((1, d), lambda b, i, g: (0, 0)),
                  pl.BlockSpec(tri.shape, lambda b, i, g: (0, 0)),
                  group_w((d, D_EXPERT)), group_w((d, D_EXPERT)), group_w((D_EXPERT, d))],
        out_specs=tokd,
        out_shape=jax.ShapeDtypeStruct((bsz, seq, d), F32),
        scratch_shapes=[pltpu.VMEM((cap, d), BF16), pltpu.VMEM((1, tm), jnp.int32),
                        pltpu.SMEM((3 * N_GROUPS,), jnp.int32)],
        compiler_params=_params("parallel", "parallel", "arbitrary"),
        name="moe_final_norm",
    )(h2, route, x1, gt, sc, sh, gf, tri, wg, wu, wd)


def _block_ones(n, blk, lower):
    i = np.arange(n)
    m = (i[:, None] // blk) == (i[None, :] // blk)
    if lower:
        m = m & (i[None, :] <= i[:, None])
    return jnp.asarray(m, dtype=BF16)


def _consts(tr, tg):
    return dict(seg=_block_ones(RWKV_WIDTH, RWKV_HEAD, False), tri=_block_ones(tr, CHUNK, True),
                tri_gla=_block_ones(tg, CHUNK, True), tri_up=_block_ones(LANES, LANES, True).T)


def _heads_block_diag(s):
    n, c = s.shape[2], s.shape[-1]
    rows = [jnp.pad(s[:, :, j], ((0, 0), (0, 0), (0, 0), (j * c, (n - 1 - j) * c))) for j in range(n)]
    return jnp.concatenate(rows, axis=2)


def _diag_blocks(t, n):
    r, c = t.shape[-2] // n, t.shape[-1] // n
    return jnp.stack([t[:, :, j * r:(j + 1) * r, j * c:(j + 1) * c] for j in range(n)], axis=2)


def _pad_rows(w, first_row):
    out = jnp.zeros((LORA_PAD, w.shape[1]), F32)
    return lax.dynamic_update_slice(out, w, (first_row, 0)).astype(BF16)


def _layer_params(l, w_in, mu_shift, w0, w_decay_up, a0, w_a_up, w_g_up, k_k, k_a, r_k, lnx_w, lnx_b,
                  w_gla_gate_up, b_gla_gate, gla_norm_g, w_out, norm2_g,
                  w_router_group, b_router_group, w_router_expert, b_router_expert):
    w_in_p = _win_layout(w_in[l].T, 256)
    n_pad = ROUTE_ROWS - N_EXPERTS - N_GROUPS
    w_router = jnp.concatenate([w_router_expert[l].T, w_router_group[l].T, jnp.zeros((n_pad, D_MODEL), F32)])
    b_router = jnp.concatenate([b_router_expert[l], b_router_group[l],
                                jnp.zeros((n_pad,), F32)]).reshape(ROUTE_ROWS, 1)
    wgate = jnp.zeros((LANES, GLA_KEY_WIDTH), F32).at[:GLA_GATE_RANK].set(w_gla_gate_up[l]).astype(BF16)
    r1 = lambda a: a.reshape(1, -1)
    return dict(
        w_in=w_in_p, mu=r1(mu_shift[l]), w0=r1(w0[l]), a0=r1(a0[l]), k_k=r1(k_k[l]), k_a=r1(k_a[l]),
        r_k=r1(r_k[l]), wd=_pad_rows(w_decay_up[l], 0), wa=_pad_rows(w_a_up[l], DECAY_LORA),
        wg=_pad_rows(w_g_up[l], DECAY_LORA + AAA_LORA), lnx_w=r1(lnx_w[l]), lnx_b=r1(lnx_b[l]),
        wgate=wgate, bgate=r1(b_gla_gate[l]), gla_g=r1(gla_norm_g[l]),
        w_out=w_out[l].astype(BF16), norm2_g=r1(norm2_g[l]), w_router=w_router, b_router=b_router)


def _run_layer(x, mod, shift0, wkv0, gla0, lp, experts, final):
    bsz, seq, d = x.shape
    assert seq % CHUNK == 0
    bb = 2 if seq >= RWKV_STEP_ROWS else min(bsz, RECURRENT_MAX_SEQS)
    assert bsz % bb == 0
    tg = min(seq, GLA_STEP_ROWS)
    consts = _consts(TOKEN_TILE_ROWS, tg)
    m = lambda j: mod[:, j:j + 1, :]
    sh1, sc1, gt1, sh2, sc2, gt2 = (m(j) for j in range(6))
    proj, tails = _in_proj(x, sc1, sh1, lp["norm1_g"], lp["w_in"], TOKEN_TILE_ROWS)
    tails = tails.reshape(bsz, -1, RWKV_PROJ)
    new_shift = tails[:, -1]
    prev = jnp.concatenate([shift0[:, None, :], tails[:, :-1]], axis=1).reshape(-1, 1, RWKV_PROJ)
    ab, rkv, bg, gl = _rwkv_prep(proj, prev, lp, consts, TOKEN_TILE_ROWS)
    gl = gl.reshape(bsz, seq // CHUNK, RWKV_WIDTH)
    s0 = _heads_block_diag(wkv0.reshape(bsz, RWKV_TILES, RWKV_TILE_HEADS, RWKV_HEAD, RWKV_HEAD))
    yr, s_bd = _rwkv_chunk(ab, rkv, gl[:, :, None, :], s0, bb, min(seq, RWKV_STEP_ROWS) // CHUNK)
    new_wkv = _diag_blocks(s_bd, RWKV_TILE_HEADS).reshape(bsz, RWKV_HEADS, RWKV_HEAD, RWKV_HEAD)
    t0 = _heads_block_diag(jnp.swapaxes(gla0, -1, -2).reshape(bsz, GLA_PAIRS, 2, GLA_DV, GLA_DK))
    yg, t_bd = _gla(proj, lp, consts["tri_gla"], t0, bb, tg)
    new_gla = jnp.swapaxes(_diag_blocks(t_bd, 2).reshape(bsz, GLA_HEADS, GLA_DV, GLA_DK), -1, -2)
    x1, h2, route = _out_proj(yr, bg, yg, x, gt1, sc2, sh2, lp, consts, TOKEN_TILE_ROWS)
    out = _moe(h2, route, x1, gt2, *final, *experts, consts["tri_up"], MOE_TILE_ROWS)
    return out, new_shift, new_wkv, new_gla


def kernel(x_prompt, x_sample, c_prompt, c_sample, state_rwkv_shift, state_rwkv_wkv, state_gla_kv, w_ada, b_ada, norm1_g, norm2_g, w_in, mu_shift, w0, w_decay_up, a0, w_a_up, w_g_up, k_k, k_a, r_k, lnx_w, lnx_b, w_gla_gate_up, b_gla_gate, gla_norm_g, w_out, w_router_group, b_router_group, w_router_expert, b_router_expert, w_expert_gate, w_expert_up, w_expert_down, w_ada_final, b_ada_final, normf_g):
    assert w_ada.shape[0] == 1, "the final norm is fused into the single layer's MoE kernel"
    bp, bs = x_prompt.shape[0], x_sample.shape[0]
    d = D_MODEL
    n_rows = -(-(bp + bs) // 8) * 8
    c_all = jnp.zeros((n_rows, d), F32).at[:bp].set(c_prompt).at[bp:bp + bs].set(c_sample)
    modf = _modulation(c_all, w_ada_final, b_ada_final, 1024).reshape(n_rows, 2, d)
    mod = _modulation(c_all, w_ada, b_ada[0], 1536, layer=0).reshape(n_rows, 6, d)
    lp = _layer_params(0, w_in, mu_shift, w0, w_decay_up, a0, w_a_up, w_g_up, k_k, k_a, r_k, lnx_w,
                       lnx_b, w_gla_gate_up, b_gla_gate, gla_norm_g, w_out, norm2_g,
                       w_router_group, b_router_group, w_router_expert, b_router_expert)
    lp["norm1_g"] = norm1_g[0].reshape(1, d)
    experts = (w_expert_gate[0].astype(BF16), w_expert_up[0].astype(BF16), w_expert_down[0].astype(BF16))
    groups = [
        (x_prompt, 0, bp, jnp.zeros((bp, RWKV_PROJ), F32),
         jnp.zeros((bp, RWKV_HEADS, RWKV_HEAD, RWKV_HEAD), F32), jnp.zeros((bp, GLA_HEADS, GLA_DK, GLA_DV), F32)),
        (x_sample, bp, bp + bs, state_rwkv_shift[0], state_rwkv_wkv[0], state_gla_kv[0]),
    ]
    ys, states = [], []
    for x, lo, hi, shift0, wkv0, gla0 in groups:
        final = (modf[lo:hi, 1:2], modf[lo:hi, 0:1], normf_g.reshape(1, d))
        y, *st = _run_layer(x, mod[lo:hi], shift0, wkv0, gla0, lp, experts, final)
        ys.append(y)
        states.extend(s[None] for s in st)
    return tuple(ys + states)
```

```python
import functools

import jax
import jax.numpy as jnp
import numpy as np
from jax import lax
from jax.experimental import pallas as pl
from jax.experimental.pallas import tpu as pltpu

F32 = jnp.float32
BF16 = jnp.bfloat16

LANES = 128
VMEM_LIMIT_BYTES = 56 * 1024 * 1024
TOKEN_TILE_ROWS = 512
MOE_TILE_ROWS = 1024
MOE_SUB_ROWS = 256
GLA_STEP_ROWS = 512
RECURRENT_MAX_SEQS = 8
MODULATION_TILE_COLS = 1024
WEIGHT_TILE_COLS = 256
RWKV_STEP_ROWS = 256

D_MODEL = 1024
CHUNK = 64
RWKV_WIDTH = 512
RWKV_HEAD = 64
RWKV_HEADS = RWKV_WIDTH // RWKV_HEAD
HEAD_LANES = RWKV_HEAD
HEAD_SHIFT = HEAD_LANES.bit_length() - 1
assert 1 << HEAD_SHIFT == HEAD_LANES
RWKV_TILE_LANES = 128
RWKV_TILE_HEADS = RWKV_TILE_LANES // RWKV_HEAD
RWKV_TILES = RWKV_WIDTH // RWKV_TILE_LANES
DECAY_LORA = 32
AAA_LORA = 32
GATE_LORA = 64
LORA_PAD = DECAY_LORA + AAA_LORA + GATE_LORA
RWKV_PROJ = 3 * RWKV_WIDTH + LORA_PAD
GLA_WIDTH = 512
GLA_HEADS = 4
GLA_PAIRS = GLA_HEADS // 2
GLA_DV = GLA_WIDTH // GLA_HEADS
GLA_DK = GLA_DV // 2
GLA_KEY_WIDTH = GLA_HEADS * GLA_DK
GLA_GATE_RANK = 16
GLA_TAU = 16.0
GLA_PROJ = 2 * GLA_KEY_WIDTH + 2 * GLA_WIDTH + GLA_GATE_RANK
GLA_PROJ_PAD = RWKV_PROJ
IN_PROJ_PAD = RWKV_PROJ + GLA_PROJ_PAD
N_GROUPS = 4
EXPERTS_PER_GROUP = 4
N_EXPERTS = N_GROUPS * EXPERTS_PER_GROUP
GROUP_SHIFT = EXPERTS_PER_GROUP.bit_length() - 1
assert 1 << GROUP_SHIFT == EXPERTS_PER_GROUP
ROUTE_ROWS = 24
D_EXPERT = 512
RMS_EPS = 1e-6
LNX_EPS = 64e-5
LOG2_E = 1.4426950408889634

_NN = (((1,), (0,)), ((), ()))
_NT = (((1,), (1,)), ((), ()))
_TN = (((0,), (0,)), ((), ()))


def _dg(a, b, dims=_NN):
    return lax.dot_general(a, b, dims, preferred_element_type=F32)


def _split2(x):
    hi = x.astype(BF16)
    lo = (x - hi.astype(F32)).astype(BF16)
    return hi, lo


def _split3(x):
    hi = x.astype(BF16)
    r1 = x - hi.astype(F32)
    mid = r1.astype(BF16)
    lo = (r1 - mid.astype(F32)).astype(BF16)
    return hi, mid, lo


def _mm1(a, b, dims=_NN):
    return _dg(a.astype(BF16), b.astype(BF16), dims)


def _mm3(a, b, dims=_NN):
    ah, al = _split2(a)
    bh, bl = _split2(b)
    return _dg(ah, bh, dims) + (_dg(ah, bl, dims) + _dg(al, bh, dims))


def _mm_exact_lhs(e, x, dims=_NN):
    h, m, l = _split3(x)
    return _dg(e, h, dims) + (_dg(e, m, dims) + _dg(e, l, dims))


def _mm2_exact_rhs(x, e, dims=_NN):
    h, l = _split2(x)
    return _dg(h, e, dims) + _dg(l, e, dims)


def _softplus(z):
    return jnp.maximum(z, 0.0) + jnp.log(1.0 + jnp.exp(-jnp.abs(z)))


def _sigmoid(z):
    return 1.0 / (1.0 + jnp.exp(-z))


def _silu(z):
    return z * _sigmoid(z)


def _params(*sem):
    return pltpu.CompilerParams(dimension_semantics=sem, vmem_limit_bytes=VMEM_LIMIT_BYTES)


def _mod_kernel(c_ref, w_ref, b_ref, o_ref):
    o_ref[...] = _mm1(_silu(c_ref[...]), w_ref[...]) + b_ref[...]


def _modulation(c, w, b, tn, layer=None):
    rows, d = c.shape
    n = w.shape[-1]
    if layer is None:
        w_spec = pl.BlockSpec((d, tn), lambda j: (0, j))
    else:
        w_spec = pl.BlockSpec((None, d, tn), lambda j: (layer, 0, j))
    return pl.pallas_call(
        _mod_kernel,
        grid=(n // tn,),
        in_specs=[pl.BlockSpec((rows, d), lambda j: (0, 0)), w_spec,
                  pl.BlockSpec((1, tn), lambda j: (0, j))],
        out_specs=pl.BlockSpec((rows, tn), lambda j: (0, j)),
        out_shape=jax.ShapeDtypeStruct((rows, n), F32),
        compiler_params=_params("parallel"),
        name="modulation",
    )(c, w, b.reshape(1, n))


def _win_layout_kernel(w_ref, o_ref):
    qkv_end = RWKV_PROJ + 2 * GLA_KEY_WIDTH + GLA_WIDTH
    gate_end = qkv_end + GLA_GATE_RANK
    o_ref[:qkv_end] = w_ref[:qkv_end].astype(BF16)
    o_ref[qkv_end:qkv_end + GLA_WIDTH] = w_ref[gate_end:gate_end + GLA_WIDTH].astype(BF16)
    o_ref[qkv_end + GLA_WIDTH:RWKV_PROJ + GLA_PROJ] = w_ref[qkv_end:gate_end].astype(BF16)
    o_ref[RWKV_PROJ + GLA_PROJ:] = jnp.zeros((IN_PROJ_PAD - RWKV_PROJ - GLA_PROJ, o_ref.shape[1]), BF16)


def _win_layout(wt, cols):
    n, d = wt.shape
    return pl.pallas_call(
        _win_layout_kernel,
        grid=(d // cols,),
        in_specs=[pl.BlockSpec((n, cols), lambda i: (0, i))],
        out_specs=pl.BlockSpec((IN_PROJ_PAD, cols), lambda i: (0, i)),
        out_shape=jax.ShapeDtypeStruct((IN_PROJ_PAD, d), BF16),
        compiler_params=_params("parallel"),
        name="w_in_layout",
    )(wt)


def _rms_mod(x, g, sc, sh):
    ms = jnp.mean(x * x, axis=-1, keepdims=True)
    return (x * lax.rsqrt(ms + RMS_EPS) * g) * (1.0 + sc) + sh


def _token_tile(bsz, seq, rows):
    if seq >= rows:
        assert seq % rows == 0
        return 1, rows
    nb = min(bsz, rows // seq)
    assert bsz % nb == 0
    return nb, seq


def _inproj_kernel(x_ref, sc_ref, sh_ref, g_ref, w_ref, o_ref, last_ref, *, n_step):
    nb, rb, d = x_ref.shape
    h = _rms_mod(x_ref[...], g_ref[...], sc_ref[...], sh_ref[...])
    hb = h.reshape(nb * rb, d).astype(BF16)
    for j in range(IN_PROJ_PAD // n_step):
        cols = slice(j * n_step, (j + 1) * n_step)
        o_ref[:, :, cols] = _dg(hb, w_ref[cols, :], _NT).reshape(nb, rb, n_step)
    last_ref[...] = o_ref[:, rb - 1:rb, :RWKV_PROJ]


def _in_proj(x, sc, sh, g, w, rows):
    bsz, seq, d = x.shape
    nb, rb = _token_tile(bsz, seq, rows)
    n_seq_tiles = seq // rb
    vec = pl.BlockSpec((nb, 1, d), lambda b, i: (b, 0, 0))
    return pl.pallas_call(
        functools.partial(_inproj_kernel, n_step=2 * LANES),
        grid=(bsz // nb, n_seq_tiles),
        in_specs=[pl.BlockSpec((nb, rb, d), lambda b, i: (b, i, 0)), vec, vec,
                  pl.BlockSpec((1, d), lambda b, i: (0, 0)),
                  pl.BlockSpec((IN_PROJ_PAD, d), lambda b, i: (0, 0))],
        out_specs=[pl.BlockSpec((nb, rb, IN_PROJ_PAD), lambda b, i: (b, i, 0)),
                   pl.BlockSpec((nb, 1, RWKV_PROJ), lambda b, i: (b * n_seq_tiles + i, 0, 0))],
        out_shape=[jax.ShapeDtypeStruct((bsz, seq, IN_PROJ_PAD), F32),
                   jax.ShapeDtypeStruct((bsz * n_seq_tiles, 1, RWKV_PROJ), F32)],
        compiler_params=_params("parallel", "parallel"),
        name="norm1_in_proj",
    )(x, sc, sh, g, w)


def _rwkv_prep_kernel(p_ref, prev_ref, mu_ref, w0_ref, a0_ref, kk_ref, ka_ref, rk_ref,
                      wd_ref, wa_ref, wg_ref, seg_ref, tri_ref,
                      ab_ref, rkv_ref, bg_ref, gl_ref):
    nb, rb, wp = p_ref.shape
    tr = nb * rb
    p = p_ref[...].reshape(tr, wp)
    row = lax.broadcasted_iota(jnp.int32, (nb, rb, wp), 1)
    xx = jnp.where(row == 0, prev_ref[...], pltpu.roll(p, 1, 0).reshape(nb, rb, wp)).reshape(tr, wp)
    ps = p + (xx - p) * mu_ref[...]
    w = RWKV_WIDTH
    r, k, v, lora = ps[:, :w], ps[:, w:2 * w], ps[:, 2 * w:3 * w], ps[:, 3 * w:]
    logw = -_softplus(-(w0_ref[...] + _mm1(jnp.tanh(lora), wd_ref[...]))) - 0.5
    lw = jnp.exp(logw) * (-LOG2_E)
    a = _sigmoid(a0_ref[...] + _mm1(lora, wa_ref[...]))
    g = _mm1(_sigmoid(lora), wg_ref[...])
    seg = seg_ref[...]
    kk = k * kk_ref[...]
    kk = kk / jnp.maximum(jnp.sqrt(_mm2_exact_rhs(kk * kk, seg)), 1e-12)
    k2 = k * (1.0 + (a - 1.0) * ka_ref[...])
    cum = _mm_exact_lhs(tri_ref[...], lw)
    lasts = [jnp.exp2(cum[c * CHUNK + CHUNK - 1:(c + 1) * CHUNK, :]) for c in range(tr // CHUNK)]
    for c, l in enumerate(lasts):
        gl_ref[c:c + 1, :] = l

    def to_chunk_end(val):
        return jnp.concatenate([val[c * CHUNK:(c + 1) * CHUNK] * l for c, l in enumerate(lasts)], axis=0)

    gamma = jnp.exp2(cum)
    ginv = 1.0 / gamma
    bt = (kk * a) * ginv
    kt = k2 * ginv

    def put(ref, slot, val):
        ref[:, :, slot * w:(slot + 1) * w] = val.reshape(nb, rb, w).astype(ref.dtype)

    put(ab_ref, 0, -kk * jnp.exp2(cum - lw))
    put(ab_ref, 1, bt)
    put(rkv_ref, 0, r * gamma)
    put(rkv_ref, 1, kt)
    put(rkv_ref, 2, to_chunk_end(bt))
    put(rkv_ref, 3, to_chunk_end(kt))
    put(rkv_ref, 4, v)
    put(bg_ref, 0, _mm2_exact_rhs(r * k2 * rk_ref[...], seg) * v)
    put(bg_ref, 1, g)


def _rwkv_prep(proj, prev, lp, consts, rows):
    bsz, seq, _ = proj.shape
    w = RWKV_WIDTH
    nb, rb = _token_tile(bsz, seq, rows)
    n_seq_tiles = seq // rb
    flat = lambda b, i: (b * n_seq_tiles + i, 0)
    row = lambda n: pl.BlockSpec((1, n), lambda b, i: (0, 0))
    full = lambda a: pl.BlockSpec(a.shape, lambda b, i: (0,) * a.ndim)
    packed = [(2, F32), (5, BF16), (2, F32)]
    return pl.pallas_call(
        _rwkv_prep_kernel,
        grid=(bsz // nb, n_seq_tiles),
        in_specs=[pl.BlockSpec((nb, rb, RWKV_PROJ), lambda b, i: (b, i, 0)),
                  pl.BlockSpec((nb, 1, RWKV_PROJ), lambda b, i: flat(b, i) + (0,)),
                  row(RWKV_PROJ), row(w), row(w), row(w), row(w), row(w),
                  full(lp["wd"]), full(lp["wa"]), full(lp["wg"]), full(consts["seg"]), full(consts["tri"])],
        out_specs=[pl.BlockSpec((nb, rb, k * w), lambda b, i: (b, i, 0)) for k, _ in packed]
        + [pl.BlockSpec((nb * rb // CHUNK, w), flat)],
        out_shape=[jax.ShapeDtypeStruct((bsz, seq, k * w), dt) for k, dt in packed]
        + [jax.ShapeDtypeStruct((bsz * seq // CHUNK, w), F32)],
        compiler_params=_params("parallel", "parallel"),
        name="rwkv_prep",
    )(proj, prev, lp["mu"], lp["w0"], lp["a0"], lp["k_k"], lp["k_a"], lp["r_k"],
      lp["wd"], lp["wa"], lp["wg"], consts["seg"], consts["tri"])


def _head_masks(width):
    lane = lax.broadcasted_iota(jnp.int32, (CHUNK, width), 1)
    row = lax.broadcasted_iota(jnp.int32, (CHUNK, width), 0)
    return lane >> HEAD_SHIFT, row, lane & (HEAD_LANES - 1)


def _block_diag(x, head):
    z = jnp.zeros_like(x)
    return jnp.concatenate([jnp.where(head == j, x, z) for j in range(x.shape[1] // HEAD_LANES)], axis=0)


def _rwkv_chunk_kernel(ab_ref, rkv_ref, gl_ref, s0_ref, y_ref, s_ref, *, bb, nck):
    @pl.when(pl.program_id(1) == 0)
    def _():
        s_ref[...] = s0_ref[...]

    tw = RWKV_TILE_LANES
    head, row, col = _head_masks(tw)
    strict = col < row
    incl = col <= row
    same8 = (col >> 3) == (row >> 3)
    lane2 = lax.broadcasted_iota(jnp.int32, (tw, tw), 1)
    row2 = lax.broadcasted_iota(jnp.int32, (tw, tw), 0)
    same_head = (lane2 >> HEAD_SHIFT) == (row2 >> HEAD_SHIFT)
    bd = functools.partial(_block_diag, head=head)
    c = CHUNK

    def pmm(p, q):
        return _dg(p.astype(BF16), bd(q.astype(BF16)))

    items = [(b, ck, pr) for b in range(bb) for ck in range(nck) for pr in range(RWKV_TILES)]
    n = range(len(items))
    rows = lambda ck: slice(ck * c, (ck + 1) * c)
    lanes = lambda pr: slice(pr * tw, (pr + 1) * tw)
    w = RWKV_WIDTH

    def ld(ref, slot):
        return [ref[b, rows(ck), slot * w + pr * tw:slot * w + (pr + 1) * tw] for b, ck, pr in items]

    cat0 = lambda *xs: jnp.concatenate(xs, axis=0)
    cat1 = lambda *xs: jnp.concatenate(xs, axis=1)
    at, bt = ld(ab_ref, 0), ld(ab_ref, 1)
    rt, kt, be, ke, v = (ld(rkv_ref, slot) for slot in range(5))
    ats, bts = [_split2(a) for a in at], [_split2(b) for b in bt]
    atb = [hi for hi, _ in ats]
    zero = jnp.zeros((c, tw), F32)
    gk = [_dg(cat0(atb[i], rt[i]), bd(kt[i]), _NT) for i in n]
    aak = [jnp.where(strict, gk[i][:c], zero) for i in n]
    ark = [jnp.where(incl, gk[i][c:], zero) for i in n]
    gb = [_dg(cat0(ats[i][0], ats[i][1], rt[i]), bd(bts[i][0]), _NT) for i in n]
    arb = [jnp.where(incl, gb[i][2 * c:], zero) for i in n]
    aab = [jnp.where(strict, gb[i][:c] + (gb[i][c:2 * c] + _dg(atb[i], bd(bts[i][1]), _NT)), zero) for i in n]
    z = [pmm(aak[i], v[i]) for i in n]
    a8 = [jnp.where(same8, aab[i], zero) for i in n]
    p2 = [pmm(a8[i], a8[i]) for i in n]
    p4 = [pmm(p2[i], p2[i]) for i in n]
    nn = [a8[i] + p2[i] + pmm(p2[i], a8[i]) for i in n]
    nn = [nn[i] + p4[i] + pmm(p4[i], nn[i]) for i in n]
    for lvl in (3, 4, 5):
        joins = ((col >> (lvl + 1)) == (row >> (lvl + 1))) & ((col >> lvl) != (row >> lvl))
        e = [jnp.where(joins, aab[i], zero) for i in n]
        te = [e[i] + pmm(nn[i], e[i]) for i in n]
        nn = [nn[i] + te[i] + pmm(te[i], nn[i]) for i in n]
    wu = [cat1(at[i], z[i]) + _dg(nn[i].astype(BF16), cat1(bd(atb[i]), bd(z[i].astype(BF16)))) for i in n]
    abk = [cat1(arb[i], ark[i]).astype(BF16) for i in n]
    s = {(b, pr): s_ref[b, pr] for b in range(bb) for pr in range(RWKV_TILES)}
    for ck in range(nck):
        cur = [i for i in n if items[i][1] == ck]
        key = lambda i: (items[i][0], items[i][2])
        x = {i: _dg(cat0(wu[i][:, :tw].astype(BF16), rt[i]), s[key(i)].astype(BF16), _NT) for i in cur}
        ub = {i: (x[i][:c] + wu[i][:, tw:]).astype(BF16) for i in cur}
        upd = {i: _dg(cat0(ub[i], v[i]), cat0(be[i], ke[i]), _TN) for i in cur}
        for i in cur:
            b, _, pr = items[i]
            y_ref[b, rows(ck), lanes(pr)] = x[i][c:] + _dg(abk[i], cat0(bd(ub[i]), bd(v[i])))
            s[b, pr] = (s[b, pr] * gl_ref[b, ck, :, lanes(pr)]
                        + jnp.where(same_head, upd[i], jnp.zeros_like(upd[i])))
    for (b, pr), val in s.items():
        s_ref[b, pr] = val


def _rwkv_chunk(ab, rkv, gl, s0, bb, nck):
    bsz, seq, _ = ab.shape
    w = RWKV_WIDTH
    tok = lambda a: pl.BlockSpec((bb, nck * CHUNK, a.shape[-1]), lambda i, c: (i, c, 0))
    st = pl.BlockSpec((bb, RWKV_TILES, RWKV_TILE_LANES, RWKV_TILE_LANES), lambda i, c: (i, 0, 0, 0))
    return pl.pallas_call(
        functools.partial(_rwkv_chunk_kernel, bb=bb, nck=nck),
        grid=(bsz // bb, seq // (nck * CHUNK)),
        in_specs=[tok(ab), tok(rkv), pl.BlockSpec((bb, nck, 1, w), lambda i, c: (i, c, 0, 0)), st],
        out_specs=[pl.BlockSpec((bb, nck * CHUNK, w), lambda i, c: (i, c, 0)), st],
        out_shape=[jax.ShapeDtypeStruct((bsz, seq, w), F32),
                   jax.ShapeDtypeStruct((bsz, RWKV_TILES, RWKV_TILE_LANES, RWKV_TILE_LANES), F32)],
        compiler_params=_params("parallel", "arbitrary"),
        name="rwkv_chunk",
    )(ab, rkv, gl, s0)


def _gla_kernel(p_ref, wgate_ref, bgate_ref, tri_ref, ng_ref, s0_ref, y_ref, s_ref, *, bb, nck):
    @pl.when(pl.program_id(1) == 0)
    def _():
        s_ref[...] = s0_ref[...]

    head, row, col = _head_masks(LANES)
    incl = col <= row
    bd = functools.partial(_block_diag, head=head)
    lane2 = lax.broadcasted_iota(jnp.int32, (2 * GLA_DV, LANES), 1)
    row2 = lax.broadcasted_iota(jnp.int32, (2 * GLA_DV, LANES), 0)
    same_head = (lane2 < GLA_DK) == (row2 < GLA_DV)
    kw, gw = GLA_KEY_WIDTH, GLA_WIDTH
    zero = jnp.zeros((CHUNK, LANES), F32)
    cat0 = lambda xs: jnp.concatenate(xs, axis=0)
    cat1 = lambda xs: jnp.concatenate(xs, axis=1)
    rows = lambda c: slice(c * CHUNK, (c + 1) * CHUNK)
    lanes = lambda pr: slice(pr * LANES, (pr + 1) * LANES)
    bs, cs = range(bb), range(nck)
    x = [p_ref[b] for b in bs]
    gate = [_mm1(x[b][:, 2 * kw + 2 * gw:], wgate_ref[...]) + bgate_ref[...] for b in bs]
    la = [(jnp.minimum(g, 0.0) - jnp.log(1.0 + jnp.exp(-jnp.abs(g)))) / GLA_TAU for g in gate]
    cum = [_mm_exact_lhs(tri_ref[...], la[b]) for b in bs]
    q_dec = [(x[b][:, :kw] * (GLA_DK ** -0.5)) * jnp.exp(cum[b]) for b in bs]
    k_inv = [x[b][:, kw:2 * kw] * jnp.exp(-cum[b]) for b in bs]
    last = [[cum[b][c * CHUNK + CHUNK - 1:(c + 1) * CHUNK, :] for c in cs] for b in bs]
    k_end = [[x[b][rows(c), kw:2 * kw] * jnp.exp(last[b][c] - cum[b][rows(c)]) for c in cs] for b in bs]
    dec = [[jnp.exp(last[b][c]) for c in cs] for b in bs]
    items = [(b, c, pr) for b in bs for c in cs for pr in range(GLA_PAIRS)]
    vp = {(b, c, pr): x[b][rows(c), 2 * kw + pr * 2 * GLA_DV:2 * kw + (pr + 1) * 2 * GLA_DV]
          for b, c, pr in items}
    att = {(b, c, pr): jnp.where(incl, _mm1(q_dec[b][rows(c), lanes(pr)],
                                            bd(k_inv[b][rows(c), lanes(pr)]), _NT), zero)
           for b, c, pr in items}
    upd = {it: _mm1(vp[it], k_end[it[0]][it[1]][:, lanes(it[2])], _TN) for it in items}
    intra = {it: _mm1(att[it], cat0([cat1([vp[it][:, :GLA_DV], zero]), cat1([zero, vp[it][:, GLA_DV:]])]))
             for it in items}
    st = {}
    for b in bs:
        for pr in range(GLA_PAIRS):
            cur = s_ref[b, pr]
            for c in cs:
                st[b, c, pr] = cur
                cur = cur * dec[b][c][:, lanes(pr)] + jnp.where(same_head, upd[b, c, pr], jnp.zeros_like(cur))
            s_ref[b, pr] = cur
    o = {(b, c, pr): intra[b, c, pr] + _mm1(q_dec[b][rows(c), lanes(pr)], st[b, c, pr], _NT)
         for b, c, pr in items}
    for b in bs:
        ob = cat0([cat1([o[b, c, pr] for pr in range(GLA_PAIRS)]) for c in cs])
        heads = [ob[:, h * GLA_DV:(h + 1) * GLA_DV] for h in range(GLA_HEADS)]
        normed = [oh * lax.rsqrt(jnp.mean(oh * oh, axis=-1, keepdims=True) + RMS_EPS) for oh in heads]
        y_ref[b] = cat1(normed) * ng_ref[...] * _silu(x[b][:, 2 * kw + gw:2 * kw + 2 * gw])


def _gla(proj, lp, tri, s0, bb, tg):
    bsz, seq, _ = proj.shape
    full = lambda a: pl.BlockSpec(a.shape, lambda i, c: (0,) * a.ndim)
    st = pl.BlockSpec((bb, GLA_PAIRS, 2 * GLA_DV, LANES), lambda i, c: (i, 0, 0, 0))
    return pl.pallas_call(
        functools.partial(_gla_kernel, bb=bb, nck=tg // CHUNK),
        grid=(bsz // bb, seq // tg),
        in_specs=[pl.BlockSpec((bb, tg, GLA_PROJ_PAD), lambda i, c: (i, c, 1)),
                  full(lp["wgate"]), full(lp["bgate"]), full(tri), full(lp["gla_g"]), st],
        out_specs=[pl.BlockSpec((bb, tg, GLA_WIDTH), lambda i, c: (i, c, 0)), st],
        out_shape=[jax.ShapeDtypeStruct((bsz, seq, GLA_WIDTH), F32),
                   jax.ShapeDtypeStruct((bsz, GLA_PAIRS, 2 * GLA_DV, LANES), F32)],
        compiler_params=_params("parallel", "arbitrary"),
        name="gla_chunk",
    )(proj, lp["wgate"], lp["bgate"], tri, lp["gla_g"], s0)


def _route(lgt):
    tm = lgt.shape[1]
    n_pad = ROUTE_ROWS - N_EXPERTS
    rowg = lax.broadcasted_iota(jnp.int32, (n_pad, tm), 0)
    rowe = lax.broadcasted_iota(jnp.int32, (N_EXPERTS, tm), 0)
    rowg_f, rowe_f = rowg.astype(F32), rowe.astype(F32)

    def first_argmax(vals, mx, rows_f):
        return jnp.min(jnp.where(vals == mx, rows_f, float(ROUTE_ROWS)), axis=0, keepdims=True)

    is_group = rowg < N_GROUPS
    lg = jnp.where(is_group, lgt[N_EXPERTS:], -jnp.inf)
    gmax = jnp.max(lg, axis=0, keepdims=True)
    gi = first_argmax(lg, gmax, rowg_f)
    pg_top = 1.0 / jnp.sum(jnp.where(is_group, jnp.exp(lg - gmax), 0.0), axis=0, keepdims=True)
    in_group = (rowe >> GROUP_SHIFT).astype(F32) == gi
    le = jnp.where(in_group, lgt[:N_EXPERTS], -jnp.inf)
    m1 = jnp.max(le, axis=0, keepdims=True)
    i1 = first_argmax(le, m1, rowe_f)
    le2 = jnp.where(rowe_f == i1, -jnp.inf, le)
    m2 = jnp.max(le2, axis=0, keepdims=True)
    i2 = first_argmax(le2, m2, rowe_f)
    p2 = jnp.exp(m2 - m1)
    w1 = pg_top / (1.0 + p2)
    w2 = pg_top * p2 / (1.0 + p2)
    comb = jnp.where(rowe_f == i1, w1, 0.0) + jnp.where(rowe_f == i2, w2, 0.0)
    return jnp.concatenate([comb, jnp.where(rowg == 0, gi, 0.0)], axis=0)


def _outproj_kernel(yr_ref, bg_ref, yg_ref, x_ref, gt_ref, sc_ref, sh_ref,
                    lnw_ref, lnb_ref, seg_ref, wout_ref, g2_ref, wr_ref, br_ref,
                    x1_ref, h2_ref, route_ref):
    nb, rb, d_model = x_ref.shape
    tm = nb * rb
    flat = lambda ref: ref[...].reshape(tm, ref.shape[-1])
    seg = seg_ref[...]
    y = flat(yr_ref)
    inv_n = 1.0 / RWKV_HEAD
    d = y - _mm2_exact_rhs(y, seg) * inv_n
    var = _mm2_exact_rhs(d * d, seg) * inv_n
    yn = d * lax.rsqrt(var + LNX_EPS) * lnw_ref[...] + lnb_ref[...]
    bg = flat(bg_ref)
    yr = (yn + bg[:, :RWKV_WIDTH]) * bg[:, RWKV_WIDTH:]
    mix = jnp.concatenate([yr, flat(yg_ref)], axis=1)
    x1 = x_ref[...] + gt_ref[...] * _mm1(mix, wout_ref[...]).reshape(nb, rb, d_model)
    x1_ref[...] = x1
    h2 = _rms_mod(x1, g2_ref[...], sc_ref[...], sh_ref[...])
    h2_ref[...] = h2.astype(BF16)
    route_ref[...] = _route(_mm3(wr_ref[...], h2.reshape(tm, d_model), _NT) + br_ref[...])


def _out_proj(yr, bg, yg, x, gt, sc, sh, lp, consts, rows):
    bsz, seq, d = x.shape
    nb, rb = _token_tile(bsz, seq, rows)
    n_seq_tiles = seq // rb
    half = pl.BlockSpec((nb, rb, RWKV_WIDTH), lambda b, i: (b, i, 0))
    tokd = pl.BlockSpec((nb, rb, d), lambda b, i: (b, i, 0))
    vec = pl.BlockSpec((nb, 1, d), lambda b, i: (b, 0, 0))
    full = lambda a: pl.BlockSpec(a.shape, lambda b, i: (0,) * a.ndim)
    args = (lp["lnx_w"], lp["lnx_b"], consts["seg"], lp["w_out"], lp["norm2_g"], lp["w_router"], lp["b_router"])
    return pl.pallas_call(
        _outproj_kernel,
        grid=(bsz // nb, seq // rb),
        in_specs=[half, pl.BlockSpec((nb, rb, 2 * RWKV_WIDTH), lambda b, i: (b, i, 0)), half, tokd,
                  vec, vec, vec] + [full(a) for a in args],
        out_specs=[tokd, tokd, pl.BlockSpec((ROUTE_ROWS, nb * rb), lambda b, i: (0, b * n_seq_tiles + i))],
        out_shape=[jax.ShapeDtypeStruct((bsz, seq, d), F32),
                   jax.ShapeDtypeStruct((bsz, seq, d), BF16),
                   jax.ShapeDtypeStruct((ROUTE_ROWS, bsz * seq), F32)],
        compiler_params=_params("parallel", "parallel"),
        name="out_proj_router",
    )(yr, bg, yg, x, gt, sc, sh, *args)


def _moe_kernel(h_ref, route_ref, x1_ref, gt_ref, sc_ref, sh_ref, gf_ref, tri_ref, wg_ref, wu_ref, wd_ref,
                y_ref, ys_ref, pos_ref, meta_ref):
    g = pl.program_id(2)
    nb, rb, d = h_ref.shape
    tm = nb * rb
    sub, tail = MOE_SUB_ROWS, MOE_SUB_ROWS // 2
    cap = ys_ref.shape[0]
    n_lane_tiles = tm // LANES
    lane_tile = lambda k: slice(k * LANES, (k + 1) * LANES)

    @pl.when(g == 0)
    def _sort():
        row8 = lax.broadcasted_iota(jnp.int32, (8, LANES), 0).astype(F32)
        carry = jnp.zeros((8, 1), F32)
        members, ranks = [], []
        for k in range(n_lane_tiles):
            blk = jnp.where(row8 == route_ref[N_EXPERTS:N_EXPERTS + 1, lane_tile(k)], 1.0, 0.0)
            members.append(blk)
            ranks.append(_dg(blk.astype(BF16), tri_ref[...]) - blk + carry)
            carry = carry + jnp.sum(blk, axis=1, keepdims=True)
        rowc = lax.broadcasted_iota(jnp.int32, (8, 1), 0)
        lane = lax.broadcasted_iota(jnp.int32, (1, LANES), 1)
        first = jnp.zeros((1, 1), F32)
        off_col = jnp.zeros((8, 1), F32)
        meta = jnp.zeros((1, LANES), F32)
        for grp in range(N_GROUPS):
            count = jnp.sum(jnp.where(rowc == grp, carry, 0.0), axis=0, keepdims=True)
            blocks = jnp.ceil(count * (1.0 / tail))
            pairs = jnp.floor(blocks * 0.5)
            odd = blocks - 2.0 * pairs
            triple = odd * jnp.where(blocks >= 3.0, 1.0, 0.0)
            n_full = pairs - triple
            off_col = off_col + jnp.where(rowc == grp, first, 0.0)
            meta = (meta + jnp.where(lane == grp, first, 0.0) + jnp.where(lane == N_GROUPS + grp, n_full, 0.0)
                    + jnp.where(lane == 2 * N_GROUPS + grp, odd + 2.0 * triple, 0.0))
            first = first + blocks * tail
        meta = meta.astype(jnp.int32)
        for i in range(3 * N_GROUPS):
            meta_ref[i] = meta[0, i]
        for k in range(n_lane_tiles):
            pos = jnp.sum(members[k] * (ranks[k] + off_col), axis=0, keepdims=True)
            pos_ref[:, lane_tile(k)] = pos.astype(jnp.int32)
        ys_ref[...] = jnp.zeros_like(ys_ref)

    def experts(base, n_rows):
        prow = lax.broadcasted_iota(jnp.int32, (n_rows, tm), 0) + base
        onehot = jnp.where(prow == pos_ref[...], 1.0, 0.0).astype(BF16)
        hs = _dg(onehot, h_ref[...].reshape(tm, d)).astype(BF16)
        comb3 = _split3(route_ref[:N_EXPERTS, :])
        cs = _dg(onehot, comb3[0], _NT) + (_dg(onehot, comb3[1], _NT) + _dg(onehot, comb3[2], _NT))
        lane = lax.broadcasted_iota(jnp.int32, cs.shape, 1)
        ys = jnp.zeros((n_rows, d), F32)
        for e in range(EXPERTS_PER_GROUP):
            hid = _silu(_dg(hs, wg_ref[e])) * _dg(hs, wu_ref[e])
            ce = jnp.sum(jnp.where(lane == g * EXPERTS_PER_GROUP + e, cs, 0.0), axis=-1, keepdims=True)
            ys = ys + ce * _dg(hid.astype(BF16), wd_ref[e])
        ys_ref[pl.ds(base, n_rows), :] = ys.astype(BF16)

    first_row = meta_ref[g]
    n_full = meta_ref[N_GROUPS + g]

    def full_sub_tile(j, carry_):
        experts(pl.multiple_of(first_row + j * sub, tail), sub)
        return carry_

    lax.fori_loop(0, n_full, full_sub_tile, 0)

    for last_blocks in (1, 3):
        @pl.when(meta_ref[2 * N_GROUPS + g] == last_blocks)
        def _last(last_blocks=last_blocks):
            experts(pl.multiple_of(first_row + n_full * sub, tail), last_blocks * tail)

    @pl.when(g == N_GROUPS - 1)
    def _unsort():
        ys_all = ys_ref[...]
        prow = lax.broadcasted_iota(jnp.int32, (cap, LANES), 0)
        seqs = max(LANES // rb, 1)
        for k in range(n_lane_tiles):
            onehot = jnp.where(prow == pos_ref[:, lane_tile(k)], 1.0, 0.0).astype(BF16)
            moe = _dg(onehot, ys_all, _TN).reshape(seqs, LANES // seqs, d)
            b0 = k * LANES // rb
            r0 = k * LANES - b0 * rb
            bs, rs = slice(b0, b0 + seqs), slice(r0, r0 + LANES // seqs)
            x2 = x1_ref[bs, rs] + gt_ref[bs] * moe
            y_ref[bs, rs] = _rms_mod(x2, gf_ref[...], sc_ref[bs], sh_ref[bs])


def _moe(h2, route, x1, gt, sc, sh, gf, wg, wu, wd, tri, rows):
    bsz, seq, d = x1.shape
    nb, rb = _token_tile(bsz, seq, rows)
    tm = nb * rb
    assert tm % LANES == 0 and (rb % LANES == 0 or LANES % rb == 0)
    n_seq_tiles = seq // rb
    cap = tm + N_GROUPS * (MOE_SUB_ROWS // 2)
    tokd = pl.BlockSpec((nb, rb, d), lambda b, i, g: (b, i, 0))
    vec = pl.BlockSpec((nb, 1, d), lambda b, i, g: (b, 0, 0))
    group_w = lambda shape: pl.BlockSpec((EXPERTS_PER_GROUP,) + shape, lambda b, i, g: (g, 0, 0))
    return pl.pallas_call(
        _moe_kernel,
        grid=(bsz // nb, n_seq_tiles, N_GROUPS),
        in_specs=[tokd, pl.BlockSpec((ROUTE_ROWS, tm), lambda b, i, g: (0, b * n_seq_tiles + i)),
                  tokd, vec, vec, vec,
                  pl.BlockSpec((1, d), lambda b, i, g: (0, 0)),
                  pl.BlockSpec(tri.shape, lambda b, i, g: (0, 0)),
                  group_w((d, D_EXPERT)), group_w((d, D_EXPERT)), group_w((D_EXPERT, d))],
        out_specs=tokd,
        out_shape=jax.ShapeDtypeStruct((bsz, seq, d), F32),
        scratch_shapes=[pltpu.VMEM((cap, d), BF16), pltpu.VMEM((1, tm), jnp.int32),
                        pltpu.SMEM((3 * N_GROUPS,), jnp.int32)],
        compiler_params=_params("parallel", "parallel", "arbitrary"),
        name="moe_final_norm",
    )(h2, route, x1, gt, sc, sh, gf, tri, wg, wu, wd)


def _block_ones(n, blk, lower):
    i = np.arange(n)
    m = (i[:, None] // blk) == (i[None, :] // blk)
    if lower:
        m = m & (i[None, :] <= i[:, None])
    return jnp.asarray(m, dtype=BF16)


def _consts(tr, tg):
    return dict(seg=_block_ones(RWKV_WIDTH, RWKV_HEAD, False), tri=_block_ones(tr, CHUNK, True),
                tri_gla=_block_ones(tg, CHUNK, True), tri_up=_block_ones(LANES, LANES, True).T)


def _heads_block_diag(s):
    n, c = s.shape[2], s.shape[-1]
    rows = [jnp.pad(s[:, :, j], ((0, 0), (0, 0), (0, 0), (j * c, (n - 1 - j) * c))) for j in range(n)]
    return jnp.concatenate(rows, axis=2)


def _diag_blocks(t, n):
    r, c = t.shape[-2] // n, t.shape[-1] // n
    return jnp.stack([t[:, :, j * r:(j + 1) * r, j * c:(j + 1) * c] for j in range(n)], axis=2)


def _pad_rows(w, first_row):
    out = jnp.zeros((LORA_PAD, w.shape[1]), F32)
    return lax.dynamic_update_slice(out, w, (first_row, 0)).astype(BF16)


def _layer_params(l, w_in, mu_shift, w0, w_decay_up, a0, w_a_up, w_g_up, k_k, k_a, r_k, lnx_w, lnx_b,
                  w_gla_gate_up, b_gla_gate, gla_norm_g, w_out, norm2_g,
                  w_router_group, b_router_group, w_router_expert, b_router_expert):
    w_in_p = _win_layout(w_in[l].T, WEIGHT_TILE_COLS)
    n_pad = ROUTE_ROWS - N_EXPERTS - N_GROUPS
    w_router = jnp.concatenate([w_router_expert[l].T, w_router_group[l].T, jnp.zeros((n_pad, D_MODEL), F32)])
    b_router = jnp.concatenate([b_router_expert[l], b_router_group[l],
                                jnp.zeros((n_pad,), F32)]).reshape(ROUTE_ROWS, 1)
    wgate = jnp.zeros((LANES, GLA_KEY_WIDTH), F32).at[:GLA_GATE_RANK].set(w_gla_gate_up[l]).astype(BF16)
    r1 = lambda a: a.reshape(1, -1)
    return dict(
        w_in=w_in_p, mu=r1(mu_shift[l]), w0=r1(w0[l]), a0=r1(a0[l]), k_k=r1(k_k[l]), k_a=r1(k_a[l]),
        r_k=r1(r_k[l]), wd=_pad_rows(w_decay_up[l], 0), wa=_pad_rows(w_a_up[l], DECAY_LORA),
        wg=_pad_rows(w_g_up[l], DECAY_LORA + AAA_LORA), lnx_w=r1(lnx_w[l]), lnx_b=r1(lnx_b[l]),
        wgate=wgate, bgate=r1(b_gla_gate[l]), gla_g=r1(gla_norm_g[l]),
        w_out=w_out[l].astype(BF16), norm2_g=r1(norm2_g[l]), w_router=w_router, b_router=b_router)


def _run_layer(x, mod, shift0, wkv0, gla0, lp, experts, final):
    bsz, seq, d = x.shape
    assert seq % CHUNK == 0
    bb = 2 if seq >= RWKV_STEP_ROWS else min(bsz, RECURRENT_MAX_SEQS)
    assert bsz % bb == 0
    tg = min(seq, GLA_STEP_ROWS)
    consts = _consts(TOKEN_TILE_ROWS, tg)
    m = lambda j: mod[:, j:j + 1, :]
    sh1, sc1, gt1, sh2, sc2, gt2 = (m(j) for j in range(6))
    proj, tails = _in_proj(x, sc1, sh1, lp["norm1_g"], lp["w_in"], TOKEN_TILE_ROWS)
    tails = tails.reshape(bsz, -1, RWKV_PROJ)
    new_shift = tails[:, -1]
    prev = jnp.concatenate([shift0[:, None, :], tails[:, :-1]], axis=1).reshape(-1, 1, RWKV_PROJ)
    ab, rkv, bg, gl = _rwkv_prep(proj, prev, lp, consts, TOKEN_TILE_ROWS)
    gl = gl.reshape(bsz, seq // CHUNK, RWKV_WIDTH)
    s0 = _heads_block_diag(wkv0.reshape(bsz, RWKV_TILES, RWKV_TILE_HEADS, RWKV_HEAD, RWKV_HEAD))
    yr, s_bd = _rwkv_chunk(ab, rkv, gl[:, :, None, :], s0, bb, min(seq, RWKV_STEP_ROWS) // CHUNK)
    new_wkv = _diag_blocks(s_bd, RWKV_TILE_HEADS).reshape(bsz, RWKV_HEADS, RWKV_HEAD, RWKV_HEAD)
    t0 = _heads_block_diag(jnp.swapaxes(gla0, -1, -2).reshape(bsz, GLA_PAIRS, 2, GLA_DV, GLA_DK))
    yg, t_bd = _gla(proj, lp, consts["tri_gla"], t0, bb, tg)
    new_gla = jnp.swapaxes(_diag_blocks(t_bd, 2).reshape(bsz, GLA_HEADS, GLA_DV, GLA_DK), -1, -2)
    x1, h2, route = _out_proj(yr, bg, yg, x, gt1, sc2, sh2, lp, consts, TOKEN_TILE_ROWS)
    out = _moe(h2, route, x1, gt2, *final, *experts, consts["tri_up"], MOE_TILE_ROWS)
    return out, new_shift, new_wkv, new_gla


def kernel(x_prompt, x_sample, c_prompt, c_sample, state_rwkv_shift, state_rwkv_wkv, state_gla_kv, w_ada, b_ada, norm1_g, norm2_g, w_in, mu_shift, w0, w_decay_up, a0, w_a_up, w_g_up, k_k, k_a, r_k, lnx_w, lnx_b, w_gla_gate_up, b_gla_gate, gla_norm_g, w_out, w_router_group, b_router_group, w_router_expert, b_router_expert, w_expert_gate, w_expert_up, w_expert_down, w_ada_final, b_ada_final, normf_g):
    assert w_ada.shape[0] == 1, "the final norm is fused into the single layer's MoE kernel"
    bp, bs = x_prompt.shape[0], x_sample.shape[0]
    d = D_MODEL
    n_rows = -(-(bp + bs) // 8) * 8
    c_all = jnp.zeros((n_rows, d), F32).at[:bp].set(c_prompt).at[bp:bp + bs].set(c_sample)
    modf = _modulation(c_all, w_ada_final, b_ada_final, MODULATION_TILE_COLS).reshape(n_rows, 2, d)
    mod = _modulation(c_all, w_ada, b_ada[0], MODULATION_TILE_COLS, layer=0).reshape(n_rows, 6, d)
    lp = _layer_params(0, w_in, mu_shift, w0, w_decay_up, a0, w_a_up, w_g_up, k_k, k_a, r_k, lnx_w,
                       lnx_b, w_gla_gate_up, b_gla_gate, gla_norm_g, w_out, norm2_g,
                       w_router_group, b_router_group, w_router_expert, b_router_expert)
    lp["norm1_g"] = norm1_g[0].reshape(1, d)
    experts = (w_expert_gate[0].astype(BF16), w_expert_up[0].astype(BF16), w_expert_down[0].astype(BF16))
    groups = [
        (x_prompt, 0, bp, jnp.zeros((bp, RWKV_PROJ), F32),
         jnp.zeros((bp, RWKV_HEADS, RWKV_HEAD, RWKV_HEAD), F32), jnp.zeros((bp, GLA_HEADS, GLA_DK, GLA_DV), F32)),
        (x_sample, bp, bp + bs, state_rwkv_shift[0], state_rwkv_wkv[0], state_gla_kv[0]),
    ]
    ys, states = [], []
    for x, lo, hi, shift0, wkv0, gla0 in groups:
        final = (modf[lo:hi, 1:2], modf[lo:hi, 0:1], normf_g.reshape(1, d))
        y, *st = _run_layer(x, mod[lo:hi], shift0, wkv0, gla0, lp, experts, final)
        ys.append(y)
        states.extend(s[None] for s in st)
    return tuple(ys + states)
```

```python
import functools

import jax
import jax.numpy as jnp
import numpy as np
from jax import lax
from jax.experimental import pallas as pl
from jax.experimental.pallas import tpu as pltpu

F32 = jnp.float32
BF16 = jnp.bfloat16

LANES = 128
VMEM_LIMIT_BYTES = 56 * 1024 * 1024
TOKEN_TILE_ROWS = 512
MOE_TILE_ROWS = 1024
MOE_SUB_ROWS = 256
GLA_STEP_ROWS = 512
RECURRENT_MAX_SEQS = 8
MODULATION_TILE_COLS = 1024
WEIGHT_TILE_COLS = 256
RWKV_STEP_ROWS = 256

D_MODEL = 1024
CHUNK = 64
RWKV_WIDTH = 512
RWKV_HEAD = 64
RWKV_HEADS = RWKV_WIDTH // RWKV_HEAD
HEAD_LANES = RWKV_HEAD
HEAD_SHIFT = HEAD_LANES.bit_length() - 1
assert 1 << HEAD_SHIFT == HEAD_LANES
RWKV_TILE_LANES = 128
RWKV_TILE_HEADS = RWKV_TILE_LANES // RWKV_HEAD
RWKV_TILES = RWKV_WIDTH // RWKV_TILE_LANES
DECAY_LORA = 32
AAA_LORA = 32
GATE_LORA = 64
LORA_PAD = DECAY_LORA + AAA_LORA + GATE_LORA
RWKV_PROJ = 3 * RWKV_WIDTH + LORA_PAD
GLA_WIDTH = 512
GLA_HEADS = 4
GLA_PAIRS = GLA_HEADS // 2
GLA_DV = GLA_WIDTH // GLA_HEADS
GLA_DK = GLA_DV // 2
GLA_KEY_WIDTH = GLA_HEADS * GLA_DK
GLA_GATE_RANK = 16
GLA_TAU = 16.0
GLA_PROJ = 2 * GLA_KEY_WIDTH + 2 * GLA_WIDTH + GLA_GATE_RANK
GLA_PROJ_PAD = RWKV_PROJ
IN_PROJ_PAD = RWKV_PROJ + GLA_PROJ_PAD
N_GROUPS = 4
EXPERTS_PER_GROUP = 4
N_EXPERTS = N_GROUPS * EXPERTS_PER_GROUP
GROUP_SHIFT = EXPERTS_PER_GROUP.bit_length() - 1
assert 1 << GROUP_SHIFT == EXPERTS_PER_GROUP
ROUTE_ROWS = 24
D_EXPERT = 512
RMS_EPS = 1e-6
LNX_EPS = 64e-5
LOG2_E = 1.4426950408889634

_NN = (((1,), (0,)), ((), ()))
_NT = (((1,), (1,)), ((), ()))
_TN = (((0,), (0,)), ((), ()))


def _dg(a, b, dims=_NN):
    return lax.dot_general(a, b, dims, preferred_element_type=F32)


def _split2(x):
    hi = x.astype(BF16)
    lo = (x - hi.astype(F32)).astype(BF16)
    return hi, lo


def _split3(x):
    hi = x.astype(BF16)
    r1 = x - hi.astype(F32)
    mid = r1.astype(BF16)
    lo = (r1 - mid.astype(F32)).astype(BF16)
    return hi, mid, lo


def _mm1(a, b, dims=_NN):
    return _dg(a.astype(BF16), b.astype(BF16), dims)


def _mm3(a, b, dims=_NN):
    ah, al = _split2(a)
    bh, bl = _split2(b)
    return _dg(ah, bh, dims) + (_dg(ah, bl, dims) + _dg(al, bh, dims))


def _mm_exact_lhs(e, x, dims=_NN):
    h, m, l = _split3(x)
    return _dg(e, h, dims) + (_dg(e, m, dims) + _dg(e, l, dims))


def _mm2_exact_rhs(x, e, dims=_NN):
    h, l = _split2(x)
    return _dg(h, e, dims) + _dg(l, e, dims)


def _softplus(z):
    return jnp.maximum(z, 0.0) + jnp.log(1.0 + jnp.exp(-jnp.abs(z)))


def _sigmoid(z):
    return 1.0 / (1.0 + jnp.exp(-z))


def _silu(z):
    return z * _sigmoid(z)


def _params(*sem):
    return pltpu.CompilerParams(dimension_semantics=sem, vmem_limit_bytes=VMEM_LIMIT_BYTES)


def _mod_kernel(c_ref, w_ref, b_ref, o_ref):
    o_ref[...] = _mm1(_silu(c_ref[...]), w_ref[...]) + b_ref[...]


def _modulation(c, w, b, tn, layer=None):
    rows, d = c.shape
    n = w.shape[-1]
    if layer is None:
        w_spec = pl.BlockSpec((d, tn), lambda j: (0, j))
    else:
        w_spec = pl.BlockSpec((None, d, tn), lambda j: (layer, 0, j))
    return pl.pallas_call(
        _mod_kernel,
        grid=(n // tn,),
        in_specs=[pl.BlockSpec((rows, d), lambda j: (0, 0)), w_spec,
                  pl.BlockSpec((1, tn), lambda j: (0, j))],
        out_specs=pl.BlockSpec((rows, tn), lambda j: (0, j)),
        out_shape=jax.ShapeDtypeStruct((rows, n), F32),
        compiler_params=_params("parallel"),
        name="modulation",
    )(c, w, b.reshape(1, n))


def _win_layout_kernel(w_ref, o_ref):
    qkv_end = RWKV_PROJ + 2 * GLA_KEY_WIDTH + GLA_WIDTH
    gate_end = qkv_end + GLA_GATE_RANK
    o_ref[:qkv_end] = w_ref[:qkv_end].astype(BF16)
    o_ref[qkv_end:qkv_end + GLA_WIDTH] = w_ref[gate_end:gate_end + GLA_WIDTH].astype(BF16)
    o_ref[qkv_end + GLA_WIDTH:RWKV_PROJ + GLA_PROJ] = w_ref[qkv_end:gate_end].astype(BF16)
    o_ref[RWKV_PROJ + GLA_PROJ:] = jnp.zeros((IN_PROJ_PAD - RWKV_PROJ - GLA_PROJ, o_ref.shape[1]), BF16)


def _win_layout(wt, cols):
    n, d = wt.shape
    return pl.pallas_call(
        _win_layout_kernel,
        grid=(d // cols,),
        in_specs=[pl.BlockSpec((n, cols), lambda i: (0, i))],
        out_specs=pl.BlockSpec((IN_PROJ_PAD, cols), lambda i: (0, i)),
        out_shape=jax.ShapeDtypeStruct((IN_PROJ_PAD, d), BF16),
        compiler_params=_params("parallel"),
        name="w_in_layout",
    )(wt)


def _rms_mod(x, g, sc, sh):
    ms = jnp.mean(x * x, axis=-1, keepdims=True)
    return (x * lax.rsqrt(ms + RMS_EPS) * g) * (1.0 + sc) + sh


def _token_tile(bsz, seq, rows):
    if seq >= rows:
        assert seq % rows == 0
        return 1, rows
    nb = min(bsz, rows // seq)
    assert bsz % nb == 0
    return nb, seq


def _inproj_kernel(x_ref, sc_ref, sh_ref, g_ref, w_ref, o_ref, last_ref, *, n_step):
    nb, rb, d = x_ref.shape
    h = _rms_mod(x_ref[...], g_ref[...], sc_ref[...], sh_ref[...])
    hb = h.reshape(nb * rb, d).astype(BF16)
    for j in range(IN_PROJ_PAD // n_step):
        cols = slice(j * n_step, (j + 1) * n_step)
        o_ref[:, :, cols] = _dg(hb, w_ref[cols, :], _NT).reshape(nb, rb, n_step)
    last_ref[...] = o_ref[:, rb - 1:rb, :RWKV_PROJ]


def _in_proj(x, sc, sh, g, w, rows):
    bsz, seq, d = x.shape
    nb, rb = _token_tile(bsz, seq, rows)
    n_seq_tiles = seq // rb
    vec = pl.BlockSpec((nb, 1, d), lambda b, i: (b, 0, 0))
    return pl.pallas_call(
        functools.partial(_inproj_kernel, n_step=2 * LANES),
        grid=(bsz // nb, n_seq_tiles),
        in_specs=[pl.BlockSpec((nb, rb, d), lambda b, i: (b, i, 0)), vec, vec,
                  pl.BlockSpec((1, d), lambda b, i: (0, 0)),
                  pl.BlockSpec((IN_PROJ_PAD, d), lambda b, i: (0, 0))],
        out_specs=[pl.BlockSpec((nb, rb, IN_PROJ_PAD), lambda b, i: (b, i, 0)),
                   pl.BlockSpec((nb, 1, RWKV_PROJ), lambda b, i: (b * n_seq_tiles + i, 0, 0))],
        out_shape=[jax.ShapeDtypeStruct((bsz, seq, IN_PROJ_PAD), F32),
                   jax.ShapeDtypeStruct((bsz * n_seq_tiles, 1, RWKV_PROJ), F32)],
        compiler_params=_params("parallel", "parallel"),
        name="norm1_in_proj",
    )(x, sc, sh, g, w)


def _rwkv_prep_kernel(p_ref, prev_ref, mu_ref, w0_ref, a0_ref, kk_ref, ka_ref, rk_ref,
                      wd_ref, wa_ref, wg_ref, seg_ref, tri_ref,
                      ab_ref, rkv_ref, bg_ref, gl_ref):
    nb, rb, wp = p_ref.shape
    tr = nb * rb
    p = p_ref[...].reshape(tr, wp)
    row = lax.broadcasted_iota(jnp.int32, (nb, rb, wp), 1)
    xx = jnp.where(row == 0, prev_ref[...], pltpu.roll(p, 1, 0).reshape(nb, rb, wp)).reshape(tr, wp)
    ps = p + (xx - p) * mu_ref[...]
    w = RWKV_WIDTH
    r, k, v, lora = ps[:, :w], ps[:, w:2 * w], ps[:, 2 * w:3 * w], ps[:, 3 * w:]
    logw = -_softplus(-(w0_ref[...] + _mm1(jnp.tanh(lora), wd_ref[...]))) - 0.5
    lw = jnp.exp(logw) * (-LOG2_E)
    a = _sigmoid(a0_ref[...] + _mm1(lora, wa_ref[...]))
    g = _mm1(_sigmoid(lora), wg_ref[...])
    seg = seg_ref[...]
    kk = k * kk_ref[...]
    kk = kk / jnp.maximum(jnp.sqrt(_mm2_exact_rhs(kk * kk, seg)), 1e-12)
    k2 = k * (1.0 + (a - 1.0) * ka_ref[...])
    cum = _mm_exact_lhs(tri_ref[...], lw)
    lasts = [jnp.exp2(cum[c * CHUNK + CHUNK - 1:(c + 1) * CHUNK, :]) for c in range(tr // CHUNK)]
    for c, l in enumerate(lasts):
        gl_ref[c:c + 1, :] = l

    def to_chunk_end(val):
        return jnp.concatenate([val[c * CHUNK:(c + 1) * CHUNK] * l for c, l in enumerate(lasts)], axis=0)

    gamma = jnp.exp2(cum)
    ginv = 1.0 / gamma
    bt = (kk * a) * ginv
    kt = k2 * ginv

    def put(ref, slot, val):
        ref[:, :, slot * w:(slot + 1) * w] = val.reshape(nb, rb, w).astype(ref.dtype)

    put(ab_ref, 0, -kk * jnp.exp2(cum - lw))
    put(ab_ref, 1, bt)
    put(rkv_ref, 0, r * gamma)
    put(rkv_ref, 1, kt)
    put(rkv_ref, 2, to_chunk_end(bt))
    put(rkv_ref, 3, to_chunk_end(kt))
    put(rkv_ref, 4, v)
    put(bg_ref, 0, _mm2_exact_rhs(r * k2 * rk_ref[...], seg) * v)
    put(bg_ref, 1, g)


def _rwkv_prep(proj, prev, lp, consts, rows):
    bsz, seq, _ = proj.shape
    w = RWKV_WIDTH
    nb, rb = _token_tile(bsz, seq, rows)
    n_seq_tiles = seq // rb
    flat = lambda b, i: (b * n_seq_tiles + i, 0)
    row = lambda n: pl.BlockSpec((1, n), lambda b, i: (0, 0))
    full = lambda a: pl.BlockSpec(a.shape, lambda b, i: (0,) * a.ndim)
    packed = [(2, F32), (5, BF16), (2, F32)]
    return pl.pallas_call(
        _rwkv_prep_kernel,
        grid=(bsz // nb, n_seq_tiles),
        in_specs=[pl.BlockSpec((nb, rb, RWKV_PROJ), lambda b, i: (b, i, 0)),
                  pl.BlockSpec((nb, 1, RWKV_PROJ), lambda b, i: flat(b, i) + (0,)),
                  row(RWKV_PROJ), row(w), row(w), row(w), row(w), row(w),
                  full(lp["wd"]), full(lp["wa"]), full(lp["wg"]), full(consts["seg"]), full(consts["tri"])],
        out_specs=[pl.BlockSpec((nb, rb, k * w), lambda b, i: (b, i, 0)) for k, _ in packed]
        + [pl.BlockSpec((nb * rb // CHUNK, w), flat)],
        out_shape=[jax.ShapeDtypeStruct((bsz, seq, k * w), dt) for k, dt in packed]
        + [jax.ShapeDtypeStruct((bsz * seq // CHUNK, w), F32)],
        compiler_params=_params("parallel", "parallel"),
        name="rwkv_prep",
    )(proj, prev, lp["mu"], lp["w0"], lp["a0"], lp["k_k"], lp["k_a"], lp["r_k"],
      lp["wd"], lp["wa"], lp["wg"], consts["seg"], consts["tri"])


def _head_masks(width):
    lane = lax.broadcasted_iota(jnp.int32, (CHUNK, width), 1)
    row = lax.broadcasted_iota(jnp.int32, (CHUNK, width), 0)
    return lane >> HEAD_SHIFT, row, lane & (HEAD_LANES - 1)


def _block_diag(x, head):
    z = jnp.zeros_like(x)
    return jnp.concatenate([jnp.where(head == j, x, z) for j in range(x.shape[1] // HEAD_LANES)], axis=0)


def _rwkv_chunk_kernel(ab_ref, rkv_ref, gl_ref, s0_ref, y_ref, s_ref, *, bb, nck):
    @pl.when(pl.program_id(1) == 0)
    def _():
        s_ref[...] = s0_ref[...]

    tw = RWKV_TILE_LANES
    head, row, col = _head_masks(tw)
    strict = col < row
    incl = col <= row
    same8 = (col >> 3) == (row >> 3)
    lane2 = lax.broadcasted_iota(jnp.int32, (tw, tw), 1)
    row2 = lax.broadcasted_iota(jnp.int32, (tw, tw), 0)
    same_head = (lane2 >> HEAD_SHIFT) == (row2 >> HEAD_SHIFT)
    bd = functools.partial(_block_diag, head=head)
    c = CHUNK

    def pmm(p, q):
        return _dg(p.astype(BF16), bd(q.astype(BF16)))

    items = [(b, ck, pr) for b in range(bb) for ck in range(nck) for pr in range(RWKV_TILES)]
    n = range(len(items))
    rows = lambda ck: slice(ck * c, (ck + 1) * c)
    lanes = lambda pr: slice(pr * tw, (pr + 1) * tw)
    w = RWKV_WIDTH

    def ld(ref, slot):
        return [ref[b, rows(ck), slot * w + pr * tw:slot * w + (pr + 1) * tw] for b, ck, pr in items]

    cat0 = lambda *xs: jnp.concatenate(xs, axis=0)
    cat1 = lambda *xs: jnp.concatenate(xs, axis=1)
    at, bt = ld(ab_ref, 0), ld(ab_ref, 1)
    rt, kt, be, ke, v = (ld(rkv_ref, slot) for slot in range(5))
    ats, bts = [_split2(a) for a in at], [_split2(b) for b in bt]
    atb = [hi for hi, _ in ats]
    zero = jnp.zeros((c, tw), F32)
    gk = [_dg(cat0(atb[i], rt[i]), bd(kt[i]), _NT) for i in n]
    aak = [jnp.where(strict, gk[i][:c], zero) for i in n]
    ark = [jnp.where(incl, gk[i][c:], zero) for i in n]
    gb = [_dg(cat0(ats[i][0], ats[i][1], rt[i]), bd(bts[i][0]), _NT) for i in n]
    arb = [jnp.where(incl, gb[i][2 * c:], zero) for i in n]
    aab = [jnp.where(strict, gb[i][:c] + (gb[i][c:2 * c] + _dg(atb[i], bd(bts[i][1]), _NT)), zero) for i in n]
    z = [pmm(aak[i], v[i]) for i in n]
    a8 = [jnp.where(same8, aab[i], zero) for i in n]
    p2 = [pmm(a8[i], a8[i]) for i in n]
    p4 = [pmm(p2[i], p2[i]) for i in n]
    nn = [a8[i] + p2[i] + pmm(p2[i], a8[i]) for i in n]
    nn = [nn[i] + p4[i] + pmm(p4[i], nn[i]) for i in n]
    for lvl in (3, 4, 5):
        joins = ((col >> (lvl + 1)) == (row >> (lvl + 1))) & ((col >> lvl) != (row >> lvl))
        e = [jnp.where(joins, aab[i], zero) for i in n]
        te = [e[i] + pmm(nn[i], e[i]) for i in n]
        nn = [nn[i] + te[i] + pmm(te[i], nn[i]) for i in n]
    wu = [cat1(at[i], z[i]) + _dg(nn[i].astype(BF16), cat1(bd(atb[i]), bd(z[i].astype(BF16)))) for i in n]
    abk = [cat1(arb[i], ark[i]).astype(BF16) for i in n]
    s = {(b, pr): s_ref[b, pr] for b in range(bb) for pr in range(RWKV_TILES)}
    for ck in range(nck):
        cur = [i for i in n if items[i][1] == ck]
        key = lambda i: (items[i][0], items[i][2])
        x = {i: _dg(cat0(wu[i][:, :tw].astype(BF16), rt[i]), s[key(i)].astype(BF16), _NT) for i in cur}
        ub = {i: (x[i][:c] + wu[i][:, tw:]).astype(BF16) for i in cur}
        upd = {i: _dg(cat0(ub[i], v[i]), cat0(be[i], ke[i]), _TN) for i in cur}
        for i in cur:
            b, _, pr = items[i]
            y_ref[b, rows(ck), lanes(pr)] = x[i][c:] + _dg(abk[i], cat0(bd(ub[i]), bd(v[i])))
            s[b, pr] = (s[b, pr] * gl_ref[b, ck, :, lanes(pr)]
                        + jnp.where(same_head, upd[i], jnp.zeros_like(upd[i])))
    for (b, pr), val in s.items():
        s_ref[b, pr] = val


def _rwkv_chunk(ab, rkv, gl, s0, bb, nck):
    bsz, seq, _ = ab.shape
    w = RWKV_WIDTH
    tok = lambda a: pl.BlockSpec((bb, nck * CHUNK, a.shape[-1]), lambda i, c: (i, c, 0))
    st = pl.BlockSpec((bb, RWKV_TILES, RWKV_TILE_LANES, RWKV_TILE_LANES), lambda i, c: (i, 0, 0, 0))
    return pl.pallas_call(
        functools.partial(_rwkv_chunk_kernel, bb=bb, nck=nck),
        grid=(bsz // bb, seq // (nck * CHUNK)),
        in_specs=[tok(ab), tok(rkv), pl.BlockSpec((bb, nck, 1, w), lambda i, c: (i, c, 0, 0)), st],
        out_specs=[pl.BlockSpec((bb, nck * CHUNK, w), lambda i, c: (i, c, 0)), st],
        out_shape=[jax.ShapeDtypeStruct((bsz, seq, w), F32),
                   jax.ShapeDtypeStruct((bsz, RWKV_TILES, RWKV_TILE_LANES, RWKV_TILE_LANES), F32)],
        compiler_params=_params("parallel", "arbitrary"),
        name="rwkv_chunk",
    )(ab, rkv, gl, s0)


def _gla_kernel(p_ref, wgate_ref, bgate_ref, tri_ref, ng_ref, s0_ref, y_ref, s_ref, *, bb, nck):
    @pl.when(pl.program_id(1) == 0)
    def _():
        s_ref[...] = s0_ref[...]

    head, row, col = _head_masks(LANES)
    incl = col <= row
    bd = functools.partial(_block_diag, head=head)
    lane2 = lax.broadcasted_iota(jnp.int32, (2 * GLA_DV, LANES), 1)
    row2 = lax.broadcasted_iota(jnp.int32, (2 * GLA_DV, LANES), 0)
    same_head = (lane2 < GLA_DK) == (row2 < GLA_DV)
    kw, gw = GLA_KEY_WIDTH, GLA_WIDTH
    zero = jnp.zeros((CHUNK, LANES), F32)
    cat0 = lambda xs: jnp.concatenate(xs, axis=0)
    cat1 = lambda xs: jnp.concatenate(xs, axis=1)
    rows = lambda c: slice(c * CHUNK, (c + 1) * CHUNK)
    lanes = lambda pr: slice(pr * LANES, (pr + 1) * LANES)
    bs, cs = range(bb), range(nck)
    x = [p_ref[b] for b in bs]
    gate = [_mm1(x[b][:, 2 * kw + 2 * gw:], wgate_ref[...]) + bgate_ref[...] for b in bs]
    la = [(jnp.minimum(g, 0.0) - jnp.log(1.0 + jnp.exp(-jnp.abs(g)))) / GLA_TAU for g in gate]
    cum = [_mm_exact_lhs(tri_ref[...], la[b]) for b in bs]
    q_dec = [(x[b][:, :kw] * (GLA_DK ** -0.5)) * jnp.exp(cum[b]) for b in bs]
    k_inv = [x[b][:, kw:2 * kw] * jnp.exp(-cum[b]) for b in bs]
    last = [[cum[b][c * CHUNK + CHUNK - 1:(c + 1) * CHUNK, :] for c in cs] for b in bs]
    k_end = [[x[b][rows(c), kw:2 * kw] * jnp.exp(last[b][c] - cum[b][rows(c)]) for c in cs] for b in bs]
    dec = [[jnp.exp(last[b][c]) for c in cs] for b in bs]
    items = [(b, c, pr) for b in bs for c in cs for pr in range(GLA_PAIRS)]
    vp = {(b, c, pr): x[b][rows(c), 2 * kw + pr * 2 * GLA_DV:2 * kw + (pr + 1) * 2 * GLA_DV]
          for b, c, pr in items}
    att = {(b, c, pr): jnp.where(incl, _mm1(q_dec[b][rows(c), lanes(pr)],
                                            bd(k_inv[b][rows(c), lanes(pr)]), _NT), zero)
           for b, c, pr in items}
    upd = {it: _mm1(vp[it], k_end[it[0]][it[1]][:, lanes(it[2])], _TN) for it in items}
    intra = {it: _mm1(att[it], cat0([cat1([vp[it][:, :GLA_DV], zero]), cat1([zero, vp[it][:, GLA_DV:]])]))
             for it in items}
    st = {}
    for b in bs:
        for pr in range(GLA_PAIRS):
            cur = s_ref[b, pr]
            for c in cs:
                st[b, c, pr] = cur
                cur = cur * dec[b][c][:, lanes(pr)] + jnp.where(same_head, upd[b, c, pr], jnp.zeros_like(cur))
            s_ref[b, pr] = cur
    o = {(b, c, pr): intra[b, c, pr] + _mm1(q_dec[b][rows(c), lanes(pr)], st[b, c, pr], _NT)
         for b, c, pr in items}
    for b in bs:
        ob = cat0([cat1([o[b, c, pr] for pr in range(GLA_PAIRS)]) for c in cs])
        heads = [ob[:, h * GLA_DV:(h + 1) * GLA_DV] for h in range(GLA_HEADS)]
        normed = [oh * lax.rsqrt(jnp.mean(oh * oh, axis=-1, keepdims=True) + RMS_EPS) for oh in heads]
        y_ref[b] = cat1(normed) * ng_ref[...] * _silu(x[b][:, 2 * kw + gw:2 * kw + 2 * gw])


def _gla(proj, lp, tri, s0, bb, tg):
    bsz, seq, _ = proj.shape
    full = lambda a: pl.BlockSpec(a.shape, lambda i, c: (0,) * a.ndim)
    st = pl.BlockSpec((bb, GLA_PAIRS, 2 * GLA_DV, LANES), lambda i, c: (i, 0, 0, 0))
    return pl.pallas_call(
        functools.partial(_gla_kernel, bb=bb, nck=tg // CHUNK),
        grid=(bsz // bb, seq // tg),
        in_specs=[pl.BlockSpec((bb, tg, GLA_PROJ_PAD), lambda i, c: (i, c, 1)),
                  full(lp["wgate"]), full(lp["bgate"]), full(tri), full(lp["gla_g"]), st],
        out_specs=[pl.BlockSpec((bb, tg, GLA_WIDTH), lambda i, c: (i, c, 0)), st],
        out_shape=[jax.ShapeDtypeStruct((bsz, seq, GLA_WIDTH), F32),
                   jax.ShapeDtypeStruct((bsz, GLA_PAIRS, 2 * GLA_DV, LANES), F32)],
        compiler_params=_params("parallel", "arbitrary"),
        name="gla_chunk",
    )(proj, lp["wgate"], lp["bgate"], tri, lp["gla_g"], s0)


def _route(lgt):
    tm = lgt.shape[1]
    n_pad = ROUTE_ROWS - N_EXPERTS
    rowg = lax.broadcasted_iota(jnp.int32, (n_pad, tm), 0)
    rowe = lax.broadcasted_iota(jnp.int32, (N_EXPERTS, tm), 0)
    rowg_f, rowe_f = rowg.astype(F32), rowe.astype(F32)

    def first_argmax(vals, mx, rows_f):
        return jnp.min(jnp.where(vals == mx, rows_f, float(ROUTE_ROWS)), axis=0, keepdims=True)

    is_group = rowg < N_GROUPS
    lg = jnp.where(is_group, lgt[N_EXPERTS:], -jnp.inf)
    gmax = jnp.max(lg, axis=0, keepdims=True)
    gi = first_argmax(lg, gmax, rowg_f)
    pg_top = 1.0 / jnp.sum(jnp.where(is_group, jnp.exp(lg - gmax), 0.0), axis=0, keepdims=True)
    in_group = (rowe >> GROUP_SHIFT).astype(F32) == gi
    le = jnp.where(in_group, lgt[:N_EXPERTS], -jnp.inf)
    m1 = jnp.max(le, axis=0, keepdims=True)
    i1 = first_argmax(le, m1, rowe_f)
    le2 = jnp.where(rowe_f == i1, -jnp.inf, le)
    m2 = jnp.max(le2, axis=0, keepdims=True)
    i2 = first_argmax(le2, m2, rowe_f)
    p2 = jnp.exp(m2 - m1)
    w1 = pg_top / (1.0 + p2)
    w2 = pg_top * p2 / (1.0 + p2)
    comb = jnp.where(rowe_f == i1, w1, 0.0) + jnp.where(rowe_f == i2, w2, 0.0)
    return jnp.concatenate([comb, jnp.where(rowg == 0, gi, 0.0)], axis=0)


def _outproj_kernel(yr_ref, bg_ref, yg_ref, x_ref, gt_ref, sc_ref, sh_ref,
                    lnw_ref, lnb_ref, seg_ref, wout_ref, g2_ref, wr_ref, br_ref,
                    x1_ref, h2_ref, route_ref):
    nb, rb, d_model = x_ref.shape
    tm = nb * rb
    flat = lambda ref: ref[...].reshape(tm, ref.shape[-1])
    seg = seg_ref[...]
    y = flat(yr_ref)
    inv_n = 1.0 / RWKV_HEAD
    d = y - _mm2_exact_rhs(y, seg) * inv_n
    var = _mm2_exact_rhs(d * d, seg) * inv_n
    yn = d * lax.rsqrt(var + LNX_EPS) * lnw_ref[...] + lnb_ref[...]
    bg = flat(bg_ref)
    yr = (yn + bg[:, :RWKV_WIDTH]) * bg[:, RWKV_WIDTH:]
    mix = jnp.concatenate([yr, flat(yg_ref)], axis=1)
    x1 = x_ref[...] + gt_ref[...] * _mm1(mix, wout_ref[...]).reshape(nb, rb, d_model)
    x1_ref[...] = x1
    h2 = _rms_mod(x1, g2_ref[...], sc_ref[...], sh_ref[...])
    h2_ref[...] = h2.astype(BF16)
    route_ref[...] = _route(_mm3(wr_ref[...], h2.reshape(tm, d_model), _NT) + br_ref[...])


def _out_proj(yr, bg, yg, x, gt, sc, sh, lp, consts, rows):
    bsz, seq, d = x.shape
    nb, rb = _token_tile(bsz, seq, rows)
    n_seq_tiles = seq // rb
    half = pl.BlockSpec((nb, rb, RWKV_WIDTH), lambda b, i: (b, i, 0))
    tokd = pl.BlockSpec((nb, rb, d), lambda b, i: (b, i, 0))
    vec = pl.BlockSpec((nb, 1, d), lambda b, i: (b, 0, 0))
    full = lambda a: pl.BlockSpec(a.shape, lambda b, i: (0,) * a.ndim)
    args = (lp["lnx_w"], lp["lnx_b"], consts["seg"], lp["w_out"], lp["norm2_g"], lp["w_router"], lp["b_router"])
    return pl.pallas_call(
        _outproj_kernel,
        grid=(bsz // nb, seq // rb),
        in_specs=[half, pl.BlockSpec((nb, rb, 2 * RWKV_WIDTH), lambda b, i: (b, i, 0)), half, tokd,
                  vec, vec, vec] + [full(a) for a in args],
        out_specs=[tokd, tokd, pl.BlockSpec((ROUTE_ROWS, nb * rb), lambda b, i: (0, b * n_seq_tiles + i))],
        out_shape=[jax.ShapeDtypeStruct((bsz, seq, d), F32),
                   jax.ShapeDtypeStruct((bsz, seq, d), BF16),
                   jax.ShapeDtypeStruct((ROUTE_ROWS, bsz * seq), F32)],
        compiler_params=_params("parallel", "parallel"),
        name="out_proj_router",
    )(yr, bg, yg, x, gt, sc, sh, *args)


def _moe_kernel(h_ref, route_ref, x1_ref, gt_ref, sc_ref, sh_ref, gf_ref, tri_ref, wg_ref, wu_ref, wd_ref,
                y_ref, ys_ref, pos_ref, meta_ref):
    g = pl.program_id(2)
    nb, rb, d = h_ref.shape
    tm = nb * rb
    sub, tail = MOE_SUB_ROWS, MOE_SUB_ROWS // 2
    cap = ys_ref.shape[0]
    n_lane_tiles = tm // LANES
    lane_tile = lambda k: slice(k * LANES, (k + 1) * LANES)

    @pl.when(g == 0)
    def _sort():
        row8 = lax.broadcasted_iota(jnp.int32, (8, LANES), 0).astype(F32)
        carry = jnp.zeros((8, 1), F32)
        members, ranks = [], []
        for k in range(n_lane_tiles):
            blk = jnp.where(row8 == route_ref[N_EXPERTS:N_EXPERTS + 1, lane_tile(k)], 1.0, 0.0)
            members.append(blk)
            ranks.append(_dg(blk.astype(BF16), tri_ref[...]) - blk + carry)
            carry = carry + jnp.sum(blk, axis=1, keepdims=True)
        rowc = lax.broadcasted_iota(jnp.int32, (8, 1), 0)
        lane = lax.broadcasted_iota(jnp.int32, (1, LANES), 1)
        first = jnp.zeros((1, 1), F32)
        off_col = jnp.zeros((8, 1), F32)
        meta = jnp.zeros((1, LANES), F32)
        for grp in range(N_GROUPS):
            count = jnp.sum(jnp.where(rowc == grp, carry, 0.0), axis=0, keepdims=True)
            blocks = jnp.ceil(count * (1.0 / tail))
            pairs = jnp.floor(blocks * 0.5)
            odd = blocks - 2.0 * pairs
            triple = odd * jnp.where(blocks >= 3.0, 1.0, 0.0)
            n_full = pairs - triple
            off_col = off_col + jnp.where(rowc == grp, first, 0.0)
            meta = (meta + jnp.where(lane == grp, first, 0.0) + jnp.where(lane == N_GROUPS + grp, n_full, 0.0)
                    + jnp.where(lane == 2 * N_GROUPS + grp, odd + 2.0 * triple, 0.0))
            first = first + blocks * tail
        meta = (meta + jnp.where(lane == 3 * N_GROUPS, first, 0.0)).astype(jnp.int32)
        for i in range(3 * N_GROUPS + 1):
            meta_ref[i] = meta[0, i]
        for k in range(n_lane_tiles):
            pos = jnp.sum(members[k] * (ranks[k] + off_col), axis=0, keepdims=True)
            pos_ref[:, lane_tile(k)] = pos.astype(jnp.int32)
        ys_ref[...] = jnp.zeros_like(ys_ref)

    def experts(base, n_rows):
        prow = lax.broadcasted_iota(jnp.int32, (n_rows, tm), 0) + base
        onehot = jnp.where(prow == pos_ref[...], 1.0, 0.0).astype(BF16)
        hs = _dg(onehot, h_ref[...].reshape(tm, d)).astype(BF16)
        comb3 = _split3(route_ref[:N_EXPERTS, :])
        cs = _dg(onehot, comb3[0], _NT) + (_dg(onehot, comb3[1], _NT) + _dg(onehot, comb3[2], _NT))
        lane = lax.broadcasted_iota(jnp.int32, cs.shape, 1)
        ys = jnp.zeros((n_rows, d), F32)
        for e in range(EXPERTS_PER_GROUP):
            hid = _silu(_dg(hs, wg_ref[e])) * _dg(hs, wu_ref[e])
            ce = jnp.sum(jnp.where(lane == g * EXPERTS_PER_GROUP + e, cs, 0.0), axis=-1, keepdims=True)
            ys = ys + ce * _dg(hid.astype(BF16), wd_ref[e])
        ys_ref[pl.ds(base, n_rows), :] = ys.astype(BF16)

    first_row = meta_ref[g]
    n_full = meta_ref[N_GROUPS + g]

    def full_sub_tile(j, carry_):
        experts(pl.multiple_of(first_row + j * sub, tail), sub)
        return carry_

    lax.fori_loop(0, n_full, full_sub_tile, 0)

    for last_blocks in (1, 3):
        @pl.when(meta_ref[2 * N_GROUPS + g] == last_blocks)
        def _last(last_blocks=last_blocks):
            experts(pl.multiple_of(first_row + n_full * sub, tail), last_blocks * tail)

    def unsort(n_sorted):
        ys_all = ys_ref[:n_sorted]
        prow = lax.broadcasted_iota(jnp.int32, (n_sorted, LANES), 0)
        seqs = max(LANES // rb, 1)
        for k in range(n_lane_tiles):
            onehot = jnp.where(prow == pos_ref[:, lane_tile(k)], 1.0, 0.0).astype(BF16)
            moe = _dg(onehot, ys_all, _TN).reshape(seqs, LANES // seqs, d)
            b0 = k * LANES // rb
            r0 = k * LANES - b0 * rb
            bs, rs = slice(b0, b0 + seqs), slice(r0, r0 + LANES // seqs)
            x2 = x1_ref[bs, rs] + gt_ref[bs] * moe
            y_ref[bs, rs] = _rms_mod(x2, gf_ref[...], sc_ref[bs], sh_ref[bs])

    usual = tm + (cap - tm) // 2
    last = g == N_GROUPS - 1
    pl.when(last & (meta_ref[3 * N_GROUPS] <= usual))(lambda: unsort(usual))
    pl.when(last & (meta_ref[3 * N_GROUPS] > usual))(lambda: unsort(cap))


def _moe(h2, route, x1, gt, sc, sh, gf, wg, wu, wd, tri, rows):
    bsz, seq, d = x1.shape
    nb, rb = _token_tile(bsz, seq, rows)
    tm = nb * rb
    assert tm % LANES == 0 and (rb % LANES == 0 or LANES % rb == 0)
    n_seq_tiles = seq // rb
    cap = tm + N_GROUPS * (MOE_SUB_ROWS // 2)
    tokd = pl.BlockSpec((nb, rb, d), lambda b, i, g: (b, i, 0))
    vec = pl.BlockSpec((nb, 1, d), lambda b, i, g: (b, 0, 0))
    group_w = lambda shape: pl.BlockSpec((EXPERTS_PER_GROUP,) + shape, lambda b, i, g: (g, 0, 0))
    return pl.pallas_call(
        _moe_kernel,
        grid=(bsz // nb, n_seq_tiles, N_GROUPS),
        in_specs=[tokd, pl.BlockSpec((ROUTE_ROWS, tm), lambda b, i, g: (0, b * n_seq_tiles + i)),
                  tokd, vec, vec, vec,
                  pl.BlockSpec((1, d), lambda b, i, g: (0, 0)),
                  pl.BlockSpec(tri.shape, lambda b, i, g: (0, 0)),
                  group_w((d, D_EXPERT)), group_w((d, D_EXPERT)), group_w((D_EXPERT, d))],
        out_specs=tokd,
        out_shape=jax.ShapeDtypeStruct((bsz, seq, d), F32),
        scratch_shapes=[pltpu.VMEM((cap, d), BF16), pltpu.VMEM((1, tm), jnp.int32),
                        pltpu.SMEM((3 * N_GROUPS + 1,), jnp.int32)],
        compiler_params=_params("parallel", "parallel", "arbitrary"),
        name="moe_final_norm",
    )(h2, route, x1, gt, sc, sh, gf, tri, wg, wu, wd)


def _block_ones(n, blk, lower):
    i = np.arange(n)
    m = (i[:, None] // blk) == (i[None, :] // blk)
    if lower:
        m = m & (i[None, :] <= i[:, None])
    return jnp.asarray(m, dtype=BF16)


def _consts(tr, tg):
    return dict(seg=_block_ones(RWKV_WIDTH, RWKV_HEAD, False), tri=_block_ones(tr, CHUNK, True),
                tri_gla=_block_ones(tg, CHUNK, True), tri_up=_block_ones(LANES, LANES, True).T)


def _heads_block_diag(s):
    n, c = s.shape[2], s.shape[-1]
    rows = [jnp.pad(s[:, :, j], ((0, 0), (0, 0), (0, 0), (j * c, (n - 1 - j) * c))) for j in range(n)]
    return jnp.concatenate(rows, axis=2)


def _diag_blocks(t, n):
    r, c = t.shape[-2] // n, t.shape[-1] // n
    return jnp.stack([t[:, :, j * r:(j + 1) * r, j * c:(j + 1) * c] for j in range(n)], axis=2)


def _pad_rows(w, first_row):
    out = jnp.zeros((LORA_PAD, w.shape[1]), F32)
    return lax.dynamic_update_slice(out, w, (first_row, 0)).astype(BF16)


def _layer_params(l, w_in, mu_shift, w0, w_decay_up, a0, w_a_up, w_g_up, k_k, k_a, r_k, lnx_w, lnx_b,
                  w_gla_gate_up, b_gla_gate, gla_norm_g, w_out, norm2_g,
                  w_router_group, b_router_group, w_router_expert, b_router_expert):
    w_in_p = _win_layout(w_in[l].T, WEIGHT_TILE_COLS)
    n_pad = ROUTE_ROWS - N_EXPERTS - N_GROUPS
    w_router = jnp.concatenate([w_router_expert[l].T, w_router_group[l].T, jnp.zeros((n_pad, D_MODEL), F32)])
    b_router = jnp.concatenate([b_router_expert[l], b_router_group[l],
                                jnp.zeros((n_pad,), F32)]).reshape(ROUTE_ROWS, 1)
    wgate = jnp.zeros((LANES, GLA_KEY_WIDTH), F32).at[:GLA_GATE_RANK].set(w_gla_gate_up[l]).astype(BF16)
    r1 = lambda a: a.reshape(1, -1)
    return dict(
        w_in=w_in_p, mu=r1(mu_shift[l]), w0=r1(w0[l]), a0=r1(a0[l]), k_k=r1(k_k[l]), k_a=r1(k_a[l]),
        r_k=r1(r_k[l]), wd=_pad_rows(w_decay_up[l], 0), wa=_pad_rows(w_a_up[l], DECAY_LORA),
        wg=_pad_rows(w_g_up[l], DECAY_LORA + AAA_LORA), lnx_w=r1(lnx_w[l]), lnx_b=r1(lnx_b[l]),
        wgate=wgate, bgate=r1(b_gla_gate[l]), gla_g=r1(gla_norm_g[l]),
        w_out=w_out[l].astype(BF16), norm2_g=r1(norm2_g[l]), w_router=w_router, b_router=b_router)


def _run_layer(x, mod, shift0, wkv0, gla0, lp, experts, final):
    bsz, seq, d = x.shape
    assert seq % CHUNK == 0
    bb = 2 if seq >= RWKV_STEP_ROWS else min(bsz, RECURRENT_MAX_SEQS)
    assert bsz % bb == 0
    tg = min(seq, GLA_STEP_ROWS)
    consts = _consts(TOKEN_TILE_ROWS, tg)
    m = lambda j: mod[:, j:j + 1, :]
    sh1, sc1, gt1, sh2, sc2, gt2 = (m(j) for j in range(6))
    proj, tails = _in_proj(x, sc1, sh1, lp["norm1_g"], lp["w_in"], TOKEN_TILE_ROWS)
    tails = tails.reshape(bsz, -1, RWKV_PROJ)
    new_shift = tails[:, -1]
    prev = jnp.concatenate([shift0[:, None, :], tails[:, :-1]], axis=1).reshape(-1, 1, RWKV_PROJ)
    ab, rkv, bg, gl = _rwkv_prep(proj, prev, lp, consts, TOKEN_TILE_ROWS)
    gl = gl.reshape(bsz, seq // CHUNK, RWKV_WIDTH)
    s0 = _heads_block_diag(wkv0.reshape(bsz, RWKV_TILES, RWKV_TILE_HEADS, RWKV_HEAD, RWKV_HEAD))
    yr, s_bd = _rwkv_chunk(ab, rkv, gl[:, :, None, :], s0, bb, min(seq, RWKV_STEP_ROWS) // CHUNK)
    new_wkv = _diag_blocks(s_bd, RWKV_TILE_HEADS).reshape(bsz, RWKV_HEADS, RWKV_HEAD, RWKV_HEAD)
    t0 = _heads_block_diag(jnp.swapaxes(gla0, -1, -2).reshape(bsz, GLA_PAIRS, 2, GLA_DV, GLA_DK))
    yg, t_bd = _gla(proj, lp, consts["tri_gla"], t0, bb, tg)
    new_gla = jnp.swapaxes(_diag_blocks(t_bd, 2).reshape(bsz, GLA_HEADS, GLA_DV, GLA_DK), -1, -2)
    x1, h2, route = _out_proj(yr, bg, yg, x, gt1, sc2, sh2, lp, consts, TOKEN_TILE_ROWS)
    out = _moe(h2, route, x1, gt2, *final, *experts, consts["tri_up"], MOE_TILE_ROWS)
    return out, new_shift, new_wkv, new_gla


def kernel(x_prompt, x_sample, c_prompt, c_sample, state_rwkv_shift, state_rwkv_wkv, state_gla_kv, w_ada, b_ada, norm1_g, norm2_g, w_in, mu_shift, w0, w_decay_up, a0, w_a_up, w_g_up, k_k, k_a, r_k, lnx_w, lnx_b, w_gla_gate_up, b_gla_gate, gla_norm_g, w_out, w_router_group, b_router_group, w_router_expert, b_router_expert, w_expert_gate, w_expert_up, w_expert_down, w_ada_final, b_ada_final, normf_g):
    assert w_ada.shape[0] == 1, "the final norm is fused into the single layer's MoE kernel"
    bp, bs = x_prompt.shape[0], x_sample.shape[0]
    d = D_MODEL
    n_rows = -(-(bp + bs) // 8) * 8
    c_all = jnp.zeros((n_rows, d), F32).at[:bp].set(c_prompt).at[bp:bp + bs].set(c_sample)
    modf = _modulation(c_all, w_ada_final, b_ada_final, MODULATION_TILE_COLS).reshape(n_rows, 2, d)
    mod = _modulation(c_all, w_ada, b_ada[0], MODULATION_TILE_COLS, layer=0).reshape(n_rows, 6, d)
    lp = _layer_params(0, w_in, mu_shift, w0, w_decay_up, a0, w_a_up, w_g_up, k_k, k_a, r_k, lnx_w,
                       lnx_b, w_gla_gate_up, b_gla_gate, gla_norm_g, w_out, norm2_g,
                       w_router_group, b_router_group, w_router_expert, b_router_expert)
    lp["norm1_g"] = norm1_g[0].reshape(1, d)
    experts = (w_expert_gate[0].astype(BF16), w_expert_up[0].astype(BF16), w_expert_down[0].astype(BF16))
    groups = [
        (x_prompt, 0, bp, jnp.zeros((bp, RWKV_PROJ), F32),
         jnp.zeros((bp, RWKV_HEADS, RWKV_HEAD, RWKV_HEAD), F32), jnp.zeros((bp, GLA_HEADS, GLA_DK, GLA_DV), F32)),
        (x_sample, bp, bp + bs, state_rwkv_shift[0], state_rwkv_wkv[0], state_gla_kv[0]),
    ]
    ys, states = [], []
    for x, lo, hi, shift0, wkv0, gla0 in groups:
        final = (modf[lo:hi, 1:2], modf[lo:hi, 0:1], normf_g.reshape(1, d))
        y, *st = _run_layer(x, mod[lo:hi], shift0, wkv0, gla0, lp, experts, final)
        ys.append(y)
        states.extend(s[None] for s in st)
    return tuple(ys + states)
```

```python
import functools

import jax
import jax.numpy as jnp
import numpy as np
from jax import lax
from jax.experimental import pallas as pl
from jax.experimental.pallas import tpu as pltpu

F32 = jnp.float32
BF16 = jnp.bfloat16

LANES = 128
VMEM_LIMIT_BYTES = 56 * 1024 * 1024
TOKEN_TILE_ROWS = 512
MOE_TILE_ROWS = 1024
MOE_SUB_ROWS = 256
GLA_STEP_ROWS = 512
RECURRENT_MAX_SEQS = 8
MODULATION_TILE_COLS = 1024
WEIGHT_TILE_COLS = 256
RWKV_STEP_ROWS = 512

D_MODEL = 1024
CHUNK = 64
RWKV_WIDTH = 512
RWKV_HEAD = 64
RWKV_HEADS = RWKV_WIDTH // RWKV_HEAD
HEAD_LANES = RWKV_HEAD
HEAD_SHIFT = HEAD_LANES.bit_length() - 1
assert 1 << HEAD_SHIFT == HEAD_LANES
RWKV_TILE_LANES = 128
RWKV_TILE_HEADS = RWKV_TILE_LANES // RWKV_HEAD
RWKV_TILES = RWKV_WIDTH // RWKV_TILE_LANES
DECAY_LORA = 32
AAA_LORA = 32
GATE_LORA = 64
LORA_PAD = DECAY_LORA + AAA_LORA + GATE_LORA
RWKV_PROJ = 3 * RWKV_WIDTH + LORA_PAD
GLA_WIDTH = 512
GLA_HEADS = 4
GLA_PAIRS = GLA_HEADS // 2
GLA_DV = GLA_WIDTH // GLA_HEADS
GLA_DK = GLA_DV // 2
GLA_KEY_WIDTH = GLA_HEADS * GLA_DK
GLA_GATE_RANK = 16
GLA_TAU = 16.0
GLA_PROJ = 2 * GLA_KEY_WIDTH + 2 * GLA_WIDTH + GLA_GATE_RANK
GLA_PROJ_PAD = RWKV_PROJ
IN_PROJ_PAD = RWKV_PROJ + GLA_PROJ_PAD
N_GROUPS = 4
EXPERTS_PER_GROUP = 4
N_EXPERTS = N_GROUPS * EXPERTS_PER_GROUP
GROUP_SHIFT = EXPERTS_PER_GROUP.bit_length() - 1
assert 1 << GROUP_SHIFT == EXPERTS_PER_GROUP
ROUTE_ROWS = 24
D_EXPERT = 512
RMS_EPS = 1e-6
LNX_EPS = 64e-5
LOG2_E = 1.4426950408889634

_NN = (((1,), (0,)), ((), ()))
_NT = (((1,), (1,)), ((), ()))
_TN = (((0,), (0,)), ((), ()))


def _dg(a, b, dims=_NN):
    return lax.dot_general(a, b, dims, preferred_element_type=F32)


def _split2(x):
    hi = x.astype(BF16)
    lo = (x - hi.astype(F32)).astype(BF16)
    return hi, lo


def _split3(x):
    hi = x.astype(BF16)
    r1 = x - hi.astype(F32)
    mid = r1.astype(BF16)
    lo = (r1 - mid.astype(F32)).astype(BF16)
    return hi, mid, lo


def _mm1(a, b, dims=_NN):
    return _dg(a.astype(BF16), b.astype(BF16), dims)


def _mm3(a, b, dims=_NN):
    ah, al = _split2(a)
    bh, bl = _split2(b)
    return _dg(ah, bh, dims) + (_dg(ah, bl, dims) + _dg(al, bh, dims))


def _mm_exact_lhs(e, x, dims=_NN):
    h, m, l = _split3(x)
    return _dg(e, h, dims) + (_dg(e, m, dims) + _dg(e, l, dims))


def _mm2_exact_rhs(x, e, dims=_NN):
    h, l = _split2(x)
    return _dg(h, e, dims) + _dg(l, e, dims)


def _softplus(z):
    return jnp.maximum(z, 0.0) + jnp.log(1.0 + jnp.exp(-jnp.abs(z)))


def _sigmoid(z):
    return 1.0 / (1.0 + jnp.exp(-z))


def _silu(z):
    return z * _sigmoid(z)


def _params(*sem):
    return pltpu.CompilerParams(dimension_semantics=sem, vmem_limit_bytes=VMEM_LIMIT_BYTES)


def _mod_kernel(c_ref, w_ref, b_ref, o_ref):
    o_ref[...] = _mm1(_silu(c_ref[...]), w_ref[...]) + b_ref[...]


def _modulation(c, w, b, tn, layer=None):
    rows, d = c.shape
    n = w.shape[-1]
    if layer is None:
        w_spec = pl.BlockSpec((d, tn), lambda j: (0, j))
    else:
        w_spec = pl.BlockSpec((None, d, tn), lambda j: (layer, 0, j))
    return pl.pallas_call(
        _mod_kernel,
        grid=(n // tn,),
        in_specs=[pl.BlockSpec((rows, d), lambda j: (0, 0)), w_spec,
                  pl.BlockSpec((1, tn), lambda j: (0, j))],
        out_specs=pl.BlockSpec((rows, tn), lambda j: (0, j)),
        out_shape=jax.ShapeDtypeStruct((rows, n), F32),
        compiler_params=_params("parallel"),
        name="modulation",
    )(c, w, b.reshape(1, n))


def _win_layout_kernel(w_ref, o_ref):
    qkv_end = RWKV_PROJ + 2 * GLA_KEY_WIDTH + GLA_WIDTH
    gate_end = qkv_end + GLA_GATE_RANK
    o_ref[:qkv_end] = w_ref[:qkv_end].astype(BF16)
    o_ref[qkv_end:qkv_end + GLA_WIDTH] = w_ref[gate_end:gate_end + GLA_WIDTH].astype(BF16)
    o_ref[qkv_end + GLA_WIDTH:RWKV_PROJ + GLA_PROJ] = w_ref[qkv_end:gate_end].astype(BF16)
    o_ref[RWKV_PROJ + GLA_PROJ:] = jnp.zeros((IN_PROJ_PAD - RWKV_PROJ - GLA_PROJ, o_ref.shape[1]), BF16)


def _win_layout(wt, cols):
    n, d = wt.shape
    return pl.pallas_call(
        _win_layout_kernel,
        grid=(d // cols,),
        in_specs=[pl.BlockSpec((n, cols), lambda i: (0, i))],
        out_specs=pl.BlockSpec((IN_PROJ_PAD, cols), lambda i: (0, i)),
        out_shape=jax.ShapeDtypeStruct((IN_PROJ_PAD, d), BF16),
        compiler_params=_params("parallel"),
        name="w_in_layout",
    )(wt)


def _rms_mod(x, g, sc, sh):
    ms = jnp.mean(x * x, axis=-1, keepdims=True)
    return (x * lax.rsqrt(ms + RMS_EPS) * g) * (1.0 + sc) + sh


def _token_tile(bsz, seq, rows):
    if seq >= rows:
        assert seq % rows == 0
        return 1, rows
    nb = min(bsz, rows // seq)
    assert bsz % nb == 0
    return nb, seq


def _inproj_kernel(x_ref, sc_ref, sh_ref, g_ref, w_ref, o_ref, last_ref, *, n_step):
    nb, rb, d = x_ref.shape
    h = _rms_mod(x_ref[...], g_ref[...], sc_ref[...], sh_ref[...])
    hb = h.reshape(nb * rb, d).astype(BF16)
    for j in range(IN_PROJ_PAD // n_step):
        cols = slice(j * n_step, (j + 1) * n_step)
        o_ref[:, :, cols] = _dg(hb, w_ref[cols, :], _NT).reshape(nb, rb, n_step)
    last_ref[...] = o_ref[:, rb - 1:rb, :RWKV_PROJ]


def _in_proj(x, sc, sh, g, w, rows):
    bsz, seq, d = x.shape
    nb, rb = _token_tile(bsz, seq, rows)
    n_seq_tiles = seq // rb
    vec = pl.BlockSpec((nb, 1, d), lambda b, i: (b, 0, 0))
    return pl.pallas_call(
        functools.partial(_inproj_kernel, n_step=2 * LANES),
        grid=(bsz // nb, n_seq_tiles),
        in_specs=[pl.BlockSpec((nb, rb, d), lambda b, i: (b, i, 0)), vec, vec,
                  pl.BlockSpec((1, d), lambda b, i: (0, 0)),
                  pl.BlockSpec((IN_PROJ_PAD, d), lambda b, i: (0, 0))],
        out_specs=[pl.BlockSpec((nb, rb, IN_PROJ_PAD), lambda b, i: (b, i, 0)),
                   pl.BlockSpec((nb, 1, RWKV_PROJ), lambda b, i: (b * n_seq_tiles + i, 0, 0))],
        out_shape=[jax.ShapeDtypeStruct((bsz, seq, IN_PROJ_PAD), F32),
                   jax.ShapeDtypeStruct((bsz * n_seq_tiles, 1, RWKV_PROJ), F32)],
        compiler_params=_params("parallel", "parallel"),
        name="norm1_in_proj",
    )(x, sc, sh, g, w)


def _rwkv_prep_kernel(p_ref, prev_ref, mu_ref, w0_ref, a0_ref, kk_ref, ka_ref, rk_ref,
                      wd_ref, wa_ref, wg_ref, seg_ref, tri_ref,
                      ab_ref, rkv_ref, bg_ref, gl_ref):
    nb, rb, wp = p_ref.shape
    tr = nb * rb
    p = p_ref[...].reshape(tr, wp)
    row = lax.broadcasted_iota(jnp.int32, (nb, rb, wp), 1)
    xx = jnp.where(row == 0, prev_ref[...], pltpu.roll(p, 1, 0).reshape(nb, rb, wp)).reshape(tr, wp)
    ps = p + (xx - p) * mu_ref[...]
    w = RWKV_WIDTH
    r, k, v, lora = ps[:, :w], ps[:, w:2 * w], ps[:, 2 * w:3 * w], ps[:, 3 * w:]
    logw = -_softplus(-(w0_ref[...] + _mm1(jnp.tanh(lora), wd_ref[...]))) - 0.5
    lw = jnp.exp(logw) * (-LOG2_E)
    a = _sigmoid(a0_ref[...] + _mm1(lora, wa_ref[...]))
    g = _mm1(_sigmoid(lora), wg_ref[...])
    seg = seg_ref[...]
    kk = k * kk_ref[...]
    kk = kk / jnp.maximum(jnp.sqrt(_mm2_exact_rhs(kk * kk, seg)), 1e-12)
    k2 = k * (1.0 + (a - 1.0) * ka_ref[...])
    cum = _mm_exact_lhs(tri_ref[...], lw)
    lasts = [jnp.exp2(cum[c * CHUNK + CHUNK - 1:(c + 1) * CHUNK, :]) for c in range(tr // CHUNK)]
    for c, l in enumerate(lasts):
        gl_ref[c:c + 1, :] = l

    def to_chunk_end(val):
        return jnp.concatenate([val[c * CHUNK:(c + 1) * CHUNK] * l for c, l in enumerate(lasts)], axis=0)

    gamma = jnp.exp2(cum)
    ginv = 1.0 / gamma
    bt = (kk * a) * ginv
    kt = k2 * ginv

    def put(ref, slot, val):
        ref[:, :, slot * w:(slot + 1) * w] = val.reshape(nb, rb, w).astype(ref.dtype)

    put(ab_ref, 0, -kk * jnp.exp2(cum - lw))
    put(ab_ref, 1, bt)
    put(rkv_ref, 0, r * gamma)
    put(rkv_ref, 1, kt)
    put(rkv_ref, 2, to_chunk_end(bt))
    put(rkv_ref, 3, to_chunk_end(kt))
    put(rkv_ref, 4, v)
    put(bg_ref, 0, _mm2_exact_rhs(r * k2 * rk_ref[...], seg) * v)
    put(bg_ref, 1, g)


def _rwkv_prep(proj, prev, lp, consts, rows):
    bsz, seq, _ = proj.shape
    w = RWKV_WIDTH
    nb, rb = _token_tile(bsz, seq, rows)
    n_seq_tiles = seq // rb
    flat = lambda b, i: (b * n_seq_tiles + i, 0)
    row = lambda n: pl.BlockSpec((1, n), lambda b, i: (0, 0))
    full = lambda a: pl.BlockSpec(a.shape, lambda b, i: (0,) * a.ndim)
    packed = [(2, F32), (5, BF16), (2, F32)]
    return pl.pallas_call(
        _rwkv_prep_kernel,
        grid=(bsz // nb, n_seq_tiles),
        in_specs=[pl.BlockSpec((nb, rb, RWKV_PROJ), lambda b, i: (b, i, 0)),
                  pl.BlockSpec((nb, 1, RWKV_PROJ), lambda b, i: flat(b, i) + (0,)),
                  row(RWKV_PROJ), row(w), row(w), row(w), row(w), row(w),
                  full(lp["wd"]), full(lp["wa"]), full(lp["wg"]), full(consts["seg"]), full(consts["tri"])],
        out_specs=[pl.BlockSpec((nb, rb, k * w), lambda b, i: (b, i, 0)) for k, _ in packed]
        + [pl.BlockSpec((nb * rb // CHUNK, w), flat)],
        out_shape=[jax.ShapeDtypeStruct((bsz, seq, k * w), dt) for k, dt in packed]
        + [jax.ShapeDtypeStruct((bsz * seq // CHUNK, w), F32)],
        compiler_params=_params("parallel", "parallel"),
        name="rwkv_prep",
    )(proj, prev, lp["mu"], lp["w0"], lp["a0"], lp["k_k"], lp["k_a"], lp["r_k"],
      lp["wd"], lp["wa"], lp["wg"], consts["seg"], consts["tri"])


def _head_masks(width):
    lane = lax.broadcasted_iota(jnp.int32, (CHUNK, width), 1)
    row = lax.broadcasted_iota(jnp.int32, (CHUNK, width), 0)
    return lane >> HEAD_SHIFT, row, lane & (HEAD_LANES - 1)


def _block_diag(x, head):
    z = jnp.zeros_like(x)
    return jnp.concatenate([jnp.where(head == j, x, z) for j in range(x.shape[1] // HEAD_LANES)], axis=0)


def _rwkv_chunk_kernel(ab_ref, rkv_ref, gl_ref, s0_ref, y_ref, s_ref, *, bb, nck):
    @pl.when(pl.program_id(1) == 0)
    def _():
        s_ref[...] = s0_ref[...]

    tw = RWKV_TILE_LANES
    head, row, col = _head_masks(tw)
    strict = col < row
    incl = col <= row
    same8 = (col >> 3) == (row >> 3)
    lane2 = lax.broadcasted_iota(jnp.int32, (tw, tw), 1)
    row2 = lax.broadcasted_iota(jnp.int32, (tw, tw), 0)
    same_head = (lane2 >> HEAD_SHIFT) == (row2 >> HEAD_SHIFT)
    bd = functools.partial(_block_diag, head=head)
    c = CHUNK

    def pmm(p, q):
        return _dg(p.astype(BF16), bd(q.astype(BF16)))

    items = [(b, ck, pr) for b in range(bb) for ck in range(nck) for pr in range(RWKV_TILES)]
    n = range(len(items))
    rows = lambda ck: slice(ck * c, (ck + 1) * c)
    lanes = lambda pr: slice(pr * tw, (pr + 1) * tw)
    w = RWKV_WIDTH

    def ld(ref, slot):
        return [ref[b, rows(ck), slot * w + pr * tw:slot * w + (pr + 1) * tw] for b, ck, pr in items]

    cat0 = lambda *xs: jnp.concatenate(xs, axis=0)
    cat1 = lambda *xs: jnp.concatenate(xs, axis=1)
    at, bt = ld(ab_ref, 0), ld(ab_ref, 1)
    rt, kt, be, ke, v = (ld(rkv_ref, slot) for slot in range(5))
    ats, bts = [_split2(a) for a in at], [_split2(b) for b in bt]
    atb = [hi for hi, _ in ats]
    zero = jnp.zeros((c, tw), F32)
    gk = [_dg(cat0(atb[i], rt[i]), bd(kt[i]), _NT) for i in n]
    aak = [jnp.where(strict, gk[i][:c], zero) for i in n]
    ark = [jnp.where(incl, gk[i][c:], zero) for i in n]
    gb = [_dg(cat0(ats[i][0], ats[i][1], rt[i]), bd(bts[i][0]), _NT) for i in n]
    arb = [jnp.where(incl, gb[i][2 * c:], zero) for i in n]
    aab = [jnp.where(strict, gb[i][:c] + (gb[i][c:2 * c] + _dg(atb[i], bd(bts[i][1]), _NT)), zero) for i in n]
    z = [pmm(aak[i], v[i]) for i in n]
    a8 = [jnp.where(same8, aab[i], zero) for i in n]
    p2 = [pmm(a8[i], a8[i]) for i in n]
    p4 = [pmm(p2[i], p2[i]) for i in n]
    nn = [a8[i] + p2[i] + pmm(p2[i], a8[i]) for i in n]
    nn = [nn[i] + p4[i] + pmm(p4[i], nn[i]) for i in n]
    for lvl in (3, 4, 5):
        joins = ((col >> (lvl + 1)) == (row >> (lvl + 1))) & ((col >> lvl) != (row >> lvl))
        e = [jnp.where(joins, aab[i], zero) for i in n]
        te = [e[i] + pmm(nn[i], e[i]) for i in n]
        nn = [nn[i] + te[i] + pmm(te[i], nn[i]) for i in n]
    wu = [cat1(at[i], z[i]) + _dg(nn[i].astype(BF16), cat1(bd(atb[i]), bd(z[i].astype(BF16)))) for i in n]
    abk = [cat1(arb[i], ark[i]).astype(BF16) for i in n]
    s = {(b, pr): s_ref[b, pr] for b in range(bb) for pr in range(RWKV_TILES)}
    for ck in range(nck):
        cur = [i for i in n if items[i][1] == ck]
        key = lambda i: (items[i][0], items[i][2])
        x = {i: _dg(cat0(wu[i][:, :tw].astype(BF16), rt[i]), s[key(i)].astype(BF16), _NT) for i in cur}
        ub = {i: (x[i][:c] + wu[i][:, tw:]).astype(BF16) for i in cur}
        upd = {i: _dg(cat0(ub[i], v[i]), cat0(be[i], ke[i]), _TN) for i in cur}
        for i in cur:
            b, _, pr = items[i]
            y_ref[b, rows(ck), lanes(pr)] = x[i][c:] + _dg(abk[i], cat0(bd(ub[i]), bd(v[i])))
            s[b, pr] = (s[b, pr] * gl_ref[b, ck, :, lanes(pr)]
                        + jnp.where(same_head, upd[i], jnp.zeros_like(upd[i])))
    for (b, pr), val in s.items():
        s_ref[b, pr] = val


def _rwkv_chunk(ab, rkv, gl, s0, bb, nck):
    bsz, seq, _ = ab.shape
    w = RWKV_WIDTH
    tok = lambda a: pl.BlockSpec((bb, nck * CHUNK, a.shape[-1]), lambda i, c: (i, c, 0))
    st = pl.BlockSpec((bb, RWKV_TILES, RWKV_TILE_LANES, RWKV_TILE_LANES), lambda i, c: (i, 0, 0, 0))
    return pl.pallas_call(
        functools.partial(_rwkv_chunk_kernel, bb=bb, nck=nck),
        grid=(bsz // bb, seq // (nck * CHUNK)),
        in_specs=[tok(ab), tok(rkv), pl.BlockSpec((bb, nck, 1, w), lambda i, c: (i, c, 0, 0)), st],
        out_specs=[pl.BlockSpec((bb, nck * CHUNK, w), lambda i, c: (i, c, 0)), st],
        out_shape=[jax.ShapeDtypeStruct((bsz, seq, w), F32),
                   jax.ShapeDtypeStruct((bsz, RWKV_TILES, RWKV_TILE_LANES, RWKV_TILE_LANES), F32)],
        compiler_params=_params("parallel", "arbitrary"),
        name="rwkv_chunk",
    )(ab, rkv, gl, s0)


def _gla_kernel(p_ref, wgate_ref, bgate_ref, tri_ref, ng_ref, s0_ref, y_ref, s_ref, *, bb, nck):
    @pl.when(pl.program_id(1) == 0)
    def _():
        s_ref[...] = s0_ref[...]

    head, row, col = _head_masks(LANES)
    incl = col <= row
    bd = functools.partial(_block_diag, head=head)
    lane2 = lax.broadcasted_iota(jnp.int32, (2 * GLA_DV, LANES), 1)
    row2 = lax.broadcasted_iota(jnp.int32, (2 * GLA_DV, LANES), 0)
    same_head = (lane2 < GLA_DK) == (row2 < GLA_DV)
    kw, gw = GLA_KEY_WIDTH, GLA_WIDTH
    zero = jnp.zeros((CHUNK, LANES), F32)
    cat0 = lambda xs: jnp.concatenate(xs, axis=0)
    cat1 = lambda xs: jnp.concatenate(xs, axis=1)
    rows = lambda c: slice(c * CHUNK, (c + 1) * CHUNK)
    lanes = lambda pr: slice(pr * LANES, (pr + 1) * LANES)
    bs, cs = range(bb), range(nck)
    x = [p_ref[b] for b in bs]
    gate = [_mm1(x[b][:, 2 * kw + 2 * gw:], wgate_ref[...]) + bgate_ref[...] for b in bs]
    la = [(jnp.minimum(g, 0.0) - jnp.log(1.0 + jnp.exp(-jnp.abs(g)))) / GLA_TAU for g in gate]
    cum = [_mm_exact_lhs(tri_ref[...], la[b]) for b in bs]
    q_dec = [(x[b][:, :kw] * (GLA_DK ** -0.5)) * jnp.exp(cum[b]) for b in bs]
    k_inv = [x[b][:, kw:2 * kw] * jnp.exp(-cum[b]) for b in bs]
    last = [[cum[b][c * CHUNK + CHUNK - 1:(c + 1) * CHUNK, :] for c in cs] for b in bs]
    k_end = [[x[b][rows(c), kw:2 * kw] * jnp.exp(last[b][c] - cum[b][rows(c)]) for c in cs] for b in bs]
    dec = [[jnp.exp(last[b][c]) for c in cs] for b in bs]
    items = [(b, c, pr) for b in bs for c in cs for pr in range(GLA_PAIRS)]
    vp = {(b, c, pr): x[b][rows(c), 2 * kw + pr * 2 * GLA_DV:2 * kw + (pr + 1) * 2 * GLA_DV]
          for b, c, pr in items}
    att = {(b, c, pr): jnp.where(incl, _mm1(q_dec[b][rows(c), lanes(pr)],
                                            bd(k_inv[b][rows(c), lanes(pr)]), _NT), zero)
           for b, c, pr in items}
    upd = {it: _mm1(vp[it], k_end[it[0]][it[1]][:, lanes(it[2])], _TN) for it in items}
    intra = {it: _mm1(att[it], cat0([cat1([vp[it][:, :GLA_DV], zero]), cat1([zero, vp[it][:, GLA_DV:]])]))
             for it in items}
    st = {}
    for b in bs:
        for pr in range(GLA_PAIRS):
            cur = s_ref[b, pr]
            for c in cs:
                st[b, c, pr] = cur
                cur = cur * dec[b][c][:, lanes(pr)] + jnp.where(same_head, upd[b, c, pr], jnp.zeros_like(cur))
            s_ref[b, pr] = cur
    o = {(b, c, pr): intra[b, c, pr] + _mm1(q_dec[b][rows(c), lanes(pr)], st[b, c, pr], _NT)
         for b, c, pr in items}
    for b in bs:
        ob = cat0([cat1([o[b, c, pr] for pr in range(GLA_PAIRS)]) for c in cs])
        heads = [ob[:, h * GLA_DV:(h + 1) * GLA_DV] for h in range(GLA_HEADS)]
        normed = [oh * lax.rsqrt(jnp.mean(oh * oh, axis=-1, keepdims=True) + RMS_EPS) for oh in heads]
        y_ref[b] = cat1(normed) * ng_ref[...] * _silu(x[b][:, 2 * kw + gw:2 * kw + 2 * gw])


def _gla(proj, lp, tri, s0, bb, tg):
    bsz, seq, _ = proj.shape
    full = lambda a: pl.BlockSpec(a.shape, lambda i, c: (0,) * a.ndim)
    st = pl.BlockSpec((bb, GLA_PAIRS, 2 * GLA_DV, LANES), lambda i, c: (i, 0, 0, 0))
    return pl.pallas_call(
        functools.partial(_gla_kernel, bb=bb, nck=tg // CHUNK),
        grid=(bsz // bb, seq // tg),
        in_specs=[pl.BlockSpec((bb, tg, GLA_PROJ_PAD), lambda i, c: (i, c, 1)),
                  full(lp["wgate"]), full(lp["bgate"]), full(tri), full(lp["gla_g"]), st],
        out_specs=[pl.BlockSpec((bb, tg, GLA_WIDTH), lambda i, c: (i, c, 0)), st],
        out_shape=[jax.ShapeDtypeStruct((bsz, seq, GLA_WIDTH), F32),
                   jax.ShapeDtypeStruct((bsz, GLA_PAIRS, 2 * GLA_DV, LANES), F32)],
        compiler_params=_params("parallel", "arbitrary"),
        name="gla_chunk",
    )(proj, lp["wgate"], lp["bgate"], tri, lp["gla_g"], s0)


def _route(lgt):
    tm = lgt.shape[1]
    n_pad = ROUTE_ROWS - N_EXPERTS
    rowg = lax.broadcasted_iota(jnp.int32, (n_pad, tm), 0)
    rowe = lax.broadcasted_iota(jnp.int32, (N_EXPERTS, tm), 0)
    rowg_f, rowe_f = rowg.astype(F32), rowe.astype(F32)

    def first_argmax(vals, mx, rows_f):
        return jnp.min(jnp.where(vals == mx, rows_f, float(ROUTE_ROWS)), axis=0, keepdims=True)

    is_group = rowg < N_GROUPS
    lg = jnp.where(is_group, lgt[N_EXPERTS:], -jnp.inf)
    gmax = jnp.max(lg, axis=0, keepdims=True)
    gi = first_argmax(lg, gmax, rowg_f)
    pg_top = 1.0 / jnp.sum(jnp.where(is_group, jnp.exp(lg - gmax), 0.0), axis=0, keepdims=True)
    in_group = (rowe >> GROUP_SHIFT).astype(F32) == gi
    le = jnp.where(in_group, lgt[:N_EXPERTS], -jnp.inf)
    m1 = jnp.max(le, axis=0, keepdims=True)
    i1 = first_argmax(le, m1, rowe_f)
    le2 = jnp.where(rowe_f == i1, -jnp.inf, le)
    m2 = jnp.max(le2, axis=0, keepdims=True)
    i2 = first_argmax(le2, m2, rowe_f)
    p2 = jnp.exp(m2 - m1)
    w1 = pg_top / (1.0 + p2)
    w2 = pg_top * p2 / (1.0 + p2)
    comb = jnp.where(rowe_f == i1, w1, 0.0) + jnp.where(rowe_f == i2, w2, 0.0)
    return jnp.concatenate([comb, jnp.where(rowg == 0, gi, 0.0)], axis=0)


def _outproj_kernel(yr_ref, bg_ref, yg_ref, x_ref, gt_ref, sc_ref, sh_ref,
                    lnw_ref, lnb_ref, seg_ref, wout_ref, g2_ref, wr_ref, br_ref,
                    x1_ref, h2_ref, route_ref):
    nb, rb, d_model = x_ref.shape
    tm = nb * rb
    flat = lambda ref: ref[...].reshape(tm, ref.shape[-1])
    seg = seg_ref[...]
    y = flat(yr_ref)
    inv_n = 1.0 / RWKV_HEAD
    d = y - _mm2_exact_rhs(y, seg) * inv_n
    var = _mm2_exact_rhs(d * d, seg) * inv_n
    yn = d * lax.rsqrt(var + LNX_EPS) * lnw_ref[...] + lnb_ref[...]
    bg = flat(bg_ref)
    yr = (yn + bg[:, :RWKV_WIDTH]) * bg[:, RWKV_WIDTH:]
    mix = jnp.concatenate([yr, flat(yg_ref)], axis=1)
    x1 = x_ref[...] + gt_ref[...] * _mm1(mix, wout_ref[...]).reshape(nb, rb, d_model)
    x1_ref[...] = x1
    h2 = _rms_mod(x1, g2_ref[...], sc_ref[...], sh_ref[...])
    h2_ref[...] = h2.astype(BF16)
    route_ref[...] = _route(_mm3(wr_ref[...], h2.reshape(tm, d_model), _NT) + br_ref[...])


def _out_proj(yr, bg, yg, x, gt, sc, sh, lp, consts, rows):
    bsz, seq, d = x.shape
    nb, rb = _token_tile(bsz, seq, rows)
    n_seq_tiles = seq // rb
    half = pl.BlockSpec((nb, rb, RWKV_WIDTH), lambda b, i: (b, i, 0))
    tokd = pl.BlockSpec((nb, rb, d), lambda b, i: (b, i, 0))
    vec = pl.BlockSpec((nb, 1, d), lambda b, i: (b, 0, 0))
    full = lambda a: pl.BlockSpec(a.shape, lambda b, i: (0,) * a.ndim)
    args = (lp["lnx_w"], lp["lnx_b"], consts["seg"], lp["w_out"], lp["norm2_g"], lp["w_router"], lp["b_router"])
    return pl.pallas_call(
        _outproj_kernel,
        grid=(bsz // nb, seq // rb),
        in_specs=[half, pl.BlockSpec((nb, rb, 2 * RWKV_WIDTH), lambda b, i: (b, i, 0)), half, tokd,
                  vec, vec, vec] + [full(a) for a in args],
        out_specs=[tokd, tokd, pl.BlockSpec((ROUTE_ROWS, nb * rb), lambda b, i: (0, b * n_seq_tiles + i))],
        out_shape=[jax.ShapeDtypeStruct((bsz, seq, d), F32),
                   jax.ShapeDtypeStruct((bsz, seq, d), BF16),
                   jax.ShapeDtypeStruct((ROUTE_ROWS, bsz * seq), F32)],
        compiler_params=_params("parallel", "parallel"),
        name="out_proj_router",
    )(yr, bg, yg, x, gt, sc, sh, *args)


def _moe_kernel(h_ref, route_ref, x1_ref, gt_ref, sc_ref, sh_ref, gf_ref, tri_ref, wg_ref, wu_ref, wd_ref,
                y_ref, ys_ref, pos_ref, meta_ref):
    g = pl.program_id(2)
    nb, rb, d = h_ref.shape
    tm = nb * rb
    sub, tail = MOE_SUB_ROWS, MOE_SUB_ROWS // 2
    cap = ys_ref.shape[0]
    n_lane_tiles = tm // LANES
    lane_tile = lambda k: slice(k * LANES, (k + 1) * LANES)

    @pl.when(g == 0)
    def _sort():
        row8 = lax.broadcasted_iota(jnp.int32, (8, LANES), 0).astype(F32)
        carry = jnp.zeros((8, 1), F32)
        members, ranks = [], []
        for k in range(n_lane_tiles):
            blk = jnp.where(row8 == route_ref[N_EXPERTS:N_EXPERTS + 1, lane_tile(k)], 1.0, 0.0)
            members.append(blk)
            ranks.append(_dg(blk.astype(BF16), tri_ref[...]) - blk + carry)
            carry = carry + jnp.sum(blk, axis=1, keepdims=True)
        rowc = lax.broadcasted_iota(jnp.int32, (8, 1), 0)
        lane = lax.broadcasted_iota(jnp.int32, (1, LANES), 1)
        first = jnp.zeros((1, 1), F32)
        off_col = jnp.zeros((8, 1), F32)
        meta = jnp.zeros((1, LANES), F32)
        for grp in range(N_GROUPS):
            count = jnp.sum(jnp.where(rowc == grp, carry, 0.0), axis=0, keepdims=True)
            blocks = jnp.ceil(count * (1.0 / tail))
            pairs = jnp.floor(blocks * 0.5)
            odd = blocks - 2.0 * pairs
            triple = odd * jnp.where(blocks >= 3.0, 1.0, 0.0)
            n_full = pairs - triple
            off_col = off_col + jnp.where(rowc == grp, first, 0.0)
            meta = (meta + jnp.where(lane == grp, first, 0.0) + jnp.where(lane == N_GROUPS + grp, n_full, 0.0)
                    + jnp.where(lane == 2 * N_GROUPS + grp, odd + 2.0 * triple, 0.0))
            first = first + blocks * tail
        meta = (meta + jnp.where(lane == 3 * N_GROUPS, first, 0.0)).astype(jnp.int32)
        for i in range(3 * N_GROUPS + 1):
            meta_ref[i] = meta[0, i]
        for k in range(n_lane_tiles):
            pos = jnp.sum(members[k] * (ranks[k] + off_col), axis=0, keepdims=True)
            pos_ref[:, lane_tile(k)] = pos.astype(jnp.int32)
        ys_ref[...] = jnp.zeros_like(ys_ref)

    def experts(base, n_rows):
        prow = lax.broadcasted_iota(jnp.int32, (n_rows, tm), 0) + base
        onehot = jnp.where(prow == pos_ref[...], 1.0, 0.0).astype(BF16)
        hs = _dg(onehot, h_ref[...].reshape(tm, d)).astype(BF16)
        comb3 = _split3(route_ref[:N_EXPERTS, :])
        cs = _dg(onehot, comb3[0], _NT) + (_dg(onehot, comb3[1], _NT) + _dg(onehot, comb3[2], _NT))
        lane = lax.broadcasted_iota(jnp.int32, cs.shape, 1)
        ys = jnp.zeros((n_rows, d), F32)
        for e in range(EXPERTS_PER_GROUP):
            hid = _silu(_dg(hs, wg_ref[e])) * _dg(hs, wu_ref[e])
            ce = jnp.sum(jnp.where(lane == g * EXPERTS_PER_GROUP + e, cs, 0.0), axis=-1, keepdims=True)
            ys = ys + ce * _dg(hid.astype(BF16), wd_ref[e])
        ys_ref[pl.ds(base, n_rows), :] = ys.astype(BF16)

    first_row = meta_ref[g]
    n_full = meta_ref[N_GROUPS + g]

    def full_sub_tile(j, carry_):
        experts(pl.multiple_of(first_row + j * sub, tail), sub)
        return carry_

    lax.fori_loop(0, n_full, full_sub_tile, 0)

    for last_blocks in (1, 3):
        @pl.when(meta_ref[2 * N_GROUPS + g] == last_blocks)
        def _last(last_blocks=last_blocks):
            experts(pl.multiple_of(first_row + n_full * sub, tail), last_blocks * tail)

    def unsort(n_sorted):
        ys_all = ys_ref[:n_sorted]
        prow = lax.broadcasted_iota(jnp.int32, (n_sorted, LANES), 0)
        seqs = max(LANES // rb, 1)
        for k in range(n_lane_tiles):
            onehot = jnp.where(prow == pos_ref[:, lane_tile(k)], 1.0, 0.0).astype(BF16)
            moe = _dg(onehot, ys_all, _TN).reshape(seqs, LANES // seqs, d)
            b0 = k * LANES // rb
            r0 = k * LANES - b0 * rb
            bs, rs = slice(b0, b0 + seqs), slice(r0, r0 + LANES // seqs)
            x2 = x1_ref[bs, rs] + gt_ref[bs] * moe
            y_ref[bs, rs] = _rms_mod(x2, gf_ref[...], sc_ref[bs], sh_ref[bs])

    usual = tm + (cap - tm) // 2
    last = g == N_GROUPS - 1
    pl.when(last & (meta_ref[3 * N_GROUPS] <= usual))(lambda: unsort(usual))
    pl.when(last & (meta_ref[3 * N_GROUPS] > usual))(lambda: unsort(cap))


def _moe(h2, route, x1, gt, sc, sh, gf, wg, wu, wd, tri, rows):
    bsz, seq, d = x1.shape
    nb, rb = _token_tile(bsz, seq, rows)
    tm = nb * rb
    assert tm % LANES == 0 and (rb % LANES == 0 or LANES % rb == 0)
    n_seq_tiles = seq // rb
    cap = tm + N_GROUPS * (MOE_SUB_ROWS // 2)
    tokd = pl.BlockSpec((nb, rb, d), lambda b, i, g: (b, i, 0))
    vec = pl.BlockSpec((nb, 1, d), lambda b, i, g: (b, 0, 0))
    group_w = lambda shape: pl.BlockSpec((EXPERTS_PER_GROUP,) + shape, lambda b, i, g: (g, 0, 0))
    return pl.pallas_call(
        _moe_kernel,
        grid=(bsz // nb, n_seq_tiles, N_GROUPS),
        in_specs=[tokd, pl.BlockSpec((ROUTE_ROWS, tm), lambda b, i, g: (0, b * n_seq_tiles + i)),
                  tokd, vec, vec, vec,
                  pl.BlockSpec((1, d), lambda b, i, g: (0, 0)),
                  pl.BlockSpec(tri.shape, lambda b, i, g: (0, 0)),
                  group_w((d, D_EXPERT)), group_w((d, D_EXPERT)), group_w((D_EXPERT, d))],
        out_specs=tokd,
        out_shape=jax.ShapeDtypeStruct((bsz, seq, d), F32),
        scratch_shapes=[pltpu.VMEM((cap, d), BF16), pltpu.VMEM((1, tm), jnp.int32),
                        pltpu.SMEM((3 * N_GROUPS + 1,), jnp.int32)],
        compiler_params=_params("parallel", "parallel", "arbitrary"),
        name="moe_final_norm",
    )(h2, route, x1, gt, sc, sh, gf, tri, wg, wu, wd)


def _block_ones(n, blk, lower):
    i = np.arange(n)
    m = (i[:, None] // blk) == (i[None, :] // blk)
    if lower:
        m = m & (i[None, :] <= i[:, None])
    return jnp.asarray(m, dtype=BF16)


def _consts(tr, tg):
    return dict(seg=_block_ones(RWKV_WIDTH, RWKV_HEAD, False), tri=_block_ones(tr, CHUNK, True),
                tri_gla=_block_ones(tg, CHUNK, True), tri_up=_block_ones(LANES, LANES, True).T)


def _heads_block_diag(s):
    n, c = s.shape[2], s.shape[-1]
    rows = [jnp.pad(s[:, :, j], ((0, 0), (0, 0), (0, 0), (j * c, (n - 1 - j) * c))) for j in range(n)]
    return jnp.concatenate(rows, axis=2)


def _diag_blocks(t, n):
    r, c = t.shape[-2] // n, t.shape[-1] // n
    return jnp.stack([t[:, :, j * r:(j + 1) * r, j * c:(j + 1) * c] for j in range(n)], axis=2)


def _pad_rows(w, first_row):
    out = jnp.zeros((LORA_PAD, w.shape[1]), F32)
    return lax.dynamic_update_slice(out, w, (first_row, 0)).astype(BF16)


def _layer_params(l, w_in, mu_shift, w0, w_decay_up, a0, w_a_up, w_g_up, k_k, k_a, r_k, lnx_w, lnx_b,
                  w_gla_gate_up, b_gla_gate, gla_norm_g, w_out, norm2_g,
                  w_router_group, b_router_group, w_router_expert, b_router_expert):
    w_in_p = _win_layout(w_in[l].T, WEIGHT_TILE_COLS)
    n_pad = ROUTE_ROWS - N_EXPERTS - N_GROUPS
    w_router = jnp.concatenate([w_router_expert[l].T, w_router_group[l].T, jnp.zeros((n_pad, D_MODEL), F32)])
    b_router = jnp.concatenate([b_router_expert[l], b_router_group[l],
                                jnp.zeros((n_pad,), F32)]).reshape(ROUTE_ROWS, 1)
    wgate = jnp.zeros((LANES, GLA_KEY_WIDTH), F32).at[:GLA_GATE_RANK].set(w_gla_gate_up[l]).astype(BF16)
    r1 = lambda a: a.reshape(1, -1)
    return dict(
        w_in=w_in_p, mu=r1(mu_shift[l]), w0=r1(w0[l]), a0=r1(a0[l]), k_k=r1(k_k[l]), k_a=r1(k_a[l]),
        r_k=r1(r_k[l]), wd=_pad_rows(w_decay_up[l], 0), wa=_pad_rows(w_a_up[l], DECAY_LORA),
        wg=_pad_rows(w_g_up[l], DECAY_LORA + AAA_LORA), lnx_w=r1(lnx_w[l]), lnx_b=r1(lnx_b[l]),
        wgate=wgate, bgate=r1(b_gla_gate[l]), gla_g=r1(gla_norm_g[l]),
        w_out=w_out[l].astype(BF16), norm2_g=r1(norm2_g[l]), w_router=w_router, b_router=b_router)


def _run_layer(x, mod, shift0, wkv0, gla0, lp, experts, final):
    bsz, seq, d = x.shape
    assert seq % CHUNK == 0
    bb = 2 if seq >= RWKV_STEP_ROWS else min(bsz, RECURRENT_MAX_SEQS)
    assert bsz % bb == 0
    tg = min(seq, GLA_STEP_ROWS)
    consts = _consts(TOKEN_TILE_ROWS, tg)
    m = lambda j: mod[:, j:j + 1, :]
    sh1, sc1, gt1, sh2, sc2, gt2 = (m(j) for j in range(6))
    proj, tails = _in_proj(x, sc1, sh1, lp["norm1_g"], lp["w_in"], TOKEN_TILE_ROWS)
    tails = tails.reshape(bsz, -1, RWKV_PROJ)
    new_shift = tails[:, -1]
    prev = jnp.concatenate([shift0[:, None, :], tails[:, :-1]], axis=1).reshape(-1, 1, RWKV_PROJ)
    ab, rkv, bg, gl = _rwkv_prep(proj, prev, lp, consts, TOKEN_TILE_ROWS)
    gl = gl.reshape(bsz, seq // CHUNK, RWKV_WIDTH)
    s0 = _heads_block_diag(wkv0.reshape(bsz, RWKV_TILES, RWKV_TILE_HEADS, RWKV_HEAD, RWKV_HEAD))
    yr, s_bd = _rwkv_chunk(ab, rkv, gl[:, :, None, :], s0, bb, min(seq, RWKV_STEP_ROWS) // CHUNK)
    new_wkv = _diag_blocks(s_bd, RWKV_TILE_HEADS).reshape(bsz, RWKV_HEADS, RWKV_HEAD, RWKV_HEAD)
    t0 = _heads_block_diag(jnp.swapaxes(gla0, -1, -2).reshape(bsz, GLA_PAIRS, 2, GLA_DV, GLA_DK))
    yg, t_bd = _gla(proj, lp, consts["tri_gla"], t0, bb, tg)
    new_gla = jnp.swapaxes(_diag_blocks(t_bd, 2).reshape(bsz, GLA_HEADS, GLA_DV, GLA_DK), -1, -2)
    x1, h2, route = _out_proj(yr, bg, yg, x, gt1, sc2, sh2, lp, consts, TOKEN_TILE_ROWS)
    out = _moe(h2, route, x1, gt2, *final, *experts, consts["tri_up"], MOE_TILE_ROWS)
    return out, new_shift, new_wkv, new_gla


def kernel(x_prompt, x_sample, c_prompt, c_sample, state_rwkv_shift, state_rwkv_wkv, state_gla_kv, w_ada, b_ada, norm1_g, norm2_g, w_in, mu_shift, w0, w_decay_up, a0, w_a_up, w_g_up, k_k, k_a, r_k, lnx_w, lnx_b, w_gla_gate_up, b_gla_gate, gla_norm_g, w_out, w_router_group, b_router_group, w_router_expert, b_router_expert, w_expert_gate, w_expert_up, w_expert_down, w_ada_final, b_ada_final, normf_g):
    assert w_ada.shape[0] == 1, "the final norm is fused into the single layer's MoE kernel"
    bp, bs = x_prompt.shape[0], x_sample.shape[0]
    d = D_MODEL
    n_rows = -(-(bp + bs) // 8) * 8
    c_all = jnp.zeros((n_rows, d), F32).at[:bp].set(c_prompt).at[bp:bp + bs].set(c_sample)
    modf = _modulation(c_all, w_ada_final, b_ada_final, MODULATION_TILE_COLS).reshape(n_rows, 2, d)
    mod = _modulation(c_all, w_ada, b_ada[0], MODULATION_TILE_COLS, layer=0).reshape(n_rows, 6, d)
    lp = _layer_params(0, w_in, mu_shift, w0, w_decay_up, a0, w_a_up, w_g_up, k_k, k_a, r_k, lnx_w,
                       lnx_b, w_gla_gate_up, b_gla_gate, gla_norm_g, w_out, norm2_g,
                       w_router_group, b_router_group, w_router_expert, b_router_expert)
    lp["norm1_g"] = norm1_g[0].reshape(1, d)
    experts = (w_expert_gate[0].astype(BF16), w_expert_up[0].astype(BF16), w_expert_down[0].astype(BF16))
    groups = [
        (x_prompt, 0, bp, jnp.zeros((bp, RWKV_PROJ), F32),
         jnp.zeros((bp, RWKV_HEADS, RWKV_HEAD, RWKV_HEAD), F32), jnp.zeros((bp, GLA_HEADS, GLA_DK, GLA_DV), F32)),
        (x_sample, bp, bp + bs, state_rwkv_shift[0], state_rwkv_wkv[0], state_gla_kv[0]),
    ]
    ys, states = [], []
    for x, lo, hi, shift0, wkv0, gla0 in groups:
        final = (modf[lo:hi, 1:2], modf[lo:hi, 0:1], normf_g.reshape(1, d))
        y, *st = _run_layer(x, mod[lo:hi], shift0, wkv0, gla0, lp, experts, final)
        ys.append(y)
        states.extend(s[None] for s in st)
    return tuple(ys + states)
```

```python
import functools

import jax
import jax.numpy as jnp
import numpy as np
from jax import lax
from jax.experimental import pallas as pl
from jax.experimental.pallas import tpu as pltpu

F32 = jnp.float32
BF16 = jnp.bfloat16

LANES = 128
VMEM_LIMIT_BYTES = 56 * 1024 * 1024
TOKEN_TILE_ROWS = 512
MOE_TILE_ROWS = 1024
MOE_SUB_ROWS = 256
GLA_STEP_ROWS = 512
RECURRENT_MAX_SEQS = 8
MODULATION_TILE_COLS = 1024
WEIGHT_TILE_COLS = 256
RWKV_STEP_ROWS = 512

D_MODEL = 1024
CHUNK = 64
RWKV_WIDTH = 512
RWKV_HEAD = 64
RWKV_HEADS = RWKV_WIDTH // RWKV_HEAD
HEAD_LANES = RWKV_HEAD
HEAD_SHIFT = HEAD_LANES.bit_length() - 1
assert 1 << HEAD_SHIFT == HEAD_LANES
RWKV_TILE_LANES = 128
RWKV_TILE_HEADS = RWKV_TILE_LANES // RWKV_HEAD
RWKV_TILES = RWKV_WIDTH // RWKV_TILE_LANES
DECAY_LORA = 32
AAA_LORA = 32
GATE_LORA = 64
LORA_PAD = DECAY_LORA + AAA_LORA + GATE_LORA
RWKV_PROJ = 3 * RWKV_WIDTH + LORA_PAD
GLA_WIDTH = 512
GLA_HEADS = 4
GLA_PAIRS = GLA_HEADS // 2
GLA_DV = GLA_WIDTH // GLA_HEADS
GLA_DK = GLA_DV // 2
GLA_KEY_WIDTH = GLA_HEADS * GLA_DK
GLA_GATE_RANK = 16
GLA_TAU = 16.0
GLA_PROJ = 2 * GLA_KEY_WIDTH + 2 * GLA_WIDTH + GLA_GATE_RANK
GLA_PROJ_PAD = RWKV_PROJ
IN_PROJ_PAD = RWKV_PROJ + GLA_PROJ_PAD
N_GROUPS = 4
EXPERTS_PER_GROUP = 4
N_EXPERTS = N_GROUPS * EXPERTS_PER_GROUP
GROUP_SHIFT = EXPERTS_PER_GROUP.bit_length() - 1
assert 1 << GROUP_SHIFT == EXPERTS_PER_GROUP
ROUTE_ROWS = 24
D_EXPERT = 512
RMS_EPS = 1e-6
LNX_EPS = 64e-5
LOG2_E = 1.4426950408889634

_NN = (((1,), (0,)), ((), ()))
_NT = (((1,), (1,)), ((), ()))
_TN = (((0,), (0,)), ((), ()))


def _dg(a, b, dims=_NN):
    return lax.dot_general(a, b, dims, preferred_element_type=F32)


def _split2(x):
    hi = x.astype(BF16)
    lo = (x - hi.astype(F32)).astype(BF16)
    return hi, lo


def _split3(x):
    hi = x.astype(BF16)
    r1 = x - hi.astype(F32)
    mid = r1.astype(BF16)
    lo = (r1 - mid.astype(F32)).astype(BF16)
    return hi, mid, lo


def _mm1(a, b, dims=_NN):
    return _dg(a.astype(BF16), b.astype(BF16), dims)


def _mm3(a, b, dims=_NN):
    ah, al = _split2(a)
    bh, bl = _split2(b)
    return _dg(ah, bh, dims) + (_dg(ah, bl, dims) + _dg(al, bh, dims))


def _mm_exact_lhs(e, x, dims=_NN):
    h, m, l = _split3(x)
    return _dg(e, h, dims) + (_dg(e, m, dims) + _dg(e, l, dims))


def _mm2_exact_rhs(x, e, dims=_NN):
    h, l = _split2(x)
    return _dg(h, e, dims) + _dg(l, e, dims)


def _softplus(z):
    return jnp.maximum(z, 0.0) + jnp.log(1.0 + jnp.exp(-jnp.abs(z)))


def _sigmoid(z):
    return 1.0 / (1.0 + jnp.exp(-z))


def _silu(z):
    return z * _sigmoid(z)


def _params(*sem):
    return pltpu.CompilerParams(dimension_semantics=sem, vmem_limit_bytes=VMEM_LIMIT_BYTES)


def _mod_kernel(c_ref, w_ref, b_ref, o_ref):
    o_ref[...] = _mm1(_silu(c_ref[...]), w_ref[...]) + b_ref[...]


def _modulation(c, w, b, tn, layer=None):
    rows, d = c.shape
    n = w.shape[-1]
    if layer is None:
        w_spec = pl.BlockSpec((d, tn), lambda j: (0, j))
    else:
        w_spec = pl.BlockSpec((None, d, tn), lambda j: (layer, 0, j))
    return pl.pallas_call(
        _mod_kernel,
        grid=(n // tn,),
        in_specs=[pl.BlockSpec((rows, d), lambda j: (0, 0)), w_spec,
                  pl.BlockSpec((1, tn), lambda j: (0, j))],
        out_specs=pl.BlockSpec((rows, tn), lambda j: (0, j)),
        out_shape=jax.ShapeDtypeStruct((rows, n), F32),
        compiler_params=_params("parallel"),
        name="modulation",
    )(c, w, b.reshape(1, n))


def _win_layout_kernel(w_ref, o_ref):
    qkv_end = RWKV_PROJ + 2 * GLA_KEY_WIDTH + GLA_WIDTH
    gate_end = qkv_end + GLA_GATE_RANK
    o_ref[:qkv_end] = w_ref[:qkv_end].astype(BF16)
    o_ref[qkv_end:qkv_end + GLA_WIDTH] = w_ref[gate_end:gate_end + GLA_WIDTH].astype(BF16)
    o_ref[qkv_end + GLA_WIDTH:RWKV_PROJ + GLA_PROJ] = w_ref[qkv_end:gate_end].astype(BF16)
    o_ref[RWKV_PROJ + GLA_PROJ:] = jnp.zeros((IN_PROJ_PAD - RWKV_PROJ - GLA_PROJ, o_ref.shape[1]), BF16)


def _win_layout(wt, cols):
    n, d = wt.shape
    return pl.pallas_call(
        _win_layout_kernel,
        grid=(d // cols,),
        in_specs=[pl.BlockSpec((n, cols), lambda i: (0, i))],
        out_specs=pl.BlockSpec((IN_PROJ_PAD, cols), lambda i: (0, i)),
        out_shape=jax.ShapeDtypeStruct((IN_PROJ_PAD, d), BF16),
        compiler_params=_params("parallel"),
        name="w_in_layout",
    )(wt)


def _rms_mod(x, g, sc, sh):
    ms = jnp.mean(x * x, axis=-1, keepdims=True)
    return (x * lax.rsqrt(ms + RMS_EPS) * g) * (1.0 + sc) + sh


def _token_tile(bsz, seq, rows):
    if seq >= rows:
        assert seq % rows == 0
        return 1, rows
    nb = min(bsz, rows // seq)
    assert bsz % nb == 0
    return nb, seq


def _inproj_kernel(x_ref, sc_ref, sh_ref, g_ref, w_ref, *rest, n_step, n_cast):
    cast_in, (o_ref, last_ref), cast_out = rest[:n_cast], rest[n_cast:n_cast + 2], rest[n_cast + 2:]
    nb, rb, d = x_ref.shape
    h = _rms_mod(x_ref[...], g_ref[...], sc_ref[...], sh_ref[...])
    hb = h.reshape(nb * rb, d).astype(BF16)
    for j in range(IN_PROJ_PAD // n_step):
        cols = slice(j * n_step, (j + 1) * n_step)
        o_ref[:, :, cols] = _dg(hb, w_ref[cols, :], _NT).reshape(nb, rb, n_step)
    last_ref[...] = o_ref[:, rb - 1:rb, :RWKV_PROJ]
    for src, dst in zip(cast_in, cast_out):
        dst[...] = src[...].astype(BF16)


def _in_proj(x, sc, sh, g, w, rows, casts=()):
    bsz, seq, d = x.shape
    nb, rb = _token_tile(bsz, seq, rows)
    n_seq_tiles = seq // rb
    n_steps = (bsz // nb) * n_seq_tiles
    vec = pl.BlockSpec((nb, 1, d), lambda b, i: (b, 0, 0))
    slab = lambda a: pl.BlockSpec((a.shape[0] // n_steps, a.shape[1]), lambda b, i: (b * n_seq_tiles + i, 0))
    assert all(a.shape[0] % (16 * n_steps) == 0 for a in casts)
    return pl.pallas_call(
        functools.partial(_inproj_kernel, n_step=2 * LANES, n_cast=len(casts)),
        grid=(bsz // nb, n_seq_tiles),
        in_specs=[pl.BlockSpec((nb, rb, d), lambda b, i: (b, i, 0)), vec, vec,
                  pl.BlockSpec((1, d), lambda b, i: (0, 0)),
                  pl.BlockSpec((IN_PROJ_PAD, d), lambda b, i: (0, 0))] + [slab(a) for a in casts],
        out_specs=[pl.BlockSpec((nb, rb, IN_PROJ_PAD), lambda b, i: (b, i, 0)),
                   pl.BlockSpec((nb, 1, RWKV_PROJ), lambda b, i: (b * n_seq_tiles + i, 0, 0))]
        + [slab(a) for a in casts],
        out_shape=[jax.ShapeDtypeStruct((bsz, seq, IN_PROJ_PAD), F32),
                   jax.ShapeDtypeStruct((bsz * n_seq_tiles, 1, RWKV_PROJ), F32)]
        + [jax.ShapeDtypeStruct(a.shape, BF16) for a in casts],
        compiler_params=_params("parallel", "parallel"),
        name="norm1_in_proj",
    )(x, sc, sh, g, w, *casts)


def _rwkv_prep_kernel(p_ref, prev_ref, mu_ref, w0_ref, a0_ref, kk_ref, ka_ref, rk_ref,
                      wd_ref, wa_ref, wg_ref, seg_ref, tri_ref,
                      ab_ref, rkv_ref, bg_ref, gl_ref):
    nb, rb, wp = p_ref.shape
    tr = nb * rb
    p = p_ref[...].reshape(tr, wp)
    row = lax.broadcasted_iota(jnp.int32, (nb, rb, wp), 1)
    xx = jnp.where(row == 0, prev_ref[...], pltpu.roll(p, 1, 0).reshape(nb, rb, wp)).reshape(tr, wp)
    ps = p + (xx - p) * mu_ref[...]
    w = RWKV_WIDTH
    r, k, v, lora = ps[:, :w], ps[:, w:2 * w], ps[:, 2 * w:3 * w], ps[:, 3 * w:]
    logw = -_softplus(-(w0_ref[...] + _mm1(jnp.tanh(lora), wd_ref[...]))) - 0.5
    lw = jnp.exp(logw) * (-LOG2_E)
    a = _sigmoid(a0_ref[...] + _mm1(lora, wa_ref[...]))
    g = _mm1(_sigmoid(lora), wg_ref[...])
    seg = seg_ref[...]
    kk = k * kk_ref[...]
    kk = kk / jnp.maximum(jnp.sqrt(_mm2_exact_rhs(kk * kk, seg)), 1e-12)
    k2 = k * (1.0 + (a - 1.0) * ka_ref[...])
    cum = _mm_exact_lhs(tri_ref[...], lw)
    lasts = [jnp.exp2(cum[c * CHUNK + CHUNK - 1:(c + 1) * CHUNK, :]) for c in range(tr // CHUNK)]
    for c, l in enumerate(lasts):
        gl_ref[c:c + 1, :] = l

    def to_chunk_end(val):
        return jnp.concatenate([val[c * CHUNK:(c + 1) * CHUNK] * l for c, l in enumerate(lasts)], axis=0)

    gamma = jnp.exp2(cum)
    ginv = 1.0 / gamma
    bt = (kk * a) * ginv
    kt = k2 * ginv

    def put(ref, slot, val):
        ref[:, :, slot * w:(slot + 1) * w] = val.reshape(nb, rb, w).astype(ref.dtype)

    put(ab_ref, 0, -kk * jnp.exp2(cum - lw))
    put(ab_ref, 1, bt)
    put(rkv_ref, 0, r * gamma)
    put(rkv_ref, 1, kt)
    put(rkv_ref, 2, to_chunk_end(bt))
    put(rkv_ref, 3, to_chunk_end(kt))
    put(rkv_ref, 4, v)
    put(bg_ref, 0, _mm2_exact_rhs(r * k2 * rk_ref[...], seg) * v)
    put(bg_ref, 1, g)


def _rwkv_prep(proj, prev, lp, consts, rows):
    bsz, seq, _ = proj.shape
    w = RWKV_WIDTH
    nb, rb = _token_tile(bsz, seq, rows)
    n_seq_tiles = seq // rb
    flat = lambda b, i: (b * n_seq_tiles + i, 0)
    row = lambda n: pl.BlockSpec((1, n), lambda b, i: (0, 0))
    full = lambda a: pl.BlockSpec(a.shape, lambda b, i: (0,) * a.ndim)
    packed = [(2, F32), (5, BF16), (2, F32)]
    return pl.pallas_call(
        _rwkv_prep_kernel,
        grid=(bsz // nb, n_seq_tiles),
        in_specs=[pl.BlockSpec((nb, rb, RWKV_PROJ), lambda b, i: (b, i, 0)),
                  pl.BlockSpec((nb, 1, RWKV_PROJ), lambda b, i: flat(b, i) + (0,)),
                  row(RWKV_PROJ), row(w), row(w), row(w), row(w), row(w),
                  full(lp["wd"]), full(lp["wa"]), full(lp["wg"]), full(consts["seg"]), full(consts["tri"])],
        out_specs=[pl.BlockSpec((nb, rb, k * w), lambda b, i: (b, i, 0)) for k, _ in packed]
        + [pl.BlockSpec((nb * rb // CHUNK, w), flat)],
        out_shape=[jax.ShapeDtypeStruct((bsz, seq, k * w), dt) for k, dt in packed]
        + [jax.ShapeDtypeStruct((bsz * seq // CHUNK, w), F32)],
        compiler_params=_params("parallel", "parallel"),
        name="rwkv_prep",
    )(proj, prev, lp["mu"], lp["w0"], lp["a0"], lp["k_k"], lp["k_a"], lp["r_k"],
      lp["wd"], lp["wa"], lp["wg"], consts["seg"], consts["tri"])


def _head_masks(width):
    lane = lax.broadcasted_iota(jnp.int32, (CHUNK, width), 1)
    row = lax.broadcasted_iota(jnp.int32, (CHUNK, width), 0)
    return lane >> HEAD_SHIFT, row, lane & (HEAD_LANES - 1)


def _block_diag(x, head):
    z = jnp.zeros_like(x)
    return jnp.concatenate([jnp.where(head == j, x, z) for j in range(x.shape[1] // HEAD_LANES)], axis=0)


def _rwkv_chunk_kernel(ab_ref, rkv_ref, gl_ref, s0_ref, y_ref, s_ref, *, bb, nck):
    @pl.when(pl.program_id(1) == 0)
    def _():
        s_ref[...] = s0_ref[...]

    tw = RWKV_TILE_LANES
    head, row, col = _head_masks(tw)
    strict = col < row
    incl = col <= row
    same8 = (col >> 3) == (row >> 3)
    lane2 = lax.broadcasted_iota(jnp.int32, (tw, tw), 1)
    row2 = lax.broadcasted_iota(jnp.int32, (tw, tw), 0)
    same_head = (lane2 >> HEAD_SHIFT) == (row2 >> HEAD_SHIFT)
    bd = functools.partial(_block_diag, head=head)
    c = CHUNK

    def pmm(p, q):
        return _dg(p.astype(BF16), bd(q.astype(BF16)))

    items = [(b, ck, pr) for b in range(bb) for ck in range(nck) for pr in range(RWKV_TILES)]
    n = range(len(items))
    rows = lambda ck: slice(ck * c, (ck + 1) * c)
    lanes = lambda pr: slice(pr * tw, (pr + 1) * tw)
    w = RWKV_WIDTH

    def ld(ref, slot):
        return [ref[b, rows(ck), slot * w + pr * tw:slot * w + (pr + 1) * tw] for b, ck, pr in items]

    cat0 = lambda *xs: jnp.concatenate(xs, axis=0)
    cat1 = lambda *xs: jnp.concatenate(xs, axis=1)
    at, bt = ld(ab_ref, 0), ld(ab_ref, 1)
    rt, kt, be, ke, v = (ld(rkv_ref, slot) for slot in range(5))
    ats, bts = [_split2(a) for a in at], [_split2(b) for b in bt]
    atb = [hi for hi, _ in ats]
    zero = jnp.zeros((c, tw), F32)
    gk = [_dg(cat0(atb[i], rt[i]), bd(kt[i]), _NT) for i in n]
    aak = [jnp.where(strict, gk[i][:c], zero) for i in n]
    ark = [jnp.where(incl, gk[i][c:], zero) for i in n]
    gb = [_dg(cat0(ats[i][0], ats[i][1], rt[i]), bd(bts[i][0]), _NT) for i in n]
    arb = [jnp.where(incl, gb[i][2 * c:], zero) for i in n]
    aab = [jnp.where(strict, gb[i][:c] + (gb[i][c:2 * c] + _dg(atb[i], bd(bts[i][1]), _NT)), zero) for i in n]
    z = [pmm(aak[i], v[i]) for i in n]
    a8 = [jnp.where(same8, aab[i], zero) for i in n]
    p2 = [pmm(a8[i], a8[i]) for i in n]
    p4 = [pmm(p2[i], p2[i]) for i in n]
    nn = [a8[i] + p2[i] + pmm(p2[i], a8[i]) for i in n]
    nn = [nn[i] + p4[i] + pmm(p4[i], nn[i]) for i in n]
    for lvl in (3, 4, 5):
        joins = ((col >> (lvl + 1)) == (row >> (lvl + 1))) & ((col >> lvl) != (row >> lvl))
        e = [jnp.where(joins, aab[i], zero) for i in n]
        te = [e[i] + pmm(nn[i], e[i]) for i in n]
        nn = [nn[i] + te[i] + pmm(te[i], nn[i]) for i in n]
    wu = [cat1(at[i], z[i]) + _dg(nn[i].astype(BF16), cat1(bd(atb[i]), bd(z[i].astype(BF16)))) for i in n]
    abk = [cat1(arb[i], ark[i]).astype(BF16) for i in n]
    s = {(b, pr): s_ref[b, pr] for b in range(bb) for pr in range(RWKV_TILES)}
    for ck in range(nck):
        cur = [i for i in n if items[i][1] == ck]
        key = lambda i: (items[i][0], items[i][2])
        x = {i: _dg(cat0(wu[i][:, :tw].astype(BF16), rt[i]), s[key(i)].astype(BF16), _NT) for i in cur}
        ub = {i: (x[i][:c] + wu[i][:, tw:]).astype(BF16) for i in cur}
        upd = {i: _dg(cat0(ub[i], v[i]), cat0(be[i], ke[i]), _TN) for i in cur}
        for i in cur:
            b, _, pr = items[i]
            y_ref[b, rows(ck), lanes(pr)] = x[i][c:] + _dg(abk[i], cat0(bd(ub[i]), bd(v[i])))
            s[b, pr] = (s[b, pr] * gl_ref[b, ck, :, lanes(pr)]
                        + jnp.where(same_head, upd[i], jnp.zeros_like(upd[i])))
    for (b, pr), val in s.items():
        s_ref[b, pr] = val


def _rwkv_chunk(ab, rkv, gl, s0, bb, nck):
    bsz, seq, _ = ab.shape
    w = RWKV_WIDTH
    tok = lambda a: pl.BlockSpec((bb, nck * CHUNK, a.shape[-1]), lambda i, c: (i, c, 0))
    st = pl.BlockSpec((bb, RWKV_TILES, RWKV_TILE_LANES, RWKV_TILE_LANES), lambda i, c: (i, 0, 0, 0))
    return pl.pallas_call(
        functools.partial(_rwkv_chunk_kernel, bb=bb, nck=nck),
        grid=(bsz // bb, seq // (nck * CHUNK)),
        in_specs=[tok(ab), tok(rkv), pl.BlockSpec((bb, nck, 1, w), lambda i, c: (i, c, 0, 0)), st],
        out_specs=[pl.BlockSpec((bb, nck * CHUNK, w), lambda i, c: (i, c, 0)), st],
        out_shape=[jax.ShapeDtypeStruct((bsz, seq, w), F32),
                   jax.ShapeDtypeStruct((bsz, RWKV_TILES, RWKV_TILE_LANES, RWKV_TILE_LANES), F32)],
        compiler_params=_params("parallel", "arbitrary"),
        name="rwkv_chunk",
    )(ab, rkv, gl, s0)


def _gla_kernel(p_ref, wgate_ref, bgate_ref, tri_ref, ng_ref, s0_ref, y_ref, s_ref, *, bb, nck):
    @pl.when(pl.program_id(1) == 0)
    def _():
        s_ref[...] = s0_ref[...]

    head, row, col = _head_masks(LANES)
    incl = col <= row
    bd = functools.partial(_block_diag, head=head)
    lane2 = lax.broadcasted_iota(jnp.int32, (2 * GLA_DV, LANES), 1)
    row2 = lax.broadcasted_iota(jnp.int32, (2 * GLA_DV, LANES), 0)
    same_head = (lane2 < GLA_DK) == (row2 < GLA_DV)
    kw, gw = GLA_KEY_WIDTH, GLA_WIDTH
    zero = jnp.zeros((CHUNK, LANES), F32)
    cat0 = lambda xs: jnp.concatenate(xs, axis=0)
    cat1 = lambda xs: jnp.concatenate(xs, axis=1)
    rows = lambda c: slice(c * CHUNK, (c + 1) * CHUNK)
    lanes = lambda pr: slice(pr * LANES, (pr + 1) * LANES)
    bs, cs = range(bb), range(nck)
    x = [p_ref[b] for b in bs]
    gate = [_mm1(x[b][:, 2 * kw + 2 * gw:], wgate_ref[...]) + bgate_ref[...] for b in bs]
    la = [(jnp.minimum(g, 0.0) - jnp.log(1.0 + jnp.exp(-jnp.abs(g)))) / GLA_TAU for g in gate]
    cum = [_mm_exact_lhs(tri_ref[...], la[b]) for b in bs]
    q_dec = [(x[b][:, :kw] * (GLA_DK ** -0.5)) * jnp.exp(cum[b]) for b in bs]
    k_inv = [x[b][:, kw:2 * kw] * jnp.exp(-cum[b]) for b in bs]
    last = [[cum[b][c * CHUNK + CHUNK - 1:(c + 1) * CHUNK, :] for c in cs] for b in bs]
    k_end = [[x[b][rows(c), kw:2 * kw] * jnp.exp(last[b][c] - cum[b][rows(c)]) for c in cs] for b in bs]
    dec = [[jnp.exp(last[b][c]) for c in cs] for b in bs]
    items = [(b, c, pr) for b in bs for c in cs for pr in range(GLA_PAIRS)]
    vp = {(b, c, pr): x[b][rows(c), 2 * kw + pr * 2 * GLA_DV:2 * kw + (pr + 1) * 2 * GLA_DV]
          for b, c, pr in items}
    att = {(b, c, pr): jnp.where(incl, _mm1(q_dec[b][rows(c), lanes(pr)],
                                            bd(k_inv[b][rows(c), lanes(pr)]), _NT), zero)
           for b, c, pr in items}
    upd = {it: _mm1(vp[it], k_end[it[0]][it[1]][:, lanes(it[2])], _TN) for it in items}
    intra = {it: _mm1(att[it], cat0([cat1([vp[it][:, :GLA_DV], zero]), cat1([zero, vp[it][:, GLA_DV:]])]))
             for it in items}
    st = {}
    for b in bs:
        for pr in range(GLA_PAIRS):
            cur = s_ref[b, pr]
            for c in cs:
                st[b, c, pr] = cur
                cur = cur * dec[b][c][:, lanes(pr)] + jnp.where(same_head, upd[b, c, pr], jnp.zeros_like(cur))
            s_ref[b, pr] = cur
    o = {(b, c, pr): intra[b, c, pr] + _mm1(q_dec[b][rows(c), lanes(pr)], st[b, c, pr], _NT)
         for b, c, pr in items}
    for b in bs:
        ob = cat0([cat1([o[b, c, pr] for pr in range(GLA_PAIRS)]) for c in cs])
        heads = [ob[:, h * GLA_DV:(h + 1) * GLA_DV] for h in range(GLA_HEADS)]
        normed = [oh * lax.rsqrt(jnp.mean(oh * oh, axis=-1, keepdims=True) + RMS_EPS) for oh in heads]
        y_ref[b] = cat1(normed) * ng_ref[...] * _silu(x[b][:, 2 * kw + gw:2 * kw + 2 * gw])


def _gla(proj, lp, tri, s0, bb, tg):
    bsz, seq, _ = proj.shape
    full = lambda a: pl.BlockSpec(a.shape, lambda i, c: (0,) * a.ndim)
    st = pl.BlockSpec((bb, GLA_PAIRS, 2 * GLA_DV, LANES), lambda i, c: (i, 0, 0, 0))
    return pl.pallas_call(
        functools.partial(_gla_kernel, bb=bb, nck=tg // CHUNK),
        grid=(bsz // bb, seq // tg),
        in_specs=[pl.BlockSpec((bb, tg, GLA_PROJ_PAD), lambda i, c: (i, c, 1)),
                  full(lp["wgate"]), full(lp["bgate"]), full(tri), full(lp["gla_g"]), st],
        out_specs=[pl.BlockSpec((bb, tg, GLA_WIDTH), lambda i, c: (i, c, 0)), st],
        out_shape=[jax.ShapeDtypeStruct((bsz, seq, GLA_WIDTH), F32),
                   jax.ShapeDtypeStruct((bsz, GLA_PAIRS, 2 * GLA_DV, LANES), F32)],
        compiler_params=_params("parallel", "arbitrary"),
        name="gla_chunk",
    )(proj, lp["wgate"], lp["bgate"], tri, lp["gla_g"], s0)


def _route(lgt):
    tm = lgt.shape[1]
    n_pad = ROUTE_ROWS - N_EXPERTS
    rowg = lax.broadcasted_iota(jnp.int32, (n_pad, tm), 0)
    rowe = lax.broadcasted_iota(jnp.int32, (N_EXPERTS, tm), 0)
    rowg_f, rowe_f = rowg.astype(F32), rowe.astype(F32)

    def first_argmax(vals, mx, rows_f):
        return jnp.min(jnp.where(vals == mx, rows_f, float(ROUTE_ROWS)), axis=0, keepdims=True)

    is_group = rowg < N_GROUPS
    lg = jnp.where(is_group, lgt[N_EXPERTS:], -jnp.inf)
    gmax = jnp.max(lg, axis=0, keepdims=True)
    gi = first_argmax(lg, gmax, rowg_f)
    pg_top = 1.0 / jnp.sum(jnp.where(is_group, jnp.exp(lg - gmax), 0.0), axis=0, keepdims=True)
    in_group = (rowe >> GROUP_SHIFT).astype(F32) == gi
    le = jnp.where(in_group, lgt[:N_EXPERTS], -jnp.inf)
    m1 = jnp.max(le, axis=0, keepdims=True)
    i1 = first_argmax(le, m1, rowe_f)
    le2 = jnp.where(rowe_f == i1, -jnp.inf, le)
    m2 = jnp.max(le2, axis=0, keepdims=True)
    i2 = first_argmax(le2, m2, rowe_f)
    p2 = jnp.exp(m2 - m1)
    w1 = pg_top / (1.0 + p2)
    w2 = pg_top * p2 / (1.0 + p2)
    comb = jnp.where(rowe_f == i1, w1, 0.0) + jnp.where(rowe_f == i2, w2, 0.0)
    return jnp.concatenate([comb, jnp.where(rowg == 0, gi, 0.0)], axis=0)


def _outproj_kernel(yr_ref, bg_ref, yg_ref, x_ref, gt_ref, sc_ref, sh_ref,
                    lnw_ref, lnb_ref, seg_ref, wout_ref, g2_ref, wr_ref, br_ref,
                    x1_ref, h2_ref, route_ref):
    nb, rb, d_model = x_ref.shape
    tm = nb * rb
    flat = lambda ref: ref[...].reshape(tm, ref.shape[-1])
    seg = seg_ref[...]
    y = flat(yr_ref)
    inv_n = 1.0 / RWKV_HEAD
    d = y - _mm2_exact_rhs(y, seg) * inv_n
    var = _mm2_exact_rhs(d * d, seg) * inv_n
    yn = d * lax.rsqrt(var + LNX_EPS) * lnw_ref[...] + lnb_ref[...]
    bg = flat(bg_ref)
    yr = (yn + bg[:, :RWKV_WIDTH]) * bg[:, RWKV_WIDTH:]
    mix = jnp.concatenate([yr, flat(yg_ref)], axis=1)
    x1 = x_ref[...] + gt_ref[...] * _mm1(mix, wout_ref[...]).reshape(nb, rb, d_model)
    x1_ref[...] = x1
    h2 = _rms_mod(x1, g2_ref[...], sc_ref[...], sh_ref[...])
    h2_ref[...] = h2.astype(BF16)
    route_ref[...] = _route(_mm3(wr_ref[...], h2.reshape(tm, d_model), _NT) + br_ref[...])


def _out_proj(yr, bg, yg, x, gt, sc, sh, lp, consts, rows):
    bsz, seq, d = x.shape
    nb, rb = _token_tile(bsz, seq, rows)
    n_seq_tiles = seq // rb
    half = pl.BlockSpec((nb, rb, RWKV_WIDTH), lambda b, i: (b, i, 0))
    tokd = pl.BlockSpec((nb, rb, d), lambda b, i: (b, i, 0))
    vec = pl.BlockSpec((nb, 1, d), lambda b, i: (b, 0, 0))
    full = lambda a: pl.BlockSpec(a.shape, lambda b, i: (0,) * a.ndim)
    args = (lp["lnx_w"], lp["lnx_b"], consts["seg"], lp["w_out"], lp["norm2_g"], lp["w_router"], lp["b_router"])
    return pl.pallas_call(
        _outproj_kernel,
        grid=(bsz // nb, seq // rb),
        in_specs=[half, pl.BlockSpec((nb, rb, 2 * RWKV_WIDTH), lambda b, i: (b, i, 0)), half, tokd,
                  vec, vec, vec] + [full(a) for a in args],
        out_specs=[tokd, tokd, pl.BlockSpec((ROUTE_ROWS, nb * rb), lambda b, i: (0, b * n_seq_tiles + i))],
        out_shape=[jax.ShapeDtypeStruct((bsz, seq, d), F32),
                   jax.ShapeDtypeStruct((bsz, seq, d), BF16),
                   jax.ShapeDtypeStruct((ROUTE_ROWS, bsz * seq), F32)],
        compiler_params=_params("parallel", "parallel"),
        name="out_proj_router",
    )(yr, bg, yg, x, gt, sc, sh, *args)


def _moe_kernel(h_ref, route_ref, x1_ref, gt_ref, sc_ref, sh_ref, gf_ref, tri_ref, wg_ref, wu_ref, wd_ref,
                y_ref, ys_ref, pos_ref, meta_ref):
    g = pl.program_id(2)
    nb, rb, d = h_ref.shape
    tm = nb * rb
    sub, tail = MOE_SUB_ROWS, MOE_SUB_ROWS // 2
    cap = ys_ref.shape[0]
    n_lane_tiles = tm // LANES
    lane_tile = lambda k: slice(k * LANES, (k + 1) * LANES)

    @pl.when(g == 0)
    def _sort():
        row8 = lax.broadcasted_iota(jnp.int32, (8, LANES), 0).astype(F32)
        carry = jnp.zeros((8, 1), F32)
        members, ranks = [], []
        for k in range(n_lane_tiles):
            blk = jnp.where(row8 == route_ref[N_EXPERTS:N_EXPERTS + 1, lane_tile(k)], 1.0, 0.0)
            members.append(blk)
            ranks.append(_dg(blk.astype(BF16), tri_ref[...]) - blk + carry)
            carry = carry + jnp.sum(blk, axis=1, keepdims=True)
        rowc = lax.broadcasted_iota(jnp.int32, (8, 1), 0)
        lane = lax.broadcasted_iota(jnp.int32, (1, LANES), 1)
        first = jnp.zeros((1, 1), F32)
        off_col = jnp.zeros((8, 1), F32)
        meta = jnp.zeros((1, LANES), F32)
        for grp in range(N_GROUPS):
            count = jnp.sum(jnp.where(rowc == grp, carry, 0.0), axis=0, keepdims=True)
            blocks = jnp.ceil(count * (1.0 / tail))
            pairs = jnp.floor(blocks * 0.5)
            odd = blocks - 2.0 * pairs
            triple = odd * jnp.where(blocks >= 3.0, 1.0, 0.0)
            n_full = pairs - triple
            off_col = off_col + jnp.where(rowc == grp, first, 0.0)
            meta = (meta + jnp.where(lane == grp, first, 0.0) + jnp.where(lane == N_GROUPS + grp, n_full, 0.0)
                    + jnp.where(lane == 2 * N_GROUPS + grp, odd + 2.0 * triple, 0.0))
            first = first + blocks * tail
        meta = (meta + jnp.where(lane == 3 * N_GROUPS, first, 0.0)).astype(jnp.int32)
        for i in range(3 * N_GROUPS + 1):
            meta_ref[i] = meta[0, i]
        for k in range(n_lane_tiles):
            pos = jnp.sum(members[k] * (ranks[k] + off_col), axis=0, keepdims=True)
            pos_ref[:, lane_tile(k)] = pos.astype(jnp.int32)
        ys_ref[...] = jnp.zeros_like(ys_ref)

    def experts(base, n_rows):
        prow = lax.broadcasted_iota(jnp.int32, (n_rows, tm), 0) + base
        onehot = jnp.where(prow == pos_ref[...], 1.0, 0.0).astype(BF16)
        hs = _dg(onehot, h_ref[...].reshape(tm, d)).astype(BF16)
        comb3 = _split3(route_ref[:N_EXPERTS, :])
        cs = _dg(onehot, comb3[0], _NT) + (_dg(onehot, comb3[1], _NT) + _dg(onehot, comb3[2], _NT))
        lane = lax.broadcasted_iota(jnp.int32, cs.shape, 1)
        ys = jnp.zeros((n_rows, d), F32)
        for e in range(EXPERTS_PER_GROUP):
            hid = _silu(_dg(hs, wg_ref[e])) * _dg(hs, wu_ref[e])
            ce = jnp.sum(jnp.where(lane == g * EXPERTS_PER_GROUP + e, cs, 0.0), axis=-1, keepdims=True)
            ys = ys + ce * _dg(hid.astype(BF16), wd_ref[e])
        ys_ref[pl.ds(base, n_rows), :] = ys.astype(BF16)

    first_row = meta_ref[g]
    n_full = meta_ref[N_GROUPS + g]

    def full_sub_tile(j, carry_):
        experts(pl.multiple_of(first_row + j * sub, tail), sub)
        return carry_

    lax.fori_loop(0, n_full, full_sub_tile, 0)

    for last_blocks in (1, 3):
        @pl.when(meta_ref[2 * N_GROUPS + g] == last_blocks)
        def _last(last_blocks=last_blocks):
            experts(pl.multiple_of(first_row + n_full * sub, tail), last_blocks * tail)

    def unsort(n_sorted):
        ys_all = ys_ref[:n_sorted]
        prow = lax.broadcasted_iota(jnp.int32, (n_sorted, LANES), 0)
        seqs = max(LANES // rb, 1)
        for k in range(n_lane_tiles):
            onehot = jnp.where(prow == pos_ref[:, lane_tile(k)], 1.0, 0.0).astype(BF16)
            moe = _dg(onehot, ys_all, _TN).reshape(seqs, LANES // seqs, d)
            b0 = k * LANES // rb
            r0 = k * LANES - b0 * rb
            bs, rs = slice(b0, b0 + seqs), slice(r0, r0 + LANES // seqs)
            x2 = x1_ref[bs, rs] + gt_ref[bs] * moe
            y_ref[bs, rs] = _rms_mod(x2, gf_ref[...], sc_ref[bs], sh_ref[bs])

    usual = tm + (cap - tm) // 2
    last = g == N_GROUPS - 1
    pl.when(last & (meta_ref[3 * N_GROUPS] <= usual))(lambda: unsort(usual))
    pl.when(last & (meta_ref[3 * N_GROUPS] > usual))(lambda: unsort(cap))


def _moe(h2, route, x1, gt, sc, sh, gf, wg, wu, wd, tri, rows):
    bsz, seq, d = x1.shape
    nb, rb = _token_tile(bsz, seq, rows)
    tm = nb * rb
    assert tm % LANES == 0 and (rb % LANES == 0 or LANES % rb == 0)
    n_seq_tiles = seq // rb
    cap = tm + N_GROUPS * (MOE_SUB_ROWS // 2)
    tokd = pl.BlockSpec((nb, rb, d), lambda b, i, g: (b, i, 0))
    vec = pl.BlockSpec((nb, 1, d), lambda b, i, g: (b, 0, 0))
    group_w = lambda shape: pl.BlockSpec((EXPERTS_PER_GROUP,) + shape, lambda b, i, g: (g, 0, 0))
    return pl.pallas_call(
        _moe_kernel,
        grid=(bsz // nb, n_seq_tiles, N_GROUPS),
        in_specs=[tokd, pl.BlockSpec((ROUTE_ROWS, tm), lambda b, i, g: (0, b * n_seq_tiles + i)),
                  tokd, vec, vec, vec,
                  pl.BlockSpec((1, d), lambda b, i, g: (0, 0)),
                  pl.BlockSpec(tri.shape, lambda b, i, g: (0, 0)),
                  group_w((d, D_EXPERT)), group_w((d, D_EXPERT)), group_w((D_EXPERT, d))],
        out_specs=tokd,
        out_shape=jax.ShapeDtypeStruct((bsz, seq, d), F32),
        scratch_shapes=[pltpu.VMEM((cap, d), BF16), pltpu.VMEM((1, tm), jnp.int32),
                        pltpu.SMEM((3 * N_GROUPS + 1,), jnp.int32)],
        compiler_params=_params("parallel", "parallel", "arbitrary"),
        name="moe_final_norm",
    )(h2, route, x1, gt, sc, sh, gf, tri, wg, wu, wd)


def _block_ones(n, blk, lower):
    i = np.arange(n)
    m = (i[:, None] // blk) == (i[None, :] // blk)
    if lower:
        m = m & (i[None, :] <= i[:, None])
    return jnp.asarray(m, dtype=BF16)


def _consts(tr, tg):
    return dict(seg=_block_ones(RWKV_WIDTH, RWKV_HEAD, False), tri=_block_ones(tr, CHUNK, True),
                tri_gla=_block_ones(tg, CHUNK, True), tri_up=_block_ones(LANES, LANES, True).T)


def _heads_block_diag(s):
    n, c = s.shape[2], s.shape[-1]
    rows = [jnp.pad(s[:, :, j], ((0, 0), (0, 0), (0, 0), (j * c, (n - 1 - j) * c))) for j in range(n)]
    return jnp.concatenate(rows, axis=2)


def _diag_blocks(t, n):
    r, c = t.shape[-2] // n, t.shape[-1] // n
    return jnp.stack([t[:, :, j * r:(j + 1) * r, j * c:(j + 1) * c] for j in range(n)], axis=2)


def _pad_rows(w, first_row):
    out = jnp.zeros((LORA_PAD, w.shape[1]), F32)
    return lax.dynamic_update_slice(out, w, (first_row, 0)).astype(BF16)


def _layer_params(l, w_in, mu_shift, w0, w_decay_up, a0, w_a_up, w_g_up, k_k, k_a, r_k, lnx_w, lnx_b,
                  w_gla_gate_up, b_gla_gate, gla_norm_g, w_out, norm2_g,
                  w_router_group, b_router_group, w_router_expert, b_router_expert):
    w_in_p = _win_layout(w_in[l].T, WEIGHT_TILE_COLS)
    n_pad = ROUTE_ROWS - N_EXPERTS - N_GROUPS
    w_router = jnp.concatenate([w_router_expert[l].T, w_router_group[l].T, jnp.zeros((n_pad, D_MODEL), F32)])
    b_router = jnp.concatenate([b_router_expert[l], b_router_group[l],
                                jnp.zeros((n_pad,), F32)]).reshape(ROUTE_ROWS, 1)
    wgate = jnp.zeros((LANES, GLA_KEY_WIDTH), F32).at[:GLA_GATE_RANK].set(w_gla_gate_up[l]).astype(BF16)
    r1 = lambda a: a.reshape(1, -1)
    return dict(
        w_in=w_in_p, mu=r1(mu_shift[l]), w0=r1(w0[l]), a0=r1(a0[l]), k_k=r1(k_k[l]), k_a=r1(k_a[l]),
        r_k=r1(r_k[l]), wd=_pad_rows(w_decay_up[l], 0), wa=_pad_rows(w_a_up[l], DECAY_LORA),
        wg=_pad_rows(w_g_up[l], DECAY_LORA + AAA_LORA), lnx_w=r1(lnx_w[l]), lnx_b=r1(lnx_b[l]),
        wgate=wgate, bgate=r1(b_gla_gate[l]), gla_g=r1(gla_norm_g[l]),
        w_out=w_out[l].astype(BF16), norm2_g=r1(norm2_g[l]), w_router=w_router, b_router=b_router)


def _run_layer(x, mod, shift0, wkv0, gla0, lp, experts, final):
    bsz, seq, d = x.shape
    assert seq % CHUNK == 0
    bb = 2 if seq >= RWKV_STEP_ROWS else min(bsz, RECURRENT_MAX_SEQS)
    assert bsz % bb == 0
    tg = min(seq, GLA_STEP_ROWS)
    consts = _consts(TOKEN_TILE_ROWS, tg)
    m = lambda j: mod[:, j:j + 1, :]
    sh1, sc1, gt1, sh2, sc2, gt2 = (m(j) for j in range(6))
    casts = tuple(e.reshape(-1, e.shape[-1]) for e in experts if e.dtype != BF16)
    proj, tails, *cast = _in_proj(x, sc1, sh1, lp["norm1_g"], lp["w_in"], TOKEN_TILE_ROWS, casts)
    if cast:
        experts = tuple(c.reshape(e.shape) for c, e in zip(cast, experts))
    tails = tails.reshape(bsz, -1, RWKV_PROJ)
    new_shift = tails[:, -1]
    prev = jnp.concatenate([shift0[:, None, :], tails[:, :-1]], axis=1).reshape(-1, 1, RWKV_PROJ)
    ab, rkv, bg, gl = _rwkv_prep(proj, prev, lp, consts, TOKEN_TILE_ROWS)
    gl = gl.reshape(bsz, seq // CHUNK, RWKV_WIDTH)
    s0 = _heads_block_diag(wkv0.reshape(bsz, RWKV_TILES, RWKV_TILE_HEADS, RWKV_HEAD, RWKV_HEAD))
    yr, s_bd = _rwkv_chunk(ab, rkv, gl[:, :, None, :], s0, bb, min(seq, RWKV_STEP_ROWS) // CHUNK)
    new_wkv = _diag_blocks(s_bd, RWKV_TILE_HEADS).reshape(bsz, RWKV_HEADS, RWKV_HEAD, RWKV_HEAD)
    t0 = _heads_block_diag(jnp.swapaxes(gla0, -1, -2).reshape(bsz, GLA_PAIRS, 2, GLA_DV, GLA_DK))
    yg, t_bd = _gla(proj, lp, consts["tri_gla"], t0, bb, tg)
    new_gla = jnp.swapaxes(_diag_blocks(t_bd, 2).reshape(bsz, GLA_HEADS, GLA_DV, GLA_DK), -1, -2)
    x1, h2, route = _out_proj(yr, bg, yg, x, gt1, sc2, sh2, lp, consts, TOKEN_TILE_ROWS)
    out = _moe(h2, route, x1, gt2, *final, *experts, consts["tri_up"], MOE_TILE_ROWS)
    return experts, out, new_shift, new_wkv, new_gla


def kernel(x_prompt, x_sample, c_prompt, c_sample, state_rwkv_shift, state_rwkv_wkv, state_gla_kv, w_ada, b_ada, norm1_g, norm2_g, w_in, mu_shift, w0, w_decay_up, a0, w_a_up, w_g_up, k_k, k_a, r_k, lnx_w, lnx_b, w_gla_gate_up, b_gla_gate, gla_norm_g, w_out, w_router_group, b_router_group, w_router_expert, b_router_expert, w_expert_gate, w_expert_up, w_expert_down, w_ada_final, b_ada_final, normf_g):
    assert w_ada.shape[0] == 1, "the final norm is fused into the single layer's MoE kernel"
    bp, bs = x_prompt.shape[0], x_sample.shape[0]
    d = D_MODEL
    n_rows = -(-(bp + bs) // 8) * 8
    c_all = jnp.zeros((n_rows, d), F32).at[:bp].set(c_prompt).at[bp:bp + bs].set(c_sample)
    modf = _modulation(c_all, w_ada_final, b_ada_final, MODULATION_TILE_COLS).reshape(n_rows, 2, d)
    mod = _modulation(c_all, w_ada, b_ada[0], MODULATION_TILE_COLS, layer=0).reshape(n_rows, 6, d)
    lp = _layer_params(0, w_in, mu_shift, w0, w_decay_up, a0, w_a_up, w_g_up, k_k, k_a, r_k, lnx_w,
                       lnx_b, w_gla_gate_up, b_gla_gate, gla_norm_g, w_out, norm2_g,
                       w_router_group, b_router_group, w_router_expert, b_router_expert)
    lp["norm1_g"] = norm1_g[0].reshape(1, d)
    experts = (w_expert_gate[0], w_expert_up[0], w_expert_down[0])
    groups = [
        (x_prompt, 0, bp, jnp.zeros((bp, RWKV_PROJ), F32),
         jnp.zeros((bp, RWKV_HEADS, RWKV_HEAD, RWKV_HEAD), F32), jnp.zeros((bp, GLA_HEADS, GLA_DK, GLA_DV), F32)),
        (x_sample, bp, bp + bs, state_rwkv_shift[0], state_rwkv_wkv[0], state_gla_kv[0]),
    ]
    ys, states = [], []
    for x, lo, hi, shift0, wkv0, gla0 in groups:
        final = (modf[lo:hi, 1:2], modf[lo:hi, 0:1], normf_g.reshape(1, d))
        experts, y, *st = _run_layer(x, mod[lo:hi], shift0, wkv0, gla0, lp, experts, final)
        ys.append(y)
        states.extend(s[None] for s in st)
    return tuple(ys + states)
```

```python
import functools

import jax
import jax.numpy as jnp
import numpy as np
from jax import lax
from jax.experimental import pallas as pl
from jax.experimental.pallas import tpu as pltpu

F32 = jnp.float32
BF16 = jnp.bfloat16

LANES = 128
VMEM_LIMIT_BYTES = 56 * 1024 * 1024
TOKEN_TILE_ROWS = 512
MOE_TILE_ROWS = 1024
MOE_SUB_ROWS = 256
GLA_STEP_ROWS = 512
RECURRENT_MAX_SEQS = 8
MODULATION_TILE_COLS = 1024
WEIGHT_TILE_COLS = 256
RWKV_STEP_ROWS = 256

D_MODEL = 1024
CHUNK = 64
RWKV_WIDTH = 512
RWKV_HEAD = 64
RWKV_HEADS = RWKV_WIDTH // RWKV_HEAD
HEAD_LANES = RWKV_HEAD
HEAD_SHIFT = HEAD_LANES.bit_length() - 1
assert 1 << HEAD_SHIFT == HEAD_LANES
RWKV_TILE_LANES = 128
RWKV_TILE_HEADS = RWKV_TILE_LANES // RWKV_HEAD
RWKV_TILES = RWKV_WIDTH // RWKV_TILE_LANES
DECAY_LORA = 32
AAA_LORA = 32
GATE_LORA = 64
LORA_PAD = DECAY_LORA + AAA_LORA + GATE_LORA
RWKV_PROJ = 3 * RWKV_WIDTH + LORA_PAD
GLA_WIDTH = 512
GLA_HEADS = 4
GLA_PAIRS = GLA_HEADS // 2
GLA_DV = GLA_WIDTH // GLA_HEADS
GLA_DK = GLA_DV // 2
GLA_KEY_WIDTH = GLA_HEADS * GLA_DK
GLA_GATE_RANK = 16
GLA_TAU = 16.0
GLA_PROJ = 2 * GLA_KEY_WIDTH + 2 * GLA_WIDTH + GLA_GATE_RANK
GLA_PROJ_PAD = RWKV_PROJ
IN_PROJ_PAD = RWKV_PROJ + GLA_PROJ_PAD
N_GROUPS = 4
EXPERTS_PER_GROUP = 4
N_EXPERTS = N_GROUPS * EXPERTS_PER_GROUP
GROUP_SHIFT = EXPERTS_PER_GROUP.bit_length() - 1
assert 1 << GROUP_SHIFT == EXPERTS_PER_GROUP
ROUTE_ROWS = 24
D_EXPERT = 512
RMS_EPS = 1e-6
LNX_EPS = 64e-5
LOG2_E = 1.4426950408889634

_NN = (((1,), (0,)), ((), ()))
_NT = (((1,), (1,)), ((), ()))
_TN = (((0,), (0,)), ((), ()))


def _dg(a, b, dims=_NN):
    return lax.dot_general(a, b, dims, preferred_element_type=F32)


def _split2(x):
    hi = x.astype(BF16)
    lo = (x - hi.astype(F32)).astype(BF16)
    return hi, lo


def _split3(x):
    hi = x.astype(BF16)
    r1 = x - hi.astype(F32)
    mid = r1.astype(BF16)
    lo = (r1 - mid.astype(F32)).astype(BF16)
    return hi, mid, lo


def _mm1(a, b, dims=_NN):
    return _dg(a.astype(BF16), b.astype(BF16), dims)


def _mm3(a, b, dims=_NN):
    ah, al = _split2(a)
    bh, bl = _split2(b)
    return _dg(ah, bh, dims) + (_dg(ah, bl, dims) + _dg(al, bh, dims))


def _mm_exact_lhs(e, x, dims=_NN):
    h, m, l = _split3(x)
    return _dg(e, h, dims) + (_dg(e, m, dims) + _dg(e, l, dims))


def _mm2_exact_rhs(x, e, dims=_NN):
    h, l = _split2(x)
    return _dg(h, e, dims) + _dg(l, e, dims)


def _softplus(z):
    return jnp.maximum(z, 0.0) + jnp.log(1.0 + jnp.exp(-jnp.abs(z)))


def _sigmoid(z):
    return 1.0 / (1.0 + jnp.exp(-z))


def _silu(z):
    return z * _sigmoid(z)


def _params(*sem):
    return pltpu.CompilerParams(dimension_semantics=sem, vmem_limit_bytes=VMEM_LIMIT_BYTES)


def _mod_kernel(c_ref, w_ref, b_ref, o_ref):
    o_ref[...] = _mm1(_silu(c_ref[...]), w_ref[...]) + b_ref[...]


def _modulation(c, w, b, tn, layer=None):
    rows, d = c.shape
    n = w.shape[-1]
    if layer is None:
        w_spec = pl.BlockSpec((d, tn), lambda j: (0, j))
    else:
        w_spec = pl.BlockSpec((None, d, tn), lambda j: (layer, 0, j))
    return pl.pallas_call(
        _mod_kernel,
        grid=(n // tn,),
        in_specs=[pl.BlockSpec((rows, d), lambda j: (0, 0)), w_spec,
                  pl.BlockSpec((1, tn), lambda j: (0, j))],
        out_specs=pl.BlockSpec((rows, tn), lambda j: (0, j)),
        out_shape=jax.ShapeDtypeStruct((rows, n), F32),
        compiler_params=_params("parallel"),
        name="modulation",
    )(c, w, b.reshape(1, n))


def _win_layout_kernel(w_ref, o_ref):
    qkv_end = RWKV_PROJ + 2 * GLA_KEY_WIDTH + GLA_WIDTH
    gate_end = qkv_end + GLA_GATE_RANK
    o_ref[:qkv_end] = w_ref[:qkv_end].astype(BF16)
    o_ref[qkv_end:qkv_end + GLA_WIDTH] = w_ref[gate_end:gate_end + GLA_WIDTH].astype(BF16)
    o_ref[qkv_end + GLA_WIDTH:RWKV_PROJ + GLA_PROJ] = w_ref[qkv_end:gate_end].astype(BF16)
    o_ref[RWKV_PROJ + GLA_PROJ:] = jnp.zeros((IN_PROJ_PAD - RWKV_PROJ - GLA_PROJ, o_ref.shape[1]), BF16)


def _win_layout(wt, cols):
    n, d = wt.shape
    return pl.pallas_call(
        _win_layout_kernel,
        grid=(d // cols,),
        in_specs=[pl.BlockSpec((n, cols), lambda i: (0, i))],
        out_specs=pl.BlockSpec((IN_PROJ_PAD, cols), lambda i: (0, i)),
        out_shape=jax.ShapeDtypeStruct((IN_PROJ_PAD, d), BF16),
        compiler_params=_params("parallel"),
        name="w_in_layout",
    )(wt)


def _rms_mod(x, g, sc, sh):
    ms = jnp.mean(x * x, axis=-1, keepdims=True)
    return (x * lax.rsqrt(ms + RMS_EPS) * g) * (1.0 + sc) + sh


def _token_tile(bsz, seq, rows):
    if seq >= rows:
        assert seq % rows == 0
        return 1, rows
    nb = min(bsz, rows // seq)
    assert bsz % nb == 0
    return nb, seq


def _inproj_kernel(x_ref, sc_ref, sh_ref, g_ref, w_ref, o_ref, last_ref, *, n_step):
    nb, rb, d = x_ref.shape
    h = _rms_mod(x_ref[...], g_ref[...], sc_ref[...], sh_ref[...])
    hb = h.reshape(nb * rb, d).astype(BF16)
    for j in range(IN_PROJ_PAD // n_step):
        cols = slice(j * n_step, (j + 1) * n_step)
        o_ref[:, :, cols] = _dg(hb, w_ref[cols, :], _NT).reshape(nb, rb, n_step)
    last_ref[...] = o_ref[:, rb - 1:rb, :RWKV_PROJ]


def _in_proj(x, sc, sh, g, w, rows):
    bsz, seq, d = x.shape
    nb, rb = _token_tile(bsz, seq, rows)
    n_seq_tiles = seq // rb
    vec = pl.BlockSpec((nb, 1, d), lambda b, i: (b, 0, 0))
    return pl.pallas_call(
        functools.partial(_inproj_kernel, n_step=2 * LANES),
        grid=(bsz // nb, n_seq_tiles),
        in_specs=[pl.BlockSpec((nb, rb, d), lambda b, i: (b, i, 0)), vec, vec,
                  pl.BlockSpec((1, d), lambda b, i: (0, 0)),
                  pl.BlockSpec((IN_PROJ_PAD, d), lambda b, i: (0, 0))],
        out_specs=[pl.BlockSpec((nb, rb, IN_PROJ_PAD), lambda b, i: (b, i, 0)),
                   pl.BlockSpec((nb, 1, RWKV_PROJ), lambda b, i: (b * n_seq_tiles + i, 0, 0))],
        out_shape=[jax.ShapeDtypeStruct((bsz, seq, IN_PROJ_PAD), F32),
                   jax.ShapeDtypeStruct((bsz * n_seq_tiles, 1, RWKV_PROJ), F32)],
        compiler_params=_params("parallel", "parallel"),
        name="norm1_in_proj",
    )(x, sc, sh, g, w)


def _rwkv_prep_kernel(p_ref, prev_ref, mu_ref, w0_ref, a0_ref, kk_ref, ka_ref, rk_ref,
                      wd_ref, wa_ref, wg_ref, seg_ref, tri_ref,
                      ab_ref, rkv_ref, bg_ref, gl_ref):
    nb, rb, wp = p_ref.shape
    tr = nb * rb
    p = p_ref[...].reshape(tr, wp)
    row = lax.broadcasted_iota(jnp.int32, (nb, rb, wp), 1)
    xx = jnp.where(row == 0, prev_ref[...], pltpu.roll(p, 1, 0).reshape(nb, rb, wp)).reshape(tr, wp)
    ps = p + (xx - p) * mu_ref[...]
    w = RWKV_WIDTH
    r, k, v, lora = ps[:, :w], ps[:, w:2 * w], ps[:, 2 * w:3 * w], ps[:, 3 * w:]
    logw = -_softplus(-(w0_ref[...] + _mm1(jnp.tanh(lora), wd_ref[...]))) - 0.5
    lw = jnp.exp(logw) * (-LOG2_E)
    a = _sigmoid(a0_ref[...] + _mm1(lora, wa_ref[...]))
    g = _mm1(_sigmoid(lora), wg_ref[...])
    seg = seg_ref[...]
    kk = k * kk_ref[...]
    kk = kk / jnp.maximum(jnp.sqrt(_mm2_exact_rhs(kk * kk, seg)), 1e-12)
    k2 = k * (1.0 + (a - 1.0) * ka_ref[...])
    cum = _mm_exact_lhs(tri_ref[...], lw)
    lasts = [jnp.exp2(cum[c * CHUNK + CHUNK - 1:(c + 1) * CHUNK, :]) for c in range(tr // CHUNK)]
    for c, l in enumerate(lasts):
        gl_ref[c:c + 1, :] = l

    def to_chunk_end(val):
        return jnp.concatenate([val[c * CHUNK:(c + 1) * CHUNK] * l for c, l in enumerate(lasts)], axis=0)

    gamma = jnp.exp2(cum)
    ginv = 1.0 / gamma
    bt = (kk * a) * ginv
    kt = k2 * ginv

    def put(ref, slot, val):
        ref[:, :, slot * w:(slot + 1) * w] = val.reshape(nb, rb, w).astype(ref.dtype)

    put(ab_ref, 0, -kk * jnp.exp2(cum - lw))
    put(ab_ref, 1, bt)
    put(rkv_ref, 0, r * gamma)
    put(rkv_ref, 1, kt)
    put(rkv_ref, 2, to_chunk_end(bt))
    put(rkv_ref, 3, to_chunk_end(kt))
    put(rkv_ref, 4, v)
    put(bg_ref, 0, _mm2_exact_rhs(r * k2 * rk_ref[...], seg) * v)
    put(bg_ref, 1, g)


def _rwkv_prep(proj, prev, lp, consts, rows):
    bsz, seq, _ = proj.shape
    w = RWKV_WIDTH
    nb, rb = _token_tile(bsz, seq, rows)
    n_seq_tiles = seq // rb
    flat = lambda b, i: (b * n_seq_tiles + i, 0)
    row = lambda n: pl.BlockSpec((1, n), lambda b, i: (0, 0))
    full = lambda a: pl.BlockSpec(a.shape, lambda b, i: (0,) * a.ndim)
    packed = [(2, F32), (5, BF16), (2, F32)]
    return pl.pallas_call(
        _rwkv_prep_kernel,
        grid=(bsz // nb, n_seq_tiles),
        in_specs=[pl.BlockSpec((nb, rb, RWKV_PROJ), lambda b, i: (b, i, 0)),
                  pl.BlockSpec((nb, 1, RWKV_PROJ), lambda b, i: flat(b, i) + (0,)),
                  row(RWKV_PROJ), row(w), row(w), row(w), row(w), row(w),
                  full(lp["wd"]), full(lp["wa"]), full(lp["wg"]), full(consts["seg"]), full(consts["tri"])],
        out_specs=[pl.BlockSpec((nb, rb, k * w), lambda b, i: (b, i, 0)) for k, _ in packed]
        + [pl.BlockSpec((nb * rb // CHUNK, w), flat)],
        out_shape=[jax.ShapeDtypeStruct((bsz, seq, k * w), dt) for k, dt in packed]
        + [jax.ShapeDtypeStruct((bsz * seq // CHUNK, w), F32)],
        compiler_params=_params("parallel", "parallel"),
        name="rwkv_prep",
    )(proj, prev, lp["mu"], lp["w0"], lp["a0"], lp["k_k"], lp["k_a"], lp["r_k"],
      lp["wd"], lp["wa"], lp["wg"], consts["seg"], consts["tri"])


def _head_masks(width):
    lane = lax.broadcasted_iota(jnp.int32, (CHUNK, width), 1)
    row = lax.broadcasted_iota(jnp.int32, (CHUNK, width), 0)
    return lane >> HEAD_SHIFT, row, lane & (HEAD_LANES - 1)


def _block_diag(x, head):
    z = jnp.zeros_like(x)
    return jnp.concatenate([jnp.where(head == j, x, z) for j in range(x.shape[1] // HEAD_LANES)], axis=0)


def _rwkv_chunk_kernel(ab_ref, rkv_ref, gl_ref, s0_ref, *rest, bb, nck, n_cast):
    cast_in, (y_ref, s_ref), cast_out = rest[:n_cast], rest[n_cast:n_cast + 2], rest[n_cast + 2:]

    @pl.when(pl.program_id(1) == 0)
    def _():
        s_ref[...] = s0_ref[...]

    for src, dst in zip(cast_in, cast_out):
        dst[...] = src[...].astype(BF16)

    tw = RWKV_TILE_LANES
    head, row, col = _head_masks(tw)
    strict = col < row
    incl = col <= row
    same8 = (col >> 3) == (row >> 3)
    lane2 = lax.broadcasted_iota(jnp.int32, (tw, tw), 1)
    row2 = lax.broadcasted_iota(jnp.int32, (tw, tw), 0)
    same_head = (lane2 >> HEAD_SHIFT) == (row2 >> HEAD_SHIFT)
    bd = functools.partial(_block_diag, head=head)
    c = CHUNK

    def pmm(p, q):
        return _dg(p.astype(BF16), bd(q.astype(BF16)))

    items = [(b, ck, pr) for b in range(bb) for ck in range(nck) for pr in range(RWKV_TILES)]
    n = range(len(items))
    rows = lambda ck: slice(ck * c, (ck + 1) * c)
    lanes = lambda pr: slice(pr * tw, (pr + 1) * tw)
    w = RWKV_WIDTH

    def ld(ref, slot):
        return [ref[b, rows(ck), slot * w + pr * tw:slot * w + (pr + 1) * tw] for b, ck, pr in items]

    cat0 = lambda *xs: jnp.concatenate(xs, axis=0)
    cat1 = lambda *xs: jnp.concatenate(xs, axis=1)
    at, bt = ld(ab_ref, 0), ld(ab_ref, 1)
    rt, kt, be, ke, v = (ld(rkv_ref, slot) for slot in range(5))
    ats, bts = [_split2(a) for a in at], [_split2(b) for b in bt]
    atb = [hi for hi, _ in ats]
    zero = jnp.zeros((c, tw), F32)
    gk = [_dg(cat0(atb[i], rt[i]), bd(kt[i]), _NT) for i in n]
    aak = [jnp.where(strict, gk[i][:c], zero) for i in n]
    ark = [jnp.where(incl, gk[i][c:], zero) for i in n]
    gb = [_dg(cat0(ats[i][0], ats[i][1], rt[i]), bd(bts[i][0]), _NT) for i in n]
    arb = [jnp.where(incl, gb[i][2 * c:], zero) for i in n]
    aab = [jnp.where(strict, gb[i][:c] + (gb[i][c:2 * c] + _dg(atb[i], bd(bts[i][1]), _NT)), zero) for i in n]
    z = [pmm(aak[i], v[i]) for i in n]
    a8 = [jnp.where(same8, aab[i], zero) for i in n]
    p2 = [pmm(a8[i], a8[i]) for i in n]
    p4 = [pmm(p2[i], p2[i]) for i in n]
    nn = [a8[i] + p2[i] + pmm(p2[i], a8[i]) for i in n]
    nn = [nn[i] + p4[i] + pmm(p4[i], nn[i]) for i in n]
    for lvl in (3, 4, 5):
        joins = ((col >> (lvl + 1)) == (row >> (lvl + 1))) & ((col >> lvl) != (row >> lvl))
        e = [jnp.where(joins, aab[i], zero) for i in n]
        te = [e[i] + pmm(nn[i], e[i]) for i in n]
        nn = [nn[i] + te[i] + pmm(te[i], nn[i]) for i in n]
    wu = [cat1(at[i], z[i]) + _dg(nn[i].astype(BF16), cat1(bd(atb[i]), bd(z[i].astype(BF16)))) for i in n]
    abk = [cat1(arb[i], ark[i]).astype(BF16) for i in n]
    s = {(b, pr): s_ref[b, pr] for b in range(bb) for pr in range(RWKV_TILES)}
    for ck in range(nck):
        cur = [i for i in n if items[i][1] == ck]
        key = lambda i: (items[i][0], items[i][2])
        x = {i: _dg(cat0(wu[i][:, :tw].astype(BF16), rt[i]), s[key(i)].astype(BF16), _NT) for i in cur}
        ub = {i: (x[i][:c] + wu[i][:, tw:]).astype(BF16) for i in cur}
        upd = {i: _dg(cat0(ub[i], v[i]), cat0(be[i], ke[i]), _TN) for i in cur}
        for i in cur:
            b, _, pr = items[i]
            y_ref[b, rows(ck), lanes(pr)] = x[i][c:] + _dg(abk[i], cat0(bd(ub[i]), bd(v[i])))
            s[b, pr] = (s[b, pr] * gl_ref[b, ck, :, lanes(pr)]
                        + jnp.where(same_head, upd[i], jnp.zeros_like(upd[i])))
    for (b, pr), val in s.items():
        s_ref[b, pr] = val


def _rwkv_chunk(ab, rkv, gl, s0, bb, nck, casts=()):
    bsz, seq, _ = ab.shape
    w = RWKV_WIDTH
    n_seq_steps = seq // (nck * CHUNK)
    n_steps = (bsz // bb) * n_seq_steps
    tok = lambda a: pl.BlockSpec((bb, nck * CHUNK, a.shape[-1]), lambda i, c: (i, c, 0))
    st = pl.BlockSpec((bb, RWKV_TILES, RWKV_TILE_LANES, RWKV_TILE_LANES), lambda i, c: (i, 0, 0, 0))
    slab = lambda a: pl.BlockSpec((a.shape[0] // n_steps, a.shape[1]), lambda i, c: (i * n_seq_steps + c, 0))
    assert all(a.shape[0] % (16 * n_steps) == 0 for a in casts)
    return pl.pallas_call(
        functools.partial(_rwkv_chunk_kernel, bb=bb, nck=nck, n_cast=len(casts)),
        grid=(bsz // bb, n_seq_steps),
        in_specs=[tok(ab), tok(rkv), pl.BlockSpec((bb, nck, 1, w), lambda i, c: (i, c, 0, 0)), st]
        + [slab(a) for a in casts],
        out_specs=[pl.BlockSpec((bb, nck * CHUNK, w), lambda i, c: (i, c, 0)), st] + [slab(a) for a in casts],
        out_shape=[jax.ShapeDtypeStruct((bsz, seq, w), F32),
                   jax.ShapeDtypeStruct((bsz, RWKV_TILES, RWKV_TILE_LANES, RWKV_TILE_LANES), F32)]
        + [jax.ShapeDtypeStruct(a.shape, BF16) for a in casts],
        compiler_params=_params("parallel", "arbitrary"),
        name="rwkv_chunk",
    )(ab, rkv, gl, s0, *casts)


def _gla_kernel(p_ref, wgate_ref, bgate_ref, tri_ref, ng_ref, s0_ref, y_ref, s_ref, *, bb, nck):
    @pl.when(pl.program_id(1) == 0)
    def _():
        s_ref[...] = s0_ref[...]

    head, row, col = _head_masks(LANES)
    incl = col <= row
    bd = functools.partial(_block_diag, head=head)
    lane2 = lax.broadcasted_iota(jnp.int32, (2 * GLA_DV, LANES), 1)
    row2 = lax.broadcasted_iota(jnp.int32, (2 * GLA_DV, LANES), 0)
    same_head = (lane2 < GLA_DK) == (row2 < GLA_DV)
    kw, gw = GLA_KEY_WIDTH, GLA_WIDTH
    zero = jnp.zeros((CHUNK, LANES), F32)
    cat0 = lambda xs: jnp.concatenate(xs, axis=0)
    cat1 = lambda xs: jnp.concatenate(xs, axis=1)
    rows = lambda c: slice(c * CHUNK, (c + 1) * CHUNK)
    lanes = lambda pr: slice(pr * LANES, (pr + 1) * LANES)
    bs, cs = range(bb), range(nck)
    x = [p_ref[b] for b in bs]
    gate = [_mm1(x[b][:, 2 * kw + 2 * gw:], wgate_ref[...]) + bgate_ref[...] for b in bs]
    la = [(jnp.minimum(g, 0.0) - jnp.log(1.0 + jnp.exp(-jnp.abs(g)))) / GLA_TAU for g in gate]
    cum = [_mm_exact_lhs(tri_ref[...], la[b]) for b in bs]
    q_dec = [(x[b][:, :kw] * (GLA_DK ** -0.5)) * jnp.exp(cum[b]) for b in bs]
    k_inv = [x[b][:, kw:2 * kw] * jnp.exp(-cum[b]) for b in bs]
    last = [[cum[b][c * CHUNK + CHUNK - 1:(c + 1) * CHUNK, :] for c in cs] for b in bs]
    k_end = [[x[b][rows(c), kw:2 * kw] * jnp.exp(last[b][c] - cum[b][rows(c)]) for c in cs] for b in bs]
    dec = [[jnp.exp(last[b][c]) for c in cs] for b in bs]
    items = [(b, c, pr) for b in bs for c in cs for pr in range(GLA_PAIRS)]
    vp = {(b, c, pr): x[b][rows(c), 2 * kw + pr * 2 * GLA_DV:2 * kw + (pr + 1) * 2 * GLA_DV]
          for b, c, pr in items}
    att = {(b, c, pr): jnp.where(incl, _mm1(q_dec[b][rows(c), lanes(pr)],
                                            bd(k_inv[b][rows(c), lanes(pr)]), _NT), zero)
           for b, c, pr in items}
    upd = {it: _mm1(vp[it], k_end[it[0]][it[1]][:, lanes(it[2])], _TN) for it in items}
    intra = {it: _mm1(att[it], cat0([cat1([vp[it][:, :GLA_DV], zero]), cat1([zero, vp[it][:, GLA_DV:]])]))
             for it in items}
    st = {}
    for b in bs:
        for pr in range(GLA_PAIRS):
            cur = s_ref[b, pr]
            for c in cs:
                st[b, c, pr] = cur
                cur = cur * dec[b][c][:, lanes(pr)] + jnp.where(same_head, upd[b, c, pr], jnp.zeros_like(cur))
            s_ref[b, pr] = cur
    o = {(b, c, pr): intra[b, c, pr] + _mm1(q_dec[b][rows(c), lanes(pr)], st[b, c, pr], _NT)
         for b, c, pr in items}
    for b in bs:
        ob = cat0([cat1([o[b, c, pr] for pr in range(GLA_PAIRS)]) for c in cs])
        heads = [ob[:, h * GLA_DV:(h + 1) * GLA_DV] for h in range(GLA_HEADS)]
        normed = [oh * lax.rsqrt(jnp.mean(oh * oh, axis=-1, keepdims=True) + RMS_EPS) for oh in heads]
        y_ref[b] = cat1(normed) * ng_ref[...] * _silu(x[b][:, 2 * kw + gw:2 * kw + 2 * gw])


def _gla(proj, lp, tri, s0, bb, tg):
    bsz, seq, _ = proj.shape
    full = lambda a: pl.BlockSpec(a.shape, lambda i, c: (0,) * a.ndim)
    st = pl.BlockSpec((bb, GLA_PAIRS, 2 * GLA_DV, LANES), lambda i, c: (i, 0, 0, 0))
    return pl.pallas_call(
        functools.partial(_gla_kernel, bb=bb, nck=tg // CHUNK),
        grid=(bsz // bb, seq // tg),
        in_specs=[pl.BlockSpec((bb, tg, GLA_PROJ_PAD), lambda i, c: (i, c, 1)),
                  full(lp["wgate"]), full(lp["bgate"]), full(tri), full(lp["gla_g"]), st],
        out_specs=[pl.BlockSpec((bb, tg, GLA_WIDTH), lambda i, c: (i, c, 0)), st],
        out_shape=[jax.ShapeDtypeStruct((bsz, seq, GLA_WIDTH), F32),
                   jax.ShapeDtypeStruct((bsz, GLA_PAIRS, 2 * GLA_DV, LANES), F32)],
        compiler_params=_params("parallel", "arbitrary"),
        name="gla_chunk",
    )(proj, lp["wgate"], lp["bgate"], tri, lp["gla_g"], s0)


def _route(lgt):
    tm = lgt.shape[1]
    n_pad = ROUTE_ROWS - N_EXPERTS
    rowg = lax.broadcasted_iota(jnp.int32, (n_pad, tm), 0)
    rowe = lax.broadcasted_iota(jnp.int32, (N_EXPERTS, tm), 0)
    rowg_f, rowe_f = rowg.astype(F32), rowe.astype(F32)

    def first_argmax(vals, mx, rows_f):
        return jnp.min(jnp.where(vals == mx, rows_f, float(ROUTE_ROWS)), axis=0, keepdims=True)

    is_group = rowg < N_GROUPS
    lg = jnp.where(is_group, lgt[N_EXPERTS:], -jnp.inf)
    gmax = jnp.max(lg, axis=0, keepdims=True)
    gi = first_argmax(lg, gmax, rowg_f)
    pg_top = 1.0 / jnp.sum(jnp.where(is_group, jnp.exp(lg - gmax), 0.0), axis=0, keepdims=True)
    in_group = (rowe >> GROUP_SHIFT).astype(F32) == gi
    le = jnp.where(in_group, lgt[:N_EXPERTS], -jnp.inf)
    m1 = jnp.max(le, axis=0, keepdims=True)
    i1 = first_argmax(le, m1, rowe_f)
    le2 = jnp.where(rowe_f == i1, -jnp.inf, le)
    m2 = jnp.max(le2, axis=0, keepdims=True)
    i2 = first_argmax(le2, m2, rowe_f)
    p2 = jnp.exp(m2 - m1)
    w1 = pg_top / (1.0 + p2)
    w2 = pg_top * p2 / (1.0 + p2)
    comb = jnp.where(rowe_f == i1, w1, 0.0) + jnp.where(rowe_f == i2, w2, 0.0)
    return jnp.concatenate([comb, jnp.where(rowg == 0, gi, 0.0)], axis=0)


def _outproj_kernel(yr_ref, bg_ref, yg_ref, x_ref, gt_ref, sc_ref, sh_ref,
                    lnw_ref, lnb_ref, seg_ref, wout_ref, g2_ref, wr_ref, br_ref,
                    x1_ref, h2_ref, route_ref):
    nb, rb, d_model = x_ref.shape
    tm = nb * rb
    flat = lambda ref: ref[...].reshape(tm, ref.shape[-1])
    seg = seg_ref[...]
    y = flat(yr_ref)
    inv_n = 1.0 / RWKV_HEAD
    d = y - _mm2_exact_rhs(y, seg) * inv_n
    var = _mm2_exact_rhs(d * d, seg) * inv_n
    yn = d * lax.rsqrt(var + LNX_EPS) * lnw_ref[...] + lnb_ref[...]
    bg = flat(bg_ref)
    yr = (yn + bg[:, :RWKV_WIDTH]) * bg[:, RWKV_WIDTH:]
    mix = jnp.concatenate([yr, flat(yg_ref)], axis=1)
    x1 = x_ref[...] + gt_ref[...] * _mm1(mix, wout_ref[...]).reshape(nb, rb, d_model)
    x1_ref[...] = x1
    h2 = _rms_mod(x1, g2_ref[...], sc_ref[...], sh_ref[...])
    h2_ref[...] = h2.astype(BF16)
    route_ref[...] = _route(_mm3(wr_ref[...], h2.reshape(tm, d_model), _NT) + br_ref[...])


def _out_proj(yr, bg, yg, x, gt, sc, sh, lp, consts, rows):
    bsz, seq, d = x.shape
    nb, rb = _token_tile(bsz, seq, rows)
    n_seq_tiles = seq // rb
    half = pl.BlockSpec((nb, rb, RWKV_WIDTH), lambda b, i: (b, i, 0))
    tokd = pl.BlockSpec((nb, rb, d), lambda b, i: (b, i, 0))
    vec = pl.BlockSpec((nb, 1, d), lambda b, i: (b, 0, 0))
    full = lambda a: pl.BlockSpec(a.shape, lambda b, i: (0,) * a.ndim)
    args = (lp["lnx_w"], lp["lnx_b"], consts["seg"], lp["w_out"], lp["norm2_g"], lp["w_router"], lp["b_router"])
    return pl.pallas_call(
        _outproj_kernel,
        grid=(bsz // nb, seq // rb),
        in_specs=[half, pl.BlockSpec((nb, rb, 2 * RWKV_WIDTH), lambda b, i: (b, i, 0)), half, tokd,
                  vec, vec, vec] + [full(a) for a in args],
        out_specs=[tokd, tokd, pl.BlockSpec((ROUTE_ROWS, nb * rb), lambda b, i: (0, b * n_seq_tiles + i))],
        out_shape=[jax.ShapeDtypeStruct((bsz, seq, d), F32),
                   jax.ShapeDtypeStruct((bsz, seq, d), BF16),
                   jax.ShapeDtypeStruct((ROUTE_ROWS, bsz * seq), F32)],
        compiler_params=_params("parallel", "parallel"),
        name="out_proj_router",
    )(yr, bg, yg, x, gt, sc, sh, *args)


def _moe_kernel(h_ref, route_ref, x1_ref, gt_ref, sc_ref, sh_ref, gf_ref, tri_ref, wg_ref, wu_ref, wd_ref,
                y_ref, ys_ref, pos_ref, meta_ref):
    g = pl.program_id(2)
    nb, rb, d = h_ref.shape
    tm = nb * rb
    sub, tail = MOE_SUB_ROWS, MOE_SUB_ROWS // 2
    cap = ys_ref.shape[0]
    n_lane_tiles = tm // LANES
    lane_tile = lambda k: slice(k * LANES, (k + 1) * LANES)

    @pl.when(g == 0)
    def _sort():
        row8 = lax.broadcasted_iota(jnp.int32, (8, LANES), 0).astype(F32)
        carry = jnp.zeros((8, 1), F32)
        members, ranks = [], []
        for k in range(n_lane_tiles):
            blk = jnp.where(row8 == route_ref[N_EXPERTS:N_EXPERTS + 1, lane_tile(k)], 1.0, 0.0)
            members.append(blk)
            ranks.append(_dg(blk.astype(BF16), tri_ref[...]) - blk + carry)
            carry = carry + jnp.sum(blk, axis=1, keepdims=True)
        rowc = lax.broadcasted_iota(jnp.int32, (8, 1), 0)
        lane = lax.broadcasted_iota(jnp.int32, (1, LANES), 1)
        first = jnp.zeros((1, 1), F32)
        off_col = jnp.zeros((8, 1), F32)
        meta = jnp.zeros((1, LANES), F32)
        for grp in range(N_GROUPS):
            count = jnp.sum(jnp.where(rowc == grp, carry, 0.0), axis=0, keepdims=True)
            blocks = jnp.ceil(count * (1.0 / tail))
            pairs = jnp.floor(blocks * 0.5)
            odd = blocks - 2.0 * pairs
            triple = odd * jnp.where(blocks >= 3.0, 1.0, 0.0)
            n_full = pairs - triple
            off_col = off_col + jnp.where(rowc == grp, first, 0.0)
            meta = (meta + jnp.where(lane == grp, first, 0.0) + jnp.where(lane == N_GROUPS + grp, n_full, 0.0)
                    + jnp.where(lane == 2 * N_GROUPS + grp, odd + 2.0 * triple, 0.0))
            first = first + blocks * tail
        meta = (meta + jnp.where(lane == 3 * N_GROUPS, first, 0.0)).astype(jnp.int32)
        for i in range(3 * N_GROUPS + 1):
            meta_ref[i] = meta[0, i]
        for k in range(n_lane_tiles):
            pos = jnp.sum(members[k] * (ranks[k] + off_col), axis=0, keepdims=True)
            pos_ref[:, lane_tile(k)] = pos.astype(jnp.int32)
        ys_ref[...] = jnp.zeros_like(ys_ref)

    def experts(base, n_rows):
        prow = lax.broadcasted_iota(jnp.int32, (n_rows, tm), 0) + base
        onehot = jnp.where(prow == pos_ref[...], 1.0, 0.0).astype(BF16)
        hs = _dg(onehot, h_ref[...].reshape(tm, d)).astype(BF16)
        comb3 = _split3(route_ref[:N_EXPERTS, :])
        cs = _dg(onehot, comb3[0], _NT) + (_dg(onehot, comb3[1], _NT) + _dg(onehot, comb3[2], _NT))
        lane = lax.broadcasted_iota(jnp.int32, cs.shape, 1)
        ys = jnp.zeros((n_rows, d), F32)
        for e in range(EXPERTS_PER_GROUP):
            hid = _silu(_dg(hs, wg_ref[e])) * _dg(hs, wu_ref[e])
            ce = jnp.sum(jnp.where(lane == g * EXPERTS_PER_GROUP + e, cs, 0.0), axis=-1, keepdims=True)
            ys = ys + ce * _dg(hid.astype(BF16), wd_ref[e])
        ys_ref[pl.ds(base, n_rows), :] = ys.astype(BF16)

    first_row = meta_ref[g]
    n_full = meta_ref[N_GROUPS + g]

    def full_sub_tile(j, carry_):
        experts(pl.multiple_of(first_row + j * sub, tail), sub)
        return carry_

    lax.fori_loop(0, n_full, full_sub_tile, 0)

    for last_blocks in (1, 3):
        @pl.when(meta_ref[2 * N_GROUPS + g] == last_blocks)
        def _last(last_blocks=last_blocks):
            experts(pl.multiple_of(first_row + n_full * sub, tail), last_blocks * tail)

    def unsort(n_sorted):
        ys_all = ys_ref[:n_sorted]
        prow = lax.broadcasted_iota(jnp.int32, (n_sorted, LANES), 0)
        seqs = max(LANES // rb, 1)
        for k in range(n_lane_tiles):
            onehot = jnp.where(prow == pos_ref[:, lane_tile(k)], 1.0, 0.0).astype(BF16)
            moe = _dg(onehot, ys_all, _TN).reshape(seqs, LANES // seqs, d)
            b0 = k * LANES // rb
            r0 = k * LANES - b0 * rb
            bs, rs = slice(b0, b0 + seqs), slice(r0, r0 + LANES // seqs)
            x2 = x1_ref[bs, rs] + gt_ref[bs] * moe
            y_ref[bs, rs] = _rms_mod(x2, gf_ref[...], sc_ref[bs], sh_ref[bs])

    usual = tm + (cap - tm) // 2
    last = g == N_GROUPS - 1
    pl.when(last & (meta_ref[3 * N_GROUPS] <= usual))(lambda: unsort(usual))
    pl.when(last & (meta_ref[3 * N_GROUPS] > usual))(lambda: unsort(cap))


def _moe(h2, route, x1, gt, sc, sh, gf, wg, wu, wd, tri, rows):
    bsz, seq, d = x1.shape
    nb, rb = _token_tile(bsz, seq, rows)
    tm = nb * rb
    assert tm % LANES == 0 and (rb % LANES == 0 or LANES % rb == 0)
    n_seq_tiles = seq // rb
    cap = tm + N_GROUPS * (MOE_SUB_ROWS // 2)
    tokd = pl.BlockSpec((nb, rb, d), lambda b, i, g: (b, i, 0))
    vec = pl.BlockSpec((nb, 1, d), lambda b, i, g: (b, 0, 0))
    group_w = lambda shape: pl.BlockSpec((EXPERTS_PER_GROUP,) + shape, lambda b, i, g: (g, 0, 0))
    return pl.pallas_call(
        _moe_kernel,
        grid=(bsz // nb, n_seq_tiles, N_GROUPS),
        in_specs=[tokd, pl.BlockSpec((ROUTE_ROWS, tm), lambda b, i, g: (0, b * n_seq_tiles + i)),
                  tokd, vec, vec, vec,
                  pl.BlockSpec((1, d), lambda b, i, g: (0, 0)),
                  pl.BlockSpec(tri.shape, lambda b, i, g: (0, 0)),
                  group_w((d, D_EXPERT)), group_w((d, D_EXPERT)), group_w((D_EXPERT, d))],
        out_specs=tokd,
        out_shape=jax.ShapeDtypeStruct((bsz, seq, d), F32),
        scratch_shapes=[pltpu.VMEM((cap, d), BF16), pltpu.VMEM((1, tm), jnp.int32),
                        pltpu.SMEM((3 * N_GROUPS + 1,), jnp.int32)],
        compiler_params=_params("parallel", "parallel", "arbitrary"),
        name="moe_final_norm",
    )(h2, route, x1, gt, sc, sh, gf, tri, wg, wu, wd)


def _block_ones(n, blk, lower):
    i = np.arange(n)
    m = (i[:, None] // blk) == (i[None, :] // blk)
    if lower:
        m = m & (i[None, :] <= i[:, None])
    return jnp.asarray(m, dtype=BF16)


def _consts(tr, tg):
    return dict(seg=_block_ones(RWKV_WIDTH, RWKV_HEAD, False), tri=_block_ones(tr, CHUNK, True),
                tri_gla=_block_ones(tg, CHUNK, True), tri_up=_block_ones(LANES, LANES, True).T)


def _heads_block_diag(s):
    n, c = s.shape[2], s.shape[-1]
    rows = [jnp.pad(s[:, :, j], ((0, 0), (0, 0), (0, 0), (j * c, (n - 1 - j) * c))) for j in range(n)]
    return jnp.concatenate(rows, axis=2)


def _diag_blocks(t, n):
    r, c = t.shape[-2] // n, t.shape[-1] // n
    return jnp.stack([t[:, :, j * r:(j + 1) * r, j * c:(j + 1) * c] for j in range(n)], axis=2)


def _pad_rows(w, first_row):
    out = jnp.zeros((LORA_PAD, w.shape[1]), F32)
    return lax.dynamic_update_slice(out, w, (first_row, 0)).astype(BF16)


def _layer_params(l, w_in, mu_shift, w0, w_decay_up, a0, w_a_up, w_g_up, k_k, k_a, r_k, lnx_w, lnx_b,
                  w_gla_gate_up, b_gla_gate, gla_norm_g, w_out, norm2_g,
                  w_router_group, b_router_group, w_router_expert, b_router_expert):
    w_in_p = _win_layout(w_in[l].T, WEIGHT_TILE_COLS)
    n_pad = ROUTE_ROWS - N_EXPERTS - N_GROUPS
    w_router = jnp.concatenate([w_router_expert[l].T, w_router_group[l].T, jnp.zeros((n_pad, D_MODEL), F32)])
    b_router = jnp.concatenate([b_router_expert[l], b_router_group[l],
                                jnp.zeros((n_pad,), F32)]).reshape(ROUTE_ROWS, 1)
    wgate = jnp.zeros((LANES, GLA_KEY_WIDTH), F32).at[:GLA_GATE_RANK].set(w_gla_gate_up[l]).astype(BF16)
    r1 = lambda a: a.reshape(1, -1)
    return dict(
        w_in=w_in_p, mu=r1(mu_shift[l]), w0=r1(w0[l]), a0=r1(a0[l]), k_k=r1(k_k[l]), k_a=r1(k_a[l]),
        r_k=r1(r_k[l]), wd=_pad_rows(w_decay_up[l], 0), wa=_pad_rows(w_a_up[l], DECAY_LORA),
        wg=_pad_rows(w_g_up[l], DECAY_LORA + AAA_LORA), lnx_w=r1(lnx_w[l]), lnx_b=r1(lnx_b[l]),
        wgate=wgate, bgate=r1(b_gla_gate[l]), gla_g=r1(gla_norm_g[l]),
        w_out=w_out[l].astype(BF16), norm2_g=r1(norm2_g[l]), w_router=w_router, b_router=b_router)


def _run_layer(x, mod, shift0, wkv0, gla0, lp, experts, final):
    bsz, seq, d = x.shape
    assert seq % CHUNK == 0
    bb = 2 if seq >= RWKV_STEP_ROWS else min(bsz, RECURRENT_MAX_SEQS)
    assert bsz % bb == 0
    tg = min(seq, GLA_STEP_ROWS)
    consts = _consts(TOKEN_TILE_ROWS, tg)
    m = lambda j: mod[:, j:j + 1, :]
    sh1, sc1, gt1, sh2, sc2, gt2 = (m(j) for j in range(6))
    proj, tails = _in_proj(x, sc1, sh1, lp["norm1_g"], lp["w_in"], TOKEN_TILE_ROWS)
    tails = tails.reshape(bsz, -1, RWKV_PROJ)
    new_shift = tails[:, -1]
    prev = jnp.concatenate([shift0[:, None, :], tails[:, :-1]], axis=1).reshape(-1, 1, RWKV_PROJ)
    ab, rkv, bg, gl = _rwkv_prep(proj, prev, lp, consts, TOKEN_TILE_ROWS)
    gl = gl.reshape(bsz, seq // CHUNK, RWKV_WIDTH)
    s0 = _heads_block_diag(wkv0.reshape(bsz, RWKV_TILES, RWKV_TILE_HEADS, RWKV_HEAD, RWKV_HEAD))
    casts = tuple(e.reshape(-1, e.shape[-1]) for e in experts if e.dtype != BF16)
    yr, s_bd, *cast = _rwkv_chunk(ab, rkv, gl[:, :, None, :], s0, bb, min(seq, RWKV_STEP_ROWS) // CHUNK, casts)
    if cast:
        experts = tuple(c.reshape(e.shape) for c, e in zip(cast, experts))
    new_wkv = _diag_blocks(s_bd, RWKV_TILE_HEADS).reshape(bsz, RWKV_HEADS, RWKV_HEAD, RWKV_HEAD)
    t0 = _heads_block_diag(jnp.swapaxes(gla0, -1, -2).reshape(bsz, GLA_PAIRS, 2, GLA_DV, GLA_DK))
    yg, t_bd = _gla(proj, lp, consts["tri_gla"], t0, bb, tg)
    new_gla = jnp.swapaxes(_diag_blocks(t_bd, 2).reshape(bsz, GLA_HEADS, GLA_DV, GLA_DK), -1, -2)
    x1, h2, route = _out_proj(yr, bg, yg, x, gt1, sc2, sh2, lp, consts, TOKEN_TILE_ROWS)
    out = _moe(h2, route, x1, gt2, *final, *experts, consts["tri_up"], MOE_TILE_ROWS)
    return experts, out, new_shift, new_wkv, new_gla


def kernel(x_prompt, x_sample, c_prompt, c_sample, state_rwkv_shift, state_rwkv_wkv, state_gla_kv, w_ada, b_ada, norm1_g, norm2_g, w_in, mu_shift, w0, w_decay_up, a0, w_a_up, w_g_up, k_k, k_a, r_k, lnx_w, lnx_b, w_gla_gate_up, b_gla_gate, gla_norm_g, w_out, w_router_group, b_router_group, w_router_expert, b_router_expert, w_expert_gate, w_expert_up, w_expert_down, w_ada_final, b_ada_final, normf_g):
    assert w_ada.shape[0] == 1, "the final norm is fused into the single layer's MoE kernel"
    bp, bs = x_prompt.shape[0], x_sample.shape[0]
    d = D_MODEL
    n_rows = -(-(bp + bs) // 8) * 8
    c_all = jnp.zeros((n_rows, d), F32).at[:bp].set(c_prompt).at[bp:bp + bs].set(c_sample)
    modf = _modulation(c_all, w_ada_final, b_ada_final, MODULATION_TILE_COLS).reshape(n_rows, 2, d)
    mod = _modulation(c_all, w_ada, b_ada[0], MODULATION_TILE_COLS, layer=0).reshape(n_rows, 6, d)
    lp = _layer_params(0, w_in, mu_shift, w0, w_decay_up, a0, w_a_up, w_g_up, k_k, k_a, r_k, lnx_w,
                       lnx_b, w_gla_gate_up, b_gla_gate, gla_norm_g, w_out, norm2_g,
                       w_router_group, b_router_group, w_router_expert, b_router_expert)
    lp["norm1_g"] = norm1_g[0].reshape(1, d)
    experts = (w_expert_gate[0], w_expert_up[0], w_expert_down[0])
    groups = [
        (x_prompt, 0, bp, jnp.zeros((bp, RWKV_PROJ), F32),
         jnp.zeros((bp, RWKV_HEADS, RWKV_HEAD, RWKV_HEAD), F32), jnp.zeros((bp, GLA_HEADS, GLA_DK, GLA_DV), F32)),
        (x_sample, bp, bp + bs, state_rwkv_shift[0], state_rwkv_wkv[0], state_gla_kv[0]),
    ]
    ys, states = [], []
    for x, lo, hi, shift0, wkv0, gla0 in groups:
        final = (modf[lo:hi, 1:2], modf[lo:hi, 0:1], normf_g.reshape(1, d))
        experts, y, *st = _run_layer(x, mod[lo:hi], shift0, wkv0, gla0, lp, experts, final)
        ys.append(y)
        states.extend(s[None] for s in st)
    return tuple(ys + states)
```

```python
import functools

import jax
import jax.numpy as jnp
import numpy as np
from jax import lax
from jax.experimental import pallas as pl
from jax.experimental.pallas import tpu as pltpu

F32 = jnp.float32
BF16 = jnp.bfloat16

LANES = 128
VMEM_LIMIT_BYTES = 56 * 1024 * 1024
TOKEN_TILE_ROWS = 512
MOE_TILE_ROWS = 1024
MOE_SUB_ROWS = 256
RECURRENT_STEP_ROWS = 256
RECURRENT_MAX_SEQS = 8
MODULATION_TILE_COLS = 1024
WEIGHT_TILE_COLS = 256

D_MODEL = 1024
CHUNK = 64
RWKV_WIDTH = 512
RWKV_HEAD = 64
RWKV_HEADS = RWKV_WIDTH // RWKV_HEAD
HEAD_LANES = RWKV_HEAD
HEAD_SHIFT = HEAD_LANES.bit_length() - 1
assert 1 << HEAD_SHIFT == HEAD_LANES
RWKV_TILE_LANES = 128
RWKV_TILE_HEADS = RWKV_TILE_LANES // RWKV_HEAD
RWKV_TILES = RWKV_WIDTH // RWKV_TILE_LANES
DECAY_LORA = 32
AAA_LORA = 32
GATE_LORA = 64
LORA_PAD = DECAY_LORA + AAA_LORA + GATE_LORA
RWKV_PROJ = 3 * RWKV_WIDTH + LORA_PAD
GLA_WIDTH = 512
GLA_HEADS = 4
GLA_PAIRS = GLA_HEADS // 2
GLA_DV = GLA_WIDTH // GLA_HEADS
GLA_DK = GLA_DV // 2
GLA_KEY_WIDTH = GLA_HEADS * GLA_DK
GLA_GATE_RANK = 16
GLA_TAU = 16.0
GLA_PROJ = 2 * GLA_KEY_WIDTH + 2 * GLA_WIDTH + GLA_GATE_RANK
GLA_PROJ_PAD = RWKV_PROJ
IN_PROJ_PAD = RWKV_PROJ + GLA_PROJ_PAD
N_GROUPS = 4
EXPERTS_PER_GROUP = 4
N_EXPERTS = N_GROUPS * EXPERTS_PER_GROUP
GROUP_SHIFT = EXPERTS_PER_GROUP.bit_length() - 1
assert 1 << GROUP_SHIFT == EXPERTS_PER_GROUP
ROUTE_ROWS = 24
D_EXPERT = 512
RMS_EPS = 1e-6
LNX_EPS = 64e-5
LOG2_E = 1.4426950408889634

_NN = (((1,), (0,)), ((), ()))
_NT = (((1,), (1,)), ((), ()))
_TN = (((0,), (0,)), ((), ()))


def _dg(a, b, dims=_NN):
    return lax.dot_general(a, b, dims, preferred_element_type=F32)


def _split2(x):
    hi = x.astype(BF16)
    lo = (x - hi.astype(F32)).astype(BF16)
    return hi, lo


def _split3(x):
    hi = x.astype(BF16)
    r1 = x - hi.astype(F32)
    mid = r1.astype(BF16)
    lo = (r1 - mid.astype(F32)).astype(BF16)
    return hi, mid, lo


def _mm1(a, b, dims=_NN):
    return _dg(a.astype(BF16), b.astype(BF16), dims)


def _mm3(a, b, dims=_NN):
    ah, al = _split2(a)
    bh, bl = _split2(b)
    return _dg(ah, bh, dims) + (_dg(ah, bl, dims) + _dg(al, bh, dims))


def _mm_exact_lhs(e, x, dims=_NN):
    h, m, l = _split3(x)
    return _dg(e, h, dims) + (_dg(e, m, dims) + _dg(e, l, dims))


def _mm2_exact_rhs(x, e, dims=_NN):
    h, l = _split2(x)
    return _dg(h, e, dims) + _dg(l, e, dims)


def _softplus(z):
    return jnp.maximum(z, 0.0) + jnp.log(1.0 + jnp.exp(-jnp.abs(z)))


def _sigmoid(z):
    return 1.0 / (1.0 + jnp.exp(-z))


def _silu(z):
    return z * _sigmoid(z)


def _params(*sem):
    return pltpu.CompilerParams(dimension_semantics=sem, vmem_limit_bytes=VMEM_LIMIT_BYTES)


def _mod_kernel(c_ref, w_ref, b_ref, o_ref):
    o_ref[...] = _mm1(_silu(c_ref[...]), w_ref[...]) + b_ref[...]


def _modulation(c, w, b, tn, layer=None):
    rows, d = c.shape
    n = w.shape[-1]
    if layer is None:
        w_spec = pl.BlockSpec((d, tn), lambda j: (0, j))
    else:
        w_spec = pl.BlockSpec((None, d, tn), lambda j: (layer, 0, j))
    return pl.pallas_call(
        _mod_kernel,
        grid=(n // tn,),
        in_specs=[pl.BlockSpec((rows, d), lambda j: (0, 0)), w_spec,
                  pl.BlockSpec((1, tn), lambda j: (0, j))],
        out_specs=pl.BlockSpec((rows, tn), lambda j: (0, j)),
        out_shape=jax.ShapeDtypeStruct((rows, n), F32),
        compiler_params=_params("parallel"),
        name="modulation",
    )(c, w, b.reshape(1, n))


def _win_layout_kernel(w_ref, o_ref):
    qkv_end = RWKV_PROJ + 2 * GLA_KEY_WIDTH + GLA_WIDTH
    gate_end = qkv_end + GLA_GATE_RANK
    o_ref[:qkv_end] = w_ref[:qkv_end].astype(BF16)
    o_ref[qkv_end:qkv_end + GLA_WIDTH] = w_ref[gate_end:gate_end + GLA_WIDTH].astype(BF16)
    o_ref[qkv_end + GLA_WIDTH:RWKV_PROJ + GLA_PROJ] = w_ref[qkv_end:gate_end].astype(BF16)
    o_ref[RWKV_PROJ + GLA_PROJ:] = jnp.zeros((IN_PROJ_PAD - RWKV_PROJ - GLA_PROJ, o_ref.shape[1]), BF16)


def _win_layout(wt, cols):
    n, d = wt.shape
    return pl.pallas_call(
        _win_layout_kernel,
        grid=(d // cols,),
        in_specs=[pl.BlockSpec((n, cols), lambda i: (0, i))],
        out_specs=pl.BlockSpec((IN_PROJ_PAD, cols), lambda i: (0, i)),
        out_shape=jax.ShapeDtypeStruct((IN_PROJ_PAD, d), BF16),
        compiler_params=_params("parallel"),
        name="w_in_layout",
    )(wt)


def _rms_mod(x, g, sc, sh):
    ms = jnp.mean(x * x, axis=-1, keepdims=True)
    return (x * lax.rsqrt(ms + RMS_EPS) * g) * (1.0 + sc) + sh


def _token_tile(bsz, seq, rows):
    if seq >= rows:
        assert seq % rows == 0
        return 1, rows
    nb = min(bsz, rows // seq)
    assert bsz % nb == 0
    return nb, seq


def _inproj_kernel(x_ref, sc_ref, sh_ref, g_ref, w_ref, o_ref, last_ref, *, n_step):
    nb, rb, d = x_ref.shape
    h = _rms_mod(x_ref[...], g_ref[...], sc_ref[...], sh_ref[...])
    hb = h.reshape(nb * rb, d).astype(BF16)
    for j in range(IN_PROJ_PAD // n_step):
        cols = slice(j * n_step, (j + 1) * n_step)
        o_ref[:, :, cols] = _dg(hb, w_ref[cols, :], _NT).reshape(nb, rb, n_step)
    last_ref[...] = o_ref[:, rb - 1:rb, :RWKV_PROJ]


def _in_proj(x, sc, sh, g, w, rows):
    bsz, seq, d = x.shape
    nb, rb = _token_tile(bsz, seq, rows)
    n_seq_tiles = seq // rb
    vec = pl.BlockSpec((nb, 1, d), lambda b, i: (b, 0, 0))
    return pl.pallas_call(
        functools.partial(_inproj_kernel, n_step=2 * LANES),
        grid=(bsz // nb, n_seq_tiles),
        in_specs=[pl.BlockSpec((nb, rb, d), lambda b, i: (b, i, 0)), vec, vec,
                  pl.BlockSpec((1, d), lambda b, i: (0, 0)),
                  pl.BlockSpec((IN_PROJ_PAD, d), lambda b, i: (0, 0))],
        out_specs=[pl.BlockSpec((nb, rb, IN_PROJ_PAD), lambda b, i: (b, i, 0)),
                   pl.BlockSpec((nb, 1, RWKV_PROJ), lambda b, i: (b * n_seq_tiles + i, 0, 0))],
        out_shape=[jax.ShapeDtypeStruct((bsz, seq, IN_PROJ_PAD), F32),
                   jax.ShapeDtypeStruct((bsz * n_seq_tiles, 1, RWKV_PROJ), F32)],
        compiler_params=_params("parallel", "parallel"),
        name="norm1_in_proj",
    )(x, sc, sh, g, w)


def _rwkv_prep_kernel(p_ref, prev_ref, mu_ref, w0_ref, a0_ref, kk_ref, ka_ref, rk_ref,
                      wd_ref, wa_ref, wg_ref, seg_ref, tri_ref,
                      ab_ref, rkv_ref, bg_ref, gl_ref):
    nb, rb, wp = p_ref.shape
    tr = nb * rb
    p = p_ref[...].reshape(tr, wp)
    row = lax.broadcasted_iota(jnp.int32, (nb, rb, wp), 1)
    xx = jnp.where(row == 0, prev_ref[...], pltpu.roll(p, 1, 0).reshape(nb, rb, wp)).reshape(tr, wp)
    ps = p + (xx - p) * mu_ref[...]
    w = RWKV_WIDTH
    r, k, v, lora = ps[:, :w], ps[:, w:2 * w], ps[:, 2 * w:3 * w], ps[:, 3 * w:]
    logw = -_softplus(-(w0_ref[...] + _mm1(jnp.tanh(lora), wd_ref[...]))) - 0.5
    lw = jnp.exp(logw) * (-LOG2_E)
    a = _sigmoid(a0_ref[...] + _mm1(lora, wa_ref[...]))
    g = _mm1(_sigmoid(lora), wg_ref[...])
    seg = seg_ref[...]
    kk = k * kk_ref[...]
    kk = kk / jnp.maximum(jnp.sqrt(_mm2_exact_rhs(kk * kk, seg)), 1e-12)
    k2 = k * (1.0 + (a - 1.0) * ka_ref[...])
    cum = _mm_exact_lhs(tri_ref[...], lw)
    lasts = [jnp.exp2(cum[c * CHUNK + CHUNK - 1:(c + 1) * CHUNK, :]) for c in range(tr // CHUNK)]
    for c, l in enumerate(lasts):
        gl_ref[c:c + 1, :] = l

    def to_chunk_end(val):
        return jnp.concatenate([val[c * CHUNK:(c + 1) * CHUNK] * l for c, l in enumerate(lasts)], axis=0)

    gamma = jnp.exp2(cum)
    ginv = 1.0 / gamma
    bt = (kk * a) * ginv
    kt = k2 * ginv

    def put(ref, slot, val):
        ref[:, :, slot * w:(slot + 1) * w] = val.reshape(nb, rb, w).astype(ref.dtype)

    put(ab_ref, 0, -kk * jnp.exp2(cum - lw))
    put(ab_ref, 1, bt)
    put(rkv_ref, 0, r * gamma)
    put(rkv_ref, 1, kt)
    put(rkv_ref, 2, to_chunk_end(bt))
    put(rkv_ref, 3, to_chunk_end(kt))
    put(rkv_ref, 4, v)
    put(bg_ref, 0, _mm2_exact_rhs(r * k2 * rk_ref[...], seg) * v)
    put(bg_ref, 1, g)


def _rwkv_prep(proj, prev, lp, consts, rows):
    bsz, seq, _ = proj.shape
    w = RWKV_WIDTH
    nb, rb = _token_tile(bsz, seq, rows)
    n_seq_tiles = seq // rb
    flat = lambda b, i: (b * n_seq_tiles + i, 0)
    row = lambda n: pl.BlockSpec((1, n), lambda b, i: (0, 0))
    full = lambda a: pl.BlockSpec(a.shape, lambda b, i: (0,) * a.ndim)
    packed = [(2, F32), (5, BF16), (2, F32)]
    return pl.pallas_call(
        _rwkv_prep_kernel,
        grid=(bsz // nb, n_seq_tiles),
        in_specs=[pl.BlockSpec((nb, rb, RWKV_PROJ), lambda b, i: (b, i, 0)),
                  pl.BlockSpec((nb, 1, RWKV_PROJ), lambda b, i: flat(b, i) + (0,)),
                  row(RWKV_PROJ), row(w), row(w), row(w), row(w), row(w),
                  full(lp["wd"]), full(lp["wa"]), full(lp["wg"]), full(consts["seg"]), full(consts["tri"])],
        out_specs=[pl.BlockSpec((nb, rb, k * w), lambda b, i: (b, i, 0)) for k, _ in packed]
        + [pl.BlockSpec((nb * rb // CHUNK, w), flat)],
        out_shape=[jax.ShapeDtypeStruct((bsz, seq, k * w), dt) for k, dt in packed]
        + [jax.ShapeDtypeStruct((bsz * seq // CHUNK, w), F32)],
        compiler_params=_params("parallel", "parallel"),
        name="rwkv_prep",
    )(proj, prev, lp["mu"], lp["w0"], lp["a0"], lp["k_k"], lp["k_a"], lp["r_k"],
      lp["wd"], lp["wa"], lp["wg"], consts["seg"], consts["tri"])


def _head_masks(width):
    lane = lax.broadcasted_iota(jnp.int32, (CHUNK, width), 1)
    row = lax.broadcasted_iota(jnp.int32, (CHUNK, width), 0)
    return lane >> HEAD_SHIFT, row, lane & (HEAD_LANES - 1)


def _block_diag(x, head):
    z = jnp.zeros_like(x)
    return jnp.concatenate([jnp.where(head == j, x, z) for j in range(x.shape[1] // HEAD_LANES)], axis=0)


def _rwkv_stages(ab_ref, rkv_ref, gl_ref, s0_ref, cast_in, y_ref, s_ref, cast_out, bb, nck):
    @pl.when(pl.program_id(1) == 0)
    def _():
        s_ref[...] = s0_ref[...]

    for src, dst in zip(cast_in, cast_out):
        dst[...] = src[...].astype(BF16)

    tw = RWKV_TILE_LANES
    head, row, col = _head_masks(tw)
    strict = col < row
    incl = col <= row
    same8 = (col >> 3) == (row >> 3)
    lane2 = lax.broadcasted_iota(jnp.int32, (tw, tw), 1)
    row2 = lax.broadcasted_iota(jnp.int32, (tw, tw), 0)
    same_head = (lane2 >> HEAD_SHIFT) == (row2 >> HEAD_SHIFT)
    bd = functools.partial(_block_diag, head=head)
    c = CHUNK

    def pmm(p, q):
        return _dg(p.astype(BF16), bd(q.astype(BF16)))

    items = [(b, ck, pr) for b in range(bb) for ck in range(nck) for pr in range(RWKV_TILES)]
    n = range(len(items))
    rows = lambda ck: slice(ck * c, (ck + 1) * c)
    lanes = lambda pr: slice(pr * tw, (pr + 1) * tw)
    w = RWKV_WIDTH

    def ld(ref, slot):
        return [ref[b, rows(ck), slot * w + pr * tw:slot * w + (pr + 1) * tw] for b, ck, pr in items]

    cat0 = lambda *xs: jnp.concatenate(xs, axis=0)
    cat1 = lambda *xs: jnp.concatenate(xs, axis=1)
    at, bt = ld(ab_ref, 0), ld(ab_ref, 1)
    rt, kt, be, ke, v = (ld(rkv_ref, slot) for slot in range(5))
    ats, bts = [_split2(a) for a in at], [_split2(b) for b in bt]
    atb = [hi for hi, _ in ats]
    zero = jnp.zeros((c, tw), F32)
    gk = [_dg(cat0(atb[i], rt[i]), bd(kt[i]), _NT) for i in n]
    aak = [jnp.where(strict, gk[i][:c], zero) for i in n]
    ark = [jnp.where(incl, gk[i][c:], zero) for i in n]
    yield
    gb = [_dg(cat0(ats[i][0], ats[i][1], rt[i]), bd(bts[i][0]), _NT) for i in n]
    arb = [jnp.where(incl, gb[i][2 * c:], zero) for i in n]
    aab = [jnp.where(strict, gb[i][:c] + (gb[i][c:2 * c] + _dg(atb[i], bd(bts[i][1]), _NT)), zero) for i in n]
    yield
    z = [pmm(aak[i], v[i]) for i in n]
    a8 = [jnp.where(same8, aab[i], zero) for i in n]
    p2 = [pmm(a8[i], a8[i]) for i in n]
    yield
    p4 = [pmm(p2[i], p2[i]) for i in n]
    yield
    nn = [a8[i] + p2[i] + pmm(p2[i], a8[i]) for i in n]
    yield
    nn = [nn[i] + p4[i] + pmm(p4[i], nn[i]) for i in n]
    yield
    for lvl in (3, 4, 5):
        joins = ((col >> (lvl + 1)) == (row >> (lvl + 1))) & ((col >> lvl) != (row >> lvl))
        e = [jnp.where(joins, aab[i], zero) for i in n]
        te = [e[i] + pmm(nn[i], e[i]) for i in n]
        yield
        nn = [nn[i] + te[i] + pmm(te[i], nn[i]) for i in n]
        yield
    wu = [cat1(at[i], z[i]) + _dg(nn[i].astype(BF16), cat1(bd(atb[i]), bd(z[i].astype(BF16)))) for i in n]
    abk = [cat1(arb[i], ark[i]).astype(BF16) for i in n]
    yield
    s = {(b, pr): s_ref[b, pr] for b in range(bb) for pr in range(RWKV_TILES)}
    for ck in range(nck):
        cur = [i for i in n if items[i][1] == ck]
        key = lambda i: (items[i][0], items[i][2])
        x = {i: _dg(cat0(wu[i][:, :tw].astype(BF16), rt[i]), s[key(i)].astype(BF16), _NT) for i in cur}
        ub = {i: (x[i][:c] + wu[i][:, tw:]).astype(BF16) for i in cur}
        yield
        upd = {i: _dg(cat0(ub[i], v[i]), cat0(be[i], ke[i]), _TN) for i in cur}
        for i in cur:
            b, _, pr = items[i]
            y_ref[b, rows(ck), lanes(pr)] = x[i][c:] + _dg(abk[i], cat0(bd(ub[i]), bd(v[i])))
            s[b, pr] = (s[b, pr] * gl_ref[b, ck, :, lanes(pr)]
                        + jnp.where(same_head, upd[i], jnp.zeros_like(upd[i])))
        yield
    for (b, pr), val in s.items():
        s_ref[b, pr] = val


def _gla_stages(p_ref, wgate_ref, bgate_ref, tri_ref, ng_ref, s0_ref, y_ref, s_ref, bb, nck):
    @pl.when(pl.program_id(1) == 0)
    def _():
        s_ref[...] = s0_ref[...]

    head, row, col = _head_masks(LANES)
    incl = col <= row
    bd = functools.partial(_block_diag, head=head)
    lane2 = lax.broadcasted_iota(jnp.int32, (2 * GLA_DV, LANES), 1)
    row2 = lax.broadcasted_iota(jnp.int32, (2 * GLA_DV, LANES), 0)
    same_head = (lane2 < GLA_DK) == (row2 < GLA_DV)
    kw, gw = GLA_KEY_WIDTH, GLA_WIDTH
    zero = jnp.zeros((CHUNK, LANES), F32)
    cat0 = lambda xs: jnp.concatenate(xs, axis=0)
    cat1 = lambda xs: jnp.concatenate(xs, axis=1)
    rows = lambda c: slice(c * CHUNK, (c + 1) * CHUNK)
    lanes = lambda pr: slice(pr * LANES, (pr + 1) * LANES)
    bs, cs = range(bb), range(nck)
    x = [p_ref[b] for b in bs]
    gate = [_mm1(x[b][:, 2 * kw + 2 * gw:], wgate_ref[...]) + bgate_ref[...] for b in bs]
    la = [(jnp.minimum(g, 0.0) - jnp.log(1.0 + jnp.exp(-jnp.abs(g)))) / GLA_TAU for g in gate]
    yield
    cum = [_mm_exact_lhs(tri_ref[...], la[b]) for b in bs]
    yield
    q_dec = [(x[b][:, :kw] * (GLA_DK ** -0.5)) * jnp.exp(cum[b]) for b in bs]
    k_inv = [x[b][:, kw:2 * kw] * jnp.exp(-cum[b]) for b in bs]
    last = [[cum[b][c * CHUNK + CHUNK - 1:(c + 1) * CHUNK, :] for c in cs] for b in bs]
    k_end = [[x[b][rows(c), kw:2 * kw] * jnp.exp(last[b][c] - cum[b][rows(c)]) for c in cs] for b in bs]
    dec = [[jnp.exp(last[b][c]) for c in cs] for b in bs]
    yield
    items = [(b, c, pr) for b in bs for c in cs for pr in range(GLA_PAIRS)]
    vp = {(b, c, pr): x[b][rows(c), 2 * kw + pr * 2 * GLA_DV:2 * kw + (pr + 1) * 2 * GLA_DV]
          for b, c, pr in items}
    att = {(b, c, pr): jnp.where(incl, _mm1(q_dec[b][rows(c), lanes(pr)],
                                            bd(k_inv[b][rows(c), lanes(pr)]), _NT), zero)
           for b, c, pr in items}
    yield
    upd = {it: _mm1(vp[it], k_end[it[0]][it[1]][:, lanes(it[2])], _TN) for it in items}
    yield
    intra = {it: _mm1(att[it], cat0([cat1([vp[it][:, :GLA_DV], zero]), cat1([zero, vp[it][:, GLA_DV:]])]))
             for it in items}
    yield
    st = {}
    for b in bs:
        for pr in range(GLA_PAIRS):
            cur = s_ref[b, pr]
            for c in cs:
                st[b, c, pr] = cur
                cur = cur * dec[b][c][:, lanes(pr)] + jnp.where(same_head, upd[b, c, pr], jnp.zeros_like(cur))
            s_ref[b, pr] = cur
    yield
    o = {(b, c, pr): intra[b, c, pr] + _mm1(q_dec[b][rows(c), lanes(pr)], st[b, c, pr], _NT)
         for b, c, pr in items}
    yield
    for b in bs:
        ob = cat0([cat1([o[b, c, pr] for pr in range(GLA_PAIRS)]) for c in cs])
        heads = [ob[:, h * GLA_DV:(h + 1) * GLA_DV] for h in range(GLA_HEADS)]
        normed = [oh * lax.rsqrt(jnp.mean(oh * oh, axis=-1, keepdims=True) + RMS_EPS) for oh in heads]
        y_ref[b] = cat1(normed) * ng_ref[...] * _silu(x[b][:, 2 * kw + gw:2 * kw + 2 * gw])


def _recurrent_kernel(*refs, bb, nck, n_cast):
    rwkv_in, refs = refs[:4], refs[4:]
    cast_in, refs = refs[:n_cast], refs[n_cast:]
    gla_in, refs = refs[:6], refs[6:]
    (y_ref, s_ref), refs = refs[:2], refs[2:]
    cast_out, (yg_ref, t_ref) = refs[:n_cast], refs[n_cast:]
    rwkv = _rwkv_stages(*rwkv_in, cast_in, y_ref, s_ref, cast_out, bb, nck)
    gla = _gla_stages(*gla_in, yg_ref, t_ref, bb, nck)
    done = object()
    live = [rwkv, rwkv, gla]
    while live:
        for gen in list(live):
            if gen in live and next(gen, done) is done:
                live = [g for g in live if g is not gen]


def _recurrent(ab, rkv, gl, s0, proj, lp, tri, t0, bb, nck, casts=()):
    bsz, seq, _ = ab.shape
    w = RWKV_WIDTH
    rows = nck * CHUNK
    n_seq_steps = seq // rows
    n_steps = (bsz // bb) * n_seq_steps
    tok = lambda a: pl.BlockSpec((bb, rows, a.shape[-1]), lambda i, c: (i, c, 0))
    full = lambda a: pl.BlockSpec(a.shape, lambda i, c: (0,) * a.ndim)
    st = pl.BlockSpec((bb, RWKV_TILES, RWKV_TILE_LANES, RWKV_TILE_LANES), lambda i, c: (i, 0, 0, 0))
    gst = pl.BlockSpec((bb, GLA_PAIRS, 2 * GLA_DV, LANES), lambda i, c: (i, 0, 0, 0))
    slab = lambda a: pl.BlockSpec((a.shape[0] // n_steps, a.shape[1]), lambda i, c: (i * n_seq_steps + c, 0))
    assert all(a.shape[0] % (16 * n_steps) == 0 for a in casts)
    out_tok = pl.BlockSpec((bb, rows, w), lambda i, c: (i, c, 0))
    return pl.pallas_call(
        functools.partial(_recurrent_kernel, bb=bb, nck=nck, n_cast=len(casts)),
        grid=(bsz // bb, n_seq_steps),
        in_specs=[tok(ab), tok(rkv), pl.BlockSpec((bb, nck, 1, w), lambda i, c: (i, c, 0, 0)), st]
        + [slab(a) for a in casts]
        + [pl.BlockSpec((bb, rows, GLA_PROJ_PAD), lambda i, c: (i, c, 1)),
           full(lp["wgate"]), full(lp["bgate"]), full(tri), full(lp["gla_g"]), gst],
        out_specs=[out_tok, st] + [slab(a) for a in casts] + [out_tok, gst],
        out_shape=[jax.ShapeDtypeStruct((bsz, seq, w), F32),
                   jax.ShapeDtypeStruct((bsz, RWKV_TILES, RWKV_TILE_LANES, RWKV_TILE_LANES), F32)]
        + [jax.ShapeDtypeStruct(a.shape, BF16) for a in casts]
        + [jax.ShapeDtypeStruct((bsz, seq, GLA_WIDTH), F32),
           jax.ShapeDtypeStruct((bsz, GLA_PAIRS, 2 * GLA_DV, LANES), F32)],
        compiler_params=_params("parallel", "arbitrary"),
        name="rwkv_gla_chunk",
    )(ab, rkv, gl, s0, *casts, proj, lp["wgate"], lp["bgate"], tri, lp["gla_g"], t0)


def _route(lgt):
    tm = lgt.shape[1]
    n_pad = ROUTE_ROWS - N_EXPERTS
    rowg = lax.broadcasted_iota(jnp.int32, (n_pad, tm), 0)
    rowe = lax.broadcasted_iota(jnp.int32, (N_EXPERTS, tm), 0)
    rowg_f, rowe_f = rowg.astype(F32), rowe.astype(F32)

    def first_argmax(vals, mx, rows_f):
        return jnp.min(jnp.where(vals == mx, rows_f, float(ROUTE_ROWS)), axis=0, keepdims=True)

    is_group = rowg < N_GROUPS
    lg = jnp.where(is_group, lgt[N_EXPERTS:], -jnp.inf)
    gmax = jnp.max(lg, axis=0, keepdims=True)
    gi = first_argmax(lg, gmax, rowg_f)
    pg_top = 1.0 / jnp.sum(jnp.where(is_group, jnp.exp(lg - gmax), 0.0), axis=0, keepdims=True)
    in_group = (rowe >> GROUP_SHIFT).astype(F32) == gi
    le = jnp.where(in_group, lgt[:N_EXPERTS], -jnp.inf)
    m1 = jnp.max(le, axis=0, keepdims=True)
    i1 = first_argmax(le, m1, rowe_f)
    le2 = jnp.where(rowe_f == i1, -jnp.inf, le)
    m2 = jnp.max(le2, axis=0, keepdims=True)
    i2 = first_argmax(le2, m2, rowe_f)
    p2 = jnp.exp(m2 - m1)
    w1 = pg_top / (1.0 + p2)
    w2 = pg_top * p2 / (1.0 + p2)
    comb = jnp.where(rowe_f == i1, w1, 0.0) + jnp.where(rowe_f == i2, w2, 0.0)
    return jnp.concatenate([comb, jnp.where(rowg == 0, gi, 0.0)], axis=0)


def _outproj_kernel(yr_ref, bg_ref, yg_ref, x_ref, gt_ref, sc_ref, sh_ref,
                    lnw_ref, lnb_ref, seg_ref, wout_ref, g2_ref, wr_ref, br_ref,
                    x1_ref, h2_ref, route_ref):
    nb, rb, d_model = x_ref.shape
    tm = nb * rb
    flat = lambda ref: ref[...].reshape(tm, ref.shape[-1])
    seg = seg_ref[...]
    y = flat(yr_ref)
    inv_n = 1.0 / RWKV_HEAD
    d = y - _mm2_exact_rhs(y, seg) * inv_n
    var = _mm2_exact_rhs(d * d, seg) * inv_n
    yn = d * lax.rsqrt(var + LNX_EPS) * lnw_ref[...] + lnb_ref[...]
    bg = flat(bg_ref)
    yr = (yn + bg[:, :RWKV_WIDTH]) * bg[:, RWKV_WIDTH:]
    mix = jnp.concatenate([yr, flat(yg_ref)], axis=1)
    x1 = x_ref[...] + gt_ref[...] * _mm1(mix, wout_ref[...]).reshape(nb, rb, d_model)
    x1_ref[...] = x1
    h2 = _rms_mod(x1, g2_ref[...], sc_ref[...], sh_ref[...])
    h2_ref[...] = h2.astype(BF16)
    route_ref[...] = _route(_mm3(wr_ref[...], h2.reshape(tm, d_model), _NT) + br_ref[...])


def _out_proj(yr, bg, yg, x, gt, sc, sh, lp, consts, rows):
    bsz, seq, d = x.shape
    nb, rb = _token_tile(bsz, seq, rows)
    n_seq_tiles = seq // rb
    half = pl.BlockSpec((nb, rb, RWKV_WIDTH), lambda b, i: (b, i, 0))
    tokd = pl.BlockSpec((nb, rb, d), lambda b, i: (b, i, 0))
    vec = pl.BlockSpec((nb, 1, d), lambda b, i: (b, 0, 0))
    full = lambda a: pl.BlockSpec(a.shape, lambda b, i: (0,) * a.ndim)
    args = (lp["lnx_w"], lp["lnx_b"], consts["seg"], lp["w_out"], lp["norm2_g"], lp["w_router"], lp["b_router"])
    return pl.pallas_call(
        _outproj_kernel,
        grid=(bsz // nb, seq // rb),
        in_specs=[half, pl.BlockSpec((nb, rb, 2 * RWKV_WIDTH), lambda b, i: (b, i, 0)), half, tokd,
                  vec, vec, vec] + [full(a) for a in args],
        out_specs=[tokd, tokd, pl.BlockSpec((ROUTE_ROWS, nb * rb), lambda b, i: (0, b * n_seq_tiles + i))],
        out_shape=[jax.ShapeDtypeStruct((bsz, seq, d), F32),
                   jax.ShapeDtypeStruct((bsz, seq, d), BF16),
                   jax.ShapeDtypeStruct((ROUTE_ROWS, bsz * seq), F32)],
        compiler_params=_params("parallel", "parallel"),
        name="out_proj_router",
    )(yr, bg, yg, x, gt, sc, sh, *args)


def _moe_kernel(h_ref, route_ref, x1_ref, gt_ref, sc_ref, sh_ref, gf_ref, tri_ref, wg_ref, wu_ref, wd_ref,
                y_ref, ys_ref, pos_ref, meta_ref):
    g = pl.program_id(2)
    nb, rb, d = h_ref.shape
    tm = nb * rb
    sub, tail = MOE_SUB_ROWS, MOE_SUB_ROWS // 2
    cap = ys_ref.shape[0]
    n_lane_tiles = tm // LANES
    lane_tile = lambda k: slice(k * LANES, (k + 1) * LANES)

    @pl.when(g == 0)
    def _sort():
        row8 = lax.broadcasted_iota(jnp.int32, (8, LANES), 0).astype(F32)
        carry = jnp.zeros((8, 1), F32)
        members, ranks = [], []
        for k in range(n_lane_tiles):
            blk = jnp.where(row8 == route_ref[N_EXPERTS:N_EXPERTS + 1, lane_tile(k)], 1.0, 0.0)
            members.append(blk)
            ranks.append(_dg(blk.astype(BF16), tri_ref[...]) - blk + carry)
            carry = carry + jnp.sum(blk, axis=1, keepdims=True)
        rowc = lax.broadcasted_iota(jnp.int32, (8, 1), 0)
        lane = lax.broadcasted_iota(jnp.int32, (1, LANES), 1)
        first = jnp.zeros((1, 1), F32)
        off_col = jnp.zeros((8, 1), F32)
        meta = jnp.zeros((1, LANES), F32)
        for grp in range(N_GROUPS):
            count = jnp.sum(jnp.where(rowc == grp, carry, 0.0), axis=0, keepdims=True)
            blocks = jnp.ceil(count * (1.0 / tail))
            pairs = jnp.floor(blocks * 0.5)
            odd = blocks - 2.0 * pairs
            triple = odd * jnp.where(blocks >= 3.0, 1.0, 0.0)
            n_full = pairs - triple
            off_col = off_col + jnp.where(rowc == grp, first, 0.0)
            meta = (meta + jnp.where(lane == grp, first, 0.0) + jnp.where(lane == N_GROUPS + grp, n_full, 0.0)
                    + jnp.where(lane == 2 * N_GROUPS + grp, odd + 2.0 * triple, 0.0))
            first = first + blocks * tail
        meta = (meta + jnp.where(lane == 3 * N_GROUPS, first, 0.0)).astype(jnp.int32)
        for i in range(3 * N_GROUPS + 1):
            meta_ref[i] = meta[0, i]
        for k in range(n_lane_tiles):
            pos = jnp.sum(members[k] * (ranks[k] + off_col), axis=0, keepdims=True)
            pos_ref[:, lane_tile(k)] = pos.astype(jnp.int32)
        ys_ref[...] = jnp.zeros_like(ys_ref)

    def experts(base, n_rows):
        prow = lax.broadcasted_iota(jnp.int32, (n_rows, tm), 0) + base
        onehot = jnp.where(prow == pos_ref[...], 1.0, 0.0).astype(BF16)
        hs = _dg(onehot, h_ref[...].reshape(tm, d)).astype(BF16)
        comb3 = _split3(route_ref[:N_EXPERTS, :])
        cs = _dg(onehot, comb3[0], _NT) + (_dg(onehot, comb3[1], _NT) + _dg(onehot, comb3[2], _NT))
        lane = lax.broadcasted_iota(jnp.int32, cs.shape, 1)
        ys = jnp.zeros((n_rows, d), F32)
        for e in range(EXPERTS_PER_GROUP):
            hid = _silu(_dg(hs, wg_ref[e])) * _dg(hs, wu_ref[e])
            ce = jnp.sum(jnp.where(lane == g * EXPERTS_PER_GROUP + e, cs, 0.0), axis=-1, keepdims=True)
            ys = ys + ce * _dg(hid.astype(BF16), wd_ref[e])
        ys_ref[pl.ds(base, n_rows), :] = ys.astype(BF16)

    first_row = meta_ref[g]
    n_full = meta_ref[N_GROUPS + g]

    def full_sub_tile(j, carry_):
        experts(pl.multiple_of(first_row + j * sub, tail), sub)
        return carry_

    lax.fori_loop(0, n_full, full_sub_tile, 0)

    for last_blocks in (1, 3):
        @pl.when(meta_ref[2 * N_GROUPS + g] == last_blocks)
        def _last(last_blocks=last_blocks):
            experts(pl.multiple_of(first_row + n_full * sub, tail), last_blocks * tail)

    def unsort(n_sorted):
        ys_all = ys_ref[:n_sorted]
        prow = lax.broadcasted_iota(jnp.int32, (n_sorted, LANES), 0)
        seqs = max(LANES // rb, 1)
        for k in range(n_lane_tiles):
            onehot = jnp.where(prow == pos_ref[:, lane_tile(k)], 1.0, 0.0).astype(BF16)
            moe = _dg(onehot, ys_all, _TN).reshape(seqs, LANES // seqs, d)
            b0 = k * LANES // rb
            r0 = k * LANES - b0 * rb
            bs, rs = slice(b0, b0 + seqs), slice(r0, r0 + LANES // seqs)
            x2 = x1_ref[bs, rs] + gt_ref[bs] * moe
            y_ref[bs, rs] = _rms_mod(x2, gf_ref[...], sc_ref[bs], sh_ref[bs])

    usual = tm + (cap - tm) // 2
    last = g == N_GROUPS - 1
    pl.when(last & (meta_ref[3 * N_GROUPS] <= usual))(lambda: unsort(usual))
    pl.when(last & (meta_ref[3 * N_GROUPS] > usual))(lambda: unsort(cap))


def _moe(h2, route, x1, gt, sc, sh, gf, wg, wu, wd, tri, rows):
    bsz, seq, d = x1.shape
    nb, rb = _token_tile(bsz, seq, rows)
    tm = nb * rb
    assert tm % LANES == 0 and (rb % LANES == 0 or LANES % rb == 0)
    n_seq_tiles = seq // rb
    cap = tm + N_GROUPS * (MOE_SUB_ROWS // 2)
    tokd = pl.BlockSpec((nb, rb, d), lambda b, i, g: (b, i, 0))
    vec = pl.BlockSpec((nb, 1, d), lambda b, i, g: (b, 0, 0))
    group_w = lambda shape: pl.BlockSpec((EXPERTS_PER_GROUP,) + shape, lambda b, i, g: (g, 0, 0))
    return pl.pallas_call(
        _moe_kernel,
        grid=(bsz // nb, n_seq_tiles, N_GROUPS),
        in_specs=[tokd, pl.BlockSpec((ROUTE_ROWS, tm), lambda b, i, g: (0, b * n_seq_tiles + i)),
                  tokd, vec, vec, vec,
                  pl.BlockSpec((1, d), lambda b, i, g: (0, 0)),
                  pl.BlockSpec(tri.shape, lambda b, i, g: (0, 0)),
                  group_w((d, D_EXPERT)), group_w((d, D_EXPERT)), group_w((D_EXPERT, d))],
        out_specs=tokd,
        out_shape=jax.ShapeDtypeStruct((bsz, seq, d), F32),
        scratch_shapes=[pltpu.VMEM((cap, d), BF16), pltpu.VMEM((1, tm), jnp.int32),
                        pltpu.SMEM((3 * N_GROUPS + 1,), jnp.int32)],
        compiler_params=_params("parallel", "parallel", "arbitrary"),
        name="moe_final_norm",
    )(h2, route, x1, gt, sc, sh, gf, tri, wg, wu, wd)


def _block_ones(n, blk, lower):
    i = np.arange(n)
    m = (i[:, None] // blk) == (i[None, :] // blk)
    if lower:
        m = m & (i[None, :] <= i[:, None])
    return jnp.asarray(m, dtype=BF16)


def _consts(tr, tg):
    return dict(seg=_block_ones(RWKV_WIDTH, RWKV_HEAD, False), tri=_block_ones(tr, CHUNK, True),
                tri_gla=_block_ones(tg, CHUNK, True), tri_up=_block_ones(LANES, LANES, True).T)


def _heads_block_diag(s):
    n, c = s.shape[2], s.shape[-1]
    rows = [jnp.pad(s[:, :, j], ((0, 0), (0, 0), (0, 0), (j * c, (n - 1 - j) * c))) for j in range(n)]
    return jnp.concatenate(rows, axis=2)


def _diag_blocks(t, n):
    r, c = t.shape[-2] // n, t.shape[-1] // n
    return jnp.stack([t[:, :, j * r:(j + 1) * r, j * c:(j + 1) * c] for j in range(n)], axis=2)


def _pad_rows(w, first_row):
    out = jnp.zeros((LORA_PAD, w.shape[1]), F32)
    return lax.dynamic_update_slice(out, w, (first_row, 0)).astype(BF16)


def _layer_params(l, w_in, mu_shift, w0, w_decay_up, a0, w_a_up, w_g_up, k_k, k_a, r_k, lnx_w, lnx_b,
                  w_gla_gate_up, b_gla_gate, gla_norm_g, w_out, norm2_g,
                  w_router_group, b_router_group, w_router_expert, b_router_expert):
    w_in_p = _win_layout(w_in[l].T, WEIGHT_TILE_COLS)
    n_pad = ROUTE_ROWS - N_EXPERTS - N_GROUPS
    w_router = jnp.concatenate([w_router_expert[l].T, w_router_group[l].T, jnp.zeros((n_pad, D_MODEL), F32)])
    b_router = jnp.concatenate([b_router_expert[l], b_router_group[l],
                                jnp.zeros((n_pad,), F32)]).reshape(ROUTE_ROWS, 1)
    wgate = jnp.zeros((LANES, GLA_KEY_WIDTH), F32).at[:GLA_GATE_RANK].set(w_gla_gate_up[l]).astype(BF16)
    r1 = lambda a: a.reshape(1, -1)
    return dict(
        w_in=w_in_p, mu=r1(mu_shift[l]), w0=r1(w0[l]), a0=r1(a0[l]), k_k=r1(k_k[l]), k_a=r1(k_a[l]),
        r_k=r1(r_k[l]), wd=_pad_rows(w_decay_up[l], 0), wa=_pad_rows(w_a_up[l], DECAY_LORA),
        wg=_pad_rows(w_g_up[l], DECAY_LORA + AAA_LORA), lnx_w=r1(lnx_w[l]), lnx_b=r1(lnx_b[l]),
        wgate=wgate, bgate=r1(b_gla_gate[l]), gla_g=r1(gla_norm_g[l]),
        w_out=w_out[l].astype(BF16), norm2_g=r1(norm2_g[l]), w_router=w_router, b_router=b_router)


def _run_layer(x, mod, shift0, wkv0, gla0, lp, experts, final):
    bsz, seq, d = x.shape
    assert seq % CHUNK == 0
    bb = 2 if seq >= RECURRENT_STEP_ROWS else min(bsz, RECURRENT_MAX_SEQS)
    assert bsz % bb == 0
    nck = min(seq, RECURRENT_STEP_ROWS) // CHUNK
    consts = _consts(TOKEN_TILE_ROWS, nck * CHUNK)
    m = lambda j: mod[:, j:j + 1, :]
    sh1, sc1, gt1, sh2, sc2, gt2 = (m(j) for j in range(6))
    proj, tails = _in_proj(x, sc1, sh1, lp["norm1_g"], lp["w_in"], TOKEN_TILE_ROWS)
    tails = tails.reshape(bsz, -1, RWKV_PROJ)
    new_shift = tails[:, -1]
    prev = jnp.concatenate([shift0[:, None, :], tails[:, :-1]], axis=1).reshape(-1, 1, RWKV_PROJ)
    ab, rkv, bg, gl = _rwkv_prep(proj, prev, lp, consts, TOKEN_TILE_ROWS)
    gl = gl.reshape(bsz, seq // CHUNK, RWKV_WIDTH)
    s0 = _heads_block_diag(wkv0.reshape(bsz, RWKV_TILES, RWKV_TILE_HEADS, RWKV_HEAD, RWKV_HEAD))
    t0 = _heads_block_diag(jnp.swapaxes(gla0, -1, -2).reshape(bsz, GLA_PAIRS, 2, GLA_DV, GLA_DK))
    casts = tuple(e.reshape(-1, e.shape[-1]) for e in experts if e.dtype != BF16)
    yr, s_bd, *cast, yg, t_bd = _recurrent(ab, rkv, gl[:, :, None, :], s0, proj, lp, consts["tri_gla"], t0,
                                           bb, nck, casts)
    if cast:
        experts = tuple(c.reshape(e.shape) for c, e in zip(cast, experts))
    new_wkv = _diag_blocks(s_bd, RWKV_TILE_HEADS).reshape(bsz, RWKV_HEADS, RWKV_HEAD, RWKV_HEAD)
    new_gla = jnp.swapaxes(_diag_blocks(t_bd, 2).reshape(bsz, GLA_HEADS, GLA_DV, GLA_DK), -1, -2)
    x1, h2, route = _out_proj(yr, bg, yg, x, gt1, sc2, sh2, lp, consts, TOKEN_TILE_ROWS)
    out = _moe(h2, route, x1, gt2, *final, *experts, consts["tri_up"], MOE_TILE_ROWS)
    return experts, out, new_shift, new_wkv, new_gla


def kernel(x_prompt, x_sample, c_prompt, c_sample, state_rwkv_shift, state_rwkv_wkv, state_gla_kv, w_ada, b_ada, norm1_g, norm2_g, w_in, mu_shift, w0, w_decay_up, a0, w_a_up, w_g_up, k_k, k_a, r_k, lnx_w, lnx_b, w_gla_gate_up, b_gla_gate, gla_norm_g, w_out, w_router_group, b_router_group, w_router_expert, b_router_expert, w_expert_gate, w_expert_up, w_expert_down, w_ada_final, b_ada_final, normf_g):
    assert w_ada.shape[0] == 1, "the final norm is fused into the single layer's MoE kernel"
    bp, bs = x_prompt.shape[0], x_sample.shape[0]
    d = D_MODEL
    n_rows = -(-(bp + bs) // 8) * 8
    c_all = jnp.zeros((n_rows, d), F32).at[:bp].set(c_prompt).at[bp:bp + bs].set(c_sample)
    modf = _modulation(c_all, w_ada_final, b_ada_final, MODULATION_TILE_COLS).reshape(n_rows, 2, d)
    mod = _modulation(c_all, w_ada, b_ada[0], MODULATION_TILE_COLS, layer=0).reshape(n_rows, 6, d)
    lp = _layer_params(0, w_in, mu_shift, w0, w_decay_up, a0, w_a_up, w_g_up, k_k, k_a, r_k, lnx_w,
                       lnx_b, w_gla_gate_up, b_gla_gate, gla_norm_g, w_out, norm2_g,
                       w_router_group, b_router_group, w_router_expert, b_router_expert)
    lp["norm1_g"] = norm1_g[0].reshape(1, d)
    experts = (w_expert_gate[0], w_expert_up[0], w_expert_down[0])
    groups = [
        (x_prompt, 0, bp, jnp.zeros((bp, RWKV_PROJ), F32),
         jnp.zeros((bp, RWKV_HEADS, RWKV_HEAD, RWKV_HEAD), F32), jnp.zeros((bp, GLA_HEADS, GLA_DK, GLA_DV), F32)),
        (x_sample, bp, bp + bs, state_rwkv_shift[0], state_rwkv_wkv[0], state_gla_kv[0]),
    ]
    ys, states = [], []
    for x, lo, hi, shift0, wkv0, gla0 in groups:
        final = (modf[lo:hi, 1:2], modf[lo:hi, 0:1], normf_g.reshape(1, d))
        experts, y, *st = _run_layer(x, mod[lo:hi], shift0, wkv0, gla0, lp, experts, final)
        ys.append(y)
        states.extend(s[None] for s in st)
    return tuple(ys + states)
```

```python
import functools

import jax
import jax.numpy as jnp
import numpy as np
from jax import lax
from jax.experimental import pallas as pl
from jax.experimental.pallas import tpu as pltpu

F32 = jnp.float32
BF16 = jnp.bfloat16

LANES = 128
VMEM_LIMIT_BYTES = 56 * 1024 * 1024
TOKEN_TILE_ROWS = 512
OUT_PROJ_TILE_ROWS = 1024
MOE_TILE_ROWS = 1024
MOE_SUB_ROWS = 256
RECURRENT_STEP_ROWS = 256
RECURRENT_MAX_SEQS = 8
MODULATION_TILE_COLS = 1024
WEIGHT_TILE_COLS = 256

D_MODEL = 1024
CHUNK = 64
RWKV_WIDTH = 512
RWKV_HEAD = 64
RWKV_HEADS = RWKV_WIDTH // RWKV_HEAD
HEAD_LANES = RWKV_HEAD
HEAD_SHIFT = HEAD_LANES.bit_length() - 1
assert 1 << HEAD_SHIFT == HEAD_LANES
RWKV_TILE_LANES = 128
RWKV_TILE_HEADS = RWKV_TILE_LANES // RWKV_HEAD
RWKV_TILES = RWKV_WIDTH // RWKV_TILE_LANES
DECAY_LORA = 32
AAA_LORA = 32
GATE_LORA = 64
LORA_PAD = DECAY_LORA + AAA_LORA + GATE_LORA
RWKV_PROJ = 3 * RWKV_WIDTH + LORA_PAD
GLA_WIDTH = 512
GLA_HEADS = 4
GLA_PAIRS = GLA_HEADS // 2
GLA_DV = GLA_WIDTH // GLA_HEADS
GLA_DK = GLA_DV // 2
GLA_KEY_WIDTH = GLA_HEADS * GLA_DK
GLA_GATE_RANK = 16
GLA_TAU = 16.0
GLA_PROJ = 2 * GLA_KEY_WIDTH + 2 * GLA_WIDTH + GLA_GATE_RANK
GLA_PROJ_PAD = RWKV_PROJ
IN_PROJ_PAD = RWKV_PROJ + GLA_PROJ_PAD
N_GROUPS = 4
EXPERTS_PER_GROUP = 4
N_EXPERTS = N_GROUPS * EXPERTS_PER_GROUP
GROUP_SHIFT = EXPERTS_PER_GROUP.bit_length() - 1
assert 1 << GROUP_SHIFT == EXPERTS_PER_GROUP
ROUTE_ROWS = 24
D_EXPERT = 512
RMS_EPS = 1e-6
LNX_EPS = 64e-5
LOG2_E = 1.4426950408889634

_NN = (((1,), (0,)), ((), ()))
_NT = (((1,), (1,)), ((), ()))
_TN = (((0,), (0,)), ((), ()))


def _dg(a, b, dims=_NN):
    return lax.dot_general(a, b, dims, preferred_element_type=F32)


def _split2(x):
    hi = x.astype(BF16)
    lo = (x - hi.astype(F32)).astype(BF16)
    return hi, lo


def _split3(x):
    hi = x.astype(BF16)
    r1 = x - hi.astype(F32)
    mid = r1.astype(BF16)
    lo = (r1 - mid.astype(F32)).astype(BF16)
    return hi, mid, lo


def _mm1(a, b, dims=_NN):
    return _dg(a.astype(BF16), b.astype(BF16), dims)


def _mm3(a, b, dims=_NN):
    ah, al = _split2(a)
    bh, bl = _split2(b)
    return _dg(ah, bh, dims) + (_dg(ah, bl, dims) + _dg(al, bh, dims))


def _mm_exact_lhs(e, x, dims=_NN):
    h, m, l = _split3(x)
    return _dg(e, h, dims) + (_dg(e, m, dims) + _dg(e, l, dims))


def _mm2_exact_rhs(x, e, dims=_NN):
    h, l = _split2(x)
    return _dg(h, e, dims) + _dg(l, e, dims)


def _softplus(z):
    return jnp.maximum(z, 0.0) + jnp.log(1.0 + jnp.exp(-jnp.abs(z)))


def _sigmoid(z):
    return 1.0 / (1.0 + jnp.exp(-z))


def _silu(z):
    return z * _sigmoid(z)


def _params(*sem):
    return pltpu.CompilerParams(dimension_semantics=sem, vmem_limit_bytes=VMEM_LIMIT_BYTES)


def _mod_kernel(c_ref, w_ref, b_ref, o_ref):
    o_ref[...] = _mm1(_silu(c_ref[...]), w_ref[...]) + b_ref[...]


def _modulation(c, w, b, tn, layer=None):
    rows, d = c.shape
    n = w.shape[-1]
    if layer is None:
        w_spec = pl.BlockSpec((d, tn), lambda j: (0, j))
    else:
        w_spec = pl.BlockSpec((None, d, tn), lambda j: (layer, 0, j))
    return pl.pallas_call(
        _mod_kernel,
        grid=(n // tn,),
        in_specs=[pl.BlockSpec((rows, d), lambda j: (0, 0)), w_spec,
                  pl.BlockSpec((1, tn), lambda j: (0, j))],
        out_specs=pl.BlockSpec((rows, tn), lambda j: (0, j)),
        out_shape=jax.ShapeDtypeStruct((rows, n), F32),
        compiler_params=_params("parallel"),
        name="modulation",
    )(c, w, b.reshape(1, n))


def _win_layout_kernel(w_ref, o_ref):
    qkv_end = RWKV_PROJ + 2 * GLA_KEY_WIDTH + GLA_WIDTH
    gate_end = qkv_end + GLA_GATE_RANK
    o_ref[:qkv_end] = w_ref[:qkv_end].astype(BF16)
    o_ref[qkv_end:qkv_end + GLA_WIDTH] = w_ref[gate_end:gate_end + GLA_WIDTH].astype(BF16)
    o_ref[qkv_end + GLA_WIDTH:RWKV_PROJ + GLA_PROJ] = w_ref[qkv_end:gate_end].astype(BF16)
    o_ref[RWKV_PROJ + GLA_PROJ:] = jnp.zeros((IN_PROJ_PAD - RWKV_PROJ - GLA_PROJ, o_ref.shape[1]), BF16)


def _win_layout(wt, cols):
    n, d = wt.shape
    return pl.pallas_call(
        _win_layout_kernel,
        grid=(d // cols,),
        in_specs=[pl.BlockSpec((n, cols), lambda i: (0, i))],
        out_specs=pl.BlockSpec((IN_PROJ_PAD, cols), lambda i: (0, i)),
        out_shape=jax.ShapeDtypeStruct((IN_PROJ_PAD, d), BF16),
        compiler_params=_params("parallel"),
        name="w_in_layout",
    )(wt)


def _rms_mod(x, g, sc, sh):
    ms = jnp.mean(x * x, axis=-1, keepdims=True)
    return (x * lax.rsqrt(ms + RMS_EPS) * g) * (1.0 + sc) + sh


def _token_tile(bsz, seq, rows):
    if seq >= rows:
        assert seq % rows == 0
        return 1, rows
    nb = min(bsz, rows // seq)
    assert bsz % nb == 0
    return nb, seq


def _inproj_kernel(x_ref, sc_ref, sh_ref, g_ref, w_ref, o_ref, last_ref, *, n_step):
    nb, rb, d = x_ref.shape
    h = _rms_mod(x_ref[...], g_ref[...], sc_ref[...], sh_ref[...])
    hb = h.reshape(nb * rb, d).astype(BF16)
    for j in range(IN_PROJ_PAD // n_step):
        cols = slice(j * n_step, (j + 1) * n_step)
        o_ref[:, :, cols] = _dg(hb, w_ref[cols, :], _NT).reshape(nb, rb, n_step)
    last_ref[...] = o_ref[:, rb - 1:rb, :RWKV_PROJ]


def _in_proj(x, sc, sh, g, w, rows):
    bsz, seq, d = x.shape
    nb, rb = _token_tile(bsz, seq, rows)
    n_seq_tiles = seq // rb
    vec = pl.BlockSpec((nb, 1, d), lambda b, i: (b, 0, 0))
    return pl.pallas_call(
        functools.partial(_inproj_kernel, n_step=2 * LANES),
        grid=(bsz // nb, n_seq_tiles),
        in_specs=[pl.BlockSpec((nb, rb, d), lambda b, i: (b, i, 0)), vec, vec,
                  pl.BlockSpec((1, d), lambda b, i: (0, 0)),
                  pl.BlockSpec((IN_PROJ_PAD, d), lambda b, i: (0, 0))],
        out_specs=[pl.BlockSpec((nb, rb, IN_PROJ_PAD), lambda b, i: (b, i, 0)),
                   pl.BlockSpec((nb, 1, RWKV_PROJ), lambda b, i: (b * n_seq_tiles + i, 0, 0))],
        out_shape=[jax.ShapeDtypeStruct((bsz, seq, IN_PROJ_PAD), F32),
                   jax.ShapeDtypeStruct((bsz * n_seq_tiles, 1, RWKV_PROJ), F32)],
        compiler_params=_params("parallel", "parallel"),
        name="norm1_in_proj",
    )(x, sc, sh, g, w)


def _rwkv_prep_kernel(p_ref, prev_ref, mu_ref, w0_ref, a0_ref, kk_ref, ka_ref, rk_ref,
                      wd_ref, wa_ref, wg_ref, seg_ref, tri_ref,
                      ab_ref, rkv_ref, bg_ref, gl_ref):
    nb, rb, wp = p_ref.shape
    tr = nb * rb
    p = p_ref[...].reshape(tr, wp)
    row = lax.broadcasted_iota(jnp.int32, (nb, rb, wp), 1)
    xx = jnp.where(row == 0, prev_ref[...], pltpu.roll(p, 1, 0).reshape(nb, rb, wp)).reshape(tr, wp)
    ps = p + (xx - p) * mu_ref[...]
    w = RWKV_WIDTH
    r, k, v, lora = ps[:, :w], ps[:, w:2 * w], ps[:, 2 * w:3 * w], ps[:, 3 * w:]
    logw = -_softplus(-(w0_ref[...] + _mm1(jnp.tanh(lora), wd_ref[...]))) - 0.5
    lw = jnp.exp(logw) * (-LOG2_E)
    a = _sigmoid(a0_ref[...] + _mm1(lora, wa_ref[...]))
    g = _mm1(_sigmoid(lora), wg_ref[...])
    seg = seg_ref[...]
    kk = k * kk_ref[...]
    kk = kk / jnp.maximum(jnp.sqrt(_mm2_exact_rhs(kk * kk, seg)), 1e-12)
    k2 = k * (1.0 + (a - 1.0) * ka_ref[...])
    cum = _mm_exact_lhs(tri_ref[...], lw)
    lasts = [jnp.exp2(cum[c * CHUNK + CHUNK - 1:(c + 1) * CHUNK, :]) for c in range(tr // CHUNK)]
    for c, l in enumerate(lasts):
        gl_ref[c:c + 1, :] = l

    def to_chunk_end(val):
        return jnp.concatenate([val[c * CHUNK:(c + 1) * CHUNK] * l for c, l in enumerate(lasts)], axis=0)

    gamma = jnp.exp2(cum)
    ginv = 1.0 / gamma
    bt = (kk * a) * ginv
    kt = k2 * ginv

    def put(ref, slot, val):
        ref[:, :, slot * w:(slot + 1) * w] = val.reshape(nb, rb, w).astype(ref.dtype)

    put(ab_ref, 0, -kk * jnp.exp2(cum - lw))
    put(ab_ref, 1, bt)
    put(rkv_ref, 0, r * gamma)
    put(rkv_ref, 1, kt)
    put(rkv_ref, 2, to_chunk_end(bt))
    put(rkv_ref, 3, to_chunk_end(kt))
    put(rkv_ref, 4, v)
    put(bg_ref, 0, _mm2_exact_rhs(r * k2 * rk_ref[...], seg) * v)
    put(bg_ref, 1, g)


def _rwkv_prep(proj, prev, lp, consts, rows):
    bsz, seq, _ = proj.shape
    w = RWKV_WIDTH
    nb, rb = _token_tile(bsz, seq, rows)
    n_seq_tiles = seq // rb
    flat = lambda b, i: (b * n_seq_tiles + i, 0)
    row = lambda n: pl.BlockSpec((1, n), lambda b, i: (0, 0))
    full = lambda a: pl.BlockSpec(a.shape, lambda b, i: (0,) * a.ndim)
    packed = [(2, F32), (5, BF16), (2, F32)]
    return pl.pallas_call(
        _rwkv_prep_kernel,
        grid=(bsz // nb, n_seq_tiles),
        in_specs=[pl.BlockSpec((nb, rb, RWKV_PROJ), lambda b, i: (b, i, 0)),
                  pl.BlockSpec((nb, 1, RWKV_PROJ), lambda b, i: flat(b, i) + (0,)),
                  row(RWKV_PROJ), row(w), row(w), row(w), row(w), row(w),
                  full(lp["wd"]), full(lp["wa"]), full(lp["wg"]), full(consts["seg"]), full(consts["tri"])],
        out_specs=[pl.BlockSpec((nb, rb, k * w), lambda b, i: (b, i, 0)) for k, _ in packed]
        + [pl.BlockSpec((nb * rb // CHUNK, w), flat)],
        out_shape=[jax.ShapeDtypeStruct((bsz, seq, k * w), dt) for k, dt in packed]
        + [jax.ShapeDtypeStruct((bsz * seq // CHUNK, w), F32)],
        compiler_params=_params("parallel", "parallel"),
        name="rwkv_prep",
    )(proj, prev, lp["mu"], lp["w0"], lp["a0"], lp["k_k"], lp["k_a"], lp["r_k"],
      lp["wd"], lp["wa"], lp["wg"], consts["seg"], consts["tri"])


def _head_masks(width):
    lane = lax.broadcasted_iota(jnp.int32, (CHUNK, width), 1)
    row = lax.broadcasted_iota(jnp.int32, (CHUNK, width), 0)
    return lane >> HEAD_SHIFT, row, lane & (HEAD_LANES - 1)


def _block_diag(x, head):
    z = jnp.zeros_like(x)
    return jnp.concatenate([jnp.where(head == j, x, z) for j in range(x.shape[1] // HEAD_LANES)], axis=0)


def _rwkv_stages(ab_ref, rkv_ref, gl_ref, s0_ref, cast_in, y_ref, s_ref, cast_out, bb, nck):
    @pl.when(pl.program_id(1) == 0)
    def _():
        s_ref[...] = s0_ref[...]

    for src, dst in zip(cast_in, cast_out):
        dst[...] = src[...].astype(BF16)

    tw = RWKV_TILE_LANES
    head, row, col = _head_masks(tw)
    strict = col < row
    incl = col <= row
    same8 = (col >> 3) == (row >> 3)
    lane2 = lax.broadcasted_iota(jnp.int32, (tw, tw), 1)
    row2 = lax.broadcasted_iota(jnp.int32, (tw, tw), 0)
    same_head = (lane2 >> HEAD_SHIFT) == (row2 >> HEAD_SHIFT)
    bd = functools.partial(_block_diag, head=head)
    c = CHUNK

    def pmm(p, q):
        return _dg(p.astype(BF16), bd(q.astype(BF16)))

    items = [(b, ck, pr) for b in range(bb) for ck in range(nck) for pr in range(RWKV_TILES)]
    n = range(len(items))
    rows = lambda ck: slice(ck * c, (ck + 1) * c)
    lanes = lambda pr: slice(pr * tw, (pr + 1) * tw)
    w = RWKV_WIDTH

    def ld(ref, slot):
        return [ref[b, rows(ck), slot * w + pr * tw:slot * w + (pr + 1) * tw] for b, ck, pr in items]

    cat0 = lambda *xs: jnp.concatenate(xs, axis=0)
    cat1 = lambda *xs: jnp.concatenate(xs, axis=1)
    at, bt = ld(ab_ref, 0), ld(ab_ref, 1)
    rt, kt, be, ke, v = (ld(rkv_ref, slot) for slot in range(5))
    ats, bts = [_split2(a) for a in at], [_split2(b) for b in bt]
    atb = [hi for hi, _ in ats]
    zero = jnp.zeros((c, tw), F32)
    gk = [_dg(cat0(atb[i], rt[i]), bd(kt[i]), _NT) for i in n]
    aak = [jnp.where(strict, gk[i][:c], zero) for i in n]
    ark = [jnp.where(incl, gk[i][c:], zero) for i in n]
    yield
    gb = [_dg(cat0(ats[i][0], ats[i][1], rt[i]), bd(bts[i][0]), _NT) for i in n]
    arb = [jnp.where(incl, gb[i][2 * c:], zero) for i in n]
    aab = [jnp.where(strict, gb[i][:c] + (gb[i][c:2 * c] + _dg(atb[i], bd(bts[i][1]), _NT)), zero) for i in n]
    yield
    z = [pmm(aak[i], v[i]) for i in n]
    a8 = [jnp.where(same8, aab[i], zero) for i in n]
    p2 = [pmm(a8[i], a8[i]) for i in n]
    yield
    p4 = [pmm(p2[i], p2[i]) for i in n]
    yield
    nn = [a8[i] + p2[i] + pmm(p2[i], a8[i]) for i in n]
    yield
    nn = [nn[i] + p4[i] + pmm(p4[i], nn[i]) for i in n]
    yield
    for lvl in (3, 4, 5):
        joins = ((col >> (lvl + 1)) == (row >> (lvl + 1))) & ((col >> lvl) != (row >> lvl))
        e = [jnp.where(joins, aab[i], zero) for i in n]
        te = [e[i] + pmm(nn[i], e[i]) for i in n]
        yield
        nn = [nn[i] + te[i] + pmm(te[i], nn[i]) for i in n]
        yield
    wu = [cat1(at[i], z[i]) + _dg(nn[i].astype(BF16), cat1(bd(atb[i]), bd(z[i].astype(BF16)))) for i in n]
    abk = [cat1(arb[i], ark[i]).astype(BF16) for i in n]
    yield
    s = {(b, pr): s_ref[b, pr] for b in range(bb) for pr in range(RWKV_TILES)}
    for ck in range(nck):
        cur = [i for i in n if items[i][1] == ck]
        key = lambda i: (items[i][0], items[i][2])
        x = {i: _dg(cat0(wu[i][:, :tw].astype(BF16), rt[i]), s[key(i)].astype(BF16), _NT) for i in cur}
        ub = {i: (x[i][:c] + wu[i][:, tw:]).astype(BF16) for i in cur}
        yield
        upd = {i: _dg(cat0(ub[i], v[i]), cat0(be[i], ke[i]), _TN) for i in cur}
        for i in cur:
            b, _, pr = items[i]
            y_ref[b, rows(ck), lanes(pr)] = x[i][c:] + _dg(abk[i], cat0(bd(ub[i]), bd(v[i])))
            s[b, pr] = (s[b, pr] * gl_ref[b, ck, :, lanes(pr)]
                        + jnp.where(same_head, upd[i], jnp.zeros_like(upd[i])))
        yield
    for (b, pr), val in s.items():
        s_ref[b, pr] = val


def _gla_stages(p_ref, wgate_ref, bgate_ref, tri_ref, ng_ref, s0_ref, y_ref, s_ref, bb, nck):
    @pl.when(pl.program_id(1) == 0)
    def _():
        s_ref[...] = s0_ref[...]

    head, row, col = _head_masks(LANES)
    incl = col <= row
    bd = functools.partial(_block_diag, head=head)
    lane2 = lax.broadcasted_iota(jnp.int32, (2 * GLA_DV, LANES), 1)
    row2 = lax.broadcasted_iota(jnp.int32, (2 * GLA_DV, LANES), 0)
    same_head = (lane2 < GLA_DK) == (row2 < GLA_DV)
    kw, gw = GLA_KEY_WIDTH, GLA_WIDTH
    zero = jnp.zeros((CHUNK, LANES), F32)
    cat0 = lambda xs: jnp.concatenate(xs, axis=0)
    cat1 = lambda xs: jnp.concatenate(xs, axis=1)
    rows = lambda c: slice(c * CHUNK, (c + 1) * CHUNK)
    lanes = lambda pr: slice(pr * LANES, (pr + 1) * LANES)
    bs, cs = range(bb), range(nck)
    x = [p_ref[b] for b in bs]
    gate = [_mm1(x[b][:, 2 * kw + 2 * gw:], wgate_ref[...]) + bgate_ref[...] for b in bs]
    la = [(jnp.minimum(g, 0.0) - jnp.log(1.0 + jnp.exp(-jnp.abs(g)))) / GLA_TAU for g in gate]
    yield
    cum = [_mm_exact_lhs(tri_ref[...], la[b]) for b in bs]
    yield
    q_dec = [(x[b][:, :kw] * (GLA_DK ** -0.5)) * jnp.exp(cum[b]) for b in bs]
    k_inv = [x[b][:, kw:2 * kw] * jnp.exp(-cum[b]) for b in bs]
    last = [[cum[b][c * CHUNK + CHUNK - 1:(c + 1) * CHUNK, :] for c in cs] for b in bs]
    k_end = [[x[b][rows(c), kw:2 * kw] * jnp.exp(last[b][c] - cum[b][rows(c)]) for c in cs] for b in bs]
    dec = [[jnp.exp(last[b][c]) for c in cs] for b in bs]
    yield
    items = [(b, c, pr) for b in bs for c in cs for pr in range(GLA_PAIRS)]
    vp = {(b, c, pr): x[b][rows(c), 2 * kw + pr * 2 * GLA_DV:2 * kw + (pr + 1) * 2 * GLA_DV]
          for b, c, pr in items}
    att = {(b, c, pr): jnp.where(incl, _mm1(q_dec[b][rows(c), lanes(pr)],
                                            bd(k_inv[b][rows(c), lanes(pr)]), _NT), zero)
           for b, c, pr in items}
    yield
    upd = {it: _mm1(vp[it], k_end[it[0]][it[1]][:, lanes(it[2])], _TN) for it in items}
    yield
    intra = {it: _mm1(att[it], cat0([cat1([vp[it][:, :GLA_DV], zero]), cat1([zero, vp[it][:, GLA_DV:]])]))
             for it in items}
    yield
    st = {}
    for b in bs:
        for pr in range(GLA_PAIRS):
            cur = s_ref[b, pr]
            for c in cs:
                st[b, c, pr] = cur
                cur = cur * dec[b][c][:, lanes(pr)] + jnp.where(same_head, upd[b, c, pr], jnp.zeros_like(cur))
            s_ref[b, pr] = cur
    yield
    o = {(b, c, pr): intra[b, c, pr] + _mm1(q_dec[b][rows(c), lanes(pr)], st[b, c, pr], _NT)
         for b, c, pr in items}
    yield
    for b in bs:
        ob = cat0([cat1([o[b, c, pr] for pr in range(GLA_PAIRS)]) for c in cs])
        heads = [ob[:, h * GLA_DV:(h + 1) * GLA_DV] for h in range(GLA_HEADS)]
        normed = [oh * lax.rsqrt(jnp.mean(oh * oh, axis=-1, keepdims=True) + RMS_EPS) for oh in heads]
        y_ref[b] = cat1(normed) * ng_ref[...] * _silu(x[b][:, 2 * kw + gw:2 * kw + 2 * gw])


def _recurrent_kernel(*refs, bb, nck, n_cast):
    rwkv_in, refs = refs[:4], refs[4:]
    cast_in, refs = refs[:n_cast], refs[n_cast:]
    gla_in, refs = refs[:6], refs[6:]
    (y_ref, s_ref), refs = refs[:2], refs[2:]
    cast_out, (yg_ref, t_ref) = refs[:n_cast], refs[n_cast:]
    rwkv = _rwkv_stages(*rwkv_in, cast_in, y_ref, s_ref, cast_out, bb, nck)
    gla = _gla_stages(*gla_in, yg_ref, t_ref, bb, nck)
    done = object()
    live = [rwkv, rwkv, gla]
    while live:
        for gen in list(live):
            if gen in live and next(gen, done) is done:
                live = [g for g in live if g is not gen]


def _recurrent(ab, rkv, gl, s0, proj, lp, tri, t0, bb, nck, casts=()):
    bsz, seq, _ = ab.shape
    w = RWKV_WIDTH
    rows = nck * CHUNK
    n_seq_steps = seq // rows
    n_steps = (bsz // bb) * n_seq_steps
    tok = lambda a: pl.BlockSpec((bb, rows, a.shape[-1]), lambda i, c: (i, c, 0))
    full = lambda a: pl.BlockSpec(a.shape, lambda i, c: (0,) * a.ndim)
    st = pl.BlockSpec((bb, RWKV_TILES, RWKV_TILE_LANES, RWKV_TILE_LANES), lambda i, c: (i, 0, 0, 0))
    gst = pl.BlockSpec((bb, GLA_PAIRS, 2 * GLA_DV, LANES), lambda i, c: (i, 0, 0, 0))
    slab = lambda a: pl.BlockSpec((a.shape[0] // n_steps, a.shape[1]), lambda i, c: (i * n_seq_steps + c, 0))
    assert all(a.shape[0] % (16 * n_steps) == 0 for a in casts)
    out_tok = pl.BlockSpec((bb, rows, w), lambda i, c: (i, c, 0))
    return pl.pallas_call(
        functools.partial(_recurrent_kernel, bb=bb, nck=nck, n_cast=len(casts)),
        grid=(bsz // bb, n_seq_steps),
        in_specs=[tok(ab), tok(rkv), pl.BlockSpec((bb, nck, 1, w), lambda i, c: (i, c, 0, 0)), st]
        + [slab(a) for a in casts]
        + [pl.BlockSpec((bb, rows, GLA_PROJ_PAD), lambda i, c: (i, c, 1)),
           full(lp["wgate"]), full(lp["bgate"]), full(tri), full(lp["gla_g"]), gst],
        out_specs=[out_tok, st] + [slab(a) for a in casts] + [out_tok, gst],
        out_shape=[jax.ShapeDtypeStruct((bsz, seq, w), F32),
                   jax.ShapeDtypeStruct((bsz, RWKV_TILES, RWKV_TILE_LANES, RWKV_TILE_LANES), F32)]
        + [jax.ShapeDtypeStruct(a.shape, BF16) for a in casts]
        + [jax.ShapeDtypeStruct((bsz, seq, GLA_WIDTH), F32),
           jax.ShapeDtypeStruct((bsz, GLA_PAIRS, 2 * GLA_DV, LANES), F32)],
        compiler_params=_params("parallel", "arbitrary"),
        name="rwkv_gla_chunk",
    )(ab, rkv, gl, s0, *casts, proj, lp["wgate"], lp["bgate"], tri, lp["gla_g"], t0)


def _route(lgt):
    tm = lgt.shape[1]
    n_pad = ROUTE_ROWS - N_EXPERTS
    rowg = lax.broadcasted_iota(jnp.int32, (n_pad, tm), 0)
    rowe = lax.broadcasted_iota(jnp.int32, (N_EXPERTS, tm), 0)
    rowg_f, rowe_f = rowg.astype(F32), rowe.astype(F32)

    def first_argmax(vals, mx, rows_f):
        return jnp.min(jnp.where(vals == mx, rows_f, float(ROUTE_ROWS)), axis=0, keepdims=True)

    is_group = rowg < N_GROUPS
    lg = jnp.where(is_group, lgt[N_EXPERTS:], -jnp.inf)
    gmax = jnp.max(lg, axis=0, keepdims=True)
    gi = first_argmax(lg, gmax, rowg_f)
    pg_top = 1.0 / jnp.sum(jnp.where(is_group, jnp.exp(lg - gmax), 0.0), axis=0, keepdims=True)
    in_group = (rowe >> GROUP_SHIFT).astype(F32) == gi
    le = jnp.where(in_group, lgt[:N_EXPERTS], -jnp.inf)
    m1 = jnp.max(le, axis=0, keepdims=True)
    i1 = first_argmax(le, m1, rowe_f)
    le2 = jnp.where(rowe_f == i1, -jnp.inf, le)
    m2 = jnp.max(le2, axis=0, keepdims=True)
    i2 = first_argmax(le2, m2, rowe_f)
    p2 = jnp.exp(m2 - m1)
    w1 = pg_top / (1.0 + p2)
    w2 = pg_top * p2 / (1.0 + p2)
    comb = jnp.where(rowe_f == i1, w1, 0.0) + jnp.where(rowe_f == i2, w2, 0.0)
    return jnp.concatenate([comb, jnp.where(rowg == 0, gi, 0.0)], axis=0)


def _outproj_kernel(yr_ref, bg_ref, yg_ref, x_ref, gt_ref, sc_ref, sh_ref,
                    lnw_ref, lnb_ref, seg_ref, wout_ref, g2_ref, wr_ref, br_ref,
                    x1_ref, h2_ref, route_ref):
    nb, rb, d_model = x_ref.shape
    tm = nb * rb
    flat = lambda ref: ref[...].reshape(tm, ref.shape[-1])
    seg = seg_ref[...]
    y = flat(yr_ref)
    inv_n = 1.0 / RWKV_HEAD
    d = y - _mm2_exact_rhs(y, seg) * inv_n
    var = _mm2_exact_rhs(d * d, seg) * inv_n
    yn = d * lax.rsqrt(var + LNX_EPS) * lnw_ref[...] + lnb_ref[...]
    bg = flat(bg_ref)
    yr = (yn + bg[:, :RWKV_WIDTH]) * bg[:, RWKV_WIDTH:]
    mix = jnp.concatenate([yr, flat(yg_ref)], axis=1)
    x1 = x_ref[...] + gt_ref[...] * _mm1(mix, wout_ref[...]).reshape(nb, rb, d_model)
    x1_ref[...] = x1
    h2 = _rms_mod(x1, g2_ref[...], sc_ref[...], sh_ref[...])
    h2_ref[...] = h2.astype(BF16)
    route_ref[...] = _route(_mm3(wr_ref[...], h2.reshape(tm, d_model), _NT) + br_ref[...])


def _out_proj(yr, bg, yg, x, gt, sc, sh, lp, consts, rows):
    bsz, seq, d = x.shape
    nb, rb = _token_tile(bsz, seq, rows)
    n_seq_tiles = seq // rb
    half = pl.BlockSpec((nb, rb, RWKV_WIDTH), lambda b, i: (b, i, 0))
    tokd = pl.BlockSpec((nb, rb, d), lambda b, i: (b, i, 0))
    vec = pl.BlockSpec((nb, 1, d), lambda b, i: (b, 0, 0))
    full = lambda a: pl.BlockSpec(a.shape, lambda b, i: (0,) * a.ndim)
    args = (lp["lnx_w"], lp["lnx_b"], consts["seg"], lp["w_out"], lp["norm2_g"], lp["w_router"], lp["b_router"])
    return pl.pallas_call(
        _outproj_kernel,
        grid=(bsz // nb, seq // rb),
        in_specs=[half, pl.BlockSpec((nb, rb, 2 * RWKV_WIDTH), lambda b, i: (b, i, 0)), half, tokd,
                  vec, vec, vec] + [full(a) for a in args],
        out_specs=[tokd, tokd, pl.BlockSpec((ROUTE_ROWS, nb * rb), lambda b, i: (0, b * n_seq_tiles + i))],
        out_shape=[jax.ShapeDtypeStruct((bsz, seq, d), F32),
                   jax.ShapeDtypeStruct((bsz, seq, d), BF16),
                   jax.ShapeDtypeStruct((ROUTE_ROWS, bsz * seq), F32)],
        compiler_params=_params("parallel", "parallel"),
        name="out_proj_router",
    )(yr, bg, yg, x, gt, sc, sh, *args)


def _moe_kernel(h_ref, route_ref, x1_ref, gt_ref, sc_ref, sh_ref, gf_ref, tri_ref, wg_ref, wu_ref, wd_ref,
                y_ref, ys_ref, pos_ref, meta_ref):
    g = pl.program_id(2)
    nb, rb, d = h_ref.shape
    tm = nb * rb
    sub, tail = MOE_SUB_ROWS, MOE_SUB_ROWS // 2
    cap = ys_ref.shape[0]
    n_lane_tiles = tm // LANES
    lane_tile = lambda k: slice(k * LANES, (k + 1) * LANES)

    @pl.when(g == 0)
    def _sort():
        row8 = lax.broadcasted_iota(jnp.int32, (8, LANES), 0).astype(F32)
        carry = jnp.zeros((8, 1), F32)
        members, ranks = [], []
        for k in range(n_lane_tiles):
            blk = jnp.where(row8 == route_ref[N_EXPERTS:N_EXPERTS + 1, lane_tile(k)], 1.0, 0.0)
            members.append(blk)
            ranks.append(_dg(blk.astype(BF16), tri_ref[...]) - blk + carry)
            carry = carry + jnp.sum(blk, axis=1, keepdims=True)
        rowc = lax.broadcasted_iota(jnp.int32, (8, 1), 0)
        lane = lax.broadcasted_iota(jnp.int32, (1, LANES), 1)
        first = jnp.zeros((1, 1), F32)
        off_col = jnp.zeros((8, 1), F32)
        meta = jnp.zeros((1, LANES), F32)
        for grp in range(N_GROUPS):
            count = jnp.sum(jnp.where(rowc == grp, carry, 0.0), axis=0, keepdims=True)
            blocks = jnp.ceil(count * (1.0 / tail))
            pairs = jnp.floor(blocks * 0.5)
            odd = blocks - 2.0 * pairs
            triple = odd * jnp.where(blocks >= 3.0, 1.0, 0.0)
            n_full = pairs - triple
            off_col = off_col + jnp.where(rowc == grp, first, 0.0)
            meta = (meta + jnp.where(lane == grp, first, 0.0) + jnp.where(lane == N_GROUPS + grp, n_full, 0.0)
                    + jnp.where(lane == 2 * N_GROUPS + grp, odd + 2.0 * triple, 0.0))
            first = first + blocks * tail
        meta = (meta + jnp.where(lane == 3 * N_GROUPS, first, 0.0)).astype(jnp.int32)
        for i in range(3 * N_GROUPS + 1):
            meta_ref[i] = meta[0, i]
        for k in range(n_lane_tiles):
            pos = jnp.sum(members[k] * (ranks[k] + off_col), axis=0, keepdims=True)
            pos_ref[:, lane_tile(k)] = pos.astype(jnp.int32)
        ys_ref[...] = jnp.zeros_like(ys_ref)

    def experts(base, n_rows):
        prow = lax.broadcasted_iota(jnp.int32, (n_rows, tm), 0) + base
        onehot = jnp.where(prow == pos_ref[...], 1.0, 0.0).astype(BF16)
        hs = _dg(onehot, h_ref[...].reshape(tm, d)).astype(BF16)
        comb3 = _split3(route_ref[:N_EXPERTS, :])
        cs = _dg(onehot, comb3[0], _NT) + (_dg(onehot, comb3[1], _NT) + _dg(onehot, comb3[2], _NT))
        lane = lax.broadcasted_iota(jnp.int32, cs.shape, 1)
        ys = jnp.zeros((n_rows, d), F32)
        for e in range(EXPERTS_PER_GROUP):
            hid = _silu(_dg(hs, wg_ref[e])) * _dg(hs, wu_ref[e])
            ce = jnp.sum(jnp.where(lane == g * EXPERTS_PER_GROUP + e, cs, 0.0), axis=-1, keepdims=True)
            ys = ys + ce * _dg(hid.astype(BF16), wd_ref[e])
        ys_ref[pl.ds(base, n_rows), :] = ys.astype(BF16)

    first_row = meta_ref[g]
    n_full = meta_ref[N_GROUPS + g]

    def full_sub_tile(j, carry_):
        experts(pl.multiple_of(first_row + j * sub, tail), sub)
        return carry_

    lax.fori_loop(0, n_full, full_sub_tile, 0)

    for last_blocks in (1, 3):
        @pl.when(meta_ref[2 * N_GROUPS + g] == last_blocks)
        def _last(last_blocks=last_blocks):
            experts(pl.multiple_of(first_row + n_full * sub, tail), last_blocks * tail)

    def unsort(n_sorted):
        ys_all = ys_ref[:n_sorted]
        prow = lax.broadcasted_iota(jnp.int32, (n_sorted, LANES), 0)
        seqs = max(LANES // rb, 1)
        for k in range(n_lane_tiles):
            onehot = jnp.where(prow == pos_ref[:, lane_tile(k)], 1.0, 0.0).astype(BF16)
            moe = _dg(onehot, ys_all, _TN).reshape(seqs, LANES // seqs, d)
            b0 = k * LANES // rb
            r0 = k * LANES - b0 * rb
            bs, rs = slice(b0, b0 + seqs), slice(r0, r0 + LANES // seqs)
            x2 = x1_ref[bs, rs] + gt_ref[bs] * moe
            y_ref[bs, rs] = _rms_mod(x2, gf_ref[...], sc_ref[bs], sh_ref[bs])

    usual = tm + (cap - tm) // 2
    last = g == N_GROUPS - 1
    pl.when(last & (meta_ref[3 * N_GROUPS] <= usual))(lambda: unsort(usual))
    pl.when(last & (meta_ref[3 * N_GROUPS] > usual))(lambda: unsort(cap))


def _moe(h2, route, x1, gt, sc, sh, gf, wg, wu, wd, tri, rows):
    bsz, seq, d = x1.shape
    nb, rb = _token_tile(bsz, seq, rows)
    tm = nb * rb
    assert tm % LANES == 0 and (rb % LANES == 0 or LANES % rb == 0)
    n_seq_tiles = seq // rb
    cap = tm + N_GROUPS * (MOE_SUB_ROWS // 2)
    tokd = pl.BlockSpec((nb, rb, d), lambda b, i, g: (b, i, 0))
    vec = pl.BlockSpec((nb, 1, d), lambda b, i, g: (b, 0, 0))
    group_w = lambda shape: pl.BlockSpec((EXPERTS_PER_GROUP,) + shape, lambda b, i, g: (g, 0, 0))
    return pl.pallas_call(
        _moe_kernel,
        grid=(bsz // nb, n_seq_tiles, N_GROUPS),
        in_specs=[tokd, pl.BlockSpec((ROUTE_ROWS, tm), lambda b, i, g: (0, b * n_seq_tiles + i)),
                  tokd, vec, vec, vec,
                  pl.BlockSpec((1, d), lambda b, i, g: (0, 0)),
                  pl.BlockSpec(tri.shape, lambda b, i, g: (0, 0)),
                  group_w((d, D_EXPERT)), group_w((d, D_EXPERT)), group_w((D_EXPERT, d))],
        out_specs=tokd,
        out_shape=jax.ShapeDtypeStruct((bsz, seq, d), F32),
        scratch_shapes=[pltpu.VMEM((cap, d), BF16), pltpu.VMEM((1, tm), jnp.int32),
                        pltpu.SMEM((3 * N_GROUPS + 1,), jnp.int32)],
        compiler_params=_params("parallel", "parallel", "arbitrary"),
        name="moe_final_norm",
    )(h2, route, x1, gt, sc, sh, gf, tri, wg, wu, wd)


def _block_ones(n, blk, lower):
    i = np.arange(n)
    m = (i[:, None] // blk) == (i[None, :] // blk)
    if lower:
        m = m & (i[None, :] <= i[:, None])
    return jnp.asarray(m, dtype=BF16)


def _consts(tr, tg):
    return dict(seg=_block_ones(RWKV_WIDTH, RWKV_HEAD, False), tri=_block_ones(tr, CHUNK, True),
                tri_gla=_block_ones(tg, CHUNK, True), tri_up=_block_ones(LANES, LANES, True).T)


def _heads_block_diag(s):
    n, c = s.shape[2], s.shape[-1]
    rows = [jnp.pad(s[:, :, j], ((0, 0), (0, 0), (0, 0), (j * c, (n - 1 - j) * c))) for j in range(n)]
    return jnp.concatenate(rows, axis=2)


def _diag_blocks(t, n):
    r, c = t.shape[-2] // n, t.shape[-1] // n
    return jnp.stack([t[:, :, j * r:(j + 1) * r, j * c:(j + 1) * c] for j in range(n)], axis=2)


def _pad_rows(w, first_row):
    out = jnp.zeros((LORA_PAD, w.shape[1]), F32)
    return lax.dynamic_update_slice(out, w, (first_row, 0)).astype(BF16)


def _layer_params(l, w_in, mu_shift, w0, w_decay_up, a0, w_a_up, w_g_up, k_k, k_a, r_k, lnx_w, lnx_b,
                  w_gla_gate_up, b_gla_gate, gla_norm_g, w_out, norm2_g,
                  w_router_group, b_router_group, w_router_expert, b_router_expert):
    w_in_p = _win_layout(w_in[l].T, WEIGHT_TILE_COLS)
    n_pad = ROUTE_ROWS - N_EXPERTS - N_GROUPS
    w_router = jnp.concatenate([w_router_expert[l].T, w_router_group[l].T, jnp.zeros((n_pad, D_MODEL), F32)])
    b_router = jnp.concatenate([b_router_expert[l], b_router_group[l],
                                jnp.zeros((n_pad,), F32)]).reshape(ROUTE_ROWS, 1)
    wgate = jnp.zeros((LANES, GLA_KEY_WIDTH), F32).at[:GLA_GATE_RANK].set(w_gla_gate_up[l]).astype(BF16)
    r1 = lambda a: a.reshape(1, -1)
    return dict(
        w_in=w_in_p, mu=r1(mu_shift[l]), w0=r1(w0[l]), a0=r1(a0[l]), k_k=r1(k_k[l]), k_a=r1(k_a[l]),
        r_k=r1(r_k[l]), wd=_pad_rows(w_decay_up[l], 0), wa=_pad_rows(w_a_up[l], DECAY_LORA),
        wg=_pad_rows(w_g_up[l], DECAY_LORA + AAA_LORA), lnx_w=r1(lnx_w[l]), lnx_b=r1(lnx_b[l]),
        wgate=wgate, bgate=r1(b_gla_gate[l]), gla_g=r1(gla_norm_g[l]),
        w_out=w_out[l].astype(BF16), norm2_g=r1(norm2_g[l]), w_router=w_router, b_router=b_router)


def _run_layer(x, mod, shift0, wkv0, gla0, lp, experts, final):
    bsz, seq, d = x.shape
    assert seq % CHUNK == 0
    bb = 2 if seq >= RECURRENT_STEP_ROWS else min(bsz, RECURRENT_MAX_SEQS)
    assert bsz % bb == 0
    nck = min(seq, RECURRENT_STEP_ROWS) // CHUNK
    consts = _consts(TOKEN_TILE_ROWS, nck * CHUNK)
    m = lambda j: mod[:, j:j + 1, :]
    sh1, sc1, gt1, sh2, sc2, gt2 = (m(j) for j in range(6))
    proj, tails = _in_proj(x, sc1, sh1, lp["norm1_g"], lp["w_in"], TOKEN_TILE_ROWS)
    tails = tails.reshape(bsz, -1, RWKV_PROJ)
    new_shift = tails[:, -1]
    prev = jnp.concatenate([shift0[:, None, :], tails[:, :-1]], axis=1).reshape(-1, 1, RWKV_PROJ)
    ab, rkv, bg, gl = _rwkv_prep(proj, prev, lp, consts, TOKEN_TILE_ROWS)
    gl = gl.reshape(bsz, seq // CHUNK, RWKV_WIDTH)
    s0 = _heads_block_diag(wkv0.reshape(bsz, RWKV_TILES, RWKV_TILE_HEADS, RWKV_HEAD, RWKV_HEAD))
    t0 = _heads_block_diag(jnp.swapaxes(gla0, -1, -2).reshape(bsz, GLA_PAIRS, 2, GLA_DV, GLA_DK))
    casts = tuple(e.reshape(-1, e.shape[-1]) for e in experts if e.dtype != BF16)
    yr, s_bd, *cast, yg, t_bd = _recurrent(ab, rkv, gl[:, :, None, :], s0, proj, lp, consts["tri_gla"], t0,
                                           bb, nck, casts)
    if cast:
        experts = tuple(c.reshape(e.shape) for c, e in zip(cast, experts))
    new_wkv = _diag_blocks(s_bd, RWKV_TILE_HEADS).reshape(bsz, RWKV_HEADS, RWKV_HEAD, RWKV_HEAD)
    new_gla = jnp.swapaxes(_diag_blocks(t_bd, 2).reshape(bsz, GLA_HEADS, GLA_DV, GLA_DK), -1, -2)
    x1, h2, route = _out_proj(yr, bg, yg, x, gt1, sc2, sh2, lp, consts, OUT_PROJ_TILE_ROWS)
    out = _moe(h2, route, x1, gt2, *final, *experts, consts["tri_up"], MOE_TILE_ROWS)
    return experts, out, new_shift, new_wkv, new_gla


def kernel(x_prompt, x_sample, c_prompt, c_sample, state_rwkv_shift, state_rwkv_wkv, state_gla_kv, w_ada, b_ada, norm1_g, norm2_g, w_in, mu_shift, w0, w_decay_up, a0, w_a_up, w_g_up, k_k, k_a, r_k, lnx_w, lnx_b, w_gla_gate_up, b_gla_gate, gla_norm_g, w_out, w_router_group, b_router_group, w_router_expert, b_router_expert, w_expert_gate, w_expert_up, w_expert_down, w_ada_final, b_ada_final, normf_g):
    assert w_ada.shape[0] == 1, "the final norm is fused into the single layer's MoE kernel"
    bp, bs = x_prompt.shape[0], x_sample.shape[0]
    d = D_MODEL
    n_rows = -(-(bp + bs) // 8) * 8
    c_all = jnp.zeros((n_rows, d), F32).at[:bp].set(c_prompt).at[bp:bp + bs].set(c_sample)
    modf = _modulation(c_all, w_ada_final, b_ada_final, MODULATION_TILE_COLS).reshape(n_rows, 2, d)
    mod = _modulation(c_all, w_ada, b_ada[0], MODULATION_TILE_COLS, layer=0).reshape(n_rows, 6, d)
    lp = _layer_params(0, w_in, mu_shift, w0, w_decay_up, a0, w_a_up, w_g_up, k_k, k_a, r_k, lnx_w,
                       lnx_b, w_gla_gate_up, b_gla_gate, gla_norm_g, w_out, norm2_g,
                       w_router_group, b_router_group, w_router_expert, b_router_expert)
    lp["norm1_g"] = norm1_g[0].reshape(1, d)
    experts = (w_expert_gate[0], w_expert_up[0], w_expert_down[0])
    groups = [
        (x_prompt, 0, bp, jnp.zeros((bp, RWKV_PROJ), F32),
         jnp.zeros((bp, RWKV_HEADS, RWKV_HEAD, RWKV_HEAD), F32), jnp.zeros((bp, GLA_HEADS, GLA_DK, GLA_DV), F32)),
        (x_sample, bp, bp + bs, state_rwkv_shift[0], state_rwkv_wkv[0], state_gla_kv[0]),
    ]
    ys, states = [], []
    for x, lo, hi, shift0, wkv0, gla0 in groups:
        final = (modf[lo:hi, 1:2], modf[lo:hi, 0:1], normf_g.reshape(1, d))
        experts, y, *st = _run_layer(x, mod[lo:hi], shift0, wkv0, gla0, lp, experts, final)
        ys.append(y)
        states.extend(s[None] for s in st)
    return tuple(ys + states)
```

```python
import functools

import jax
import jax.numpy as jnp
import numpy as np
from jax import lax
from jax.experimental import pallas as pl
from jax.experimental.pallas import tpu as pltpu

F32 = jnp.float32
BF16 = jnp.bfloat16

LANES = 128
VMEM_LIMIT_BYTES = 56 * 1024 * 1024
TOKEN_TILE_ROWS = 512
PROJ_TILE_ROWS = 1024
MOE_TILE_ROWS = 1024
MOE_SUB_ROWS = 256
RECURRENT_STEP_ROWS = 256
RECURRENT_MAX_SEQS = 8
MODULATION_TILE_COLS = 1024
WEIGHT_TILE_COLS = 256

D_MODEL = 1024
CHUNK = 64
RWKV_WIDTH = 512
RWKV_HEAD = 64
RWKV_HEADS = RWKV_WIDTH // RWKV_HEAD
HEAD_LANES = RWKV_HEAD
HEAD_SHIFT = HEAD_LANES.bit_length() - 1
assert 1 << HEAD_SHIFT == HEAD_LANES
RWKV_TILE_LANES = 128
RWKV_TILE_HEADS = RWKV_TILE_LANES // RWKV_HEAD
RWKV_TILES = RWKV_WIDTH // RWKV_TILE_LANES
DECAY_LORA = 32
AAA_LORA = 32
GATE_LORA = 64
LORA_PAD = DECAY_LORA + AAA_LORA + GATE_LORA
RWKV_PROJ = 3 * RWKV_WIDTH + LORA_PAD
GLA_WIDTH = 512
GLA_HEADS = 4
GLA_PAIRS = GLA_HEADS // 2
GLA_DV = GLA_WIDTH // GLA_HEADS
GLA_DK = GLA_DV // 2
GLA_KEY_WIDTH = GLA_HEADS * GLA_DK
GLA_GATE_RANK = 16
GLA_TAU = 16.0
GLA_PROJ = 2 * GLA_KEY_WIDTH + 2 * GLA_WIDTH + GLA_GATE_RANK
GLA_PROJ_PAD = RWKV_PROJ
IN_PROJ_PAD = RWKV_PROJ + GLA_PROJ_PAD
N_GROUPS = 4
EXPERTS_PER_GROUP = 4
N_EXPERTS = N_GROUPS * EXPERTS_PER_GROUP
GROUP_SHIFT = EXPERTS_PER_GROUP.bit_length() - 1
assert 1 << GROUP_SHIFT == EXPERTS_PER_GROUP
ROUTE_ROWS = 24
D_EXPERT = 512
RMS_EPS = 1e-6
LNX_EPS = 64e-5
LOG2_E = 1.4426950408889634

_NN = (((1,), (0,)), ((), ()))
_NT = (((1,), (1,)), ((), ()))
_TN = (((0,), (0,)), ((), ()))


def _dg(a, b, dims=_NN):
    return lax.dot_general(a, b, dims, preferred_element_type=F32)


def _split2(x):
    hi = x.astype(BF16)
    lo = (x - hi.astype(F32)).astype(BF16)
    return hi, lo


def _split3(x):
    hi = x.astype(BF16)
    r1 = x - hi.astype(F32)
    mid = r1.astype(BF16)
    lo = (r1 - mid.astype(F32)).astype(BF16)
    return hi, mid, lo


def _mm1(a, b, dims=_NN):
    return _dg(a.astype(BF16), b.astype(BF16), dims)


def _mm3(a, b, dims=_NN):
    ah, al = _split2(a)
    bh, bl = _split2(b)
    return _dg(ah, bh, dims) + (_dg(ah, bl, dims) + _dg(al, bh, dims))


def _mm_exact_lhs(e, x, dims=_NN):
    h, m, l = _split3(x)
    return _dg(e, h, dims) + (_dg(e, m, dims) + _dg(e, l, dims))


def _mm2_exact_rhs(x, e, dims=_NN):
    h, l = _split2(x)
    return _dg(h, e, dims) + _dg(l, e, dims)


def _softplus(z):
    return jnp.maximum(z, 0.0) + jnp.log(1.0 + jnp.exp(-jnp.abs(z)))


def _sigmoid(z):
    return 1.0 / (1.0 + jnp.exp(-z))


def _silu(z):
    return z * _sigmoid(z)


def _params(*sem):
    return pltpu.CompilerParams(dimension_semantics=sem, vmem_limit_bytes=VMEM_LIMIT_BYTES)


def _mod_kernel(c_ref, w_ref, b_ref, o_ref):
    o_ref[...] = _mm1(_silu(c_ref[...]), w_ref[...]) + b_ref[...]


def _modulation(c, w, b, tn, layer=None):
    rows, d = c.shape
    n = w.shape[-1]
    if layer is None:
        w_spec = pl.BlockSpec((d, tn), lambda j: (0, j))
    else:
        w_spec = pl.BlockSpec((None, d, tn), lambda j: (layer, 0, j))
    return pl.pallas_call(
        _mod_kernel,
        grid=(n // tn,),
        in_specs=[pl.BlockSpec((rows, d), lambda j: (0, 0)), w_spec,
                  pl.BlockSpec((1, tn), lambda j: (0, j))],
        out_specs=pl.BlockSpec((rows, tn), lambda j: (0, j)),
        out_shape=jax.ShapeDtypeStruct((rows, n), F32),
        compiler_params=_params("parallel"),
        name="modulation",
    )(c, w, b.reshape(1, n))


def _win_layout_kernel(w_ref, o_ref):
    qkv_end = RWKV_PROJ + 2 * GLA_KEY_WIDTH + GLA_WIDTH
    gate_end = qkv_end + GLA_GATE_RANK
    o_ref[:qkv_end] = w_ref[:qkv_end].astype(BF16)
    o_ref[qkv_end:qkv_end + GLA_WIDTH] = w_ref[gate_end:gate_end + GLA_WIDTH].astype(BF16)
    o_ref[qkv_end + GLA_WIDTH:RWKV_PROJ + GLA_PROJ] = w_ref[qkv_end:gate_end].astype(BF16)
    o_ref[RWKV_PROJ + GLA_PROJ:] = jnp.zeros((IN_PROJ_PAD - RWKV_PROJ - GLA_PROJ, o_ref.shape[1]), BF16)


def _win_layout(wt, cols):
    n, d = wt.shape
    return pl.pallas_call(
        _win_layout_kernel,
        grid=(d // cols,),
        in_specs=[pl.BlockSpec((n, cols), lambda i: (0, i))],
        out_specs=pl.BlockSpec((IN_PROJ_PAD, cols), lambda i: (0, i)),
        out_shape=jax.ShapeDtypeStruct((IN_PROJ_PAD, d), BF16),
        compiler_params=_params("parallel"),
        name="w_in_layout",
    )(wt)


def _rms_mod(x, g, sc, sh):
    ms = jnp.mean(x * x, axis=-1, keepdims=True)
    return (x * lax.rsqrt(ms + RMS_EPS) * g) * (1.0 + sc) + sh


def _token_tile(bsz, seq, rows):
    if seq >= rows:
        assert seq % rows == 0
        return 1, rows
    nb = min(bsz, rows // seq)
    assert bsz % nb == 0
    return nb, seq


def _inproj_kernel(x_ref, sc_ref, sh_ref, g_ref, w_ref, o_ref, last_ref, *, n_step, n_slices):
    nb, rb, d = x_ref.shape
    h = _rms_mod(x_ref[...], g_ref[...], sc_ref[...], sh_ref[...])
    hb = h.reshape(nb * rb, d).astype(BF16)
    for j in range(IN_PROJ_PAD // n_step):
        cols = slice(j * n_step, (j + 1) * n_step)
        o_ref[:, :, cols] = _dg(hb, w_ref[cols, :], _NT).reshape(nb, rb, n_step)
    rows = rb // n_slices
    for j in range(n_slices):
        last_ref[j * nb:(j + 1) * nb] = o_ref[:, (j + 1) * rows - 1:(j + 1) * rows, :RWKV_PROJ]


def _in_proj(x, sc, sh, g, w, rows, slice_rows):
    bsz, seq, d = x.shape
    nb, rb = _token_tile(bsz, seq, rows)
    n_seq_tiles = seq // rb
    n_slices = max(rb // slice_rows, 1)
    assert nb == 1 or n_slices == 1
    vec = pl.BlockSpec((nb, 1, d), lambda b, i: (b, 0, 0))
    return pl.pallas_call(
        functools.partial(_inproj_kernel, n_step=2 * LANES, n_slices=n_slices),
        grid=(bsz // nb, n_seq_tiles),
        in_specs=[pl.BlockSpec((nb, rb, d), lambda b, i: (b, i, 0)), vec, vec,
                  pl.BlockSpec((1, d), lambda b, i: (0, 0)),
                  pl.BlockSpec((IN_PROJ_PAD, d), lambda b, i: (0, 0))],
        out_specs=[pl.BlockSpec((nb, rb, IN_PROJ_PAD), lambda b, i: (b, i, 0)),
                   pl.BlockSpec((nb * n_slices, 1, RWKV_PROJ), lambda b, i: (b * n_seq_tiles + i, 0, 0))],
        out_shape=[jax.ShapeDtypeStruct((bsz, seq, IN_PROJ_PAD), F32),
                   jax.ShapeDtypeStruct((bsz * n_seq_tiles * n_slices, 1, RWKV_PROJ), F32)],
        compiler_params=_params("parallel", "parallel"),
        name="norm1_in_proj",
    )(x, sc, sh, g, w)


def _rwkv_prep_kernel(p_ref, prev_ref, mu_ref, w0_ref, a0_ref, kk_ref, ka_ref, rk_ref,
                      wd_ref, wa_ref, wg_ref, seg_ref, tri_ref,
                      ab_ref, rkv_ref, bg_ref, gl_ref):
    nb, rb, wp = p_ref.shape
    tr = nb * rb
    p = p_ref[...].reshape(tr, wp)
    row = lax.broadcasted_iota(jnp.int32, (nb, rb, wp), 1)
    xx = jnp.where(row == 0, prev_ref[...], pltpu.roll(p, 1, 0).reshape(nb, rb, wp)).reshape(tr, wp)
    ps = p + (xx - p) * mu_ref[...]
    w = RWKV_WIDTH
    r, k, v, lora = ps[:, :w], ps[:, w:2 * w], ps[:, 2 * w:3 * w], ps[:, 3 * w:]
    logw = -_softplus(-(w0_ref[...] + _mm1(jnp.tanh(lora), wd_ref[...]))) - 0.5
    lw = jnp.exp(logw) * (-LOG2_E)
    a = _sigmoid(a0_ref[...] + _mm1(lora, wa_ref[...]))
    g = _mm1(_sigmoid(lora), wg_ref[...])
    seg = seg_ref[...]
    kk = k * kk_ref[...]
    kk = kk / jnp.maximum(jnp.sqrt(_mm2_exact_rhs(kk * kk, seg)), 1e-12)
    k2 = k * (1.0 + (a - 1.0) * ka_ref[...])
    cum = _mm_exact_lhs(tri_ref[...], lw)
    lasts = [jnp.exp2(cum[c * CHUNK + CHUNK - 1:(c + 1) * CHUNK, :]) for c in range(tr // CHUNK)]
    for c, l in enumerate(lasts):
        gl_ref[c:c + 1, :] = l

    def to_chunk_end(val):
        return jnp.concatenate([val[c * CHUNK:(c + 1) * CHUNK] * l for c, l in enumerate(lasts)], axis=0)

    gamma = jnp.exp2(cum)
    ginv = 1.0 / gamma
    bt = (kk * a) * ginv
    kt = k2 * ginv

    def put(ref, slot, val):
        ref[:, :, slot * w:(slot + 1) * w] = val.reshape(nb, rb, w).astype(ref.dtype)

    put(ab_ref, 0, -kk * jnp.exp2(cum - lw))
    put(ab_ref, 1, bt)
    put(rkv_ref, 0, r * gamma)
    put(rkv_ref, 1, kt)
    put(rkv_ref, 2, to_chunk_end(bt))
    put(rkv_ref, 3, to_chunk_end(kt))
    put(rkv_ref, 4, v)
    put(bg_ref, 0, _mm2_exact_rhs(r * k2 * rk_ref[...], seg) * v)
    put(bg_ref, 1, g)


def _rwkv_prep(proj, prev, lp, consts, rows):
    bsz, seq, _ = proj.shape
    w = RWKV_WIDTH
    nb, rb = _token_tile(bsz, seq, rows)
    n_seq_tiles = seq // rb
    flat = lambda b, i: (b * n_seq_tiles + i, 0)
    row = lambda n: pl.BlockSpec((1, n), lambda b, i: (0, 0))
    full = lambda a: pl.BlockSpec(a.shape, lambda b, i: (0,) * a.ndim)
    packed = [(2, F32), (5, BF16), (2, F32)]
    return pl.pallas_call(
        _rwkv_prep_kernel,
        grid=(bsz // nb, n_seq_tiles),
        in_specs=[pl.BlockSpec((nb, rb, RWKV_PROJ), lambda b, i: (b, i, 0)),
                  pl.BlockSpec((nb, 1, RWKV_PROJ), lambda b, i: flat(b, i) + (0,)),
                  row(RWKV_PROJ), row(w), row(w), row(w), row(w), row(w),
                  full(lp["wd"]), full(lp["wa"]), full(lp["wg"]), full(consts["seg"]), full(consts["tri"])],
        out_specs=[pl.BlockSpec((nb, rb, k * w), lambda b, i: (b, i, 0)) for k, _ in packed]
        + [pl.BlockSpec((nb * rb // CHUNK, w), flat)],
        out_shape=[jax.ShapeDtypeStruct((bsz, seq, k * w), dt) for k, dt in packed]
        + [jax.ShapeDtypeStruct((bsz * seq // CHUNK, w), F32)],
        compiler_params=_params("parallel", "parallel"),
        name="rwkv_prep",
    )(proj, prev, lp["mu"], lp["w0"], lp["a0"], lp["k_k"], lp["k_a"], lp["r_k"],
      lp["wd"], lp["wa"], lp["wg"], consts["seg"], consts["tri"])


def _head_masks(width):
    lane = lax.broadcasted_iota(jnp.int32, (CHUNK, width), 1)
    row = lax.broadcasted_iota(jnp.int32, (CHUNK, width), 0)
    return lane >> HEAD_SHIFT, row, lane & (HEAD_LANES - 1)


def _block_diag(x, head):
    z = jnp.zeros_like(x)
    return jnp.concatenate([jnp.where(head == j, x, z) for j in range(x.shape[1] // HEAD_LANES)], axis=0)


def _rwkv_stages(ab_ref, rkv_ref, gl_ref, s0_ref, cast_in, y_ref, s_ref, cast_out, bb, nck):
    @pl.when(pl.program_id(1) == 0)
    def _():
        s_ref[...] = s0_ref[...]

    for src, dst in zip(cast_in, cast_out):
        dst[...] = src[...].astype(BF16)

    tw = RWKV_TILE_LANES
    head, row, col = _head_masks(tw)
    strict = col < row
    incl = col <= row
    same8 = (col >> 3) == (row >> 3)
    lane2 = lax.broadcasted_iota(jnp.int32, (tw, tw), 1)
    row2 = lax.broadcasted_iota(jnp.int32, (tw, tw), 0)
    same_head = (lane2 >> HEAD_SHIFT) == (row2 >> HEAD_SHIFT)
    bd = functools.partial(_block_diag, head=head)
    c = CHUNK

    def pmm(p, q):
        return _dg(p.astype(BF16), bd(q.astype(BF16)))

    items = [(b, ck, pr) for b in range(bb) for ck in range(nck) for pr in range(RWKV_TILES)]
    n = range(len(items))
    rows = lambda ck: slice(ck * c, (ck + 1) * c)
    lanes = lambda pr: slice(pr * tw, (pr + 1) * tw)
    w = RWKV_WIDTH

    def ld(ref, slot):
        return [ref[b, rows(ck), slot * w + pr * tw:slot * w + (pr + 1) * tw] for b, ck, pr in items]

    cat0 = lambda *xs: jnp.concatenate(xs, axis=0)
    cat1 = lambda *xs: jnp.concatenate(xs, axis=1)
    at, bt = ld(ab_ref, 0), ld(ab_ref, 1)
    rt, kt, be, ke, v = (ld(rkv_ref, slot) for slot in range(5))
    ats, bts = [_split2(a) for a in at], [_split2(b) for b in bt]
    atb = [hi for hi, _ in ats]
    zero = jnp.zeros((c, tw), F32)
    gk = [_dg(cat0(atb[i], rt[i]), bd(kt[i]), _NT) for i in n]
    aak = [jnp.where(strict, gk[i][:c], zero) for i in n]
    ark = [jnp.where(incl, gk[i][c:], zero) for i in n]
    yield
    gb = [_dg(cat0(ats[i][0], ats[i][1], rt[i]), bd(bts[i][0]), _NT) for i in n]
    arb = [jnp.where(incl, gb[i][2 * c:], zero) for i in n]
    aab = [jnp.where(strict, gb[i][:c] + (gb[i][c:2 * c] + _dg(atb[i], bd(bts[i][1]), _NT)), zero) for i in n]
    yield
    z = [pmm(aak[i], v[i]) for i in n]
    a8 = [jnp.where(same8, aab[i], zero) for i in n]
    p2 = [pmm(a8[i], a8[i]) for i in n]
    yield
    p4 = [pmm(p2[i], p2[i]) for i in n]
    yield
    nn = [a8[i] + p2[i] + pmm(p2[i], a8[i]) for i in n]
    yield
    nn = [nn[i] + p4[i] + pmm(p4[i], nn[i]) for i in n]
    yield
    for lvl in (3, 4, 5):
        joins = ((col >> (lvl + 1)) == (row >> (lvl + 1))) & ((col >> lvl) != (row >> lvl))
        e = [jnp.where(joins, aab[i], zero) for i in n]
        te = [e[i] + pmm(nn[i], e[i]) for i in n]
        yield
        nn = [nn[i] + te[i] + pmm(te[i], nn[i]) for i in n]
        yield
    wu = [cat1(at[i], z[i]) + _dg(nn[i].astype(BF16), cat1(bd(atb[i]), bd(z[i].astype(BF16)))) for i in n]
    abk = [cat1(arb[i], ark[i]).astype(BF16) for i in n]
    yield
    s = {(b, pr): s_ref[b, pr] for b in range(bb) for pr in range(RWKV_TILES)}
    for ck in range(nck):
        cur = [i for i in n if items[i][1] == ck]
        key = lambda i: (items[i][0], items[i][2])
        x = {i: _dg(cat0(wu[i][:, :tw].astype(BF16), rt[i]), s[key(i)].astype(BF16), _NT) for i in cur}
        ub = {i: (x[i][:c] + wu[i][:, tw:]).astype(BF16) for i in cur}
        yield
        upd = {i: _dg(cat0(ub[i], v[i]), cat0(be[i], ke[i]), _TN) for i in cur}
        for i in cur:
            b, _, pr = items[i]
            y_ref[b, rows(ck), lanes(pr)] = x[i][c:] + _dg(abk[i], cat0(bd(ub[i]), bd(v[i])))
            s[b, pr] = (s[b, pr] * gl_ref[b, ck, :, lanes(pr)]
                        + jnp.where(same_head, upd[i], jnp.zeros_like(upd[i])))
        yield
    for (b, pr), val in s.items():
        s_ref[b, pr] = val


def _gla_stages(p_ref, wgate_ref, bgate_ref, tri_ref, ng_ref, s0_ref, y_ref, s_ref, bb, nck):
    @pl.when(pl.program_id(1) == 0)
    def _():
        s_ref[...] = s0_ref[...]

    head, row, col = _head_masks(LANES)
    incl = col <= row
    bd = functools.partial(_block_diag, head=head)
    lane2 = lax.broadcasted_iota(jnp.int32, (2 * GLA_DV, LANES), 1)
    row2 = lax.broadcasted_iota(jnp.int32, (2 * GLA_DV, LANES), 0)
    same_head = (lane2 < GLA_DK) == (row2 < GLA_DV)
    kw, gw = GLA_KEY_WIDTH, GLA_WIDTH
    zero = jnp.zeros((CHUNK, LANES), F32)
    cat0 = lambda xs: jnp.concatenate(xs, axis=0)
    cat1 = lambda xs: jnp.concatenate(xs, axis=1)
    rows = lambda c: slice(c * CHUNK, (c + 1) * CHUNK)
    lanes = lambda pr: slice(pr * LANES, (pr + 1) * LANES)
    bs, cs = range(bb), range(nck)
    x = [p_ref[b] for b in bs]
    gate = [_mm1(x[b][:, 2 * kw + 2 * gw:], wgate_ref[...]) + bgate_ref[...] for b in bs]
    la = [(jnp.minimum(g, 0.0) - jnp.log(1.0 + jnp.exp(-jnp.abs(g)))) / GLA_TAU for g in gate]
    yield
    cum = [_mm_exact_lhs(tri_ref[...], la[b]) for b in bs]
    yield
    q_dec = [(x[b][:, :kw] * (GLA_DK ** -0.5)) * jnp.exp(cum[b]) for b in bs]
    k_inv = [x[b][:, kw:2 * kw] * jnp.exp(-cum[b]) for b in bs]
    last = [[cum[b][c * CHUNK + CHUNK - 1:(c + 1) * CHUNK, :] for c in cs] for b in bs]
    k_end = [[x[b][rows(c), kw:2 * kw] * jnp.exp(last[b][c] - cum[b][rows(c)]) for c in cs] for b in bs]
    dec = [[jnp.exp(last[b][c]) for c in cs] for b in bs]
    yield
    items = [(b, c, pr) for b in bs for c in cs for pr in range(GLA_PAIRS)]
    vp = {(b, c, pr): x[b][rows(c), 2 * kw + pr * 2 * GLA_DV:2 * kw + (pr + 1) * 2 * GLA_DV]
          for b, c, pr in items}
    att = {(b, c, pr): jnp.where(incl, _mm1(q_dec[b][rows(c), lanes(pr)],
                                            bd(k_inv[b][rows(c), lanes(pr)]), _NT), zero)
           for b, c, pr in items}
    yield
    upd = {it: _mm1(vp[it], k_end[it[0]][it[1]][:, lanes(it[2])], _TN) for it in items}
    yield
    intra = {it: _mm1(att[it], cat0([cat1([vp[it][:, :GLA_DV], zero]), cat1([zero, vp[it][:, GLA_DV:]])]))
             for it in items}
    yield
    st = {}
    for b in bs:
        for pr in range(GLA_PAIRS):
            cur = s_ref[b, pr]
            for c in cs:
                st[b, c, pr] = cur
                cur = cur * dec[b][c][:, lanes(pr)] + jnp.where(same_head, upd[b, c, pr], jnp.zeros_like(cur))
            s_ref[b, pr] = cur
    yield
    o = {(b, c, pr): intra[b, c, pr] + _mm1(q_dec[b][rows(c), lanes(pr)], st[b, c, pr], _NT)
         for b, c, pr in items}
    yield
    for b in bs:
        ob = cat0([cat1([o[b, c, pr] for pr in range(GLA_PAIRS)]) for c in cs])
        heads = [ob[:, h * GLA_DV:(h + 1) * GLA_DV] for h in range(GLA_HEADS)]
        normed = [oh * lax.rsqrt(jnp.mean(oh * oh, axis=-1, keepdims=True) + RMS_EPS) for oh in heads]
        y_ref[b] = cat1(normed) * ng_ref[...] * _silu(x[b][:, 2 * kw + gw:2 * kw + 2 * gw])


def _recurrent_kernel(*refs, bb, nck, n_cast):
    rwkv_in, refs = refs[:4], refs[4:]
    cast_in, refs = refs[:n_cast], refs[n_cast:]
    gla_in, refs = refs[:6], refs[6:]
    (y_ref, s_ref), refs = refs[:2], refs[2:]
    cast_out, (yg_ref, t_ref) = refs[:n_cast], refs[n_cast:]
    rwkv = _rwkv_stages(*rwkv_in, cast_in, y_ref, s_ref, cast_out, bb, nck)
    gla = _gla_stages(*gla_in, yg_ref, t_ref, bb, nck)
    done = object()
    live = [rwkv, rwkv, gla]
    while live:
        for gen in list(live):
            if gen in live and next(gen, done) is done:
                live = [g for g in live if g is not gen]


def _recurrent(ab, rkv, gl, s0, proj, lp, tri, t0, bb, nck, casts=()):
    bsz, seq, _ = ab.shape
    w = RWKV_WIDTH
    rows = nck * CHUNK
    n_seq_steps = seq // rows
    n_steps = (bsz // bb) * n_seq_steps
    tok = lambda a: pl.BlockSpec((bb, rows, a.shape[-1]), lambda i, c: (i, c, 0))
    full = lambda a: pl.BlockSpec(a.shape, lambda i, c: (0,) * a.ndim)
    st = pl.BlockSpec((bb, RWKV_TILES, RWKV_TILE_LANES, RWKV_TILE_LANES), lambda i, c: (i, 0, 0, 0))
    gst = pl.BlockSpec((bb, GLA_PAIRS, 2 * GLA_DV, LANES), lambda i, c: (i, 0, 0, 0))
    slab = lambda a: pl.BlockSpec((a.shape[0] // n_steps, a.shape[1]), lambda i, c: (i * n_seq_steps + c, 0))
    assert all(a.shape[0] % (16 * n_steps) == 0 for a in casts)
    out_tok = pl.BlockSpec((bb, rows, w), lambda i, c: (i, c, 0))
    return pl.pallas_call(
        functools.partial(_recurrent_kernel, bb=bb, nck=nck, n_cast=len(casts)),
        grid=(bsz // bb, n_seq_steps),
        in_specs=[tok(ab), tok(rkv), pl.BlockSpec((bb, nck, 1, w), lambda i, c: (i, c, 0, 0)), st]
        + [slab(a) for a in casts]
        + [pl.BlockSpec((bb, rows, GLA_PROJ_PAD), lambda i, c: (i, c, 1)),
           full(lp["wgate"]), full(lp["bgate"]), full(tri), full(lp["gla_g"]), gst],
        out_specs=[out_tok, st] + [slab(a) for a in casts] + [out_tok, gst],
        out_shape=[jax.ShapeDtypeStruct((bsz, seq, w), F32),
                   jax.ShapeDtypeStruct((bsz, RWKV_TILES, RWKV_TILE_LANES, RWKV_TILE_LANES), F32)]
        + [jax.ShapeDtypeStruct(a.shape, BF16) for a in casts]
        + [jax.ShapeDtypeStruct((bsz, seq, GLA_WIDTH), F32),
           jax.ShapeDtypeStruct((bsz, GLA_PAIRS, 2 * GLA_DV, LANES), F32)],
        compiler_params=_params("parallel", "arbitrary"),
        name="rwkv_gla_chunk",
    )(ab, rkv, gl, s0, *casts, proj, lp["wgate"], lp["bgate"], tri, lp["gla_g"], t0)


def _route(lgt):
    tm = lgt.shape[1]
    n_pad = ROUTE_ROWS - N_EXPERTS
    rowg = lax.broadcasted_iota(jnp.int32, (n_pad, tm), 0)
    rowe = lax.broadcasted_iota(jnp.int32, (N_EXPERTS, tm), 0)
    rowg_f, rowe_f = rowg.astype(F32), rowe.astype(F32)

    def first_argmax(vals, mx, rows_f):
        return jnp.min(jnp.where(vals == mx, rows_f, float(ROUTE_ROWS)), axis=0, keepdims=True)

    is_group = rowg < N_GROUPS
    lg = jnp.where(is_group, lgt[N_EXPERTS:], -jnp.inf)
    gmax = jnp.max(lg, axis=0, keepdims=True)
    gi = first_argmax(lg, gmax, rowg_f)
    pg_top = 1.0 / jnp.sum(jnp.where(is_group, jnp.exp(lg - gmax), 0.0), axis=0, keepdims=True)
    in_group = (rowe >> GROUP_SHIFT).astype(F32) == gi
    le = jnp.where(in_group, lgt[:N_EXPERTS], -jnp.inf)
    m1 = jnp.max(le, axis=0, keepdims=True)
    i1 = first_argmax(le, m1, rowe_f)
    le2 = jnp.where(rowe_f == i1, -jnp.inf, le)
    m2 = jnp.max(le2, axis=0, keepdims=True)
    i2 = first_argmax(le2, m2, rowe_f)
    p2 = jnp.exp(m2 - m1)
    w1 = pg_top / (1.0 + p2)
    w2 = pg_top * p2 / (1.0 + p2)
    comb = jnp.where(rowe_f == i1, w1, 0.0) + jnp.where(rowe_f == i2, w2, 0.0)
    return jnp.concatenate([comb, jnp.where(rowg == 0, gi, 0.0)], axis=0)


def _outproj_kernel(yr_ref, bg_ref, yg_ref, x_ref, gt_ref, sc_ref, sh_ref,
                    lnw_ref, lnb_ref, seg_ref, wout_ref, g2_ref, wr_ref, br_ref,
                    x1_ref, h2_ref, route_ref):
    nb, rb, d_model = x_ref.shape
    tm = nb * rb
    flat = lambda ref: ref[...].reshape(tm, ref.shape[-1])
    seg = seg_ref[...]
    y = flat(yr_ref)
    inv_n = 1.0 / RWKV_HEAD
    d = y - _mm2_exact_rhs(y, seg) * inv_n
    var = _mm2_exact_rhs(d * d, seg) * inv_n
    yn = d * lax.rsqrt(var + LNX_EPS) * lnw_ref[...] + lnb_ref[...]
    bg = flat(bg_ref)
    yr = (yn + bg[:, :RWKV_WIDTH]) * bg[:, RWKV_WIDTH:]
    mix = jnp.concatenate([yr, flat(yg_ref)], axis=1)
    x1 = x_ref[...] + gt_ref[...] * _mm1(mix, wout_ref[...]).reshape(nb, rb, d_model)
    x1_ref[...] = x1
    h2 = _rms_mod(x1, g2_ref[...], sc_ref[...], sh_ref[...])
    h2_ref[...] = h2.astype(BF16)
    route_ref[...] = _route(_mm3(wr_ref[...], h2.reshape(tm, d_model), _NT) + br_ref[...])


def _out_proj(yr, bg, yg, x, gt, sc, sh, lp, consts, rows):
    bsz, seq, d = x.shape
    nb, rb = _token_tile(bsz, seq, rows)
    n_seq_tiles = seq // rb
    half = pl.BlockSpec((nb, rb, RWKV_WIDTH), lambda b, i: (b, i, 0))
    tokd = pl.BlockSpec((nb, rb, d), lambda b, i: (b, i, 0))
    vec = pl.BlockSpec((nb, 1, d), lambda b, i: (b, 0, 0))
    full = lambda a: pl.BlockSpec(a.shape, lambda b, i: (0,) * a.ndim)
    args = (lp["lnx_w"], lp["lnx_b"], consts["seg"], lp["w_out"], lp["norm2_g"], lp["w_router"], lp["b_router"])
    return pl.pallas_call(
        _outproj_kernel,
        grid=(bsz // nb, seq // rb),
        in_specs=[half, pl.BlockSpec((nb, rb, 2 * RWKV_WIDTH), lambda b, i: (b, i, 0)), half, tokd,
                  vec, vec, vec] + [full(a) for a in args],
        out_specs=[tokd, tokd, pl.BlockSpec((ROUTE_ROWS, nb * rb), lambda b, i: (0, b * n_seq_tiles + i))],
        out_shape=[jax.ShapeDtypeStruct((bsz, seq, d), F32),
                   jax.ShapeDtypeStruct((bsz, seq, d), BF16),
                   jax.ShapeDtypeStruct((ROUTE_ROWS, bsz * seq), F32)],
        compiler_params=_params("parallel", "parallel"),
        name="out_proj_router",
    )(yr, bg, yg, x, gt, sc, sh, *args)


def _moe_kernel(h_ref, route_ref, x1_ref, gt_ref, sc_ref, sh_ref, gf_ref, tri_ref, wg_ref, wu_ref, wd_ref,
                y_ref, ys_ref, pos_ref, meta_ref):
    g = pl.program_id(2)
    nb, rb, d = h_ref.shape
    tm = nb * rb
    sub, tail = MOE_SUB_ROWS, MOE_SUB_ROWS // 2
    cap = ys_ref.shape[0]
    n_lane_tiles = tm // LANES
    lane_tile = lambda k: slice(k * LANES, (k + 1) * LANES)

    @pl.when(g == 0)
    def _sort():
        row8 = lax.broadcasted_iota(jnp.int32, (8, LANES), 0).astype(F32)
        carry = jnp.zeros((8, 1), F32)
        members, ranks = [], []
        for k in range(n_lane_tiles):
            blk = jnp.where(row8 == route_ref[N_EXPERTS:N_EXPERTS + 1, lane_tile(k)], 1.0, 0.0)
            members.append(blk)
            ranks.append(_dg(blk.astype(BF16), tri_ref[...]) - blk + carry)
            carry = carry + jnp.sum(blk, axis=1, keepdims=True)
        rowc = lax.broadcasted_iota(jnp.int32, (8, 1), 0)
        lane = lax.broadcasted_iota(jnp.int32, (1, LANES), 1)
        first = jnp.zeros((1, 1), F32)
        off_col = jnp.zeros((8, 1), F32)
        meta = jnp.zeros((1, LANES), F32)
        for grp in range(N_GROUPS):
            count = jnp.sum(jnp.where(rowc == grp, carry, 0.0), axis=0, keepdims=True)
            blocks = jnp.ceil(count * (1.0 / tail))
            pairs = jnp.floor(blocks * 0.5)
            odd = blocks - 2.0 * pairs
            triple = odd * jnp.where(blocks >= 3.0, 1.0, 0.0)
            n_full = pairs - triple
            off_col = off_col + jnp.where(rowc == grp, first, 0.0)
            meta = (meta + jnp.where(lane == grp, first, 0.0) + jnp.where(lane == N_GROUPS + grp, n_full, 0.0)
                    + jnp.where(lane == 2 * N_GROUPS + grp, odd + 2.0 * triple, 0.0))
            first = first + blocks * tail
        meta = (meta + jnp.where(lane == 3 * N_GROUPS, first, 0.0)).astype(jnp.int32)
        for i in range(3 * N_GROUPS + 1):
            meta_ref[i] = meta[0, i]
        for k in range(n_lane_tiles):
            pos = jnp.sum(members[k] * (ranks[k] + off_col), axis=0, keepdims=True)
            pos_ref[:, lane_tile(k)] = pos.astype(jnp.int32)
        ys_ref[...] = jnp.zeros_like(ys_ref)

    def experts(base, n_rows):
        prow = lax.broadcasted_iota(jnp.int32, (n_rows, tm), 0) + base
        onehot = jnp.where(prow == pos_ref[...], 1.0, 0.0).astype(BF16)
        hs = _dg(onehot, h_ref[...].reshape(tm, d)).astype(BF16)
        comb3 = _split3(route_ref[:N_EXPERTS, :])
        cs = _dg(onehot, comb3[0], _NT) + (_dg(onehot, comb3[1], _NT) + _dg(onehot, comb3[2], _NT))
        lane = lax.broadcasted_iota(jnp.int32, cs.shape, 1)
        ys = jnp.zeros((n_rows, d), F32)
        for e in range(EXPERTS_PER_GROUP):
            hid = _silu(_dg(hs, wg_ref[e])) * _dg(hs, wu_ref[e])
            ce = jnp.sum(jnp.where(lane == g * EXPERTS_PER_GROUP + e, cs, 0.0), axis=-1, keepdims=True)
            ys = ys + ce * _dg(hid.astype(BF16), wd_ref[e])
        ys_ref[pl.ds(base, n_rows), :] = ys.astype(BF16)

    first_row = meta_ref[g]
    n_full = meta_ref[N_GROUPS + g]

    def full_sub_tile(j, carry_):
        experts(pl.multiple_of(first_row + j * sub, tail), sub)
        return carry_

    lax.fori_loop(0, n_full, full_sub_tile, 0)

    for last_blocks in (1, 3):
        @pl.when(meta_ref[2 * N_GROUPS + g] == last_blocks)
        def _last(last_blocks=last_blocks):
            experts(pl.multiple_of(first_row + n_full * sub, tail), last_blocks * tail)

    def unsort(n_sorted):
        ys_all = ys_ref[:n_sorted]
        prow = lax.broadcasted_iota(jnp.int32, (n_sorted, LANES), 0)
        seqs = max(LANES // rb, 1)
        for k in range(n_lane_tiles):
            onehot = jnp.where(prow == pos_ref[:, lane_tile(k)], 1.0, 0.0).astype(BF16)
            moe = _dg(onehot, ys_all, _TN).reshape(seqs, LANES // seqs, d)
            b0 = k * LANES // rb
            r0 = k * LANES - b0 * rb
            bs, rs = slice(b0, b0 + seqs), slice(r0, r0 + LANES // seqs)
            x2 = x1_ref[bs, rs] + gt_ref[bs] * moe
            y_ref[bs, rs] = _rms_mod(x2, gf_ref[...], sc_ref[bs], sh_ref[bs])

    usual = tm + (cap - tm) // 2
    last = g == N_GROUPS - 1
    pl.when(last & (meta_ref[3 * N_GROUPS] <= usual))(lambda: unsort(usual))
    pl.when(last & (meta_ref[3 * N_GROUPS] > usual))(lambda: unsort(cap))


def _moe(h2, route, x1, gt, sc, sh, gf, wg, wu, wd, tri, rows):
    bsz, seq, d = x1.shape
    nb, rb = _token_tile(bsz, seq, rows)
    tm = nb * rb
    assert tm % LANES == 0 and (rb % LANES == 0 or LANES % rb == 0)
    n_seq_tiles = seq // rb
    cap = tm + N_GROUPS * (MOE_SUB_ROWS // 2)
    tokd = pl.BlockSpec((nb, rb, d), lambda b, i, g: (b, i, 0))
    vec = pl.BlockSpec((nb, 1, d), lambda b, i, g: (b, 0, 0))
    group_w = lambda shape: pl.BlockSpec((EXPERTS_PER_GROUP,) + shape, lambda b, i, g: (g, 0, 0))
    return pl.pallas_call(
        _moe_kernel,
        grid=(bsz // nb, n_seq_tiles, N_GROUPS),
        in_specs=[tokd, pl.BlockSpec((ROUTE_ROWS, tm), lambda b, i, g: (0, b * n_seq_tiles + i)),
                  tokd, vec, vec, vec,
                  pl.BlockSpec((1, d), lambda b, i, g: (0, 0)),
                  pl.BlockSpec(tri.shape, lambda b, i, g: (0, 0)),
                  group_w((d, D_EXPERT)), group_w((d, D_EXPERT)), group_w((D_EXPERT, d))],
        out_specs=tokd,
        out_shape=jax.ShapeDtypeStruct((bsz, seq, d), F32),
        scratch_shapes=[pltpu.VMEM((cap, d), BF16), pltpu.VMEM((1, tm), jnp.int32),
                        pltpu.SMEM((3 * N_GROUPS + 1,), jnp.int32)],
        compiler_params=_params("parallel", "parallel", "arbitrary"),
        name="moe_final_norm",
    )(h2, route, x1, gt, sc, sh, gf, tri, wg, wu, wd)


def _block_ones(n, blk, lower):
    i = np.arange(n)
    m = (i[:, None] // blk) == (i[None, :] // blk)
    if lower:
        m = m & (i[None, :] <= i[:, None])
    return jnp.asarray(m, dtype=BF16)


def _consts(tr, tg):
    return dict(seg=_block_ones(RWKV_WIDTH, RWKV_HEAD, False), tri=_block_ones(tr, CHUNK, True),
                tri_gla=_block_ones(tg, CHUNK, True), tri_up=_block_ones(LANES, LANES, True).T)


def _heads_block_diag(s):
    n, c = s.shape[2], s.shape[-1]
    rows = [jnp.pad(s[:, :, j], ((0, 0), (0, 0), (0, 0), (j * c, (n - 1 - j) * c))) for j in range(n)]
    return jnp.concatenate(rows, axis=2)


def _diag_blocks(t, n):
    r, c = t.shape[-2] // n, t.shape[-1] // n
    return jnp.stack([t[:, :, j * r:(j + 1) * r, j * c:(j + 1) * c] for j in range(n)], axis=2)


def _pad_rows(w, first_row):
    out = jnp.zeros((LORA_PAD, w.shape[1]), F32)
    return lax.dynamic_update_slice(out, w, (first_row, 0)).astype(BF16)


def _layer_params(l, w_in, mu_shift, w0, w_decay_up, a0, w_a_up, w_g_up, k_k, k_a, r_k, lnx_w, lnx_b,
                  w_gla_gate_up, b_gla_gate, gla_norm_g, w_out, norm2_g,
                  w_router_group, b_router_group, w_router_expert, b_router_expert):
    w_in_p = _win_layout(w_in[l].T, WEIGHT_TILE_COLS)
    n_pad = ROUTE_ROWS - N_EXPERTS - N_GROUPS
    w_router = jnp.concatenate([w_router_expert[l].T, w_router_group[l].T, jnp.zeros((n_pad, D_MODEL), F32)])
    b_router = jnp.concatenate([b_router_expert[l], b_router_group[l],
                                jnp.zeros((n_pad,), F32)]).reshape(ROUTE_ROWS, 1)
    wgate = jnp.zeros((LANES, GLA_KEY_WIDTH), F32).at[:GLA_GATE_RANK].set(w_gla_gate_up[l]).astype(BF16)
    r1 = lambda a: a.reshape(1, -1)
    return dict(
        w_in=w_in_p, mu=r1(mu_shift[l]), w0=r1(w0[l]), a0=r1(a0[l]), k_k=r1(k_k[l]), k_a=r1(k_a[l]),
        r_k=r1(r_k[l]), wd=_pad_rows(w_decay_up[l], 0), wa=_pad_rows(w_a_up[l], DECAY_LORA),
        wg=_pad_rows(w_g_up[l], DECAY_LORA + AAA_LORA), lnx_w=r1(lnx_w[l]), lnx_b=r1(lnx_b[l]),
        wgate=wgate, bgate=r1(b_gla_gate[l]), gla_g=r1(gla_norm_g[l]),
        w_out=w_out[l].astype(BF16), norm2_g=r1(norm2_g[l]), w_router=w_router, b_router=b_router)


def _run_layer(x, mod, shift0, wkv0, gla0, lp, experts, final):
    bsz, seq, d = x.shape
    assert seq % CHUNK == 0
    bb = 2 if seq >= RECURRENT_STEP_ROWS else min(bsz, RECURRENT_MAX_SEQS)
    assert bsz % bb == 0
    nck = min(seq, RECURRENT_STEP_ROWS) // CHUNK
    consts = _consts(TOKEN_TILE_ROWS, nck * CHUNK)
    m = lambda j: mod[:, j:j + 1, :]
    sh1, sc1, gt1, sh2, sc2, gt2 = (m(j) for j in range(6))
    proj, tails = _in_proj(x, sc1, sh1, lp["norm1_g"], lp["w_in"], PROJ_TILE_ROWS, TOKEN_TILE_ROWS)
    tails = tails.reshape(bsz, -1, RWKV_PROJ)
    new_shift = tails[:, -1]
    prev = jnp.concatenate([shift0[:, None, :], tails[:, :-1]], axis=1).reshape(-1, 1, RWKV_PROJ)
    ab, rkv, bg, gl = _rwkv_prep(proj, prev, lp, consts, TOKEN_TILE_ROWS)
    gl = gl.reshape(bsz, seq // CHUNK, RWKV_WIDTH)
    s0 = _heads_block_diag(wkv0.reshape(bsz, RWKV_TILES, RWKV_TILE_HEADS, RWKV_HEAD, RWKV_HEAD))
    t0 = _heads_block_diag(jnp.swapaxes(gla0, -1, -2).reshape(bsz, GLA_PAIRS, 2, GLA_DV, GLA_DK))
    casts = tuple(e.reshape(-1, e.shape[-1]) for e in experts if e.dtype != BF16)
    yr, s_bd, *cast, yg, t_bd = _recurrent(ab, rkv, gl[:, :, None, :], s0, proj, lp, consts["tri_gla"], t0,
                                           bb, nck, casts)
    if cast:
        experts = tuple(c.reshape(e.shape) for c, e in zip(cast, experts))
    new_wkv = _diag_blocks(s_bd, RWKV_TILE_HEADS).reshape(bsz, RWKV_HEADS, RWKV_HEAD, RWKV_HEAD)
    new_gla = jnp.swapaxes(_diag_blocks(t_bd, 2).reshape(bsz, GLA_HEADS, GLA_DV, GLA_DK), -1, -2)
    x1, h2, route = _out_proj(yr, bg, yg, x, gt1, sc2, sh2, lp, consts, PROJ_TILE_ROWS)
    out = _moe(h2, route, x1, gt2, *final, *experts, consts["tri_up"], MOE_TILE_ROWS)
    return experts, out, new_shift, new_wkv, new_gla


def kernel(x_prompt, x_sample, c_prompt, c_sample, state_rwkv_shift, state_rwkv_wkv, state_gla_kv, w_ada, b_ada, norm1_g, norm2_g, w_in, mu_shift, w0, w_decay_up, a0, w_a_up, w_g_up, k_k, k_a, r_k, lnx_w, lnx_b, w_gla_gate_up, b_gla_gate, gla_norm_g, w_out, w_router_group, b_router_group, w_router_expert, b_router_expert, w_expert_gate, w_expert_up, w_expert_down, w_ada_final, b_ada_final, normf_g):
    assert w_ada.shape[0] == 1, "the final norm is fused into the single layer's MoE kernel"
    bp, bs = x_prompt.shape[0], x_sample.shape[0]
    d = D_MODEL
    n_rows = -(-(bp + bs) // 8) * 8
    c_all = jnp.zeros((n_rows, d), F32).at[:bp].set(c_prompt).at[bp:bp + bs].set(c_sample)
    modf = _modulation(c_all, w_ada_final, b_ada_final, MODULATION_TILE_COLS).reshape(n_rows, 2, d)
    mod = _modulation(c_all, w_ada, b_ada[0], MODULATION_TILE_COLS, layer=0).reshape(n_rows, 6, d)
    lp = _layer_params(0, w_in, mu_shift, w0, w_decay_up, a0, w_a_up, w_g_up, k_k, k_a, r_k, lnx_w,
                       lnx_b, w_gla_gate_up, b_gla_gate, gla_norm_g, w_out, norm2_g,
                       w_router_group, b_router_group, w_router_expert, b_router_expert)
    lp["norm1_g"] = norm1_g[0].reshape(1, d)
    experts = (w_expert_gate[0], w_expert_up[0], w_expert_down[0])
    groups = [
        (x_prompt, 0, bp, jnp.zeros((bp, RWKV_PROJ), F32),
         jnp.zeros((bp, RWKV_HEADS, RWKV_HEAD, RWKV_HEAD), F32), jnp.zeros((bp, GLA_HEADS, GLA_DK, GLA_DV), F32)),
        (x_sample, bp, bp + bs, state_rwkv_shift[0], state_rwkv_wkv[0], state_gla_kv[0]),
    ]
    ys, states = [], []
    for x, lo, hi, shift0, wkv0, gla0 in groups:
        final = (modf[lo:hi, 1:2], modf[lo:hi, 0:1], normf_g.reshape(1, d))
        experts, y, *st = _run_layer(x, mod[lo:hi], shift0, wkv0, gla0, lp, experts, final)
        ys.append(y)
        states.extend(s[None] for s in st)
    return tuple(ys + states)
```

```python
import functools

import jax
import jax.numpy as jnp
import numpy as np
from jax import lax
from jax.experimental import pallas as pl
from jax.experimental.pallas import tpu as pltpu

F32 = jnp.float32
BF16 = jnp.bfloat16

LANES = 128
VMEM_LIMIT_BYTES = 56 * 1024 * 1024
TOKEN_TILE_ROWS = 512
PROJ_TILE_ROWS = 1024
MOE_TILE_ROWS = 1024
MOE_SUB_ROWS = 256
RECURRENT_STEP_ROWS = 256
RECURRENT_MAX_SEQS = 8
MODULATION_TILE_COLS = 1024
WEIGHT_TILE_COLS = 256

D_MODEL = 1024
CHUNK = 64
RWKV_WIDTH = 512
RWKV_HEAD = 64
RWKV_HEADS = RWKV_WIDTH // RWKV_HEAD
HEAD_LANES = RWKV_HEAD
HEAD_SHIFT = HEAD_LANES.bit_length() - 1
assert 1 << HEAD_SHIFT == HEAD_LANES
RWKV_TILE_LANES = 128
RWKV_TILE_HEADS = RWKV_TILE_LANES // RWKV_HEAD
RWKV_TILES = RWKV_WIDTH // RWKV_TILE_LANES
DECAY_LORA = 32
AAA_LORA = 32
GATE_LORA = 64
LORA_PAD = DECAY_LORA + AAA_LORA + GATE_LORA
RWKV_PROJ = 3 * RWKV_WIDTH + LORA_PAD
GLA_WIDTH = 512
GLA_HEADS = 4
GLA_PAIRS = GLA_HEADS // 2
GLA_DV = GLA_WIDTH // GLA_HEADS
GLA_DK = GLA_DV // 2
GLA_KEY_WIDTH = GLA_HEADS * GLA_DK
GLA_GATE_RANK = 16
GLA_TAU = 16.0
GLA_PROJ = 2 * GLA_KEY_WIDTH + 2 * GLA_WIDTH + GLA_GATE_RANK
GLA_PROJ_PAD = RWKV_PROJ
IN_PROJ_PAD = RWKV_PROJ + GLA_PROJ_PAD
N_GROUPS = 4
EXPERTS_PER_GROUP = 4
N_EXPERTS = N_GROUPS * EXPERTS_PER_GROUP
GROUP_SHIFT = EXPERTS_PER_GROUP.bit_length() - 1
assert 1 << GROUP_SHIFT == EXPERTS_PER_GROUP
ROUTE_ROWS = 24
D_EXPERT = 512
RMS_EPS = 1e-6
LNX_EPS = 64e-5
LOG2_E = 1.4426950408889634

_NN = (((1,), (0,)), ((), ()))
_NT = (((1,), (1,)), ((), ()))
_TN = (((0,), (0,)), ((), ()))


def _dg(a, b, dims=_NN):
    return lax.dot_general(a, b, dims, preferred_element_type=F32)


def _split2(x):
    hi = x.astype(BF16)
    lo = (x - hi.astype(F32)).astype(BF16)
    return hi, lo


def _split3(x):
    hi = x.astype(BF16)
    r1 = x - hi.astype(F32)
    mid = r1.astype(BF16)
    lo = (r1 - mid.astype(F32)).astype(BF16)
    return hi, mid, lo


def _mm1(a, b, dims=_NN):
    return _dg(a.astype(BF16), b.astype(BF16), dims)


def _mm3(a, b, dims=_NN):
    ah, al = _split2(a)
    bh, bl = _split2(b)
    return _dg(ah, bh, dims) + (_dg(ah, bl, dims) + _dg(al, bh, dims))


def _mm_exact_lhs(e, x, dims=_NN):
    h, m, l = _split3(x)
    return _dg(e, h, dims) + (_dg(e, m, dims) + _dg(e, l, dims))


def _mm2_exact_rhs(x, e, dims=_NN):
    h, l = _split2(x)
    return _dg(h, e, dims) + _dg(l, e, dims)


def _softplus(z):
    return jnp.maximum(z, 0.0) + jnp.log(1.0 + jnp.exp(-jnp.abs(z)))


def _sigmoid(z):
    return 1.0 / (1.0 + jnp.exp(-z))


def _silu(z):
    return z * _sigmoid(z)


def _params(*sem):
    return pltpu.CompilerParams(dimension_semantics=sem, vmem_limit_bytes=VMEM_LIMIT_BYTES)


def _mod_kernel(c_ref, w_ref, b_ref, o_ref):
    o_ref[...] = _mm1(_silu(c_ref[...]), w_ref[...]) + b_ref[...]


def _modulation(c, w, b, tn, layer=None):
    rows, d = c.shape
    n = w.shape[-1]
    if layer is None:
        w_spec = pl.BlockSpec((d, tn), lambda j: (0, j))
    else:
        w_spec = pl.BlockSpec((None, d, tn), lambda j: (layer, 0, j))
    return pl.pallas_call(
        _mod_kernel,
        grid=(n // tn,),
        in_specs=[pl.BlockSpec((rows, d), lambda j: (0, 0)), w_spec,
                  pl.BlockSpec((1, tn), lambda j: (0, j))],
        out_specs=pl.BlockSpec((rows, tn), lambda j: (0, j)),
        out_shape=jax.ShapeDtypeStruct((rows, n), F32),
        compiler_params=_params("parallel"),
        name="modulation",
    )(c, w, b.reshape(1, n))


def _win_layout_kernel(w_ref, o_ref):
    qkv_end = RWKV_PROJ + 2 * GLA_KEY_WIDTH + GLA_WIDTH
    gate_end = qkv_end + GLA_GATE_RANK
    o_ref[:qkv_end] = w_ref[:qkv_end].astype(BF16)
    o_ref[qkv_end:qkv_end + GLA_WIDTH] = w_ref[gate_end:gate_end + GLA_WIDTH].astype(BF16)
    o_ref[qkv_end + GLA_WIDTH:RWKV_PROJ + GLA_PROJ] = w_ref[qkv_end:gate_end].astype(BF16)
    o_ref[RWKV_PROJ + GLA_PROJ:] = jnp.zeros((IN_PROJ_PAD - RWKV_PROJ - GLA_PROJ, o_ref.shape[1]), BF16)


def _win_layout(wt, cols):
    n, d = wt.shape
    return pl.pallas_call(
        _win_layout_kernel,
        grid=(d // cols,),
        in_specs=[pl.BlockSpec((n, cols), lambda i: (0, i))],
        out_specs=pl.BlockSpec((IN_PROJ_PAD, cols), lambda i: (0, i)),
        out_shape=jax.ShapeDtypeStruct((IN_PROJ_PAD, d), BF16),
        compiler_params=_params("parallel"),
        name="w_in_layout",
    )(wt)


def _rms_mod(x, g, sc, sh):
    ms = jnp.mean(x * x, axis=-1, keepdims=True)
    return (x * lax.rsqrt(ms + RMS_EPS) * g) * (1.0 + sc) + sh


def _token_tile(bsz, seq, rows):
    if seq >= rows:
        assert seq % rows == 0
        return 1, rows
    nb = min(bsz, rows // seq)
    assert bsz % nb == 0
    return nb, seq


def _inproj_kernel(x_ref, sc_ref, sh_ref, g_ref, w_ref, o_ref, last_ref, *, n_step, n_slices):
    nb, rb, d = x_ref.shape
    h = _rms_mod(x_ref[...], g_ref[...], sc_ref[...], sh_ref[...])
    hb = h.reshape(nb * rb, d).astype(BF16)
    for j in range(IN_PROJ_PAD // n_step):
        cols = slice(j * n_step, (j + 1) * n_step)
        o_ref[:, :, cols] = _dg(hb, w_ref[cols, :], _NT).reshape(nb, rb, n_step)
    rows = rb // n_slices
    for j in range(n_slices):
        last_ref[j * nb:(j + 1) * nb] = o_ref[:, (j + 1) * rows - 1:(j + 1) * rows, :RWKV_PROJ]


def _in_proj(x, sc, sh, g, w, rows, slice_rows):
    bsz, seq, d = x.shape
    nb, rb = _token_tile(bsz, seq, rows)
    n_seq_tiles = seq // rb
    n_slices = max(rb // slice_rows, 1)
    assert nb == 1 or n_slices == 1
    vec = pl.BlockSpec((nb, 1, d), lambda b, i: (b, 0, 0))
    return pl.pallas_call(
        functools.partial(_inproj_kernel, n_step=2 * LANES, n_slices=n_slices),
        grid=(bsz // nb, n_seq_tiles),
        in_specs=[pl.BlockSpec((nb, rb, d), lambda b, i: (b, i, 0)), vec, vec,
                  pl.BlockSpec((1, d), lambda b, i: (0, 0)),
                  pl.BlockSpec((IN_PROJ_PAD, d), lambda b, i: (0, 0))],
        out_specs=[pl.BlockSpec((nb, rb, IN_PROJ_PAD), lambda b, i: (b, i, 0)),
                   pl.BlockSpec((nb * n_slices, 1, RWKV_PROJ), lambda b, i: (b * n_seq_tiles + i, 0, 0))],
        out_shape=[jax.ShapeDtypeStruct((bsz, seq, IN_PROJ_PAD), F32),
                   jax.ShapeDtypeStruct((bsz * n_seq_tiles * n_slices, 1, RWKV_PROJ), F32)],
        compiler_params=_params("parallel", "parallel"),
        name="norm1_in_proj",
    )(x, sc, sh, g, w)


def _rwkv_prep_kernel(p_ref, prev_ref, mu_ref, w0_ref, a0_ref, kk_ref, ka_ref, rk_ref,
                      wd_ref, wa_ref, wg_ref, seg_ref, tri_ref,
                      ab_ref, rkv_ref, bg_ref, gl_ref):
    nb, rb, wp = p_ref.shape
    tr = nb * rb
    p = p_ref[...].reshape(tr, wp)
    row = lax.broadcasted_iota(jnp.int32, (nb, rb, wp), 1)
    xx = jnp.where(row == 0, prev_ref[...], pltpu.roll(p, 1, 0).reshape(nb, rb, wp)).reshape(tr, wp)
    ps = p + (xx - p) * mu_ref[...]
    w = RWKV_WIDTH
    r, k, v, lora = ps[:, :w], ps[:, w:2 * w], ps[:, 2 * w:3 * w], ps[:, 3 * w:]
    logw = -_softplus(-(w0_ref[...] + _mm1(jnp.tanh(lora), wd_ref[...]))) - 0.5
    lw = jnp.exp(logw) * (-LOG2_E)
    a = _sigmoid(a0_ref[...] + _mm1(lora, wa_ref[...]))
    g = _mm1(_sigmoid(lora), wg_ref[...])
    seg = seg_ref[...]
    kk = k * kk_ref[...]
    kk = kk / jnp.maximum(jnp.sqrt(_mm2_exact_rhs(kk * kk, seg)), 1e-12)
    k2 = k * (1.0 + (a - 1.0) * ka_ref[...])
    cum = _mm_exact_lhs(tri_ref[...], lw)
    lasts = [jnp.exp2(cum[c * CHUNK + CHUNK - 1:(c + 1) * CHUNK, :]) for c in range(tr // CHUNK)]
    for c, l in enumerate(lasts):
        gl_ref[c] = l

    def to_chunk_end(val):
        return jnp.concatenate([val[c * CHUNK:(c + 1) * CHUNK] * l for c, l in enumerate(lasts)], axis=0)

    gamma = jnp.exp2(cum)
    ginv = 1.0 / gamma
    bt = (kk * a) * ginv
    kt = k2 * ginv

    def put(ref, slot, val):
        ref[:, :, slot * w:(slot + 1) * w] = val.reshape(nb, rb, w).astype(ref.dtype)

    put(ab_ref, 0, -kk * jnp.exp2(cum - lw))
    put(ab_ref, 1, bt)
    put(rkv_ref, 0, r * gamma)
    put(rkv_ref, 1, kt)
    put(rkv_ref, 2, to_chunk_end(bt))
    put(rkv_ref, 3, to_chunk_end(kt))
    put(rkv_ref, 4, v)
    put(bg_ref, 0, _mm2_exact_rhs(r * k2 * rk_ref[...], seg) * v)
    put(bg_ref, 1, g)


def _rwkv_prep(proj, prev, lp, consts, rows):
    bsz, seq, _ = proj.shape
    w = RWKV_WIDTH
    nb, rb = _token_tile(bsz, seq, rows)
    n_seq_tiles = seq // rb
    flat = lambda b, i: (b * n_seq_tiles + i, 0)
    row = lambda n: pl.BlockSpec((1, n), lambda b, i: (0, 0))
    full = lambda a: pl.BlockSpec(a.shape, lambda b, i: (0,) * a.ndim)
    packed = [(2, F32), (5, BF16), (2, F32)]
    return pl.pallas_call(
        _rwkv_prep_kernel,
        grid=(bsz // nb, n_seq_tiles),
        in_specs=[pl.BlockSpec((nb, rb, RWKV_PROJ), lambda b, i: (b, i, 0)),
                  pl.BlockSpec((nb, 1, RWKV_PROJ), lambda b, i: flat(b, i) + (0,)),
                  row(RWKV_PROJ), row(w), row(w), row(w), row(w), row(w),
                  full(lp["wd"]), full(lp["wa"]), full(lp["wg"]), full(consts["seg"]), full(consts["tri"])],
        out_specs=[pl.BlockSpec((nb, rb, k * w), lambda b, i: (b, i, 0)) for k, _ in packed]
        + [pl.BlockSpec((nb * rb // CHUNK, 1, w), lambda b, i: flat(b, i) + (0,))],
        out_shape=[jax.ShapeDtypeStruct((bsz, seq, k * w), dt) for k, dt in packed]
        + [jax.ShapeDtypeStruct((bsz * seq // CHUNK, 1, w), F32)],
        compiler_params=_params("parallel", "parallel"),
        name="rwkv_prep",
    )(proj, prev, lp["mu"], lp["w0"], lp["a0"], lp["k_k"], lp["k_a"], lp["r_k"],
      lp["wd"], lp["wa"], lp["wg"], consts["seg"], consts["tri"])


def _head_masks(width):
    lane = lax.broadcasted_iota(jnp.int32, (CHUNK, width), 1)
    row = lax.broadcasted_iota(jnp.int32, (CHUNK, width), 0)
    return lane >> HEAD_SHIFT, row, lane & (HEAD_LANES - 1)


def _block_diag(x, head):
    z = jnp.zeros_like(x)
    return jnp.concatenate([jnp.where(head == j, x, z) for j in range(x.shape[1] // HEAD_LANES)], axis=0)


def _rwkv_stages(ab_ref, rkv_ref, gl_ref, s0_ref, cast_in, y_ref, s_ref, cast_out, bb, nck):
    @pl.when(pl.program_id(1) == 0)
    def _():
        s_ref[...] = s0_ref[...]

    for src, dst in zip(cast_in, cast_out):
        dst[...] = src[...].astype(BF16)

    tw = RWKV_TILE_LANES
    head, row, col = _head_masks(tw)
    strict = col < row
    incl = col <= row
    same8 = (col >> 3) == (row >> 3)
    lane2 = lax.broadcasted_iota(jnp.int32, (tw, tw), 1)
    row2 = lax.broadcasted_iota(jnp.int32, (tw, tw), 0)
    same_head = (lane2 >> HEAD_SHIFT) == (row2 >> HEAD_SHIFT)
    bd = functools.partial(_block_diag, head=head)
    c = CHUNK

    def pmm(p, q):
        return _dg(p.astype(BF16), bd(q.astype(BF16)))

    items = [(b, ck, pr) for b in range(bb) for ck in range(nck) for pr in range(RWKV_TILES)]
    n = range(len(items))
    rows = lambda ck: slice(ck * c, (ck + 1) * c)
    lanes = lambda pr: slice(pr * tw, (pr + 1) * tw)
    w = RWKV_WIDTH

    def ld(ref, slot):
        return [ref[b, rows(ck), slot * w + pr * tw:slot * w + (pr + 1) * tw] for b, ck, pr in items]

    cat0 = lambda *xs: jnp.concatenate(xs, axis=0)
    cat1 = lambda *xs: jnp.concatenate(xs, axis=1)
    at, bt = ld(ab_ref, 0), ld(ab_ref, 1)
    rt, kt, be, ke, v = (ld(rkv_ref, slot) for slot in range(5))
    ats, bts = [_split2(a) for a in at], [_split2(b) for b in bt]
    atb = [hi for hi, _ in ats]
    zero = jnp.zeros((c, tw), F32)
    gk = [_dg(cat0(atb[i], rt[i]), bd(kt[i]), _NT) for i in n]
    aak = [jnp.where(strict, gk[i][:c], zero) for i in n]
    ark = [jnp.where(incl, gk[i][c:], zero) for i in n]
    yield
    gb = [_dg(cat0(ats[i][0], ats[i][1], rt[i]), bd(bts[i][0]), _NT) for i in n]
    arb = [jnp.where(incl, gb[i][2 * c:], zero) for i in n]
    aab = [jnp.where(strict, gb[i][:c] + (gb[i][c:2 * c] + _dg(atb[i], bd(bts[i][1]), _NT)), zero) for i in n]
    yield
    z = [pmm(aak[i], v[i]) for i in n]
    a8 = [jnp.where(same8, aab[i], zero) for i in n]
    p2 = [pmm(a8[i], a8[i]) for i in n]
    yield
    p4 = [pmm(p2[i], p2[i]) for i in n]
    yield
    nn = [a8[i] + p2[i] + pmm(p2[i], a8[i]) for i in n]
    yield
    nn = [nn[i] + p4[i] + pmm(p4[i], nn[i]) for i in n]
    yield
    for lvl in (3, 4, 5):
        joins = ((col >> (lvl + 1)) == (row >> (lvl + 1))) & ((col >> lvl) != (row >> lvl))
        e = [jnp.where(joins, aab[i], zero) for i in n]
        te = [e[i] + pmm(nn[i], e[i]) for i in n]
        yield
        nn = [nn[i] + te[i] + pmm(te[i], nn[i]) for i in n]
        yield
    wu = [cat1(at[i], z[i]) + _dg(nn[i].astype(BF16), cat1(bd(atb[i]), bd(z[i].astype(BF16)))) for i in n]
    abk = [cat1(arb[i], ark[i]).astype(BF16) for i in n]
    yield
    s = {(b, pr): s_ref[b, pr] for b in range(bb) for pr in range(RWKV_TILES)}
    for ck in range(nck):
        cur = [i for i in n if items[i][1] == ck]
        key = lambda i: (items[i][0], items[i][2])
        x = {i: _dg(cat0(wu[i][:, :tw].astype(BF16), rt[i]), s[key(i)].astype(BF16), _NT) for i in cur}
        ub = {i: (x[i][:c] + wu[i][:, tw:]).astype(BF16) for i in cur}
        yield
        upd = {i: _dg(cat0(ub[i], v[i]), cat0(be[i], ke[i]), _TN) for i in cur}
        for i in cur:
            b, _, pr = items[i]
            y_ref[b, rows(ck), lanes(pr)] = x[i][c:] + _dg(abk[i], cat0(bd(ub[i]), bd(v[i])))
            s[b, pr] = (s[b, pr] * gl_ref[b, ck, :, lanes(pr)]
                        + jnp.where(same_head, upd[i], jnp.zeros_like(upd[i])))
        yield
    for (b, pr), val in s.items():
        s_ref[b, pr] = val


def _gla_stages(p_ref, wgate_ref, bgate_ref, tri_ref, ng_ref, s0_ref, y_ref, s_ref, bb, nck):
    @pl.when(pl.program_id(1) == 0)
    def _():
        s_ref[...] = s0_ref[...]

    head, row, col = _head_masks(LANES)
    incl = col <= row
    bd = functools.partial(_block_diag, head=head)
    lane2 = lax.broadcasted_iota(jnp.int32, (2 * GLA_DV, LANES), 1)
    row2 = lax.broadcasted_iota(jnp.int32, (2 * GLA_DV, LANES), 0)
    same_head = (lane2 < GLA_DK) == (row2 < GLA_DV)
    kw, gw = GLA_KEY_WIDTH, GLA_WIDTH
    zero = jnp.zeros((CHUNK, LANES), F32)
    cat0 = lambda xs: jnp.concatenate(xs, axis=0)
    cat1 = lambda xs: jnp.concatenate(xs, axis=1)
    rows = lambda c: slice(c * CHUNK, (c + 1) * CHUNK)
    lanes = lambda pr: slice(pr * LANES, (pr + 1) * LANES)
    bs, cs = range(bb), range(nck)
    x = [p_ref[b] for b in bs]
    gate = [_mm1(x[b][:, 2 * kw + 2 * gw:], wgate_ref[...]) + bgate_ref[...] for b in bs]
    la = [(jnp.minimum(g, 0.0) - jnp.log(1.0 + jnp.exp(-jnp.abs(g)))) / GLA_TAU for g in gate]
    yield
    cum = [_mm_exact_lhs(tri_ref[...], la[b]) for b in bs]
    yield
    q_dec = [(x[b][:, :kw] * (GLA_DK ** -0.5)) * jnp.exp(cum[b]) for b in bs]
    k_inv = [x[b][:, kw:2 * kw] * jnp.exp(-cum[b]) for b in bs]
    last = [[cum[b][c * CHUNK + CHUNK - 1:(c + 1) * CHUNK, :] for c in cs] for b in bs]
    k_end = [[x[b][rows(c), kw:2 * kw] * jnp.exp(last[b][c] - cum[b][rows(c)]) for c in cs] for b in bs]
    dec = [[jnp.exp(last[b][c]) for c in cs] for b in bs]
    yield
    items = [(b, c, pr) for b in bs for c in cs for pr in range(GLA_PAIRS)]
    vp = {(b, c, pr): x[b][rows(c), 2 * kw + pr * 2 * GLA_DV:2 * kw + (pr + 1) * 2 * GLA_DV]
          for b, c, pr in items}
    att = {(b, c, pr): jnp.where(incl, _mm1(q_dec[b][rows(c), lanes(pr)],
                                            bd(k_inv[b][rows(c), lanes(pr)]), _NT), zero)
           for b, c, pr in items}
    yield
    upd = {it: _mm1(vp[it], k_end[it[0]][it[1]][:, lanes(it[2])], _TN) for it in items}
    yield
    intra = {it: _mm1(att[it], cat0([cat1([vp[it][:, :GLA_DV], zero]), cat1([zero, vp[it][:, GLA_DV:]])]))
             for it in items}
    yield
    st = {}
    for b in bs:
        for pr in range(GLA_PAIRS):
            cur = s_ref[b, pr]
            for c in cs:
                st[b, c, pr] = cur
                cur = cur * dec[b][c][:, lanes(pr)] + jnp.where(same_head, upd[b, c, pr], jnp.zeros_like(cur))
            s_ref[b, pr] = cur
    yield
    o = {(b, c, pr): intra[b, c, pr] + _mm1(q_dec[b][rows(c), lanes(pr)], st[b, c, pr], _NT)
         for b, c, pr in items}
    yield
    for b in bs:
        ob = cat0([cat1([o[b, c, pr] for pr in range(GLA_PAIRS)]) for c in cs])
        heads = [ob[:, h * GLA_DV:(h + 1) * GLA_DV] for h in range(GLA_HEADS)]
        normed = [oh * lax.rsqrt(jnp.mean(oh * oh, axis=-1, keepdims=True) + RMS_EPS) for oh in heads]
        y_ref[b] = cat1(normed) * ng_ref[...] * _silu(x[b][:, 2 * kw + gw:2 * kw + 2 * gw])


def _recurrent_kernel(*refs, bb, nck, n_cast):
    rwkv_in, refs = refs[:4], refs[4:]
    cast_in, refs = refs[:n_cast], refs[n_cast:]
    gla_in, refs = refs[:6], refs[6:]
    (y_ref, s_ref), refs = refs[:2], refs[2:]
    cast_out, (yg_ref, t_ref) = refs[:n_cast], refs[n_cast:]
    rwkv = _rwkv_stages(*rwkv_in, cast_in, y_ref, s_ref, cast_out, bb, nck)
    gla = _gla_stages(*gla_in, yg_ref, t_ref, bb, nck)
    done = object()
    live = [rwkv, rwkv, gla]
    while live:
        for gen in list(live):
            if gen in live and next(gen, done) is done:
                live = [g for g in live if g is not gen]


def _recurrent(ab, rkv, gl, s0, proj, lp, tri, t0, bb, nck, casts=()):
    bsz, seq, _ = ab.shape
    w = RWKV_WIDTH
    rows = nck * CHUNK
    n_seq_steps = seq // rows
    n_steps = (bsz // bb) * n_seq_steps
    tok = lambda a: pl.BlockSpec((bb, rows, a.shape[-1]), lambda i, c: (i, c, 0))
    full = lambda a: pl.BlockSpec(a.shape, lambda i, c: (0,) * a.ndim)
    st = pl.BlockSpec((bb, RWKV_TILES, RWKV_TILE_LANES, RWKV_TILE_LANES), lambda i, c: (i, 0, 0, 0))
    gst = pl.BlockSpec((bb, GLA_PAIRS, 2 * GLA_DV, LANES), lambda i, c: (i, 0, 0, 0))
    slab = lambda a: pl.BlockSpec((a.shape[0] // n_steps, a.shape[1]), lambda i, c: (i * n_seq_steps + c, 0))
    assert all(a.shape[0] % (16 * n_steps) == 0 for a in casts)
    out_tok = pl.BlockSpec((bb, rows, w), lambda i, c: (i, c, 0))
    return pl.pallas_call(
        functools.partial(_recurrent_kernel, bb=bb, nck=nck, n_cast=len(casts)),
        grid=(bsz // bb, n_seq_steps),
        in_specs=[tok(ab), tok(rkv), pl.BlockSpec((bb, nck, 1, w), lambda i, c: (i, c, 0, 0)), st]
        + [slab(a) for a in casts]
        + [pl.BlockSpec((bb, rows, GLA_PROJ_PAD), lambda i, c: (i, c, 1)),
           full(lp["wgate"]), full(lp["bgate"]), full(tri), full(lp["gla_g"]), gst],
        out_specs=[out_tok, st] + [slab(a) for a in casts] + [out_tok, gst],
        out_shape=[jax.ShapeDtypeStruct((bsz, seq, w), F32),
                   jax.ShapeDtypeStruct((bsz, RWKV_TILES, RWKV_TILE_LANES, RWKV_TILE_LANES), F32)]
        + [jax.ShapeDtypeStruct(a.shape, BF16) for a in casts]
        + [jax.ShapeDtypeStruct((bsz, seq, GLA_WIDTH), F32),
           jax.ShapeDtypeStruct((bsz, GLA_PAIRS, 2 * GLA_DV, LANES), F32)],
        compiler_params=_params("parallel", "arbitrary"),
        name="rwkv_gla_chunk",
    )(ab, rkv, gl, s0, *casts, proj, lp["wgate"], lp["bgate"], tri, lp["gla_g"], t0)


def _route(lgt):
    tm = lgt.shape[1]
    n_pad = ROUTE_ROWS - N_EXPERTS
    rowg = lax.broadcasted_iota(jnp.int32, (n_pad, tm), 0)
    rowe = lax.broadcasted_iota(jnp.int32, (N_EXPERTS, tm), 0)
    rowg_f, rowe_f = rowg.astype(F32), rowe.astype(F32)

    def first_argmax(vals, mx, rows_f):
        return jnp.min(jnp.where(vals == mx, rows_f, float(ROUTE_ROWS)), axis=0, keepdims=True)

    is_group = rowg < N_GROUPS
    lg = jnp.where(is_group, lgt[N_EXPERTS:], -jnp.inf)
    gmax = jnp.max(lg, axis=0, keepdims=True)
    gi = first_argmax(lg, gmax, rowg_f)
    pg_top = 1.0 / jnp.sum(jnp.where(is_group, jnp.exp(lg - gmax), 0.0), axis=0, keepdims=True)
    in_group = (rowe >> GROUP_SHIFT).astype(F32) == gi
    le = jnp.where(in_group, lgt[:N_EXPERTS], -jnp.inf)
    m1 = jnp.max(le, axis=0, keepdims=True)
    i1 = first_argmax(le, m1, rowe_f)
    le2 = jnp.where(rowe_f == i1, -jnp.inf, le)
    m2 = jnp.max(le2, axis=0, keepdims=True)
    i2 = first_argmax(le2, m2, rowe_f)
    p2 = jnp.exp(m2 - m1)
    w1 = pg_top / (1.0 + p2)
    w2 = pg_top * p2 / (1.0 + p2)
    comb = jnp.where(rowe_f == i1, w1, 0.0) + jnp.where(rowe_f == i2, w2, 0.0)
    return jnp.concatenate([comb, jnp.where(rowg == 0, gi, 0.0)], axis=0)


def _outproj_kernel(yr_ref, bg_ref, yg_ref, x_ref, gt_ref, sc_ref, sh_ref,
                    lnw_ref, lnb_ref, seg_ref, wout_ref, g2_ref, wr_ref, br_ref,
                    x1_ref, h2_ref, route_ref):
    nb, rb, d_model = x_ref.shape
    tm = nb * rb
    flat = lambda ref: ref[...].reshape(tm, ref.shape[-1])
    seg = seg_ref[...]
    y = flat(yr_ref)
    inv_n = 1.0 / RWKV_HEAD
    d = y - _mm2_exact_rhs(y, seg) * inv_n
    var = _mm2_exact_rhs(d * d, seg) * inv_n
    yn = d * lax.rsqrt(var + LNX_EPS) * lnw_ref[...] + lnb_ref[...]
    bg = flat(bg_ref)
    yr = (yn + bg[:, :RWKV_WIDTH]) * bg[:, RWKV_WIDTH:]
    mix = jnp.concatenate([yr, flat(yg_ref)], axis=1)
    x1 = x_ref[...] + gt_ref[...] * _mm1(mix, wout_ref[...]).reshape(nb, rb, d_model)
    x1_ref[...] = x1
    h2 = _rms_mod(x1, g2_ref[...], sc_ref[...], sh_ref[...])
    h2_ref[...] = h2.astype(BF16)
    route_ref[...] = _route(_mm3(wr_ref[...], h2.reshape(tm, d_model), _NT) + br_ref[...])


def _out_proj(yr, bg, yg, x, gt, sc, sh, lp, consts, rows):
    bsz, seq, d = x.shape
    nb, rb = _token_tile(bsz, seq, rows)
    n_seq_tiles = seq // rb
    half = pl.BlockSpec((nb, rb, RWKV_WIDTH), lambda b, i: (b, i, 0))
    tokd = pl.BlockSpec((nb, rb, d), lambda b, i: (b, i, 0))
    vec = pl.BlockSpec((nb, 1, d), lambda b, i: (b, 0, 0))
    full = lambda a: pl.BlockSpec(a.shape, lambda b, i: (0,) * a.ndim)
    args = (lp["lnx_w"], lp["lnx_b"], consts["seg"], lp["w_out"], lp["norm2_g"], lp["w_router"], lp["b_router"])
    return pl.pallas_call(
        _outproj_kernel,
        grid=(bsz // nb, seq // rb),
        in_specs=[half, pl.BlockSpec((nb, rb, 2 * RWKV_WIDTH), lambda b, i: (b, i, 0)), half, tokd,
                  vec, vec, vec] + [full(a) for a in args],
        out_specs=[tokd, tokd, pl.BlockSpec((ROUTE_ROWS, nb * rb), lambda b, i: (0, b * n_seq_tiles + i))],
        out_shape=[jax.ShapeDtypeStruct((bsz, seq, d), F32),
                   jax.ShapeDtypeStruct((bsz, seq, d), BF16),
                   jax.ShapeDtypeStruct((ROUTE_ROWS, bsz * seq), F32)],
        compiler_params=_params("parallel", "parallel"),
        name="out_proj_router",
    )(yr, bg, yg, x, gt, sc, sh, *args)


def _moe_kernel(h_ref, route_ref, x1_ref, gt_ref, sc_ref, sh_ref, gf_ref, tri_ref, wg_ref, wu_ref, wd_ref,
                y_ref, ys_ref, pos_ref, meta_ref):
    g = pl.program_id(2)
    nb, rb, d = h_ref.shape
    tm = nb * rb
    sub, tail = MOE_SUB_ROWS, MOE_SUB_ROWS // 2
    cap = ys_ref.shape[0]
    n_lane_tiles = tm // LANES
    lane_tile = lambda k: slice(k * LANES, (k + 1) * LANES)

    @pl.when(g == 0)
    def _sort():
        row8 = lax.broadcasted_iota(jnp.int32, (8, LANES), 0).astype(F32)
        carry = jnp.zeros((8, 1), F32)
        members, ranks = [], []
        for k in range(n_lane_tiles):
            blk = jnp.where(row8 == route_ref[N_EXPERTS:N_EXPERTS + 1, lane_tile(k)], 1.0, 0.0)
            members.append(blk)
            ranks.append(_dg(blk.astype(BF16), tri_ref[...]) - blk + carry)
            carry = carry + jnp.sum(blk, axis=1, keepdims=True)
        rowc = lax.broadcasted_iota(jnp.int32, (8, 1), 0)
        lane = lax.broadcasted_iota(jnp.int32, (1, LANES), 1)
        first = jnp.zeros((1, 1), F32)
        off_col = jnp.zeros((8, 1), F32)
        meta = jnp.zeros((1, LANES), F32)
        for grp in range(N_GROUPS):
            count = jnp.sum(jnp.where(rowc == grp, carry, 0.0), axis=0, keepdims=True)
            blocks = jnp.ceil(count * (1.0 / tail))
            pairs = jnp.floor(blocks * 0.5)
            odd = blocks - 2.0 * pairs
            triple = odd * jnp.where(blocks >= 3.0, 1.0, 0.0)
            n_full = pairs - triple
            off_col = off_col + jnp.where(rowc == grp, first, 0.0)
            meta = (meta + jnp.where(lane == grp, first, 0.0) + jnp.where(lane == N_GROUPS + grp, n_full, 0.0)
                    + jnp.where(lane == 2 * N_GROUPS + grp, odd + 2.0 * triple, 0.0))
            first = first + blocks * tail
        meta = (meta + jnp.where(lane == 3 * N_GROUPS, first, 0.0)).astype(jnp.int32)
        for i in range(3 * N_GROUPS + 1):
            meta_ref[i] = meta[0, i]
        for k in range(n_lane_tiles):
            pos = jnp.sum(members[k] * (ranks[k] + off_col), axis=0, keepdims=True)
            pos_ref[:, lane_tile(k)] = pos.astype(jnp.int32)
        ys_ref[...] = jnp.zeros_like(ys_ref)

    def experts(base, n_rows):
        prow = lax.broadcasted_iota(jnp.int32, (n_rows, tm), 0) + base
        onehot = jnp.where(prow == pos_ref[...], 1.0, 0.0).astype(BF16)
        hs = _dg(onehot, h_ref[...].reshape(tm, d)).astype(BF16)
        comb3 = _split3(route_ref[:N_EXPERTS, :])
        cs = _dg(onehot, comb3[0], _NT) + (_dg(onehot, comb3[1], _NT) + _dg(onehot, comb3[2], _NT))
        lane = lax.broadcasted_iota(jnp.int32, cs.shape, 1)
        ys = jnp.zeros((n_rows, d), F32)
        for e in range(EXPERTS_PER_GROUP):
            hid = _silu(_dg(hs, wg_ref[e])) * _dg(hs, wu_ref[e])
            ce = jnp.sum(jnp.where(lane == g * EXPERTS_PER_GROUP + e, cs, 0.0), axis=-1, keepdims=True)
            ys = ys + ce * _dg(hid.astype(BF16), wd_ref[e])
        ys_ref[pl.ds(base, n_rows), :] = ys.astype(BF16)

    first_row = meta_ref[g]
    n_full = meta_ref[N_GROUPS + g]

    def full_sub_tile(j, carry_):
        experts(pl.multiple_of(first_row + j * sub, tail), sub)
        return carry_

    lax.fori_loop(0, n_full, full_sub_tile, 0)

    for last_blocks in (1, 3):
        @pl.when(meta_ref[2 * N_GROUPS + g] == last_blocks)
        def _last(last_blocks=last_blocks):
            experts(pl.multiple_of(first_row + n_full * sub, tail), last_blocks * tail)

    def unsort(n_sorted):
        ys_all = ys_ref[:n_sorted]
        prow = lax.broadcasted_iota(jnp.int32, (n_sorted, LANES), 0)
        seqs = max(LANES // rb, 1)
        for k in range(n_lane_tiles):
            onehot = jnp.where(prow == pos_ref[:, lane_tile(k)], 1.0, 0.0).astype(BF16)
            moe = _dg(onehot, ys_all, _TN).reshape(seqs, LANES // seqs, d)
            b0 = k * LANES // rb
            r0 = k * LANES - b0 * rb
            bs, rs = slice(b0, b0 + seqs), slice(r0, r0 + LANES // seqs)
            x2 = x1_ref[bs, rs] + gt_ref[bs] * moe
            y_ref[bs, rs] = _rms_mod(x2, gf_ref[...], sc_ref[bs], sh_ref[bs])

    usual = tm + (cap - tm) // 2
    last = g == N_GROUPS - 1
    pl.when(last & (meta_ref[3 * N_GROUPS] <= usual))(lambda: unsort(usual))
    pl.when(last & (meta_ref[3 * N_GROUPS] > usual))(lambda: unsort(cap))


def _moe(h2, route, x1, gt, sc, sh, gf, wg, wu, wd, tri, rows):
    bsz, seq, d = x1.shape
    nb, rb = _token_tile(bsz, seq, rows)
    tm = nb * rb
    assert tm % LANES == 0 and (rb % LANES == 0 or LANES % rb == 0)
    n_seq_tiles = seq // rb
    cap = tm + N_GROUPS * (MOE_SUB_ROWS // 2)
    tokd = pl.BlockSpec((nb, rb, d), lambda b, i, g: (b, i, 0))
    vec = pl.BlockSpec((nb, 1, d), lambda b, i, g: (b, 0, 0))
    group_w = lambda shape: pl.BlockSpec((EXPERTS_PER_GROUP,) + shape, lambda b, i, g: (g, 0, 0))
    return pl.pallas_call(
        _moe_kernel,
        grid=(bsz // nb, n_seq_tiles, N_GROUPS),
        in_specs=[tokd, pl.BlockSpec((ROUTE_ROWS, tm), lambda b, i, g: (0, b * n_seq_tiles + i)),
                  tokd, vec, vec, vec,
                  pl.BlockSpec((1, d), lambda b, i, g: (0, 0)),
                  pl.BlockSpec(tri.shape, lambda b, i, g: (0, 0)),
                  group_w((d, D_EXPERT)), group_w((d, D_EXPERT)), group_w((D_EXPERT, d))],
        out_specs=tokd,
        out_shape=jax.ShapeDtypeStruct((bsz, seq, d), F32),
        scratch_shapes=[pltpu.VMEM((cap, d), BF16), pltpu.VMEM((1, tm), jnp.int32),
                        pltpu.SMEM((3 * N_GROUPS + 1,), jnp.int32)],
        compiler_params=_params("parallel", "parallel", "arbitrary"),
        name="moe_final_norm",
    )(h2, route, x1, gt, sc, sh, gf, tri, wg, wu, wd)


def _block_ones(n, blk, lower):
    i = np.arange(n)
    m = (i[:, None] // blk) == (i[None, :] // blk)
    if lower:
        m = m & (i[None, :] <= i[:, None])
    return jnp.asarray(m, dtype=BF16)


def _consts(tr, tg):
    return dict(seg=_block_ones(RWKV_WIDTH, RWKV_HEAD, False), tri=_block_ones(tr, CHUNK, True),
                tri_gla=_block_ones(tg, CHUNK, True), tri_up=_block_ones(LANES, LANES, True).T)


def _heads_block_diag(s):
    n, c = s.shape[2], s.shape[-1]
    rows = [jnp.pad(s[:, :, j], ((0, 0), (0, 0), (0, 0), (j * c, (n - 1 - j) * c))) for j in range(n)]
    return jnp.concatenate(rows, axis=2)


def _diag_blocks(t, n):
    r, c = t.shape[-2] // n, t.shape[-1] // n
    return jnp.stack([t[:, :, j * r:(j + 1) * r, j * c:(j + 1) * c] for j in range(n)], axis=2)


def _pad_rows(w, first_row):
    out = jnp.zeros((LORA_PAD, w.shape[1]), F32)
    return lax.dynamic_update_slice(out, w, (first_row, 0)).astype(BF16)


def _layer_params(l, w_in, mu_shift, w0, w_decay_up, a0, w_a_up, w_g_up, k_k, k_a, r_k, lnx_w, lnx_b,
                  w_gla_gate_up, b_gla_gate, gla_norm_g, w_out, norm2_g,
                  w_router_group, b_router_group, w_router_expert, b_router_expert):
    w_in_p = _win_layout(w_in[l].T, WEIGHT_TILE_COLS)
    n_pad = ROUTE_ROWS - N_EXPERTS - N_GROUPS
    w_router = jnp.concatenate([w_router_expert[l].T, w_router_group[l].T, jnp.zeros((n_pad, D_MODEL), F32)])
    b_router = jnp.concatenate([b_router_expert[l], b_router_group[l],
                                jnp.zeros((n_pad,), F32)]).reshape(ROUTE_ROWS, 1)
    wgate = jnp.zeros((LANES, GLA_KEY_WIDTH), F32).at[:GLA_GATE_RANK].set(w_gla_gate_up[l]).astype(BF16)
    r1 = lambda a: a.reshape(1, -1)
    return dict(
        w_in=w_in_p, mu=r1(mu_shift[l]), w0=r1(w0[l]), a0=r1(a0[l]), k_k=r1(k_k[l]), k_a=r1(k_a[l]),
        r_k=r1(r_k[l]), wd=_pad_rows(w_decay_up[l], 0), wa=_pad_rows(w_a_up[l], DECAY_LORA),
        wg=_pad_rows(w_g_up[l], DECAY_LORA + AAA_LORA), lnx_w=r1(lnx_w[l]), lnx_b=r1(lnx_b[l]),
        wgate=wgate, bgate=r1(b_gla_gate[l]), gla_g=r1(gla_norm_g[l]),
        w_out=w_out[l].astype(BF16), norm2_g=r1(norm2_g[l]), w_router=w_router, b_router=b_router)


def _run_layer(x, mod, shift0, wkv0, gla0, lp, experts, final):
    bsz, seq, d = x.shape
    assert seq % CHUNK == 0
    bb = 2 if seq >= RECURRENT_STEP_ROWS else min(bsz, RECURRENT_MAX_SEQS)
    assert bsz % bb == 0
    nck = min(seq, RECURRENT_STEP_ROWS) // CHUNK
    consts = _consts(TOKEN_TILE_ROWS, nck * CHUNK)
    m = lambda j: mod[:, j:j + 1, :]
    sh1, sc1, gt1, sh2, sc2, gt2 = (m(j) for j in range(6))
    proj, tails = _in_proj(x, sc1, sh1, lp["norm1_g"], lp["w_in"], PROJ_TILE_ROWS, TOKEN_TILE_ROWS)
    tails = tails.reshape(bsz, -1, RWKV_PROJ)
    new_shift = tails[:, -1]
    prev = jnp.concatenate([shift0[:, None, :], tails[:, :-1]], axis=1).reshape(-1, 1, RWKV_PROJ)
    ab, rkv, bg, gl = _rwkv_prep(proj, prev, lp, consts, TOKEN_TILE_ROWS)
    gl = gl.reshape(bsz, seq // CHUNK, 1, RWKV_WIDTH)
    s0 = _heads_block_diag(wkv0.reshape(bsz, RWKV_TILES, RWKV_TILE_HEADS, RWKV_HEAD, RWKV_HEAD))
    t0 = _heads_block_diag(jnp.swapaxes(gla0, -1, -2).reshape(bsz, GLA_PAIRS, 2, GLA_DV, GLA_DK))
    casts = tuple(e.reshape(-1, e.shape[-1]) for e in experts if e.dtype != BF16)
    yr, s_bd, *cast, yg, t_bd = _recurrent(ab, rkv, gl, s0, proj, lp, consts["tri_gla"], t0,
                                           bb, nck, casts)
    if cast:
        experts = tuple(c.reshape(e.shape) for c, e in zip(cast, experts))
    new_wkv = _diag_blocks(s_bd, RWKV_TILE_HEADS).reshape(bsz, RWKV_HEADS, RWKV_HEAD, RWKV_HEAD)
    new_gla = jnp.swapaxes(_diag_blocks(t_bd, 2).reshape(bsz, GLA_HEADS, GLA_DV, GLA_DK), -1, -2)
    x1, h2, route = _out_proj(yr, bg, yg, x, gt1, sc2, sh2, lp, consts, PROJ_TILE_ROWS)
    out = _moe(h2, route, x1, gt2, *final, *experts, consts["tri_up"], MOE_TILE_ROWS)
    return experts, out, new_shift, new_wkv, new_gla


def kernel(x_prompt, x_sample, c_prompt, c_sample, state_rwkv_shift, state_rwkv_wkv, state_gla_kv, w_ada, b_ada, norm1_g, norm2_g, w_in, mu_shift, w0, w_decay_up, a0, w_a_up, w_g_up, k_k, k_a, r_k, lnx_w, lnx_b, w_gla_gate_up, b_gla_gate, gla_norm_g, w_out, w_router_group, b_router_group, w_router_expert, b_router_expert, w_expert_gate, w_expert_up, w_expert_down, w_ada_final, b_ada_final, normf_g):
    assert w_ada.shape[0] == 1, "the final norm is fused into the single layer's MoE kernel"
    bp, bs = x_prompt.shape[0], x_sample.shape[0]
    d = D_MODEL
    n_rows = -(-(bp + bs) // 8) * 8
    c_all = jnp.zeros((n_rows, d), F32).at[:bp].set(c_prompt).at[bp:bp + bs].set(c_sample)
    modf = _modulation(c_all, w_ada_final, b_ada_final, MODULATION_TILE_COLS).reshape(n_rows, 2, d)
    mod = _modulation(c_all, w_ada, b_ada[0], MODULATION_TILE_COLS, layer=0).reshape(n_rows, 6, d)
    lp = _layer_params(0, w_in, mu_shift, w0, w_decay_up, a0, w_a_up, w_g_up, k_k, k_a, r_k, lnx_w,
                       lnx_b, w_gla_gate_up, b_gla_gate, gla_norm_g, w_out, norm2_g,
                       w_router_group, b_router_group, w_router_expert, b_router_expert)
    lp["norm1_g"] = norm1_g[0].reshape(1, d)
    experts = (w_expert_gate[0], w_expert_up[0], w_expert_down[0])
    groups = [
        (x_prompt, 0, bp, jnp.zeros((bp, RWKV_PROJ), F32),
         jnp.zeros((bp, RWKV_HEADS, RWKV_HEAD, RWKV_HEAD), F32), jnp.zeros((bp, GLA_HEADS, GLA_DK, GLA_DV), F32)),
        (x_sample, bp, bp + bs, state_rwkv_shift[0], state_rwkv_wkv[0], state_gla_kv[0]),
    ]
    ys, states = [], []
    for x, lo, hi, shift0, wkv0, gla0 in groups:
        final = (modf[lo:hi, 1:2], modf[lo:hi, 0:1], normf_g.reshape(1, d))
        experts, y, *st = _run_layer(x, mod[lo:hi], shift0, wkv0, gla0, lp, experts, final)
        ys.append(y)
        states.extend(s[None] for s in st)
    return tuple(ys + states)
```

```python
import functools

import jax
import jax.numpy as jnp
import numpy as np
from jax import lax
from jax.experimental import pallas as pl
from jax.experimental.pallas import tpu as pltpu

F32 = jnp.float32
BF16 = jnp.bfloat16

LANES = 128
VMEM_LIMIT_BYTES = 56 * 1024 * 1024
TOKEN_TILE_ROWS = 512
PROJ_TILE_ROWS = 1024
OUT_PROJ_PARTS = 2
MOE_TILE_ROWS = 1024
MOE_SUB_ROWS = 256
RECURRENT_STEP_ROWS = 256
RECURRENT_MAX_SEQS = 8
MODULATION_TILE_COLS = 1024
WEIGHT_TILE_COLS = 256

D_MODEL = 1024
CHUNK = 64
RWKV_WIDTH = 512
RWKV_HEAD = 64
RWKV_HEADS = RWKV_WIDTH // RWKV_HEAD
HEAD_LANES = RWKV_HEAD
HEAD_SHIFT = HEAD_LANES.bit_length() - 1
assert 1 << HEAD_SHIFT == HEAD_LANES
RWKV_TILE_LANES = 128
RWKV_TILE_HEADS = RWKV_TILE_LANES // RWKV_HEAD
RWKV_TILES = RWKV_WIDTH // RWKV_TILE_LANES
DECAY_LORA = 32
AAA_LORA = 32
GATE_LORA = 64
LORA_PAD = DECAY_LORA + AAA_LORA + GATE_LORA
RWKV_PROJ = 3 * RWKV_WIDTH + LORA_PAD
GLA_WIDTH = 512
GLA_HEADS = 4
GLA_PAIRS = GLA_HEADS // 2
GLA_DV = GLA_WIDTH // GLA_HEADS
GLA_DK = GLA_DV // 2
GLA_KEY_WIDTH = GLA_HEADS * GLA_DK
GLA_GATE_RANK = 16
GLA_TAU = 16.0
GLA_PROJ = 2 * GLA_KEY_WIDTH + 2 * GLA_WIDTH + GLA_GATE_RANK
GLA_PROJ_PAD = RWKV_PROJ
IN_PROJ_PAD = RWKV_PROJ + GLA_PROJ_PAD
N_GROUPS = 4
EXPERTS_PER_GROUP = 4
N_EXPERTS = N_GROUPS * EXPERTS_PER_GROUP
GROUP_SHIFT = EXPERTS_PER_GROUP.bit_length() - 1
assert 1 << GROUP_SHIFT == EXPERTS_PER_GROUP
ROUTE_ROWS = 24
D_EXPERT = 512
RMS_EPS = 1e-6
LNX_EPS = 64e-5
LOG2_E = 1.4426950408889634

_NN = (((1,), (0,)), ((), ()))
_NT = (((1,), (1,)), ((), ()))
_TN = (((0,), (0,)), ((), ()))


def _dg(a, b, dims=_NN):
    return lax.dot_general(a, b, dims, preferred_element_type=F32)


def _split2(x):
    hi = x.astype(BF16)
    lo = (x - hi.astype(F32)).astype(BF16)
    return hi, lo


def _split3(x):
    hi = x.astype(BF16)
    r1 = x - hi.astype(F32)
    mid = r1.astype(BF16)
    lo = (r1 - mid.astype(F32)).astype(BF16)
    return hi, mid, lo


def _mm1(a, b, dims=_NN):
    return _dg(a.astype(BF16), b.astype(BF16), dims)


def _mm3(a, b, dims=_NN):
    ah, al = _split2(a)
    bh, bl = _split2(b)
    return _dg(ah, bh, dims) + (_dg(ah, bl, dims) + _dg(al, bh, dims))


def _mm_exact_lhs(e, x, dims=_NN):
    h, m, l = _split3(x)
    return _dg(e, h, dims) + (_dg(e, m, dims) + _dg(e, l, dims))


def _mm2_exact_rhs(x, e, dims=_NN):
    h, l = _split2(x)
    return _dg(h, e, dims) + _dg(l, e, dims)


def _softplus(z):
    return jnp.maximum(z, 0.0) + jnp.log(1.0 + jnp.exp(-jnp.abs(z)))


def _sigmoid(z):
    return 1.0 / (1.0 + jnp.exp(-z))


def _silu(z):
    return z * _sigmoid(z)


def _params(*sem):
    return pltpu.CompilerParams(dimension_semantics=sem, vmem_limit_bytes=VMEM_LIMIT_BYTES)


def _mod_kernel(c_ref, w_ref, b_ref, o_ref):
    o_ref[...] = _mm1(_silu(c_ref[...]), w_ref[...]) + b_ref[...]


def _modulation(c, w, b, tn, layer=None):
    rows, d = c.shape
    n = w.shape[-1]
    if layer is None:
        w_spec = pl.BlockSpec((d, tn), lambda j: (0, j))
    else:
        w_spec = pl.BlockSpec((None, d, tn), lambda j: (layer, 0, j))
    return pl.pallas_call(
        _mod_kernel,
        grid=(n // tn,),
        in_specs=[pl.BlockSpec((rows, d), lambda j: (0, 0)), w_spec,
                  pl.BlockSpec((1, tn), lambda j: (0, j))],
        out_specs=pl.BlockSpec((rows, tn), lambda j: (0, j)),
        out_shape=jax.ShapeDtypeStruct((rows, n), F32),
        compiler_params=_params("parallel"),
        name="modulation",
    )(c, w, b.reshape(1, n))


def _win_layout_kernel(w_ref, o_ref):
    qkv_end = RWKV_PROJ + 2 * GLA_KEY_WIDTH + GLA_WIDTH
    gate_end = qkv_end + GLA_GATE_RANK
    o_ref[:qkv_end] = w_ref[:qkv_end].astype(BF16)
    o_ref[qkv_end:qkv_end + GLA_WIDTH] = w_ref[gate_end:gate_end + GLA_WIDTH].astype(BF16)
    o_ref[qkv_end + GLA_WIDTH:RWKV_PROJ + GLA_PROJ] = w_ref[qkv_end:gate_end].astype(BF16)
    o_ref[RWKV_PROJ + GLA_PROJ:] = jnp.zeros((IN_PROJ_PAD - RWKV_PROJ - GLA_PROJ, o_ref.shape[1]), BF16)


def _win_layout(wt, cols):
    n, d = wt.shape
    return pl.pallas_call(
        _win_layout_kernel,
        grid=(d // cols,),
        in_specs=[pl.BlockSpec((n, cols), lambda i: (0, i))],
        out_specs=pl.BlockSpec((IN_PROJ_PAD, cols), lambda i: (0, i)),
        out_shape=jax.ShapeDtypeStruct((IN_PROJ_PAD, d), BF16),
        compiler_params=_params("parallel"),
        name="w_in_layout",
    )(wt)


def _rms_mod(x, g, sc, sh):
    ms = jnp.mean(x * x, axis=-1, keepdims=True)
    return (x * lax.rsqrt(ms + RMS_EPS) * g) * (1.0 + sc) + sh


def _token_tile(bsz, seq, rows):
    if seq >= rows:
        assert seq % rows == 0
        return 1, rows
    nb = min(bsz, rows // seq)
    assert bsz % nb == 0
    return nb, seq


def _inproj_kernel(x_ref, sc_ref, sh_ref, g_ref, w_ref, o_ref, last_ref, *, n_step, n_slices):
    nb, rb, d = x_ref.shape
    h = _rms_mod(x_ref[...], g_ref[...], sc_ref[...], sh_ref[...])
    hb = h.reshape(nb * rb, d).astype(BF16)
    for j in range(IN_PROJ_PAD // n_step):
        cols = slice(j * n_step, (j + 1) * n_step)
        o_ref[:, :, cols] = _dg(hb, w_ref[cols, :], _NT).reshape(nb, rb, n_step)
    rows = rb // n_slices
    for j in range(n_slices):
        last_ref[j * nb:(j + 1) * nb] = o_ref[:, (j + 1) * rows - 1:(j + 1) * rows, :RWKV_PROJ]


def _in_proj(x, sc, sh, g, w, rows, slice_rows):
    bsz, seq, d = x.shape
    nb, rb = _token_tile(bsz, seq, rows)
    n_seq_tiles = seq // rb
    n_slices = max(rb // slice_rows, 1)
    assert nb == 1 or n_slices == 1
    vec = pl.BlockSpec((nb, 1, d), lambda b, i: (b, 0, 0))
    return pl.pallas_call(
        functools.partial(_inproj_kernel, n_step=2 * LANES, n_slices=n_slices),
        grid=(bsz // nb, n_seq_tiles),
        in_specs=[pl.BlockSpec((nb, rb, d), lambda b, i: (b, i, 0)), vec, vec,
                  pl.BlockSpec((1, d), lambda b, i: (0, 0)),
                  pl.BlockSpec((IN_PROJ_PAD, d), lambda b, i: (0, 0))],
        out_specs=[pl.BlockSpec((nb, rb, IN_PROJ_PAD), lambda b, i: (b, i, 0)),
                   pl.BlockSpec((nb * n_slices, 1, RWKV_PROJ), lambda b, i: (b * n_seq_tiles + i, 0, 0))],
        out_shape=[jax.ShapeDtypeStruct((bsz, seq, IN_PROJ_PAD), F32),
                   jax.ShapeDtypeStruct((bsz * n_seq_tiles * n_slices, 1, RWKV_PROJ), F32)],
        compiler_params=_params("parallel", "parallel"),
        name="norm1_in_proj",
    )(x, sc, sh, g, w)


def _rwkv_prep_kernel(p_ref, prev_ref, mu_ref, w0_ref, a0_ref, kk_ref, ka_ref, rk_ref,
                      wd_ref, wa_ref, wg_ref, seg_ref, tri_ref,
                      ab_ref, rkv_ref, bg_ref, gl_ref):
    nb, rb, wp = p_ref.shape
    tr = nb * rb
    p = p_ref[...].reshape(tr, wp)
    row = lax.broadcasted_iota(jnp.int32, (nb, rb, wp), 1)
    xx = jnp.where(row == 0, prev_ref[...], pltpu.roll(p, 1, 0).reshape(nb, rb, wp)).reshape(tr, wp)
    ps = p + (xx - p) * mu_ref[...]
    w = RWKV_WIDTH
    r, k, v, lora = ps[:, :w], ps[:, w:2 * w], ps[:, 2 * w:3 * w], ps[:, 3 * w:]
    logw = -_softplus(-(w0_ref[...] + _mm1(jnp.tanh(lora), wd_ref[...]))) - 0.5
    lw = jnp.exp(logw) * (-LOG2_E)
    a = _sigmoid(a0_ref[...] + _mm1(lora, wa_ref[...]))
    g = _mm1(_sigmoid(lora), wg_ref[...])
    seg = seg_ref[...]
    kk = k * kk_ref[...]
    kk = kk * lax.rsqrt(jnp.maximum(_mm2_exact_rhs(kk * kk, seg), 1e-24))
    k2 = k * (1.0 + (a - 1.0) * ka_ref[...])
    cum = _mm_exact_lhs(tri_ref[...], lw)
    lasts = [jnp.exp2(cum[c * CHUNK + CHUNK - 1:(c + 1) * CHUNK, :]) for c in range(tr // CHUNK)]
    for c, l in enumerate(lasts):
        gl_ref[c] = l

    def to_chunk_end(val):
        return jnp.concatenate([val[c * CHUNK:(c + 1) * CHUNK] * l for c, l in enumerate(lasts)], axis=0)

    gamma = jnp.exp2(cum)
    ginv = 1.0 / gamma
    bt = (kk * a) * ginv
    kt = k2 * ginv

    def put(ref, slot, val):
        ref[:, :, slot * w:(slot + 1) * w] = val.reshape(nb, rb, w).astype(ref.dtype)

    put(ab_ref, 0, -kk * jnp.exp2(cum - lw))
    put(ab_ref, 1, bt)
    put(rkv_ref, 0, r * gamma)
    put(rkv_ref, 1, kt)
    put(rkv_ref, 2, to_chunk_end(kt))
    put(rkv_ref, 3, v)
    put(bg_ref, 0, _mm2_exact_rhs(r * k2 * rk_ref[...], seg) * v)
    put(bg_ref, 1, g)


def _rwkv_prep(proj, prev, lp, consts, rows):
    bsz, seq, _ = proj.shape
    w = RWKV_WIDTH
    nb, rb = _token_tile(bsz, seq, rows)
    n_seq_tiles = seq // rb
    flat = lambda b, i: (b * n_seq_tiles + i, 0)
    row = lambda n: pl.BlockSpec((1, n), lambda b, i: (0, 0))
    full = lambda a: pl.BlockSpec(a.shape, lambda b, i: (0,) * a.ndim)
    packed = [(2, F32), (4, BF16), (2, F32)]
    return pl.pallas_call(
        _rwkv_prep_kernel,
        grid=(bsz // nb, n_seq_tiles),
        in_specs=[pl.BlockSpec((nb, rb, RWKV_PROJ), lambda b, i: (b, i, 0)),
                  pl.BlockSpec((nb, 1, RWKV_PROJ), lambda b, i: flat(b, i) + (0,)),
                  row(RWKV_PROJ), row(w), row(w), row(w), row(w), row(w),
                  full(lp["wd"]), full(lp["wa"]), full(lp["wg"]), full(consts["seg"]), full(consts["tri"])],
        out_specs=[pl.BlockSpec((nb, rb, k * w), lambda b, i: (b, i, 0)) for k, _ in packed]
        + [pl.BlockSpec((nb * rb // CHUNK, 1, w), lambda b, i: flat(b, i) + (0,))],
        out_shape=[jax.ShapeDtypeStruct((bsz, seq, k * w), dt) for k, dt in packed]
        + [jax.ShapeDtypeStruct((bsz * seq // CHUNK, 1, w), F32)],
        compiler_params=_params("parallel", "parallel"),
        name="rwkv_prep",
    )(proj, prev, lp["mu"], lp["w0"], lp["a0"], lp["k_k"], lp["k_a"], lp["r_k"],
      lp["wd"], lp["wa"], lp["wg"], consts["seg"], consts["tri"])


def _head_masks(width):
    lane = lax.broadcasted_iota(jnp.int32, (CHUNK, width), 1)
    row = lax.broadcasted_iota(jnp.int32, (CHUNK, width), 0)
    return lane >> HEAD_SHIFT, row, lane & (HEAD_LANES - 1)


def _block_diag(x, head):
    z = jnp.zeros_like(x)
    return jnp.concatenate([jnp.where(head == j, x, z) for j in range(x.shape[1] // HEAD_LANES)], axis=0)


def _rwkv_stages(ab_ref, rkv_ref, gl_ref, s0_ref, cast_in, y_ref, s_ref, cast_out, bb, nck):
    @pl.when(pl.program_id(1) == 0)
    def _():
        s_ref[...] = s0_ref[...]

    for src, dst in zip(cast_in, cast_out):
        dst[...] = src[...].astype(BF16)

    tw = RWKV_TILE_LANES
    head, row, col = _head_masks(tw)
    strict = col < row
    incl = col <= row
    same8 = (col >> 3) == (row >> 3)
    lane2 = lax.broadcasted_iota(jnp.int32, (tw, tw), 1)
    row2 = lax.broadcasted_iota(jnp.int32, (tw, tw), 0)
    same_head = (lane2 >> HEAD_SHIFT) == (row2 >> HEAD_SHIFT)
    bd = functools.partial(_block_diag, head=head)
    c = CHUNK

    def pmm(p, q):
        return _dg(p.astype(BF16), bd(q.astype(BF16)))

    items = [(b, ck, pr) for b in range(bb) for ck in range(nck) for pr in range(RWKV_TILES)]
    n = range(len(items))
    rows = lambda ck: slice(ck * c, (ck + 1) * c)
    lanes = lambda pr: slice(pr * tw, (pr + 1) * tw)
    w = RWKV_WIDTH

    def ld(ref, slot):
        return [ref[b, rows(ck), slot * w + pr * tw:slot * w + (pr + 1) * tw] for b, ck, pr in items]

    cat0 = lambda *xs: jnp.concatenate(xs, axis=0)
    cat1 = lambda *xs: jnp.concatenate(xs, axis=1)
    at, bt = ld(ab_ref, 0), ld(ab_ref, 1)
    rt, kt, ke, v = (ld(rkv_ref, slot) for slot in range(4))
    be = [(bt[i] * gl_ref[b, ck, :, lanes(pr)]).astype(BF16) for i, (b, ck, pr) in enumerate(items)]
    ats, bts = [_split2(a) for a in at], [_split2(b) for b in bt]
    atb = [hi for hi, _ in ats]
    zero = jnp.zeros((c, tw), F32)
    gk = [_dg(cat0(atb[i], rt[i]), bd(kt[i]), _NT) for i in n]
    aak = [jnp.where(strict, gk[i][:c], zero) for i in n]
    ark = [jnp.where(incl, gk[i][c:], zero) for i in n]
    yield
    gb = [_dg(cat0(ats[i][0], ats[i][1], rt[i]), bd(bts[i][0]), _NT) for i in n]
    arb = [jnp.where(incl, gb[i][2 * c:], zero) for i in n]
    aab = [jnp.where(strict, gb[i][:c] + (gb[i][c:2 * c] + _dg(atb[i], bd(bts[i][1]), _NT)), zero) for i in n]
    yield
    z = [pmm(aak[i], v[i]) for i in n]
    a8 = [jnp.where(same8, aab[i], zero) for i in n]
    p2 = [pmm(a8[i], a8[i]) for i in n]
    yield
    p4 = [pmm(p2[i], p2[i]) for i in n]
    yield
    nn = [a8[i] + p2[i] + pmm(p2[i], a8[i]) for i in n]
    yield
    nn = [nn[i] + p4[i] + pmm(p4[i], nn[i]) for i in n]
    yield
    for lvl in (3, 4, 5):
        joins = ((col >> (lvl + 1)) == (row >> (lvl + 1))) & ((col >> lvl) != (row >> lvl))
        e = [jnp.where(joins, aab[i], zero) for i in n]
        te = [e[i] + pmm(nn[i], e[i]) for i in n]
        yield
        nn = [nn[i] + te[i] + pmm(te[i], nn[i]) for i in n]
        yield
    wu = [cat1(at[i], z[i]) + _dg(nn[i].astype(BF16), cat1(bd(atb[i]), bd(z[i].astype(BF16)))) for i in n]
    abk = [cat1(arb[i], ark[i]).astype(BF16) for i in n]
    yield
    s = {(b, pr): s_ref[b, pr] for b in range(bb) for pr in range(RWKV_TILES)}
    for ck in range(nck):
        cur = [i for i in n if items[i][1] == ck]
        key = lambda i: (items[i][0], items[i][2])
        x = {i: _dg(cat0(wu[i][:, :tw].astype(BF16), rt[i]), s[key(i)].astype(BF16), _NT) for i in cur}
        ub = {i: (x[i][:c] + wu[i][:, tw:]).astype(BF16) for i in cur}
        yield
        upd = {i: _dg(cat0(ub[i], v[i]), cat0(be[i], ke[i]), _TN) for i in cur}
        for i in cur:
            b, _, pr = items[i]
            y_ref[b, rows(ck), lanes(pr)] = x[i][c:] + _dg(abk[i], cat0(bd(ub[i]), bd(v[i])))
            s[b, pr] = (s[b, pr] * gl_ref[b, ck, :, lanes(pr)]
                        + jnp.where(same_head, upd[i], jnp.zeros_like(upd[i])))
        yield
    for (b, pr), val in s.items():
        s_ref[b, pr] = val


def _gla_stages(p_ref, wgate_ref, bgate_ref, tri_ref, ng_ref, s0_ref, y_ref, s_ref, bb, nck):
    @pl.when(pl.program_id(1) == 0)
    def _():
        s_ref[...] = s0_ref[...]

    head, row, col = _head_masks(LANES)
    incl = col <= row
    bd = functools.partial(_block_diag, head=head)
    lane2 = lax.broadcasted_iota(jnp.int32, (2 * GLA_DV, LANES), 1)
    row2 = lax.broadcasted_iota(jnp.int32, (2 * GLA_DV, LANES), 0)
    same_head = (lane2 < GLA_DK) == (row2 < GLA_DV)
    kw, gw = GLA_KEY_WIDTH, GLA_WIDTH
    zero = jnp.zeros((CHUNK, LANES), F32)
    cat0 = lambda xs: jnp.concatenate(xs, axis=0)
    cat1 = lambda xs: jnp.concatenate(xs, axis=1)
    rows = lambda c: slice(c * CHUNK, (c + 1) * CHUNK)
    lanes = lambda pr: slice(pr * LANES, (pr + 1) * LANES)
    bs, cs = range(bb), range(nck)
    x = [p_ref[b] for b in bs]
    gate = [_mm1(x[b][:, 2 * kw + 2 * gw:], wgate_ref[...]) + bgate_ref[...] for b in bs]
    la = [(jnp.minimum(g, 0.0) - jnp.log(1.0 + jnp.exp(-jnp.abs(g)))) / GLA_TAU for g in gate]
    yield
    cum = [_mm_exact_lhs(tri_ref[...], la[b]) for b in bs]
    yield
    q_dec = [(x[b][:, :kw] * (GLA_DK ** -0.5)) * jnp.exp(cum[b]) for b in bs]
    k_inv = [x[b][:, kw:2 * kw] * jnp.exp(-cum[b]) for b in bs]
    last = [[cum[b][c * CHUNK + CHUNK - 1:(c + 1) * CHUNK, :] for c in cs] for b in bs]
    k_end = [[x[b][rows(c), kw:2 * kw] * jnp.exp(last[b][c] - cum[b][rows(c)]) for c in cs] for b in bs]
    dec = [[jnp.exp(last[b][c]) for c in cs] for b in bs]
    yield
    items = [(b, c, pr) for b in bs for c in cs for pr in range(GLA_PAIRS)]
    vp = {(b, c, pr): x[b][rows(c), 2 * kw + pr * 2 * GLA_DV:2 * kw + (pr + 1) * 2 * GLA_DV]
          for b, c, pr in items}
    att = {(b, c, pr): jnp.where(incl, _mm1(q_dec[b][rows(c), lanes(pr)],
                                            bd(k_inv[b][rows(c), lanes(pr)]), _NT), zero)
           for b, c, pr in items}
    yield
    upd = {it: _mm1(vp[it], k_end[it[0]][it[1]][:, lanes(it[2])], _TN) for it in items}
    yield
    intra = {it: _mm1(att[it], cat0([cat1([vp[it][:, :GLA_DV], zero]), cat1([zero, vp[it][:, GLA_DV:]])]))
             for it in items}
    yield
    st = {}
    for b in bs:
        for pr in range(GLA_PAIRS):
            cur = s_ref[b, pr]
            for c in cs:
                st[b, c, pr] = cur
                cur = cur * dec[b][c][:, lanes(pr)] + jnp.where(same_head, upd[b, c, pr], jnp.zeros_like(cur))
            s_ref[b, pr] = cur
    yield
    o = {(b, c, pr): intra[b, c, pr] + _mm1(q_dec[b][rows(c), lanes(pr)], st[b, c, pr], _NT)
         for b, c, pr in items}
    yield
    for b in bs:
        ob = cat0([cat1([o[b, c, pr] for pr in range(GLA_PAIRS)]) for c in cs])
        heads = [ob[:, h * GLA_DV:(h + 1) * GLA_DV] for h in range(GLA_HEADS)]
        normed = [oh * lax.rsqrt(jnp.mean(oh * oh, axis=-1, keepdims=True) + RMS_EPS) for oh in heads]
        y_ref[b] = cat1(normed) * ng_ref[...] * _silu(x[b][:, 2 * kw + gw:2 * kw + 2 * gw])


def _recurrent_kernel(*refs, bb, nck, n_cast):
    rwkv_in, refs = refs[:4], refs[4:]
    cast_in, refs = refs[:n_cast], refs[n_cast:]
    gla_in, refs = refs[:6], refs[6:]
    (y_ref, s_ref), refs = refs[:2], refs[2:]
    cast_out, (yg_ref, t_ref) = refs[:n_cast], refs[n_cast:]
    rwkv = _rwkv_stages(*rwkv_in, cast_in, y_ref, s_ref, cast_out, bb, nck)
    gla = _gla_stages(*gla_in, yg_ref, t_ref, bb, nck)
    done = object()
    live = [rwkv, rwkv, gla]
    while live:
        for gen in list(live):
            if gen in live and next(gen, done) is done:
                live = [g for g in live if g is not gen]


def _recurrent(ab, rkv, gl, s0, proj, lp, tri, t0, bb, nck, casts=()):
    bsz, seq, _ = ab.shape
    w = RWKV_WIDTH
    rows = nck * CHUNK
    n_seq_steps = seq // rows
    n_steps = (bsz // bb) * n_seq_steps
    tok = lambda a: pl.BlockSpec((bb, rows, a.shape[-1]), lambda i, c: (i, c, 0))
    full = lambda a: pl.BlockSpec(a.shape, lambda i, c: (0,) * a.ndim)
    st = pl.BlockSpec((bb, RWKV_TILES, RWKV_TILE_LANES, RWKV_TILE_LANES), lambda i, c: (i, 0, 0, 0))
    gst = pl.BlockSpec((bb, GLA_PAIRS, 2 * GLA_DV, LANES), lambda i, c: (i, 0, 0, 0))
    slab = lambda a: pl.BlockSpec((a.shape[0] // n_steps, a.shape[1]), lambda i, c: (i * n_seq_steps + c, 0))
    assert all(a.shape[0] % (16 * n_steps) == 0 for a in casts)
    out_tok = pl.BlockSpec((bb, rows, w), lambda i, c: (i, c, 0))
    return pl.pallas_call(
        functools.partial(_recurrent_kernel, bb=bb, nck=nck, n_cast=len(casts)),
        grid=(bsz // bb, n_seq_steps),
        in_specs=[tok(ab), tok(rkv), pl.BlockSpec((bb, nck, 1, w), lambda i, c: (i, c, 0, 0)), st]
        + [slab(a) for a in casts]
        + [pl.BlockSpec((bb, rows, GLA_PROJ_PAD), lambda i, c: (i, c, 1)),
           full(lp["wgate"]), full(lp["bgate"]), full(tri), full(lp["gla_g"]), gst],
        out_specs=[out_tok, st] + [slab(a) for a in casts] + [out_tok, gst],
        out_shape=[jax.ShapeDtypeStruct((bsz, seq, w), F32),
                   jax.ShapeDtypeStruct((bsz, RWKV_TILES, RWKV_TILE_LANES, RWKV_TILE_LANES), F32)]
        + [jax.ShapeDtypeStruct(a.shape, BF16) for a in casts]
        + [jax.ShapeDtypeStruct((bsz, seq, GLA_WIDTH), F32),
           jax.ShapeDtypeStruct((bsz, GLA_PAIRS, 2 * GLA_DV, LANES), F32)],
        compiler_params=_params("parallel", "arbitrary"),
        name="rwkv_gla_chunk",
    )(ab, rkv, gl, s0, *casts, proj, lp["wgate"], lp["bgate"], tri, lp["gla_g"], t0)


def _route(lgt):
    tm = lgt.shape[1]
    n_pad = ROUTE_ROWS - N_EXPERTS
    rowg = lax.broadcasted_iota(jnp.int32, (n_pad, tm), 0)
    rowe = lax.broadcasted_iota(jnp.int32, (N_EXPERTS, tm), 0)
    rowg_f, rowe_f = rowg.astype(F32), rowe.astype(F32)

    def first_argmax(vals, mx, rows_f):
        return jnp.min(jnp.where(vals == mx, rows_f, float(ROUTE_ROWS)), axis=0, keepdims=True)

    is_group = rowg < N_GROUPS
    lg = jnp.where(is_group, lgt[N_EXPERTS:], -jnp.inf)
    gmax = jnp.max(lg, axis=0, keepdims=True)
    gi = first_argmax(lg, gmax, rowg_f)
    pg_top = 1.0 / jnp.sum(jnp.where(is_group, jnp.exp(lg - gmax), 0.0), axis=0, keepdims=True)
    in_group = (rowe >> GROUP_SHIFT).astype(F32) == gi
    le = jnp.where(in_group, lgt[:N_EXPERTS], -jnp.inf)
    m1 = jnp.max(le, axis=0, keepdims=True)
    i1 = first_argmax(le, m1, rowe_f)
    le2 = jnp.where(rowe_f == i1, -jnp.inf, le)
    m2 = jnp.max(le2, axis=0, keepdims=True)
    i2 = first_argmax(le2, m2, rowe_f)
    p2 = jnp.exp(m2 - m1)
    w1 = pg_top / (1.0 + p2)
    w2 = pg_top * p2 / (1.0 + p2)
    comb = jnp.where(rowe_f == i1, w1, 0.0) + jnp.where(rowe_f == i2, w2, 0.0)
    return jnp.concatenate([comb, jnp.where(rowg == 0, gi, 0.0)], axis=0)


def _outproj_kernel(yr_ref, bg_ref, yg_ref, x_ref, gt_ref, sc_ref, sh_ref,
                    lnw_ref, lnb_ref, seg_ref, wout_ref, g2_ref, wr_ref, br_ref,
                    x1_ref, h2_ref, route_ref, *, n_parts):
    nb, rb, d_model = x_ref.shape
    if nb % n_parts == 0:
        pnb, prb = nb // n_parts, rb
        tok = lambda p: (slice(p * pnb, (p + 1) * pnb), slice(None), slice(None))
        seq_of = tok
    else:
        pnb, prb = nb, rb // n_parts
        tok = lambda p: (slice(None), slice(p * prb, (p + 1) * prb), slice(None))
        seq_of = lambda p: (slice(None), slice(None), slice(None))
    tm = pnb * prb
    parts = range(n_parts)
    flat = lambda ref, p: ref[tok(p)].reshape(tm, ref.shape[-1])
    seg = seg_ref[...]
    inv_n = 1.0 / RWKV_HEAD
    ds = [y - _mm2_exact_rhs(y, seg) * inv_n for y in (flat(yr_ref, p) for p in parts)]
    vs = [_mm2_exact_rhs(d * d, seg) * inv_n for d in ds]
    mixes = []
    for p in parts:
        yn = ds[p] * lax.rsqrt(vs[p] + LNX_EPS) * lnw_ref[...] + lnb_ref[...]
        bg = flat(bg_ref, p)
        yr = (yn + bg[:, :RWKV_WIDTH]) * bg[:, RWKV_WIDTH:]
        mixes.append(jnp.concatenate([yr, flat(yg_ref, p)], axis=1))
    outs = [_mm1(mix, wout_ref[...]) for mix in mixes]
    h2s = []
    for p in parts:
        x1 = x_ref[tok(p)] + gt_ref[seq_of(p)] * outs[p].reshape(pnb, prb, d_model)
        x1_ref[tok(p)] = x1
        h2 = _rms_mod(x1, g2_ref[...], sc_ref[seq_of(p)], sh_ref[seq_of(p)])
        h2_ref[tok(p)] = h2.astype(BF16)
        h2s.append(h2.reshape(tm, d_model))
    logits = [_mm3(wr_ref[...], h2, _NT) + br_ref[...] for h2 in h2s]
    for p in parts:
        route_ref[:, p * tm:(p + 1) * tm] = _route(logits[p])


def _out_proj(yr, bg, yg, x, gt, sc, sh, lp, consts, rows):
    bsz, seq, d = x.shape
    nb, rb = _token_tile(bsz, seq, rows)
    n_seq_tiles = seq // rb
    half = pl.BlockSpec((nb, rb, RWKV_WIDTH), lambda b, i: (b, i, 0))
    tokd = pl.BlockSpec((nb, rb, d), lambda b, i: (b, i, 0))
    vec = pl.BlockSpec((nb, 1, d), lambda b, i: (b, 0, 0))
    full = lambda a: pl.BlockSpec(a.shape, lambda b, i: (0,) * a.ndim)
    args = (lp["lnx_w"], lp["lnx_b"], consts["seg"], lp["w_out"], lp["norm2_g"], lp["w_router"], lp["b_router"])
    return pl.pallas_call(
        functools.partial(_outproj_kernel, n_parts=OUT_PROJ_PARTS),
        grid=(bsz // nb, seq // rb),
        in_specs=[half, pl.BlockSpec((nb, rb, 2 * RWKV_WIDTH), lambda b, i: (b, i, 0)), half, tokd,
                  vec, vec, vec] + [full(a) for a in args],
        out_specs=[tokd, tokd, pl.BlockSpec((ROUTE_ROWS, nb * rb), lambda b, i: (0, b * n_seq_tiles + i))],
        out_shape=[jax.ShapeDtypeStruct((bsz, seq, d), F32),
                   jax.ShapeDtypeStruct((bsz, seq, d), BF16),
                   jax.ShapeDtypeStruct((ROUTE_ROWS, bsz * seq), F32)],
        compiler_params=_params("parallel", "parallel"),
        name="out_proj_router",
    )(yr, bg, yg, x, gt, sc, sh, *args)


def _moe_kernel(h_ref, route_ref, x1_ref, gt_ref, sc_ref, sh_ref, gf_ref, tri_ref, wg_ref, wu_ref, wd_ref,
                y_ref, ys_ref, pos_ref, meta_ref):
    g = pl.program_id(2)
    nb, rb, d = h_ref.shape
    tm = nb * rb
    sub, tail = MOE_SUB_ROWS, MOE_SUB_ROWS // 2
    cap = ys_ref.shape[0]
    n_lane_tiles = tm // LANES
    lane_tile = lambda k: slice(k * LANES, (k + 1) * LANES)

    @pl.when(g == 0)
    def _sort():
        row8 = lax.broadcasted_iota(jnp.int32, (8, LANES), 0).astype(F32)
        carry = jnp.zeros((8, 1), F32)
        members, ranks = [], []
        for k in range(n_lane_tiles):
            blk = jnp.where(row8 == route_ref[N_EXPERTS:N_EXPERTS + 1, lane_tile(k)], 1.0, 0.0)
            members.append(blk)
            ranks.append(_dg(blk.astype(BF16), tri_ref[...]) - blk + carry)
            carry = carry + jnp.sum(blk, axis=1, keepdims=True)
        rowc = lax.broadcasted_iota(jnp.int32, (8, 1), 0)
        lane = lax.broadcasted_iota(jnp.int32, (1, LANES), 1)
        first = jnp.zeros((1, 1), F32)
        off_col = jnp.zeros((8, 1), F32)
        meta = jnp.zeros((1, LANES), F32)
        for grp in range(N_GROUPS):
            count = jnp.sum(jnp.where(rowc == grp, carry, 0.0), axis=0, keepdims=True)
            blocks = jnp.ceil(count * (1.0 / tail))
            pairs = jnp.floor(blocks * 0.5)
            odd = blocks - 2.0 * pairs
            triple = odd * jnp.where(blocks >= 3.0, 1.0, 0.0)
            n_full = pairs - triple
            off_col = off_col + jnp.where(rowc == grp, first, 0.0)
            meta = (meta + jnp.where(lane == grp, first, 0.0) + jnp.where(lane == N_GROUPS + grp, n_full, 0.0)
                    + jnp.where(lane == 2 * N_GROUPS + grp, odd + 2.0 * triple, 0.0))
            first = first + blocks * tail
        meta = (meta + jnp.where(lane == 3 * N_GROUPS, first, 0.0)).astype(jnp.int32)
        for i in range(3 * N_GROUPS + 1):
            meta_ref[i] = meta[0, i]
        for k in range(n_lane_tiles):
            pos = jnp.sum(members[k] * (ranks[k] + off_col), axis=0, keepdims=True)
            pos_ref[:, lane_tile(k)] = pos.astype(jnp.int32)
        ys_ref[...] = jnp.zeros_like(ys_ref)

    def experts(base, n_rows):
        prow = lax.broadcasted_iota(jnp.int32, (n_rows, tm), 0) + base
        onehot = jnp.where(prow == pos_ref[...], 1.0, 0.0).astype(BF16)
        hs = _dg(onehot, h_ref[...].reshape(tm, d)).astype(BF16)
        comb3 = _split3(route_ref[:N_EXPERTS, :])
        cs = _dg(onehot, comb3[0], _NT) + (_dg(onehot, comb3[1], _NT) + _dg(onehot, comb3[2], _NT))
        lane = lax.broadcasted_iota(jnp.int32, cs.shape, 1)
        ys = jnp.zeros((n_rows, d), F32)
        for e in range(EXPERTS_PER_GROUP):
            hid = _silu(_dg(hs, wg_ref[e])) * _dg(hs, wu_ref[e])
            ce = jnp.sum(jnp.where(lane == g * EXPERTS_PER_GROUP + e, cs, 0.0), axis=-1, keepdims=True)
            ys = ys + ce * _dg(hid.astype(BF16), wd_ref[e])
        ys_ref[pl.ds(base, n_rows), :] = ys.astype(BF16)

    first_row = meta_ref[g]
    n_full = meta_ref[N_GROUPS + g]

    def full_sub_tile(j, carry_):
        experts(pl.multiple_of(first_row + j * sub, tail), sub)
        return carry_

    lax.fori_loop(0, n_full, full_sub_tile, 0)

    for last_blocks in (1, 3):
        @pl.when(meta_ref[2 * N_GROUPS + g] == last_blocks)
        def _last(last_blocks=last_blocks):
            experts(pl.multiple_of(first_row + n_full * sub, tail), last_blocks * tail)

    def unsort(n_sorted):
        ys_all = ys_ref[:n_sorted]
        prow = lax.broadcasted_iota(jnp.int32, (n_sorted, LANES), 0)
        seqs = max(LANES // rb, 1)
        for k in range(n_lane_tiles):
            onehot = jnp.where(prow == pos_ref[:, lane_tile(k)], 1.0, 0.0).astype(BF16)
            moe = _dg(onehot, ys_all, _TN).reshape(seqs, LANES // seqs, d)
            b0 = k * LANES // rb
            r0 = k * LANES - b0 * rb
            bs, rs = slice(b0, b0 + seqs), slice(r0, r0 + LANES // seqs)
            x2 = x1_ref[bs, rs] + gt_ref[bs] * moe
            y_ref[bs, rs] = _rms_mod(x2, gf_ref[...], sc_ref[bs], sh_ref[bs])

    usual = tm + (cap - tm) // 2
    last = g == N_GROUPS - 1
    pl.when(last & (meta_ref[3 * N_GROUPS] <= usual))(lambda: unsort(usual))
    pl.when(last & (meta_ref[3 * N_GROUPS] > usual))(lambda: unsort(cap))


def _moe(h2, route, x1, gt, sc, sh, gf, wg, wu, wd, tri, rows):
    bsz, seq, d = x1.shape
    nb, rb = _token_tile(bsz, seq, rows)
    tm = nb * rb
    assert tm % LANES == 0 and (rb % LANES == 0 or LANES % rb == 0)
    n_seq_tiles = seq // rb
    cap = tm + N_GROUPS * (MOE_SUB_ROWS // 2)
    tokd = pl.BlockSpec((nb, rb, d), lambda b, i, g: (b, i, 0))
    vec = pl.BlockSpec((nb, 1, d), lambda b, i, g: (b, 0, 0))
    group_w = lambda shape: pl.BlockSpec((EXPERTS_PER_GROUP,) + shape, lambda b, i, g: (g, 0, 0))
    return pl.pallas_call(
        _moe_kernel,
        grid=(bsz // nb, n_seq_tiles, N_GROUPS),
        in_specs=[tokd, pl.BlockSpec((ROUTE_ROWS, tm), lambda b, i, g: (0, b * n_seq_tiles + i)),
                  tokd, vec, vec, vec,
                  pl.BlockSpec((1, d), lambda b, i, g: (0, 0)),
                  pl.BlockSpec(tri.shape, lambda b, i, g: (0, 0)),
                  group_w((d, D_EXPERT)), group_w((d, D_EXPERT)), group_w((D_EXPERT, d))],
        out_specs=tokd,
        out_shape=jax.ShapeDtypeStruct((bsz, seq, d), F32),
        scratch_shapes=[pltpu.VMEM((cap, d), BF16), pltpu.VMEM((1, tm), jnp.int32),
                        pltpu.SMEM((3 * N_GROUPS + 1,), jnp.int32)],
        compiler_params=_params("parallel", "parallel", "arbitrary"),
        name="moe_final_norm",
    )(h2, route, x1, gt, sc, sh, gf, tri, wg, wu, wd)


def _block_ones(n, blk, lower):
    i = np.arange(n)
    m = (i[:, None] // blk) == (i[None, :] // blk)
    if lower:
        m = m & (i[None, :] <= i[:, None])
    return jnp.asarray(m, dtype=BF16)


def _consts(tr, tg):
    return dict(seg=_block_ones(RWKV_WIDTH, RWKV_HEAD, False), tri=_block_ones(tr, CHUNK, True),
                tri_gla=_block_ones(tg, CHUNK, True), tri_up=_block_ones(LANES, LANES, True).T)


def _heads_block_diag(s):
    n, c = s.shape[2], s.shape[-1]
    rows = [jnp.pad(s[:, :, j], ((0, 0), (0, 0), (0, 0), (j * c, (n - 1 - j) * c))) for j in range(n)]
    return jnp.concatenate(rows, axis=2)


def _diag_blocks(t, n):
    r, c = t.shape[-2] // n, t.shape[-1] // n
    return jnp.stack([t[:, :, j * r:(j + 1) * r, j * c:(j + 1) * c] for j in range(n)], axis=2)


def _pad_rows(w, first_row):
    out = jnp.zeros((LORA_PAD, w.shape[1]), F32)
    return lax.dynamic_update_slice(out, w, (first_row, 0)).astype(BF16)


def _layer_params(l, w_in, mu_shift, w0, w_decay_up, a0, w_a_up, w_g_up, k_k, k_a, r_k, lnx_w, lnx_b,
                  w_gla_gate_up, b_gla_gate, gla_norm_g, w_out, norm2_g,
                  w_router_group, b_router_group, w_router_expert, b_router_expert):
    w_in_p = _win_layout(w_in[l].T, WEIGHT_TILE_COLS)
    n_pad = ROUTE_ROWS - N_EXPERTS - N_GROUPS
    w_router = jnp.concatenate([w_router_expert[l].T, w_router_group[l].T, jnp.zeros((n_pad, D_MODEL), F32)])
    b_router = jnp.concatenate([b_router_expert[l], b_router_group[l],
                                jnp.zeros((n_pad,), F32)]).reshape(ROUTE_ROWS, 1)
    wgate = jnp.zeros((LANES, GLA_KEY_WIDTH), F32).at[:GLA_GATE_RANK].set(w_gla_gate_up[l]).astype(BF16)
    r1 = lambda a: a.reshape(1, -1)
    return dict(
        w_in=w_in_p, mu=r1(mu_shift[l]), w0=r1(w0[l]), a0=r1(a0[l]), k_k=r1(k_k[l]), k_a=r1(k_a[l]),
        r_k=r1(r_k[l]), wd=_pad_rows(w_decay_up[l], 0), wa=_pad_rows(w_a_up[l], DECAY_LORA),
        wg=_pad_rows(w_g_up[l], DECAY_LORA + AAA_LORA), lnx_w=r1(lnx_w[l]), lnx_b=r1(lnx_b[l]),
        wgate=wgate, bgate=r1(b_gla_gate[l]), gla_g=r1(gla_norm_g[l]),
        w_out=w_out[l].astype(BF16), norm2_g=r1(norm2_g[l]), w_router=w_router, b_router=b_router)


def _run_layer(x, mod, shift0, wkv0, gla0, lp, experts, final):
    bsz, seq, d = x.shape
    assert seq % CHUNK == 0
    bb = 2 if seq >= RECURRENT_STEP_ROWS else min(bsz, RECURRENT_MAX_SEQS)
    assert bsz % bb == 0
    nck = min(seq, RECURRENT_STEP_ROWS) // CHUNK
    consts = _consts(TOKEN_TILE_ROWS, nck * CHUNK)
    m = lambda j: mod[:, j:j + 1, :]
    sh1, sc1, gt1, sh2, sc2, gt2 = (m(j) for j in range(6))
    proj, tails = _in_proj(x, sc1, sh1, lp["norm1_g"], lp["w_in"], PROJ_TILE_ROWS, TOKEN_TILE_ROWS)
    tails = tails.reshape(bsz, -1, RWKV_PROJ)
    new_shift = tails[:, -1]
    prev = jnp.concatenate([shift0[:, None, :], tails[:, :-1]], axis=1).reshape(-1, 1, RWKV_PROJ)
    ab, rkv, bg, gl = _rwkv_prep(proj, prev, lp, consts, TOKEN_TILE_ROWS)
    gl = gl.reshape(bsz, seq // CHUNK, 1, RWKV_WIDTH)
    s0 = _heads_block_diag(wkv0.reshape(bsz, RWKV_TILES, RWKV_TILE_HEADS, RWKV_HEAD, RWKV_HEAD))
    t0 = _heads_block_diag(jnp.swapaxes(gla0, -1, -2).reshape(bsz, GLA_PAIRS, 2, GLA_DV, GLA_DK))
    casts = tuple(e.reshape(-1, e.shape[-1]) for e in experts if e.dtype != BF16)
    yr, s_bd, *cast, yg, t_bd = _recurrent(ab, rkv, gl, s0, proj, lp, consts["tri_gla"], t0,
                                           bb, nck, casts)
    if cast:
        experts = tuple(c.reshape(e.shape) for c, e in zip(cast, experts))
    new_wkv = _diag_blocks(s_bd, RWKV_TILE_HEADS).reshape(bsz, RWKV_HEADS, RWKV_HEAD, RWKV_HEAD)
    new_gla = jnp.swapaxes(_diag_blocks(t_bd, 2).reshape(bsz, GLA_HEADS, GLA_DV, GLA_DK), -1, -2)
    x1, h2, route = _out_proj(yr, bg, yg, x, gt1, sc2, sh2, lp, consts, PROJ_TILE_ROWS)
    out = _moe(h2, route, x1, gt2, *final, *experts, consts["tri_up"], MOE_TILE_ROWS)
    return experts, out, new_shift, new_wkv, new_gla


def kernel(x_prompt, x_sample, c_prompt, c_sample, state_rwkv_shift, state_rwkv_wkv, state_gla_kv, w_ada, b_ada, norm1_g, norm2_g, w_in, mu_shift, w0, w_decay_up, a0, w_a_up, w_g_up, k_k, k_a, r_k, lnx_w, lnx_b, w_gla_gate_up, b_gla_gate, gla_norm_g, w_out, w_router_group, b_router_group, w_router_expert, b_router_expert, w_expert_gate, w_expert_up, w_expert_down, w_ada_final, b_ada_final, normf_g):
    assert w_ada.shape[0] == 1, "the final norm is fused into the single layer's MoE kernel"
    bp, bs = x_prompt.shape[0], x_sample.shape[0]
    d = D_MODEL
    n_rows = -(-(bp + bs) // 8) * 8
    c_all = jnp.zeros((n_rows, d), F32).at[:bp].set(c_prompt).at[bp:bp + bs].set(c_sample)
    modf = _modulation(c_all, w_ada_final, b_ada_final, MODULATION_TILE_COLS).reshape(n_rows, 2, d)
    mod = _modulation(c_all, w_ada, b_ada[0], MODULATION_TILE_COLS, layer=0).reshape(n_rows, 6, d)
    lp = _layer_params(0, w_in, mu_shift, w0, w_decay_up, a0, w_a_up, w_g_up, k_k, k_a, r_k, lnx_w,
                       lnx_b, w_gla_gate_up, b_gla_gate, gla_norm_g, w_out, norm2_g,
                       w_router_group, b_router_group, w_router_expert, b_router_expert)
    lp["norm1_g"] = norm1_g[0].reshape(1, d)
    experts = (w_expert_gate[0], w_expert_up[0], w_expert_down[0])
    groups = [
        (x_prompt, 0, bp, jnp.zeros((bp, RWKV_PROJ), F32),
         jnp.zeros((bp, RWKV_HEADS, RWKV_HEAD, RWKV_HEAD), F32), jnp.zeros((bp, GLA_HEADS, GLA_DK, GLA_DV), F32)),
        (x_sample, bp, bp + bs, state_rwkv_shift[0], state_rwkv_wkv[0], state_gla_kv[0]),
    ]
    ys, states = [], []
    for x, lo, hi, shift0, wkv0, gla0 in groups:
        final = (modf[lo:hi, 1:2], modf[lo:hi, 0:1], normf_g.reshape(1, d))
        experts, y, *st = _run_layer(x, mod[lo:hi], shift0, wkv0, gla0, lp, experts, final)
        ys.append(y)
        states.extend(s[None] for s in st)
    return tuple(ys + states)
```

```python
import functools

import jax
import jax.numpy as jnp
import numpy as np
from jax import lax
from jax.experimental import pallas as pl
from jax.experimental.pallas import tpu as pltpu

F32 = jnp.float32
BF16 = jnp.bfloat16

LANES = 128
VMEM_LIMIT_BYTES = 56 * 1024 * 1024
TOKEN_TILE_ROWS = 512
PROJ_TILE_ROWS = 1024
OUT_PROJ_PARTS = 2
MOE_TILE_ROWS = 1024
MOE_SUB_ROWS = 256
RECURRENT_STEP_ROWS = 256
RECURRENT_MAX_SEQS = 8
MODULATION_TILE_COLS = 1024
MODULATION_WEIGHT_STREAMS = 4
WEIGHT_TILE_COLS = 256

D_MODEL = 1024
CHUNK = 64
RWKV_WIDTH = 512
RWKV_HEAD = 64
RWKV_HEADS = RWKV_WIDTH // RWKV_HEAD
HEAD_LANES = RWKV_HEAD
HEAD_SHIFT = HEAD_LANES.bit_length() - 1
assert 1 << HEAD_SHIFT == HEAD_LANES
RWKV_TILE_LANES = 128
RWKV_TILE_HEADS = RWKV_TILE_LANES // RWKV_HEAD
RWKV_TILES = RWKV_WIDTH // RWKV_TILE_LANES
DECAY_LORA = 32
AAA_LORA = 32
GATE_LORA = 64
LORA_PAD = DECAY_LORA + AAA_LORA + GATE_LORA
RWKV_PROJ = 3 * RWKV_WIDTH + LORA_PAD
GLA_WIDTH = 512
GLA_HEADS = 4
GLA_PAIRS = GLA_HEADS // 2
GLA_DV = GLA_WIDTH // GLA_HEADS
GLA_DK = GLA_DV // 2
GLA_KEY_WIDTH = GLA_HEADS * GLA_DK
GLA_GATE_RANK = 16
GLA_TAU = 16.0
GLA_PROJ = 2 * GLA_KEY_WIDTH + 2 * GLA_WIDTH + GLA_GATE_RANK
GLA_PROJ_PAD = RWKV_PROJ
IN_PROJ_PAD = RWKV_PROJ + GLA_PROJ_PAD
N_GROUPS = 4
EXPERTS_PER_GROUP = 4
N_EXPERTS = N_GROUPS * EXPERTS_PER_GROUP
GROUP_SHIFT = EXPERTS_PER_GROUP.bit_length() - 1
assert 1 << GROUP_SHIFT == EXPERTS_PER_GROUP
ROUTE_ROWS = 24
D_EXPERT = 512
RMS_EPS = 1e-6
LNX_EPS = 64e-5
LOG2_E = 1.4426950408889634

_NN = (((1,), (0,)), ((), ()))
_NT = (((1,), (1,)), ((), ()))
_TN = (((0,), (0,)), ((), ()))


def _dg(a, b, dims=_NN):
    return lax.dot_general(a, b, dims, preferred_element_type=F32)


def _split2(x):
    hi = x.astype(BF16)
    lo = (x - hi.astype(F32)).astype(BF16)
    return hi, lo


def _split3(x):
    hi = x.astype(BF16)
    r1 = x - hi.astype(F32)
    mid = r1.astype(BF16)
    lo = (r1 - mid.astype(F32)).astype(BF16)
    return hi, mid, lo


def _mm1(a, b, dims=_NN):
    return _dg(a.astype(BF16), b.astype(BF16), dims)


def _mm3(a, b, dims=_NN):
    ah, al = _split2(a)
    bh, bl = _split2(b)
    return _dg(ah, bh, dims) + (_dg(ah, bl, dims) + _dg(al, bh, dims))


def _mm_exact_lhs(e, x, dims=_NN):
    h, m, l = _split3(x)
    return _dg(e, h, dims) + (_dg(e, m, dims) + _dg(e, l, dims))


def _mm2_exact_rhs(x, e, dims=_NN):
    h, l = _split2(x)
    return _dg(h, e, dims) + _dg(l, e, dims)


def _softplus(z):
    return jnp.maximum(z, 0.0) + jnp.log(1.0 + jnp.exp(-jnp.abs(z)))


def _sigmoid(z):
    return 1.0 / (1.0 + jnp.exp(-z))


def _silu(z):
    return z * _sigmoid(z)


def _params(*sem):
    return pltpu.CompilerParams(dimension_semantics=sem, vmem_limit_bytes=VMEM_LIMIT_BYTES)


def _mod_kernel(c_ref, *refs):
    *w_refs, b_ref, o_ref = refs
    a = _silu(c_ref[...])
    ts = w_refs[0].shape[-1]
    for s, w_ref in enumerate(w_refs):
        o_ref[:, s * ts:(s + 1) * ts] = _mm1(a, w_ref[...]) + b_ref[:, s * ts:(s + 1) * ts]


def _modulation(c, w, b, tn, layer=None):
    rows, d = c.shape
    n = w.shape[-1]
    ns = MODULATION_WEIGHT_STREAMS
    ts = tn // ns
    if layer is None:
        w_specs = [pl.BlockSpec((d, ts), lambda j, s=s: (0, j * ns + s)) for s in range(ns)]
    else:
        w_specs = [pl.BlockSpec((None, d, ts), lambda j, s=s: (layer, 0, j * ns + s)) for s in range(ns)]
    return pl.pallas_call(
        _mod_kernel,
        grid=(n // tn,),
        in_specs=[pl.BlockSpec((rows, d), lambda j: (0, 0)), *w_specs,
                  pl.BlockSpec((1, tn), lambda j: (0, j))],
        out_specs=pl.BlockSpec((rows, tn), lambda j: (0, j)),
        out_shape=jax.ShapeDtypeStruct((rows, n), F32),
        compiler_params=_params("parallel"),
        name="modulation",
    )(c, *[w] * ns, b.reshape(1, n))


def _win_layout_kernel(w_ref, o_ref):
    qkv_end = RWKV_PROJ + 2 * GLA_KEY_WIDTH + GLA_WIDTH
    gate_end = qkv_end + GLA_GATE_RANK
    o_ref[:qkv_end] = w_ref[:qkv_end].astype(BF16)
    o_ref[qkv_end:qkv_end + GLA_WIDTH] = w_ref[gate_end:gate_end + GLA_WIDTH].astype(BF16)
    o_ref[qkv_end + GLA_WIDTH:RWKV_PROJ + GLA_PROJ] = w_ref[qkv_end:gate_end].astype(BF16)
    o_ref[RWKV_PROJ + GLA_PROJ:] = jnp.zeros((IN_PROJ_PAD - RWKV_PROJ - GLA_PROJ, o_ref.shape[1]), BF16)


def _win_layout(wt, cols):
    n, d = wt.shape
    return pl.pallas_call(
        _win_layout_kernel,
        grid=(d // cols,),
        in_specs=[pl.BlockSpec((n, cols), lambda i: (0, i))],
        out_specs=pl.BlockSpec((IN_PROJ_PAD, cols), lambda i: (0, i)),
        out_shape=jax.ShapeDtypeStruct((IN_PROJ_PAD, d), BF16),
        compiler_params=_params("parallel"),
        name="w_in_layout",
    )(wt)


def _rms_mod(x, g, sc, sh):
    ms = jnp.mean(x * x, axis=-1, keepdims=True)
    return (x * lax.rsqrt(ms + RMS_EPS) * g) * (1.0 + sc) + sh


def _token_tile(bsz, seq, rows):
    if seq >= rows:
        assert seq % rows == 0
        return 1, rows
    nb = min(bsz, rows // seq)
    assert bsz % nb == 0
    return nb, seq


def _inproj_kernel(x_ref, sc_ref, sh_ref, g_ref, w_ref, o_ref, last_ref, *, n_step, n_slices):
    nb, rb, d = x_ref.shape
    h = _rms_mod(x_ref[...], g_ref[...], sc_ref[...], sh_ref[...])
    hb = h.reshape(nb * rb, d).astype(BF16)
    for j in range(IN_PROJ_PAD // n_step):
        cols = slice(j * n_step, (j + 1) * n_step)
        o_ref[:, :, cols] = _dg(hb, w_ref[cols, :], _NT).reshape(nb, rb, n_step)
    rows = rb // n_slices
    for j in range(n_slices):
        last_ref[j * nb:(j + 1) * nb] = o_ref[:, (j + 1) * rows - 1:(j + 1) * rows, :RWKV_PROJ]


def _in_proj(x, sc, sh, g, w, rows, slice_rows):
    bsz, seq, d = x.shape
    nb, rb = _token_tile(bsz, seq, rows)
    n_seq_tiles = seq // rb
    n_slices = max(rb // slice_rows, 1)
    assert nb == 1 or n_slices == 1
    vec = pl.BlockSpec((nb, 1, d), lambda b, i: (b, 0, 0))
    return pl.pallas_call(
        functools.partial(_inproj_kernel, n_step=2 * LANES, n_slices=n_slices),
        grid=(bsz // nb, n_seq_tiles),
        in_specs=[pl.BlockSpec((nb, rb, d), lambda b, i: (b, i, 0)), vec, vec,
                  pl.BlockSpec((1, d), lambda b, i: (0, 0)),
                  pl.BlockSpec((IN_PROJ_PAD, d), lambda b, i: (0, 0))],
        out_specs=[pl.BlockSpec((nb, rb, IN_PROJ_PAD), lambda b, i: (b, i, 0)),
                   pl.BlockSpec((nb * n_slices, 1, RWKV_PROJ), lambda b, i: (b * n_seq_tiles + i, 0, 0))],
        out_shape=[jax.ShapeDtypeStruct((bsz, seq, IN_PROJ_PAD), F32),
                   jax.ShapeDtypeStruct((bsz * n_seq_tiles * n_slices, 1, RWKV_PROJ), F32)],
        compiler_params=_params("parallel", "parallel"),
        name="norm1_in_proj",
    )(x, sc, sh, g, w)


def _rwkv_prep_kernel(p_ref, prev_ref, mu_ref, w0_ref, a0_ref, kk_ref, ka_ref, rk_ref,
                      wd_ref, wa_ref, wg_ref, seg_ref, tri_ref,
                      ab_ref, rkv_ref, bg_ref, gl_ref):
    nb, rb, wp = p_ref.shape
    tr = nb * rb
    p = p_ref[...].reshape(tr, wp)
    row = lax.broadcasted_iota(jnp.int32, (nb, rb, wp), 1)
    xx = jnp.where(row == 0, prev_ref[...], pltpu.roll(p, 1, 0).reshape(nb, rb, wp)).reshape(tr, wp)
    ps = p + (xx - p) * mu_ref[...]
    w = RWKV_WIDTH
    r, k, v, lora = ps[:, :w], ps[:, w:2 * w], ps[:, 2 * w:3 * w], ps[:, 3 * w:]
    logw = -_softplus(-(w0_ref[...] + _mm1(jnp.tanh(lora), wd_ref[...]))) - 0.5
    lw = jnp.exp(logw) * (-LOG2_E)
    a = _sigmoid(a0_ref[...] + _mm1(lora, wa_ref[...]))
    g = _mm1(_sigmoid(lora), wg_ref[...])
    seg = seg_ref[...]
    kk = k * kk_ref[...]
    kk = kk * lax.rsqrt(jnp.maximum(_mm2_exact_rhs(kk * kk, seg), 1e-24))
    k2 = k * (1.0 + (a - 1.0) * ka_ref[...])
    cum = _mm_exact_lhs(tri_ref[...], lw)
    lasts = [jnp.exp2(cum[c * CHUNK + CHUNK - 1:(c + 1) * CHUNK, :]) for c in range(tr // CHUNK)]
    for c, l in enumerate(lasts):
        gl_ref[c] = l

    def to_chunk_end(val):
        return jnp.concatenate([val[c * CHUNK:(c + 1) * CHUNK] * l for c, l in enumerate(lasts)], axis=0)

    gamma = jnp.exp2(cum)
    ginv = 1.0 / gamma
    bt = (kk * a) * ginv
    kt = k2 * ginv

    def put(ref, slot, val):
        ref[:, :, slot * w:(slot + 1) * w] = val.reshape(nb, rb, w).astype(ref.dtype)

    put(ab_ref, 0, -kk * jnp.exp2(cum - lw))
    put(ab_ref, 1, bt)
    put(rkv_ref, 0, r * gamma)
    put(rkv_ref, 1, kt)
    put(rkv_ref, 2, to_chunk_end(kt))
    put(rkv_ref, 3, v)
    put(bg_ref, 0, _mm2_exact_rhs(r * k2 * rk_ref[...], seg) * v)
    put(bg_ref, 1, g)


def _rwkv_prep(proj, prev, lp, consts, rows):
    bsz, seq, _ = proj.shape
    w = RWKV_WIDTH
    nb, rb = _token_tile(bsz, seq, rows)
    n_seq_tiles = seq // rb
    flat = lambda b, i: (b * n_seq_tiles + i, 0)
    row = lambda n: pl.BlockSpec((1, n), lambda b, i: (0, 0))
    full = lambda a: pl.BlockSpec(a.shape, lambda b, i: (0,) * a.ndim)
    packed = [(2, F32), (4, BF16), (2, F32)]
    return pl.pallas_call(
        _rwkv_prep_kernel,
        grid=(bsz // nb, n_seq_tiles),
        in_specs=[pl.BlockSpec((nb, rb, RWKV_PROJ), lambda b, i: (b, i, 0)),
                  pl.BlockSpec((nb, 1, RWKV_PROJ), lambda b, i: flat(b, i) + (0,)),
                  row(RWKV_PROJ), row(w), row(w), row(w), row(w), row(w),
                  full(lp["wd"]), full(lp["wa"]), full(lp["wg"]), full(consts["seg"]), full(consts["tri"])],
        out_specs=[pl.BlockSpec((nb, rb, k * w), lambda b, i: (b, i, 0)) for k, _ in packed]
        + [pl.BlockSpec((nb * rb // CHUNK, 1, w), lambda b, i: flat(b, i) + (0,))],
        out_shape=[jax.ShapeDtypeStruct((bsz, seq, k * w), dt) for k, dt in packed]
        + [jax.ShapeDtypeStruct((bsz * seq // CHUNK, 1, w), F32)],
        compiler_params=_params("parallel", "parallel"),
        name="rwkv_prep",
    )(proj, prev, lp["mu"], lp["w0"], lp["a0"], lp["k_k"], lp["k_a"], lp["r_k"],
      lp["wd"], lp["wa"], lp["wg"], consts["seg"], consts["tri"])


def _head_masks(width):
    lane = lax.broadcasted_iota(jnp.int32, (CHUNK, width), 1)
    row = lax.broadcasted_iota(jnp.int32, (CHUNK, width), 0)
    return lane >> HEAD_SHIFT, row, lane & (HEAD_LANES - 1)


def _block_diag(x, head):
    z = jnp.zeros_like(x)
    return jnp.concatenate([jnp.where(head == j, x, z) for j in range(x.shape[1] // HEAD_LANES)], axis=0)


def _rwkv_stages(ab_ref, rkv_ref, gl_ref, s0_ref, cast_in, y_ref, s_ref, cast_out, bb, nck):
    @pl.when(pl.program_id(1) == 0)
    def _():
        s_ref[...] = s0_ref[...]

    for src, dst in zip(cast_in, cast_out):
        dst[...] = src[...].astype(BF16)

    tw = RWKV_TILE_LANES
    head, row, col = _head_masks(tw)
    strict = col < row
    incl = col <= row
    same8 = (col >> 3) == (row >> 3)
    lane2 = lax.broadcasted_iota(jnp.int32, (tw, tw), 1)
    row2 = lax.broadcasted_iota(jnp.int32, (tw, tw), 0)
    same_head = (lane2 >> HEAD_SHIFT) == (row2 >> HEAD_SHIFT)
    bd = functools.partial(_block_diag, head=head)
    c = CHUNK

    def pmm(p, q):
        return _dg(p.astype(BF16), bd(q.astype(BF16)))

    items = [(b, ck, pr) for b in range(bb) for ck in range(nck) for pr in range(RWKV_TILES)]
    n = range(len(items))
    rows = lambda ck: slice(ck * c, (ck + 1) * c)
    lanes = lambda pr: slice(pr * tw, (pr + 1) * tw)
    w = RWKV_WIDTH

    def ld(ref, slot):
        return [ref[b, rows(ck), slot * w + pr * tw:slot * w + (pr + 1) * tw] for b, ck, pr in items]

    cat0 = lambda *xs: jnp.concatenate(xs, axis=0)
    cat1 = lambda *xs: jnp.concatenate(xs, axis=1)
    at, bt = ld(ab_ref, 0), ld(ab_ref, 1)
    rt, kt, ke, v = (ld(rkv_ref, slot) for slot in range(4))
    be = [(bt[i] * gl_ref[b, ck, :, lanes(pr)]).astype(BF16) for i, (b, ck, pr) in enumerate(items)]
    ats, bts = [_split2(a) for a in at], [_split2(b) for b in bt]
    atb = [hi for hi, _ in ats]
    zero = jnp.zeros((c, tw), F32)
    gk = [_dg(cat0(atb[i], rt[i]), bd(kt[i]), _NT) for i in n]
    aak = [jnp.where(strict, gk[i][:c], zero) for i in n]
    ark = [jnp.where(incl, gk[i][c:], zero) for i in n]
    yield
    gb = [_dg(cat0(ats[i][0], ats[i][1], rt[i]), bd(bts[i][0]), _NT) for i in n]
    arb = [jnp.where(incl, gb[i][2 * c:], zero) for i in n]
    aab = [jnp.where(strict, gb[i][:c] + (gb[i][c:2 * c] + _dg(atb[i], bd(bts[i][1]), _NT)), zero) for i in n]
    yield
    z = [pmm(aak[i], v[i]) for i in n]
    a8 = [jnp.where(same8, aab[i], zero) for i in n]
    p2 = [pmm(a8[i], a8[i]) for i in n]
    yield
    p4 = [pmm(p2[i], p2[i]) for i in n]
    yield
    nn = [a8[i] + p2[i] + pmm(p2[i], a8[i]) for i in n]
    yield
    nn = [nn[i] + p4[i] + pmm(p4[i], nn[i]) for i in n]
    yield
    for lvl in (3, 4, 5):
        joins = ((col >> (lvl + 1)) == (row >> (lvl + 1))) & ((col >> lvl) != (row >> lvl))
        e = [jnp.where(joins, aab[i], zero) for i in n]
        te = [e[i] + pmm(nn[i], e[i]) for i in n]
        yield
        nn = [nn[i] + te[i] + pmm(te[i], nn[i]) for i in n]
        yield
    wu = [cat1(at[i], z[i]) + _dg(nn[i].astype(BF16), cat1(bd(atb[i]), bd(z[i].astype(BF16)))) for i in n]
    abk = [cat1(arb[i], ark[i]).astype(BF16) for i in n]
    yield
    s = {(b, pr): s_ref[b, pr] for b in range(bb) for pr in range(RWKV_TILES)}
    for ck in range(nck):
        cur = [i for i in n if items[i][1] == ck]
        key = lambda i: (items[i][0], items[i][2])
        x = {i: _dg(cat0(wu[i][:, :tw].astype(BF16), rt[i]), s[key(i)].astype(BF16), _NT) for i in cur}
        ub = {i: (x[i][:c] + wu[i][:, tw:]).astype(BF16) for i in cur}
        yield
        upd = {i: _dg(cat0(ub[i], v[i]), cat0(be[i], ke[i]), _TN) for i in cur}
        for i in cur:
            b, _, pr = items[i]
            y_ref[b, rows(ck), lanes(pr)] = x[i][c:] + _dg(abk[i], cat0(bd(ub[i]), bd(v[i])))
            s[b, pr] = (s[b, pr] * gl_ref[b, ck, :, lanes(pr)]
                        + jnp.where(same_head, upd[i], jnp.zeros_like(upd[i])))
        yield
    for (b, pr), val in s.items():
        s_ref[b, pr] = val


def _gla_stages(p_ref, wgate_ref, bgate_ref, tri_ref, ng_ref, s0_ref, y_ref, s_ref, bb, nck):
    @pl.when(pl.program_id(1) == 0)
    def _():
        s_ref[...] = s0_ref[...]

    head, row, col = _head_masks(LANES)
    incl = col <= row
    bd = functools.partial(_block_diag, head=head)
    lane2 = lax.broadcasted_iota(jnp.int32, (2 * GLA_DV, LANES), 1)
    row2 = lax.broadcasted_iota(jnp.int32, (2 * GLA_DV, LANES), 0)
    same_head = (lane2 < GLA_DK) == (row2 < GLA_DV)
    kw, gw = GLA_KEY_WIDTH, GLA_WIDTH
    zero = jnp.zeros((CHUNK, LANES), F32)
    cat0 = lambda xs: jnp.concatenate(xs, axis=0)
    cat1 = lambda xs: jnp.concatenate(xs, axis=1)
    rows = lambda c: slice(c * CHUNK, (c + 1) * CHUNK)
    lanes = lambda pr: slice(pr * LANES, (pr + 1) * LANES)
    bs, cs = range(bb), range(nck)
    x = [p_ref[b] for b in bs]
    gate = [_mm1(x[b][:, 2 * kw + 2 * gw:], wgate_ref[...]) + bgate_ref[...] for b in bs]
    la = [(jnp.minimum(g, 0.0) - jnp.log(1.0 + jnp.exp(-jnp.abs(g)))) / GLA_TAU for g in gate]
    yield
    cum = [_mm_exact_lhs(tri_ref[...], la[b]) for b in bs]
    yield
    q_dec = [(x[b][:, :kw] * (GLA_DK ** -0.5)) * jnp.exp(cum[b]) for b in bs]
    k_inv = [x[b][:, kw:2 * kw] * jnp.exp(-cum[b]) for b in bs]
    last = [[cum[b][c * CHUNK + CHUNK - 1:(c + 1) * CHUNK, :] for c in cs] for b in bs]
    k_end = [[x[b][rows(c), kw:2 * kw] * jnp.exp(last[b][c] - cum[b][rows(c)]) for c in cs] for b in bs]
    dec = [[jnp.exp(last[b][c]) for c in cs] for b in bs]
    yield
    items = [(b, c, pr) for b in bs for c in cs for pr in range(GLA_PAIRS)]
    vp = {(b, c, pr): x[b][rows(c), 2 * kw + pr * 2 * GLA_DV:2 * kw + (pr + 1) * 2 * GLA_DV]
          for b, c, pr in items}
    att = {(b, c, pr): jnp.where(incl, _mm1(q_dec[b][rows(c), lanes(pr)],
                                            bd(k_inv[b][rows(c), lanes(pr)]), _NT), zero)
           for b, c, pr in items}
    yield
    upd = {it: _mm1(vp[it], k_end[it[0]][it[1]][:, lanes(it[2])], _TN) for it in items}
    yield
    intra = {it: _mm1(att[it], cat0([cat1([vp[it][:, :GLA_DV], zero]), cat1([zero, vp[it][:, GLA_DV:]])]))
             for it in items}
    yield
    st = {}
    for b in bs:
        for pr in range(GLA_PAIRS):
            cur = s_ref[b, pr]
            for c in cs:
                st[b, c, pr] = cur
                cur = cur * dec[b][c][:, lanes(pr)] + jnp.where(same_head, upd[b, c, pr], jnp.zeros_like(cur))
            s_ref[b, pr] = cur
    yield
    o = {(b, c, pr): intra[b, c, pr] + _mm1(q_dec[b][rows(c), lanes(pr)], st[b, c, pr], _NT)
         for b, c, pr in items}
    yield
    for b in bs:
        ob = cat0([cat1([o[b, c, pr] for pr in range(GLA_PAIRS)]) for c in cs])
        heads = [ob[:, h * GLA_DV:(h + 1) * GLA_DV] for h in range(GLA_HEADS)]
        normed = [oh * lax.rsqrt(jnp.mean(oh * oh, axis=-1, keepdims=True) + RMS_EPS) for oh in heads]
        y_ref[b] = cat1(normed) * ng_ref[...] * _silu(x[b][:, 2 * kw + gw:2 * kw + 2 * gw])


def _recurrent_kernel(*refs, bb, nck, n_cast):
    rwkv_in, refs = refs[:4], refs[4:]
    cast_in, refs = refs[:n_cast], refs[n_cast:]
    gla_in, refs = refs[:6], refs[6:]
    (y_ref, s_ref), refs = refs[:2], refs[2:]
    cast_out, (yg_ref, t_ref) = refs[:n_cast], refs[n_cast:]
    rwkv = _rwkv_stages(*rwkv_in, cast_in, y_ref, s_ref, cast_out, bb, nck)
    gla = _gla_stages(*gla_in, yg_ref, t_ref, bb, nck)
    done = object()
    live = [rwkv, rwkv, gla]
    while live:
        for gen in list(live):
            if gen in live and next(gen, done) is done:
                live = [g for g in live if g is not gen]


def _recurrent(ab, rkv, gl, s0, proj, lp, tri, t0, bb, nck, casts=()):
    bsz, seq, _ = ab.shape
    w = RWKV_WIDTH
    rows = nck * CHUNK
    n_seq_steps = seq // rows
    n_steps = (bsz // bb) * n_seq_steps
    tok = lambda a: pl.BlockSpec((bb, rows, a.shape[-1]), lambda i, c: (i, c, 0))
    full = lambda a: pl.BlockSpec(a.shape, lambda i, c: (0,) * a.ndim)
    st = pl.BlockSpec((bb, RWKV_TILES, RWKV_TILE_LANES, RWKV_TILE_LANES), lambda i, c: (i, 0, 0, 0))
    gst = pl.BlockSpec((bb, GLA_PAIRS, 2 * GLA_DV, LANES), lambda i, c: (i, 0, 0, 0))
    slab = lambda a: pl.BlockSpec((a.shape[0] // n_steps, a.shape[1]), lambda i, c: (i * n_seq_steps + c, 0))
    assert all(a.shape[0] % (16 * n_steps) == 0 for a in casts)
    out_tok = pl.BlockSpec((bb, rows, w), lambda i, c: (i, c, 0))
    return pl.pallas_call(
        functools.partial(_recurrent_kernel, bb=bb, nck=nck, n_cast=len(casts)),
        grid=(bsz // bb, n_seq_steps),
        in_specs=[tok(ab), tok(rkv), pl.BlockSpec((bb, nck, 1, w), lambda i, c: (i, c, 0, 0)), st]
        + [slab(a) for a in casts]
        + [pl.BlockSpec((bb, rows, GLA_PROJ_PAD), lambda i, c: (i, c, 1)),
           full(lp["wgate"]), full(lp["bgate"]), full(tri), full(lp["gla_g"]), gst],
        out_specs=[out_tok, st] + [slab(a) for a in casts] + [out_tok, gst],
        out_shape=[jax.ShapeDtypeStruct((bsz, seq, w), F32),
                   jax.ShapeDtypeStruct((bsz, RWKV_TILES, RWKV_TILE_LANES, RWKV_TILE_LANES), F32)]
        + [jax.ShapeDtypeStruct(a.shape, BF16) for a in casts]
        + [jax.ShapeDtypeStruct((bsz, seq, GLA_WIDTH), F32),
           jax.ShapeDtypeStruct((bsz, GLA_PAIRS, 2 * GLA_DV, LANES), F32)],
        compiler_params=_params("parallel", "arbitrary"),
        name="rwkv_gla_chunk",
    )(ab, rkv, gl, s0, *casts, proj, lp["wgate"], lp["bgate"], tri, lp["gla_g"], t0)


def _route(lgt):
    tm = lgt.shape[1]
    n_pad = ROUTE_ROWS - N_EXPERTS
    rowg = lax.broadcasted_iota(jnp.int32, (n_pad, tm), 0)
    rowe = lax.broadcasted_iota(jnp.int32, (N_EXPERTS, tm), 0)
    rowg_f, rowe_f = rowg.astype(F32), rowe.astype(F32)

    def first_argmax(vals, mx, rows_f):
        return jnp.min(jnp.where(vals == mx, rows_f, float(ROUTE_ROWS)), axis=0, keepdims=True)

    is_group = rowg < N_GROUPS
    lg = jnp.where(is_group, lgt[N_EXPERTS:], -jnp.inf)
    gmax = jnp.max(lg, axis=0, keepdims=True)
    gi = first_argmax(lg, gmax, rowg_f)
    pg_top = 1.0 / jnp.sum(jnp.where(is_group, jnp.exp(lg - gmax), 0.0), axis=0, keepdims=True)
    in_group = (rowe >> GROUP_SHIFT).astype(F32) == gi
    le = jnp.where(in_group, lgt[:N_EXPERTS], -jnp.inf)
    m1 = jnp.max(le, axis=0, keepdims=True)
    i1 = first_argmax(le, m1, rowe_f)
    le2 = jnp.where(rowe_f == i1, -jnp.inf, le)
    m2 = jnp.max(le2, axis=0, keepdims=True)
    i2 = first_argmax(le2, m2, rowe_f)
    p2 = jnp.exp(m2 - m1)
    w1 = pg_top / (1.0 + p2)
    w2 = pg_top * p2 / (1.0 + p2)
    comb = jnp.where(rowe_f == i1, w1, 0.0) + jnp.where(rowe_f == i2, w2, 0.0)
    return jnp.concatenate([comb, jnp.where(rowg == 0, gi, 0.0)], axis=0)


def _outproj_kernel(yr_ref, bg_ref, yg_ref, x_ref, gt_ref, sc_ref, sh_ref,
                    lnw_ref, lnb_ref, seg_ref, wout_ref, g2_ref, wr_ref, br_ref,
                    x1_ref, h2_ref, route_ref, *, n_parts):
    nb, rb, d_model = x_ref.shape
    if nb % n_parts == 0:
        pnb, prb = nb // n_parts, rb
        tok = lambda p: (slice(p * pnb, (p + 1) * pnb), slice(None), slice(None))
        seq_of = tok
    else:
        pnb, prb = nb, rb // n_parts
        tok = lambda p: (slice(None), slice(p * prb, (p + 1) * prb), slice(None))
        seq_of = lambda p: (slice(None), slice(None), slice(None))
    tm = pnb * prb
    parts = range(n_parts)
    flat = lambda ref, p: ref[tok(p)].reshape(tm, ref.shape[-1])
    seg = seg_ref[...]
    inv_n = 1.0 / RWKV_HEAD
    ds = [y - _mm2_exact_rhs(y, seg) * inv_n for y in (flat(yr_ref, p) for p in parts)]
    vs = [_mm2_exact_rhs(d * d, seg) * inv_n for d in ds]
    mixes = []
    for p in parts:
        yn = ds[p] * lax.rsqrt(vs[p] + LNX_EPS) * lnw_ref[...] + lnb_ref[...]
        bg = flat(bg_ref, p)
        yr = (yn + bg[:, :RWKV_WIDTH]) * bg[:, RWKV_WIDTH:]
        mixes.append(jnp.concatenate([yr, flat(yg_ref, p)], axis=1))
    outs = [_mm1(mix, wout_ref[...]) for mix in mixes]
    h2s = []
    for p in parts:
        x1 = x_ref[tok(p)] + gt_ref[seq_of(p)] * outs[p].reshape(pnb, prb, d_model)
        x1_ref[tok(p)] = x1
        h2 = _rms_mod(x1, g2_ref[...], sc_ref[seq_of(p)], sh_ref[seq_of(p)])
        h2_ref[tok(p)] = h2.astype(BF16)
        h2s.append(h2.reshape(tm, d_model))
    logits = [_mm3(wr_ref[...], h2, _NT) + br_ref[...] for h2 in h2s]
    for p in parts:
        route_ref[:, p * tm:(p + 1) * tm] = _route(logits[p])


def _out_proj(yr, bg, yg, x, gt, sc, sh, lp, consts, rows):
    bsz, seq, d = x.shape
    nb, rb = _token_tile(bsz, seq, rows)
    n_seq_tiles = seq // rb
    half = pl.BlockSpec((nb, rb, RWKV_WIDTH), lambda b, i: (b, i, 0))
    tokd = pl.BlockSpec((nb, rb, d), lambda b, i: (b, i, 0))
    vec = pl.BlockSpec((nb, 1, d), lambda b, i: (b, 0, 0))
    full = lambda a: pl.BlockSpec(a.shape, lambda b, i: (0,) * a.ndim)
    args = (lp["lnx_w"], lp["lnx_b"], consts["seg"], lp["w_out"], lp["norm2_g"], lp["w_router"], lp["b_router"])
    return pl.pallas_call(
        functools.partial(_outproj_kernel, n_parts=OUT_PROJ_PARTS),
        grid=(bsz // nb, seq // rb),
        in_specs=[half, pl.BlockSpec((nb, rb, 2 * RWKV_WIDTH), lambda b, i: (b, i, 0)), half, tokd,
                  vec, vec, vec] + [full(a) for a in args],
        out_specs=[tokd, tokd, pl.BlockSpec((ROUTE_ROWS, nb * rb), lambda b, i: (0, b * n_seq_tiles + i))],
        out_shape=[jax.ShapeDtypeStruct((bsz, seq, d), F32),
                   jax.ShapeDtypeStruct((bsz, seq, d), BF16),
                   jax.ShapeDtypeStruct((ROUTE_ROWS, bsz * seq), F32)],
        compiler_params=_params("parallel", "parallel"),
        name="out_proj_router",
    )(yr, bg, yg, x, gt, sc, sh, *args)


def _moe_kernel(h_ref, route_ref, x1_ref, gt_ref, sc_ref, sh_ref, gf_ref, tri_ref, wg_ref, wu_ref, wd_ref,
                y_ref, ys_ref, pos_ref, meta_ref):
    g = pl.program_id(2)
    nb, rb, d = h_ref.shape
    tm = nb * rb
    sub, tail = MOE_SUB_ROWS, MOE_SUB_ROWS // 2
    cap = ys_ref.shape[0]
    n_lane_tiles = tm // LANES
    lane_tile = lambda k: slice(k * LANES, (k + 1) * LANES)

    @pl.when(g == 0)
    def _sort():
        row8 = lax.broadcasted_iota(jnp.int32, (8, LANES), 0).astype(F32)
        carry = jnp.zeros((8, 1), F32)
        members, ranks = [], []
        for k in range(n_lane_tiles):
            blk = jnp.where(row8 == route_ref[N_EXPERTS:N_EXPERTS + 1, lane_tile(k)], 1.0, 0.0)
            members.append(blk)
            ranks.append(_dg(blk.astype(BF16), tri_ref[...]) - blk + carry)
            carry = carry + jnp.sum(blk, axis=1, keepdims=True)
        rowc = lax.broadcasted_iota(jnp.int32, (8, 1), 0)
        lane = lax.broadcasted_iota(jnp.int32, (1, LANES), 1)
        first = jnp.zeros((1, 1), F32)
        off_col = jnp.zeros((8, 1), F32)
        meta = jnp.zeros((1, LANES), F32)
        for grp in range(N_GROUPS):
            count = jnp.sum(jnp.where(rowc == grp, carry, 0.0), axis=0, keepdims=True)
            blocks = jnp.ceil(count * (1.0 / tail))
            pairs = jnp.floor(blocks * 0.5)
            odd = blocks - 2.0 * pairs
            triple = odd * jnp.where(blocks >= 3.0, 1.0, 0.0)
            n_full = pairs - triple
            off_col = off_col + jnp.where(rowc == grp, first, 0.0)
            meta = (meta + jnp.where(lane == grp, first, 0.0) + jnp.where(lane == N_GROUPS + grp, n_full, 0.0)
                    + jnp.where(lane == 2 * N_GROUPS + grp, odd + 2.0 * triple, 0.0))
            first = first + blocks * tail
        meta = (meta + jnp.where(lane == 3 * N_GROUPS, first, 0.0)).astype(jnp.int32)
        for i in range(3 * N_GROUPS + 1):
            meta_ref[i] = meta[0, i]
        for k in range(n_lane_tiles):
            pos = jnp.sum(members[k] * (ranks[k] + off_col), axis=0, keepdims=True)
            pos_ref[:, lane_tile(k)] = pos.astype(jnp.int32)
        ys_ref[...] = jnp.zeros_like(ys_ref)

    def experts(base, n_rows):
        prow = lax.broadcasted_iota(jnp.int32, (n_rows, tm), 0) + base
        onehot = jnp.where(prow == pos_ref[...], 1.0, 0.0).astype(BF16)
        hs = _dg(onehot, h_ref[...].reshape(tm, d)).astype(BF16)
        comb3 = _split3(route_ref[:N_EXPERTS, :])
        cs = _dg(onehot, comb3[0], _NT) + (_dg(onehot, comb3[1], _NT) + _dg(onehot, comb3[2], _NT))
        lane = lax.broadcasted_iota(jnp.int32, cs.shape, 1)
        ys = jnp.zeros((n_rows, d), F32)
        for e in range(EXPERTS_PER_GROUP):
            hid = _silu(_dg(hs, wg_ref[e])) * _dg(hs, wu_ref[e])
            ce = jnp.sum(jnp.where(lane == g * EXPERTS_PER_GROUP + e, cs, 0.0), axis=-1, keepdims=True)
            ys = ys + ce * _dg(hid.astype(BF16), wd_ref[e])
        ys_ref[pl.ds(base, n_rows), :] = ys.astype(BF16)

    first_row = meta_ref[g]
    n_full = meta_ref[N_GROUPS + g]

    def full_sub_tile(j, carry_):
        experts(pl.multiple_of(first_row + j * sub, tail), sub)
        return carry_

    lax.fori_loop(0, n_full, full_sub_tile, 0)

    for last_blocks in (1, 3):
        @pl.when(meta_ref[2 * N_GROUPS + g] == last_blocks)
        def _last(last_blocks=last_blocks):
            experts(pl.multiple_of(first_row + n_full * sub, tail), last_blocks * tail)

    def unsort(n_sorted):
        ys_all = ys_ref[:n_sorted]
        prow = lax.broadcasted_iota(jnp.int32, (n_sorted, LANES), 0)
        seqs = max(LANES // rb, 1)
        for k in range(n_lane_tiles):
            onehot = jnp.where(prow == pos_ref[:, lane_tile(k)], 1.0, 0.0).astype(BF16)
            moe = _dg(onehot, ys_all, _TN).reshape(seqs, LANES // seqs, d)
            b0 = k * LANES // rb
            r0 = k * LANES - b0 * rb
            bs, rs = slice(b0, b0 + seqs), slice(r0, r0 + LANES // seqs)
            x2 = x1_ref[bs, rs] + gt_ref[bs] * moe
            y_ref[bs, rs] = _rms_mod(x2, gf_ref[...], sc_ref[bs], sh_ref[bs])

    usual = tm + (cap - tm) // 2
    last = g == N_GROUPS - 1
    pl.when(last & (meta_ref[3 * N_GROUPS] <= usual))(lambda: unsort(usual))
    pl.when(last & (meta_ref[3 * N_GROUPS] > usual))(lambda: unsort(cap))


def _moe(h2, route, x1, gt, sc, sh, gf, wg, wu, wd, tri, rows):
    bsz, seq, d = x1.shape
    nb, rb = _token_tile(bsz, seq, rows)
    tm = nb * rb
    assert tm % LANES == 0 and (rb % LANES == 0 or LANES % rb == 0)
    n_seq_tiles = seq // rb
    cap = tm + N_GROUPS * (MOE_SUB_ROWS // 2)
    tokd = pl.BlockSpec((nb, rb, d), lambda b, i, g: (b, i, 0))
    vec = pl.BlockSpec((nb, 1, d), lambda b, i, g: (b, 0, 0))
    group_w = lambda shape: pl.BlockSpec((EXPERTS_PER_GROUP,) + shape, lambda b, i, g: (g, 0, 0))
    return pl.pallas_call(
        _moe_kernel,
        grid=(bsz // nb, n_seq_tiles, N_GROUPS),
        in_specs=[tokd, pl.BlockSpec((ROUTE_ROWS, tm), lambda b, i, g: (0, b * n_seq_tiles + i)),
                  tokd, vec, vec, vec,
                  pl.BlockSpec((1, d), lambda b, i, g: (0, 0)),
                  pl.BlockSpec(tri.shape, lambda b, i, g: (0, 0)),
                  group_w((d, D_EXPERT)), group_w((d, D_EXPERT)), group_w((D_EXPERT, d))],
        out_specs=tokd,
        out_shape=jax.ShapeDtypeStruct((bsz, seq, d), F32),
        scratch_shapes=[pltpu.VMEM((cap, d), BF16), pltpu.VMEM((1, tm), jnp.int32),
                        pltpu.SMEM((3 * N_GROUPS + 1,), jnp.int32)],
        compiler_params=_params("parallel", "parallel", "arbitrary"),
        name="moe_final_norm",
    )(h2, route, x1, gt, sc, sh, gf, tri, wg, wu, wd)


def _block_ones(n, blk, lower):
    i = np.arange(n)
    m = (i[:, None] // blk) == (i[None, :] // blk)
    if lower:
        m = m & (i[None, :] <= i[:, None])
    return jnp.asarray(m, dtype=BF16)


def _consts(tr, tg):
    return dict(seg=_block_ones(RWKV_WIDTH, RWKV_HEAD, False), tri=_block_ones(tr, CHUNK, True),
                tri_gla=_block_ones(tg, CHUNK, True), tri_up=_block_ones(LANES, LANES, True).T)


def _heads_block_diag(s):
    n, c = s.shape[2], s.shape[-1]
    rows = [jnp.pad(s[:, :, j], ((0, 0), (0, 0), (0, 0), (j * c, (n - 1 - j) * c))) for j in range(n)]
    return jnp.concatenate(rows, axis=2)


def _diag_blocks(t, n):
    r, c = t.shape[-2] // n, t.shape[-1] // n
    return jnp.stack([t[:, :, j * r:(j + 1) * r, j * c:(j + 1) * c] for j in range(n)], axis=2)


def _pad_rows(w, first_row):
    out = jnp.zeros((LORA_PAD, w.shape[1]), F32)
    return lax.dynamic_update_slice(out, w, (first_row, 0)).astype(BF16)


def _layer_params(l, w_in, mu_shift, w0, w_decay_up, a0, w_a_up, w_g_up, k_k, k_a, r_k, lnx_w, lnx_b,
                  w_gla_gate_up, b_gla_gate, gla_norm_g, w_out, norm2_g,
                  w_router_group, b_router_group, w_router_expert, b_router_expert):
    w_in_p = _win_layout(w_in[l].T, WEIGHT_TILE_COLS)
    n_pad = ROUTE_ROWS - N_EXPERTS - N_GROUPS
    w_router = jnp.concatenate([w_router_expert[l].T, w_router_group[l].T, jnp.zeros((n_pad, D_MODEL), F32)])
    b_router = jnp.concatenate([b_router_expert[l], b_router_group[l],
                                jnp.zeros((n_pad,), F32)]).reshape(ROUTE_ROWS, 1)
    wgate = jnp.zeros((LANES, GLA_KEY_WIDTH), F32).at[:GLA_GATE_RANK].set(w_gla_gate_up[l]).astype(BF16)
    r1 = lambda a: a.reshape(1, -1)
    return dict(
        w_in=w_in_p, mu=r1(mu_shift[l]), w0=r1(w0[l]), a0=r1(a0[l]), k_k=r1(k_k[l]), k_a=r1(k_a[l]),
        r_k=r1(r_k[l]), wd=_pad_rows(w_decay_up[l], 0), wa=_pad_rows(w_a_up[l], DECAY_LORA),
        wg=_pad_rows(w_g_up[l], DECAY_LORA + AAA_LORA), lnx_w=r1(lnx_w[l]), lnx_b=r1(lnx_b[l]),
        wgate=wgate, bgate=r1(b_gla_gate[l]), gla_g=r1(gla_norm_g[l]),
        w_out=w_out[l].astype(BF16), norm2_g=r1(norm2_g[l]), w_router=w_router, b_router=b_router)


def _run_layer(x, mod, shift0, wkv0, gla0, lp, experts, final):
    bsz, seq, d = x.shape
    assert seq % CHUNK == 0
    bb = 2 if seq >= RECURRENT_STEP_ROWS else min(bsz, RECURRENT_MAX_SEQS)
    assert bsz % bb == 0
    nck = min(seq, RECURRENT_STEP_ROWS) // CHUNK
    consts = _consts(TOKEN_TILE_ROWS, nck * CHUNK)
    m = lambda j: mod[:, j:j + 1, :]
    sh1, sc1, gt1, sh2, sc2, gt2 = (m(j) for j in range(6))
    proj, tails = _in_proj(x, sc1, sh1, lp["norm1_g"], lp["w_in"], PROJ_TILE_ROWS, TOKEN_TILE_ROWS)
    tails = tails.reshape(bsz, -1, RWKV_PROJ)
    new_shift = tails[:, -1]
    prev = jnp.concatenate([shift0[:, None, :], tails[:, :-1]], axis=1).reshape(-1, 1, RWKV_PROJ)
    ab, rkv, bg, gl = _rwkv_prep(proj, prev, lp, consts, TOKEN_TILE_ROWS)
    gl = gl.reshape(bsz, seq // CHUNK, 1, RWKV_WIDTH)
    s0 = _heads_block_diag(wkv0.reshape(bsz, RWKV_TILES, RWKV_TILE_HEADS, RWKV_HEAD, RWKV_HEAD))
    t0 = _heads_block_diag(jnp.swapaxes(gla0, -1, -2).reshape(bsz, GLA_PAIRS, 2, GLA_DV, GLA_DK))
    casts = tuple(e.reshape(-1, e.shape[-1]) for e in experts if e.dtype != BF16)
    yr, s_bd, *cast, yg, t_bd = _recurrent(ab, rkv, gl, s0, proj, lp, consts["tri_gla"], t0,
                                           bb, nck, casts)
    if cast:
        experts = tuple(c.reshape(e.shape) for c, e in zip(cast, experts))
    new_wkv = _diag_blocks(s_bd, RWKV_TILE_HEADS).reshape(bsz, RWKV_HEADS, RWKV_HEAD, RWKV_HEAD)
    new_gla = jnp.swapaxes(_diag_blocks(t_bd, 2).reshape(bsz, GLA_HEADS, GLA_DV, GLA_DK), -1, -2)
    x1, h2, route = _out_proj(yr, bg, yg, x, gt1, sc2, sh2, lp, consts, PROJ_TILE_ROWS)
    out = _moe(h2, route, x1, gt2, *final, *experts, consts["tri_up"], MOE_TILE_ROWS)
    return experts, out, new_shift, new_wkv, new_gla


def kernel(x_prompt, x_sample, c_prompt, c_sample, state_rwkv_shift, state_rwkv_wkv, state_gla_kv, w_ada, b_ada, norm1_g, norm2_g, w_in, mu_shift, w0, w_decay_up, a0, w_a_up, w_g_up, k_k, k_a, r_k, lnx_w, lnx_b, w_gla_gate_up, b_gla_gate, gla_norm_g, w_out, w_router_group, b_router_group, w_router_expert, b_router_expert, w_expert_gate, w_expert_up, w_expert_down, w_ada_final, b_ada_final, normf_g):
    assert w_ada.shape[0] == 1, "the final norm is fused into the single layer's MoE kernel"
    bp, bs = x_prompt.shape[0], x_sample.shape[0]
    d = D_MODEL
    n_rows = -(-(bp + bs) // 8) * 8
    c_all = jnp.zeros((n_rows, d), F32).at[:bp].set(c_prompt).at[bp:bp + bs].set(c_sample)
    modf = _modulation(c_all, w_ada_final, b_ada_final, MODULATION_TILE_COLS).reshape(n_rows, 2, d)
    mod = _modulation(c_all, w_ada, b_ada[0], MODULATION_TILE_COLS, layer=0).reshape(n_rows, 6, d)
    lp = _layer_params(0, w_in, mu_shift, w0, w_decay_up, a0, w_a_up, w_g_up, k_k, k_a, r_k, lnx_w,
                       lnx_b, w_gla_gate_up, b_gla_gate, gla_norm_g, w_out, norm2_g,
                       w_router_group, b_router_group, w_router_expert, b_router_expert)
    lp["norm1_g"] = norm1_g[0].reshape(1, d)
    experts = (w_expert_gate[0], w_expert_up[0], w_expert_down[0])
    groups = [
        (x_prompt, 0, bp, jnp.zeros((bp, RWKV_PROJ), F32),
         jnp.zeros((bp, RWKV_HEADS, RWKV_HEAD, RWKV_HEAD), F32), jnp.zeros((bp, GLA_HEADS, GLA_DK, GLA_DV), F32)),
        (x_sample, bp, bp + bs, state_rwkv_shift[0], state_rwkv_wkv[0], state_gla_kv[0]),
    ]
    ys, states = [], []
    for x, lo, hi, shift0, wkv0, gla0 in groups:
        final = (modf[lo:hi, 1:2], modf[lo:hi, 0:1], normf_g.reshape(1, d))
        experts, y, *st = _run_layer(x, mod[lo:hi], shift0, wkv0, gla0, lp, experts, final)
        ys.append(y)
        states.extend(s[None] for s in st)
    return tuple(ys + states)
```

```python
import functools

import jax
import jax.numpy as jnp
import numpy as np
from jax import lax
from jax.experimental import pallas as pl
from jax.experimental.pallas import tpu as pltpu

F32 = jnp.float32
BF16 = jnp.bfloat16

LANES = 128
VMEM_LIMIT_BYTES = 56 * 1024 * 1024
TOKEN_TILE_ROWS = 512
PROJ_TILE_ROWS = 1024
OUT_PROJ_PARTS = 2
MOE_TILE_ROWS = 1024
MOE_SUB_ROWS = 256
RECURRENT_STEP_ROWS = 256
RECURRENT_MAX_SEQS = 8
MODULATION_TILE_ROWS = 128
WEIGHT_TILE_COLS = 256

D_MODEL = 1024
CHUNK = 64
RWKV_WIDTH = 512
RWKV_HEAD = 64
RWKV_HEADS = RWKV_WIDTH // RWKV_HEAD
HEAD_LANES = RWKV_HEAD
HEAD_SHIFT = HEAD_LANES.bit_length() - 1
assert 1 << HEAD_SHIFT == HEAD_LANES
RWKV_TILE_LANES = 128
RWKV_TILE_HEADS = RWKV_TILE_LANES // RWKV_HEAD
RWKV_TILES = RWKV_WIDTH // RWKV_TILE_LANES
DECAY_LORA = 32
AAA_LORA = 32
GATE_LORA = 64
LORA_PAD = DECAY_LORA + AAA_LORA + GATE_LORA
RWKV_PROJ = 3 * RWKV_WIDTH + LORA_PAD
GLA_WIDTH = 512
GLA_HEADS = 4
GLA_PAIRS = GLA_HEADS // 2
GLA_DV = GLA_WIDTH // GLA_HEADS
GLA_DK = GLA_DV // 2
GLA_KEY_WIDTH = GLA_HEADS * GLA_DK
GLA_GATE_RANK = 16
GLA_TAU = 16.0
GLA_PROJ = 2 * GLA_KEY_WIDTH + 2 * GLA_WIDTH + GLA_GATE_RANK
GLA_PROJ_PAD = RWKV_PROJ
IN_PROJ_PAD = RWKV_PROJ + GLA_PROJ_PAD
N_GROUPS = 4
EXPERTS_PER_GROUP = 4
N_EXPERTS = N_GROUPS * EXPERTS_PER_GROUP
GROUP_SHIFT = EXPERTS_PER_GROUP.bit_length() - 1
assert 1 << GROUP_SHIFT == EXPERTS_PER_GROUP
ROUTE_ROWS = 24
D_EXPERT = 512
RMS_EPS = 1e-6
LNX_EPS = 64e-5
LOG2_E = 1.4426950408889634

_NN = (((1,), (0,)), ((), ()))
_NT = (((1,), (1,)), ((), ()))
_TN = (((0,), (0,)), ((), ()))


def _dg(a, b, dims=_NN):
    return lax.dot_general(a, b, dims, preferred_element_type=F32)


def _split2(x):
    hi = x.astype(BF16)
    lo = (x - hi.astype(F32)).astype(BF16)
    return hi, lo


def _split3(x):
    hi = x.astype(BF16)
    r1 = x - hi.astype(F32)
    mid = r1.astype(BF16)
    lo = (r1 - mid.astype(F32)).astype(BF16)
    return hi, mid, lo


def _mm1(a, b, dims=_NN):
    return _dg(a.astype(BF16), b.astype(BF16), dims)


def _mm3(a, b, dims=_NN):
    ah, al = _split2(a)
    bh, bl = _split2(b)
    return _dg(ah, bh, dims) + (_dg(ah, bl, dims) + _dg(al, bh, dims))


def _mm_exact_lhs(e, x, dims=_NN):
    h, m, l = _split3(x)
    return _dg(e, h, dims) + (_dg(e, m, dims) + _dg(e, l, dims))


def _mm2_exact_rhs(x, e, dims=_NN):
    h, l = _split2(x)
    return _dg(h, e, dims) + _dg(l, e, dims)


def _softplus(z):
    return jnp.maximum(z, 0.0) + jnp.log(1.0 + jnp.exp(-jnp.abs(z)))


def _sigmoid(z):
    return 1.0 / (1.0 + jnp.exp(-z))


def _silu(z):
    return z * _sigmoid(z)


def _params(*sem):
    return pltpu.CompilerParams(dimension_semantics=sem, vmem_limit_bytes=VMEM_LIMIT_BYTES)


def _mod_kernel(c_ref, w_ref, b_ref, o_ref):
    k = pl.program_id(0)
    part = _mm1(_silu(c_ref[...]), w_ref[...])

    @pl.when(k == 0)
    def _():
        o_ref[...] = part + b_ref[...]

    @pl.when(k > 0)
    def _():
        o_ref[...] += part


def _modulation(c, w, b, tk, layer=None):
    rows, d = c.shape
    n = w.shape[-1]
    if layer is None:
        w_spec = pl.BlockSpec((tk, n), lambda k: (k, 0))
    else:
        w_spec = pl.BlockSpec((None, tk, n), lambda k: (layer, k, 0))
    return pl.pallas_call(
        _mod_kernel,
        grid=(d // tk,),
        in_specs=[pl.BlockSpec((rows, tk), lambda k: (0, k)), w_spec,
                  pl.BlockSpec((1, n), lambda k: (0, 0))],
        out_specs=pl.BlockSpec((rows, n), lambda k: (0, 0)),
        out_shape=jax.ShapeDtypeStruct((rows, n), F32),
        compiler_params=_params("arbitrary"),
        name="modulation",
    )(c, w, b.reshape(1, n))


def _win_layout_kernel(w_ref, o_ref):
    qkv_end = RWKV_PROJ + 2 * GLA_KEY_WIDTH + GLA_WIDTH
    gate_end = qkv_end + GLA_GATE_RANK
    o_ref[:qkv_end] = w_ref[:qkv_end].astype(BF16)
    o_ref[qkv_end:qkv_end + GLA_WIDTH] = w_ref[gate_end:gate_end + GLA_WIDTH].astype(BF16)
    o_ref[qkv_end + GLA_WIDTH:RWKV_PROJ + GLA_PROJ] = w_ref[qkv_end:gate_end].astype(BF16)
    o_ref[RWKV_PROJ + GLA_PROJ:] = jnp.zeros((IN_PROJ_PAD - RWKV_PROJ - GLA_PROJ, o_ref.shape[1]), BF16)


def _win_layout(wt, cols):
    n, d = wt.shape
    return pl.pallas_call(
        _win_layout_kernel,
        grid=(d // cols,),
        in_specs=[pl.BlockSpec((n, cols), lambda i: (0, i))],
        out_specs=pl.BlockSpec((IN_PROJ_PAD, cols), lambda i: (0, i)),
        out_shape=jax.ShapeDtypeStruct((IN_PROJ_PAD, d), BF16),
        compiler_params=_params("parallel"),
        name="w_in_layout",
    )(wt)


def _rms_mod(x, g, sc, sh):
    ms = jnp.mean(x * x, axis=-1, keepdims=True)
    return (x * lax.rsqrt(ms + RMS_EPS) * g) * (1.0 + sc) + sh


def _token_tile(bsz, seq, rows):
    if seq >= rows:
        assert seq % rows == 0
        return 1, rows
    nb = min(bsz, rows // seq)
    assert bsz % nb == 0
    return nb, seq


def _inproj_kernel(x_ref, sc_ref, sh_ref, g_ref, w_ref, o_ref, last_ref, *, n_step, n_slices):
    nb, rb, d = x_ref.shape
    h = _rms_mod(x_ref[...], g_ref[...], sc_ref[...], sh_ref[...])
    hb = h.reshape(nb * rb, d).astype(BF16)
    for j in range(IN_PROJ_PAD // n_step):
        cols = slice(j * n_step, (j + 1) * n_step)
        o_ref[:, :, cols] = _dg(hb, w_ref[cols, :], _NT).reshape(nb, rb, n_step)
    rows = rb // n_slices
    for j in range(n_slices):
        last_ref[j * nb:(j + 1) * nb] = o_ref[:, (j + 1) * rows - 1:(j + 1) * rows, :RWKV_PROJ]


def _in_proj(x, sc, sh, g, w, rows, slice_rows):
    bsz, seq, d = x.shape
    nb, rb = _token_tile(bsz, seq, rows)
    n_seq_tiles = seq // rb
    n_slices = max(rb // slice_rows, 1)
    assert nb == 1 or n_slices == 1
    vec = pl.BlockSpec((nb, 1, d), lambda b, i: (b, 0, 0))
    return pl.pallas_call(
        functools.partial(_inproj_kernel, n_step=2 * LANES, n_slices=n_slices),
        grid=(bsz // nb, n_seq_tiles),
        in_specs=[pl.BlockSpec((nb, rb, d), lambda b, i: (b, i, 0)), vec, vec,
                  pl.BlockSpec((1, d), lambda b, i: (0, 0)),
                  pl.BlockSpec((IN_PROJ_PAD, d), lambda b, i: (0, 0))],
        out_specs=[pl.BlockSpec((nb, rb, IN_PROJ_PAD), lambda b, i: (b, i, 0)),
                   pl.BlockSpec((nb * n_slices, 1, RWKV_PROJ), lambda b, i: (b * n_seq_tiles + i, 0, 0))],
        out_shape=[jax.ShapeDtypeStruct((bsz, seq, IN_PROJ_PAD), F32),
                   jax.ShapeDtypeStruct((bsz * n_seq_tiles * n_slices, 1, RWKV_PROJ), F32)],
        compiler_params=_params("parallel", "parallel"),
        name="norm1_in_proj",
    )(x, sc, sh, g, w)


def _rwkv_prep_kernel(p_ref, prev_ref, mu_ref, w0_ref, a0_ref, kk_ref, ka_ref, rk_ref,
                      wd_ref, wa_ref, wg_ref, seg_ref, tri_ref,
                      ab_ref, rkv_ref, bg_ref, gl_ref):
    nb, rb, wp = p_ref.shape
    tr = nb * rb
    p = p_ref[...].reshape(tr, wp)
    row = lax.broadcasted_iota(jnp.int32, (nb, rb, wp), 1)
    xx = jnp.where(row == 0, prev_ref[...], pltpu.roll(p, 1, 0).reshape(nb, rb, wp)).reshape(tr, wp)
    ps = p + (xx - p) * mu_ref[...]
    w = RWKV_WIDTH
    r, k, v, lora = ps[:, :w], ps[:, w:2 * w], ps[:, 2 * w:3 * w], ps[:, 3 * w:]
    logw = -_softplus(-(w0_ref[...] + _mm1(jnp.tanh(lora), wd_ref[...]))) - 0.5
    lw = jnp.exp(logw) * (-LOG2_E)
    a = _sigmoid(a0_ref[...] + _mm1(lora, wa_ref[...]))
    g = _mm1(_sigmoid(lora), wg_ref[...])
    seg = seg_ref[...]
    kk = k * kk_ref[...]
    kk = kk * lax.rsqrt(jnp.maximum(_mm2_exact_rhs(kk * kk, seg), 1e-24))
    k2 = k * (1.0 + (a - 1.0) * ka_ref[...])
    cum = _mm_exact_lhs(tri_ref[...], lw)
    lasts = [jnp.exp2(cum[c * CHUNK + CHUNK - 1:(c + 1) * CHUNK, :]) for c in range(tr // CHUNK)]
    for c, l in enumerate(lasts):
        gl_ref[c] = l

    def to_chunk_end(val):
        return jnp.concatenate([val[c * CHUNK:(c + 1) * CHUNK] * l for c, l in enumerate(lasts)], axis=0)

    gamma = jnp.exp2(cum)
    ginv = 1.0 / gamma
    bt = (kk * a) * ginv
    kt = k2 * ginv

    def put(ref, slot, val):
        ref[:, :, slot * w:(slot + 1) * w] = val.reshape(nb, rb, w).astype(ref.dtype)

    put(ab_ref, 0, -kk * jnp.exp2(cum - lw))
    put(ab_ref, 1, bt)
    put(rkv_ref, 0, r * gamma)
    put(rkv_ref, 1, kt)
    put(rkv_ref, 2, to_chunk_end(kt))
    put(rkv_ref, 3, v)
    put(bg_ref, 0, _mm2_exact_rhs(r * k2 * rk_ref[...], seg) * v)
    put(bg_ref, 1, g)


def _rwkv_prep(proj, prev, lp, consts, rows):
    bsz, seq, _ = proj.shape
    w = RWKV_WIDTH
    nb, rb = _token_tile(bsz, seq, rows)
    n_seq_tiles = seq // rb
    flat = lambda b, i: (b * n_seq_tiles + i, 0)
    row = lambda n: pl.BlockSpec((1, n), lambda b, i: (0, 0))
    full = lambda a: pl.BlockSpec(a.shape, lambda b, i: (0,) * a.ndim)
    packed = [(2, F32), (4, BF16), (2, F32)]
    return pl.pallas_call(
        _rwkv_prep_kernel,
        grid=(bsz // nb, n_seq_tiles),
        in_specs=[pl.BlockSpec((nb, rb, RWKV_PROJ), lambda b, i: (b, i, 0)),
                  pl.BlockSpec((nb, 1, RWKV_PROJ), lambda b, i: flat(b, i) + (0,)),
                  row(RWKV_PROJ), row(w), row(w), row(w), row(w), row(w),
                  full(lp["wd"]), full(lp["wa"]), full(lp["wg"]), full(consts["seg"]), full(consts["tri"])],
        out_specs=[pl.BlockSpec((nb, rb, k * w), lambda b, i: (b, i, 0)) for k, _ in packed]
        + [pl.BlockSpec((nb * rb // CHUNK, 1, w), lambda b, i: flat(b, i) + (0,))],
        out_shape=[jax.ShapeDtypeStruct((bsz, seq, k * w), dt) for k, dt in packed]
        + [jax.ShapeDtypeStruct((bsz * seq // CHUNK, 1, w), F32)],
        compiler_params=_params("parallel", "parallel"),
        name="rwkv_prep",
    )(proj, prev, lp["mu"], lp["w0"], lp["a0"], lp["k_k"], lp["k_a"], lp["r_k"],
      lp["wd"], lp["wa"], lp["wg"], consts["seg"], consts["tri"])


def _head_masks(width):
    lane = lax.broadcasted_iota(jnp.int32, (CHUNK, width), 1)
    row = lax.broadcasted_iota(jnp.int32, (CHUNK, width), 0)
    return lane >> HEAD_SHIFT, row, lane & (HEAD_LANES - 1)


def _block_diag(x, head):
    z = jnp.zeros_like(x)
    return jnp.concatenate([jnp.where(head == j, x, z) for j in range(x.shape[1] // HEAD_LANES)], axis=0)


def _rwkv_stages(ab_ref, rkv_ref, gl_ref, s0_ref, cast_in, y_ref, s_ref, cast_out, bb, nck):
    @pl.when(pl.program_id(1) == 0)
    def _():
        s_ref[...] = s0_ref[...]

    for src, dst in zip(cast_in, cast_out):
        dst[...] = src[...].astype(BF16)

    tw = RWKV_TILE_LANES
    head, row, col = _head_masks(tw)
    strict = col < row
    incl = col <= row
    same8 = (col >> 3) == (row >> 3)
    lane2 = lax.broadcasted_iota(jnp.int32, (tw, tw), 1)
    row2 = lax.broadcasted_iota(jnp.int32, (tw, tw), 0)
    same_head = (lane2 >> HEAD_SHIFT) == (row2 >> HEAD_SHIFT)
    bd = functools.partial(_block_diag, head=head)
    c = CHUNK

    def pmm(p, q):
        return _dg(p.astype(BF16), bd(q.astype(BF16)))

    items = [(b, ck, pr) for b in range(bb) for ck in range(nck) for pr in range(RWKV_TILES)]
    n = range(len(items))
    rows = lambda ck: slice(ck * c, (ck + 1) * c)
    lanes = lambda pr: slice(pr * tw, (pr + 1) * tw)
    w = RWKV_WIDTH

    def ld(ref, slot):
        return [ref[b, rows(ck), slot * w + pr * tw:slot * w + (pr + 1) * tw] for b, ck, pr in items]

    cat0 = lambda *xs: jnp.concatenate(xs, axis=0)
    cat1 = lambda *xs: jnp.concatenate(xs, axis=1)
    at, bt = ld(ab_ref, 0), ld(ab_ref, 1)
    rt, kt, ke, v = (ld(rkv_ref, slot) for slot in range(4))
    be = [(bt[i] * gl_ref[b, ck, :, lanes(pr)]).astype(BF16) for i, (b, ck, pr) in enumerate(items)]
    ats, bts = [_split2(a) for a in at], [_split2(b) for b in bt]
    atb = [hi for hi, _ in ats]
    zero = jnp.zeros((c, tw), F32)
    gk = [_dg(cat0(atb[i], rt[i]), bd(kt[i]), _NT) for i in n]
    aak = [jnp.where(strict, gk[i][:c], zero) for i in n]
    ark = [jnp.where(incl, gk[i][c:], zero) for i in n]
    yield
    gb = [_dg(cat0(ats[i][0], ats[i][1], rt[i]), bd(bts[i][0]), _NT) for i in n]
    arb = [jnp.where(incl, gb[i][2 * c:], zero) for i in n]
    aab = [jnp.where(strict, gb[i][:c] + (gb[i][c:2 * c] + _dg(atb[i], bd(bts[i][1]), _NT)), zero) for i in n]
    yield
    z = [pmm(aak[i], v[i]) for i in n]
    a8 = [jnp.where(same8, aab[i], zero) for i in n]
    p2 = [pmm(a8[i], a8[i]) for i in n]
    yield
    p4 = [pmm(p2[i], p2[i]) for i in n]
    yield
    nn = [a8[i] + p2[i] + pmm(p2[i], a8[i]) for i in n]
    yield
    nn = [nn[i] + p4[i] + pmm(p4[i], nn[i]) for i in n]
    yield
    for lvl in (3, 4, 5):
        joins = ((col >> (lvl + 1)) == (row >> (lvl + 1))) & ((col >> lvl) != (row >> lvl))
        e = [jnp.where(joins, aab[i], zero) for i in n]
        te = [e[i] + pmm(nn[i], e[i]) for i in n]
        yield
        nn = [nn[i] + te[i] + pmm(te[i], nn[i]) for i in n]
        yield
    wu = [cat1(at[i], z[i]) + _dg(nn[i].astype(BF16), cat1(bd(atb[i]), bd(z[i].astype(BF16)))) for i in n]
    abk = [cat1(arb[i], ark[i]).astype(BF16) for i in n]
    yield
    s = {(b, pr): s_ref[b, pr] for b in range(bb) for pr in range(RWKV_TILES)}
    for ck in range(nck):
        cur = [i for i in n if items[i][1] == ck]
        key = lambda i: (items[i][0], items[i][2])
        x = {i: _dg(cat0(wu[i][:, :tw].astype(BF16), rt[i]), s[key(i)].astype(BF16), _NT) for i in cur}
        ub = {i: (x[i][:c] + wu[i][:, tw:]).astype(BF16) for i in cur}
        yield
        upd = {i: _dg(cat0(ub[i], v[i]), cat0(be[i], ke[i]), _TN) for i in cur}
        for i in cur:
            b, _, pr = items[i]
            y_ref[b, rows(ck), lanes(pr)] = x[i][c:] + _dg(abk[i], cat0(bd(ub[i]), bd(v[i])))
            s[b, pr] = (s[b, pr] * gl_ref[b, ck, :, lanes(pr)]
                        + jnp.where(same_head, upd[i], jnp.zeros_like(upd[i])))
        yield
    for (b, pr), val in s.items():
        s_ref[b, pr] = val


def _gla_stages(p_ref, wgate_ref, bgate_ref, tri_ref, ng_ref, s0_ref, y_ref, s_ref, bb, nck):
    @pl.when(pl.program_id(1) == 0)
    def _():
        s_ref[...] = s0_ref[...]

    head, row, col = _head_masks(LANES)
    incl = col <= row
    bd = functools.partial(_block_diag, head=head)
    lane2 = lax.broadcasted_iota(jnp.int32, (2 * GLA_DV, LANES), 1)
    row2 = lax.broadcasted_iota(jnp.int32, (2 * GLA_DV, LANES), 0)
    same_head = (lane2 < GLA_DK) == (row2 < GLA_DV)
    kw, gw = GLA_KEY_WIDTH, GLA_WIDTH
    zero = jnp.zeros((CHUNK, LANES), F32)
    cat0 = lambda xs: jnp.concatenate(xs, axis=0)
    cat1 = lambda xs: jnp.concatenate(xs, axis=1)
    rows = lambda c: slice(c * CHUNK, (c + 1) * CHUNK)
    lanes = lambda pr: slice(pr * LANES, (pr + 1) * LANES)
    bs, cs = range(bb), range(nck)
    x = [p_ref[b] for b in bs]
    gate = [_mm1(x[b][:, 2 * kw + 2 * gw:], wgate_ref[...]) + bgate_ref[...] for b in bs]
    la = [(jnp.minimum(g, 0.0) - jnp.log(1.0 + jnp.exp(-jnp.abs(g)))) / GLA_TAU for g in gate]
    yield
    cum = [_mm_exact_lhs(tri_ref[...], la[b]) for b in bs]
    yield
    q_dec = [(x[b][:, :kw] * (GLA_DK ** -0.5)) * jnp.exp(cum[b]) for b in bs]
    k_inv = [x[b][:, kw:2 * kw] * jnp.exp(-cum[b]) for b in bs]
    last = [[cum[b][c * CHUNK + CHUNK - 1:(c + 1) * CHUNK, :] for c in cs] for b in bs]
    k_end = [[x[b][rows(c), kw:2 * kw] * jnp.exp(last[b][c] - cum[b][rows(c)]) for c in cs] for b in bs]
    dec = [[jnp.exp(last[b][c]) for c in cs] for b in bs]
    yield
    items = [(b, c, pr) for b in bs for c in cs for pr in range(GLA_PAIRS)]
    vp = {(b, c, pr): x[b][rows(c), 2 * kw + pr * 2 * GLA_DV:2 * kw + (pr + 1) * 2 * GLA_DV]
          for b, c, pr in items}
    att = {(b, c, pr): jnp.where(incl, _mm1(q_dec[b][rows(c), lanes(pr)],
                                            bd(k_inv[b][rows(c), lanes(pr)]), _NT), zero)
           for b, c, pr in items}
    yield
    upd = {it: _mm1(vp[it], k_end[it[0]][it[1]][:, lanes(it[2])], _TN) for it in items}
    yield
    intra = {it: _mm1(att[it], cat0([cat1([vp[it][:, :GLA_DV], zero]), cat1([zero, vp[it][:, GLA_DV:]])]))
             for it in items}
    yield
    st = {}
    for b in bs:
        for pr in range(GLA_PAIRS):
            cur = s_ref[b, pr]
            for c in cs:
                st[b, c, pr] = cur
                cur = cur * dec[b][c][:, lanes(pr)] + jnp.where(same_head, upd[b, c, pr], jnp.zeros_like(cur))
            s_ref[b, pr] = cur
    yield
    o = {(b, c, pr): intra[b, c, pr] + _mm1(q_dec[b][rows(c), lanes(pr)], st[b, c, pr], _NT)
         for b, c, pr in items}
    yield
    for b in bs:
        ob = cat0([cat1([o[b, c, pr] for pr in range(GLA_PAIRS)]) for c in cs])
        heads = [ob[:, h * GLA_DV:(h + 1) * GLA_DV] for h in range(GLA_HEADS)]
        normed = [oh * lax.rsqrt(jnp.mean(oh * oh, axis=-1, keepdims=True) + RMS_EPS) for oh in heads]
        y_ref[b] = cat1(normed) * ng_ref[...] * _silu(x[b][:, 2 * kw + gw:2 * kw + 2 * gw])


def _recurrent_kernel(*refs, bb, nck, n_cast):
    rwkv_in, refs = refs[:4], refs[4:]
    cast_in, refs = refs[:n_cast], refs[n_cast:]
    gla_in, refs = refs[:6], refs[6:]
    (y_ref, s_ref), refs = refs[:2], refs[2:]
    cast_out, (yg_ref, t_ref) = refs[:n_cast], refs[n_cast:]
    rwkv = _rwkv_stages(*rwkv_in, cast_in, y_ref, s_ref, cast_out, bb, nck)
    gla = _gla_stages(*gla_in, yg_ref, t_ref, bb, nck)
    done = object()
    live = [rwkv, rwkv, gla]
    while live:
        for gen in list(live):
            if gen in live and next(gen, done) is done:
                live = [g for g in live if g is not gen]


def _recurrent(ab, rkv, gl, s0, proj, lp, tri, t0, bb, nck, casts=()):
    bsz, seq, _ = ab.shape
    w = RWKV_WIDTH
    rows = nck * CHUNK
    n_seq_steps = seq // rows
    n_steps = (bsz // bb) * n_seq_steps
    tok = lambda a: pl.BlockSpec((bb, rows, a.shape[-1]), lambda i, c: (i, c, 0))
    full = lambda a: pl.BlockSpec(a.shape, lambda i, c: (0,) * a.ndim)
    st = pl.BlockSpec((bb, RWKV_TILES, RWKV_TILE_LANES, RWKV_TILE_LANES), lambda i, c: (i, 0, 0, 0))
    gst = pl.BlockSpec((bb, GLA_PAIRS, 2 * GLA_DV, LANES), lambda i, c: (i, 0, 0, 0))
    slab = lambda a: pl.BlockSpec((a.shape[0] // n_steps, a.shape[1]), lambda i, c: (i * n_seq_steps + c, 0))
    assert all(a.shape[0] % (16 * n_steps) == 0 for a in casts)
    out_tok = pl.BlockSpec((bb, rows, w), lambda i, c: (i, c, 0))
    return pl.pallas_call(
        functools.partial(_recurrent_kernel, bb=bb, nck=nck, n_cast=len(casts)),
        grid=(bsz // bb, n_seq_steps),
        in_specs=[tok(ab), tok(rkv), pl.BlockSpec((bb, nck, 1, w), lambda i, c: (i, c, 0, 0)), st]
        + [slab(a) for a in casts]
        + [pl.BlockSpec((bb, rows, GLA_PROJ_PAD), lambda i, c: (i, c, 1)),
           full(lp["wgate"]), full(lp["bgate"]), full(tri), full(lp["gla_g"]), gst],
        out_specs=[out_tok, st] + [slab(a) for a in casts] + [out_tok, gst],
        out_shape=[jax.ShapeDtypeStruct((bsz, seq, w), F32),
                   jax.ShapeDtypeStruct((bsz, RWKV_TILES, RWKV_TILE_LANES, RWKV_TILE_LANES), F32)]
        + [jax.ShapeDtypeStruct(a.shape, BF16) for a in casts]
        + [jax.ShapeDtypeStruct((bsz, seq, GLA_WIDTH), F32),
           jax.ShapeDtypeStruct((bsz, GLA_PAIRS, 2 * GLA_DV, LANES), F32)],
        compiler_params=_params("parallel", "arbitrary"),
        name="rwkv_gla_chunk",
    )(ab, rkv, gl, s0, *casts, proj, lp["wgate"], lp["bgate"], tri, lp["gla_g"], t0)


def _route(lgt):
    tm = lgt.shape[1]
    n_pad = ROUTE_ROWS - N_EXPERTS
    rowg = lax.broadcasted_iota(jnp.int32, (n_pad, tm), 0)
    rowe = lax.broadcasted_iota(jnp.int32, (N_EXPERTS, tm), 0)
    rowg_f, rowe_f = rowg.astype(F32), rowe.astype(F32)

    def first_argmax(vals, mx, rows_f):
        return jnp.min(jnp.where(vals == mx, rows_f, float(ROUTE_ROWS)), axis=0, keepdims=True)

    is_group = rowg < N_GROUPS
    lg = jnp.where(is_group, lgt[N_EXPERTS:], -jnp.inf)
    gmax = jnp.max(lg, axis=0, keepdims=True)
    gi = first_argmax(lg, gmax, rowg_f)
    pg_top = 1.0 / jnp.sum(jnp.where(is_group, jnp.exp(lg - gmax), 0.0), axis=0, keepdims=True)
    in_group = (rowe >> GROUP_SHIFT).astype(F32) == gi
    le = jnp.where(in_group, lgt[:N_EXPERTS], -jnp.inf)
    m1 = jnp.max(le, axis=0, keepdims=True)
    i1 = first_argmax(le, m1, rowe_f)
    le2 = jnp.where(rowe_f == i1, -jnp.inf, le)
    m2 = jnp.max(le2, axis=0, keepdims=True)
    i2 = first_argmax(le2, m2, rowe_f)
    p2 = jnp.exp(m2 - m1)
    w1 = pg_top / (1.0 + p2)
    w2 = pg_top * p2 / (1.0 + p2)
    comb = jnp.where(rowe_f == i1, w1, 0.0) + jnp.where(rowe_f == i2, w2, 0.0)
    return jnp.concatenate([comb, jnp.where(rowg == 0, gi, 0.0)], axis=0)


def _outproj_kernel(yr_ref, bg_ref, yg_ref, x_ref, gt_ref, sc_ref, sh_ref,
                    lnw_ref, lnb_ref, seg_ref, wout_ref, g2_ref, wr_ref, br_ref,
                    x1_ref, h2_ref, route_ref, *, n_parts):
    nb, rb, d_model = x_ref.shape
    if nb % n_parts == 0:
        pnb, prb = nb // n_parts, rb
        tok = lambda p: (slice(p * pnb, (p + 1) * pnb), slice(None), slice(None))
        seq_of = tok
    else:
        pnb, prb = nb, rb // n_parts
        tok = lambda p: (slice(None), slice(p * prb, (p + 1) * prb), slice(None))
        seq_of = lambda p: (slice(None), slice(None), slice(None))
    tm = pnb * prb
    parts = range(n_parts)
    flat = lambda ref, p: ref[tok(p)].reshape(tm, ref.shape[-1])
    seg = seg_ref[...]
    inv_n = 1.0 / RWKV_HEAD
    ds = [y - _mm2_exact_rhs(y, seg) * inv_n for y in (flat(yr_ref, p) for p in parts)]
    vs = [_mm2_exact_rhs(d * d, seg) * inv_n for d in ds]
    mixes = []
    for p in parts:
        yn = ds[p] * lax.rsqrt(vs[p] + LNX_EPS) * lnw_ref[...] + lnb_ref[...]
        bg = flat(bg_ref, p)
        yr = (yn + bg[:, :RWKV_WIDTH]) * bg[:, RWKV_WIDTH:]
        mixes.append(jnp.concatenate([yr, flat(yg_ref, p)], axis=1))
    outs = [_mm1(mix, wout_ref[...]) for mix in mixes]
    h2s = []
    for p in parts:
        x1 = x_ref[tok(p)] + gt_ref[seq_of(p)] * outs[p].reshape(pnb, prb, d_model)
        x1_ref[tok(p)] = x1
        h2 = _rms_mod(x1, g2_ref[...], sc_ref[seq_of(p)], sh_ref[seq_of(p)])
        h2_ref[tok(p)] = h2.astype(BF16)
        h2s.append(h2.reshape(tm, d_model))
    logits = [_mm3(wr_ref[...], h2, _NT) + br_ref[...] for h2 in h2s]
    for p in parts:
        route_ref[:, p * tm:(p + 1) * tm] = _route(logits[p])


def _out_proj(yr, bg, yg, x, gt, sc, sh, lp, consts, rows):
    bsz, seq, d = x.shape
    nb, rb = _token_tile(bsz, seq, rows)
    n_seq_tiles = seq // rb
    half = pl.BlockSpec((nb, rb, RWKV_WIDTH), lambda b, i: (b, i, 0))
    tokd = pl.BlockSpec((nb, rb, d), lambda b, i: (b, i, 0))
    vec = pl.BlockSpec((nb, 1, d), lambda b, i: (b, 0, 0))
    full = lambda a: pl.BlockSpec(a.shape, lambda b, i: (0,) * a.ndim)
    args = (lp["lnx_w"], lp["lnx_b"], consts["seg"], lp["w_out"], lp["norm2_g"], lp["w_router"], lp["b_router"])
    return pl.pallas_call(
        functools.partial(_outproj_kernel, n_parts=OUT_PROJ_PARTS),
        grid=(bsz // nb, seq // rb),
        in_specs=[half, pl.BlockSpec((nb, rb, 2 * RWKV_WIDTH), lambda b, i: (b, i, 0)), half, tokd,
                  vec, vec, vec] + [full(a) for a in args],
        out_specs=[tokd, tokd, pl.BlockSpec((ROUTE_ROWS, nb * rb), lambda b, i: (0, b * n_seq_tiles + i))],
        out_shape=[jax.ShapeDtypeStruct((bsz, seq, d), F32),
                   jax.ShapeDtypeStruct((bsz, seq, d), BF16),
                   jax.ShapeDtypeStruct((ROUTE_ROWS, bsz * seq), F32)],
        compiler_params=_params("parallel", "parallel"),
        name="out_proj_router",
    )(yr, bg, yg, x, gt, sc, sh, *args)


def _moe_kernel(h_ref, route_ref, x1_ref, gt_ref, sc_ref, sh_ref, gf_ref, tri_ref, wg_ref, wu_ref, wd_ref,
                y_ref, ys_ref, pos_ref, meta_ref):
    g = pl.program_id(2)
    nb, rb, d = h_ref.shape
    tm = nb * rb
    sub, tail = MOE_SUB_ROWS, MOE_SUB_ROWS // 2
    cap = ys_ref.shape[0]
    n_lane_tiles = tm // LANES
    lane_tile = lambda k: slice(k * LANES, (k + 1) * LANES)

    @pl.when(g == 0)
    def _sort():
        row8 = lax.broadcasted_iota(jnp.int32, (8, LANES), 0).astype(F32)
        carry = jnp.zeros((8, 1), F32)
        members, ranks = [], []
        for k in range(n_lane_tiles):
            blk = jnp.where(row8 == route_ref[N_EXPERTS:N_EXPERTS + 1, lane_tile(k)], 1.0, 0.0)
            members.append(blk)
            ranks.append(_dg(blk.astype(BF16), tri_ref[...]) - blk + carry)
            carry = carry + jnp.sum(blk, axis=1, keepdims=True)
        rowc = lax.broadcasted_iota(jnp.int32, (8, 1), 0)
        lane = lax.broadcasted_iota(jnp.int32, (1, LANES), 1)
        first = jnp.zeros((1, 1), F32)
        off_col = jnp.zeros((8, 1), F32)
        meta = jnp.zeros((1, LANES), F32)
        for grp in range(N_GROUPS):
            count = jnp.sum(jnp.where(rowc == grp, carry, 0.0), axis=0, keepdims=True)
            blocks = jnp.ceil(count * (1.0 / tail))
            pairs = jnp.floor(blocks * 0.5)
            odd = blocks - 2.0 * pairs
            triple = odd * jnp.where(blocks >= 3.0, 1.0, 0.0)
            n_full = pairs - triple
            off_col = off_col + jnp.where(rowc == grp, first, 0.0)
            meta = (meta + jnp.where(lane == grp, first, 0.0) + jnp.where(lane == N_GROUPS + grp, n_full, 0.0)
                    + jnp.where(lane == 2 * N_GROUPS + grp, odd + 2.0 * triple, 0.0))
            first = first + blocks * tail
        meta = (meta + jnp.where(lane == 3 * N_GROUPS, first, 0.0)).astype(jnp.int32)
        for i in range(3 * N_GROUPS + 1):
            meta_ref[i] = meta[0, i]
        for k in range(n_lane_tiles):
            pos = jnp.sum(members[k] * (ranks[k] + off_col), axis=0, keepdims=True)
            pos_ref[:, lane_tile(k)] = pos.astype(jnp.int32)
        ys_ref[...] = jnp.zeros_like(ys_ref)

    def experts(base, n_rows):
        prow = lax.broadcasted_iota(jnp.int32, (n_rows, tm), 0) + base
        onehot = jnp.where(prow == pos_ref[...], 1.0, 0.0).astype(BF16)
        hs = _dg(onehot, h_ref[...].reshape(tm, d)).astype(BF16)
        comb3 = _split3(route_ref[:N_EXPERTS, :])
        cs = _dg(onehot, comb3[0], _NT) + (_dg(onehot, comb3[1], _NT) + _dg(onehot, comb3[2], _NT))
        lane = lax.broadcasted_iota(jnp.int32, cs.shape, 1)
        ys = jnp.zeros((n_rows, d), F32)
        for e in range(EXPERTS_PER_GROUP):
            hid = _silu(_dg(hs, wg_ref[e])) * _dg(hs, wu_ref[e])
            ce = jnp.sum(jnp.where(lane == g * EXPERTS_PER_GROUP + e, cs, 0.0), axis=-1, keepdims=True)
            ys = ys + ce * _dg(hid.astype(BF16), wd_ref[e])
        ys_ref[pl.ds(base, n_rows), :] = ys.astype(BF16)

    first_row = meta_ref[g]
    n_full = meta_ref[N_GROUPS + g]

    def full_sub_tile(j, carry_):
        experts(pl.multiple_of(first_row + j * sub, tail), sub)
        return carry_

    lax.fori_loop(0, n_full, full_sub_tile, 0)

    for last_blocks in (1, 3):
        @pl.when(meta_ref[2 * N_GROUPS + g] == last_blocks)
        def _last(last_blocks=last_blocks):
            experts(pl.multiple_of(first_row + n_full * sub, tail), last_blocks * tail)

    def unsort(n_sorted):
        ys_all = ys_ref[:n_sorted]
        prow = lax.broadcasted_iota(jnp.int32, (n_sorted, LANES), 0)
        seqs = max(LANES // rb, 1)
        for k in range(n_lane_tiles):
            onehot = jnp.where(prow == pos_ref[:, lane_tile(k)], 1.0, 0.0).astype(BF16)
            moe = _dg(onehot, ys_all, _TN).reshape(seqs, LANES // seqs, d)
            b0 = k * LANES // rb
            r0 = k * LANES - b0 * rb
            bs, rs = slice(b0, b0 + seqs), slice(r0, r0 + LANES // seqs)
            x2 = x1_ref[bs, rs] + gt_ref[bs] * moe
            y_ref[bs, rs] = _rms_mod(x2, gf_ref[...], sc_ref[bs], sh_ref[bs])

    usual = tm + (cap - tm) // 2
    last = g == N_GROUPS - 1
    pl.when(last & (meta_ref[3 * N_GROUPS] <= usual))(lambda: unsort(usual))
    pl.when(last & (meta_ref[3 * N_GROUPS] > usual))(lambda: unsort(cap))


def _moe(h2, route, x1, gt, sc, sh, gf, wg, wu, wd, tri, rows):
    bsz, seq, d = x1.shape
    nb, rb = _token_tile(bsz, seq, rows)
    tm = nb * rb
    assert tm % LANES == 0 and (rb % LANES == 0 or LANES % rb == 0)
    n_seq_tiles = seq // rb
    cap = tm + N_GROUPS * (MOE_SUB_ROWS // 2)
    tokd = pl.BlockSpec((nb, rb, d), lambda b, i, g: (b, i, 0))
    vec = pl.BlockSpec((nb, 1, d), lambda b, i, g: (b, 0, 0))
    group_w = lambda shape: pl.BlockSpec((EXPERTS_PER_GROUP,) + shape, lambda b, i, g: (g, 0, 0))
    return pl.pallas_call(
        _moe_kernel,
        grid=(bsz // nb, n_seq_tiles, N_GROUPS),
        in_specs=[tokd, pl.BlockSpec((ROUTE_ROWS, tm), lambda b, i, g: (0, b * n_seq_tiles + i)),
                  tokd, vec, vec, vec,
                  pl.BlockSpec((1, d), lambda b, i, g: (0, 0)),
                  pl.BlockSpec(tri.shape, lambda b, i, g: (0, 0)),
                  group_w((d, D_EXPERT)), group_w((d, D_EXPERT)), group_w((D_EXPERT, d))],
        out_specs=tokd,
        out_shape=jax.ShapeDtypeStruct((bsz, seq, d), F32),
        scratch_shapes=[pltpu.VMEM((cap, d), BF16), pltpu.VMEM((1, tm), jnp.int32),
                        pltpu.SMEM((3 * N_GROUPS + 1,), jnp.int32)],
        compiler_params=_params("parallel", "parallel", "arbitrary"),
        name="moe_final_norm",
    )(h2, route, x1, gt, sc, sh, gf, tri, wg, wu, wd)


def _block_ones(n, blk, lower):
    i = np.arange(n)
    m = (i[:, None] // blk) == (i[None, :] // blk)
    if lower:
        m = m & (i[None, :] <= i[:, None])
    return jnp.asarray(m, dtype=BF16)


def _consts(tr, tg):
    return dict(seg=_block_ones(RWKV_WIDTH, RWKV_HEAD, False), tri=_block_ones(tr, CHUNK, True),
                tri_gla=_block_ones(tg, CHUNK, True), tri_up=_block_ones(LANES, LANES, True).T)


def _heads_block_diag(s):
    n, c = s.shape[2], s.shape[-1]
    rows = [jnp.pad(s[:, :, j], ((0, 0), (0, 0), (0, 0), (j * c, (n - 1 - j) * c))) for j in range(n)]
    return jnp.concatenate(rows, axis=2)


def _diag_blocks(t, n):
    r, c = t.shape[-2] // n, t.shape[-1] // n
    return jnp.stack([t[:, :, j * r:(j + 1) * r, j * c:(j + 1) * c] for j in range(n)], axis=2)


def _pad_rows(w, first_row):
    out = jnp.zeros((LORA_PAD, w.shape[1]), F32)
    return lax.dynamic_update_slice(out, w, (first_row, 0)).astype(BF16)


def _layer_params(l, w_in, mu_shift, w0, w_decay_up, a0, w_a_up, w_g_up, k_k, k_a, r_k, lnx_w, lnx_b,
                  w_gla_gate_up, b_gla_gate, gla_norm_g, w_out, norm2_g,
                  w_router_group, b_router_group, w_router_expert, b_router_expert):
    w_in_p = _win_layout(w_in[l].T, WEIGHT_TILE_COLS)
    n_pad = ROUTE_ROWS - N_EXPERTS - N_GROUPS
    w_router = jnp.concatenate([w_router_expert[l].T, w_router_group[l].T, jnp.zeros((n_pad, D_MODEL), F32)])
    b_router = jnp.concatenate([b_router_expert[l], b_router_group[l],
                                jnp.zeros((n_pad,), F32)]).reshape(ROUTE_ROWS, 1)
    wgate = jnp.zeros((LANES, GLA_KEY_WIDTH), F32).at[:GLA_GATE_RANK].set(w_gla_gate_up[l]).astype(BF16)
    r1 = lambda a: a.reshape(1, -1)
    return dict(
        w_in=w_in_p, mu=r1(mu_shift[l]), w0=r1(w0[l]), a0=r1(a0[l]), k_k=r1(k_k[l]), k_a=r1(k_a[l]),
        r_k=r1(r_k[l]), wd=_pad_rows(w_decay_up[l], 0), wa=_pad_rows(w_a_up[l], DECAY_LORA),
        wg=_pad_rows(w_g_up[l], DECAY_LORA + AAA_LORA), lnx_w=r1(lnx_w[l]), lnx_b=r1(lnx_b[l]),
        wgate=wgate, bgate=r1(b_gla_gate[l]), gla_g=r1(gla_norm_g[l]),
        w_out=w_out[l].astype(BF16), norm2_g=r1(norm2_g[l]), w_router=w_router, b_router=b_router)


def _run_layer(x, mod, shift0, wkv0, gla0, lp, experts, final):
    bsz, seq, d = x.shape
    assert seq % CHUNK == 0
    bb = 2 if seq >= RECURRENT_STEP_ROWS else min(bsz, RECURRENT_MAX_SEQS)
    assert bsz % bb == 0
    nck = min(seq, RECURRENT_STEP_ROWS) // CHUNK
    consts = _consts(TOKEN_TILE_ROWS, nck * CHUNK)
    m = lambda j: mod[:, j:j + 1, :]
    sh1, sc1, gt1, sh2, sc2, gt2 = (m(j) for j in range(6))
    proj, tails = _in_proj(x, sc1, sh1, lp["norm1_g"], lp["w_in"], PROJ_TILE_ROWS, TOKEN_TILE_ROWS)
    tails = tails.reshape(bsz, -1, RWKV_PROJ)
    new_shift = tails[:, -1]
    prev = jnp.concatenate([shift0[:, None, :], tails[:, :-1]], axis=1).reshape(-1, 1, RWKV_PROJ)
    ab, rkv, bg, gl = _rwkv_prep(proj, prev, lp, consts, TOKEN_TILE_ROWS)
    gl = gl.reshape(bsz, seq // CHUNK, 1, RWKV_WIDTH)
    s0 = _heads_block_diag(wkv0.reshape(bsz, RWKV_TILES, RWKV_TILE_HEADS, RWKV_HEAD, RWKV_HEAD))
    t0 = _heads_block_diag(jnp.swapaxes(gla0, -1, -2).reshape(bsz, GLA_PAIRS, 2, GLA_DV, GLA_DK))
    casts = tuple(e.reshape(-1, e.shape[-1]) for e in experts if e.dtype != BF16)
    yr, s_bd, *cast, yg, t_bd = _recurrent(ab, rkv, gl, s0, proj, lp, consts["tri_gla"], t0,
                                           bb, nck, casts)
    if cast:
        experts = tuple(c.reshape(e.shape) for c, e in zip(cast, experts))
    new_wkv = _diag_blocks(s_bd, RWKV_TILE_HEADS).reshape(bsz, RWKV_HEADS, RWKV_HEAD, RWKV_HEAD)
    new_gla = jnp.swapaxes(_diag_blocks(t_bd, 2).reshape(bsz, GLA_HEADS, GLA_DV, GLA_DK), -1, -2)
    x1, h2, route = _out_proj(yr, bg, yg, x, gt1, sc2, sh2, lp, consts, PROJ_TILE_ROWS)
    out = _moe(h2, route, x1, gt2, *final, *experts, consts["tri_up"], MOE_TILE_ROWS)
    return experts, out, new_shift, new_wkv, new_gla


def kernel(x_prompt, x_sample, c_prompt, c_sample, state_rwkv_shift, state_rwkv_wkv, state_gla_kv, w_ada, b_ada, norm1_g, norm2_g, w_in, mu_shift, w0, w_decay_up, a0, w_a_up, w_g_up, k_k, k_a, r_k, lnx_w, lnx_b, w_gla_gate_up, b_gla_gate, gla_norm_g, w_out, w_router_group, b_router_group, w_router_expert, b_router_expert, w_expert_gate, w_expert_up, w_expert_down, w_ada_final, b_ada_final, normf_g):
    assert w_ada.shape[0] == 1, "the final norm is fused into the single layer's MoE kernel"
    bp, bs = x_prompt.shape[0], x_sample.shape[0]
    d = D_MODEL
    n_rows = -(-(bp + bs) // 8) * 8
    c_all = jnp.zeros((n_rows, d), F32).at[:bp].set(c_prompt).at[bp:bp + bs].set(c_sample)
    modf = _modulation(c_all, w_ada_final, b_ada_final, MODULATION_TILE_ROWS).reshape(n_rows, 2, d)
    mod = _modulation(c_all, w_ada, b_ada[0], MODULATION_TILE_ROWS, layer=0).reshape(n_rows, 6, d)
    lp = _layer_params(0, w_in, mu_shift, w0, w_decay_up, a0, w_a_up, w_g_up, k_k, k_a, r_k, lnx_w,
                       lnx_b, w_gla_gate_up, b_gla_gate, gla_norm_g, w_out, norm2_g,
                       w_router_group, b_router_group, w_router_expert, b_router_expert)
    lp["norm1_g"] = norm1_g[0].reshape(1, d)
    experts = (w_expert_gate[0], w_expert_up[0], w_expert_down[0])
    groups = [
        (x_prompt, 0, bp, jnp.zeros((bp, RWKV_PROJ), F32),
         jnp.zeros((bp, RWKV_HEADS, RWKV_HEAD, RWKV_HEAD), F32), jnp.zeros((bp, GLA_HEADS, GLA_DK, GLA_DV), F32)),
        (x_sample, bp, bp + bs, state_rwkv_shift[0], state_rwkv_wkv[0], state_gla_kv[0]),
    ]
    ys, states = [], []
    for x, lo, hi, shift0, wkv0, gla0 in groups:
        final = (modf[lo:hi, 1:2], modf[lo:hi, 0:1], normf_g.reshape(1, d))
        experts, y, *st = _run_layer(x, mod[lo:hi], shift0, wkv0, gla0, lp, experts, final)
        ys.append(y)
        states.extend(s[None] for s in st)
    return tuple(ys + states)
```
